```python
import jax, jax.numpy as jnp
from jax import lax
import numpy as np

D_MODEL = 1024
BATCH = 8
SEQ = 2048
DEPTH = 1

CHUNK = 64
POOL_WIDTH = D_MODEL // 2
N_POOL_GROUPS = 4
POOL_GROUP = POOL_WIDTH // N_POOL_GROUPS
POOL_WINDOWS = (2, 4, 8, 16)
CONV_WIDTH = D_MODEL // 2
CONV_K = 3
N_BRANCHES = 2
N_EXPERTS = 64
TOP_K = 8
EXPERT_HIDDEN = 256
SHARED_HIDDEN = 256
ROUTED_SCALE = 2.5
ROW_BLOCK = 128
EPS = 1e-6
IN_COLS = POOL_WIDTH + 3 * CONV_WIDTH + N_BRANCHES * D_MODEL

kernel_name = "hybrid_pool_shortconv_moe_block"


def rmsnorm(x, g):
    xf = x.astype(jnp.float32)
    r = lax.rsqrt(jnp.mean(xf * xf, axis=-1, keepdims=True) + EPS)
    return (xf * r).astype(x.dtype) * g


def pool_mixer(u, w_group, scale):
    b, s, _ = u.shape
    ug = u.reshape(b, s, N_POOL_GROUPS, POOL_GROUP)
    cs = jnp.cumsum(ug.astype(jnp.float32), axis=1)
    t = jnp.arange(s)
    outs = []
    for gi, w in enumerate(POOL_WINDOWS):
        c = cs[:, :, gi]
        prev = jnp.pad(c, ((0, 0), (w, 0), (0, 0)))[:, :s]
        cnt = jnp.minimum(t + 1, w).astype(jnp.float32)[None, :, None]
        outs.append((c - prev) / cnt)
    pooled = jnp.stack(outs, axis=2).astype(u.dtype) - ug
    mixed = jnp.einsum('bsgc,gcd->bsgd', pooled, w_group)
    return mixed.reshape(b, s, POOL_WIDTH) * scale


def causal_depthwise_conv3(u, w):
    s = u.shape[1]
    up = jnp.pad(u, ((0, 0), (CONV_K - 1, 0), (0, 0)))
    return up[:, 0:s] * w[0] + up[:, 1:s + 1] * w[1] + up[:, 2:s + 2] * w[2]


def swiglu(h, w_gate, w_up, w_down):
    return (jax.nn.silu(h @ w_gate) * (h @ w_up)) @ w_down


def routed_experts(h, w_router, router_bias, e_gate, e_up, e_down):
    t_tok, d = h.shape
    logits = (h.astype(jnp.float32) @ w_router.astype(jnp.float32))
    scores = jax.nn.sigmoid(logits)
    _, idx = lax.top_k(scores + router_bias.astype(jnp.float32), TOP_K)
    wsel = jnp.take_along_axis(scores, idx, axis=1)
    wsel = wsel / jnp.sum(wsel, axis=1, keepdims=True) * ROUTED_SCALE

    n_assign = t_tok * TOP_K
    flat_e = idx.reshape(n_assign)
    flat_tok = jnp.repeat(jnp.arange(t_tok, dtype=jnp.int32), TOP_K)
    flat_w = wsel.reshape(n_assign)
    order = jnp.argsort(flat_e, stable=True)
    sorted_e = flat_e[order]
    counts = jnp.bincount(flat_e, length=N_EXPERTS)
    padded = (counts + ROW_BLOCK - 1) // ROW_BLOCK * ROW_BLOCK
    start = jnp.cumsum(counts) - counts
    pad_end = jnp.cumsum(padded)
    pad_start = pad_end - padded
    rank = jnp.arange(n_assign) - start[sorted_e]
    dest = pad_start[sorted_e] + rank

    n_rows = n_assign + N_EXPERTS * ROW_BLOCK
    n_blk = n_rows // ROW_BLOCK
    row_tok = jnp.zeros((n_rows,), jnp.int32).at[dest].set(flat_tok[order])
    row_w = jnp.zeros((n_rows,), jnp.float32).at[dest].set(flat_w[order])
    blk_e = jnp.clip(jnp.searchsorted(pad_end, jnp.arange(n_blk) * ROW_BLOCK, side='right'),
                     0, N_EXPERTS - 1)
    xs = h[row_tok].reshape(n_blk, ROW_BLOCK, d)

    def expert_block(args):
        xb, e = args
        return swiglu(xb, e_gate[e], e_up[e], e_down[e])

    ys = lax.map(expert_block, (xs, blk_e)).reshape(n_rows, d)
    ys = ys * row_w[:, None].astype(ys.dtype)
    return jax.ops.segment_sum(ys, row_tok, num_segments=t_tok)


def setup_inputs(seed: int = 0) -> dict:
    key = jax.random.key(seed)
    ks = jax.random.split(key, 24)
    f32 = jnp.float32
    nrm = lambda k, shape, fan_in: jax.random.normal(k, shape, f32) * (fan_in ** -0.5)
    d = D_MODEL
    return {
        "x": jax.random.normal(ks[0], (BATCH, SEQ, d), f32),
        "g_mix": 1.0 + 0.05 * jax.random.normal(ks[1], (d,), f32),
        "w_in": nrm(ks[2], (d, IN_COLS), d),
        "b_gate": 0.05 * jax.random.normal(ks[3], (N_BRANCHES * d,), f32),
        "w_pool_group": nrm(ks[4], (N_POOL_GROUPS, POOL_GROUP, POOL_GROUP), POOL_GROUP),
        "pool_scale": 1.0 + 0.1 * jax.random.normal(ks[5], (POOL_WIDTH,), f32),
        "w_pool_out": nrm(ks[6], (POOL_WIDTH, d), POOL_WIDTH),
        "conv_w": nrm(ks[7], (CONV_K, CONV_WIDTH), CONV_K),
        "w_conv_out": nrm(ks[8], (CONV_WIDTH, d), CONV_WIDTH),
        "w_o": nrm(ks[9], (d, d), d),
        "g_ffn": 1.0 + 0.05 * jax.random.normal(ks[10], (d,), f32),
        "w_router": nrm(ks[11], (d, N_EXPERTS), d),
        "router_bias": 0.01 * jax.random.normal(ks[12], (N_EXPERTS,), f32),
        "e_gate": nrm(ks[13], (N_EXPERTS, d, EXPERT_HIDDEN), d),
        "e_up": nrm(ks[14], (N_EXPERTS, d, EXPERT_HIDDEN), d),
        "e_down": nrm(ks[15], (N_EXPERTS, EXPERT_HIDDEN, d), EXPERT_HIDDEN),
        "s_gate": nrm(ks[16], (d, SHARED_HIDDEN), d),
        "s_up": nrm(ks[17], (d, SHARED_HIDDEN), d),
        "s_down": nrm(ks[18], (SHARED_HIDDEN, d), SHARED_HIDDEN),
        "g_final": 1.0 + 0.05 * jax.random.normal(ks[19], (d,), f32),
    }


def reference(x, g_mix, w_in, b_gate, w_pool_group, pool_scale, w_pool_out, conv_w,
              w_conv_out, w_o, g_ffn, w_router, router_bias, e_gate, e_up, e_down,
              s_gate, s_up, s_down, g_final):
    b, s, d = x.shape
    for _ in range(DEPTH):
        h = rmsnorm(x, g_mix)
        proj = h @ w_in
        o0 = POOL_WIDTH
        o1 = o0 + CONV_WIDTH
        o2 = o1 + CONV_WIDTH
        o3 = o2 + CONV_WIDTH
        u_pool = proj[..., :o0]
        gb = proj[..., o0:o1]
        gc = proj[..., o1:o2]
        v = proj[..., o2:o3]
        gates = jax.nn.sigmoid(proj[..., o3:] + b_gate)
        gate_a = gates[..., :d]
        gate_b = gates[..., d:]

        branch_a = pool_mixer(u_pool, w_pool_group, pool_scale) @ w_pool_out
        branch_b = (gb * causal_depthwise_conv3(gc * v, conv_w)) @ w_conv_out
        merged = gate_a * branch_a + gate_b * branch_b
        x = x + merged @ w_o

        h2 = rmsnorm(x, g_ffn).reshape(b * s, d)
        shared = swiglu(h2, s_gate, s_up, s_down)
        routed = routed_experts(h2, w_router, router_bias, e_gate, e_up, e_down)
        x = x + (shared + routed).reshape(b, s, d)
    return rmsnorm(x, g_final)
```

```python
import functools

import jax
import jax.numpy as jnp
from jax import lax
from jax.experimental import pallas as pl
from jax.experimental.pallas import tpu as pltpu

D_MODEL = 1024
POOL_WIDTH = 512
N_POOL_GROUPS = 4
POOL_GROUP = 128
POOL_WINDOWS = (2, 4, 8, 16)
CONV_WIDTH = 512
N_EXPERTS = 64
TOP_K = 8
EXPERT_HIDDEN = 256
SHARED_HIDDEN = 256
ROUTED_SCALE = 2.5
EPS = 1e-6

POOL_HALO = 16
CONV_HALO = 8
TM_MIX = 256
TM_DEST = 2048
TM_DISP = 512
ROW_BLOCK = 256
TM_COMB = 256
VMEM_LIMIT = 56 * 1024 * 1024

BF16 = jnp.bfloat16
F32 = jnp.float32


def _rms(x, g):
    r = lax.rsqrt(jnp.mean(x * x, axis=-1, keepdims=True) + EPS)
    return (x * r) * g


def _dot(a, b):
    return jnp.dot(a, b, preferred_element_type=F32)


def _mixer_router_kernel(n_seq_tiles,
                         x_ref, g_mix_ref, w_in_ref, b_gate_ref, w_grp_ref, pool_scale_ref,
                         w_po_ref, conv_w_ref, w_co_ref, w_o_ref, g_ffn_ref,
                         wr_hi_ref, wr_lo_ref, rbias_ref, s_gate_ref, s_up_ref, s_down_ref,
                         xres_ref, h2_ref, idx_ref, wsel_ref, rank_ref, counts_ref,
                         ext_pool, ext_conv, cnt_carry, tri):
    tm = x_ref.shape[0]
    i = pl.program_id(0)
    st = i % n_seq_tiles

    @pl.when(i == 0)
    def _():
        r = lax.broadcasted_iota(jnp.int32, (tm, tm), 0)
        c = lax.broadcasted_iota(jnp.int32, (tm, tm), 1)
        tri[...] = (r < c).astype(BF16)
        cnt_carry[...] = jnp.zeros_like(cnt_carry)

    @pl.when(st == 0)
    def _():
        ext_pool[0:POOL_HALO, :] = jnp.zeros((POOL_HALO, POOL_WIDTH), F32)
        ext_conv[0:CONV_HALO, :] = jnp.zeros((CONV_HALO, CONV_WIDTH), F32)

    x = x_ref[...]
    hb = _rms(x, g_mix_ref[...]).astype(BF16)

    o0 = POOL_WIDTH
    o1 = o0 + CONV_WIDTH
    o2 = o1 + CONV_WIDTH
    o3 = o2 + CONV_WIDTH

    u = _dot(hb, w_in_ref[:, 0:o0])
    ext_pool[POOL_HALO:POOL_HALO + tm, :] = u
    t_glob = st * tm + lax.broadcasted_iota(jnp.int32, (tm, 1), 0)
    mixed = []
    for gi, w in enumerate(POOL_WINDOWS):
        cols = slice(gi * POOL_GROUP, (gi + 1) * POOL_GROUP)
        ug = u[:, cols]
        acc = ug
        for j in range(1, w):
            acc = acc + ext_pool[POOL_HALO - j:POOL_HALO - j + tm, cols]
        cnt = jnp.minimum(t_glob + 1, w).astype(F32)
        pooled = acc * (1.0 / cnt) - ug
        mixed.append(_dot(pooled.astype(BF16), w_grp_ref[gi]))
    mixed = jnp.concatenate(mixed, axis=1) * pool_scale_ref[...]
    branch_a = _dot(mixed.astype(BF16), w_po_ref[...])
    ext_pool[0:POOL_HALO, :] = ext_pool[tm:tm + POOL_HALO, :]

    gb = _dot(hb, w_in_ref[:, o0:o1])
    gc = _dot(hb, w_in_ref[:, o1:o2])
    v = _dot(hb, w_in_ref[:, o2:o3])
    cv = gc * v
    ext_conv[CONV_HALO:CONV_HALO + tm, :] = cv
    conv = (ext_conv[CONV_HALO - 2:CONV_HALO - 2 + tm, :] * conv_w_ref[0:1, :]
            + ext_conv[CONV_HALO - 1:CONV_HALO - 1 + tm, :] * conv_w_ref[1:2, :]
            + cv * conv_w_ref[2:3, :])
    branch_b = _dot((gb * conv).astype(BF16), w_co_ref[...])
    ext_conv[0:CONV_HALO, :] = ext_conv[tm:tm + CONV_HALO, :]

    gate_a = jax.nn.sigmoid(_dot(hb, w_in_ref[:, o3:o3 + D_MODEL]) + b_gate_ref[:, 0:D_MODEL])
    merged = gate_a * branch_a
    gate_b = jax.nn.sigmoid(_dot(hb, w_in_ref[:, o3 + D_MODEL:o3 + 2 * D_MODEL])
                            + b_gate_ref[:, D_MODEL:2 * D_MODEL])
    merged = merged + gate_b * branch_b
    x1 = x + _dot(merged.astype(BF16), w_o_ref[...])

    h2 = _rms(x1, g_ffn_ref[...])
    h2_ref[...] = h2
    h2b = h2.astype(BF16)
    sg = _dot(h2b, s_gate_ref[...])
    su = _dot(h2b, s_up_ref[...])
    shared = _dot((sg * jax.nn.sigmoid(sg) * su).astype(BF16), s_down_ref[...])
    xres_ref[...] = x1 + shared

    h2_lo = (h2 - h2b.astype(F32)).astype(BF16)
    nt = (((1,), (1,)), ((), ()))
    logits = (lax.dot_general(wr_hi_ref[...], h2b, nt, preferred_element_type=F32)
              + lax.dot_general(wr_hi_ref[...], h2_lo, nt, preferred_element_type=F32)
              + lax.dot_general(wr_lo_ref[...], h2b, nt, preferred_element_type=F32))
    scores = jax.nn.sigmoid(logits)
    sel = scores + rbias_ref[...]
    eidx = lax.broadcasted_iota(jnp.int32, (N_EXPERTS, tm), 0).astype(F32)
    e_rows, w_rows = [], []
    mask = jnp.zeros((N_EXPERTS, tm), F32)
    for _ in range(TOP_K):
        m = jnp.max(sel, axis=0, keepdims=True)
        ek = jnp.min(jnp.where(sel == m, eidx, float(N_EXPERTS)), axis=0, keepdims=True)
        oh = eidx == ek
        w_rows.append(jnp.sum(jnp.where(oh, scores, 0.0), axis=0, keepdims=True))
        e_rows.append(ek)
        mask = mask + oh.astype(F32)
        sel = jnp.where(oh, -jnp.inf, sel)
    wsum = w_rows[0]
    for k in range(1, TOP_K):
        wsum = wsum + w_rows[k]

    before = _dot(mask.astype(BF16), tri[...]) + cnt_carry[...]
    for k in range(TOP_K):
        oh = eidx == e_rows[k]
        rank_ref[k:k + 1, :] = jnp.sum(jnp.where(oh, before, 0.0), axis=0,
                                       keepdims=True).astype(jnp.int32)
        idx_ref[k:k + 1, :] = e_rows[k].astype(jnp.int32)
        wsel_ref[k:k + 1, :] = w_rows[k] / wsum * ROUTED_SCALE
    total = cnt_carry[...] + jnp.sum(mask, axis=1, keepdims=True)
    cnt_carry[...] = total
    counts_ref[...] = total.astype(jnp.int32)


def _mixer_router(x2d, seq_len, g_mix, w_in, b_gate, w_grp, pool_scale, w_po, conv_w, w_co, w_o,
                  g_ffn, wr_hi, wr_lo, rbias, s_gate, s_up, s_down):
    t_tok = x2d.shape[0]
    tm = TM_MIX
    n_seq_tiles = seq_len // tm
    const = lambda shape: pl.BlockSpec(shape, lambda i: (0,) * len(shape),
                                       pipeline_mode=pl.Buffered(1))
    row_blk = pl.BlockSpec((tm, D_MODEL), lambda i: (i, 0))
    slot_blk = pl.BlockSpec((TOP_K, tm), lambda i: (0, i))
    return pl.pallas_call(
        functools.partial(_mixer_router_kernel, n_seq_tiles),
        grid=(t_tok // tm,),
        in_specs=[row_blk, const(g_mix.shape), const(w_in.shape), const(b_gate.shape),
                  const(w_grp.shape), const(pool_scale.shape), const(w_po.shape),
                  const(conv_w.shape), const(w_co.shape), const(w_o.shape), const(g_ffn.shape),
                  const(wr_hi.shape), const(wr_lo.shape), const(rbias.shape),
                  const(s_gate.shape), const(s_up.shape), const(s_down.shape)],
        out_specs=[row_blk, row_blk, slot_blk, slot_blk, slot_blk,
                   pl.BlockSpec((N_EXPERTS, 1), lambda i: (0, 0))],
        out_shape=[jax.ShapeDtypeStruct((t_tok, D_MODEL), F32),
                   jax.ShapeDtypeStruct((t_tok, D_MODEL), F32),
                   jax.ShapeDtypeStruct((TOP_K, t_tok), jnp.int32),
                   jax.ShapeDtypeStruct((TOP_K, t_tok), F32),
                   jax.ShapeDtypeStruct((TOP_K, t_tok), jnp.int32),
                   jax.ShapeDtypeStruct((N_EXPERTS, 1), jnp.int32)],
        scratch_shapes=[pltpu.VMEM((POOL_HALO + tm, POOL_WIDTH), F32),
                        pltpu.VMEM((CONV_HALO + tm, CONV_WIDTH), F32),
                        pltpu.VMEM((N_EXPERTS, 1), F32),
                        pltpu.VMEM((tm, tm), BF16)],
        compiler_params=pltpu.CompilerParams(dimension_semantics=("arbitrary",),
                                             vmem_limit_bytes=VMEM_LIMIT),
        name="mixer_router",
    )(x2d, g_mix, w_in, b_gate, w_grp, pool_scale, w_po, conv_w, w_co, w_o, g_ffn,
      wr_hi, wr_lo, rbias, s_gate, s_up, s_down)


def _dest_kernel(idx_ref, rank_ref, pad_start_ref, dest_ref):
    tm = idx_ref.shape[1]
    eidx = lax.broadcasted_iota(jnp.int32, (N_EXPERTS, tm), 0)
    ps = pad_start_ref[...].astype(F32)
    for k in range(TOP_K):
        oh = eidx == idx_ref[k:k + 1, :]
        start = jnp.sum(jnp.where(oh, ps, 0.0), axis=0, keepdims=True)
        dest_ref[k:k + 1, :] = start.astype(jnp.int32) + rank_ref[k:k + 1, :]


def _dest(idx_t, rank_t, pad_start):
    t_tok = idx_t.shape[1]
    slot_blk = pl.BlockSpec((TOP_K, TM_DEST), lambda i: (0, i))
    return pl.pallas_call(
        _dest_kernel,
        grid=(t_tok // TM_DEST,),
        in_specs=[slot_blk, slot_blk, pl.BlockSpec((N_EXPERTS, 1), lambda i: (0, 0))],
        out_specs=slot_blk,
        out_shape=jax.ShapeDtypeStruct((TOP_K, t_tok), jnp.int32),
        compiler_params=pltpu.CompilerParams(dimension_semantics=("arbitrary",)),
        name="dest",
    )(idx_t, rank_t, pad_start)


def _dispatch_kernel(n_blk, pad_end_ref, padded_ref, n_used_ref,
                     dest_ref, h2_ref, xs_ref, zbuf, sem, zsem):
    tm = h2_ref.shape[0]
    i = pl.program_id(0)

    def zero_copy(row0):
        return pltpu.make_async_copy(zbuf, xs_ref.at[pl.ds(pl.multiple_of(row0, ROW_BLOCK),
                                                           ROW_BLOCK)], zsem)

    @pl.when(i == 0)
    def _():
        zbuf[...] = jnp.zeros_like(zbuf)

        def fill(start):
            def per_expert(e, c):
                @pl.when(padded_ref[e] > 0)
                def _():
                    cp = zero_copy(pad_end_ref[e] - ROW_BLOCK)
                    cp.start() if start else cp.wait()
                return c
            lax.fori_loop(0, N_EXPERTS, per_expert, 0)

            def per_tail(b, c):
                cp = zero_copy(b * ROW_BLOCK)
                cp.start() if start else cp.wait()
                return c
            lax.fori_loop(n_used_ref[0], n_blk, per_tail, 0)

        fill(True)
        fill(False)

    def row_copy(r, k):
        return pltpu.make_async_copy(h2_ref.at[pl.ds(r, 1)],
                                     xs_ref.at[pl.ds(dest_ref[k, r], 1)], sem)

    def issue(r, c):
        for k in range(TOP_K):
            row_copy(r, k).start()
        return c
    lax.fori_loop(0, tm, issue, 0)

    def drain(r, c):
        for k in range(TOP_K):
            pltpu.make_async_copy(h2_ref.at[pl.ds(0, 1)], xs_ref.at[pl.ds(0, 1)], sem).wait()
        return c
    lax.fori_loop(0, tm, drain, 0)


def _dispatch(dest_t, h2, pad_end, padded, n_used, n_rows):
    t_tok = h2.shape[0]
    n_blk = n_rows // ROW_BLOCK
    grid_spec = pltpu.PrefetchScalarGridSpec(
        num_scalar_prefetch=3,
        grid=(t_tok // TM_DISP,),
        in_specs=[pl.BlockSpec((TOP_K, TM_DISP), lambda i, *_: (0, i), memory_space=pltpu.SMEM),
                  pl.BlockSpec((TM_DISP, D_MODEL), lambda i, *_: (i, 0))],
        out_specs=pl.BlockSpec(memory_space=pl.ANY),
        scratch_shapes=[pltpu.VMEM((ROW_BLOCK, D_MODEL), F32),
                        pltpu.SemaphoreType.DMA, pltpu.SemaphoreType.DMA],
    )
    return pl.pallas_call(
        functools.partial(_dispatch_kernel, n_blk),
        grid_spec=grid_spec,
        out_shape=jax.ShapeDtypeStruct((n_rows, D_MODEL), F32),
        compiler_params=pltpu.CompilerParams(dimension_semantics=("arbitrary",)),
        name="dispatch",
    )(pad_end, padded, n_used, dest_t, h2)


def _experts_kernel(blk_e_ref, n_used_ref, xs_ref, eg_ref, eu_ref, ed_ref, ys_ref, wg, wu, wd):
    b = pl.program_id(0)
    used = b < n_used_ref[0]

    @pl.when(used)
    def _():
        prev = blk_e_ref[jnp.maximum(b - 1, 0)]
        new_expert = jnp.logical_or(b == 0, blk_e_ref[b] != prev)

        @pl.when(new_expert)
        def _():
            wg[...] = eg_ref[0].astype(BF16)
            wu[...] = eu_ref[0].astype(BF16)
            wd[...] = ed_ref[0].astype(BF16)

        xb = xs_ref[...].astype(BF16)
        g = _dot(xb, wg[...])
        up = _dot(xb, wu[...])
        hid = (g * jax.nn.sigmoid(g) * up).astype(BF16)
        ys_ref[...] = _dot(hid, wd[...])

    @pl.when(jnp.logical_not(used))
    def _():
        ys_ref[...] = jnp.zeros_like(ys_ref)


def _experts(blk_e, n_used, xs, e_gate, e_up, e_down):
    n_rows = xs.shape[0]
    n_blk = n_rows // ROW_BLOCK
    grid_spec = pltpu.PrefetchScalarGridSpec(
        num_scalar_prefetch=2,
        grid=(n_blk,),
        in_specs=[pl.BlockSpec((ROW_BLOCK, D_MODEL),
                               lambda b, be, nu: (jnp.minimum(b, nu[0] - 1), 0)),
                  pl.BlockSpec((1, D_MODEL, EXPERT_HIDDEN), lambda b, be, nu: (be[b], 0, 0)),
                  pl.BlockSpec((1, D_MODEL, EXPERT_HIDDEN), lambda b, be, nu: (be[b], 0, 0)),
                  pl.BlockSpec((1, EXPERT_HIDDEN, D_MODEL), lambda b, be, nu: (be[b], 0, 0))],
        out_specs=pl.BlockSpec((ROW_BLOCK, D_MODEL), lambda b, be, nu: (b, 0)),
        scratch_shapes=[pltpu.VMEM((D_MODEL, EXPERT_HIDDEN), BF16),
                        pltpu.VMEM((D_MODEL, EXPERT_HIDDEN), BF16),
                        pltpu.VMEM((EXPERT_HIDDEN, D_MODEL), BF16)],
    )
    return pl.pallas_call(
        _experts_kernel,
        grid_spec=grid_spec,
        out_shape=jax.ShapeDtypeStruct((n_rows, D_MODEL), F32),
        compiler_params=pltpu.CompilerParams(dimension_semantics=("arbitrary",)),
        name="experts",
    )(blk_e, n_used, xs, e_gate, e_up, e_down)


def _combine_kernel(dest_ref, wsel_ref, xres_ref, g_final_ref, ys_ref, out_ref, buf, sem):
    tm = xres_ref.shape[0]

    def issue(r, c):
        for k in range(TOP_K):
            pltpu.make_async_copy(ys_ref.at[pl.ds(dest_ref[k, r], 1)],
                                  buf.at[k, pl.ds(r, 1)], sem).start()
        return c
    lax.fori_loop(0, tm, issue, 0)

    def drain(r, c):
        for k in range(TOP_K):
            pltpu.make_async_copy(ys_ref.at[pl.ds(0, 1)], buf.at[0, pl.ds(0, 1)], sem).wait()
        return c
    lax.fori_loop(0, tm, drain, 0)

    acc = xres_ref[...]
    wsel = wsel_ref[...]
    for k in range(TOP_K):
        acc = acc + buf[k] * wsel[:, k:k + 1]
    out_ref[...] = _rms(acc, g_final_ref[...])


def _combine(dest_t, wsel_tk, xres, g_final, ys):
    t_tok = xres.shape[0]
    tm = TM_COMB
    return pl.pallas_call(
        _combine_kernel,
        grid=(t_tok // tm,),
        in_specs=[pl.BlockSpec((TOP_K, tm), lambda i: (0, i), memory_space=pltpu.SMEM),
                  pl.BlockSpec((tm, TOP_K), lambda i: (i, 0)),
                  pl.BlockSpec((tm, D_MODEL), lambda i: (i, 0)),
                  pl.BlockSpec((1, D_MODEL), lambda i: (0, 0)),
                  pl.BlockSpec(memory_space=pl.ANY)],
        out_specs=pl.BlockSpec((tm, D_MODEL), lambda i: (i, 0)),
        out_shape=jax.ShapeDtypeStruct((t_tok, D_MODEL), F32),
        scratch_shapes=[pltpu.VMEM((TOP_K, tm, D_MODEL), F32), pltpu.SemaphoreType.DMA],
        compiler_params=pltpu.CompilerParams(dimension_semantics=("arbitrary",),
                                             vmem_limit_bytes=VMEM_LIMIT),
        name="combine",
    )(dest_t, wsel_tk, xres, g_final, ys)


def kernel(x, g_mix, w_in, b_gate, w_pool_group, pool_scale, w_pool_out, conv_w, w_conv_out, w_o,
           g_ffn, w_router, router_bias, e_gate, e_up, e_down, s_gate, s_up, s_down, g_final):
    b, s, d = x.shape
    t_tok = b * s
    assert d == D_MODEL and s % TM_MIX == 0 and TM_MIX >= POOL_HALO
    assert t_tok % TM_DEST == 0 and t_tok % TM_DISP == 0 and t_tok % TM_COMB == 0

    row = lambda a: a.reshape(1, -1)
    wr_t = w_router.T.astype(F32)
    wr_hi = wr_t.astype(BF16)
    wr_lo = (wr_t - wr_hi.astype(F32)).astype(BF16)

    xres, h2, idx_t, wsel_t, rank_t, counts = _mixer_router(
        x.reshape(t_tok, d), s, row(g_mix), w_in.astype(BF16), row(b_gate),
        w_pool_group.astype(BF16), row(pool_scale), w_pool_out.astype(BF16), conv_w,
        w_conv_out.astype(BF16), w_o.astype(BF16), row(g_ffn), wr_hi, wr_lo,
        router_bias.astype(F32).reshape(N_EXPERTS, 1), s_gate.astype(BF16), s_up.astype(BF16),
        s_down.astype(BF16))

    counts = counts.reshape(N_EXPERTS)
    padded = (counts + ROW_BLOCK - 1) // ROW_BLOCK * ROW_BLOCK
    pad_end = jnp.cumsum(padded)
    pad_start = pad_end - padded
    n_rows = t_tok * TOP_K + N_EXPERTS * ROW_BLOCK
    n_blk = n_rows // ROW_BLOCK
    n_used = (pad_end[-1:] // ROW_BLOCK).astype(jnp.int32)
    blk_row0 = jnp.arange(n_blk, dtype=jnp.int32) * ROW_BLOCK
    blk_e = jnp.minimum(jnp.sum(pad_end[None, :] <= blk_row0[:, None], axis=1),
                        N_EXPERTS - 1).astype(jnp.int32)

    dest_t = _dest(idx_t, rank_t, pad_start.reshape(N_EXPERTS, 1).astype(jnp.int32))
    xs = _dispatch(dest_t, h2, pad_end.astype(jnp.int32), padded.astype(jnp.int32), n_used, n_rows)
    ys = _experts(blk_e, n_used, xs, e_gate, e_up, e_down)
    out = _combine(dest_t, wsel_t.T, xres, row(g_final), ys)
    return out.reshape(b, s, d)
```

```python
import functools

import jax
import jax.numpy as jnp
from jax import lax
from jax.experimental import pallas as pl
from jax.experimental.pallas import tpu as pltpu
from jax.experimental.pallas import tpu_sc as plsc

D_MODEL = 1024
HALF = D_MODEL // 2
POOL_WIDTH = 512
N_POOL_GROUPS = 4
POOL_GROUP = 128
POOL_WINDOWS = (2, 4, 8, 16)
CONV_WIDTH = 512
N_EXPERTS = 64
TOP_K = 8
EXPERT_HIDDEN = 256
SHARED_HIDDEN = 256
ROUTED_SCALE = 2.5
EPS = 1e-6

POOL_HALO = 16
CONV_HALO = 8
TM_MIX = 256
TM_DEST = 2048
ROW_BLOCK = 256
TM_COMB = 512
VMEM_LIMIT = 56 * 1024 * 1024

SC_CORES = 2
SC_SUBCORES = 16
SC_WORKERS = SC_CORES * SC_SUBCORES
SC_CHUNK = 64

BF16 = jnp.bfloat16
F32 = jnp.float32
U32 = jnp.uint32


def _rms(x, g):
    r = lax.rsqrt(jnp.mean(x * x, axis=-1, keepdims=True) + EPS)
    return (x * r) * g


def _dot(a, b):
    return jnp.dot(a, b, preferred_element_type=F32)


def _pack_pair(a, b):
    ra = lax.bitcast_convert_type(a.astype(BF16).astype(F32), U32)
    rb = lax.bitcast_convert_type(b.astype(BF16).astype(F32), U32)
    return ra | (rb >> 16)


def _unpack_pair(w):
    a = lax.bitcast_convert_type(w & jnp.uint32(0xFFFF0000), F32)
    b = lax.bitcast_convert_type(w << 16, F32)
    return a, b


def _mixer_router_kernel(n_seq_tiles,
                         x_ref, g_mix_ref, w_in_ref, b_gate_ref, w_grp_ref, pool_scale_ref,
                         w_po_ref, conv_w_ref, w_co_ref, w_o_ref, g_ffn_ref,
                         wr_hi_ref, wr_lo_ref, rbias_ref, s_gate_ref, s_up_ref, s_down_ref,
                         xres_ref, h2p_ref, idx_ref, wsel_ref, rank_ref, counts_ref,
                         ext_pool, ext_conv, cnt_carry, tri):
    tm = x_ref.shape[0]
    i = pl.program_id(0)
    st = i % n_seq_tiles

    @pl.when(i == 0)
    def _():
        r = lax.broadcasted_iota(jnp.int32, (tm, tm), 0)
        c = lax.broadcasted_iota(jnp.int32, (tm, tm), 1)
        tri[...] = (r < c).astype(BF16)
        cnt_carry[...] = jnp.zeros_like(cnt_carry)

    @pl.when(st == 0)
    def _():
        ext_pool[0:POOL_HALO, :] = jnp.zeros((POOL_HALO, POOL_WIDTH), F32)
        ext_conv[0:CONV_HALO, :] = jnp.zeros((CONV_HALO, CONV_WIDTH), F32)

    x = x_ref[...]
    hb = _rms(x, g_mix_ref[...]).astype(BF16)

    o0 = POOL_WIDTH
    o1 = o0 + CONV_WIDTH
    o2 = o1 + CONV_WIDTH
    o3 = o2 + CONV_WIDTH

    u = _dot(hb, w_in_ref[:, 0:o0])
    ext_pool[POOL_HALO:POOL_HALO + tm, :] = u
    t_glob = st * tm + lax.broadcasted_iota(jnp.int32, (tm, 1), 0)
    mixed = []
    for gi, w in enumerate(POOL_WINDOWS):
        cols = slice(gi * POOL_GROUP, (gi + 1) * POOL_GROUP)
        ug = u[:, cols]
        acc = ug
        for j in range(1, w):
            acc = acc + ext_pool[POOL_HALO - j:POOL_HALO - j + tm, cols]
        cnt = jnp.minimum(t_glob + 1, w).astype(F32)
        pooled = acc * (1.0 / cnt) - ug
        mixed.append(_dot(pooled.astype(BF16), w_grp_ref[gi]))
    mixed = jnp.concatenate(mixed, axis=1) * pool_scale_ref[...]
    branch_a = _dot(mixed.astype(BF16), w_po_ref[...])
    ext_pool[0:POOL_HALO, :] = ext_pool[tm:tm + POOL_HALO, :]

    gb = _dot(hb, w_in_ref[:, o0:o1])
    gc = _dot(hb, w_in_ref[:, o1:o2])
    v = _dot(hb, w_in_ref[:, o2:o3])
    cv = gc * v
    ext_conv[CONV_HALO:CONV_HALO + tm, :] = cv
    conv = (ext_conv[CONV_HALO - 2:CONV_HALO - 2 + tm, :] * conv_w_ref[0:1, :]
            + ext_conv[CONV_HALO - 1:CONV_HALO - 1 + tm, :] * conv_w_ref[1:2, :]
            + cv * conv_w_ref[2:3, :])
    branch_b = _dot((gb * conv).astype(BF16), w_co_ref[...])
    ext_conv[0:CONV_HALO, :] = ext_conv[tm:tm + CONV_HALO, :]

    gate_a = jax.nn.sigmoid(_dot(hb, w_in_ref[:, o3:o3 + D_MODEL]) + b_gate_ref[:, 0:D_MODEL])
    merged = gate_a * branch_a
    gate_b = jax.nn.sigmoid(_dot(hb, w_in_ref[:, o3 + D_MODEL:o3 + 2 * D_MODEL])
                            + b_gate_ref[:, D_MODEL:2 * D_MODEL])
    merged = merged + gate_b * branch_b
    x1 = x + _dot(merged.astype(BF16), w_o_ref[...])

    h2 = _rms(x1, g_ffn_ref[...])
    h2p_ref[...] = _pack_pair(h2[:, 0:HALF], h2[:, HALF:D_MODEL])
    h2b = h2.astype(BF16)
    sg = _dot(h2b, s_gate_ref[...])
    su = _dot(h2b, s_up_ref[...])
    shared = _dot((sg * jax.nn.sigmoid(sg) * su).astype(BF16), s_down_ref[...])
    xres_ref[...] = x1 + shared

    h2_lo = (h2 - h2b.astype(F32)).astype(BF16)
    nt = (((1,), (1,)), ((), ()))
    logits = (lax.dot_general(wr_hi_ref[...], h2b, nt, preferred_element_type=F32)
              + lax.dot_general(wr_hi_ref[...], h2_lo, nt, preferred_element_type=F32)
              + lax.dot_general(wr_lo_ref[...], h2b, nt, preferred_element_type=F32))
    scores = jax.nn.sigmoid(logits)
    sel = scores + rbias_ref[...]
    eidx = lax.broadcasted_iota(jnp.int32, (N_EXPERTS, tm), 0).astype(F32)
    e_rows, w_rows = [], []
    mask = jnp.zeros((N_EXPERTS, tm), F32)
    for _ in range(TOP_K):
        m = jnp.max(sel, axis=0, keepdims=True)
        ek = jnp.min(jnp.where(sel == m, eidx, float(N_EXPERTS)), axis=0, keepdims=True)
        oh = eidx == ek
        w_rows.append(jnp.sum(jnp.where(oh, scores, 0.0), axis=0, keepdims=True))
        e_rows.append(ek)
        mask = mask + oh.astype(F32)
        sel = jnp.where(oh, -jnp.inf, sel)
    wsum = w_rows[0]
    for k in range(1, TOP_K):
        wsum = wsum + w_rows[k]

    before = _dot(mask.astype(BF16), tri[...]) + cnt_carry[...]
    for k in range(TOP_K):
        oh = eidx == e_rows[k]
        rank_ref[k:k + 1, :] = jnp.sum(jnp.where(oh, before, 0.0), axis=0,
                                       keepdims=True).astype(jnp.int32)
        idx_ref[k:k + 1, :] = e_rows[k].astype(jnp.int32)
        wsel_ref[k:k + 1, :] = w_rows[k] / wsum * ROUTED_SCALE
    total = cnt_carry[...] + jnp.sum(mask, axis=1, keepdims=True)
    cnt_carry[...] = total
    counts_ref[...] = total.astype(jnp.int32)


def _mixer_router(x2d, seq_len, g_mix, w_in, b_gate, w_grp, pool_scale, w_po, conv_w, w_co, w_o,
                  g_ffn, wr_hi, wr_lo, rbias, s_gate, s_up, s_down):
    t_tok = x2d.shape[0]
    tm = TM_MIX
    n_seq_tiles = seq_len // tm
    const = lambda shape: pl.BlockSpec(shape, lambda i: (0,) * len(shape),
                                       pipeline_mode=pl.Buffered(1))
    row_blk = pl.BlockSpec((tm, D_MODEL), lambda i: (i, 0))
    half_blk = pl.BlockSpec((tm, HALF), lambda i: (i, 0))
    slot_blk = pl.BlockSpec((TOP_K, tm), lambda i: (0, i))
    return pl.pallas_call(
        functools.partial(_mixer_router_kernel, n_seq_tiles),
        grid=(t_tok // tm,),
        in_specs=[row_blk, const(g_mix.shape), const(w_in.shape), const(b_gate.shape),
                  const(w_grp.shape), const(pool_scale.shape), const(w_po.shape),
                  const(conv_w.shape), const(w_co.shape), const(w_o.shape), const(g_ffn.shape),
                  const(wr_hi.shape), const(wr_lo.shape), const(rbias.shape),
                  const(s_gate.shape), const(s_up.shape), const(s_down.shape)],
        out_specs=[row_blk, half_blk, slot_blk, slot_blk, slot_blk,
                   pl.BlockSpec((N_EXPERTS, 1), lambda i: (0, 0))],
        out_shape=[jax.ShapeDtypeStruct((t_tok, D_MODEL), F32),
                   jax.ShapeDtypeStruct((t_tok, HALF), U32),
                   jax.ShapeDtypeStruct((TOP_K, t_tok), jnp.int32),
                   jax.ShapeDtypeStruct((TOP_K, t_tok), F32),
                   jax.ShapeDtypeStruct((TOP_K, t_tok), jnp.int32),
                   jax.ShapeDtypeStruct((N_EXPERTS, 1), jnp.int32)],
        scratch_shapes=[pltpu.VMEM((POOL_HALO + tm, POOL_WIDTH), F32),
                        pltpu.VMEM((CONV_HALO + tm, CONV_WIDTH), F32),
                        pltpu.VMEM((N_EXPERTS, 1), F32),
                        pltpu.VMEM((tm, tm), BF16)],
        compiler_params=pltpu.CompilerParams(dimension_semantics=("arbitrary",),
                                             vmem_limit_bytes=VMEM_LIMIT),
        name="mixer_router",
    )(x2d, g_mix, w_in, b_gate, w_grp, pool_scale, w_po, conv_w, w_co, w_o, g_ffn,
      wr_hi, wr_lo, rbias, s_gate, s_up, s_down)


def _dest_kernel(idx_ref, rank_ref, pad_start_ref, dest_ref):
    tm = idx_ref.shape[1]
    eidx = lax.broadcasted_iota(jnp.int32, (N_EXPERTS, tm), 0)
    ps = pad_start_ref[...].astype(F32)
    for k in range(TOP_K):
        oh = eidx == idx_ref[k:k + 1, :]
        start = jnp.sum(jnp.where(oh, ps, 0.0), axis=0, keepdims=True)
        dest_ref[k:k + 1, :] = start.astype(jnp.int32) + rank_ref[k:k + 1, :]


def _dest(idx_t, rank_t, pad_start):
    t_tok = idx_t.shape[1]
    slot_blk = pl.BlockSpec((TOP_K, TM_DEST), lambda i: (0, i))
    return pl.pallas_call(
        _dest_kernel,
        grid=(t_tok // TM_DEST,),
        in_specs=[slot_blk, slot_blk, pl.BlockSpec((N_EXPERTS, 1), lambda i: (0, 0))],
        out_specs=slot_blk,
        out_shape=jax.ShapeDtypeStruct((TOP_K, t_tok), jnp.int32),
        compiler_params=pltpu.CompilerParams(dimension_semantics=("arbitrary",)),
        name="dest",
    )(idx_t, rank_t, pad_start)


def _sc_mesh():
    return plsc.VectorSubcoreMesh(core_axis_name="c", subcore_axis_name="s")


def _sc_worker_id():
    return lax.axis_index("s") * SC_CORES + lax.axis_index("c")


def _dispatch(h2p, dest_t, zero_rows, n_rows):
    t_tok, width = h2p.shape
    per_w = t_tok // SC_WORKERS
    n_chunks = per_w // SC_CHUNK
    z_chunks = zero_rows.shape[0] // (SC_WORKERS * SC_CHUNK)
    dest_w = (dest_t.reshape(TOP_K, SC_WORKERS, n_chunks, SC_CHUNK)
              .transpose(1, 2, 0, 3).reshape(SC_WORKERS * n_chunks * TOP_K, SC_CHUNK))
    zero_w = zero_rows.reshape(SC_WORKERS * z_chunks, SC_CHUNK)

    @functools.partial(
        pl.kernel, mesh=_sc_mesh(),
        out_type=jax.ShapeDtypeStruct((n_rows, width), h2p.dtype),
        scratch_types=[pltpu.VMEM((n_chunks * TOP_K, SC_CHUNK), jnp.int32),
                       pltpu.VMEM((z_chunks, SC_CHUNK), jnp.int32),
                       pltpu.VMEM((2, SC_CHUNK, width), h2p.dtype),
                       pltpu.VMEM((SC_CHUNK, width), h2p.dtype),
                       pltpu.SemaphoreType.DMA, pltpu.SemaphoreType.DMA,
                       pltpu.SemaphoreType.DMA],
        name="dispatch",
    )
    def k(h2p_hbm, dest_hbm, zidx_hbm, zsrc_hbm, xs_hbm, idx_v, zidx_v, rows_v, zero_v,
          gsem, wsem, zsem):
        wid = _sc_worker_id()
        base = wid * per_w
        pltpu.sync_copy(dest_hbm.at[pl.ds(wid * n_chunks * TOP_K, n_chunks * TOP_K)], idx_v)
        pltpu.sync_copy(zidx_hbm.at[pl.ds(wid * z_chunks, z_chunks)], zidx_v)
        pltpu.sync_copy(zsrc_hbm, zero_v)

        def zput(j):
            return pltpu.make_async_copy(zero_v, xs_hbm.at[zidx_v.at[j]], zsem)

        for j in range(z_chunks):
            zput(j).start()

        def get(j, slot):
            return pltpu.make_async_copy(h2p_hbm.at[pl.ds(base + j * SC_CHUNK, SC_CHUNK)],
                                         rows_v.at[slot], gsem)

        def put(j, slot, kk):
            return pltpu.make_async_copy(rows_v.at[slot], xs_hbm.at[idx_v.at[j * TOP_K + kk]], wsem)

        get(0, 0).start()

        @pl.loop(0, n_chunks, step=2)
        def _(j):
            for b in range(2):
                jj = j + b
                get(jj, b).wait()

                @pl.when(jj >= 1)
                def _():
                    for kk in range(TOP_K):
                        put(jj - 1, 1 - b, kk).wait()

                @pl.when(jj + 1 < n_chunks)
                def _():
                    get(jj + 1, 1 - b).start()
                for kk in range(TOP_K):
                    put(jj, b, kk).start()

        for kk in range(TOP_K):
            put(n_chunks - 1, (n_chunks - 1) % 2, kk).wait()
        for j in range(z_chunks):
            zput(j).wait()

    return k(h2p, dest_w, zero_w, jnp.zeros((SC_CHUNK, width), h2p.dtype))


def _regroup(ys, dest_t):
    width = ys.shape[1]
    n_slots, t_tok = dest_t.shape
    m_rows = n_slots * t_tok
    per_w = m_rows // SC_WORKERS
    n_chunks = per_w // SC_CHUNK
    idx2 = dest_t.reshape(SC_WORKERS * n_chunks, SC_CHUNK)

    @functools.partial(
        pl.kernel, mesh=_sc_mesh(),
        out_type=jax.ShapeDtypeStruct((m_rows, width), ys.dtype),
        scratch_types=[pltpu.VMEM((n_chunks, SC_CHUNK), jnp.int32),
                       pltpu.VMEM((2, SC_CHUNK, width), ys.dtype),
                       pltpu.SemaphoreType.DMA, pltpu.SemaphoreType.DMA],
        name="regroup",
    )
    def k(ys_hbm, idx_hbm, out_hbm, idx_v, rows_v, gsem, wsem):
        wid = _sc_worker_id()
        base = wid * per_w
        pltpu.sync_copy(idx_hbm.at[pl.ds(wid * n_chunks, n_chunks)], idx_v)

        def get(j, slot):
            return pltpu.make_async_copy(ys_hbm.at[idx_v.at[j]], rows_v.at[slot], gsem)

        def put(j, slot):
            return pltpu.make_async_copy(rows_v.at[slot],
                                         out_hbm.at[pl.ds(base + j * SC_CHUNK, SC_CHUNK)], wsem)

        get(0, 0).start()

        @pl.loop(0, n_chunks, step=2)
        def _(j):
            for b in range(2):
                jj = j + b
                get(jj, b).wait()

                @pl.when(jj >= 1)
                def _():
                    put(jj - 1, 1 - b).wait()

                @pl.when(jj + 1 < n_chunks)
                def _():
                    get(jj + 1, 1 - b).start()
                put(jj, b).start()

        put(n_chunks - 1, (n_chunks - 1) % 2).wait()

    return k(ys, idx2).reshape(n_slots, t_tok, width)


def _experts_kernel(blk_e_ref, n_used_ref, xs_ref, eg_ref, eu_ref, ed_ref, ys_ref, wg, wu, wd):
    b = pl.program_id(0)
    used = b < n_used_ref[0]

    @pl.when(used)
    def _():
        prev = blk_e_ref[jnp.maximum(b - 1, 0)]
        new_expert = jnp.logical_or(b == 0, blk_e_ref[b] != prev)

        @pl.when(new_expert)
        def _():
            wg[...] = eg_ref[0].astype(BF16)
            wu[...] = eu_ref[0].astype(BF16)
            wd[...] = ed_ref[0].astype(BF16)

        xa, xb = _unpack_pair(xs_ref[...])
        xb16 = jnp.concatenate([xa, xb], axis=1).astype(BF16)
        g = _dot(xb16, wg[...])
        up = _dot(xb16, wu[...])
        hid = (g * jax.nn.sigmoid(g) * up).astype(BF16)
        y = _dot(hid, wd[...])
        ys_ref[...] = _pack_pair(y[:, 0:HALF], y[:, HALF:D_MODEL])

    @pl.when(jnp.logical_not(used))
    def _():
        ys_ref[...] = jnp.zeros_like(ys_ref)


def _experts(blk_e, n_used, xs, e_gate, e_up, e_down):
    n_rows = xs.shape[0]
    n_blk = n_rows // ROW_BLOCK
    grid_spec = pltpu.PrefetchScalarGridSpec(
        num_scalar_prefetch=2,
        grid=(n_blk,),
        in_specs=[pl.BlockSpec((ROW_BLOCK, HALF),
                               lambda b, be, nu: (jnp.minimum(b, nu[0] - 1), 0)),
                  pl.BlockSpec((1, D_MODEL, EXPERT_HIDDEN), lambda b, be, nu: (be[b], 0, 0)),
                  pl.BlockSpec((1, D_MODEL, EXPERT_HIDDEN), lambda b, be, nu: (be[b], 0, 0)),
                  pl.BlockSpec((1, EXPERT_HIDDEN, D_MODEL), lambda b, be, nu: (be[b], 0, 0))],
        out_specs=pl.BlockSpec((ROW_BLOCK, HALF), lambda b, be, nu: (b, 0)),
        scratch_shapes=[pltpu.VMEM((D_MODEL, EXPERT_HIDDEN), BF16),
                        pltpu.VMEM((D_MODEL, EXPERT_HIDDEN), BF16),
                        pltpu.VMEM((EXPERT_HIDDEN, D_MODEL), BF16)],
    )
    return pl.pallas_call(
        _experts_kernel,
        grid_spec=grid_spec,
        out_shape=jax.ShapeDtypeStruct((n_rows, HALF), U32),
        compiler_params=pltpu.CompilerParams(dimension_semantics=("arbitrary",)),
        name="experts",
    )(blk_e, n_used, xs, e_gate, e_up, e_down)


def _combine_kernel(yg_ref, wsel_ref, xres_ref, g_final_ref, out_ref):
    xres = xres_ref[...]
    acc_a = xres[:, 0:HALF]
    acc_b = xres[:, HALF:D_MODEL]
    wsel = wsel_ref[...]
    for k in range(TOP_K):
        ya, yb = _unpack_pair(yg_ref[k])
        wk = wsel[:, k:k + 1]
        acc_a = acc_a + ya * wk
        acc_b = acc_b + yb * wk
    out_ref[...] = _rms(jnp.concatenate([acc_a, acc_b], axis=1), g_final_ref[...])


def _combine(yg, wsel_tk, xres, g_final):
    t_tok = xres.shape[0]
    tm = TM_COMB
    return pl.pallas_call(
        _combine_kernel,
        grid=(t_tok // tm,),
        in_specs=[pl.BlockSpec((TOP_K, tm, HALF), lambda i: (0, i, 0)),
                  pl.BlockSpec((tm, TOP_K), lambda i: (i, 0)),
                  pl.BlockSpec((tm, D_MODEL), lambda i: (i, 0)),
                  pl.BlockSpec((1, D_MODEL), lambda i: (0, 0))],
        out_specs=pl.BlockSpec((tm, D_MODEL), lambda i: (i, 0)),
        out_shape=jax.ShapeDtypeStruct((t_tok, D_MODEL), F32),
        compiler_params=pltpu.CompilerParams(dimension_semantics=("arbitrary",),
                                             vmem_limit_bytes=VMEM_LIMIT),
        name="combine",
    )(yg, wsel_tk, xres, g_final)


def kernel(x, g_mix, w_in, b_gate, w_pool_group, pool_scale, w_pool_out, conv_w, w_conv_out, w_o,
           g_ffn, w_router, router_bias, e_gate, e_up, e_down, s_gate, s_up, s_down, g_final):
    b, s, d = x.shape
    t_tok = b * s
    n_pad = N_EXPERTS * ROW_BLOCK
    assert d == D_MODEL and s % TM_MIX == 0 and TM_MIX >= POOL_HALO
    assert t_tok % TM_DEST == 0 and t_tok % TM_COMB == 0
    assert t_tok % (2 * SC_WORKERS * SC_CHUNK) == 0 and n_pad % (SC_WORKERS * SC_CHUNK) == 0

    row = lambda a: a.reshape(1, -1)
    wr_t = w_router.T.astype(F32)
    wr_hi = wr_t.astype(BF16)
    wr_lo = (wr_t - wr_hi.astype(F32)).astype(BF16)

    xres, h2p, idx_t, wsel_t, rank_t, counts = _mixer_router(
        x.reshape(t_tok, d), s, row(g_mix), w_in.astype(BF16), row(b_gate),
        w_pool_group.astype(BF16), row(pool_scale), w_pool_out.astype(BF16), conv_w,
        w_conv_out.astype(BF16), w_o.astype(BF16), row(g_ffn), wr_hi, wr_lo,
        router_bias.astype(F32).reshape(N_EXPERTS, 1), s_gate.astype(BF16), s_up.astype(BF16),
        s_down.astype(BF16))

    counts = counts.reshape(N_EXPERTS)
    padded = (counts + ROW_BLOCK - 1) // ROW_BLOCK * ROW_BLOCK
    pad_end = jnp.cumsum(padded)
    pad_start = pad_end - padded
    n_rows = t_tok * TOP_K + n_pad
    n_blk = n_rows // ROW_BLOCK
    n_used = (pad_end[-1:] // ROW_BLOCK).astype(jnp.int32)
    blk_row0 = jnp.arange(n_blk, dtype=jnp.int32) * ROW_BLOCK
    blk_e = jnp.minimum(jnp.sum(pad_end[None, :] <= blk_row0[:, None], axis=1),
                        N_EXPERTS - 1).astype(jnp.int32)
    gap = padded - counts
    gap_before = jnp.cumsum(gap) - gap
    j_pad = jnp.arange(n_pad, dtype=jnp.int32)
    zero_rows = (j_pad + jnp.sum(jnp.where(gap_before[None, :] <= j_pad[:, None],
                                           counts[None, :], 0), axis=1)).astype(jnp.int32)

    dest_t = _dest(idx_t, rank_t, pad_start.reshape(N_EXPERTS, 1).astype(jnp.int32))
    xs = _dispatch(h2p, dest_t, zero_rows, n_rows)
    ys = _experts(blk_e, n_used, xs, e_gate, e_up, e_down)
    yg = _regroup(ys, dest_t)
    out = _combine(yg, wsel_t.T, xres, row(g_final))
    return out.reshape(b, s, d)
```

```python
import functools

import jax
import jax.numpy as jnp
from jax import lax
from jax.experimental import pallas as pl
from jax.experimental.pallas import tpu as pltpu
from jax.experimental.pallas import tpu_sc as plsc

D_MODEL = 1024
HALF = D_MODEL // 2
POOL_WIDTH = 512
N_POOL_GROUPS = 4
POOL_GROUP = 128
POOL_WINDOWS = (2, 4, 8, 16)
CONV_WIDTH = 512
N_EXPERTS = 64
TOP_K = 8
EXPERT_HIDDEN = 256
SHARED_HIDDEN = 256
ROUTED_SCALE = 2.5
EPS = 1e-6

POOL_HALO = 16
CONV_HALO = 8
TM_MIX = 256
TM_DEST = 2048
ROW_BLOCK = 512
ROW_CHAIN = 256
TM_COMB = 512
VMEM_LIMIT = 56 * 1024 * 1024

SC_CORES = 2
SC_SUBCORES = 16
SC_WORKERS = SC_CORES * SC_SUBCORES
SC_CHUNK = 64

BF16 = jnp.bfloat16
F32 = jnp.float32
U32 = jnp.uint32


def _rms(x, g):
    r = lax.rsqrt(jnp.mean(x * x, axis=-1, keepdims=True) + EPS)
    return (x * r) * g


def _dot(a, b):
    return jnp.dot(a, b, preferred_element_type=F32)


def _pack_pair(a, b):
    ra = lax.bitcast_convert_type(a.astype(BF16).astype(F32), U32)
    rb = lax.bitcast_convert_type(b.astype(BF16).astype(F32), U32)
    return ra | (rb >> 16)


def _unpack_pair(w):
    a = lax.bitcast_convert_type(w & jnp.uint32(0xFFFF0000), F32)
    b = lax.bitcast_convert_type(w << 16, F32)
    return a, b


def _mixer_router_kernel(n_seq_tiles,
                         x_ref, g_mix_ref, w_in_ref, b_gate_ref, w_grp_ref, pool_scale_ref,
                         w_po_ref, conv_w_ref, w_co_ref, w_o_ref, g_ffn_ref,
                         wr_hi_ref, wr_lo_ref, rbias_ref, s_gate_ref, s_up_ref, s_down_ref,
                         xres_ref, h2p_ref, idx_ref, wsel_ref, rank_ref, counts_ref,
                         ext_pool, ext_conv, cnt_carry, tri):
    tm = x_ref.shape[0]
    i = pl.program_id(0)
    st = i % n_seq_tiles

    @pl.when(i == 0)
    def _():
        r = lax.broadcasted_iota(jnp.int32, (tm, tm), 0)
        c = lax.broadcasted_iota(jnp.int32, (tm, tm), 1)
        tri[...] = (r < c).astype(BF16)
        cnt_carry[...] = jnp.zeros_like(cnt_carry)

    @pl.when(st == 0)
    def _():
        ext_pool[0:POOL_HALO, :] = jnp.zeros((POOL_HALO, POOL_WIDTH), F32)
        ext_conv[0:CONV_HALO, :] = jnp.zeros((CONV_HALO, CONV_WIDTH), F32)

    x = x_ref[...]
    hb = _rms(x, g_mix_ref[...]).astype(BF16)

    o0 = POOL_WIDTH
    o1 = o0 + CONV_WIDTH
    o2 = o1 + CONV_WIDTH
    o3 = o2 + CONV_WIDTH

    u = _dot(hb, w_in_ref[:, 0:o0])
    ext_pool[POOL_HALO:POOL_HALO + tm, :] = u
    t_glob = st * tm + lax.broadcasted_iota(jnp.int32, (tm, 1), 0)
    mixed = []
    for gi, w in enumerate(POOL_WINDOWS):
        cols = slice(gi * POOL_GROUP, (gi + 1) * POOL_GROUP)
        ug = u[:, cols]
        acc = ug
        for j in range(1, w):
            acc = acc + ext_pool[POOL_HALO - j:POOL_HALO - j + tm, cols]
        cnt = jnp.minimum(t_glob + 1, w).astype(F32)
        pooled = acc * (1.0 / cnt) - ug
        mixed.append(_dot(pooled.astype(BF16), w_grp_ref[gi]))
    mixed = jnp.concatenate(mixed, axis=1) * pool_scale_ref[...]
    branch_a = _dot(mixed.astype(BF16), w_po_ref[...])
    ext_pool[0:POOL_HALO, :] = ext_pool[tm:tm + POOL_HALO, :]

    gb = _dot(hb, w_in_ref[:, o0:o1])
    gc = _dot(hb, w_in_ref[:, o1:o2])
    v = _dot(hb, w_in_ref[:, o2:o3])
    cv = gc * v
    ext_conv[CONV_HALO:CONV_HALO + tm, :] = cv
    conv = (ext_conv[CONV_HALO - 2:CONV_HALO - 2 + tm, :] * conv_w_ref[0:1, :]
            + ext_conv[CONV_HALO - 1:CONV_HALO - 1 + tm, :] * conv_w_ref[1:2, :]
            + cv * conv_w_ref[2:3, :])
    branch_b = _dot((gb * conv).astype(BF16), w_co_ref[...])
    ext_conv[0:CONV_HALO, :] = ext_conv[tm:tm + CONV_HALO, :]

    gate_a = jax.nn.sigmoid(_dot(hb, w_in_ref[:, o3:o3 + D_MODEL]) + b_gate_ref[:, 0:D_MODEL])
    merged = gate_a * branch_a
    gate_b = jax.nn.sigmoid(_dot(hb, w_in_ref[:, o3 + D_MODEL:o3 + 2 * D_MODEL])
                            + b_gate_ref[:, D_MODEL:2 * D_MODEL])
    merged = merged + gate_b * branch_b
    x1 = x + _dot(merged.astype(BF16), w_o_ref[...])

    h2 = _rms(x1, g_ffn_ref[...])
    h2p_ref[...] = _pack_pair(h2[:, 0:HALF], h2[:, HALF:D_MODEL])
    h2b = h2.astype(BF16)
    sg = _dot(h2b, s_gate_ref[...])
    su = _dot(h2b, s_up_ref[...])
    shared = _dot((sg * jax.nn.sigmoid(sg) * su).astype(BF16), s_down_ref[...])
    xres_ref[...] = x1 + shared

    h2_lo = (h2 - h2b.astype(F32)).astype(BF16)
    nt = (((1,), (1,)), ((), ()))
    logits = (lax.dot_general(wr_hi_ref[...], h2b, nt, preferred_element_type=F32)
              + lax.dot_general(wr_hi_ref[...], h2_lo, nt, preferred_element_type=F32)
              + lax.dot_general(wr_lo_ref[...], h2b, nt, preferred_element_type=F32))
    scores = jax.nn.sigmoid(logits)
    sel = scores + rbias_ref[...]
    eidx = lax.broadcasted_iota(jnp.int32, (N_EXPERTS, tm), 0).astype(F32)
    e_rows, w_rows = [], []
    mask = jnp.zeros((N_EXPERTS, tm), F32)
    for _ in range(TOP_K):
        m = jnp.max(sel, axis=0, keepdims=True)
        ek = jnp.min(jnp.where(sel == m, eidx, float(N_EXPERTS)), axis=0, keepdims=True)
        oh = eidx == ek
        w_rows.append(jnp.sum(jnp.where(oh, scores, 0.0), axis=0, keepdims=True))
        e_rows.append(ek)
        mask = mask + oh.astype(F32)
        sel = jnp.where(oh, -jnp.inf, sel)
    wsum = w_rows[0]
    for k in range(1, TOP_K):
        wsum = wsum + w_rows[k]

    before = _dot(mask.astype(BF16), tri[...]) + cnt_carry[...]
    for k in range(TOP_K):
        oh = eidx == e_rows[k]
        rank_ref[k:k + 1, :] = jnp.sum(jnp.where(oh, before, 0.0), axis=0,
                                       keepdims=True).astype(jnp.int32)
        idx_ref[k:k + 1, :] = e_rows[k].astype(jnp.int32)
        wsel_ref[k:k + 1, :] = w_rows[k] / wsum * ROUTED_SCALE
    total = cnt_carry[...] + jnp.sum(mask, axis=1, keepdims=True)
    cnt_carry[...] = total
    counts_ref[...] = total.astype(jnp.int32)


def _mixer_router(x2d, seq_len, g_mix, w_in, b_gate, w_grp, pool_scale, w_po, conv_w, w_co, w_o,
                  g_ffn, wr_hi, wr_lo, rbias, s_gate, s_up, s_down):
    t_tok = x2d.shape[0]
    tm = TM_MIX
    n_seq_tiles = seq_len // tm
    const = lambda shape: pl.BlockSpec(shape, lambda i: (0,) * len(shape),
                                       pipeline_mode=pl.Buffered(1))
    row_blk = pl.BlockSpec((tm, D_MODEL), lambda i: (i, 0))
    half_blk = pl.BlockSpec((tm, HALF), lambda i: (i, 0))
    slot_blk = pl.BlockSpec((TOP_K, tm), lambda i: (0, i))
    return pl.pallas_call(
        functools.partial(_mixer_router_kernel, n_seq_tiles),
        grid=(t_tok // tm,),
        in_specs=[row_blk, const(g_mix.shape), const(w_in.shape), const(b_gate.shape),
                  const(w_grp.shape), const(pool_scale.shape), const(w_po.shape),
                  const(conv_w.shape), const(w_co.shape), const(w_o.shape), const(g_ffn.shape),
                  const(wr_hi.shape), const(wr_lo.shape), const(rbias.shape),
                  const(s_gate.shape), const(s_up.shape), const(s_down.shape)],
        out_specs=[row_blk, half_blk, slot_blk, slot_blk, slot_blk,
                   pl.BlockSpec((N_EXPERTS, 1), lambda i: (0, 0))],
        out_shape=[jax.ShapeDtypeStruct((t_tok, D_MODEL), F32),
                   jax.ShapeDtypeStruct((t_tok, HALF), U32),
                   jax.ShapeDtypeStruct((TOP_K, t_tok), jnp.int32),
                   jax.ShapeDtypeStruct((TOP_K, t_tok), F32),
                   jax.ShapeDtypeStruct((TOP_K, t_tok), jnp.int32),
                   jax.ShapeDtypeStruct((N_EXPERTS, 1), jnp.int32)],
        scratch_shapes=[pltpu.VMEM((POOL_HALO + tm, POOL_WIDTH), F32),
                        pltpu.VMEM((CONV_HALO + tm, CONV_WIDTH), F32),
                        pltpu.VMEM((N_EXPERTS, 1), F32),
                        pltpu.VMEM((tm, tm), BF16)],
        compiler_params=pltpu.CompilerParams(dimension_semantics=("arbitrary",),
                                             vmem_limit_bytes=VMEM_LIMIT),
        name="mixer_router",
    )(x2d, g_mix, w_in, b_gate, w_grp, pool_scale, w_po, conv_w, w_co, w_o, g_ffn,
      wr_hi, wr_lo, rbias, s_gate, s_up, s_down)


def _dest_kernel(idx_ref, rank_ref, pad_start_ref, dest_ref):
    tm = idx_ref.shape[1]
    eidx = lax.broadcasted_iota(jnp.int32, (N_EXPERTS, tm), 0)
    ps = pad_start_ref[...].astype(F32)
    for k in range(TOP_K):
        oh = eidx == idx_ref[k:k + 1, :]
        start = jnp.sum(jnp.where(oh, ps, 0.0), axis=0, keepdims=True)
        dest_ref[k:k + 1, :] = start.astype(jnp.int32) + rank_ref[k:k + 1, :]


def _dest(idx_t, rank_t, pad_start):
    t_tok = idx_t.shape[1]
    slot_blk = pl.BlockSpec((TOP_K, TM_DEST), lambda i: (0, i))
    return pl.pallas_call(
        _dest_kernel,
        grid=(t_tok // TM_DEST,),
        in_specs=[slot_blk, slot_blk, pl.BlockSpec((N_EXPERTS, 1), lambda i: (0, 0))],
        out_specs=slot_blk,
        out_shape=jax.ShapeDtypeStruct((TOP_K, t_tok), jnp.int32),
        compiler_params=pltpu.CompilerParams(dimension_semantics=("arbitrary",)),
        name="dest",
    )(idx_t, rank_t, pad_start)


def _sc_mesh():
    return plsc.VectorSubcoreMesh(core_axis_name="c", subcore_axis_name="s")


def _sc_worker_id():
    return lax.axis_index("s") * SC_CORES + lax.axis_index("c")


def _dispatch(h2p, dest_t, zero_rows, n_rows):
    t_tok, width = h2p.shape
    per_w = t_tok // SC_WORKERS
    n_chunks = per_w // SC_CHUNK
    z_chunks = zero_rows.shape[0] // (SC_WORKERS * SC_CHUNK)
    dest_w = (dest_t.reshape(TOP_K, SC_WORKERS, n_chunks, SC_CHUNK)
              .transpose(1, 2, 0, 3).reshape(SC_WORKERS * n_chunks * TOP_K, SC_CHUNK))
    zero_w = zero_rows.reshape(SC_WORKERS * z_chunks, SC_CHUNK)

    @functools.partial(
        pl.kernel, mesh=_sc_mesh(),
        out_type=jax.ShapeDtypeStruct((n_rows, width), h2p.dtype),
        scratch_types=[pltpu.VMEM((n_chunks * TOP_K, SC_CHUNK), jnp.int32),
                       pltpu.VMEM((z_chunks, SC_CHUNK), jnp.int32),
                       pltpu.VMEM((2, SC_CHUNK, width), h2p.dtype),
                       pltpu.VMEM((SC_CHUNK, width), h2p.dtype),
                       pltpu.SemaphoreType.DMA, pltpu.SemaphoreType.DMA,
                       pltpu.SemaphoreType.DMA],
        name="dispatch",
    )
    def k(h2p_hbm, dest_hbm, zidx_hbm, zsrc_hbm, xs_hbm, idx_v, zidx_v, rows_v, zero_v,
          gsem, wsem, zsem):
        wid = _sc_worker_id()
        base = wid * per_w
        pltpu.sync_copy(dest_hbm.at[pl.ds(wid * n_chunks * TOP_K, n_chunks * TOP_K)], idx_v)
        pltpu.sync_copy(zidx_hbm.at[pl.ds(wid * z_chunks, z_chunks)], zidx_v)
        pltpu.sync_copy(zsrc_hbm, zero_v)

        def zput(j):
            return pltpu.make_async_copy(zero_v, xs_hbm.at[zidx_v.at[j]], zsem)

        for j in range(z_chunks):
            zput(j).start()

        def get(j, slot):
            return pltpu.make_async_copy(h2p_hbm.at[pl.ds(base + j * SC_CHUNK, SC_CHUNK)],
                                         rows_v.at[slot], gsem)

        def put(j, slot, kk):
            return pltpu.make_async_copy(rows_v.at[slot], xs_hbm.at[idx_v.at[j * TOP_K + kk]], wsem)

        get(0, 0).start()

        @pl.loop(0, n_chunks, step=2)
        def _(j):
            for b in range(2):
                jj = j + b
                get(jj, b).wait()

                @pl.when(jj >= 1)
                def _():
                    for kk in range(TOP_K):
                        put(jj - 1, 1 - b, kk).wait()

                @pl.when(jj + 1 < n_chunks)
                def _():
                    get(jj + 1, 1 - b).start()
                for kk in range(TOP_K):
                    put(jj, b, kk).start()

        for kk in range(TOP_K):
            put(n_chunks - 1, (n_chunks - 1) % 2, kk).wait()
        for j in range(z_chunks):
            zput(j).wait()

    return k(h2p, dest_w, zero_w, jnp.zeros((SC_CHUNK, width), h2p.dtype))


def _regroup(ys, dest_t):
    width = ys.shape[1]
    n_slots, t_tok = dest_t.shape
    m_rows = n_slots * t_tok
    per_w = m_rows // SC_WORKERS
    n_chunks = per_w // SC_CHUNK
    idx2 = dest_t.reshape(SC_WORKERS * n_chunks, SC_CHUNK)

    @functools.partial(
        pl.kernel, mesh=_sc_mesh(),
        out_type=jax.ShapeDtypeStruct((m_rows, width), ys.dtype),
        scratch_types=[pltpu.VMEM((n_chunks, SC_CHUNK), jnp.int32),
                       pltpu.VMEM((2, SC_CHUNK, width), ys.dtype),
                       pltpu.SemaphoreType.DMA, pltpu.SemaphoreType.DMA],
        name="regroup",
    )
    def k(ys_hbm, idx_hbm, out_hbm, idx_v, rows_v, gsem, wsem):
        wid = _sc_worker_id()
        base = wid * per_w
        pltpu.sync_copy(idx_hbm.at[pl.ds(wid * n_chunks, n_chunks)], idx_v)

        def get(j, slot):
            return pltpu.make_async_copy(ys_hbm.at[idx_v.at[j]], rows_v.at[slot], gsem)

        def put(j, slot):
            return pltpu.make_async_copy(rows_v.at[slot],
                                         out_hbm.at[pl.ds(base + j * SC_CHUNK, SC_CHUNK)], wsem)

        get(0, 0).start()

        @pl.loop(0, n_chunks, step=2)
        def _(j):
            for b in range(2):
                jj = j + b
                get(jj, b).wait()

                @pl.when(jj >= 1)
                def _():
                    put(jj - 1, 1 - b).wait()

                @pl.when(jj + 1 < n_chunks)
                def _():
                    get(jj + 1, 1 - b).start()
                put(jj, b).start()

        put(n_chunks - 1, (n_chunks - 1) % 2).wait()

    return k(ys, idx2).reshape(n_slots, t_tok, width)


def _experts_kernel(blk_e_ref, n_used_ref, xs_ref, eg_ref, eu_ref, ed_ref, ys_ref, wg, wu, wd):
    b = pl.program_id(0)
    used = b < n_used_ref[0]

    @pl.when(used)
    def _():
        prev = blk_e_ref[jnp.maximum(b - 1, 0)]
        new_expert = jnp.logical_or(b == 0, blk_e_ref[b] != prev)

        @pl.when(new_expert)
        def _():
            wg[...] = eg_ref[0].astype(BF16)
            wu[...] = eu_ref[0].astype(BF16)
            wd[...] = ed_ref[0].astype(BF16)

        for c in range(ROW_BLOCK // ROW_CHAIN):
            rows = slice(c * ROW_CHAIN, (c + 1) * ROW_CHAIN)
            xa, xb = _unpack_pair(xs_ref[rows, :])
            xb16 = jnp.concatenate([xa, xb], axis=1).astype(BF16)
            g = _dot(xb16, wg[...])
            up = _dot(xb16, wu[...])
            hid = (g * jax.nn.sigmoid(g) * up).astype(BF16)
            y = _dot(hid, wd[...])
            ys_ref[rows, :] = _pack_pair(y[:, 0:HALF], y[:, HALF:D_MODEL])

    @pl.when(jnp.logical_not(used))
    def _():
        ys_ref[...] = jnp.zeros_like(ys_ref)


def _experts(blk_e, n_used, xs, e_gate, e_up, e_down):
    n_rows = xs.shape[0]
    n_blk = n_rows // ROW_BLOCK
    grid_spec = pltpu.PrefetchScalarGridSpec(
        num_scalar_prefetch=2,
        grid=(n_blk,),
        in_specs=[pl.BlockSpec((ROW_BLOCK, HALF),
                               lambda b, be, nu: (jnp.minimum(b, nu[0] - 1), 0)),
                  pl.BlockSpec((1, D_MODEL, EXPERT_HIDDEN), lambda b, be, nu: (be[b], 0, 0)),
                  pl.BlockSpec((1, D_MODEL, EXPERT_HIDDEN), lambda b, be, nu: (be[b], 0, 0)),
                  pl.BlockSpec((1, EXPERT_HIDDEN, D_MODEL), lambda b, be, nu: (be[b], 0, 0))],
        out_specs=pl.BlockSpec((ROW_BLOCK, HALF), lambda b, be, nu: (b, 0)),
        scratch_shapes=[pltpu.VMEM((D_MODEL, EXPERT_HIDDEN), BF16),
                        pltpu.VMEM((D_MODEL, EXPERT_HIDDEN), BF16),
                        pltpu.VMEM((EXPERT_HIDDEN, D_MODEL), BF16)],
    )
    return pl.pallas_call(
        _experts_kernel,
        grid_spec=grid_spec,
        out_shape=jax.ShapeDtypeStruct((n_rows, HALF), U32),
        compiler_params=pltpu.CompilerParams(dimension_semantics=("arbitrary",)),
        name="experts",
    )(blk_e, n_used, xs, e_gate, e_up, e_down)


def _combine_kernel(yg_ref, wsel_ref, xres_ref, g_final_ref, out_ref):
    xres = xres_ref[...]
    acc_a = xres[:, 0:HALF]
    acc_b = xres[:, HALF:D_MODEL]
    wsel = wsel_ref[...]
    for k in range(TOP_K):
        ya, yb = _unpack_pair(yg_ref[k])
        wk = wsel[:, k:k + 1]
        acc_a = acc_a + ya * wk
        acc_b = acc_b + yb * wk
    out_ref[...] = _rms(jnp.concatenate([acc_a, acc_b], axis=1), g_final_ref[...])


def _combine(yg, wsel_tk, xres, g_final):
    t_tok = xres.shape[0]
    tm = TM_COMB
    return pl.pallas_call(
        _combine_kernel,
        grid=(t_tok // tm,),
        in_specs=[pl.BlockSpec((TOP_K, tm, HALF), lambda i: (0, i, 0)),
                  pl.BlockSpec((tm, TOP_K), lambda i: (i, 0)),
                  pl.BlockSpec((tm, D_MODEL), lambda i: (i, 0)),
                  pl.BlockSpec((1, D_MODEL), lambda i: (0, 0))],
        out_specs=pl.BlockSpec((tm, D_MODEL), lambda i: (i, 0)),
        out_shape=jax.ShapeDtypeStruct((t_tok, D_MODEL), F32),
        compiler_params=pltpu.CompilerParams(dimension_semantics=("arbitrary",),
                                             vmem_limit_bytes=VMEM_LIMIT),
        name="combine",
    )(yg, wsel_tk, xres, g_final)


def kernel(x, g_mix, w_in, b_gate, w_pool_group, pool_scale, w_pool_out, conv_w, w_conv_out, w_o,
           g_ffn, w_router, router_bias, e_gate, e_up, e_down, s_gate, s_up, s_down, g_final):
    b, s, d = x.shape
    t_tok = b * s
    n_pad = N_EXPERTS * ROW_BLOCK
    assert d == D_MODEL and s % TM_MIX == 0 and TM_MIX >= POOL_HALO
    assert t_tok % TM_DEST == 0 and t_tok % TM_COMB == 0
    assert t_tok % (2 * SC_WORKERS * SC_CHUNK) == 0 and n_pad % (SC_WORKERS * SC_CHUNK) == 0

    row = lambda a: a.reshape(1, -1)
    wr_t = w_router.T.astype(F32)
    wr_hi = wr_t.astype(BF16)
    wr_lo = (wr_t - wr_hi.astype(F32)).astype(BF16)

    xres, h2p, idx_t, wsel_t, rank_t, counts = _mixer_router(
        x.reshape(t_tok, d), s, row(g_mix), w_in.astype(BF16), row(b_gate),
        w_pool_group.astype(BF16), row(pool_scale), w_pool_out.astype(BF16), conv_w,
        w_conv_out.astype(BF16), w_o.astype(BF16), row(g_ffn), wr_hi, wr_lo,
        router_bias.astype(F32).reshape(N_EXPERTS, 1), s_gate.astype(BF16), s_up.astype(BF16),
        s_down.astype(BF16))

    counts = counts.reshape(N_EXPERTS)
    padded = (counts + ROW_BLOCK - 1) // ROW_BLOCK * ROW_BLOCK
    pad_end = jnp.cumsum(padded)
    pad_start = pad_end - padded
    n_rows = t_tok * TOP_K + n_pad
    n_blk = n_rows // ROW_BLOCK
    n_used = (pad_end[-1:] // ROW_BLOCK).astype(jnp.int32)
    blk_row0 = jnp.arange(n_blk, dtype=jnp.int32) * ROW_BLOCK
    blk_e = jnp.minimum(jnp.sum(pad_end[None, :] <= blk_row0[:, None], axis=1),
                        N_EXPERTS - 1).astype(jnp.int32)
    gap = padded - counts
    gap_before = jnp.cumsum(gap) - gap
    j_pad = jnp.arange(n_pad, dtype=jnp.int32)
    zero_rows = (j_pad + jnp.sum(jnp.where(gap_before[None, :] <= j_pad[:, None],
                                           counts[None, :], 0), axis=1)).astype(jnp.int32)

    dest_t = _dest(idx_t, rank_t, pad_start.reshape(N_EXPERTS, 1).astype(jnp.int32))
    xs = _dispatch(h2p, dest_t, zero_rows, n_rows)
    ys = _experts(blk_e, n_used, xs, e_gate, e_up, e_down)
    yg = _regroup(ys, dest_t)
    out = _combine(yg, wsel_t.T, xres, row(g_final))
    return out.reshape(b, s, d)
```

```python
import functools

import jax
import jax.numpy as jnp
from jax import lax
from jax.experimental import pallas as pl
from jax.experimental.pallas import tpu as pltpu
from jax.experimental.pallas import tpu_sc as plsc

D_MODEL = 1024
HALF = D_MODEL // 2
POOL_WIDTH = 512
N_POOL_GROUPS = 4
POOL_GROUP = 128
POOL_WINDOWS = (2, 4, 8, 16)
CONV_WIDTH = 512
N_EXPERTS = 64
TOP_K = 8
EXPERT_HIDDEN = 256
SHARED_HIDDEN = 256
ROUTED_SCALE = 2.5
EPS = 1e-6

POOL_HALO = 16
CONV_HALO = 8
TM_MIX = 256
TM_DEST = 2048
ROW_BLOCK = 256
TM_COMB = 512
VMEM_LIMIT = 56 * 1024 * 1024

SC_CORES = 2
SC_SUBCORES = 16
SC_WORKERS = SC_CORES * SC_SUBCORES
SC_CHUNK = 64

BF16 = jnp.bfloat16
F32 = jnp.float32
U32 = jnp.uint32


def _rms(x, g):
    r = lax.rsqrt(jnp.mean(x * x, axis=-1, keepdims=True) + EPS)
    return (x * r) * g


def _dot(a, b):
    return jnp.dot(a, b, preferred_element_type=F32)


def _pack_pair(a, b):
    ra = lax.bitcast_convert_type(a.astype(BF16).astype(F32), U32)
    rb = lax.bitcast_convert_type(b.astype(BF16).astype(F32), U32)
    return ra | (rb >> 16)


def _unpack_pair(w):
    a = lax.bitcast_convert_type(w & jnp.uint32(0xFFFF0000), F32)
    b = lax.bitcast_convert_type(w << 16, F32)
    return a, b


def _mixer_router_kernel(n_seq_tiles,
                         x_ref, g_mix_ref, w_in_ref, b_gate_ref, w_grp_ref, pool_scale_ref,
                         w_po_ref, conv_w_ref, w_co_ref, w_o_ref, g_ffn_ref,
                         wr_hi_ref, wr_lo_ref, rbias_ref, s_gate_ref, s_up_ref, s_down_ref,
                         xres_ref, h2p_ref, idx_ref, wsel_ref, rank_ref, counts_ref,
                         ext_pool, ext_conv, cnt_carry, tri):
    tm = x_ref.shape[0]
    i = pl.program_id(0)
    st = i % n_seq_tiles

    @pl.when(i == 0)
    def _():
        r = lax.broadcasted_iota(jnp.int32, (tm, tm), 0)
        c = lax.broadcasted_iota(jnp.int32, (tm, tm), 1)
        tri[...] = (r < c).astype(BF16)
        cnt_carry[...] = jnp.zeros_like(cnt_carry)

    @pl.when(st == 0)
    def _():
        ext_pool[0:POOL_HALO, :] = jnp.zeros((POOL_HALO, POOL_WIDTH), F32)
        ext_conv[0:CONV_HALO, :] = jnp.zeros((CONV_HALO, CONV_WIDTH), F32)

    x = x_ref[...]
    hb = _rms(x, g_mix_ref[...]).astype(BF16)

    o0 = POOL_WIDTH
    o1 = o0 + CONV_WIDTH
    o2 = o1 + CONV_WIDTH
    o3 = o2 + CONV_WIDTH

    u = _dot(hb, w_in_ref[:, 0:o0])
    ext_pool[POOL_HALO:POOL_HALO + tm, :] = u
    t_glob = st * tm + lax.broadcasted_iota(jnp.int32, (tm, 1), 0)
    mixed = []
    for gi, w in enumerate(POOL_WINDOWS):
        cols = slice(gi * POOL_GROUP, (gi + 1) * POOL_GROUP)
        ug = u[:, cols]
        acc = ug
        for j in range(1, w):
            acc = acc + ext_pool[POOL_HALO - j:POOL_HALO - j + tm, cols]
        cnt = jnp.minimum(t_glob + 1, w).astype(F32)
        pooled = acc * (1.0 / cnt) - ug
        mixed.append(_dot(pooled.astype(BF16), w_grp_ref[gi]))
    mixed = jnp.concatenate(mixed, axis=1) * pool_scale_ref[...]
    branch_a = _dot(mixed.astype(BF16), w_po_ref[...])
    ext_pool[0:POOL_HALO, :] = ext_pool[tm:tm + POOL_HALO, :]

    gb = _dot(hb, w_in_ref[:, o0:o1])
    gc = _dot(hb, w_in_ref[:, o1:o2])
    v = _dot(hb, w_in_ref[:, o2:o3])
    cv = gc * v
    ext_conv[CONV_HALO:CONV_HALO + tm, :] = cv
    conv = (ext_conv[CONV_HALO - 2:CONV_HALO - 2 + tm, :] * conv_w_ref[0:1, :]
            + ext_conv[CONV_HALO - 1:CONV_HALO - 1 + tm, :] * conv_w_ref[1:2, :]
            + cv * conv_w_ref[2:3, :])
    branch_b = _dot((gb * conv).astype(BF16), w_co_ref[...])
    ext_conv[0:CONV_HALO, :] = ext_conv[tm:tm + CONV_HALO, :]

    gate_a = jax.nn.sigmoid(_dot(hb, w_in_ref[:, o3:o3 + D_MODEL]) + b_gate_ref[:, 0:D_MODEL])
    merged = gate_a * branch_a
    gate_b = jax.nn.sigmoid(_dot(hb, w_in_ref[:, o3 + D_MODEL:o3 + 2 * D_MODEL])
                            + b_gate_ref[:, D_MODEL:2 * D_MODEL])
    merged = merged + gate_b * branch_b
    x1 = x + _dot(merged.astype(BF16), w_o_ref[...])

    h2 = _rms(x1, g_ffn_ref[...])
    h2p_ref[...] = _pack_pair(h2[:, 0:HALF], h2[:, HALF:D_MODEL])
    h2b = h2.astype(BF16)
    sg = _dot(h2b, s_gate_ref[...])
    su = _dot(h2b, s_up_ref[...])
    shared = _dot((sg * jax.nn.sigmoid(sg) * su).astype(BF16), s_down_ref[...])
    xres_ref[...] = x1 + shared

    h2_lo = (h2 - h2b.astype(F32)).astype(BF16)
    nt = (((1,), (1,)), ((), ()))
    logits = (lax.dot_general(wr_hi_ref[...], h2b, nt, preferred_element_type=F32)
              + lax.dot_general(wr_hi_ref[...], h2_lo, nt, preferred_element_type=F32)
              + lax.dot_general(wr_lo_ref[...], h2b, nt, preferred_element_type=F32))
    scores = jax.nn.sigmoid(logits)
    sel = scores + rbias_ref[...]
    eidx = lax.broadcasted_iota(jnp.int32, (N_EXPERTS, tm), 0).astype(F32)
    e_rows, w_rows = [], []
    mask = jnp.zeros((N_EXPERTS, tm), F32)
    for _ in range(TOP_K):
        m = jnp.max(sel, axis=0, keepdims=True)
        ek = jnp.min(jnp.where(sel == m, eidx, float(N_EXPERTS)), axis=0, keepdims=True)
        oh = eidx == ek
        w_rows.append(jnp.sum(jnp.where(oh, scores, 0.0), axis=0, keepdims=True))
        e_rows.append(ek)
        mask = mask + oh.astype(F32)
        sel = jnp.where(oh, -jnp.inf, sel)
    wsum = w_rows[0]
    for k in range(1, TOP_K):
        wsum = wsum + w_rows[k]

    before = _dot(mask.astype(BF16), tri[...]) + cnt_carry[...]
    for k in range(TOP_K):
        oh = eidx == e_rows[k]
        rank_ref[k:k + 1, :] = jnp.sum(jnp.where(oh, before, 0.0), axis=0,
                                       keepdims=True).astype(jnp.int32)
        idx_ref[k:k + 1, :] = e_rows[k].astype(jnp.int32)
        wsel_ref[k:k + 1, :] = w_rows[k] / wsum * ROUTED_SCALE
    total = cnt_carry[...] + jnp.sum(mask, axis=1, keepdims=True)
    cnt_carry[...] = total
    counts_ref[...] = total.astype(jnp.int32)


def _mixer_router(x2d, seq_len, g_mix, w_in, b_gate, w_grp, pool_scale, w_po, conv_w, w_co, w_o,
                  g_ffn, wr_hi, wr_lo, rbias, s_gate, s_up, s_down):
    t_tok = x2d.shape[0]
    tm = TM_MIX
    n_seq_tiles = seq_len // tm
    const = lambda shape: pl.BlockSpec(shape, lambda i: (0,) * len(shape),
                                       pipeline_mode=pl.Buffered(1))
    row_blk = pl.BlockSpec((tm, D_MODEL), lambda i: (i, 0))
    half_blk = pl.BlockSpec((tm, HALF), lambda i: (i, 0))
    slot_blk = pl.BlockSpec((TOP_K, tm), lambda i: (0, i))
    return pl.pallas_call(
        functools.partial(_mixer_router_kernel, n_seq_tiles),
        grid=(t_tok // tm,),
        in_specs=[row_blk, const(g_mix.shape), const(w_in.shape), const(b_gate.shape),
                  const(w_grp.shape), const(pool_scale.shape), const(w_po.shape),
                  const(conv_w.shape), const(w_co.shape), const(w_o.shape), const(g_ffn.shape),
                  const(wr_hi.shape), const(wr_lo.shape), const(rbias.shape),
                  const(s_gate.shape), const(s_up.shape), const(s_down.shape)],
        out_specs=[row_blk, half_blk, slot_blk, slot_blk, slot_blk,
                   pl.BlockSpec((N_EXPERTS, 1), lambda i: (0, 0))],
        out_shape=[jax.ShapeDtypeStruct((t_tok, D_MODEL), F32),
                   jax.ShapeDtypeStruct((t_tok, HALF), U32),
                   jax.ShapeDtypeStruct((TOP_K, t_tok), jnp.int32),
                   jax.ShapeDtypeStruct((TOP_K, t_tok), F32),
                   jax.ShapeDtypeStruct((TOP_K, t_tok), jnp.int32),
                   jax.ShapeDtypeStruct((N_EXPERTS, 1), jnp.int32)],
        scratch_shapes=[pltpu.VMEM((POOL_HALO + tm, POOL_WIDTH), F32),
                        pltpu.VMEM((CONV_HALO + tm, CONV_WIDTH), F32),
                        pltpu.VMEM((N_EXPERTS, 1), F32),
                        pltpu.VMEM((tm, tm), BF16)],
        compiler_params=pltpu.CompilerParams(dimension_semantics=("arbitrary",),
                                             vmem_limit_bytes=VMEM_LIMIT),
        name="mixer_router",
    )(x2d, g_mix, w_in, b_gate, w_grp, pool_scale, w_po, conv_w, w_co, w_o, g_ffn,
      wr_hi, wr_lo, rbias, s_gate, s_up, s_down)


def _dest_kernel(idx_ref, rank_ref, pad_start_ref, dest_ref):
    tm = idx_ref.shape[1]
    eidx = lax.broadcasted_iota(jnp.int32, (N_EXPERTS, tm), 0)
    ps = pad_start_ref[...].astype(F32)
    for k in range(TOP_K):
        oh = eidx == idx_ref[k:k + 1, :]
        start = jnp.sum(jnp.where(oh, ps, 0.0), axis=0, keepdims=True)
        dest_ref[k:k + 1, :] = start.astype(jnp.int32) + rank_ref[k:k + 1, :]


def _dest(idx_t, rank_t, pad_start):
    t_tok = idx_t.shape[1]
    slot_blk = pl.BlockSpec((TOP_K, TM_DEST), lambda i: (0, i))
    return pl.pallas_call(
        _dest_kernel,
        grid=(t_tok // TM_DEST,),
        in_specs=[slot_blk, slot_blk, pl.BlockSpec((N_EXPERTS, 1), lambda i: (0, 0))],
        out_specs=slot_blk,
        out_shape=jax.ShapeDtypeStruct((TOP_K, t_tok), jnp.int32),
        compiler_params=pltpu.CompilerParams(dimension_semantics=("arbitrary",)),
        name="dest",
    )(idx_t, rank_t, pad_start)


def _sc_mesh():
    return plsc.VectorSubcoreMesh(core_axis_name="c", subcore_axis_name="s")


def _sc_worker_id():
    return lax.axis_index("s") * SC_CORES + lax.axis_index("c")


def _dispatch(h2p, dest_t, zero_rows, n_rows):
    t_tok, width = h2p.shape
    per_w = t_tok // SC_WORKERS
    n_chunks = per_w // SC_CHUNK
    z_chunks = zero_rows.shape[0] // (SC_WORKERS * SC_CHUNK)
    dest_w = (dest_t.reshape(TOP_K, SC_WORKERS, n_chunks, SC_CHUNK)
              .transpose(1, 2, 0, 3).reshape(SC_WORKERS * n_chunks * TOP_K, SC_CHUNK))
    zero_w = zero_rows.reshape(SC_WORKERS * z_chunks, SC_CHUNK)

    @functools.partial(
        pl.kernel, mesh=_sc_mesh(),
        out_type=jax.ShapeDtypeStruct((n_rows, width), h2p.dtype),
        scratch_types=[pltpu.VMEM((n_chunks * TOP_K, SC_CHUNK), jnp.int32),
                       pltpu.VMEM((z_chunks, SC_CHUNK), jnp.int32),
                       pltpu.VMEM((2, SC_CHUNK, width), h2p.dtype),
                       pltpu.VMEM((SC_CHUNK, width), h2p.dtype),
                       pltpu.SemaphoreType.DMA, pltpu.SemaphoreType.DMA,
                       pltpu.SemaphoreType.DMA],
        name="dispatch",
    )
    def k(h2p_hbm, dest_hbm, zidx_hbm, zsrc_hbm, xs_hbm, idx_v, zidx_v, rows_v, zero_v,
          gsem, wsem, zsem):
        wid = _sc_worker_id()
        base = wid * per_w
        pltpu.sync_copy(dest_hbm.at[pl.ds(wid * n_chunks * TOP_K, n_chunks * TOP_K)], idx_v)
        pltpu.sync_copy(zidx_hbm.at[pl.ds(wid * z_chunks, z_chunks)], zidx_v)
        pltpu.sync_copy(zsrc_hbm, zero_v)

        def zput(j):
            return pltpu.make_async_copy(zero_v, xs_hbm.at[zidx_v.at[j]], zsem)

        for j in range(z_chunks):
            zput(j).start()

        def get(j, slot):
            return pltpu.make_async_copy(h2p_hbm.at[pl.ds(base + j * SC_CHUNK, SC_CHUNK)],
                                         rows_v.at[slot], gsem)

        def put(j, slot, kk):
            return pltpu.make_async_copy(rows_v.at[slot], xs_hbm.at[idx_v.at[j * TOP_K + kk]], wsem)

        get(0, 0).start()

        @pl.loop(0, n_chunks, step=2)
        def _(j):
            for b in range(2):
                jj = j + b
                get(jj, b).wait()

                @pl.when(jj >= 1)
                def _():
                    for kk in range(TOP_K):
                        put(jj - 1, 1 - b, kk).wait()

                @pl.when(jj + 1 < n_chunks)
                def _():
                    get(jj + 1, 1 - b).start()
                for kk in range(TOP_K):
                    put(jj, b, kk).start()

        for kk in range(TOP_K):
            put(n_chunks - 1, (n_chunks - 1) % 2, kk).wait()
        for j in range(z_chunks):
            zput(j).wait()

    return k(h2p, dest_w, zero_w, jnp.zeros((SC_CHUNK, width), h2p.dtype))


def _regroup(ys, dest_t):
    width = ys.shape[1]
    n_slots, t_tok = dest_t.shape
    m_rows = n_slots * t_tok
    per_w = m_rows // SC_WORKERS
    n_chunks = per_w // SC_CHUNK
    idx2 = dest_t.reshape(SC_WORKERS * n_chunks, SC_CHUNK)

    @functools.partial(
        pl.kernel, mesh=_sc_mesh(),
        out_type=jax.ShapeDtypeStruct((m_rows, width), ys.dtype),
        scratch_types=[pltpu.VMEM((n_chunks, SC_CHUNK), jnp.int32),
                       pltpu.VMEM((2, SC_CHUNK, width), ys.dtype),
                       pltpu.SemaphoreType.DMA, pltpu.SemaphoreType.DMA],
        name="regroup",
    )
    def k(ys_hbm, idx_hbm, out_hbm, idx_v, rows_v, gsem, wsem):
        wid = _sc_worker_id()
        base = wid * per_w
        pltpu.sync_copy(idx_hbm.at[pl.ds(wid * n_chunks, n_chunks)], idx_v)

        def get(j, slot):
            return pltpu.make_async_copy(ys_hbm.at[idx_v.at[j]], rows_v.at[slot], gsem)

        def put(j, slot):
            return pltpu.make_async_copy(rows_v.at[slot],
                                         out_hbm.at[pl.ds(base + j * SC_CHUNK, SC_CHUNK)], wsem)

        get(0, 0).start()

        @pl.loop(0, n_chunks, step=2)
        def _(j):
            for b in range(2):
                jj = j + b
                get(jj, b).wait()

                @pl.when(jj >= 1)
                def _():
                    put(jj - 1, 1 - b).wait()

                @pl.when(jj + 1 < n_chunks)
                def _():
                    get(jj + 1, 1 - b).start()
                put(jj, b).start()

        put(n_chunks - 1, (n_chunks - 1) % 2).wait()

    return k(ys, idx2).reshape(n_slots, t_tok, width)


def _experts_kernel(n_blk, blk_e_ref, next_e_ref, n_used_ref,
                    xs_hbm, eg_hbm, eu_hbm, ed_hbm, ys_hbm,
                    xbuf, ybuf, hid, stg_g, stg_u, stg_d, wg, wu, wd, xsem, ysem, wsem):
    n = n_used_ref[0]

    def x_copy(b, slot):
        return pltpu.make_async_copy(xs_hbm.at[pl.ds(pl.multiple_of(b * ROW_BLOCK, ROW_BLOCK),
                                                     ROW_BLOCK)], xbuf.at[slot], xsem.at[slot])

    def y_copy(b, slot):
        return pltpu.make_async_copy(ybuf.at[slot],
                                     ys_hbm.at[pl.ds(pl.multiple_of(b * ROW_BLOCK, ROW_BLOCK),
                                                     ROW_BLOCK)], ysem.at[slot])

    def w_copies(e):
        return (pltpu.make_async_copy(eg_hbm.at[e], stg_g, wsem.at[0]),
                pltpu.make_async_copy(eu_hbm.at[e], stg_u, wsem.at[1]),
                pltpu.make_async_copy(ed_hbm.at[e], stg_d, wsem.at[2]))

    def switch_expert(e, wslot):
        for cp in w_copies(e):
            cp.wait()
        wg[wslot] = stg_g[...].astype(BF16)
        wu[wslot] = stg_u[...].astype(BF16)
        wd[wslot] = stg_d[...].astype(BF16)
        nxt = next_e_ref[e]

        @pl.when(nxt >= 0)
        def _():
            for cp in w_copies(nxt):
                cp.start()

    def gate_up(slot, wslot):
        xa, xb = _unpack_pair(xbuf[slot])
        xb16 = jnp.concatenate([xa, xb], axis=1).astype(BF16)
        g = _dot(xb16, wg[wslot])
        up = _dot(xb16, wu[wslot])
        hid[slot] = (g * jax.nn.sigmoid(g) * up).astype(BF16)

    def down(slot, wslot):
        y = _dot(hid[slot], wd[wslot])
        ybuf[slot] = _pack_pair(y[:, 0:HALF], y[:, HALF:D_MODEL])

    e0 = blk_e_ref[0]
    for cp in w_copies(e0):
        cp.start()
    x_copy(0, 0).start()
    switch_expert(e0, 0)
    x_copy(0, 0).wait()

    @pl.when(n > 1)
    def _():
        x_copy(1, 1).start()
    gate_up(0, 0)

    def body(b, wslot_prev):
        slot = b % 2
        pslot = 1 - slot
        e = blk_e_ref[b]
        first = e != blk_e_ref[b - 1]
        wslot = jnp.where(first, 1 - wslot_prev, wslot_prev)

        @pl.when(first)
        def _():
            switch_expert(e, wslot)

        x_copy(b, slot).wait()

        @pl.when(b + 1 < n)
        def _():
            x_copy(b + 1, pslot).start()

        @pl.when(b >= 3)
        def _():
            y_copy(b - 3, pslot).wait()

        gate_up(slot, wslot)
        down(pslot, wslot_prev)
        y_copy(b - 1, pslot).start()
        return wslot

    wslot_last = lax.fori_loop(1, n, body, jnp.int32(0))

    last = n - 1
    lslot = last % 2

    @pl.when(n >= 3)
    def _():
        y_copy(last - 2, lslot).wait()
    down(lslot, wslot_last)
    y_copy(last, lslot).start()

    @pl.when(n >= 2)
    def _():
        y_copy(last - 1, 1 - lslot).wait()
    y_copy(last, lslot).wait()

    ybuf[0] = jnp.zeros((ROW_BLOCK, HALF), U32)

    def zero_tail(b, c):
        cp = y_copy(b, 0)
        cp.start()
        cp.wait()
        return c
    lax.fori_loop(n, n_blk, zero_tail, 0)


def _experts(blk_e, next_e, n_used, xs, e_gate, e_up, e_down):
    n_rows = xs.shape[0]
    n_blk = n_rows // ROW_BLOCK
    any_spec = pl.BlockSpec(memory_space=pl.ANY)
    grid_spec = pltpu.PrefetchScalarGridSpec(
        num_scalar_prefetch=3,
        grid=(1,),
        in_specs=[any_spec, any_spec, any_spec, any_spec],
        out_specs=any_spec,
        scratch_shapes=[pltpu.VMEM((2, ROW_BLOCK, HALF), U32),
                        pltpu.VMEM((2, ROW_BLOCK, HALF), U32),
                        pltpu.VMEM((2, ROW_BLOCK, EXPERT_HIDDEN), BF16),
                        pltpu.VMEM((D_MODEL, EXPERT_HIDDEN), F32),
                        pltpu.VMEM((D_MODEL, EXPERT_HIDDEN), F32),
                        pltpu.VMEM((EXPERT_HIDDEN, D_MODEL), F32),
                        pltpu.VMEM((2, D_MODEL, EXPERT_HIDDEN), BF16),
                        pltpu.VMEM((2, D_MODEL, EXPERT_HIDDEN), BF16),
                        pltpu.VMEM((2, EXPERT_HIDDEN, D_MODEL), BF16),
                        pltpu.SemaphoreType.DMA((2,)),
                        pltpu.SemaphoreType.DMA((2,)),
                        pltpu.SemaphoreType.DMA((3,))],
    )
    return pl.pallas_call(
        functools.partial(_experts_kernel, n_blk),
        grid_spec=grid_spec,
        out_shape=jax.ShapeDtypeStruct((n_rows, HALF), U32),
        compiler_params=pltpu.CompilerParams(dimension_semantics=("arbitrary",)),
        name="experts",
    )(blk_e, next_e, n_used, xs, e_gate, e_up, e_down)


def _combine_kernel(yg_ref, wsel_ref, xres_ref, g_final_ref, out_ref):
    xres = xres_ref[...]
    acc_a = xres[:, 0:HALF]
    acc_b = xres[:, HALF:D_MODEL]
    wsel = wsel_ref[...]
    for k in range(TOP_K):
        ya, yb = _unpack_pair(yg_ref[k])
        wk = wsel[:, k:k + 1]
        acc_a = acc_a + ya * wk
        acc_b = acc_b + yb * wk
    out_ref[...] = _rms(jnp.concatenate([acc_a, acc_b], axis=1), g_final_ref[...])


def _combine(yg, wsel_tk, xres, g_final):
    t_tok = xres.shape[0]
    tm = TM_COMB
    return pl.pallas_call(
        _combine_kernel,
        grid=(t_tok // tm,),
        in_specs=[pl.BlockSpec((TOP_K, tm, HALF), lambda i: (0, i, 0)),
                  pl.BlockSpec((tm, TOP_K), lambda i: (i, 0)),
                  pl.BlockSpec((tm, D_MODEL), lambda i: (i, 0)),
                  pl.BlockSpec((1, D_MODEL), lambda i: (0, 0))],
        out_specs=pl.BlockSpec((tm, D_MODEL), lambda i: (i, 0)),
        out_shape=jax.ShapeDtypeStruct((t_tok, D_MODEL), F32),
        compiler_params=pltpu.CompilerParams(dimension_semantics=("arbitrary",),
                                             vmem_limit_bytes=VMEM_LIMIT),
        name="combine",
    )(yg, wsel_tk, xres, g_final)


def kernel(x, g_mix, w_in, b_gate, w_pool_group, pool_scale, w_pool_out, conv_w, w_conv_out, w_o,
           g_ffn, w_router, router_bias, e_gate, e_up, e_down, s_gate, s_up, s_down, g_final):
    b, s, d = x.shape
    t_tok = b * s
    n_pad = N_EXPERTS * ROW_BLOCK
    assert d == D_MODEL and s % TM_MIX == 0 and TM_MIX >= POOL_HALO
    assert t_tok % TM_DEST == 0 and t_tok % TM_COMB == 0
    assert t_tok % (2 * SC_WORKERS * SC_CHUNK) == 0 and n_pad % (SC_WORKERS * SC_CHUNK) == 0

    row = lambda a: a.reshape(1, -1)
    wr_t = w_router.T.astype(F32)
    wr_hi = wr_t.astype(BF16)
    wr_lo = (wr_t - wr_hi.astype(F32)).astype(BF16)

    xres, h2p, idx_t, wsel_t, rank_t, counts = _mixer_router(
        x.reshape(t_tok, d), s, row(g_mix), w_in.astype(BF16), row(b_gate),
        w_pool_group.astype(BF16), row(pool_scale), w_pool_out.astype(BF16), conv_w,
        w_conv_out.astype(BF16), w_o.astype(BF16), row(g_ffn), wr_hi, wr_lo,
        router_bias.astype(F32).reshape(N_EXPERTS, 1), s_gate.astype(BF16), s_up.astype(BF16),
        s_down.astype(BF16))

    counts = counts.reshape(N_EXPERTS)
    padded = (counts + ROW_BLOCK - 1) // ROW_BLOCK * ROW_BLOCK
    pad_end = jnp.cumsum(padded)
    pad_start = pad_end - padded
    n_rows = t_tok * TOP_K + n_pad
    n_blk = n_rows // ROW_BLOCK
    n_used = (pad_end[-1:] // ROW_BLOCK).astype(jnp.int32)
    blk_row0 = jnp.arange(n_blk, dtype=jnp.int32) * ROW_BLOCK
    blk_e = jnp.minimum(jnp.sum(pad_end[None, :] <= blk_row0[:, None], axis=1),
                        N_EXPERTS - 1).astype(jnp.int32)
    e_ids = jnp.arange(N_EXPERTS, dtype=jnp.int32)
    later_used = jnp.logical_and(e_ids[None, :] > e_ids[:, None], padded[None, :] > 0)
    next_e = jnp.min(jnp.where(later_used, e_ids[None, :], N_EXPERTS), axis=1)
    next_e = jnp.where(next_e < N_EXPERTS, next_e, -1).astype(jnp.int32)
    gap = padded - counts
    gap_before = jnp.cumsum(gap) - gap
    j_pad = jnp.arange(n_pad, dtype=jnp.int32)
    zero_rows = (j_pad + jnp.sum(jnp.where(gap_before[None, :] <= j_pad[:, None],
                                           counts[None, :], 0), axis=1)).astype(jnp.int32)

    dest_t = _dest(idx_t, rank_t, pad_start.reshape(N_EXPERTS, 1).astype(jnp.int32))
    xs = _dispatch(h2p, dest_t, zero_rows, n_rows)
    ys = _experts(blk_e, next_e, n_used, xs, e_gate, e_up, e_down)
    yg = _regroup(ys, dest_t)
    out = _combine(yg, wsel_t.T, xres, row(g_final))
    return out.reshape(b, s, d)
```

```python
import functools

import jax
import jax.numpy as jnp
from jax import lax
from jax.experimental import pallas as pl
from jax.experimental.pallas import tpu as pltpu
from jax.experimental.pallas import tpu_sc as plsc

D_MODEL = 1024
HALF = D_MODEL // 2
POOL_WIDTH = 512
N_POOL_GROUPS = 4
POOL_GROUP = 128
POOL_WINDOWS = (2, 4, 8, 16)
CONV_WIDTH = 512
N_EXPERTS = 64
TOP_K = 8
EXPERT_HIDDEN = 256
SHARED_HIDDEN = 256
ROUTED_SCALE = 2.5
EPS = 1e-6

POOL_HALO = 16
CONV_HALO = 8
TM_MIX = 256
TM_DEST = 2048
ROW_BLOCK = 256
ROW_RING = 4
TM_COMB = 512
VMEM_LIMIT = 56 * 1024 * 1024

SC_CORES = 2
SC_SUBCORES = 16
SC_WORKERS = SC_CORES * SC_SUBCORES
SC_CHUNK = 64

BF16 = jnp.bfloat16
F32 = jnp.float32
U32 = jnp.uint32


def _rms(x, g):
    r = lax.rsqrt(jnp.mean(x * x, axis=-1, keepdims=True) + EPS)
    return (x * r) * g


def _dot(a, b):
    return jnp.dot(a, b, preferred_element_type=F32)


def _pack_pair(a, b):
    ra = lax.bitcast_convert_type(a.astype(BF16).astype(F32), U32)
    rb = lax.bitcast_convert_type(b.astype(BF16).astype(F32), U32)
    return ra | (rb >> 16)


def _unpack_pair(w):
    a = lax.bitcast_convert_type(w & jnp.uint32(0xFFFF0000), F32)
    b = lax.bitcast_convert_type(w << 16, F32)
    return a, b


def _mixer_router_kernel(n_seq_tiles,
                         x_ref, g_mix_ref, w_in_ref, b_gate_ref, w_grp_ref, pool_scale_ref,
                         w_po_ref, conv_w_ref, w_co_ref, w_o_ref, g_ffn_ref,
                         wr_hi_ref, wr_lo_ref, rbias_ref, s_gate_ref, s_up_ref, s_down_ref,
                         xres_ref, h2p_ref, idx_ref, wsel_ref, rank_ref, counts_ref,
                         ext_pool, ext_conv, cnt_carry, tri):
    tm = x_ref.shape[0]
    i = pl.program_id(0)
    st = i % n_seq_tiles

    @pl.when(i == 0)
    def _():
        r = lax.broadcasted_iota(jnp.int32, (tm, tm), 0)
        c = lax.broadcasted_iota(jnp.int32, (tm, tm), 1)
        tri[...] = (r < c).astype(BF16)
        cnt_carry[...] = jnp.zeros_like(cnt_carry)

    @pl.when(st == 0)
    def _():
        ext_pool[0:POOL_HALO, :] = jnp.zeros((POOL_HALO, POOL_WIDTH), F32)
        ext_conv[0:CONV_HALO, :] = jnp.zeros((CONV_HALO, CONV_WIDTH), F32)

    x = x_ref[...]
    hb = _rms(x, g_mix_ref[...]).astype(BF16)

    o0 = POOL_WIDTH
    o1 = o0 + CONV_WIDTH
    o2 = o1 + CONV_WIDTH
    o3 = o2 + CONV_WIDTH

    u = _dot(hb, w_in_ref[:, 0:o0])
    ext_pool[POOL_HALO:POOL_HALO + tm, :] = u
    t_glob = st * tm + lax.broadcasted_iota(jnp.int32, (tm, 1), 0)
    mixed = []
    for gi, w in enumerate(POOL_WINDOWS):
        cols = slice(gi * POOL_GROUP, (gi + 1) * POOL_GROUP)
        ug = u[:, cols]
        acc = ug
        for j in range(1, w):
            acc = acc + ext_pool[POOL_HALO - j:POOL_HALO - j + tm, cols]
        cnt = jnp.minimum(t_glob + 1, w).astype(F32)
        pooled = acc * (1.0 / cnt) - ug
        mixed.append(_dot(pooled.astype(BF16), w_grp_ref[gi]))
    mixed = jnp.concatenate(mixed, axis=1) * pool_scale_ref[...]
    branch_a = _dot(mixed.astype(BF16), w_po_ref[...])
    ext_pool[0:POOL_HALO, :] = ext_pool[tm:tm + POOL_HALO, :]

    gb = _dot(hb, w_in_ref[:, o0:o1])
    gc = _dot(hb, w_in_ref[:, o1:o2])
    v = _dot(hb, w_in_ref[:, o2:o3])
    cv = gc * v
    ext_conv[CONV_HALO:CONV_HALO + tm, :] = cv
    conv = (ext_conv[CONV_HALO - 2:CONV_HALO - 2 + tm, :] * conv_w_ref[0:1, :]
            + ext_conv[CONV_HALO - 1:CONV_HALO - 1 + tm, :] * conv_w_ref[1:2, :]
            + cv * conv_w_ref[2:3, :])
    branch_b = _dot((gb * conv).astype(BF16), w_co_ref[...])
    ext_conv[0:CONV_HALO, :] = ext_conv[tm:tm + CONV_HALO, :]

    gate_a = jax.nn.sigmoid(_dot(hb, w_in_ref[:, o3:o3 + D_MODEL]) + b_gate_ref[:, 0:D_MODEL])
    merged = gate_a * branch_a
    gate_b = jax.nn.sigmoid(_dot(hb, w_in_ref[:, o3 + D_MODEL:o3 + 2 * D_MODEL])
                            + b_gate_ref[:, D_MODEL:2 * D_MODEL])
    merged = merged + gate_b * branch_b
    x1 = x + _dot(merged.astype(BF16), w_o_ref[...])

    h2 = _rms(x1, g_ffn_ref[...])
    h2p_ref[...] = _pack_pair(h2[:, 0:HALF], h2[:, HALF:D_MODEL])
    h2b = h2.astype(BF16)
    sg = _dot(h2b, s_gate_ref[...])
    su = _dot(h2b, s_up_ref[...])
    shared = _dot((sg * jax.nn.sigmoid(sg) * su).astype(BF16), s_down_ref[...])
    xres_ref[...] = x1 + shared

    h2_lo = (h2 - h2b.astype(F32)).astype(BF16)
    nt = (((1,), (1,)), ((), ()))
    logits = (lax.dot_general(wr_hi_ref[...], h2b, nt, preferred_element_type=F32)
              + lax.dot_general(wr_hi_ref[...], h2_lo, nt, preferred_element_type=F32)
              + lax.dot_general(wr_lo_ref[...], h2b, nt, preferred_element_type=F32))
    scores = jax.nn.sigmoid(logits)
    sel = scores + rbias_ref[...]
    eidx = lax.broadcasted_iota(jnp.int32, (N_EXPERTS, tm), 0).astype(F32)
    e_rows, w_rows = [], []
    mask = jnp.zeros((N_EXPERTS, tm), F32)
    for _ in range(TOP_K):
        m = jnp.max(sel, axis=0, keepdims=True)
        ek = jnp.min(jnp.where(sel == m, eidx, float(N_EXPERTS)), axis=0, keepdims=True)
        oh = eidx == ek
        w_rows.append(jnp.sum(jnp.where(oh, scores, 0.0), axis=0, keepdims=True))
        e_rows.append(ek)
        mask = mask + oh.astype(F32)
        sel = jnp.where(oh, -jnp.inf, sel)
    wsum = w_rows[0]
    for k in range(1, TOP_K):
        wsum = wsum + w_rows[k]

    before = _dot(mask.astype(BF16), tri[...]) + cnt_carry[...]
    for k in range(TOP_K):
        oh = eidx == e_rows[k]
        rank_ref[k:k + 1, :] = jnp.sum(jnp.where(oh, before, 0.0), axis=0,
                                       keepdims=True).astype(jnp.int32)
        idx_ref[k:k + 1, :] = e_rows[k].astype(jnp.int32)
        wsel_ref[k:k + 1, :] = w_rows[k] / wsum * ROUTED_SCALE
    total = cnt_carry[...] + jnp.sum(mask, axis=1, keepdims=True)
    cnt_carry[...] = total
    counts_ref[...] = total.astype(jnp.int32)


def _mixer_router(x2d, seq_len, g_mix, w_in, b_gate, w_grp, pool_scale, w_po, conv_w, w_co, w_o,
                  g_ffn, wr_hi, wr_lo, rbias, s_gate, s_up, s_down):
    t_tok = x2d.shape[0]
    tm = TM_MIX
    n_seq_tiles = seq_len // tm
    const = lambda shape: pl.BlockSpec(shape, lambda i: (0,) * len(shape),
                                       pipeline_mode=pl.Buffered(1))
    row_blk = pl.BlockSpec((tm, D_MODEL), lambda i: (i, 0))
    half_blk = pl.BlockSpec((tm, HALF), lambda i: (i, 0))
    slot_blk = pl.BlockSpec((TOP_K, tm), lambda i: (0, i))
    return pl.pallas_call(
        functools.partial(_mixer_router_kernel, n_seq_tiles),
        grid=(t_tok // tm,),
        in_specs=[row_blk, const(g_mix.shape), const(w_in.shape), const(b_gate.shape),
                  const(w_grp.shape), const(pool_scale.shape), const(w_po.shape),
                  const(conv_w.shape), const(w_co.shape), const(w_o.shape), const(g_ffn.shape),
                  const(wr_hi.shape), const(wr_lo.shape), const(rbias.shape),
                  const(s_gate.shape), const(s_up.shape), const(s_down.shape)],
        out_specs=[row_blk, half_blk, slot_blk, slot_blk, slot_blk,
                   pl.BlockSpec((N_EXPERTS, 1), lambda i: (0, 0))],
        out_shape=[jax.ShapeDtypeStruct((t_tok, D_MODEL), F32),
                   jax.ShapeDtypeStruct((t_tok, HALF), U32),
                   jax.ShapeDtypeStruct((TOP_K, t_tok), jnp.int32),
                   jax.ShapeDtypeStruct((TOP_K, t_tok), F32),
                   jax.ShapeDtypeStruct((TOP_K, t_tok), jnp.int32),
                   jax.ShapeDtypeStruct((N_EXPERTS, 1), jnp.int32)],
        scratch_shapes=[pltpu.VMEM((POOL_HALO + tm, POOL_WIDTH), F32),
                        pltpu.VMEM((CONV_HALO + tm, CONV_WIDTH), F32),
                        pltpu.VMEM((N_EXPERTS, 1), F32),
                        pltpu.VMEM((tm, tm), BF16)],
        compiler_params=pltpu.CompilerParams(dimension_semantics=("arbitrary",),
                                             vmem_limit_bytes=VMEM_LIMIT),
        name="mixer_router",
    )(x2d, g_mix, w_in, b_gate, w_grp, pool_scale, w_po, conv_w, w_co, w_o, g_ffn,
      wr_hi, wr_lo, rbias, s_gate, s_up, s_down)


def _dest_kernel(idx_ref, rank_ref, pad_start_ref, dest_ref):
    tm = idx_ref.shape[1]
    eidx = lax.broadcasted_iota(jnp.int32, (N_EXPERTS, tm), 0)
    ps = pad_start_ref[...].astype(F32)
    for k in range(TOP_K):
        oh = eidx == idx_ref[k:k + 1, :]
        start = jnp.sum(jnp.where(oh, ps, 0.0), axis=0, keepdims=True)
        dest_ref[k:k + 1, :] = start.astype(jnp.int32) + rank_ref[k:k + 1, :]


def _dest(idx_t, rank_t, pad_start):
    t_tok = idx_t.shape[1]
    slot_blk = pl.BlockSpec((TOP_K, TM_DEST), lambda i: (0, i))
    return pl.pallas_call(
        _dest_kernel,
        grid=(t_tok // TM_DEST,),
        in_specs=[slot_blk, slot_blk, pl.BlockSpec((N_EXPERTS, 1), lambda i: (0, 0))],
        out_specs=slot_blk,
        out_shape=jax.ShapeDtypeStruct((TOP_K, t_tok), jnp.int32),
        compiler_params=pltpu.CompilerParams(dimension_semantics=("arbitrary",)),
        name="dest",
    )(idx_t, rank_t, pad_start)


def _sc_mesh():
    return plsc.VectorSubcoreMesh(core_axis_name="c", subcore_axis_name="s")


def _sc_worker_id():
    return lax.axis_index("s") * SC_CORES + lax.axis_index("c")


def _dispatch(h2p, dest_t, zero_rows, n_rows):
    t_tok, width = h2p.shape
    per_w = t_tok // SC_WORKERS
    n_chunks = per_w // SC_CHUNK
    z_chunks = zero_rows.shape[0] // (SC_WORKERS * SC_CHUNK)
    dest_w = (dest_t.reshape(TOP_K, SC_WORKERS, n_chunks, SC_CHUNK)
              .transpose(1, 2, 0, 3).reshape(SC_WORKERS * n_chunks * TOP_K, SC_CHUNK))
    zero_w = zero_rows.reshape(SC_WORKERS * z_chunks, SC_CHUNK)

    @functools.partial(
        pl.kernel, mesh=_sc_mesh(),
        out_type=jax.ShapeDtypeStruct((n_rows, width), h2p.dtype),
        scratch_types=[pltpu.VMEM((n_chunks * TOP_K, SC_CHUNK), jnp.int32),
                       pltpu.VMEM((z_chunks, SC_CHUNK), jnp.int32),
                       pltpu.VMEM((2, SC_CHUNK, width), h2p.dtype),
                       pltpu.VMEM((SC_CHUNK, width), h2p.dtype),
                       pltpu.SemaphoreType.DMA, pltpu.SemaphoreType.DMA,
                       pltpu.SemaphoreType.DMA],
        name="dispatch",
    )
    def k(h2p_hbm, dest_hbm, zidx_hbm, zsrc_hbm, xs_hbm, idx_v, zidx_v, rows_v, zero_v,
          gsem, wsem, zsem):
        wid = _sc_worker_id()
        base = wid * per_w
        pltpu.sync_copy(dest_hbm.at[pl.ds(wid * n_chunks * TOP_K, n_chunks * TOP_K)], idx_v)
        pltpu.sync_copy(zidx_hbm.at[pl.ds(wid * z_chunks, z_chunks)], zidx_v)
        pltpu.sync_copy(zsrc_hbm, zero_v)

        def zput(j):
            return pltpu.make_async_copy(zero_v, xs_hbm.at[zidx_v.at[j]], zsem)

        for j in range(z_chunks):
            zput(j).start()

        def get(j, slot):
            return pltpu.make_async_copy(h2p_hbm.at[pl.ds(base + j * SC_CHUNK, SC_CHUNK)],
                                         rows_v.at[slot], gsem)

        def put(j, slot, kk):
            return pltpu.make_async_copy(rows_v.at[slot], xs_hbm.at[idx_v.at[j * TOP_K + kk]], wsem)

        get(0, 0).start()

        @pl.loop(0, n_chunks, step=2)
        def _(j):
            for b in range(2):
                jj = j + b
                get(jj, b).wait()

                @pl.when(jj >= 1)
                def _():
                    for kk in range(TOP_K):
                        put(jj - 1, 1 - b, kk).wait()

                @pl.when(jj + 1 < n_chunks)
                def _():
                    get(jj + 1, 1 - b).start()
                for kk in range(TOP_K):
                    put(jj, b, kk).start()

        for kk in range(TOP_K):
            put(n_chunks - 1, (n_chunks - 1) % 2, kk).wait()
        for j in range(z_chunks):
            zput(j).wait()

    return k(h2p, dest_w, zero_w, jnp.zeros((SC_CHUNK, width), h2p.dtype))


def _regroup(ys, dest_t):
    width = ys.shape[1]
    n_slots, t_tok = dest_t.shape
    m_rows = n_slots * t_tok
    per_w = m_rows // SC_WORKERS
    n_chunks = per_w // SC_CHUNK
    idx2 = dest_t.reshape(SC_WORKERS * n_chunks, SC_CHUNK)

    @functools.partial(
        pl.kernel, mesh=_sc_mesh(),
        out_type=jax.ShapeDtypeStruct((m_rows, width), ys.dtype),
        scratch_types=[pltpu.VMEM((n_chunks, SC_CHUNK), jnp.int32),
                       pltpu.VMEM((2, SC_CHUNK, width), ys.dtype),
                       pltpu.SemaphoreType.DMA, pltpu.SemaphoreType.DMA],
        name="regroup",
    )
    def k(ys_hbm, idx_hbm, out_hbm, idx_v, rows_v, gsem, wsem):
        wid = _sc_worker_id()
        base = wid * per_w
        pltpu.sync_copy(idx_hbm.at[pl.ds(wid * n_chunks, n_chunks)], idx_v)

        def get(j, slot):
            return pltpu.make_async_copy(ys_hbm.at[idx_v.at[j]], rows_v.at[slot], gsem)

        def put(j, slot):
            return pltpu.make_async_copy(rows_v.at[slot],
                                         out_hbm.at[pl.ds(base + j * SC_CHUNK, SC_CHUNK)], wsem)

        get(0, 0).start()

        @pl.loop(0, n_chunks, step=2)
        def _(j):
            for b in range(2):
                jj = j + b
                get(jj, b).wait()

                @pl.when(jj >= 1)
                def _():
                    put(jj - 1, 1 - b).wait()

                @pl.when(jj + 1 < n_chunks)
                def _():
                    get(jj + 1, 1 - b).start()
                put(jj, b).start()

        put(n_chunks - 1, (n_chunks - 1) % 2).wait()

    return k(ys, idx2).reshape(n_slots, t_tok, width)


def _experts_kernel(n_blk, blk_e_ref, next_e_ref, n_used_ref,
                    xs_hbm, eg_hbm, eu_hbm, ed_hbm, ys_hbm,
                    xbuf, ybuf, hid, stg_g, stg_u, stg_d, wg, wu, wd, xsem, ysem, wsem):
    n = n_used_ref[0]

    def ring(b):
        return jnp.bitwise_and(b, ROW_RING - 1)

    def x_copy(b):
        return pltpu.make_async_copy(xs_hbm.at[pl.ds(pl.multiple_of(b * ROW_BLOCK, ROW_BLOCK),
                                                     ROW_BLOCK)], xbuf.at[ring(b)], xsem.at[ring(b)])

    def y_copy(b):
        return pltpu.make_async_copy(ybuf.at[ring(b)],
                                     ys_hbm.at[pl.ds(pl.multiple_of(b * ROW_BLOCK, ROW_BLOCK),
                                                     ROW_BLOCK)], ysem.at[ring(b)])

    def w_copies(e):
        return (pltpu.make_async_copy(eg_hbm.at[e], stg_g, wsem.at[0]),
                pltpu.make_async_copy(eu_hbm.at[e], stg_u, wsem.at[1]),
                pltpu.make_async_copy(ed_hbm.at[e], stg_d, wsem.at[2]))

    def switch_expert(e, wslot):
        for cp in w_copies(e):
            cp.wait()
        wg[wslot] = stg_g[...].astype(BF16)
        wu[wslot] = stg_u[...].astype(BF16)
        wd[wslot] = stg_d[...].astype(BF16)
        nxt = next_e_ref[e]

        @pl.when(nxt >= 0)
        def _():
            for cp in w_copies(nxt):
                cp.start()

    def gate_up(b, wslot):
        xa, xb = _unpack_pair(xbuf[ring(b)])
        xb16 = jnp.concatenate([xa, xb], axis=1).astype(BF16)
        g = _dot(xb16, wg[wslot])
        up = _dot(xb16, wu[wslot])
        hid[jnp.bitwise_and(b, 1)] = (g * jax.nn.sigmoid(g) * up).astype(BF16)

    def down(b, wslot):
        y = _dot(hid[jnp.bitwise_and(b, 1)], wd[wslot])
        ybuf[ring(b)] = _pack_pair(y[:, 0:HALF], y[:, HALF:D_MODEL])

    e0 = blk_e_ref[0]
    for cp in w_copies(e0):
        cp.start()
    for j in range(ROW_RING):
        @pl.when(j < n)
        def _():
            x_copy(j).start()
    switch_expert(e0, 0)
    x_copy(0).wait()
    gate_up(0, 0)

    def body(b, wslot_prev):
        e = blk_e_ref[b]
        first = e != blk_e_ref[b - 1]
        wslot = jnp.where(first, 1 - wslot_prev, wslot_prev)

        @pl.when(first)
        def _():
            switch_expert(e, wslot)

        x_copy(b).wait()

        @pl.when(b + ROW_RING - 1 < n)
        def _():
            x_copy(b + ROW_RING - 1).start()

        @pl.when(b >= ROW_RING + 1)
        def _():
            y_copy(b - 1 - ROW_RING).wait()

        gate_up(b, wslot)
        down(b - 1, wslot_prev)
        y_copy(b - 1).start()
        return wslot

    wslot_last = lax.fori_loop(1, n, body, jnp.int32(0))

    last = n - 1

    @pl.when(last >= ROW_RING)
    def _():
        y_copy(last - ROW_RING).wait()
    down(last, wslot_last)
    y_copy(last).start()
    for j in range(ROW_RING - 1, -1, -1):
        @pl.when(last - j >= 0)
        def _():
            y_copy(last - j).wait()

    ybuf[0] = jnp.zeros((ROW_BLOCK, HALF), U32)

    def zero_tail(b, c):
        cp = pltpu.make_async_copy(ybuf.at[0],
                                   ys_hbm.at[pl.ds(pl.multiple_of(b * ROW_BLOCK, ROW_BLOCK),
                                                   ROW_BLOCK)], ysem.at[0])
        cp.start()
        cp.wait()
        return c
    lax.fori_loop(n, n_blk, zero_tail, 0)


def _experts(blk_e, next_e, n_used, xs, e_gate, e_up, e_down):
    n_rows = xs.shape[0]
    n_blk = n_rows // ROW_BLOCK
    any_spec = pl.BlockSpec(memory_space=pl.ANY)
    grid_spec = pltpu.PrefetchScalarGridSpec(
        num_scalar_prefetch=3,
        grid=(1,),
        in_specs=[any_spec, any_spec, any_spec, any_spec],
        out_specs=any_spec,
        scratch_shapes=[pltpu.VMEM((ROW_RING, ROW_BLOCK, HALF), U32),
                        pltpu.VMEM((ROW_RING, ROW_BLOCK, HALF), U32),
                        pltpu.VMEM((2, ROW_BLOCK, EXPERT_HIDDEN), BF16),
                        pltpu.VMEM((D_MODEL, EXPERT_HIDDEN), F32),
                        pltpu.VMEM((D_MODEL, EXPERT_HIDDEN), F32),
                        pltpu.VMEM((EXPERT_HIDDEN, D_MODEL), F32),
                        pltpu.VMEM((2, D_MODEL, EXPERT_HIDDEN), BF16),
                        pltpu.VMEM((2, D_MODEL, EXPERT_HIDDEN), BF16),
                        pltpu.VMEM((2, EXPERT_HIDDEN, D_MODEL), BF16),
                        pltpu.SemaphoreType.DMA((ROW_RING,)),
                        pltpu.SemaphoreType.DMA((ROW_RING,)),
                        pltpu.SemaphoreType.DMA((3,))],
    )
    return pl.pallas_call(
        functools.partial(_experts_kernel, n_blk),
        grid_spec=grid_spec,
        out_shape=jax.ShapeDtypeStruct((n_rows, HALF), U32),
        compiler_params=pltpu.CompilerParams(dimension_semantics=("arbitrary",)),
        name="experts",
    )(blk_e, next_e, n_used, xs, e_gate, e_up, e_down)


def _combine_kernel(yg_ref, wsel_ref, xres_ref, g_final_ref, out_ref):
    xres = xres_ref[...]
    acc_a = xres[:, 0:HALF]
    acc_b = xres[:, HALF:D_MODEL]
    wsel = wsel_ref[...]
    for k in range(TOP_K):
        ya, yb = _unpack_pair(yg_ref[k])
        wk = wsel[:, k:k + 1]
        acc_a = acc_a + ya * wk
        acc_b = acc_b + yb * wk
    out_ref[...] = _rms(jnp.concatenate([acc_a, acc_b], axis=1), g_final_ref[...])


def _combine(yg, wsel_tk, xres, g_final):
    t_tok = xres.shape[0]
    tm = TM_COMB
    return pl.pallas_call(
        _combine_kernel,
        grid=(t_tok // tm,),
        in_specs=[pl.BlockSpec((TOP_K, tm, HALF), lambda i: (0, i, 0)),
                  pl.BlockSpec((tm, TOP_K), lambda i: (i, 0)),
                  pl.BlockSpec((tm, D_MODEL), lambda i: (i, 0)),
                  pl.BlockSpec((1, D_MODEL), lambda i: (0, 0))],
        out_specs=pl.BlockSpec((tm, D_MODEL), lambda i: (i, 0)),
        out_shape=jax.ShapeDtypeStruct((t_tok, D_MODEL), F32),
        compiler_params=pltpu.CompilerParams(dimension_semantics=("arbitrary",),
                                             vmem_limit_bytes=VMEM_LIMIT),
        name="combine",
    )(yg, wsel_tk, xres, g_final)


def kernel(x, g_mix, w_in, b_gate, w_pool_group, pool_scale, w_pool_out, conv_w, w_conv_out, w_o,
           g_ffn, w_router, router_bias, e_gate, e_up, e_down, s_gate, s_up, s_down, g_final):
    b, s, d = x.shape
    t_tok = b * s
    n_pad = N_EXPERTS * ROW_BLOCK
    assert d == D_MODEL and s % TM_MIX == 0 and TM_MIX >= POOL_HALO
    assert t_tok % TM_DEST == 0 and t_tok % TM_COMB == 0
    assert t_tok % (2 * SC_WORKERS * SC_CHUNK) == 0 and n_pad % (SC_WORKERS * SC_CHUNK) == 0

    row = lambda a: a.reshape(1, -1)
    wr_t = w_router.T.astype(F32)
    wr_hi = wr_t.astype(BF16)
    wr_lo = (wr_t - wr_hi.astype(F32)).astype(BF16)

    xres, h2p, idx_t, wsel_t, rank_t, counts = _mixer_router(
        x.reshape(t_tok, d), s, row(g_mix), w_in.astype(BF16), row(b_gate),
        w_pool_group.astype(BF16), row(pool_scale), w_pool_out.astype(BF16), conv_w,
        w_conv_out.astype(BF16), w_o.astype(BF16), row(g_ffn), wr_hi, wr_lo,
        router_bias.astype(F32).reshape(N_EXPERTS, 1), s_gate.astype(BF16), s_up.astype(BF16),
        s_down.astype(BF16))

    counts = counts.reshape(N_EXPERTS)
    padded = (counts + ROW_BLOCK - 1) // ROW_BLOCK * ROW_BLOCK
    pad_end = jnp.cumsum(padded)
    pad_start = pad_end - padded
    n_rows = t_tok * TOP_K + n_pad
    n_blk = n_rows // ROW_BLOCK
    n_used = (pad_end[-1:] // ROW_BLOCK).astype(jnp.int32)
    blk_row0 = jnp.arange(n_blk, dtype=jnp.int32) * ROW_BLOCK
    blk_e = jnp.minimum(jnp.sum(pad_end[None, :] <= blk_row0[:, None], axis=1),
                        N_EXPERTS - 1).astype(jnp.int32)
    e_ids = jnp.arange(N_EXPERTS, dtype=jnp.int32)
    later_used = jnp.logical_and(e_ids[None, :] > e_ids[:, None], padded[None, :] > 0)
    next_e = jnp.min(jnp.where(later_used, e_ids[None, :], N_EXPERTS), axis=1)
    next_e = jnp.where(next_e < N_EXPERTS, next_e, -1).astype(jnp.int32)
    gap = padded - counts
    gap_before = jnp.cumsum(gap) - gap
    j_pad = jnp.arange(n_pad, dtype=jnp.int32)
    zero_rows = (j_pad + jnp.sum(jnp.where(gap_before[None, :] <= j_pad[:, None],
                                           counts[None, :], 0), axis=1)).astype(jnp.int32)

    dest_t = _dest(idx_t, rank_t, pad_start.reshape(N_EXPERTS, 1).astype(jnp.int32))
    xs = _dispatch(h2p, dest_t, zero_rows, n_rows)
    ys = _experts(blk_e, next_e, n_used, xs, e_gate, e_up, e_down)
    yg = _regroup(ys, dest_t)
    out = _combine(yg, wsel_t.T, xres, row(g_final))
    return out.reshape(b, s, d)
```

```python
import functools

import jax
import jax.numpy as jnp
from jax import lax
from jax.experimental import pallas as pl
from jax.experimental.pallas import tpu as pltpu
from jax.experimental.pallas import tpu_sc as plsc

D_MODEL = 1024
HALF = D_MODEL // 2
POOL_WIDTH = 512
N_POOL_GROUPS = 4
POOL_GROUP = 128
POOL_WINDOWS = (2, 4, 8, 16)
CONV_WIDTH = 512
N_EXPERTS = 64
TOP_K = 8
EXPERT_HIDDEN = 256
SHARED_HIDDEN = 256
ROUTED_SCALE = 2.5
EPS = 1e-6

POOL_HALO = 16
CONV_HALO = 8
TM_MIX = 512
TM_DEST = 2048
ROW_BLOCK = 256
ROW_RING = 4
TM_COMB = 512
N_COMB_CHUNKS = 4
VMEM_LIMIT = 56 * 1024 * 1024

SC_CORES = 2
SC_SUBCORES = 16
SC_WORKERS = SC_CORES * SC_SUBCORES
SC_CHUNK = 64

BF16 = jnp.bfloat16
F32 = jnp.float32
U32 = jnp.uint32


def _rms(x, g):
    r = lax.rsqrt(jnp.mean(x * x, axis=-1, keepdims=True) + EPS)
    return (x * r) * g


def _dot(a, b):
    return jnp.dot(a, b, preferred_element_type=F32)


def _pack_pair(a, b):
    ra = lax.bitcast_convert_type(a.astype(BF16).astype(F32), U32)
    rb = lax.bitcast_convert_type(b.astype(BF16).astype(F32), U32)
    return ra | (rb >> 16)


def _unpack_pair(w):
    a = lax.bitcast_convert_type(w & jnp.uint32(0xFFFF0000), F32)
    b = lax.bitcast_convert_type(w << 16, F32)
    return a, b


def _mixer_router_kernel(n_seq_tiles,
                         x_ref, g_mix_ref, w_in_ref, b_gate_ref, w_grp_ref, pool_scale_ref,
                         w_po_ref, conv_w_ref, w_co_ref, w_o_ref, g_ffn_ref,
                         wr_hi_ref, wr_lo_ref, rbias_ref, s_gate_ref, s_up_ref, s_down_ref,
                         xres_ref, h2p_ref, idx_ref, wsel_ref, rank_ref, counts_ref,
                         ext_pool, ext_conv, cnt_carry, tri):
    tm = x_ref.shape[0]
    i = pl.program_id(0)
    st = i % n_seq_tiles

    @pl.when(i == 0)
    def _():
        r = lax.broadcasted_iota(jnp.int32, (tm, tm), 0)
        c = lax.broadcasted_iota(jnp.int32, (tm, tm), 1)
        tri[...] = (r < c).astype(BF16)
        cnt_carry[...] = jnp.zeros_like(cnt_carry)

    @pl.when(st == 0)
    def _():
        ext_pool[0:POOL_HALO, :] = jnp.zeros((POOL_HALO, POOL_WIDTH), F32)
        ext_conv[0:CONV_HALO, :] = jnp.zeros((CONV_HALO, CONV_WIDTH), F32)

    x = x_ref[...]
    hb = _rms(x, g_mix_ref[...]).astype(BF16)

    o0 = POOL_WIDTH
    o1 = o0 + CONV_WIDTH
    o2 = o1 + CONV_WIDTH
    o3 = o2 + CONV_WIDTH

    u = _dot(hb, w_in_ref[:, 0:o0])
    ext_pool[POOL_HALO:POOL_HALO + tm, :] = u
    t_glob = st * tm + lax.broadcasted_iota(jnp.int32, (tm, 1), 0)
    mixed = []
    for gi, w in enumerate(POOL_WINDOWS):
        cols = slice(gi * POOL_GROUP, (gi + 1) * POOL_GROUP)
        ug = u[:, cols]
        acc = ug
        for j in range(1, w):
            acc = acc + ext_pool[POOL_HALO - j:POOL_HALO - j + tm, cols]
        cnt = jnp.minimum(t_glob + 1, w).astype(F32)
        pooled = acc * (1.0 / cnt) - ug
        mixed.append(_dot(pooled.astype(BF16), w_grp_ref[gi]))
    mixed = jnp.concatenate(mixed, axis=1) * pool_scale_ref[...]
    branch_a = _dot(mixed.astype(BF16), w_po_ref[...])
    ext_pool[0:POOL_HALO, :] = ext_pool[tm:tm + POOL_HALO, :]

    gb = _dot(hb, w_in_ref[:, o0:o1])
    gc = _dot(hb, w_in_ref[:, o1:o2])
    v = _dot(hb, w_in_ref[:, o2:o3])
    cv = gc * v
    ext_conv[CONV_HALO:CONV_HALO + tm, :] = cv
    conv = (ext_conv[CONV_HALO - 2:CONV_HALO - 2 + tm, :] * conv_w_ref[0:1, :]
            + ext_conv[CONV_HALO - 1:CONV_HALO - 1 + tm, :] * conv_w_ref[1:2, :]
            + cv * conv_w_ref[2:3, :])
    branch_b = _dot((gb * conv).astype(BF16), w_co_ref[...])
    ext_conv[0:CONV_HALO, :] = ext_conv[tm:tm + CONV_HALO, :]

    gate_a = jax.nn.sigmoid(_dot(hb, w_in_ref[:, o3:o3 + D_MODEL]) + b_gate_ref[:, 0:D_MODEL])
    merged = gate_a * branch_a
    gate_b = jax.nn.sigmoid(_dot(hb, w_in_ref[:, o3 + D_MODEL:o3 + 2 * D_MODEL])
                            + b_gate_ref[:, D_MODEL:2 * D_MODEL])
    merged = merged + gate_b * branch_b
    x1 = x + _dot(merged.astype(BF16), w_o_ref[...])

    h2 = _rms(x1, g_ffn_ref[...])
    h2p_ref[...] = _pack_pair(h2[:, 0:HALF], h2[:, HALF:D_MODEL])
    h2b = h2.astype(BF16)
    sg = _dot(h2b, s_gate_ref[...])
    su = _dot(h2b, s_up_ref[...])
    shared = _dot((sg * jax.nn.sigmoid(sg) * su).astype(BF16), s_down_ref[...])
    xres_ref[...] = x1 + shared

    h2_lo = (h2 - h2b.astype(F32)).astype(BF16)
    nt = (((1,), (1,)), ((), ()))
    logits = (lax.dot_general(wr_hi_ref[...], h2b, nt, preferred_element_type=F32)
              + lax.dot_general(wr_hi_ref[...], h2_lo, nt, preferred_element_type=F32)
              + lax.dot_general(wr_lo_ref[...], h2b, nt, preferred_element_type=F32))
    scores = jax.nn.sigmoid(logits)
    sel = scores + rbias_ref[...]
    eidx = lax.broadcasted_iota(jnp.int32, (N_EXPERTS, tm), 0).astype(F32)
    e_rows, w_rows = [], []
    mask = jnp.zeros((N_EXPERTS, tm), F32)
    for _ in range(TOP_K):
        m = jnp.max(sel, axis=0, keepdims=True)
        ek = jnp.min(jnp.where(sel == m, eidx, float(N_EXPERTS)), axis=0, keepdims=True)
        oh = eidx == ek
        w_rows.append(jnp.sum(jnp.where(oh, scores, 0.0), axis=0, keepdims=True))
        e_rows.append(ek)
        mask = mask + oh.astype(F32)
        sel = jnp.where(oh, -jnp.inf, sel)
    wsum = w_rows[0]
    for k in range(1, TOP_K):
        wsum = wsum + w_rows[k]

    before = _dot(mask.astype(BF16), tri[...]) + cnt_carry[...]
    for k in range(TOP_K):
        oh = eidx == e_rows[k]
        rank_ref[k:k + 1, :] = jnp.sum(jnp.where(oh, before, 0.0), axis=0,
                                       keepdims=True).astype(jnp.int32)
        idx_ref[k:k + 1, :] = e_rows[k].astype(jnp.int32)
        wsel_ref[k:k + 1, :] = w_rows[k] / wsum * ROUTED_SCALE
    total = cnt_carry[...] + jnp.sum(mask, axis=1, keepdims=True)
    cnt_carry[...] = total
    counts_ref[...] = total.astype(jnp.int32)


def _mixer_router(x2d, seq_len, g_mix, w_in, b_gate, w_grp, pool_scale, w_po, conv_w, w_co, w_o,
                  g_ffn, wr_hi, wr_lo, rbias, s_gate, s_up, s_down):
    t_tok = x2d.shape[0]
    tm = TM_MIX
    n_seq_tiles = seq_len // tm
    const = lambda shape: pl.BlockSpec(shape, lambda i: (0,) * len(shape),
                                       pipeline_mode=pl.Buffered(1))
    row_blk = pl.BlockSpec((tm, D_MODEL), lambda i: (i, 0))
    half_blk = pl.BlockSpec((tm, HALF), lambda i: (i, 0))
    slot_blk = pl.BlockSpec((TOP_K, tm), lambda i: (0, i))
    return pl.pallas_call(
        functools.partial(_mixer_router_kernel, n_seq_tiles),
        grid=(t_tok // tm,),
        in_specs=[row_blk, const(g_mix.shape), const(w_in.shape), const(b_gate.shape),
                  const(w_grp.shape), const(pool_scale.shape), const(w_po.shape),
                  const(conv_w.shape), const(w_co.shape), const(w_o.shape), const(g_ffn.shape),
                  const(wr_hi.shape), const(wr_lo.shape), const(rbias.shape),
                  const(s_gate.shape), const(s_up.shape), const(s_down.shape)],
        out_specs=[row_blk, half_blk, slot_blk, slot_blk, slot_blk,
                   pl.BlockSpec((N_EXPERTS, 1), lambda i: (0, 0))],
        out_shape=[jax.ShapeDtypeStruct((t_tok, D_MODEL), F32),
                   jax.ShapeDtypeStruct((t_tok, HALF), U32),
                   jax.ShapeDtypeStruct((TOP_K, t_tok), jnp.int32),
                   jax.ShapeDtypeStruct((TOP_K, t_tok), F32),
                   jax.ShapeDtypeStruct((TOP_K, t_tok), jnp.int32),
                   jax.ShapeDtypeStruct((N_EXPERTS, 1), jnp.int32)],
        scratch_shapes=[pltpu.VMEM((POOL_HALO + tm, POOL_WIDTH), F32),
                        pltpu.VMEM((CONV_HALO + tm, CONV_WIDTH), F32),
                        pltpu.VMEM((N_EXPERTS, 1), F32),
                        pltpu.VMEM((tm, tm), BF16)],
        compiler_params=pltpu.CompilerParams(dimension_semantics=("arbitrary",),
                                             vmem_limit_bytes=VMEM_LIMIT),
        name="mixer_router",
    )(x2d, g_mix, w_in, b_gate, w_grp, pool_scale, w_po, conv_w, w_co, w_o, g_ffn,
      wr_hi, wr_lo, rbias, s_gate, s_up, s_down)


def _dest_kernel(idx_ref, rank_ref, pad_start_ref, dest_ref):
    tm = idx_ref.shape[1]
    eidx = lax.broadcasted_iota(jnp.int32, (N_EXPERTS, tm), 0)
    ps = pad_start_ref[...].astype(F32)
    for k in range(TOP_K):
        oh = eidx == idx_ref[k:k + 1, :]
        start = jnp.sum(jnp.where(oh, ps, 0.0), axis=0, keepdims=True)
        dest_ref[k:k + 1, :] = start.astype(jnp.int32) + rank_ref[k:k + 1, :]


def _dest(idx_t, rank_t, pad_start):
    t_tok = idx_t.shape[1]
    slot_blk = pl.BlockSpec((TOP_K, TM_DEST), lambda i: (0, i))
    return pl.pallas_call(
        _dest_kernel,
        grid=(t_tok // TM_DEST,),
        in_specs=[slot_blk, slot_blk, pl.BlockSpec((N_EXPERTS, 1), lambda i: (0, 0))],
        out_specs=slot_blk,
        out_shape=jax.ShapeDtypeStruct((TOP_K, t_tok), jnp.int32),
        compiler_params=pltpu.CompilerParams(dimension_semantics=("arbitrary",)),
        name="dest",
    )(idx_t, rank_t, pad_start)


def _sc_mesh():
    return plsc.VectorSubcoreMesh(core_axis_name="c", subcore_axis_name="s")


def _sc_worker_id():
    return lax.axis_index("s") * SC_CORES + lax.axis_index("c")


def _dispatch(h2p, dest_t, zero_rows, n_rows):
    t_tok, width = h2p.shape
    per_w = t_tok // SC_WORKERS
    n_chunks = per_w // SC_CHUNK
    z_chunks = zero_rows.shape[0] // (SC_WORKERS * SC_CHUNK)
    dest_w = (dest_t.reshape(TOP_K, SC_WORKERS, n_chunks, SC_CHUNK)
              .transpose(1, 2, 0, 3).reshape(SC_WORKERS * n_chunks * TOP_K, SC_CHUNK))
    zero_w = zero_rows.reshape(SC_WORKERS * z_chunks, SC_CHUNK)

    @functools.partial(
        pl.kernel, mesh=_sc_mesh(),
        out_type=jax.ShapeDtypeStruct((n_rows, width), h2p.dtype),
        scratch_types=[pltpu.VMEM((n_chunks * TOP_K, SC_CHUNK), jnp.int32),
                       pltpu.VMEM((z_chunks, SC_CHUNK), jnp.int32),
                       pltpu.VMEM((2, SC_CHUNK, width), h2p.dtype),
                       pltpu.VMEM((SC_CHUNK, width), h2p.dtype),
                       pltpu.SemaphoreType.DMA, pltpu.SemaphoreType.DMA,
                       pltpu.SemaphoreType.DMA],
        name="dispatch",
    )
    def k(h2p_hbm, dest_hbm, zidx_hbm, zsrc_hbm, xs_hbm, idx_v, zidx_v, rows_v, zero_v,
          gsem, wsem, zsem):
        wid = _sc_worker_id()
        base = wid * per_w
        pltpu.sync_copy(dest_hbm.at[pl.ds(wid * n_chunks * TOP_K, n_chunks * TOP_K)], idx_v)
        pltpu.sync_copy(zidx_hbm.at[pl.ds(wid * z_chunks, z_chunks)], zidx_v)
        pltpu.sync_copy(zsrc_hbm, zero_v)

        def zput(j):
            return pltpu.make_async_copy(zero_v, xs_hbm.at[zidx_v.at[j]], zsem)

        for j in range(z_chunks):
            zput(j).start()

        def get(j, slot):
            return pltpu.make_async_copy(h2p_hbm.at[pl.ds(base + j * SC_CHUNK, SC_CHUNK)],
                                         rows_v.at[slot], gsem)

        def put(j, slot, kk):
            return pltpu.make_async_copy(rows_v.at[slot], xs_hbm.at[idx_v.at[j * TOP_K + kk]], wsem)

        get(0, 0).start()

        @pl.loop(0, n_chunks, step=2)
        def _(j):
            for b in range(2):
                jj = j + b
                get(jj, b).wait()

                @pl.when(jj >= 1)
                def _():
                    for kk in range(TOP_K):
                        put(jj - 1, 1 - b, kk).wait()

                @pl.when(jj + 1 < n_chunks)
                def _():
                    get(jj + 1, 1 - b).start()
                for kk in range(TOP_K):
                    put(jj, b, kk).start()

        for kk in range(TOP_K):
            put(n_chunks - 1, (n_chunks - 1) % 2, kk).wait()
        for j in range(z_chunks):
            zput(j).wait()

    return k(h2p, dest_w, zero_w, jnp.zeros((SC_CHUNK, width), h2p.dtype))


def _regroup(ys, dest_t):
    width = ys.shape[1]
    n_slots, t_tok = dest_t.shape
    m_rows = n_slots * t_tok
    per_w = m_rows // SC_WORKERS
    n_chunks = per_w // SC_CHUNK
    idx2 = dest_t.reshape(SC_WORKERS * n_chunks, SC_CHUNK)

    @functools.partial(
        pl.kernel, mesh=_sc_mesh(),
        out_type=jax.ShapeDtypeStruct((m_rows, width), ys.dtype),
        scratch_types=[pltpu.VMEM((n_chunks, SC_CHUNK), jnp.int32),
                       pltpu.VMEM((2, SC_CHUNK, width), ys.dtype),
                       pltpu.SemaphoreType.DMA, pltpu.SemaphoreType.DMA],
        name="regroup",
    )
    def k(ys_hbm, idx_hbm, out_hbm, idx_v, rows_v, gsem, wsem):
        wid = _sc_worker_id()
        base = wid * per_w
        pltpu.sync_copy(idx_hbm.at[pl.ds(wid * n_chunks, n_chunks)], idx_v)

        def get(j, slot):
            return pltpu.make_async_copy(ys_hbm.at[idx_v.at[j]], rows_v.at[slot], gsem)

        def put(j, slot):
            return pltpu.make_async_copy(rows_v.at[slot],
                                         out_hbm.at[pl.ds(base + j * SC_CHUNK, SC_CHUNK)], wsem)

        get(0, 0).start()

        @pl.loop(0, n_chunks, step=2)
        def _(j):
            for b in range(2):
                jj = j + b
                get(jj, b).wait()

                @pl.when(jj >= 1)
                def _():
                    put(jj - 1, 1 - b).wait()

                @pl.when(jj + 1 < n_chunks)
                def _():
                    get(jj + 1, 1 - b).start()
                put(jj, b).start()

        put(n_chunks - 1, (n_chunks - 1) % 2).wait()

    return k(ys, idx2).reshape(n_slots, t_tok, width)


def _experts_kernel(n_blk, blk_e_ref, next_e_ref, n_used_ref,
                    xs_hbm, eg_hbm, eu_hbm, ed_hbm, ys_hbm,
                    xbuf, ybuf, hid, stg_g, stg_u, stg_d, wg, wu, wd, xsem, ysem, wsem):
    n = n_used_ref[0]

    def ring(b):
        return jnp.bitwise_and(b, ROW_RING - 1)

    def x_copy(b):
        return pltpu.make_async_copy(xs_hbm.at[pl.ds(pl.multiple_of(b * ROW_BLOCK, ROW_BLOCK),
                                                     ROW_BLOCK)], xbuf.at[ring(b)], xsem.at[ring(b)])

    def y_copy(b):
        return pltpu.make_async_copy(ybuf.at[ring(b)],
                                     ys_hbm.at[pl.ds(pl.multiple_of(b * ROW_BLOCK, ROW_BLOCK),
                                                     ROW_BLOCK)], ysem.at[ring(b)])

    def w_copies(e):
        return (pltpu.make_async_copy(eg_hbm.at[e], stg_g, wsem.at[0]),
                pltpu.make_async_copy(eu_hbm.at[e], stg_u, wsem.at[1]),
                pltpu.make_async_copy(ed_hbm.at[e], stg_d, wsem.at[2]))

    def switch_expert(e, wslot):
        for cp in w_copies(e):
            cp.wait()
        wg[wslot] = stg_g[...].astype(BF16)
        wu[wslot] = stg_u[...].astype(BF16)
        wd[wslot] = stg_d[...].astype(BF16)
        nxt = next_e_ref[e]

        @pl.when(nxt >= 0)
        def _():
            for cp in w_copies(nxt):
                cp.start()

    def gate_up(b, wslot):
        xa, xb = _unpack_pair(xbuf[ring(b)])
        xb16 = jnp.concatenate([xa, xb], axis=1).astype(BF16)
        g = _dot(xb16, wg[wslot])
        up = _dot(xb16, wu[wslot])
        hid[jnp.bitwise_and(b, 1)] = (g * jax.nn.sigmoid(g) * up).astype(BF16)

    def down(b, wslot):
        y = _dot(hid[jnp.bitwise_and(b, 1)], wd[wslot])
        ybuf[ring(b)] = _pack_pair(y[:, 0:HALF], y[:, HALF:D_MODEL])

    e0 = blk_e_ref[0]
    for cp in w_copies(e0):
        cp.start()
    for j in range(ROW_RING):
        @pl.when(j < n)
        def _():
            x_copy(j).start()
    switch_expert(e0, 0)
    x_copy(0).wait()
    gate_up(0, 0)

    def body(b, wslot_prev):
        e = blk_e_ref[b]
        first = e != blk_e_ref[b - 1]
        wslot = jnp.where(first, 1 - wslot_prev, wslot_prev)

        @pl.when(first)
        def _():
            switch_expert(e, wslot)

        x_copy(b).wait()

        @pl.when(b + ROW_RING - 1 < n)
        def _():
            x_copy(b + ROW_RING - 1).start()

        @pl.when(b >= ROW_RING + 1)
        def _():
            y_copy(b - 1 - ROW_RING).wait()

        gate_up(b, wslot)
        down(b - 1, wslot_prev)
        y_copy(b - 1).start()
        return wslot

    wslot_last = lax.fori_loop(1, n, body, jnp.int32(0))

    last = n - 1

    @pl.when(last >= ROW_RING)
    def _():
        y_copy(last - ROW_RING).wait()
    down(last, wslot_last)
    y_copy(last).start()
    for j in range(ROW_RING - 1, -1, -1):
        @pl.when(last - j >= 0)
        def _():
            y_copy(last - j).wait()

    ybuf[0] = jnp.zeros((ROW_BLOCK, HALF), U32)

    def zero_tail(b, c):
        cp = pltpu.make_async_copy(ybuf.at[0],
                                   ys_hbm.at[pl.ds(pl.multiple_of(b * ROW_BLOCK, ROW_BLOCK),
                                                   ROW_BLOCK)], ysem.at[0])
        cp.start()
        cp.wait()
        return c
    lax.fori_loop(n, n_blk, zero_tail, 0)


def _experts(blk_e, next_e, n_used, xs, e_gate, e_up, e_down):
    n_rows = xs.shape[0]
    n_blk = n_rows // ROW_BLOCK
    any_spec = pl.BlockSpec(memory_space=pl.ANY)
    grid_spec = pltpu.PrefetchScalarGridSpec(
        num_scalar_prefetch=3,
        grid=(1,),
        in_specs=[any_spec, any_spec, any_spec, any_spec],
        out_specs=any_spec,
        scratch_shapes=[pltpu.VMEM((ROW_RING, ROW_BLOCK, HALF), U32),
                        pltpu.VMEM((ROW_RING, ROW_BLOCK, HALF), U32),
                        pltpu.VMEM((2, ROW_BLOCK, EXPERT_HIDDEN), BF16),
                        pltpu.VMEM((D_MODEL, EXPERT_HIDDEN), F32),
                        pltpu.VMEM((D_MODEL, EXPERT_HIDDEN), F32),
                        pltpu.VMEM((EXPERT_HIDDEN, D_MODEL), F32),
                        pltpu.VMEM((2, D_MODEL, EXPERT_HIDDEN), BF16),
                        pltpu.VMEM((2, D_MODEL, EXPERT_HIDDEN), BF16),
                        pltpu.VMEM((2, EXPERT_HIDDEN, D_MODEL), BF16),
                        pltpu.SemaphoreType.DMA((ROW_RING,)),
                        pltpu.SemaphoreType.DMA((ROW_RING,)),
                        pltpu.SemaphoreType.DMA((3,))],
    )
    return pl.pallas_call(
        functools.partial(_experts_kernel, n_blk),
        grid_spec=grid_spec,
        out_shape=jax.ShapeDtypeStruct((n_rows, HALF), U32),
        compiler_params=pltpu.CompilerParams(dimension_semantics=("arbitrary",)),
        name="experts",
    )(blk_e, next_e, n_used, xs, e_gate, e_up, e_down)


def _combine_kernel(yg_ref, wsel_ref, xres_ref, g_final_ref, *rest):
    out_ref = rest[-1]
    xres = xres_ref[...]
    acc_a = xres[:, 0:HALF]
    acc_b = xres[:, HALF:D_MODEL]
    wsel = wsel_ref[...]
    for k in range(TOP_K):
        ya, yb = _unpack_pair(yg_ref[k])
        wk = wsel[:, k:k + 1]
        acc_a = acc_a + ya * wk
        acc_b = acc_b + yb * wk
    out_ref[...] = _rms(jnp.concatenate([acc_a, acc_b], axis=1), g_final_ref[...])


def _combine(yg, wsel_tk, xres, g_final, out_prev, chunk):
    t_tok = xres.shape[0]
    tm = TM_COMB
    steps = yg.shape[1] // tm
    off = chunk * steps
    in_specs = [pl.BlockSpec((TOP_K, tm, HALF), lambda i: (0, i, 0)),
                pl.BlockSpec((tm, TOP_K), lambda i: (i + off, 0)),
                pl.BlockSpec((tm, D_MODEL), lambda i: (i + off, 0)),
                pl.BlockSpec((1, D_MODEL), lambda i: (0, 0))]
    args = [yg, wsel_tk, xres, g_final]
    aliases = {}
    if out_prev is not None:
        in_specs.append(pl.BlockSpec(memory_space=pl.ANY))
        args.append(out_prev)
        aliases = {4: 0}
    return pl.pallas_call(
        _combine_kernel,
        grid=(steps,),
        in_specs=in_specs,
        out_specs=pl.BlockSpec((tm, D_MODEL), lambda i: (i + off, 0)),
        out_shape=jax.ShapeDtypeStruct((t_tok, D_MODEL), F32),
        input_output_aliases=aliases,
        compiler_params=pltpu.CompilerParams(dimension_semantics=("arbitrary",),
                                             vmem_limit_bytes=VMEM_LIMIT),
        name="combine",
    )(*args)


def kernel(x, g_mix, w_in, b_gate, w_pool_group, pool_scale, w_pool_out, conv_w, w_conv_out, w_o,
           g_ffn, w_router, router_bias, e_gate, e_up, e_down, s_gate, s_up, s_down, g_final):
    b, s, d = x.shape
    t_tok = b * s
    n_pad = N_EXPERTS * ROW_BLOCK
    assert d == D_MODEL and s % TM_MIX == 0 and TM_MIX >= POOL_HALO
    assert t_tok % TM_DEST == 0 and t_tok % (N_COMB_CHUNKS * TM_COMB) == 0
    assert (t_tok // N_COMB_CHUNKS * TOP_K) % (2 * SC_WORKERS * SC_CHUNK) == 0
    assert t_tok % (2 * SC_WORKERS * SC_CHUNK) == 0 and n_pad % (SC_WORKERS * SC_CHUNK) == 0

    row = lambda a: a.reshape(1, -1)
    wr_t = w_router.T.astype(F32)
    wr_hi = wr_t.astype(BF16)
    wr_lo = (wr_t - wr_hi.astype(F32)).astype(BF16)

    xres, h2p, idx_t, wsel_t, rank_t, counts = _mixer_router(
        x.reshape(t_tok, d), s, row(g_mix), w_in.astype(BF16), row(b_gate),
        w_pool_group.astype(BF16), row(pool_scale), w_pool_out.astype(BF16), conv_w,
        w_conv_out.astype(BF16), w_o.astype(BF16), row(g_ffn), wr_hi, wr_lo,
        router_bias.astype(F32).reshape(N_EXPERTS, 1), s_gate.astype(BF16), s_up.astype(BF16),
        s_down.astype(BF16))

    counts = counts.reshape(N_EXPERTS)
    padded = (counts + ROW_BLOCK - 1) // ROW_BLOCK * ROW_BLOCK
    pad_end = jnp.cumsum(padded)
    pad_start = pad_end - padded
    n_rows = t_tok * TOP_K + n_pad
    n_blk = n_rows // ROW_BLOCK
    n_used = (pad_end[-1:] // ROW_BLOCK).astype(jnp.int32)
    blk_row0 = jnp.arange(n_blk, dtype=jnp.int32) * ROW_BLOCK
    blk_e = jnp.minimum(jnp.sum(pad_end[None, :] <= blk_row0[:, None], axis=1),
                        N_EXPERTS - 1).astype(jnp.int32)
    e_ids = jnp.arange(N_EXPERTS, dtype=jnp.int32)
    later_used = jnp.logical_and(e_ids[None, :] > e_ids[:, None], padded[None, :] > 0)
    next_e = jnp.min(jnp.where(later_used, e_ids[None, :], N_EXPERTS), axis=1)
    next_e = jnp.where(next_e < N_EXPERTS, next_e, -1).astype(jnp.int32)
    gap = padded - counts
    gap_before = jnp.cumsum(gap) - gap
    j_pad = jnp.arange(n_pad, dtype=jnp.int32)
    zero_rows = (j_pad + jnp.sum(jnp.where(gap_before[None, :] <= j_pad[:, None],
                                           counts[None, :], 0), axis=1)).astype(jnp.int32)

    dest_t = _dest(idx_t, rank_t, pad_start.reshape(N_EXPERTS, 1).astype(jnp.int32))
    xs = _dispatch(h2p, dest_t, zero_rows, n_rows)
    ys = _experts(blk_e, next_e, n_used, xs, e_gate, e_up, e_down)
    wsel_tk = wsel_t.T
    t_chunk = t_tok // N_COMB_CHUNKS
    out = None
    for c in range(N_COMB_CHUNKS):
        yg = _regroup(ys, dest_t[:, c * t_chunk:(c + 1) * t_chunk])
        out = _combine(yg, wsel_tk, xres, row(g_final), out, c)
    return out.reshape(b, s, d)
```

```python
import functools

import jax
import jax.numpy as jnp
from jax import lax
from jax.experimental import pallas as pl
from jax.experimental.pallas import tpu as pltpu
from jax.experimental.pallas import tpu_sc as plsc

D_MODEL = 1024
HALF = D_MODEL // 2
POOL_WIDTH = 512
N_POOL_GROUPS = 4
POOL_GROUP = 128
POOL_WINDOWS = (2, 4, 8, 16)
CONV_WIDTH = 512
N_EXPERTS = 64
TOP_K = 8
EXPERT_HIDDEN = 256
SHARED_HIDDEN = 256
ROUTED_SCALE = 2.5
EPS = 1e-6

POOL_HALO = 16
CONV_HALO = 8
TM_MIX = 512
TM_DEST = 2048
ROW_BLOCK = 256
ROW_RING = 4
TM_COMB = 512
N_COMB_CHUNKS = 4
VMEM_LIMIT = 56 * 1024 * 1024

SC_CORES = 2
SC_SUBCORES = 16
SC_WORKERS = SC_CORES * SC_SUBCORES
SC_CHUNK = 64

BF16 = jnp.bfloat16
F32 = jnp.float32
U32 = jnp.uint32


def _rms(x, g):
    r = lax.rsqrt(jnp.mean(x * x, axis=-1, keepdims=True) + EPS)
    return (x * r) * g


def _dot(a, b):
    return jnp.dot(a, b, preferred_element_type=F32)


def _sigmoid(z):
    return 0.5 * jnp.tanh(0.5 * z) + 0.5


def _pack_pair(a, b):
    ra = lax.bitcast_convert_type(a.astype(BF16).astype(F32), U32)
    rb = lax.bitcast_convert_type(b.astype(BF16).astype(F32), U32)
    return ra | (rb >> 16)


def _unpack_pair(w):
    a = lax.bitcast_convert_type(w & jnp.uint32(0xFFFF0000), F32)
    b = lax.bitcast_convert_type(w << 16, F32)
    return a, b


def _mixer_router_kernel(n_seq_tiles,
                         x_ref, g_mix_ref, w_in_ref, b_gate_ref, w_grp_ref, pool_scale_ref,
                         w_po_ref, conv_w_ref, w_co_ref, w_o_ref, g_ffn_ref,
                         wr_ref, rbias_ref, s_gate_ref, s_up_ref, s_down_ref,
                         xres_ref, h2p_ref, idx_ref, wsel_ref, rank_ref, counts_ref,
                         ext_pool, ext_conv, cnt_carry, tri):
    tm = x_ref.shape[0]
    i = pl.program_id(0)
    st = i % n_seq_tiles

    @pl.when(i == 0)
    def _():
        r = lax.broadcasted_iota(jnp.int32, (tm, tm), 0)
        c = lax.broadcasted_iota(jnp.int32, (tm, tm), 1)
        tri[...] = (r < c).astype(BF16)
        cnt_carry[...] = jnp.zeros_like(cnt_carry)

    @pl.when(st == 0)
    def _():
        ext_pool[0:POOL_HALO, :] = jnp.zeros((POOL_HALO, POOL_WIDTH), F32)
        ext_conv[0:CONV_HALO, :] = jnp.zeros((CONV_HALO, CONV_WIDTH), F32)

    x = x_ref[...]
    hb = _rms(x, g_mix_ref[...]).astype(BF16)

    o0 = POOL_WIDTH
    o1 = o0 + CONV_WIDTH
    o2 = o1 + CONV_WIDTH
    o3 = o2 + CONV_WIDTH

    u = _dot(hb, w_in_ref[:, 0:o0])
    ext_pool[POOL_HALO:POOL_HALO + tm, :] = u
    gc = _dot(hb, w_in_ref[:, o1:o2])
    v = _dot(hb, w_in_ref[:, o2:o3])
    pre_a = _dot(hb, w_in_ref[:, o3:o3 + D_MODEL])

    t_glob = st * tm + lax.broadcasted_iota(jnp.int32, (tm, 1), 0)
    mixed = []
    for gi, w in enumerate(POOL_WINDOWS):
        cols = slice(gi * POOL_GROUP, (gi + 1) * POOL_GROUP)
        ug = u[:, cols]
        acc = ug
        for j in range(1, w):
            acc = acc + ext_pool[POOL_HALO - j:POOL_HALO - j + tm, cols]
        cnt = jnp.minimum(t_glob + 1, w).astype(F32)
        pooled = acc * (1.0 / cnt) - ug
        mixed.append(_dot(pooled.astype(BF16), w_grp_ref[gi]))
    ext_pool[0:POOL_HALO, :] = ext_pool[tm:tm + POOL_HALO, :]
    pre_b = _dot(hb, w_in_ref[:, o3 + D_MODEL:o3 + 2 * D_MODEL])
    gb = _dot(hb, w_in_ref[:, o0:o1])

    cv = gc * v
    ext_conv[CONV_HALO:CONV_HALO + tm, :] = cv
    conv = (ext_conv[CONV_HALO - 2:CONV_HALO - 2 + tm, :] * conv_w_ref[0:1, :]
            + ext_conv[CONV_HALO - 1:CONV_HALO - 1 + tm, :] * conv_w_ref[1:2, :]
            + cv * conv_w_ref[2:3, :])
    ext_conv[0:CONV_HALO, :] = ext_conv[tm:tm + CONV_HALO, :]
    branch_b = _dot((gb * conv).astype(BF16), w_co_ref[...])
    mixed = jnp.concatenate(mixed, axis=1) * pool_scale_ref[...]
    branch_a = _dot(mixed.astype(BF16), w_po_ref[...])

    merged = (_sigmoid(pre_a + b_gate_ref[:, 0:D_MODEL]) * branch_a
              + _sigmoid(pre_b + b_gate_ref[:, D_MODEL:2 * D_MODEL]) * branch_b)
    x1 = x + _dot(merged.astype(BF16), w_o_ref[...])

    h2 = _rms(x1, g_ffn_ref[...])
    h2p_ref[...] = _pack_pair(h2[:, 0:HALF], h2[:, HALF:D_MODEL])
    h2b = h2.astype(BF16)

    nt = (((1,), (1,)), ((), ()))
    parts = lax.dot_general(wr_ref[...], h2b, nt, preferred_element_type=F32)
    logits = parts[0:N_EXPERTS, :] + parts[N_EXPERTS:2 * N_EXPERTS, :]
    sg = _dot(h2b, s_gate_ref[...])
    su = _dot(h2b, s_up_ref[...])
    scores = jax.nn.sigmoid(logits)
    sel = scores + rbias_ref[...]
    eidx = lax.broadcasted_iota(jnp.int32, (N_EXPERTS, tm), 0).astype(F32)
    e_rows, w_rows = [], []
    mask = jnp.zeros((N_EXPERTS, tm), F32)
    for _ in range(TOP_K):
        m = jnp.max(sel, axis=0, keepdims=True)
        ek = jnp.min(jnp.where(sel == m, eidx, float(N_EXPERTS)), axis=0, keepdims=True)
        oh = eidx == ek
        w_rows.append(jnp.sum(jnp.where(oh, scores, 0.0), axis=0, keepdims=True))
        e_rows.append(ek)
        mask = mask + oh.astype(F32)
        sel = jnp.where(oh, -jnp.inf, sel)

    shared = _dot((sg * _sigmoid(sg) * su).astype(BF16), s_down_ref[...])
    xres_ref[...] = x1 + shared

    wsum = w_rows[0]
    for k in range(1, TOP_K):
        wsum = wsum + w_rows[k]

    before = _dot(mask.astype(BF16), tri[...]) + cnt_carry[...]
    for k in range(TOP_K):
        oh = eidx == e_rows[k]
        rank_ref[k:k + 1, :] = jnp.sum(jnp.where(oh, before, 0.0), axis=0,
                                       keepdims=True).astype(jnp.int32)
        idx_ref[k:k + 1, :] = e_rows[k].astype(jnp.int32)
        wsel_ref[k:k + 1, :] = w_rows[k] / wsum * ROUTED_SCALE
    total = cnt_carry[...] + jnp.sum(mask, axis=1, keepdims=True)
    cnt_carry[...] = total
    counts_ref[...] = total.astype(jnp.int32)


def _mixer_router(x2d, seq_len, g_mix, w_in, b_gate, w_grp, pool_scale, w_po, conv_w, w_co, w_o,
                  g_ffn, wr, rbias, s_gate, s_up, s_down):
    t_tok = x2d.shape[0]
    tm = TM_MIX
    n_seq_tiles = seq_len // tm
    const = lambda shape: pl.BlockSpec(shape, lambda i: (0,) * len(shape),
                                       pipeline_mode=pl.Buffered(1))
    row_blk = pl.BlockSpec((tm, D_MODEL), lambda i: (i, 0))
    half_blk = pl.BlockSpec((tm, HALF), lambda i: (i, 0))
    slot_blk = pl.BlockSpec((TOP_K, tm), lambda i: (0, i))
    return pl.pallas_call(
        functools.partial(_mixer_router_kernel, n_seq_tiles),
        grid=(t_tok // tm,),
        in_specs=[row_blk, const(g_mix.shape), const(w_in.shape), const(b_gate.shape),
                  const(w_grp.shape), const(pool_scale.shape), const(w_po.shape),
                  const(conv_w.shape), const(w_co.shape), const(w_o.shape), const(g_ffn.shape),
                  const(wr.shape), const(rbias.shape),
                  const(s_gate.shape), const(s_up.shape), const(s_down.shape)],
        out_specs=[row_blk, half_blk, slot_blk, slot_blk, slot_blk,
                   pl.BlockSpec((N_EXPERTS, 1), lambda i: (0, 0))],
        out_shape=[jax.ShapeDtypeStruct((t_tok, D_MODEL), F32),
                   jax.ShapeDtypeStruct((t_tok, HALF), U32),
                   jax.ShapeDtypeStruct((TOP_K, t_tok), jnp.int32),
                   jax.ShapeDtypeStruct((TOP_K, t_tok), F32),
                   jax.ShapeDtypeStruct((TOP_K, t_tok), jnp.int32),
                   jax.ShapeDtypeStruct((N_EXPERTS, 1), jnp.int32)],
        scratch_shapes=[pltpu.VMEM((POOL_HALO + tm, POOL_WIDTH), F32),
                        pltpu.VMEM((CONV_HALO + tm, CONV_WIDTH), F32),
                        pltpu.VMEM((N_EXPERTS, 1), F32),
                        pltpu.VMEM((tm, tm), BF16)],
        compiler_params=pltpu.CompilerParams(dimension_semantics=("arbitrary",),
                                             vmem_limit_bytes=VMEM_LIMIT),
        name="mixer_router",
    )(x2d, g_mix, w_in, b_gate, w_grp, pool_scale, w_po, conv_w, w_co, w_o, g_ffn,
      wr, rbias, s_gate, s_up, s_down)


def _dest_kernel(idx_ref, rank_ref, pad_start_ref, dest_ref):
    tm = idx_ref.shape[1]
    eidx = lax.broadcasted_iota(jnp.int32, (N_EXPERTS, tm), 0)
    ps = pad_start_ref[...].astype(F32)
    for k in range(TOP_K):
        oh = eidx == idx_ref[k:k + 1, :]
        start = jnp.sum(jnp.where(oh, ps, 0.0), axis=0, keepdims=True)
        dest_ref[k:k + 1, :] = start.astype(jnp.int32) + rank_ref[k:k + 1, :]


def _dest(idx_t, rank_t, pad_start):
    t_tok = idx_t.shape[1]
    slot_blk = pl.BlockSpec((TOP_K, TM_DEST), lambda i: (0, i))
    return pl.pallas_call(
        _dest_kernel,
        grid=(t_tok // TM_DEST,),
        in_specs=[slot_blk, slot_blk, pl.BlockSpec((N_EXPERTS, 1), lambda i: (0, 0))],
        out_specs=slot_blk,
        out_shape=jax.ShapeDtypeStruct((TOP_K, t_tok), jnp.int32),
        compiler_params=pltpu.CompilerParams(dimension_semantics=("arbitrary",)),
        name="dest",
    )(idx_t, rank_t, pad_start)


def _sc_mesh():
    return plsc.VectorSubcoreMesh(core_axis_name="c", subcore_axis_name="s")


def _sc_worker_id():
    return lax.axis_index("s") * SC_CORES + lax.axis_index("c")


def _dispatch(h2p, dest_t, zero_rows, n_rows):
    t_tok, width = h2p.shape
    per_w = t_tok // SC_WORKERS
    n_chunks = per_w // SC_CHUNK
    z_chunks = zero_rows.shape[0] // (SC_WORKERS * SC_CHUNK)
    dest_w = (dest_t.reshape(TOP_K, SC_WORKERS, n_chunks, SC_CHUNK)
              .transpose(1, 2, 0, 3).reshape(SC_WORKERS * n_chunks * TOP_K, SC_CHUNK))
    zero_w = zero_rows.reshape(SC_WORKERS * z_chunks, SC_CHUNK)

    @functools.partial(
        pl.kernel, mesh=_sc_mesh(),
        out_type=jax.ShapeDtypeStruct((n_rows, width), h2p.dtype),
        scratch_types=[pltpu.VMEM((n_chunks * TOP_K, SC_CHUNK), jnp.int32),
                       pltpu.VMEM((z_chunks, SC_CHUNK), jnp.int32),
                       pltpu.VMEM((2, SC_CHUNK, width), h2p.dtype),
                       pltpu.VMEM((SC_CHUNK, width), h2p.dtype),
                       pltpu.SemaphoreType.DMA, pltpu.SemaphoreType.DMA,
                       pltpu.SemaphoreType.DMA],
        name="dispatch",
    )
    def k(h2p_hbm, dest_hbm, zidx_hbm, zsrc_hbm, xs_hbm, idx_v, zidx_v, rows_v, zero_v,
          gsem, wsem, zsem):
        wid = _sc_worker_id()
        base = wid * per_w
        pltpu.sync_copy(dest_hbm.at[pl.ds(wid * n_chunks * TOP_K, n_chunks * TOP_K)], idx_v)
        pltpu.sync_copy(zidx_hbm.at[pl.ds(wid * z_chunks, z_chunks)], zidx_v)
        pltpu.sync_copy(zsrc_hbm, zero_v)

        def zput(j):
            return pltpu.make_async_copy(zero_v, xs_hbm.at[zidx_v.at[j]], zsem)

        for j in range(z_chunks):
            zput(j).start()

        def get(j, slot):
            return pltpu.make_async_copy(h2p_hbm.at[pl.ds(base + j * SC_CHUNK, SC_CHUNK)],
                                         rows_v.at[slot], gsem)

        def put(j, slot, kk):
            return pltpu.make_async_copy(rows_v.at[slot], xs_hbm.at[idx_v.at[j * TOP_K + kk]], wsem)

        get(0, 0).start()

        @pl.loop(0, n_chunks, step=2)
        def _(j):
            for b in range(2):
                jj = j + b
                get(jj, b).wait()

                @pl.when(jj >= 1)
                def _():
                    for kk in range(TOP_K):
                        put(jj - 1, 1 - b, kk).wait()

                @pl.when(jj + 1 < n_chunks)
                def _():
                    get(jj + 1, 1 - b).start()
                for kk in range(TOP_K):
                    put(jj, b, kk).start()

        for kk in range(TOP_K):
            put(n_chunks - 1, (n_chunks - 1) % 2, kk).wait()
        for j in range(z_chunks):
            zput(j).wait()

    return k(h2p, dest_w, zero_w, jnp.zeros((SC_CHUNK, width), h2p.dtype))


def _regroup(ys, dest_t):
    width = ys.shape[1]
    n_slots, t_tok = dest_t.shape
    m_rows = n_slots * t_tok
    per_w = m_rows // SC_WORKERS
    n_chunks = per_w // SC_CHUNK
    idx2 = dest_t.reshape(SC_WORKERS * n_chunks, SC_CHUNK)

    @functools.partial(
        pl.kernel, mesh=_sc_mesh(),
        out_type=jax.ShapeDtypeStruct((m_rows, width), ys.dtype),
        scratch_types=[pltpu.VMEM((n_chunks, SC_CHUNK), jnp.int32),
                       pltpu.VMEM((2, SC_CHUNK, width), ys.dtype),
                       pltpu.SemaphoreType.DMA, pltpu.SemaphoreType.DMA],
        name="regroup",
    )
    def k(ys_hbm, idx_hbm, out_hbm, idx_v, rows_v, gsem, wsem):
        wid = _sc_worker_id()
        base = wid * per_w
        pltpu.sync_copy(idx_hbm.at[pl.ds(wid * n_chunks, n_chunks)], idx_v)

        def get(j, slot):
            return pltpu.make_async_copy(ys_hbm.at[idx_v.at[j]], rows_v.at[slot], gsem)

        def put(j, slot):
            return pltpu.make_async_copy(rows_v.at[slot],
                                         out_hbm.at[pl.ds(base + j * SC_CHUNK, SC_CHUNK)], wsem)

        get(0, 0).start()

        @pl.loop(0, n_chunks, step=2)
        def _(j):
            for b in range(2):
                jj = j + b
                get(jj, b).wait()

                @pl.when(jj >= 1)
                def _():
                    put(jj - 1, 1 - b).wait()

                @pl.when(jj + 1 < n_chunks)
                def _():
                    get(jj + 1, 1 - b).start()
                put(jj, b).start()

        put(n_chunks - 1, (n_chunks - 1) % 2).wait()

    return k(ys, idx2).reshape(n_slots, t_tok, width)


def _experts_kernel(n_blk, blk_e_ref, next_e_ref, n_used_ref,
                    xs_hbm, eg_hbm, eu_hbm, ed_hbm, ys_hbm,
                    xbuf, ybuf, hid, stg_g, stg_u, stg_d, wg, wu, wd, xsem, ysem, wsem):
    n = n_used_ref[0]

    def ring(b):
        return jnp.bitwise_and(b, ROW_RING - 1)

    def x_copy(b):
        return pltpu.make_async_copy(xs_hbm.at[pl.ds(pl.multiple_of(b * ROW_BLOCK, ROW_BLOCK),
                                                     ROW_BLOCK)], xbuf.at[ring(b)], xsem.at[ring(b)])

    def y_copy(b):
        return pltpu.make_async_copy(ybuf.at[ring(b)],
                                     ys_hbm.at[pl.ds(pl.multiple_of(b * ROW_BLOCK, ROW_BLOCK),
                                                     ROW_BLOCK)], ysem.at[ring(b)])

    def w_copies(e):
        return (pltpu.make_async_copy(eg_hbm.at[e], stg_g, wsem.at[0]),
                pltpu.make_async_copy(eu_hbm.at[e], stg_u, wsem.at[1]),
                pltpu.make_async_copy(ed_hbm.at[e], stg_d, wsem.at[2]))

    def switch_expert(e, wslot):
        for cp in w_copies(e):
            cp.wait()
        wg[wslot] = stg_g[...].astype(BF16)
        wu[wslot] = stg_u[...].astype(BF16)
        wd[wslot] = stg_d[...].astype(BF16)
        nxt = next_e_ref[e]

        @pl.when(nxt >= 0)
        def _():
            for cp in w_copies(nxt):
                cp.start()

    def gate_up(b, wslot):
        xa, xb = _unpack_pair(xbuf[ring(b)])
        xb16 = jnp.concatenate([xa, xb], axis=1).astype(BF16)
        g = _dot(xb16, wg[wslot])
        up = _dot(xb16, wu[wslot])
        hid[jnp.bitwise_and(b, 1)] = (g * _sigmoid(g) * up).astype(BF16)

    def down(b, wslot):
        y = _dot(hid[jnp.bitwise_and(b, 1)], wd[wslot])
        ybuf[ring(b)] = _pack_pair(y[:, 0:HALF], y[:, HALF:D_MODEL])

    e0 = blk_e_ref[0]
    for cp in w_copies(e0):
        cp.start()
    for j in range(ROW_RING):
        @pl.when(j < n)
        def _():
            x_copy(j).start()
    switch_expert(e0, 0)
    x_copy(0).wait()
    gate_up(0, 0)

    def body(b, wslot_prev):
        e = blk_e_ref[b]
        first = e != blk_e_ref[b - 1]
        wslot = jnp.where(first, 1 - wslot_prev, wslot_prev)

        @pl.when(first)
        def _():
            switch_expert(e, wslot)

        x_copy(b).wait()

        @pl.when(b + ROW_RING - 1 < n)
        def _():
            x_copy(b + ROW_RING - 1).start()

        @pl.when(b >= ROW_RING + 1)
        def _():
            y_copy(b - 1 - ROW_RING).wait()

        gate_up(b, wslot)
        down(b - 1, wslot_prev)
        y_copy(b - 1).start()
        return wslot

    wslot_last = lax.fori_loop(1, n, body, jnp.int32(0))

    last = n - 1

    @pl.when(last >= ROW_RING)
    def _():
        y_copy(last - ROW_RING).wait()
    down(last, wslot_last)
    y_copy(last).start()
    for j in range(ROW_RING - 1, -1, -1):
        @pl.when(last - j >= 0)
        def _():
            y_copy(last - j).wait()

    ybuf[0] = jnp.zeros((ROW_BLOCK, HALF), U32)

    def zero_tail(b, c):
        cp = pltpu.make_async_copy(ybuf.at[0],
                                   ys_hbm.at[pl.ds(pl.multiple_of(b * ROW_BLOCK, ROW_BLOCK),
                                                   ROW_BLOCK)], ysem.at[0])
        cp.start()
        cp.wait()
        return c
    lax.fori_loop(n, n_blk, zero_tail, 0)


def _experts(blk_e, next_e, n_used, xs, e_gate, e_up, e_down):
    n_rows = xs.shape[0]
    n_blk = n_rows // ROW_BLOCK
    any_spec = pl.BlockSpec(memory_space=pl.ANY)
    grid_spec = pltpu.PrefetchScalarGridSpec(
        num_scalar_prefetch=3,
        grid=(1,),
        in_specs=[any_spec, any_spec, any_spec, any_spec],
        out_specs=any_spec,
        scratch_shapes=[pltpu.VMEM((ROW_RING, ROW_BLOCK, HALF), U32),
                        pltpu.VMEM((ROW_RING, ROW_BLOCK, HALF), U32),
                        pltpu.VMEM((2, ROW_BLOCK, EXPERT_HIDDEN), BF16),
                        pltpu.VMEM((D_MODEL, EXPERT_HIDDEN), F32),
                        pltpu.VMEM((D_MODEL, EXPERT_HIDDEN), F32),
                        pltpu.VMEM((EXPERT_HIDDEN, D_MODEL), F32),
                        pltpu.VMEM((2, D_MODEL, EXPERT_HIDDEN), BF16),
                        pltpu.VMEM((2, D_MODEL, EXPERT_HIDDEN), BF16),
                        pltpu.VMEM((2, EXPERT_HIDDEN, D_MODEL), BF16),
                        pltpu.SemaphoreType.DMA((ROW_RING,)),
                        pltpu.SemaphoreType.DMA((ROW_RING,)),
                        pltpu.SemaphoreType.DMA((3,))],
    )
    return pl.pallas_call(
        functools.partial(_experts_kernel, n_blk),
        grid_spec=grid_spec,
        out_shape=jax.ShapeDtypeStruct((n_rows, HALF), U32),
        compiler_params=pltpu.CompilerParams(dimension_semantics=("arbitrary",)),
        name="experts",
    )(blk_e, next_e, n_used, xs, e_gate, e_up, e_down)


def _combine_kernel(yg_ref, wsel_ref, xres_ref, g_final_ref, *rest):
    out_ref = rest[-1]
    xres = xres_ref[...]
    acc_a = xres[:, 0:HALF]
    acc_b = xres[:, HALF:D_MODEL]
    wsel = wsel_ref[...]
    for k in range(TOP_K):
        ya, yb = _unpack_pair(yg_ref[k])
        wk = wsel[:, k:k + 1]
        acc_a = acc_a + ya * wk
        acc_b = acc_b + yb * wk
    out_ref[...] = _rms(jnp.concatenate([acc_a, acc_b], axis=1), g_final_ref[...])


def _combine(yg, wsel_tk, xres, g_final, out_prev, chunk):
    t_tok = xres.shape[0]
    tm = TM_COMB
    steps = yg.shape[1] // tm
    off = chunk * steps
    in_specs = [pl.BlockSpec((TOP_K, tm, HALF), lambda i: (0, i, 0)),
                pl.BlockSpec((tm, TOP_K), lambda i: (i + off, 0)),
                pl.BlockSpec((tm, D_MODEL), lambda i: (i + off, 0)),
                pl.BlockSpec((1, D_MODEL), lambda i: (0, 0))]
    args = [yg, wsel_tk, xres, g_final]
    aliases = {}
    if out_prev is not None:
        in_specs.append(pl.BlockSpec(memory_space=pl.ANY))
        args.append(out_prev)
        aliases = {4: 0}
    return pl.pallas_call(
        _combine_kernel,
        grid=(steps,),
        in_specs=in_specs,
        out_specs=pl.BlockSpec((tm, D_MODEL), lambda i: (i + off, 0)),
        out_shape=jax.ShapeDtypeStruct((t_tok, D_MODEL), F32),
        input_output_aliases=aliases,
        compiler_params=pltpu.CompilerParams(dimension_semantics=("arbitrary",),
                                             vmem_limit_bytes=VMEM_LIMIT),
        name="combine",
    )(*args)


def kernel(x, g_mix, w_in, b_gate, w_pool_group, pool_scale, w_pool_out, conv_w, w_conv_out, w_o,
           g_ffn, w_router, router_bias, e_gate, e_up, e_down, s_gate, s_up, s_down, g_final):
    b, s, d = x.shape
    t_tok = b * s
    n_pad = N_EXPERTS * ROW_BLOCK
    assert d == D_MODEL and s % TM_MIX == 0 and TM_MIX >= POOL_HALO
    assert t_tok % TM_DEST == 0 and t_tok % (N_COMB_CHUNKS * TM_COMB) == 0
    assert (t_tok // N_COMB_CHUNKS * TOP_K) % (2 * SC_WORKERS * SC_CHUNK) == 0
    assert t_tok % (2 * SC_WORKERS * SC_CHUNK) == 0 and n_pad % (SC_WORKERS * SC_CHUNK) == 0

    row = lambda a: a.reshape(1, -1)
    wr_t = w_router.T.astype(F32)
    wr_hi = wr_t.astype(BF16)
    wr = jnp.concatenate([wr_hi, (wr_t - wr_hi.astype(F32)).astype(BF16)], axis=0)

    xres, h2p, idx_t, wsel_t, rank_t, counts = _mixer_router(
        x.reshape(t_tok, d), s, row(g_mix), w_in.astype(BF16), row(b_gate),
        w_pool_group.astype(BF16), row(pool_scale), w_pool_out.astype(BF16), conv_w,
        w_conv_out.astype(BF16), w_o.astype(BF16), row(g_ffn), wr,
        router_bias.astype(F32).reshape(N_EXPERTS, 1), s_gate.astype(BF16), s_up.astype(BF16),
        s_down.astype(BF16))

    counts = counts.reshape(N_EXPERTS)
    padded = (counts + ROW_BLOCK - 1) // ROW_BLOCK * ROW_BLOCK
    pad_end = jnp.cumsum(padded)
    pad_start = pad_end - padded
    n_rows = t_tok * TOP_K + n_pad
    n_blk = n_rows // ROW_BLOCK
    n_used = (pad_end[-1:] // ROW_BLOCK).astype(jnp.int32)
    blk_row0 = jnp.arange(n_blk, dtype=jnp.int32) * ROW_BLOCK
    blk_e = jnp.minimum(jnp.sum(pad_end[None, :] <= blk_row0[:, None], axis=1),
                        N_EXPERTS - 1).astype(jnp.int32)
    e_ids = jnp.arange(N_EXPERTS, dtype=jnp.int32)
    later_used = jnp.logical_and(e_ids[None, :] > e_ids[:, None], padded[None, :] > 0)
    next_e = jnp.min(jnp.where(later_used, e_ids[None, :], N_EXPERTS), axis=1)
    next_e = jnp.where(next_e < N_EXPERTS, next_e, -1).astype(jnp.int32)
    gap = padded - counts
    gap_before = jnp.cumsum(gap) - gap
    j_pad = jnp.arange(n_pad, dtype=jnp.int32)
    zero_rows = (j_pad + jnp.sum(jnp.where(gap_before[None, :] <= j_pad[:, None],
                                           counts[None, :], 0), axis=1)).astype(jnp.int32)

    dest_t = _dest(idx_t, rank_t, pad_start.reshape(N_EXPERTS, 1).astype(jnp.int32))
    xs = _dispatch(h2p, dest_t, zero_rows, n_rows)
    ys = _experts(blk_e, next_e, n_used, xs, e_gate, e_up, e_down)
    wsel_tk = wsel_t.T
    t_chunk = t_tok // N_COMB_CHUNKS
    out = None
    for c in range(N_COMB_CHUNKS):
        yg = _regroup(ys, dest_t[:, c * t_chunk:(c + 1) * t_chunk])
        out = _combine(yg, wsel_tk, xres, row(g_final), out, c)
    return out.reshape(b, s, d)
```

```python
import functools

import jax
import jax.numpy as jnp
from jax import lax
from jax.experimental import pallas as pl
from jax.experimental.pallas import tpu as pltpu
from jax.experimental.pallas import tpu_sc as plsc

D_MODEL = 1024
HALF = D_MODEL // 2
POOL_WIDTH = 512
N_POOL_GROUPS = 4
POOL_GROUP = 128
POOL_WINDOWS = (2, 4, 8, 16)
CONV_WIDTH = 512
N_EXPERTS = 64
TOP_K = 8
EXPERT_HIDDEN = 256
SHARED_HIDDEN = 256
ROUTED_SCALE = 2.5
EPS = 1e-6

POOL_HALO = 16
CONV_HALO = 8
TM_MIX = 512
TM_DEST = 2048
ROW_BLOCK = 256
ROW_RING = 4
TM_COMB = 512
N_COMB_CHUNKS = 4
VMEM_LIMIT = 56 * 1024 * 1024

SC_CORES = 2
SC_SUBCORES = 16
SC_WORKERS = SC_CORES * SC_SUBCORES
SC_CHUNK = 64

BF16 = jnp.bfloat16
F32 = jnp.float32
U32 = jnp.uint32


def _rms(x, g):
    r = lax.rsqrt(jnp.mean(x * x, axis=-1, keepdims=True) + EPS)
    return (x * r) * g


def _dot(a, b):
    return jnp.dot(a, b, preferred_element_type=F32)


def _sigmoid(z):
    return 0.5 * jnp.tanh(0.5 * z) + 0.5


def _pack_pair(a, b):
    ra = lax.bitcast_convert_type(a.astype(BF16).astype(F32), U32)
    rb = lax.bitcast_convert_type(b.astype(BF16).astype(F32), U32)
    return ra | (rb >> 16)


def _unpack_pair(w):
    a = lax.bitcast_convert_type(w & jnp.uint32(0xFFFF0000), F32)
    b = lax.bitcast_convert_type(w << 16, F32)
    return a, b


def _mixer_router_kernel(n_seq_tiles,
                         x_ref, g_mix_ref, w_in_ref, b_gate_ref, w_grp_ref, pool_scale_ref,
                         w_po_ref, conv_w_ref, w_co_ref, w_o_ref, g_ffn_ref,
                         wr_ref, rbias_ref, s_gate_ref, s_up_ref, s_down_ref,
                         xres_ref, h2p_ref, idx_ref, wsel_ref, rank_ref, counts_ref,
                         ext_pool, ext_conv, cnt_carry, tri):
    tm = x_ref.shape[0]
    i = pl.program_id(0)
    st = i % n_seq_tiles

    @pl.when(i == 0)
    def _():
        r = lax.broadcasted_iota(jnp.int32, (tm, tm), 0)
        c = lax.broadcasted_iota(jnp.int32, (tm, tm), 1)
        tri[...] = (r < c).astype(BF16)
        cnt_carry[...] = jnp.zeros_like(cnt_carry)

    @pl.when(st == 0)
    def _():
        ext_pool[0:POOL_HALO, :] = jnp.zeros((POOL_HALO, POOL_WIDTH), F32)
        ext_conv[0:CONV_HALO, :] = jnp.zeros((CONV_HALO, CONV_WIDTH), F32)

    x = x_ref[...]
    hb = _rms(x, g_mix_ref[...]).astype(BF16)

    o0 = POOL_WIDTH
    o1 = o0 + CONV_WIDTH
    o2 = o1 + CONV_WIDTH
    o3 = o2 + CONV_WIDTH

    u = _dot(hb, w_in_ref[:, 0:o0])
    ext_pool[POOL_HALO:POOL_HALO + tm, :] = u
    gc = _dot(hb, w_in_ref[:, o1:o2])
    v = _dot(hb, w_in_ref[:, o2:o3])
    pre_a = _dot(hb, w_in_ref[:, o3:o3 + D_MODEL])

    t_glob = st * tm + lax.broadcasted_iota(jnp.int32, (tm, 1), 0)
    mixed = []
    for gi, w in enumerate(POOL_WINDOWS):
        cols = slice(gi * POOL_GROUP, (gi + 1) * POOL_GROUP)
        ug = u[:, cols]
        acc = ug
        for j in range(1, w):
            acc = acc + ext_pool[POOL_HALO - j:POOL_HALO - j + tm, cols]
        cnt = jnp.minimum(t_glob + 1, w).astype(F32)
        pooled = acc * (1.0 / cnt) - ug
        mixed.append(_dot(pooled.astype(BF16), w_grp_ref[gi]))
    ext_pool[0:POOL_HALO, :] = ext_pool[tm:tm + POOL_HALO, :]
    pre_b = _dot(hb, w_in_ref[:, o3 + D_MODEL:o3 + 2 * D_MODEL])
    gb = _dot(hb, w_in_ref[:, o0:o1])

    cv = gc * v
    ext_conv[CONV_HALO:CONV_HALO + tm, :] = cv
    conv = (ext_conv[CONV_HALO - 2:CONV_HALO - 2 + tm, :] * conv_w_ref[0:1, :]
            + ext_conv[CONV_HALO - 1:CONV_HALO - 1 + tm, :] * conv_w_ref[1:2, :]
            + cv * conv_w_ref[2:3, :])
    ext_conv[0:CONV_HALO, :] = ext_conv[tm:tm + CONV_HALO, :]
    branch_b = _dot((gb * conv).astype(BF16), w_co_ref[...])
    mixed = jnp.concatenate(mixed, axis=1) * pool_scale_ref[...]
    branch_a = _dot(mixed.astype(BF16), w_po_ref[...])

    merged = (_sigmoid(pre_a + b_gate_ref[:, 0:D_MODEL]) * branch_a
              + _sigmoid(pre_b + b_gate_ref[:, D_MODEL:2 * D_MODEL]) * branch_b)
    x1 = x + _dot(merged.astype(BF16), w_o_ref[...])

    h2 = _rms(x1, g_ffn_ref[...])
    h2p_ref[...] = _pack_pair(h2[:, 0:HALF], h2[:, HALF:D_MODEL])
    h2b = h2.astype(BF16)

    nt = (((1,), (1,)), ((), ()))
    parts = lax.dot_general(wr_ref[...], h2b, nt, preferred_element_type=F32)
    logits = parts[0:N_EXPERTS, :] + parts[N_EXPERTS:2 * N_EXPERTS, :]
    sg = _dot(h2b, s_gate_ref[...])
    su = _dot(h2b, s_up_ref[...])
    scores = jax.nn.sigmoid(logits)
    sel = scores + rbias_ref[...]
    eidx = lax.broadcasted_iota(jnp.int32, (N_EXPERTS, tm), 0).astype(F32)
    e_rows, w_rows = [], []
    mask = jnp.zeros((N_EXPERTS, tm), F32)
    for _ in range(TOP_K):
        m = jnp.max(sel, axis=0, keepdims=True)
        ek = jnp.min(jnp.where(sel == m, eidx, float(N_EXPERTS)), axis=0, keepdims=True)
        oh = eidx == ek
        w_rows.append(jnp.sum(jnp.where(oh, scores, 0.0), axis=0, keepdims=True))
        e_rows.append(ek)
        mask = mask + oh.astype(F32)
        sel = jnp.where(oh, -jnp.inf, sel)

    shared = _dot((sg * _sigmoid(sg) * su).astype(BF16), s_down_ref[...])
    xres_ref[...] = x1 + shared

    wsum = w_rows[0]
    for k in range(1, TOP_K):
        wsum = wsum + w_rows[k]

    before = _dot(mask.astype(BF16), tri[...]) + cnt_carry[...]
    for k in range(TOP_K):
        oh = eidx == e_rows[k]
        rank_ref[k:k + 1, :] = jnp.sum(jnp.where(oh, before, 0.0), axis=0,
                                       keepdims=True).astype(jnp.int32)
        idx_ref[k:k + 1, :] = e_rows[k].astype(jnp.int32)
        wsel_ref[k:k + 1, :] = w_rows[k] / wsum * ROUTED_SCALE
    total = cnt_carry[...] + jnp.sum(mask, axis=1, keepdims=True)
    cnt_carry[...] = total
    counts_ref[...] = total.astype(jnp.int32)


def _mixer_router(x2d, seq_len, g_mix, w_in, b_gate, w_grp, pool_scale, w_po, conv_w, w_co, w_o,
                  g_ffn, wr, rbias, s_gate, s_up, s_down):
    t_tok = x2d.shape[0]
    tm = TM_MIX
    n_seq_tiles = seq_len // tm
    const = lambda shape: pl.BlockSpec(shape, lambda i: (0,) * len(shape),
                                       pipeline_mode=pl.Buffered(1))
    row_blk = pl.BlockSpec((tm, D_MODEL), lambda i: (i, 0))
    half_blk = pl.BlockSpec((tm, HALF), lambda i: (i, 0))
    slot_blk = pl.BlockSpec((TOP_K, tm), lambda i: (0, i))
    return pl.pallas_call(
        functools.partial(_mixer_router_kernel, n_seq_tiles),
        grid=(t_tok // tm,),
        in_specs=[row_blk, const(g_mix.shape), const(w_in.shape), const(b_gate.shape),
                  const(w_grp.shape), const(pool_scale.shape), const(w_po.shape),
                  const(conv_w.shape), const(w_co.shape), const(w_o.shape), const(g_ffn.shape),
                  const(wr.shape), const(rbias.shape),
                  const(s_gate.shape), const(s_up.shape), const(s_down.shape)],
        out_specs=[row_blk, half_blk, slot_blk, slot_blk, slot_blk,
                   pl.BlockSpec((N_EXPERTS, 1), lambda i: (0, 0))],
        out_shape=[jax.ShapeDtypeStruct((t_tok, D_MODEL), F32),
                   jax.ShapeDtypeStruct((t_tok, HALF), U32),
                   jax.ShapeDtypeStruct((TOP_K, t_tok), jnp.int32),
                   jax.ShapeDtypeStruct((TOP_K, t_tok), F32),
                   jax.ShapeDtypeStruct((TOP_K, t_tok), jnp.int32),
                   jax.ShapeDtypeStruct((N_EXPERTS, 1), jnp.int32)],
        scratch_shapes=[pltpu.VMEM((POOL_HALO + tm, POOL_WIDTH), F32),
                        pltpu.VMEM((CONV_HALO + tm, CONV_WIDTH), F32),
                        pltpu.VMEM((N_EXPERTS, 1), F32),
                        pltpu.VMEM((tm, tm), BF16)],
        compiler_params=pltpu.CompilerParams(dimension_semantics=("arbitrary",),
                                             vmem_limit_bytes=VMEM_LIMIT),
        name="mixer_router",
    )(x2d, g_mix, w_in, b_gate, w_grp, pool_scale, w_po, conv_w, w_co, w_o, g_ffn,
      wr, rbias, s_gate, s_up, s_down)


def _dest_kernel(idx_ref, rank_ref, pad_start_ref, dest_ref):
    tm = idx_ref.shape[1]
    eidx = lax.broadcasted_iota(jnp.int32, (N_EXPERTS, tm), 0)
    ps = pad_start_ref[...].astype(F32)
    for k in range(TOP_K):
        oh = eidx == idx_ref[k:k + 1, :]
        start = jnp.sum(jnp.where(oh, ps, 0.0), axis=0, keepdims=True)
        dest_ref[k:k + 1, :] = start.astype(jnp.int32) + rank_ref[k:k + 1, :]


def _dest(idx_t, rank_t, pad_start):
    t_tok = idx_t.shape[1]
    slot_blk = pl.BlockSpec((TOP_K, TM_DEST), lambda i: (0, i))
    return pl.pallas_call(
        _dest_kernel,
        grid=(t_tok // TM_DEST,),
        in_specs=[slot_blk, slot_blk, pl.BlockSpec((N_EXPERTS, 1), lambda i: (0, 0))],
        out_specs=slot_blk,
        out_shape=jax.ShapeDtypeStruct((TOP_K, t_tok), jnp.int32),
        compiler_params=pltpu.CompilerParams(dimension_semantics=("arbitrary",)),
        name="dest",
    )(idx_t, rank_t, pad_start)


def _sc_mesh():
    return plsc.VectorSubcoreMesh(core_axis_name="c", subcore_axis_name="s")


def _sc_worker_id():
    return lax.axis_index("s") * SC_CORES + lax.axis_index("c")


def _dispatch(h2p, dest_t, zero_rows, n_rows):
    t_tok, width = h2p.shape
    per_w = t_tok // SC_WORKERS
    n_chunks = per_w // SC_CHUNK
    z_chunks = zero_rows.shape[0] // (SC_WORKERS * SC_CHUNK)
    dest_w = (dest_t.reshape(TOP_K, SC_WORKERS, n_chunks, SC_CHUNK)
              .transpose(1, 2, 0, 3).reshape(SC_WORKERS * n_chunks * TOP_K, SC_CHUNK))
    zero_w = zero_rows.reshape(SC_WORKERS * z_chunks, SC_CHUNK)

    @functools.partial(
        pl.kernel, mesh=_sc_mesh(),
        out_type=jax.ShapeDtypeStruct((n_rows, width), h2p.dtype),
        scratch_types=[pltpu.VMEM((n_chunks * TOP_K, SC_CHUNK), jnp.int32),
                       pltpu.VMEM((z_chunks, SC_CHUNK), jnp.int32),
                       pltpu.VMEM((2, SC_CHUNK, width), h2p.dtype),
                       pltpu.VMEM((SC_CHUNK, width), h2p.dtype),
                       pltpu.SemaphoreType.DMA, pltpu.SemaphoreType.DMA,
                       pltpu.SemaphoreType.DMA],
        name="dispatch",
    )
    def k(h2p_hbm, dest_hbm, zidx_hbm, zsrc_hbm, xs_hbm, idx_v, zidx_v, rows_v, zero_v,
          gsem, wsem, zsem):
        wid = _sc_worker_id()
        base = wid * per_w
        pltpu.sync_copy(dest_hbm.at[pl.ds(wid * n_chunks * TOP_K, n_chunks * TOP_K)], idx_v)
        pltpu.sync_copy(zidx_hbm.at[pl.ds(wid * z_chunks, z_chunks)], zidx_v)
        pltpu.sync_copy(zsrc_hbm, zero_v)

        def zput(j):
            return pltpu.make_async_copy(zero_v, xs_hbm.at[zidx_v.at[j]], zsem)

        for j in range(z_chunks):
            zput(j).start()

        def get(j, slot):
            return pltpu.make_async_copy(h2p_hbm.at[pl.ds(base + j * SC_CHUNK, SC_CHUNK)],
                                         rows_v.at[slot], gsem)

        def put(j, slot, kk):
            return pltpu.make_async_copy(rows_v.at[slot], xs_hbm.at[idx_v.at[j * TOP_K + kk]], wsem)

        get(0, 0).start()

        @pl.loop(0, n_chunks, step=2)
        def _(j):
            for b in range(2):
                jj = j + b
                get(jj, b).wait()

                @pl.when(jj >= 1)
                def _():
                    for kk in range(TOP_K):
                        put(jj - 1, 1 - b, kk).wait()

                @pl.when(jj + 1 < n_chunks)
                def _():
                    get(jj + 1, 1 - b).start()
                for kk in range(TOP_K):
                    put(jj, b, kk).start()

        for kk in range(TOP_K):
            put(n_chunks - 1, (n_chunks - 1) % 2, kk).wait()
        for j in range(z_chunks):
            zput(j).wait()

    return k(h2p, dest_w, zero_w, jnp.zeros((SC_CHUNK, width), h2p.dtype))


def _regroup(ys, dest_t):
    width = ys.shape[1]
    n_slots, t_tok = dest_t.shape
    m_rows = n_slots * t_tok
    per_w = m_rows // SC_WORKERS
    n_chunks = per_w // SC_CHUNK
    idx2 = dest_t.reshape(SC_WORKERS * n_chunks, SC_CHUNK)

    @functools.partial(
        pl.kernel, mesh=_sc_mesh(),
        out_type=jax.ShapeDtypeStruct((m_rows, width), ys.dtype),
        scratch_types=[pltpu.VMEM((n_chunks, SC_CHUNK), jnp.int32),
                       pltpu.VMEM((2, SC_CHUNK, width), ys.dtype),
                       pltpu.SemaphoreType.DMA, pltpu.SemaphoreType.DMA],
        name="regroup",
    )
    def k(ys_hbm, idx_hbm, out_hbm, idx_v, rows_v, gsem, wsem):
        wid = _sc_worker_id()
        base = wid * per_w
        pltpu.sync_copy(idx_hbm.at[pl.ds(wid * n_chunks, n_chunks)], idx_v)

        def get(j, slot):
            return pltpu.make_async_copy(ys_hbm.at[idx_v.at[j]], rows_v.at[slot], gsem)

        def put(j, slot):
            return pltpu.make_async_copy(rows_v.at[slot],
                                         out_hbm.at[pl.ds(base + j * SC_CHUNK, SC_CHUNK)], wsem)

        get(0, 0).start()

        @pl.loop(0, n_chunks, step=2)
        def _(j):
            for b in range(2):
                jj = j + b
                get(jj, b).wait()

                @pl.when(jj >= 1)
                def _():
                    put(jj - 1, 1 - b).wait()

                @pl.when(jj + 1 < n_chunks)
                def _():
                    get(jj + 1, 1 - b).start()
                put(jj, b).start()

        put(n_chunks - 1, (n_chunks - 1) % 2).wait()

    return k(ys, idx2).reshape(n_slots, t_tok, width)


def _experts_kernel(n_blk, blk_e_ref, next_e_ref, n_used_ref,
                    xs_hbm, eg_hbm, eu_hbm, ed_hbm, ys_hbm,
                    xbuf, ybuf, hid, stg_g, stg_u, stg_d, wg, wu, wd, xsem, ysem, wsem):
    n = n_used_ref[0]

    def ring(b):
        return jnp.bitwise_and(b, ROW_RING - 1)

    def x_copy(b):
        return pltpu.make_async_copy(xs_hbm.at[pl.ds(pl.multiple_of(b * ROW_BLOCK, ROW_BLOCK),
                                                     ROW_BLOCK)], xbuf.at[ring(b)], xsem.at[ring(b)])

    def y_copy(b):
        return pltpu.make_async_copy(ybuf.at[ring(b)],
                                     ys_hbm.at[pl.ds(pl.multiple_of(b * ROW_BLOCK, ROW_BLOCK),
                                                     ROW_BLOCK)], ysem.at[ring(b)])

    def w_copies(e):
        return (pltpu.make_async_copy(eg_hbm.at[e], stg_g, wsem.at[0]),
                pltpu.make_async_copy(eu_hbm.at[e], stg_u, wsem.at[1]),
                pltpu.make_async_copy(ed_hbm.at[e], stg_d, wsem.at[2]))

    def switch_expert(e, wslot):
        for cp in w_copies(e):
            cp.wait()
        wg[wslot] = stg_g[...].astype(BF16)
        wu[wslot] = stg_u[...].astype(BF16)
        wd[wslot] = stg_d[...].astype(BF16)
        nxt = next_e_ref[e]

        @pl.when(nxt >= 0)
        def _():
            for cp in w_copies(nxt):
                cp.start()

    def gate_up(b, wslot):
        xa, xb = _unpack_pair(xbuf[ring(b)])
        xb16 = jnp.concatenate([xa, xb], axis=1).astype(BF16)
        g = _dot(xb16, wg[wslot])
        up = _dot(xb16, wu[wslot])
        hid[jnp.bitwise_and(b, 1)] = (g * _sigmoid(g) * up).astype(BF16)

    def down(b, wslot):
        y = _dot(hid[jnp.bitwise_and(b, 1)], wd[wslot])
        ybuf[ring(b)] = _pack_pair(y[:, 0:HALF], y[:, HALF:D_MODEL])

    e0 = blk_e_ref[0]
    for cp in w_copies(e0):
        cp.start()
    for j in range(ROW_RING):
        @pl.when(j < n)
        def _():
            x_copy(j).start()
    switch_expert(e0, 0)
    x_copy(0).wait()
    gate_up(0, 0)

    def body(b, wslot_prev):
        e = blk_e_ref[b]
        first = e != blk_e_ref[b - 1]
        wslot = jnp.where(first, 1 - wslot_prev, wslot_prev)

        @pl.when(first)
        def _():
            switch_expert(e, wslot)

        x_copy(b).wait()

        @pl.when(b + ROW_RING - 1 < n)
        def _():
            x_copy(b + ROW_RING - 1).start()

        @pl.when(b >= ROW_RING + 1)
        def _():
            y_copy(b - 1 - ROW_RING).wait()

        down(b - 1, wslot_prev)
        gate_up(b, wslot)
        y_copy(b - 1).start()
        return wslot

    wslot_last = lax.fori_loop(1, n, body, jnp.int32(0))

    last = n - 1

    @pl.when(last >= ROW_RING)
    def _():
        y_copy(last - ROW_RING).wait()
    down(last, wslot_last)
    y_copy(last).start()
    for j in range(ROW_RING - 1, -1, -1):
        @pl.when(last - j >= 0)
        def _():
            y_copy(last - j).wait()

    ybuf[0] = jnp.zeros((ROW_BLOCK, HALF), U32)

    def zero_tail(b, c):
        cp = pltpu.make_async_copy(ybuf.at[0],
                                   ys_hbm.at[pl.ds(pl.multiple_of(b * ROW_BLOCK, ROW_BLOCK),
                                                   ROW_BLOCK)], ysem.at[0])
        cp.start()
        cp.wait()
        return c
    lax.fori_loop(n, n_blk, zero_tail, 0)


def _experts(blk_e, next_e, n_used, xs, e_gate, e_up, e_down):
    n_rows = xs.shape[0]
    n_blk = n_rows // ROW_BLOCK
    any_spec = pl.BlockSpec(memory_space=pl.ANY)
    grid_spec = pltpu.PrefetchScalarGridSpec(
        num_scalar_prefetch=3,
        grid=(1,),
        in_specs=[any_spec, any_spec, any_spec, any_spec],
        out_specs=any_spec,
        scratch_shapes=[pltpu.VMEM((ROW_RING, ROW_BLOCK, HALF), U32),
                        pltpu.VMEM((ROW_RING, ROW_BLOCK, HALF), U32),
                        pltpu.VMEM((2, ROW_BLOCK, EXPERT_HIDDEN), BF16),
                        pltpu.VMEM((D_MODEL, EXPERT_HIDDEN), F32),
                        pltpu.VMEM((D_MODEL, EXPERT_HIDDEN), F32),
                        pltpu.VMEM((EXPERT_HIDDEN, D_MODEL), F32),
                        pltpu.VMEM((2, D_MODEL, EXPERT_HIDDEN), BF16),
                        pltpu.VMEM((2, D_MODEL, EXPERT_HIDDEN), BF16),
                        pltpu.VMEM((2, EXPERT_HIDDEN, D_MODEL), BF16),
                        pltpu.SemaphoreType.DMA((ROW_RING,)),
                        pltpu.SemaphoreType.DMA((ROW_RING,)),
                        pltpu.SemaphoreType.DMA((3,))],
    )
    return pl.pallas_call(
        functools.partial(_experts_kernel, n_blk),
        grid_spec=grid_spec,
        out_shape=jax.ShapeDtypeStruct((n_rows, HALF), U32),
        compiler_params=pltpu.CompilerParams(dimension_semantics=("arbitrary",)),
        name="experts",
    )(blk_e, next_e, n_used, xs, e_gate, e_up, e_down)


def _combine_kernel(yg_ref, wsel_ref, xres_ref, g_final_ref, *rest):
    out_ref = rest[-1]
    xres = xres_ref[...]
    acc_a = xres[:, 0:HALF]
    acc_b = xres[:, HALF:D_MODEL]
    wsel = wsel_ref[...]
    for k in range(TOP_K):
        ya, yb = _unpack_pair(yg_ref[k])
        wk = wsel[:, k:k + 1]
        acc_a = acc_a + ya * wk
        acc_b = acc_b + yb * wk
    out_ref[...] = _rms(jnp.concatenate([acc_a, acc_b], axis=1), g_final_ref[...])


def _combine(yg, wsel_tk, xres, g_final, out_prev, chunk):
    t_tok = xres.shape[0]
    tm = TM_COMB
    steps = yg.shape[1] // tm
    off = chunk * steps
    in_specs = [pl.BlockSpec((TOP_K, tm, HALF), lambda i: (0, i, 0)),
                pl.BlockSpec((tm, TOP_K), lambda i: (i + off, 0)),
                pl.BlockSpec((tm, D_MODEL), lambda i: (i + off, 0)),
                pl.BlockSpec((1, D_MODEL), lambda i: (0, 0))]
    args = [yg, wsel_tk, xres, g_final]
    aliases = {}
    if out_prev is not None:
        in_specs.append(pl.BlockSpec(memory_space=pl.ANY))
        args.append(out_prev)
        aliases = {4: 0}
    return pl.pallas_call(
        _combine_kernel,
        grid=(steps,),
        in_specs=in_specs,
        out_specs=pl.BlockSpec((tm, D_MODEL), lambda i: (i + off, 0)),
        out_shape=jax.ShapeDtypeStruct((t_tok, D_MODEL), F32),
        input_output_aliases=aliases,
        compiler_params=pltpu.CompilerParams(dimension_semantics=("arbitrary",),
                                             vmem_limit_bytes=VMEM_LIMIT),
        name="combine",
    )(*args)


def kernel(x, g_mix, w_in, b_gate, w_pool_group, pool_scale, w_pool_out, conv_w, w_conv_out, w_o,
           g_ffn, w_router, router_bias, e_gate, e_up, e_down, s_gate, s_up, s_down, g_final):
    b, s, d = x.shape
    t_tok = b * s
    n_pad = N_EXPERTS * ROW_BLOCK
    assert d == D_MODEL and s % TM_MIX == 0 and TM_MIX >= POOL_HALO
    assert t_tok % TM_DEST == 0 and t_tok % (N_COMB_CHUNKS * TM_COMB) == 0
    assert (t_tok // N_COMB_CHUNKS * TOP_K) % (2 * SC_WORKERS * SC_CHUNK) == 0
    assert t_tok % (2 * SC_WORKERS * SC_CHUNK) == 0 and n_pad % (SC_WORKERS * SC_CHUNK) == 0

    row = lambda a: a.reshape(1, -1)
    wr_t = w_router.T.astype(F32)
    wr_hi = wr_t.astype(BF16)
    wr = jnp.concatenate([wr_hi, (wr_t - wr_hi.astype(F32)).astype(BF16)], axis=0)

    xres, h2p, idx_t, wsel_t, rank_t, counts = _mixer_router(
        x.reshape(t_tok, d), s, row(g_mix), w_in.astype(BF16), row(b_gate),
        w_pool_group.astype(BF16), row(pool_scale), w_pool_out.astype(BF16), conv_w,
        w_conv_out.astype(BF16), w_o.astype(BF16), row(g_ffn), wr,
        router_bias.astype(F32).reshape(N_EXPERTS, 1), s_gate.astype(BF16), s_up.astype(BF16),
        s_down.astype(BF16))

    counts = counts.reshape(N_EXPERTS)
    padded = (counts + ROW_BLOCK - 1) // ROW_BLOCK * ROW_BLOCK
    pad_end = jnp.cumsum(padded)
    pad_start = pad_end - padded
    n_rows = t_tok * TOP_K + n_pad
    n_blk = n_rows // ROW_BLOCK
    n_used = (pad_end[-1:] // ROW_BLOCK).astype(jnp.int32)
    blk_row0 = jnp.arange(n_blk, dtype=jnp.int32) * ROW_BLOCK
    blk_e = jnp.minimum(jnp.sum(pad_end[None, :] <= blk_row0[:, None], axis=1),
                        N_EXPERTS - 1).astype(jnp.int32)
    e_ids = jnp.arange(N_EXPERTS, dtype=jnp.int32)
    later_used = jnp.logical_and(e_ids[None, :] > e_ids[:, None], padded[None, :] > 0)
    next_e = jnp.min(jnp.where(later_used, e_ids[None, :], N_EXPERTS), axis=1)
    next_e = jnp.where(next_e < N_EXPERTS, next_e, -1).astype(jnp.int32)
    gap = padded - counts
    gap_before = jnp.cumsum(gap) - gap
    j_pad = jnp.arange(n_pad, dtype=jnp.int32)
    zero_rows = (j_pad + jnp.sum(jnp.where(gap_before[None, :] <= j_pad[:, None],
                                           counts[None, :], 0), axis=1)).astype(jnp.int32)

    dest_t = _dest(idx_t, rank_t, pad_start.reshape(N_EXPERTS, 1).astype(jnp.int32))
    xs = _dispatch(h2p, dest_t, zero_rows, n_rows)
    ys = _experts(blk_e, next_e, n_used, xs, e_gate, e_up, e_down)
    wsel_tk = wsel_t.T
    t_chunk = t_tok // N_COMB_CHUNKS
    out = None
    for c in range(N_COMB_CHUNKS):
        yg = _regroup(ys, dest_t[:, c * t_chunk:(c + 1) * t_chunk])
        out = _combine(yg, wsel_tk, xres, row(g_final), out, c)
    return out.reshape(b, s, d)
```

```python
import functools

import jax
import jax.numpy as jnp
from jax import lax
from jax.experimental import pallas as pl
from jax.experimental.pallas import tpu as pltpu
from jax.experimental.pallas import tpu_sc as plsc

D_MODEL = 1024
HALF = D_MODEL // 2
POOL_WIDTH = 512
N_POOL_GROUPS = 4
POOL_GROUP = 128
POOL_WINDOWS = (2, 4, 8, 16)
CONV_WIDTH = 512
N_EXPERTS = 64
TOP_K = 8
EXPERT_HIDDEN = 256
SHARED_HIDDEN = 256
ROUTED_SCALE = 2.5
EPS = 1e-6

POOL_HALO = 16
CONV_HALO = 8
TM_MIX = 512
TM_DEST = 2048
ROW_BLOCK = 256
ROW_RING = 4
TM_COMB = 512
N_SPLITS = 2
N_COMB_CHUNKS = 2
VMEM_LIMIT = 56 * 1024 * 1024

SC_CORES = 2
SC_SUBCORES = 16
SC_WORKERS = SC_CORES * SC_SUBCORES
SC_CHUNK = 64

BF16 = jnp.bfloat16
F32 = jnp.float32
U32 = jnp.uint32


def _rms(x, g):
    r = lax.rsqrt(jnp.mean(x * x, axis=-1, keepdims=True) + EPS)
    return (x * r) * g


def _dot(a, b):
    return jnp.dot(a, b, preferred_element_type=F32)


def _sigmoid(z):
    return 0.5 * jnp.tanh(0.5 * z) + 0.5


def _pack_pair(a, b):
    ra = lax.bitcast_convert_type(a.astype(BF16).astype(F32), U32)
    rb = lax.bitcast_convert_type(b.astype(BF16).astype(F32), U32)
    return ra | (rb >> 16)


def _unpack_pair(w):
    a = lax.bitcast_convert_type(w & jnp.uint32(0xFFFF0000), F32)
    b = lax.bitcast_convert_type(w << 16, F32)
    return a, b


def _mixer_router_kernel(n_seq_tiles,
                         x_ref, g_mix_ref, w_in_ref, b_gate_ref, w_grp_ref, pool_scale_ref,
                         w_po_ref, conv_w_ref, w_co_ref, w_o_ref, g_ffn_ref,
                         wr_ref, rbias_ref, s_gate_ref, s_up_ref, s_down_ref,
                         xres_ref, h2p_ref, idx_ref, wsel_ref, rank_ref, counts_ref,
                         ext_pool, ext_conv, cnt_carry, tri):
    tm = x_ref.shape[0]
    i = pl.program_id(0)
    st = i % n_seq_tiles

    @pl.when(i == 0)
    def _():
        r = lax.broadcasted_iota(jnp.int32, (tm, tm), 0)
        c = lax.broadcasted_iota(jnp.int32, (tm, tm), 1)
        tri[...] = (r < c).astype(BF16)
        cnt_carry[...] = jnp.zeros_like(cnt_carry)

    @pl.when(st == 0)
    def _():
        ext_pool[0:POOL_HALO, :] = jnp.zeros((POOL_HALO, POOL_WIDTH), F32)
        ext_conv[0:CONV_HALO, :] = jnp.zeros((CONV_HALO, CONV_WIDTH), F32)

    x = x_ref[...]
    hb = _rms(x, g_mix_ref[...]).astype(BF16)

    o0 = POOL_WIDTH
    o1 = o0 + CONV_WIDTH
    o2 = o1 + CONV_WIDTH
    o3 = o2 + CONV_WIDTH

    u = _dot(hb, w_in_ref[:, 0:o0])
    ext_pool[POOL_HALO:POOL_HALO + tm, :] = u
    gc = _dot(hb, w_in_ref[:, o1:o2])
    v = _dot(hb, w_in_ref[:, o2:o3])
    pre_a = _dot(hb, w_in_ref[:, o3:o3 + D_MODEL])

    t_glob = st * tm + lax.broadcasted_iota(jnp.int32, (tm, 1), 0)
    mixed = []
    for gi, w in enumerate(POOL_WINDOWS):
        cols = slice(gi * POOL_GROUP, (gi + 1) * POOL_GROUP)
        ug = u[:, cols]
        acc = ug
        for j in range(1, w):
            acc = acc + ext_pool[POOL_HALO - j:POOL_HALO - j + tm, cols]
        cnt = jnp.minimum(t_glob + 1, w).astype(F32)
        pooled = acc * (1.0 / cnt) - ug
        mixed.append(_dot(pooled.astype(BF16), w_grp_ref[gi]))
    ext_pool[0:POOL_HALO, :] = ext_pool[tm:tm + POOL_HALO, :]
    pre_b = _dot(hb, w_in_ref[:, o3 + D_MODEL:o3 + 2 * D_MODEL])
    gb = _dot(hb, w_in_ref[:, o0:o1])

    cv = gc * v
    ext_conv[CONV_HALO:CONV_HALO + tm, :] = cv
    conv = (ext_conv[CONV_HALO - 2:CONV_HALO - 2 + tm, :] * conv_w_ref[0:1, :]
            + ext_conv[CONV_HALO - 1:CONV_HALO - 1 + tm, :] * conv_w_ref[1:2, :]
            + cv * conv_w_ref[2:3, :])
    ext_conv[0:CONV_HALO, :] = ext_conv[tm:tm + CONV_HALO, :]
    branch_b = _dot((gb * conv).astype(BF16), w_co_ref[...])
    mixed = jnp.concatenate(mixed, axis=1) * pool_scale_ref[...]
    branch_a = _dot(mixed.astype(BF16), w_po_ref[...])

    merged = (_sigmoid(pre_a + b_gate_ref[:, 0:D_MODEL]) * branch_a
              + _sigmoid(pre_b + b_gate_ref[:, D_MODEL:2 * D_MODEL]) * branch_b)
    x1 = x + _dot(merged.astype(BF16), w_o_ref[...])

    h2 = _rms(x1, g_ffn_ref[...])
    h2p_ref[...] = _pack_pair(h2[:, 0:HALF], h2[:, HALF:D_MODEL])
    h2b = h2.astype(BF16)

    nt = (((1,), (1,)), ((), ()))
    parts = lax.dot_general(wr_ref[...], h2b, nt, preferred_element_type=F32)
    logits = parts[0:N_EXPERTS, :] + parts[N_EXPERTS:2 * N_EXPERTS, :]
    sg = _dot(h2b, s_gate_ref[...])
    su = _dot(h2b, s_up_ref[...])
    scores = jax.nn.sigmoid(logits)
    sel = scores + rbias_ref[...]
    eidx = lax.broadcasted_iota(jnp.int32, (N_EXPERTS, tm), 0).astype(F32)
    e_rows, w_rows = [], []
    mask = jnp.zeros((N_EXPERTS, tm), F32)
    for _ in range(TOP_K):
        m = jnp.max(sel, axis=0, keepdims=True)
        ek = jnp.min(jnp.where(sel == m, eidx, float(N_EXPERTS)), axis=0, keepdims=True)
        oh = eidx == ek
        w_rows.append(jnp.sum(jnp.where(oh, scores, 0.0), axis=0, keepdims=True))
        e_rows.append(ek)
        mask = mask + oh.astype(F32)
        sel = jnp.where(oh, -jnp.inf, sel)

    shared = _dot((sg * _sigmoid(sg) * su).astype(BF16), s_down_ref[...])
    xres_ref[...] = x1 + shared

    wsum = w_rows[0]
    for k in range(1, TOP_K):
        wsum = wsum + w_rows[k]

    before = _dot(mask.astype(BF16), tri[...]) + cnt_carry[...]
    for k in range(TOP_K):
        oh = eidx == e_rows[k]
        rank_ref[k:k + 1, :] = jnp.sum(jnp.where(oh, before, 0.0), axis=0,
                                       keepdims=True).astype(jnp.int32)
        idx_ref[k:k + 1, :] = e_rows[k].astype(jnp.int32)
        wsel_ref[k:k + 1, :] = w_rows[k] / wsum * ROUTED_SCALE
    total = cnt_carry[...] + jnp.sum(mask, axis=1, keepdims=True)
    cnt_carry[...] = total
    counts_ref[...] = total.astype(jnp.int32)


def _mixer_router(x2d, tok0, t_tok, seq_len, g_mix, w_in, b_gate, w_grp, pool_scale, w_po, conv_w,
                  w_co, w_o, g_ffn, wr, rbias, s_gate, s_up, s_down):
    tm = TM_MIX
    n_seq_tiles = seq_len // tm
    off = tok0 // tm
    const = lambda shape: pl.BlockSpec(shape, lambda i: (0,) * len(shape),
                                       pipeline_mode=pl.Buffered(1))
    x_blk = pl.BlockSpec((tm, D_MODEL), lambda i: (i + off, 0))
    row_blk = pl.BlockSpec((tm, D_MODEL), lambda i: (i, 0))
    half_blk = pl.BlockSpec((tm, HALF), lambda i: (i, 0))
    slot_blk = pl.BlockSpec((TOP_K, tm), lambda i: (0, i))
    return pl.pallas_call(
        functools.partial(_mixer_router_kernel, n_seq_tiles),
        grid=(t_tok // tm,),
        in_specs=[x_blk, const(g_mix.shape), const(w_in.shape), const(b_gate.shape),
                  const(w_grp.shape), const(pool_scale.shape), const(w_po.shape),
                  const(conv_w.shape), const(w_co.shape), const(w_o.shape), const(g_ffn.shape),
                  const(wr.shape), const(rbias.shape),
                  const(s_gate.shape), const(s_up.shape), const(s_down.shape)],
        out_specs=[row_blk, half_blk, slot_blk, slot_blk, slot_blk,
                   pl.BlockSpec((N_EXPERTS, 1), lambda i: (0, 0))],
        out_shape=[jax.ShapeDtypeStruct((t_tok, D_MODEL), F32),
                   jax.ShapeDtypeStruct((t_tok, HALF), U32),
                   jax.ShapeDtypeStruct((TOP_K, t_tok), jnp.int32),
                   jax.ShapeDtypeStruct((TOP_K, t_tok), F32),
                   jax.ShapeDtypeStruct((TOP_K, t_tok), jnp.int32),
                   jax.ShapeDtypeStruct((N_EXPERTS, 1), jnp.int32)],
        scratch_shapes=[pltpu.VMEM((POOL_HALO + tm, POOL_WIDTH), F32),
                        pltpu.VMEM((CONV_HALO + tm, CONV_WIDTH), F32),
                        pltpu.VMEM((N_EXPERTS, 1), F32),
                        pltpu.VMEM((tm, tm), BF16)],
        compiler_params=pltpu.CompilerParams(dimension_semantics=("arbitrary",),
                                             vmem_limit_bytes=VMEM_LIMIT),
        name="mixer_router",
    )(x2d, g_mix, w_in, b_gate, w_grp, pool_scale, w_po, conv_w, w_co, w_o, g_ffn,
      wr, rbias, s_gate, s_up, s_down)


def _dest_kernel(idx_ref, rank_ref, pad_start_ref, dest_ref):
    tm = idx_ref.shape[1]
    eidx = lax.broadcasted_iota(jnp.int32, (N_EXPERTS, tm), 0)
    ps = pad_start_ref[...].astype(F32)
    for k in range(TOP_K):
        oh = eidx == idx_ref[k:k + 1, :]
        start = jnp.sum(jnp.where(oh, ps, 0.0), axis=0, keepdims=True)
        dest_ref[k:k + 1, :] = start.astype(jnp.int32) + rank_ref[k:k + 1, :]


def _dest(idx_t, rank_t, pad_start):
    t_tok = idx_t.shape[1]
    slot_blk = pl.BlockSpec((TOP_K, TM_DEST), lambda i: (0, i))
    return pl.pallas_call(
        _dest_kernel,
        grid=(t_tok // TM_DEST,),
        in_specs=[slot_blk, slot_blk, pl.BlockSpec((N_EXPERTS, 1), lambda i: (0, 0))],
        out_specs=slot_blk,
        out_shape=jax.ShapeDtypeStruct((TOP_K, t_tok), jnp.int32),
        compiler_params=pltpu.CompilerParams(dimension_semantics=("arbitrary",)),
        name="dest",
    )(idx_t, rank_t, pad_start)


def _sc_mesh():
    return plsc.VectorSubcoreMesh(core_axis_name="c", subcore_axis_name="s")


def _sc_worker_id():
    return lax.axis_index("s") * SC_CORES + lax.axis_index("c")


def _dispatch(h2p, dest_t, zero_rows, n_rows):
    t_tok, width = h2p.shape
    per_w = t_tok // SC_WORKERS
    n_chunks = per_w // SC_CHUNK
    z_chunks = zero_rows.shape[0] // (SC_WORKERS * SC_CHUNK)
    dest_w = (dest_t.reshape(TOP_K, SC_WORKERS, n_chunks, SC_CHUNK)
              .transpose(1, 2, 0, 3).reshape(SC_WORKERS * n_chunks * TOP_K, SC_CHUNK))
    zero_w = zero_rows.reshape(SC_WORKERS * z_chunks, SC_CHUNK)

    @functools.partial(
        pl.kernel, mesh=_sc_mesh(),
        out_type=jax.ShapeDtypeStruct((n_rows, width), h2p.dtype),
        scratch_types=[pltpu.VMEM((n_chunks * TOP_K, SC_CHUNK), jnp.int32),
                       pltpu.VMEM((z_chunks, SC_CHUNK), jnp.int32),
                       pltpu.VMEM((2, SC_CHUNK, width), h2p.dtype),
                       pltpu.VMEM((SC_CHUNK, width), h2p.dtype),
                       pltpu.SemaphoreType.DMA, pltpu.SemaphoreType.DMA,
                       pltpu.SemaphoreType.DMA],
        name="dispatch",
    )
    def k(h2p_hbm, dest_hbm, zidx_hbm, zsrc_hbm, xs_hbm, idx_v, zidx_v, rows_v, zero_v,
          gsem, wsem, zsem):
        wid = _sc_worker_id()
        base = wid * per_w
        pltpu.sync_copy(dest_hbm.at[pl.ds(wid * n_chunks * TOP_K, n_chunks * TOP_K)], idx_v)
        pltpu.sync_copy(zidx_hbm.at[pl.ds(wid * z_chunks, z_chunks)], zidx_v)
        pltpu.sync_copy(zsrc_hbm, zero_v)

        def zput(j):
            return pltpu.make_async_copy(zero_v, xs_hbm.at[zidx_v.at[j]], zsem)

        for j in range(z_chunks):
            zput(j).start()

        def get(j, slot):
            return pltpu.make_async_copy(h2p_hbm.at[pl.ds(base + j * SC_CHUNK, SC_CHUNK)],
                                         rows_v.at[slot], gsem)

        def put(j, slot, kk):
            return pltpu.make_async_copy(rows_v.at[slot], xs_hbm.at[idx_v.at[j * TOP_K + kk]], wsem)

        get(0, 0).start()

        @pl.loop(0, n_chunks, step=2)
        def _(j):
            for b in range(2):
                jj = j + b
                get(jj, b).wait()

                @pl.when(jj >= 1)
                def _():
                    for kk in range(TOP_K):
                        put(jj - 1, 1 - b, kk).wait()

                @pl.when(jj + 1 < n_chunks)
                def _():
                    get(jj + 1, 1 - b).start()
                for kk in range(TOP_K):
                    put(jj, b, kk).start()

        for kk in range(TOP_K):
            put(n_chunks - 1, (n_chunks - 1) % 2, kk).wait()
        for j in range(z_chunks):
            zput(j).wait()

    return k(h2p, dest_w, zero_w, jnp.zeros((SC_CHUNK, width), h2p.dtype))


def _regroup(ys, dest_t):
    width = ys.shape[1]
    n_slots, t_tok = dest_t.shape
    m_rows = n_slots * t_tok
    per_w = m_rows // SC_WORKERS
    n_chunks = per_w // SC_CHUNK
    idx2 = dest_t.reshape(SC_WORKERS * n_chunks, SC_CHUNK)

    @functools.partial(
        pl.kernel, mesh=_sc_mesh(),
        out_type=jax.ShapeDtypeStruct((m_rows, width), ys.dtype),
        scratch_types=[pltpu.VMEM((n_chunks, SC_CHUNK), jnp.int32),
                       pltpu.VMEM((2, SC_CHUNK, width), ys.dtype),
                       pltpu.SemaphoreType.DMA, pltpu.SemaphoreType.DMA],
        name="regroup",
    )
    def k(ys_hbm, idx_hbm, out_hbm, idx_v, rows_v, gsem, wsem):
        wid = _sc_worker_id()
        base = wid * per_w
        pltpu.sync_copy(idx_hbm.at[pl.ds(wid * n_chunks, n_chunks)], idx_v)

        def get(j, slot):
            return pltpu.make_async_copy(ys_hbm.at[idx_v.at[j]], rows_v.at[slot], gsem)

        def put(j, slot):
            return pltpu.make_async_copy(rows_v.at[slot],
                                         out_hbm.at[pl.ds(base + j * SC_CHUNK, SC_CHUNK)], wsem)

        get(0, 0).start()

        @pl.loop(0, n_chunks, step=2)
        def _(j):
            for b in range(2):
                jj = j + b
                get(jj, b).wait()

                @pl.when(jj >= 1)
                def _():
                    put(jj - 1, 1 - b).wait()

                @pl.when(jj + 1 < n_chunks)
                def _():
                    get(jj + 1, 1 - b).start()
                put(jj, b).start()

        put(n_chunks - 1, (n_chunks - 1) % 2).wait()

    return k(ys, idx2).reshape(n_slots, t_tok, width)


def _experts_kernel(n_blk, blk_e_ref, next_e_ref, n_used_ref,
                    xs_hbm, eg_hbm, eu_hbm, ed_hbm, ys_hbm,
                    xbuf, ybuf, hid, stg_g, stg_u, stg_d, wg, wu, wd, xsem, ysem, wsem):
    n = n_used_ref[0]

    def ring(b):
        return jnp.bitwise_and(b, ROW_RING - 1)

    def x_copy(b):
        return pltpu.make_async_copy(xs_hbm.at[pl.ds(pl.multiple_of(b * ROW_BLOCK, ROW_BLOCK),
                                                     ROW_BLOCK)], xbuf.at[ring(b)], xsem.at[ring(b)])

    def y_copy(b):
        return pltpu.make_async_copy(ybuf.at[ring(b)],
                                     ys_hbm.at[pl.ds(pl.multiple_of(b * ROW_BLOCK, ROW_BLOCK),
                                                     ROW_BLOCK)], ysem.at[ring(b)])

    def w_copies(e):
        return (pltpu.make_async_copy(eg_hbm.at[e], stg_g, wsem.at[0]),
                pltpu.make_async_copy(eu_hbm.at[e], stg_u, wsem.at[1]),
                pltpu.make_async_copy(ed_hbm.at[e], stg_d, wsem.at[2]))

    def switch_expert(e, wslot):
        for cp in w_copies(e):
            cp.wait()
        wg[wslot] = stg_g[...].astype(BF16)
        wu[wslot] = stg_u[...].astype(BF16)
        wd[wslot] = stg_d[...].astype(BF16)
        nxt = next_e_ref[e]

        @pl.when(nxt >= 0)
        def _():
            for cp in w_copies(nxt):
                cp.start()

    def gate_up(b, wslot):
        xa, xb = _unpack_pair(xbuf[ring(b)])
        xb16 = jnp.concatenate([xa, xb], axis=1).astype(BF16)
        g = _dot(xb16, wg[wslot])
        up = _dot(xb16, wu[wslot])
        hid[jnp.bitwise_and(b, 1)] = (g * _sigmoid(g) * up).astype(BF16)

    def down(b, wslot):
        y = _dot(hid[jnp.bitwise_and(b, 1)], wd[wslot])
        ybuf[ring(b)] = _pack_pair(y[:, 0:HALF], y[:, HALF:D_MODEL])

    e0 = blk_e_ref[0]
    for cp in w_copies(e0):
        cp.start()
    for j in range(ROW_RING):
        @pl.when(j < n)
        def _():
            x_copy(j).start()
    switch_expert(e0, 0)
    x_copy(0).wait()
    gate_up(0, 0)

    def body(b, wslot_prev):
        e = blk_e_ref[b]
        first = e != blk_e_ref[b - 1]
        wslot = jnp.where(first, 1 - wslot_prev, wslot_prev)

        @pl.when(first)
        def _():
            switch_expert(e, wslot)

        x_copy(b).wait()

        @pl.when(b + ROW_RING - 1 < n)
        def _():
            x_copy(b + ROW_RING - 1).start()

        @pl.when(b >= ROW_RING + 1)
        def _():
            y_copy(b - 1 - ROW_RING).wait()

        down(b - 1, wslot_prev)
        gate_up(b, wslot)
        y_copy(b - 1).start()
        return wslot

    wslot_last = lax.fori_loop(1, n, body, jnp.int32(0))

    last = n - 1

    @pl.when(last >= ROW_RING)
    def _():
        y_copy(last - ROW_RING).wait()
    down(last, wslot_last)
    y_copy(last).start()
    for j in range(ROW_RING - 1, -1, -1):
        @pl.when(last - j >= 0)
        def _():
            y_copy(last - j).wait()

    ybuf[0] = jnp.zeros((ROW_BLOCK, HALF), U32)

    def zero_tail(b, c):
        cp = pltpu.make_async_copy(ybuf.at[0],
                                   ys_hbm.at[pl.ds(pl.multiple_of(b * ROW_BLOCK, ROW_BLOCK),
                                                   ROW_BLOCK)], ysem.at[0])
        cp.start()
        cp.wait()
        return c
    lax.fori_loop(n, n_blk, zero_tail, 0)


def _experts(blk_e, next_e, n_used, xs, e_gate, e_up, e_down):
    n_rows = xs.shape[0]
    n_blk = n_rows // ROW_BLOCK
    any_spec = pl.BlockSpec(memory_space=pl.ANY)
    grid_spec = pltpu.PrefetchScalarGridSpec(
        num_scalar_prefetch=3,
        grid=(1,),
        in_specs=[any_spec, any_spec, any_spec, any_spec],
        out_specs=any_spec,
        scratch_shapes=[pltpu.VMEM((ROW_RING, ROW_BLOCK, HALF), U32),
                        pltpu.VMEM((ROW_RING, ROW_BLOCK, HALF), U32),
                        pltpu.VMEM((2, ROW_BLOCK, EXPERT_HIDDEN), BF16),
                        pltpu.VMEM((D_MODEL, EXPERT_HIDDEN), F32),
                        pltpu.VMEM((D_MODEL, EXPERT_HIDDEN), F32),
                        pltpu.VMEM((EXPERT_HIDDEN, D_MODEL), F32),
                        pltpu.VMEM((2, D_MODEL, EXPERT_HIDDEN), BF16),
                        pltpu.VMEM((2, D_MODEL, EXPERT_HIDDEN), BF16),
                        pltpu.VMEM((2, EXPERT_HIDDEN, D_MODEL), BF16),
                        pltpu.SemaphoreType.DMA((ROW_RING,)),
                        pltpu.SemaphoreType.DMA((ROW_RING,)),
                        pltpu.SemaphoreType.DMA((3,))],
    )
    return pl.pallas_call(
        functools.partial(_experts_kernel, n_blk),
        grid_spec=grid_spec,
        out_shape=jax.ShapeDtypeStruct((n_rows, HALF), U32),
        compiler_params=pltpu.CompilerParams(dimension_semantics=("arbitrary",)),
        name="experts",
    )(blk_e, next_e, n_used, xs, e_gate, e_up, e_down)


def _combine_kernel(yg_ref, wsel_ref, xres_ref, g_final_ref, *rest):
    out_ref = rest[-1]
    xres = xres_ref[...]
    acc_a = xres[:, 0:HALF]
    acc_b = xres[:, HALF:D_MODEL]
    wsel = wsel_ref[...]
    for k in range(TOP_K):
        ya, yb = _unpack_pair(yg_ref[k])
        wk = wsel[:, k:k + 1]
        acc_a = acc_a + ya * wk
        acc_b = acc_b + yb * wk
    out_ref[...] = _rms(jnp.concatenate([acc_a, acc_b], axis=1), g_final_ref[...])


def _combine(yg, wsel_tk, xres, g_final, out_prev, tok_local, tok_out, t_out):
    tm = TM_COMB
    steps = yg.shape[1] // tm
    off = tok_local // tm
    off_out = tok_out // tm
    in_specs = [pl.BlockSpec((TOP_K, tm, HALF), lambda i: (0, i, 0)),
                pl.BlockSpec((tm, TOP_K), lambda i: (i + off, 0)),
                pl.BlockSpec((tm, D_MODEL), lambda i: (i + off, 0)),
                pl.BlockSpec((1, D_MODEL), lambda i: (0, 0))]
    args = [yg, wsel_tk, xres, g_final]
    aliases = {}
    if out_prev is not None:
        in_specs.append(pl.BlockSpec(memory_space=pl.ANY))
        args.append(out_prev)
        aliases = {4: 0}
    return pl.pallas_call(
        _combine_kernel,
        grid=(steps,),
        in_specs=in_specs,
        out_specs=pl.BlockSpec((tm, D_MODEL), lambda i: (i + off_out, 0)),
        out_shape=jax.ShapeDtypeStruct((t_out, D_MODEL), F32),
        input_output_aliases=aliases,
        compiler_params=pltpu.CompilerParams(dimension_semantics=("arbitrary",),
                                             vmem_limit_bytes=VMEM_LIMIT),
        name="combine",
    )(*args)


def kernel(x, g_mix, w_in, b_gate, w_pool_group, pool_scale, w_pool_out, conv_w, w_conv_out, w_o,
           g_ffn, w_router, router_bias, e_gate, e_up, e_down, s_gate, s_up, s_down, g_final):
    b, s, d = x.shape
    t_tok = b * s
    t_split = t_tok // N_SPLITS
    t_chunk = t_split // N_COMB_CHUNKS
    n_pad = N_EXPERTS * ROW_BLOCK
    assert d == D_MODEL and s % TM_MIX == 0 and TM_MIX >= POOL_HALO
    assert b % N_SPLITS == 0 and t_split % TM_DEST == 0 and t_chunk % TM_COMB == 0
    assert (t_chunk * TOP_K) % (2 * SC_WORKERS * SC_CHUNK) == 0
    assert t_split % (2 * SC_WORKERS * SC_CHUNK) == 0 and n_pad % (SC_WORKERS * SC_CHUNK) == 0

    row = lambda a: a.reshape(1, -1)
    wr_t = w_router.T.astype(F32)
    wr_hi = wr_t.astype(BF16)
    wr = jnp.concatenate([wr_hi, (wr_t - wr_hi.astype(F32)).astype(BF16)], axis=0)
    mixer_weights = (row(g_mix), w_in.astype(BF16), row(b_gate), w_pool_group.astype(BF16),
                     row(pool_scale), w_pool_out.astype(BF16), conv_w, w_conv_out.astype(BF16),
                     w_o.astype(BF16), row(g_ffn), wr, router_bias.astype(F32).reshape(N_EXPERTS, 1),
                     s_gate.astype(BF16), s_up.astype(BF16), s_down.astype(BF16))
    x2d = x.reshape(t_tok, d)
    e_ids = jnp.arange(N_EXPERTS, dtype=jnp.int32)
    n_rows = t_split * TOP_K + n_pad
    n_blk = n_rows // ROW_BLOCK
    blk_row0 = jnp.arange(n_blk, dtype=jnp.int32) * ROW_BLOCK
    j_pad = jnp.arange(n_pad, dtype=jnp.int32)

    out = None
    for sp in range(N_SPLITS):
        tok0 = sp * t_split
        xres, h2p, idx_t, wsel_t, rank_t, counts = _mixer_router(x2d, tok0, t_split, s,
                                                                 *mixer_weights)

        counts = counts.reshape(N_EXPERTS)
        padded = (counts + ROW_BLOCK - 1) // ROW_BLOCK * ROW_BLOCK
        pad_end = jnp.cumsum(padded)
        pad_start = pad_end - padded
        n_used = (pad_end[-1:] // ROW_BLOCK).astype(jnp.int32)
        blk_e = jnp.minimum(jnp.sum(pad_end[None, :] <= blk_row0[:, None], axis=1),
                            N_EXPERTS - 1).astype(jnp.int32)
        later_used = jnp.logical_and(e_ids[None, :] > e_ids[:, None], padded[None, :] > 0)
        next_e = jnp.min(jnp.where(later_used, e_ids[None, :], N_EXPERTS), axis=1)
        next_e = jnp.where(next_e < N_EXPERTS, next_e, -1).astype(jnp.int32)
        gap = padded - counts
        gap_before = jnp.cumsum(gap) - gap
        zero_rows = (j_pad + jnp.sum(jnp.where(gap_before[None, :] <= j_pad[:, None],
                                               counts[None, :], 0), axis=1)).astype(jnp.int32)

        dest_t = _dest(idx_t, rank_t, pad_start.reshape(N_EXPERTS, 1).astype(jnp.int32))
        xs = _dispatch(h2p, dest_t, zero_rows, n_rows)
        ys = _experts(blk_e, next_e, n_used, xs, e_gate, e_up, e_down)
        wsel_tk = wsel_t.T
        for c in range(N_COMB_CHUNKS):
            yg = _regroup(ys, dest_t[:, c * t_chunk:(c + 1) * t_chunk])
            out = _combine(yg, wsel_tk, xres, row(g_final), out, c * t_chunk, tok0 + c * t_chunk,
                           t_tok)
    return out.reshape(b, s, d)
```

```python
import functools

import jax
import jax.numpy as jnp
from jax import lax
from jax.experimental import pallas as pl
from jax.experimental.pallas import tpu as pltpu
from jax.experimental.pallas import tpu_sc as plsc

D_MODEL = 1024
HALF = D_MODEL // 2
POOL_WIDTH = 512
N_POOL_GROUPS = 4
POOL_GROUP = 128
POOL_WINDOWS = (2, 4, 8, 16)
CONV_WIDTH = 512
N_EXPERTS = 64
TOP_K = 8
EXPERT_HIDDEN = 256
SHARED_HIDDEN = 256
ROUTED_SCALE = 2.5
EPS = 1e-6

POOL_HALO = 16
CONV_HALO = 8
TM_MIX = 512
TM_DEST = 2048
ROW_BLOCK = 256
ROW_RING = 8
TM_COMB = 512
N_SPLITS = 1
N_COMB_CHUNKS = 4
VMEM_LIMIT = 56 * 1024 * 1024

SC_CORES = 2
SC_SUBCORES = 16
SC_WORKERS = SC_CORES * SC_SUBCORES
SC_CHUNK = 64

BF16 = jnp.bfloat16
F32 = jnp.float32
U32 = jnp.uint32


def _rms(x, g):
    r = lax.rsqrt(jnp.mean(x * x, axis=-1, keepdims=True) + EPS)
    return (x * r) * g


def _dot(a, b):
    return jnp.dot(a, b, preferred_element_type=F32)


def _sigmoid(z):
    return 0.5 * jnp.tanh(0.5 * z) + 0.5


def _pack_pair(a, b):
    ra = lax.bitcast_convert_type(a.astype(BF16).astype(F32), U32)
    rb = lax.bitcast_convert_type(b.astype(BF16).astype(F32), U32)
    return ra | (rb >> 16)


def _unpack_pair(w):
    a = lax.bitcast_convert_type(w & jnp.uint32(0xFFFF0000), F32)
    b = lax.bitcast_convert_type(w << 16, F32)
    return a, b


def _mixer_router_kernel(n_seq_tiles,
                         x_ref, g_mix_ref, w_in_ref, b_gate_ref, w_grp_ref, pool_scale_ref,
                         w_po_ref, conv_w_ref, w_co_ref, w_o_ref, g_ffn_ref,
                         wr_ref, rbias_ref, s_gate_ref, s_up_ref, s_down_ref,
                         xres_ref, h2p_ref, idx_ref, wsel_ref, rank_ref, counts_ref,
                         ext_pool, ext_conv, cnt_carry, tri):
    tm = x_ref.shape[0]
    i = pl.program_id(0)
    st = i % n_seq_tiles

    @pl.when(i == 0)
    def _():
        r = lax.broadcasted_iota(jnp.int32, (tm, tm), 0)
        c = lax.broadcasted_iota(jnp.int32, (tm, tm), 1)
        tri[...] = (r < c).astype(BF16)
        cnt_carry[...] = jnp.zeros_like(cnt_carry)

    @pl.when(st == 0)
    def _():
        ext_pool[0:POOL_HALO, :] = jnp.zeros((POOL_HALO, POOL_WIDTH), F32)
        ext_conv[0:CONV_HALO, :] = jnp.zeros((CONV_HALO, CONV_WIDTH), F32)

    x = x_ref[...]
    hb = _rms(x, g_mix_ref[...]).astype(BF16)

    o0 = POOL_WIDTH
    o1 = o0 + CONV_WIDTH
    o2 = o1 + CONV_WIDTH
    o3 = o2 + CONV_WIDTH

    u = _dot(hb, w_in_ref[:, 0:o0])
    ext_pool[POOL_HALO:POOL_HALO + tm, :] = u
    gc = _dot(hb, w_in_ref[:, o1:o2])
    v = _dot(hb, w_in_ref[:, o2:o3])
    pre_a = _dot(hb, w_in_ref[:, o3:o3 + D_MODEL])

    t_glob = st * tm + lax.broadcasted_iota(jnp.int32, (tm, 1), 0)
    mixed = []
    for gi, w in enumerate(POOL_WINDOWS):
        cols = slice(gi * POOL_GROUP, (gi + 1) * POOL_GROUP)
        ug = u[:, cols]
        acc = ug
        for j in range(1, w):
            acc = acc + ext_pool[POOL_HALO - j:POOL_HALO - j + tm, cols]
        cnt = jnp.minimum(t_glob + 1, w).astype(F32)
        pooled = acc * (1.0 / cnt) - ug
        mixed.append(_dot(pooled.astype(BF16), w_grp_ref[gi]))
    ext_pool[0:POOL_HALO, :] = ext_pool[tm:tm + POOL_HALO, :]
    pre_b = _dot(hb, w_in_ref[:, o3 + D_MODEL:o3 + 2 * D_MODEL])
    gb = _dot(hb, w_in_ref[:, o0:o1])

    cv = gc * v
    ext_conv[CONV_HALO:CONV_HALO + tm, :] = cv
    conv = (ext_conv[CONV_HALO - 2:CONV_HALO - 2 + tm, :] * conv_w_ref[0:1, :]
            + ext_conv[CONV_HALO - 1:CONV_HALO - 1 + tm, :] * conv_w_ref[1:2, :]
            + cv * conv_w_ref[2:3, :])
    ext_conv[0:CONV_HALO, :] = ext_conv[tm:tm + CONV_HALO, :]
    branch_b = _dot((gb * conv).astype(BF16), w_co_ref[...])
    mixed = jnp.concatenate(mixed, axis=1) * pool_scale_ref[...]
    branch_a = _dot(mixed.astype(BF16), w_po_ref[...])

    merged = (_sigmoid(pre_a + b_gate_ref[:, 0:D_MODEL]) * branch_a
              + _sigmoid(pre_b + b_gate_ref[:, D_MODEL:2 * D_MODEL]) * branch_b)
    x1 = x + _dot(merged.astype(BF16), w_o_ref[...])

    h2 = _rms(x1, g_ffn_ref[...])
    h2p_ref[...] = _pack_pair(h2[:, 0:HALF], h2[:, HALF:D_MODEL])
    h2b = h2.astype(BF16)

    nt = (((1,), (1,)), ((), ()))
    parts = lax.dot_general(wr_ref[...], h2b, nt, preferred_element_type=F32)
    logits = parts[0:N_EXPERTS, :] + parts[N_EXPERTS:2 * N_EXPERTS, :]
    sg = _dot(h2b, s_gate_ref[...])
    su = _dot(h2b, s_up_ref[...])
    scores = jax.nn.sigmoid(logits)
    sel = scores + rbias_ref[...]
    eidx = lax.broadcasted_iota(jnp.int32, (N_EXPERTS, tm), 0).astype(F32)
    e_rows, w_rows = [], []
    mask = jnp.zeros((N_EXPERTS, tm), F32)
    for _ in range(TOP_K):
        m = jnp.max(sel, axis=0, keepdims=True)
        ek = jnp.min(jnp.where(sel == m, eidx, float(N_EXPERTS)), axis=0, keepdims=True)
        oh = eidx == ek
        w_rows.append(jnp.sum(jnp.where(oh, scores, 0.0), axis=0, keepdims=True))
        e_rows.append(ek)
        mask = mask + oh.astype(F32)
        sel = jnp.where(oh, -jnp.inf, sel)

    shared = _dot((sg * _sigmoid(sg) * su).astype(BF16), s_down_ref[...])
    xres_ref[...] = x1 + shared

    wsum = w_rows[0]
    for k in range(1, TOP_K):
        wsum = wsum + w_rows[k]

    before = _dot(mask.astype(BF16), tri[...]) + cnt_carry[...]
    for k in range(TOP_K):
        oh = eidx == e_rows[k]
        rank_ref[k:k + 1, :] = jnp.sum(jnp.where(oh, before, 0.0), axis=0,
                                       keepdims=True).astype(jnp.int32)
        idx_ref[k:k + 1, :] = e_rows[k].astype(jnp.int32)
        wsel_ref[k:k + 1, :] = w_rows[k] / wsum * ROUTED_SCALE
    total = cnt_carry[...] + jnp.sum(mask, axis=1, keepdims=True)
    cnt_carry[...] = total
    counts_ref[...] = total.astype(jnp.int32)


def _mixer_router(x2d, tok0, t_tok, seq_len, g_mix, w_in, b_gate, w_grp, pool_scale, w_po, conv_w,
                  w_co, w_o, g_ffn, wr, rbias, s_gate, s_up, s_down):
    tm = TM_MIX
    n_seq_tiles = seq_len // tm
    off = tok0 // tm
    const = lambda shape: pl.BlockSpec(shape, lambda i: (0,) * len(shape),
                                       pipeline_mode=pl.Buffered(1))
    x_blk = pl.BlockSpec((tm, D_MODEL), lambda i: (i + off, 0))
    row_blk = pl.BlockSpec((tm, D_MODEL), lambda i: (i, 0))
    half_blk = pl.BlockSpec((tm, HALF), lambda i: (i, 0))
    slot_blk = pl.BlockSpec((TOP_K, tm), lambda i: (0, i))
    return pl.pallas_call(
        functools.partial(_mixer_router_kernel, n_seq_tiles),
        grid=(t_tok // tm,),
        in_specs=[x_blk, const(g_mix.shape), const(w_in.shape), const(b_gate.shape),
                  const(w_grp.shape), const(pool_scale.shape), const(w_po.shape),
                  const(conv_w.shape), const(w_co.shape), const(w_o.shape), const(g_ffn.shape),
                  const(wr.shape), const(rbias.shape),
                  const(s_gate.shape), const(s_up.shape), const(s_down.shape)],
        out_specs=[row_blk, half_blk, slot_blk, slot_blk, slot_blk,
                   pl.BlockSpec((N_EXPERTS, 1), lambda i: (0, 0))],
        out_shape=[jax.ShapeDtypeStruct((t_tok, D_MODEL), F32),
                   jax.ShapeDtypeStruct((t_tok, HALF), U32),
                   jax.ShapeDtypeStruct((TOP_K, t_tok), jnp.int32),
                   jax.ShapeDtypeStruct((TOP_K, t_tok), F32),
                   jax.ShapeDtypeStruct((TOP_K, t_tok), jnp.int32),
                   jax.ShapeDtypeStruct((N_EXPERTS, 1), jnp.int32)],
        scratch_shapes=[pltpu.VMEM((POOL_HALO + tm, POOL_WIDTH), F32),
                        pltpu.VMEM((CONV_HALO + tm, CONV_WIDTH), F32),
                        pltpu.VMEM((N_EXPERTS, 1), F32),
                        pltpu.VMEM((tm, tm), BF16)],
        compiler_params=pltpu.CompilerParams(dimension_semantics=("arbitrary",),
                                             vmem_limit_bytes=VMEM_LIMIT),
        name="mixer_router",
    )(x2d, g_mix, w_in, b_gate, w_grp, pool_scale, w_po, conv_w, w_co, w_o, g_ffn,
      wr, rbias, s_gate, s_up, s_down)


def _dest_kernel(idx_ref, rank_ref, pad_start_ref, dest_ref):
    tm = idx_ref.shape[1]
    eidx = lax.broadcasted_iota(jnp.int32, (N_EXPERTS, tm), 0)
    ps = pad_start_ref[...].astype(F32)
    for k in range(TOP_K):
        oh = eidx == idx_ref[k:k + 1, :]
        start = jnp.sum(jnp.where(oh, ps, 0.0), axis=0, keepdims=True)
        dest_ref[k:k + 1, :] = start.astype(jnp.int32) + rank_ref[k:k + 1, :]


def _dest(idx_t, rank_t, pad_start):
    t_tok = idx_t.shape[1]
    slot_blk = pl.BlockSpec((TOP_K, TM_DEST), lambda i: (0, i))
    return pl.pallas_call(
        _dest_kernel,
        grid=(t_tok // TM_DEST,),
        in_specs=[slot_blk, slot_blk, pl.BlockSpec((N_EXPERTS, 1), lambda i: (0, 0))],
        out_specs=slot_blk,
        out_shape=jax.ShapeDtypeStruct((TOP_K, t_tok), jnp.int32),
        compiler_params=pltpu.CompilerParams(dimension_semantics=("arbitrary",)),
        name="dest",
    )(idx_t, rank_t, pad_start)


def _sc_mesh():
    return plsc.VectorSubcoreMesh(core_axis_name="c", subcore_axis_name="s")


def _sc_worker_id():
    return lax.axis_index("s") * SC_CORES + lax.axis_index("c")


def _dispatch(h2p, dest_t, zero_rows, n_rows):
    t_tok, width = h2p.shape
    per_w = t_tok // SC_WORKERS
    n_chunks = per_w // SC_CHUNK
    z_chunks = zero_rows.shape[0] // (SC_WORKERS * SC_CHUNK)
    dest_w = (dest_t.reshape(TOP_K, SC_WORKERS, n_chunks, SC_CHUNK)
              .transpose(1, 2, 0, 3).reshape(SC_WORKERS * n_chunks * TOP_K, SC_CHUNK))
    zero_w = zero_rows.reshape(SC_WORKERS * z_chunks, SC_CHUNK)

    @functools.partial(
        pl.kernel, mesh=_sc_mesh(),
        out_type=jax.ShapeDtypeStruct((n_rows, width), h2p.dtype),
        scratch_types=[pltpu.VMEM((n_chunks * TOP_K, SC_CHUNK), jnp.int32),
                       pltpu.VMEM((z_chunks, SC_CHUNK), jnp.int32),
                       pltpu.VMEM((2, SC_CHUNK, width), h2p.dtype),
                       pltpu.VMEM((SC_CHUNK, width), h2p.dtype),
                       pltpu.SemaphoreType.DMA, pltpu.SemaphoreType.DMA,
                       pltpu.SemaphoreType.DMA],
        name="dispatch",
    )
    def k(h2p_hbm, dest_hbm, zidx_hbm, zsrc_hbm, xs_hbm, idx_v, zidx_v, rows_v, zero_v,
          gsem, wsem, zsem):
        wid = _sc_worker_id()
        base = wid * per_w
        pltpu.sync_copy(dest_hbm.at[pl.ds(wid * n_chunks * TOP_K, n_chunks * TOP_K)], idx_v)
        pltpu.sync_copy(zidx_hbm.at[pl.ds(wid * z_chunks, z_chunks)], zidx_v)
        pltpu.sync_copy(zsrc_hbm, zero_v)

        def zput(j):
            return pltpu.make_async_copy(zero_v, xs_hbm.at[zidx_v.at[j]], zsem)

        for j in range(z_chunks):
            zput(j).start()

        def get(j, slot):
            return pltpu.make_async_copy(h2p_hbm.at[pl.ds(base + j * SC_CHUNK, SC_CHUNK)],
                                         rows_v.at[slot], gsem)

        def put(j, slot, kk):
            return pltpu.make_async_copy(rows_v.at[slot], xs_hbm.at[idx_v.at[j * TOP_K + kk]], wsem)

        get(0, 0).start()

        @pl.loop(0, n_chunks, step=2)
        def _(j):
            for b in range(2):
                jj = j + b
                get(jj, b).wait()

                @pl.when(jj >= 1)
                def _():
                    for kk in range(TOP_K):
                        put(jj - 1, 1 - b, kk).wait()

                @pl.when(jj + 1 < n_chunks)
                def _():
                    get(jj + 1, 1 - b).start()
                for kk in range(TOP_K):
                    put(jj, b, kk).start()

        for kk in range(TOP_K):
            put(n_chunks - 1, (n_chunks - 1) % 2, kk).wait()
        for j in range(z_chunks):
            zput(j).wait()

    return k(h2p, dest_w, zero_w, jnp.zeros((SC_CHUNK, width), h2p.dtype))


def _regroup(ys, dest_t):
    width = ys.shape[1]
    n_slots, t_tok = dest_t.shape
    m_rows = n_slots * t_tok
    per_w = m_rows // SC_WORKERS
    n_chunks = per_w // SC_CHUNK
    idx2 = dest_t.reshape(SC_WORKERS * n_chunks, SC_CHUNK)

    @functools.partial(
        pl.kernel, mesh=_sc_mesh(),
        out_type=jax.ShapeDtypeStruct((m_rows, width), ys.dtype),
        scratch_types=[pltpu.VMEM((n_chunks, SC_CHUNK), jnp.int32),
                       pltpu.VMEM((2, SC_CHUNK, width), ys.dtype),
                       pltpu.SemaphoreType.DMA, pltpu.SemaphoreType.DMA],
        name="regroup",
    )
    def k(ys_hbm, idx_hbm, out_hbm, idx_v, rows_v, gsem, wsem):
        wid = _sc_worker_id()
        base = wid * per_w
        pltpu.sync_copy(idx_hbm.at[pl.ds(wid * n_chunks, n_chunks)], idx_v)

        def get(j, slot):
            return pltpu.make_async_copy(ys_hbm.at[idx_v.at[j]], rows_v.at[slot], gsem)

        def put(j, slot):
            return pltpu.make_async_copy(rows_v.at[slot],
                                         out_hbm.at[pl.ds(base + j * SC_CHUNK, SC_CHUNK)], wsem)

        get(0, 0).start()

        @pl.loop(0, n_chunks, step=2)
        def _(j):
            for b in range(2):
                jj = j + b
                get(jj, b).wait()

                @pl.when(jj >= 1)
                def _():
                    put(jj - 1, 1 - b).wait()

                @pl.when(jj + 1 < n_chunks)
                def _():
                    get(jj + 1, 1 - b).start()
                put(jj, b).start()

        put(n_chunks - 1, (n_chunks - 1) % 2).wait()

    return k(ys, idx2).reshape(n_slots, t_tok, width)


def _experts_kernel(n_blk, blk_e_ref, next_e_ref, n_used_ref,
                    xs_hbm, eg_hbm, eu_hbm, ed_hbm, ys_hbm,
                    xbuf, ybuf, hid, stg_g, stg_u, stg_d, wg, wu, wd, xsem, ysem, wsem):
    n = n_used_ref[0]

    def ring(b):
        return jnp.bitwise_and(b, ROW_RING - 1)

    def x_copy(b):
        return pltpu.make_async_copy(xs_hbm.at[pl.ds(pl.multiple_of(b * ROW_BLOCK, ROW_BLOCK),
                                                     ROW_BLOCK)], xbuf.at[ring(b)], xsem.at[ring(b)])

    def y_copy(b):
        return pltpu.make_async_copy(ybuf.at[ring(b)],
                                     ys_hbm.at[pl.ds(pl.multiple_of(b * ROW_BLOCK, ROW_BLOCK),
                                                     ROW_BLOCK)], ysem.at[ring(b)])

    def w_copies(e):
        return (pltpu.make_async_copy(eg_hbm.at[e], stg_g, wsem.at[0]),
                pltpu.make_async_copy(eu_hbm.at[e], stg_u, wsem.at[1]),
                pltpu.make_async_copy(ed_hbm.at[e], stg_d, wsem.at[2]))

    def switch_expert(e, wslot):
        for cp in w_copies(e):
            cp.wait()
        wg[wslot] = stg_g[...].astype(BF16)
        wu[wslot] = stg_u[...].astype(BF16)
        wd[wslot] = stg_d[...].astype(BF16)
        nxt = next_e_ref[e]

        @pl.when(nxt >= 0)
        def _():
            for cp in w_copies(nxt):
                cp.start()

    def gate_up(b, wslot):
        xa, xb = _unpack_pair(xbuf[ring(b)])
        xb16 = jnp.concatenate([xa, xb], axis=1).astype(BF16)
        g = _dot(xb16, wg[wslot])
        up = _dot(xb16, wu[wslot])
        hid[jnp.bitwise_and(b, 1)] = (g * _sigmoid(g) * up).astype(BF16)

    def down(b, wslot):
        y = _dot(hid[jnp.bitwise_and(b, 1)], wd[wslot])
        ybuf[ring(b)] = _pack_pair(y[:, 0:HALF], y[:, HALF:D_MODEL])

    e0 = blk_e_ref[0]
    for cp in w_copies(e0):
        cp.start()
    for j in range(ROW_RING):
        @pl.when(j < n)
        def _():
            x_copy(j).start()
    switch_expert(e0, 0)
    x_copy(0).wait()
    gate_up(0, 0)

    def body(b, wslot_prev):
        e = blk_e_ref[b]
        first = e != blk_e_ref[b - 1]
        wslot = jnp.where(first, 1 - wslot_prev, wslot_prev)

        @pl.when(first)
        def _():
            switch_expert(e, wslot)

        x_copy(b).wait()

        @pl.when(b + ROW_RING - 1 < n)
        def _():
            x_copy(b + ROW_RING - 1).start()

        @pl.when(b >= ROW_RING + 1)
        def _():
            y_copy(b - 1 - ROW_RING).wait()

        down(b - 1, wslot_prev)
        gate_up(b, wslot)
        y_copy(b - 1).start()
        return wslot

    wslot_last = lax.fori_loop(1, n, body, jnp.int32(0))

    last = n - 1

    @pl.when(last >= ROW_RING)
    def _():
        y_copy(last - ROW_RING).wait()
    down(last, wslot_last)
    y_copy(last).start()
    for j in range(ROW_RING - 1, -1, -1):
        @pl.when(last - j >= 0)
        def _():
            y_copy(last - j).wait()

    ybuf[0] = jnp.zeros((ROW_BLOCK, HALF), U32)

    def zero_tail(b, c):
        cp = pltpu.make_async_copy(ybuf.at[0],
                                   ys_hbm.at[pl.ds(pl.multiple_of(b * ROW_BLOCK, ROW_BLOCK),
                                                   ROW_BLOCK)], ysem.at[0])
        cp.start()
        cp.wait()
        return c
    lax.fori_loop(n, n_blk, zero_tail, 0)


def _experts(blk_e, next_e, n_used, xs, e_gate, e_up, e_down):
    n_rows = xs.shape[0]
    n_blk = n_rows // ROW_BLOCK
    any_spec = pl.BlockSpec(memory_space=pl.ANY)
    grid_spec = pltpu.PrefetchScalarGridSpec(
        num_scalar_prefetch=3,
        grid=(1,),
        in_specs=[any_spec, any_spec, any_spec, any_spec],
        out_specs=any_spec,
        scratch_shapes=[pltpu.VMEM((ROW_RING, ROW_BLOCK, HALF), U32),
                        pltpu.VMEM((ROW_RING, ROW_BLOCK, HALF), U32),
                        pltpu.VMEM((2, ROW_BLOCK, EXPERT_HIDDEN), BF16),
                        pltpu.VMEM((D_MODEL, EXPERT_HIDDEN), F32),
                        pltpu.VMEM((D_MODEL, EXPERT_HIDDEN), F32),
                        pltpu.VMEM((EXPERT_HIDDEN, D_MODEL), F32),
                        pltpu.VMEM((2, D_MODEL, EXPERT_HIDDEN), BF16),
                        pltpu.VMEM((2, D_MODEL, EXPERT_HIDDEN), BF16),
                        pltpu.VMEM((2, EXPERT_HIDDEN, D_MODEL), BF16),
                        pltpu.SemaphoreType.DMA((ROW_RING,)),
                        pltpu.SemaphoreType.DMA((ROW_RING,)),
                        pltpu.SemaphoreType.DMA((3,))],
    )
    return pl.pallas_call(
        functools.partial(_experts_kernel, n_blk),
        grid_spec=grid_spec,
        out_shape=jax.ShapeDtypeStruct((n_rows, HALF), U32),
        compiler_params=pltpu.CompilerParams(dimension_semantics=("arbitrary",)),
        name="experts",
    )(blk_e, next_e, n_used, xs, e_gate, e_up, e_down)


def _combine_kernel(yg_ref, wsel_ref, xres_ref, g_final_ref, *rest):
    out_ref = rest[-1]
    xres = xres_ref[...]
    acc_a = xres[:, 0:HALF]
    acc_b = xres[:, HALF:D_MODEL]
    wsel = wsel_ref[...]
    for k in range(TOP_K):
        ya, yb = _unpack_pair(yg_ref[k])
        wk = wsel[:, k:k + 1]
        acc_a = acc_a + ya * wk
        acc_b = acc_b + yb * wk
    out_ref[...] = _rms(jnp.concatenate([acc_a, acc_b], axis=1), g_final_ref[...])


def _combine(yg, wsel_tk, xres, g_final, out_prev, tok_local, tok_out, t_out):
    tm = TM_COMB
    steps = yg.shape[1] // tm
    off = tok_local // tm
    off_out = tok_out // tm
    in_specs = [pl.BlockSpec((TOP_K, tm, HALF), lambda i: (0, i, 0)),
                pl.BlockSpec((tm, TOP_K), lambda i: (i + off, 0)),
                pl.BlockSpec((tm, D_MODEL), lambda i: (i + off, 0)),
                pl.BlockSpec((1, D_MODEL), lambda i: (0, 0))]
    args = [yg, wsel_tk, xres, g_final]
    aliases = {}
    if out_prev is not None:
        in_specs.append(pl.BlockSpec(memory_space=pl.ANY))
        args.append(out_prev)
        aliases = {4: 0}
    return pl.pallas_call(
        _combine_kernel,
        grid=(steps,),
        in_specs=in_specs,
        out_specs=pl.BlockSpec((tm, D_MODEL), lambda i: (i + off_out, 0)),
        out_shape=jax.ShapeDtypeStruct((t_out, D_MODEL), F32),
        input_output_aliases=aliases,
        compiler_params=pltpu.CompilerParams(dimension_semantics=("arbitrary",),
                                             vmem_limit_bytes=VMEM_LIMIT),
        name="combine",
    )(*args)


def kernel(x, g_mix, w_in, b_gate, w_pool_group, pool_scale, w_pool_out, conv_w, w_conv_out, w_o,
           g_ffn, w_router, router_bias, e_gate, e_up, e_down, s_gate, s_up, s_down, g_final):
    b, s, d = x.shape
    t_tok = b * s
    t_split = t_tok // N_SPLITS
    t_chunk = t_split // N_COMB_CHUNKS
    n_pad = N_EXPERTS * ROW_BLOCK
    assert d == D_MODEL and s % TM_MIX == 0 and TM_MIX >= POOL_HALO
    assert b % N_SPLITS == 0 and t_split % TM_DEST == 0 and t_chunk % TM_COMB == 0
    assert (t_chunk * TOP_K) % (2 * SC_WORKERS * SC_CHUNK) == 0
    assert t_split % (2 * SC_WORKERS * SC_CHUNK) == 0 and n_pad % (SC_WORKERS * SC_CHUNK) == 0

    row = lambda a: a.reshape(1, -1)
    wr_t = w_router.T.astype(F32)
    wr_hi = wr_t.astype(BF16)
    wr = jnp.concatenate([wr_hi, (wr_t - wr_hi.astype(F32)).astype(BF16)], axis=0)
    mixer_weights = (row(g_mix), w_in.astype(BF16), row(b_gate), w_pool_group.astype(BF16),
                     row(pool_scale), w_pool_out.astype(BF16), conv_w, w_conv_out.astype(BF16),
                     w_o.astype(BF16), row(g_ffn), wr, router_bias.astype(F32).reshape(N_EXPERTS, 1),
                     s_gate.astype(BF16), s_up.astype(BF16), s_down.astype(BF16))
    x2d = x.reshape(t_tok, d)
    e_ids = jnp.arange(N_EXPERTS, dtype=jnp.int32)
    n_rows = t_split * TOP_K + n_pad
    n_blk = n_rows // ROW_BLOCK
    blk_row0 = jnp.arange(n_blk, dtype=jnp.int32) * ROW_BLOCK
    j_pad = jnp.arange(n_pad, dtype=jnp.int32)

    out = None
    for sp in range(N_SPLITS):
        tok0 = sp * t_split
        xres, h2p, idx_t, wsel_t, rank_t, counts = _mixer_router(x2d, tok0, t_split, s,
                                                                 *mixer_weights)

        counts = counts.reshape(N_EXPERTS)
        padded = (counts + ROW_BLOCK - 1) // ROW_BLOCK * ROW_BLOCK
        pad_end = jnp.cumsum(padded)
        pad_start = pad_end - padded
        n_used = (pad_end[-1:] // ROW_BLOCK).astype(jnp.int32)
        blk_e = jnp.minimum(jnp.sum(pad_end[None, :] <= blk_row0[:, None], axis=1),
                            N_EXPERTS - 1).astype(jnp.int32)
        later_used = jnp.logical_and(e_ids[None, :] > e_ids[:, None], padded[None, :] > 0)
        next_e = jnp.min(jnp.where(later_used, e_ids[None, :], N_EXPERTS), axis=1)
        next_e = jnp.where(next_e < N_EXPERTS, next_e, -1).astype(jnp.int32)
        gap = padded - counts
        gap_before = jnp.cumsum(gap) - gap
        zero_rows = (j_pad + jnp.sum(jnp.where(gap_before[None, :] <= j_pad[:, None],
                                               counts[None, :], 0), axis=1)).astype(jnp.int32)

        dest_t = _dest(idx_t, rank_t, pad_start.reshape(N_EXPERTS, 1).astype(jnp.int32))
        xs = _dispatch(h2p, dest_t, zero_rows, n_rows)
        ys = _experts(blk_e, next_e, n_used, xs, e_gate, e_up, e_down)
        wsel_tk = wsel_t.T
        for c in range(N_COMB_CHUNKS):
            yg = _regroup(ys, dest_t[:, c * t_chunk:(c + 1) * t_chunk])
            out = _combine(yg, wsel_tk, xres, row(g_final), out, c * t_chunk, tok0 + c * t_chunk,
                           t_tok)
    return out.reshape(b, s, d)
```

```python
import functools

import jax
import jax.numpy as jnp
from jax import lax
from jax.experimental import pallas as pl
from jax.experimental.pallas import tpu as pltpu
from jax.experimental.pallas import tpu_sc as plsc

D_MODEL = 1024
HALF = D_MODEL // 2
POOL_WIDTH = 512
N_POOL_GROUPS = 4
POOL_GROUP = 128
POOL_WINDOWS = (2, 4, 8, 16)
CONV_WIDTH = 512
N_EXPERTS = 64
TOP_K = 8
EXPERT_HIDDEN = 256
SHARED_HIDDEN = 256
ROUTED_SCALE = 2.5
EPS = 1e-6

POOL_HALO = 16
CONV_HALO = 8
TM_MIX = 512
TM_DEST = 2048
ROW_BLOCK = 512
ROW_RING = 8
TM_COMB = 512
N_SPLITS = 1
N_COMB_CHUNKS = 4
VMEM_LIMIT = 56 * 1024 * 1024

SC_CORES = 2
SC_SUBCORES = 16
SC_WORKERS = SC_CORES * SC_SUBCORES
SC_CHUNK = 64

BF16 = jnp.bfloat16
F32 = jnp.float32
U32 = jnp.uint32


def _rms(x, g):
    r = lax.rsqrt(jnp.mean(x * x, axis=-1, keepdims=True) + EPS)
    return (x * r) * g


def _dot(a, b):
    return jnp.dot(a, b, preferred_element_type=F32)


def _sigmoid(z):
    return 0.5 * jnp.tanh(0.5 * z) + 0.5


def _pack_pair(a, b):
    ra = lax.bitcast_convert_type(a.astype(BF16).astype(F32), U32)
    rb = lax.bitcast_convert_type(b.astype(BF16).astype(F32), U32)
    return ra | (rb >> 16)


def _unpack_pair(w):
    a = lax.bitcast_convert_type(w & jnp.uint32(0xFFFF0000), F32)
    b = lax.bitcast_convert_type(w << 16, F32)
    return a, b


def _mixer_router_kernel(n_seq_tiles,
                         x_ref, g_mix_ref, w_in_ref, b_gate_ref, w_grp_ref, pool_scale_ref,
                         w_po_ref, conv_w_ref, w_co_ref, w_o_ref, g_ffn_ref,
                         wr_ref, rbias_ref, s_gate_ref, s_up_ref, s_down_ref,
                         xres_ref, h2p_ref, idx_ref, wsel_ref, rank_ref, counts_ref,
                         ext_pool, ext_conv, cnt_carry, tri):
    tm = x_ref.shape[0]
    i = pl.program_id(0)
    st = i % n_seq_tiles

    @pl.when(i == 0)
    def _():
        r = lax.broadcasted_iota(jnp.int32, (tm, tm), 0)
        c = lax.broadcasted_iota(jnp.int32, (tm, tm), 1)
        tri[...] = (r < c).astype(BF16)
        cnt_carry[...] = jnp.zeros_like(cnt_carry)

    @pl.when(st == 0)
    def _():
        ext_pool[0:POOL_HALO, :] = jnp.zeros((POOL_HALO, POOL_WIDTH), F32)
        ext_conv[0:CONV_HALO, :] = jnp.zeros((CONV_HALO, CONV_WIDTH), F32)

    x = x_ref[...]
    hb = _rms(x, g_mix_ref[...]).astype(BF16)

    o0 = POOL_WIDTH
    o1 = o0 + CONV_WIDTH
    o2 = o1 + CONV_WIDTH
    o3 = o2 + CONV_WIDTH

    u = _dot(hb, w_in_ref[:, 0:o0])
    ext_pool[POOL_HALO:POOL_HALO + tm, :] = u
    gc = _dot(hb, w_in_ref[:, o1:o2])
    v = _dot(hb, w_in_ref[:, o2:o3])
    pre_a = _dot(hb, w_in_ref[:, o3:o3 + D_MODEL])

    t_glob = st * tm + lax.broadcasted_iota(jnp.int32, (tm, 1), 0)
    mixed = []
    for gi, w in enumerate(POOL_WINDOWS):
        cols = slice(gi * POOL_GROUP, (gi + 1) * POOL_GROUP)
        ug = u[:, cols]
        acc = ug
        for j in range(1, w):
            acc = acc + ext_pool[POOL_HALO - j:POOL_HALO - j + tm, cols]
        cnt = jnp.minimum(t_glob + 1, w).astype(F32)
        pooled = acc * (1.0 / cnt) - ug
        mixed.append(_dot(pooled.astype(BF16), w_grp_ref[gi]))
    ext_pool[0:POOL_HALO, :] = ext_pool[tm:tm + POOL_HALO, :]
    pre_b = _dot(hb, w_in_ref[:, o3 + D_MODEL:o3 + 2 * D_MODEL])
    gb = _dot(hb, w_in_ref[:, o0:o1])

    cv = gc * v
    ext_conv[CONV_HALO:CONV_HALO + tm, :] = cv
    conv = (ext_conv[CONV_HALO - 2:CONV_HALO - 2 + tm, :] * conv_w_ref[0:1, :]
            + ext_conv[CONV_HALO - 1:CONV_HALO - 1 + tm, :] * conv_w_ref[1:2, :]
            + cv * conv_w_ref[2:3, :])
    ext_conv[0:CONV_HALO, :] = ext_conv[tm:tm + CONV_HALO, :]
    branch_b = _dot((gb * conv).astype(BF16), w_co_ref[...])
    mixed = jnp.concatenate(mixed, axis=1) * pool_scale_ref[...]
    branch_a = _dot(mixed.astype(BF16), w_po_ref[...])

    merged = (_sigmoid(pre_a + b_gate_ref[:, 0:D_MODEL]) * branch_a
              + _sigmoid(pre_b + b_gate_ref[:, D_MODEL:2 * D_MODEL]) * branch_b)
    x1 = x + _dot(merged.astype(BF16), w_o_ref[...])

    h2 = _rms(x1, g_ffn_ref[...])
    h2p_ref[...] = _pack_pair(h2[:, 0:HALF], h2[:, HALF:D_MODEL])
    h2b = h2.astype(BF16)

    nt = (((1,), (1,)), ((), ()))
    parts = lax.dot_general(wr_ref[...], h2b, nt, preferred_element_type=F32)
    logits = parts[0:N_EXPERTS, :] + parts[N_EXPERTS:2 * N_EXPERTS, :]
    sg = _dot(h2b, s_gate_ref[...])
    su = _dot(h2b, s_up_ref[...])
    scores = jax.nn.sigmoid(logits)
    sel = scores + rbias_ref[...]
    eidx = lax.broadcasted_iota(jnp.int32, (N_EXPERTS, tm), 0).astype(F32)
    e_rows, w_rows = [], []
    mask = jnp.zeros((N_EXPERTS, tm), F32)
    for _ in range(TOP_K):
        m = jnp.max(sel, axis=0, keepdims=True)
        ek = jnp.min(jnp.where(sel == m, eidx, float(N_EXPERTS)), axis=0, keepdims=True)
        oh = eidx == ek
        w_rows.append(jnp.sum(jnp.where(oh, scores, 0.0), axis=0, keepdims=True))
        e_rows.append(ek)
        mask = mask + oh.astype(F32)
        sel = jnp.where(oh, -jnp.inf, sel)

    shared = _dot((sg * _sigmoid(sg) * su).astype(BF16), s_down_ref[...])
    xres_ref[...] = x1 + shared

    wsum = w_rows[0]
    for k in range(1, TOP_K):
        wsum = wsum + w_rows[k]

    before = _dot(mask.astype(BF16), tri[...]) + cnt_carry[...]
    for k in range(TOP_K):
        oh = eidx == e_rows[k]
        rank_ref[k:k + 1, :] = jnp.sum(jnp.where(oh, before, 0.0), axis=0,
                                       keepdims=True).astype(jnp.int32)
        idx_ref[k:k + 1, :] = e_rows[k].astype(jnp.int32)
        wsel_ref[k:k + 1, :] = w_rows[k] / wsum * ROUTED_SCALE
    total = cnt_carry[...] + jnp.sum(mask, axis=1, keepdims=True)
    cnt_carry[...] = total
    counts_ref[...] = total.astype(jnp.int32)


def _mixer_router(x2d, tok0, t_tok, seq_len, g_mix, w_in, b_gate, w_grp, pool_scale, w_po, conv_w,
                  w_co, w_o, g_ffn, wr, rbias, s_gate, s_up, s_down):
    tm = TM_MIX
    n_seq_tiles = seq_len // tm
    off = tok0 // tm
    const = lambda shape: pl.BlockSpec(shape, lambda i: (0,) * len(shape),
                                       pipeline_mode=pl.Buffered(1))
    x_blk = pl.BlockSpec((tm, D_MODEL), lambda i: (i + off, 0))
    row_blk = pl.BlockSpec((tm, D_MODEL), lambda i: (i, 0))
    half_blk = pl.BlockSpec((tm, HALF), lambda i: (i, 0))
    slot_blk = pl.BlockSpec((TOP_K, tm), lambda i: (0, i))
    return pl.pallas_call(
        functools.partial(_mixer_router_kernel, n_seq_tiles),
        grid=(t_tok // tm,),
        in_specs=[x_blk, const(g_mix.shape), const(w_in.shape), const(b_gate.shape),
                  const(w_grp.shape), const(pool_scale.shape), const(w_po.shape),
                  const(conv_w.shape), const(w_co.shape), const(w_o.shape), const(g_ffn.shape),
                  const(wr.shape), const(rbias.shape),
                  const(s_gate.shape), const(s_up.shape), const(s_down.shape)],
        out_specs=[row_blk, half_blk, slot_blk, slot_blk, slot_blk,
                   pl.BlockSpec((N_EXPERTS, 1), lambda i: (0, 0))],
        out_shape=[jax.ShapeDtypeStruct((t_tok, D_MODEL), F32),
                   jax.ShapeDtypeStruct((t_tok, HALF), U32),
                   jax.ShapeDtypeStruct((TOP_K, t_tok), jnp.int32),
                   jax.ShapeDtypeStruct((TOP_K, t_tok), F32),
                   jax.ShapeDtypeStruct((TOP_K, t_tok), jnp.int32),
                   jax.ShapeDtypeStruct((N_EXPERTS, 1), jnp.int32)],
        scratch_shapes=[pltpu.VMEM((POOL_HALO + tm, POOL_WIDTH), F32),
                        pltpu.VMEM((CONV_HALO + tm, CONV_WIDTH), F32),
                        pltpu.VMEM((N_EXPERTS, 1), F32),
                        pltpu.VMEM((tm, tm), BF16)],
        compiler_params=pltpu.CompilerParams(dimension_semantics=("arbitrary",),
                                             vmem_limit_bytes=VMEM_LIMIT),
        name="mixer_router",
    )(x2d, g_mix, w_in, b_gate, w_grp, pool_scale, w_po, conv_w, w_co, w_o, g_ffn,
      wr, rbias, s_gate, s_up, s_down)


def _dest_kernel(idx_ref, rank_ref, pad_start_ref, dest_ref):
    tm = idx_ref.shape[1]
    eidx = lax.broadcasted_iota(jnp.int32, (N_EXPERTS, tm), 0)
    ps = pad_start_ref[...].astype(F32)
    for k in range(TOP_K):
        oh = eidx == idx_ref[k:k + 1, :]
        start = jnp.sum(jnp.where(oh, ps, 0.0), axis=0, keepdims=True)
        dest_ref[k:k + 1, :] = start.astype(jnp.int32) + rank_ref[k:k + 1, :]


def _dest(idx_t, rank_t, pad_start):
    t_tok = idx_t.shape[1]
    slot_blk = pl.BlockSpec((TOP_K, TM_DEST), lambda i: (0, i))
    return pl.pallas_call(
        _dest_kernel,
        grid=(t_tok // TM_DEST,),
        in_specs=[slot_blk, slot_blk, pl.BlockSpec((N_EXPERTS, 1), lambda i: (0, 0))],
        out_specs=slot_blk,
        out_shape=jax.ShapeDtypeStruct((TOP_K, t_tok), jnp.int32),
        compiler_params=pltpu.CompilerParams(dimension_semantics=("arbitrary",)),
        name="dest",
    )(idx_t, rank_t, pad_start)


def _sc_mesh():
    return plsc.VectorSubcoreMesh(core_axis_name="c", subcore_axis_name="s")


def _sc_worker_id():
    return lax.axis_index("s") * SC_CORES + lax.axis_index("c")


def _dispatch(h2p, dest_t, zero_rows, n_rows):
    t_tok, width = h2p.shape
    per_w = t_tok // SC_WORKERS
    n_chunks = per_w // SC_CHUNK
    z_chunks = zero_rows.shape[0] // (SC_WORKERS * SC_CHUNK)
    dest_w = (dest_t.reshape(TOP_K, SC_WORKERS, n_chunks, SC_CHUNK)
              .transpose(1, 2, 0, 3).reshape(SC_WORKERS * n_chunks * TOP_K, SC_CHUNK))
    zero_w = zero_rows.reshape(SC_WORKERS * z_chunks, SC_CHUNK)

    @functools.partial(
        pl.kernel, mesh=_sc_mesh(),
        out_type=jax.ShapeDtypeStruct((n_rows, width), h2p.dtype),
        scratch_types=[pltpu.VMEM((n_chunks * TOP_K, SC_CHUNK), jnp.int32),
                       pltpu.VMEM((z_chunks, SC_CHUNK), jnp.int32),
                       pltpu.VMEM((2, SC_CHUNK, width), h2p.dtype),
                       pltpu.VMEM((SC_CHUNK, width), h2p.dtype),
                       pltpu.SemaphoreType.DMA, pltpu.SemaphoreType.DMA,
                       pltpu.SemaphoreType.DMA],
        name="dispatch",
    )
    def k(h2p_hbm, dest_hbm, zidx_hbm, zsrc_hbm, xs_hbm, idx_v, zidx_v, rows_v, zero_v,
          gsem, wsem, zsem):
        wid = _sc_worker_id()
        base = wid * per_w
        pltpu.sync_copy(dest_hbm.at[pl.ds(wid * n_chunks * TOP_K, n_chunks * TOP_K)], idx_v)
        pltpu.sync_copy(zidx_hbm.at[pl.ds(wid * z_chunks, z_chunks)], zidx_v)
        pltpu.sync_copy(zsrc_hbm, zero_v)

        def zput(j):
            return pltpu.make_async_copy(zero_v, xs_hbm.at[zidx_v.at[j]], zsem)

        for j in range(z_chunks):
            zput(j).start()

        def get(j, slot):
            return pltpu.make_async_copy(h2p_hbm.at[pl.ds(base + j * SC_CHUNK, SC_CHUNK)],
                                         rows_v.at[slot], gsem)

        def put(j, slot, kk):
            return pltpu.make_async_copy(rows_v.at[slot], xs_hbm.at[idx_v.at[j * TOP_K + kk]], wsem)

        get(0, 0).start()

        @pl.loop(0, n_chunks, step=2)
        def _(j):
            for b in range(2):
                jj = j + b
                get(jj, b).wait()

                @pl.when(jj >= 1)
                def _():
                    for kk in range(TOP_K):
                        put(jj - 1, 1 - b, kk).wait()

                @pl.when(jj + 1 < n_chunks)
                def _():
                    get(jj + 1, 1 - b).start()
                for kk in range(TOP_K):
                    put(jj, b, kk).start()

        for kk in range(TOP_K):
            put(n_chunks - 1, (n_chunks - 1) % 2, kk).wait()
        for j in range(z_chunks):
            zput(j).wait()

    return k(h2p, dest_w, zero_w, jnp.zeros((SC_CHUNK, width), h2p.dtype))


def _regroup(ys, dest_t):
    width = ys.shape[1]
    n_slots, t_tok = dest_t.shape
    m_rows = n_slots * t_tok
    per_w = m_rows // SC_WORKERS
    n_chunks = per_w // SC_CHUNK
    idx2 = dest_t.reshape(SC_WORKERS * n_chunks, SC_CHUNK)

    @functools.partial(
        pl.kernel, mesh=_sc_mesh(),
        out_type=jax.ShapeDtypeStruct((m_rows, width), ys.dtype),
        scratch_types=[pltpu.VMEM((n_chunks, SC_CHUNK), jnp.int32),
                       pltpu.VMEM((2, SC_CHUNK, width), ys.dtype),
                       pltpu.SemaphoreType.DMA, pltpu.SemaphoreType.DMA],
        name="regroup",
    )
    def k(ys_hbm, idx_hbm, out_hbm, idx_v, rows_v, gsem, wsem):
        wid = _sc_worker_id()
        base = wid * per_w
        pltpu.sync_copy(idx_hbm.at[pl.ds(wid * n_chunks, n_chunks)], idx_v)

        def get(j, slot):
            return pltpu.make_async_copy(ys_hbm.at[idx_v.at[j]], rows_v.at[slot], gsem)

        def put(j, slot):
            return pltpu.make_async_copy(rows_v.at[slot],
                                         out_hbm.at[pl.ds(base + j * SC_CHUNK, SC_CHUNK)], wsem)

        get(0, 0).start()

        @pl.loop(0, n_chunks, step=2)
        def _(j):
            for b in range(2):
                jj = j + b
                get(jj, b).wait()

                @pl.when(jj >= 1)
                def _():
                    put(jj - 1, 1 - b).wait()

                @pl.when(jj + 1 < n_chunks)
                def _():
                    get(jj + 1, 1 - b).start()
                put(jj, b).start()

        put(n_chunks - 1, (n_chunks - 1) % 2).wait()

    return k(ys, idx2).reshape(n_slots, t_tok, width)


def _experts_kernel(n_blk, blk_e_ref, next_e_ref, n_used_ref,
                    xs_hbm, eg_hbm, eu_hbm, ed_hbm, ys_hbm,
                    xbuf, ybuf, hid, stg_g, stg_u, stg_d, wg, wu, wd, xsem, ysem, wsem):
    n = n_used_ref[0]

    def ring(b):
        return jnp.bitwise_and(b, ROW_RING - 1)

    def x_copy(b):
        return pltpu.make_async_copy(xs_hbm.at[pl.ds(pl.multiple_of(b * ROW_BLOCK, ROW_BLOCK),
                                                     ROW_BLOCK)], xbuf.at[ring(b)], xsem.at[ring(b)])

    def y_copy(b):
        return pltpu.make_async_copy(ybuf.at[ring(b)],
                                     ys_hbm.at[pl.ds(pl.multiple_of(b * ROW_BLOCK, ROW_BLOCK),
                                                     ROW_BLOCK)], ysem.at[ring(b)])

    def w_copies(e):
        return (pltpu.make_async_copy(eg_hbm.at[e], stg_g, wsem.at[0]),
                pltpu.make_async_copy(eu_hbm.at[e], stg_u, wsem.at[1]),
                pltpu.make_async_copy(ed_hbm.at[e], stg_d, wsem.at[2]))

    def switch_expert(e, wslot):
        for cp in w_copies(e):
            cp.wait()
        wg[wslot] = stg_g[...].astype(BF16)
        wu[wslot] = stg_u[...].astype(BF16)
        wd[wslot] = stg_d[...].astype(BF16)
        nxt = next_e_ref[e]

        @pl.when(nxt >= 0)
        def _():
            for cp in w_copies(nxt):
                cp.start()

    def gate_up(b, wslot):
        xa, xb = _unpack_pair(xbuf[ring(b)])
        xb16 = jnp.concatenate([xa, xb], axis=1).astype(BF16)
        g = _dot(xb16, wg[wslot])
        up = _dot(xb16, wu[wslot])
        hid[jnp.bitwise_and(b, 1)] = (g * _sigmoid(g) * up).astype(BF16)

    def down(b, wslot):
        y = _dot(hid[jnp.bitwise_and(b, 1)], wd[wslot])
        ybuf[ring(b)] = _pack_pair(y[:, 0:HALF], y[:, HALF:D_MODEL])

    e0 = blk_e_ref[0]
    for cp in w_copies(e0):
        cp.start()
    for j in range(ROW_RING):
        @pl.when(j < n)
        def _():
            x_copy(j).start()
    switch_expert(e0, 0)
    x_copy(0).wait()
    gate_up(0, 0)

    def body(b, wslot_prev):
        e = blk_e_ref[b]
        first = e != blk_e_ref[b - 1]
        wslot = jnp.where(first, 1 - wslot_prev, wslot_prev)

        @pl.when(first)
        def _():
            switch_expert(e, wslot)

        x_copy(b).wait()

        @pl.when(b + ROW_RING - 1 < n)
        def _():
            x_copy(b + ROW_RING - 1).start()

        @pl.when(b >= ROW_RING + 1)
        def _():
            y_copy(b - 1 - ROW_RING).wait()

        down(b - 1, wslot_prev)
        gate_up(b, wslot)
        y_copy(b - 1).start()
        return wslot

    wslot_last = lax.fori_loop(1, n, body, jnp.int32(0))

    last = n - 1

    @pl.when(last >= ROW_RING)
    def _():
        y_copy(last - ROW_RING).wait()
    down(last, wslot_last)
    y_copy(last).start()
    for j in range(ROW_RING - 1, -1, -1):
        @pl.when(last - j >= 0)
        def _():
            y_copy(last - j).wait()

    ybuf[0] = jnp.zeros((ROW_BLOCK, HALF), U32)

    def zero_tail(b, c):
        cp = pltpu.make_async_copy(ybuf.at[0],
                                   ys_hbm.at[pl.ds(pl.multiple_of(b * ROW_BLOCK, ROW_BLOCK),
                                                   ROW_BLOCK)], ysem.at[0])
        cp.start()
        cp.wait()
        return c
    lax.fori_loop(n, n_blk, zero_tail, 0)


def _experts(blk_e, next_e, n_used, xs, e_gate, e_up, e_down):
    n_rows = xs.shape[0]
    n_blk = n_rows // ROW_BLOCK
    any_spec = pl.BlockSpec(memory_space=pl.ANY)
    grid_spec = pltpu.PrefetchScalarGridSpec(
        num_scalar_prefetch=3,
        grid=(1,),
        in_specs=[any_spec, any_spec, any_spec, any_spec],
        out_specs=any_spec,
        scratch_shapes=[pltpu.VMEM((ROW_RING, ROW_BLOCK, HALF), U32),
                        pltpu.VMEM((ROW_RING, ROW_BLOCK, HALF), U32),
                        pltpu.VMEM((2, ROW_BLOCK, EXPERT_HIDDEN), BF16),
                        pltpu.VMEM((D_MODEL, EXPERT_HIDDEN), F32),
                        pltpu.VMEM((D_MODEL, EXPERT_HIDDEN), F32),
                        pltpu.VMEM((EXPERT_HIDDEN, D_MODEL), F32),
                        pltpu.VMEM((2, D_MODEL, EXPERT_HIDDEN), BF16),
                        pltpu.VMEM((2, D_MODEL, EXPERT_HIDDEN), BF16),
                        pltpu.VMEM((2, EXPERT_HIDDEN, D_MODEL), BF16),
                        pltpu.SemaphoreType.DMA((ROW_RING,)),
                        pltpu.SemaphoreType.DMA((ROW_RING,)),
                        pltpu.SemaphoreType.DMA((3,))],
    )
    return pl.pallas_call(
        functools.partial(_experts_kernel, n_blk),
        grid_spec=grid_spec,
        out_shape=jax.ShapeDtypeStruct((n_rows, HALF), U32),
        compiler_params=pltpu.CompilerParams(dimension_semantics=("arbitrary",)),
        name="experts",
    )(blk_e, next_e, n_used, xs, e_gate, e_up, e_down)


def _combine_kernel(yg_ref, wsel_ref, xres_ref, g_final_ref, *rest):
    out_ref = rest[-1]
    xres = xres_ref[...]
    acc_a = xres[:, 0:HALF]
    acc_b = xres[:, HALF:D_MODEL]
    wsel = wsel_ref[...].T
    for k in range(TOP_K):
        ya, yb = _unpack_pair(yg_ref[k])
        wk = wsel[:, k:k + 1]
        acc_a = acc_a + ya * wk
        acc_b = acc_b + yb * wk
    out_ref[...] = _rms(jnp.concatenate([acc_a, acc_b], axis=1), g_final_ref[...])


def _combine(yg, wsel_t, xres, g_final, out_prev, tok_local, tok_out, t_out):
    tm = TM_COMB
    steps = yg.shape[1] // tm
    off = tok_local // tm
    off_out = tok_out // tm
    in_specs = [pl.BlockSpec((TOP_K, tm, HALF), lambda i: (0, i, 0)),
                pl.BlockSpec((TOP_K, tm), lambda i: (0, i + off)),
                pl.BlockSpec((tm, D_MODEL), lambda i: (i + off, 0)),
                pl.BlockSpec((1, D_MODEL), lambda i: (0, 0))]
    args = [yg, wsel_t, xres, g_final]
    aliases = {}
    if out_prev is not None:
        in_specs.append(pl.BlockSpec(memory_space=pl.ANY))
        args.append(out_prev)
        aliases = {4: 0}
    return pl.pallas_call(
        _combine_kernel,
        grid=(steps,),
        in_specs=in_specs,
        out_specs=pl.BlockSpec((tm, D_MODEL), lambda i: (i + off_out, 0)),
        out_shape=jax.ShapeDtypeStruct((t_out, D_MODEL), F32),
        input_output_aliases=aliases,
        compiler_params=pltpu.CompilerParams(dimension_semantics=("arbitrary",),
                                             vmem_limit_bytes=VMEM_LIMIT),
        name="combine",
    )(*args)


def kernel(x, g_mix, w_in, b_gate, w_pool_group, pool_scale, w_pool_out, conv_w, w_conv_out, w_o,
           g_ffn, w_router, router_bias, e_gate, e_up, e_down, s_gate, s_up, s_down, g_final):
    b, s, d = x.shape
    t_tok = b * s
    t_split = t_tok // N_SPLITS
    t_chunk = t_split // N_COMB_CHUNKS
    n_pad = N_EXPERTS * ROW_BLOCK
    assert d == D_MODEL and s % TM_MIX == 0 and TM_MIX >= POOL_HALO
    assert b % N_SPLITS == 0 and t_split % TM_DEST == 0 and t_chunk % TM_COMB == 0
    assert (t_chunk * TOP_K) % (2 * SC_WORKERS * SC_CHUNK) == 0
    assert t_split % (2 * SC_WORKERS * SC_CHUNK) == 0 and n_pad % (SC_WORKERS * SC_CHUNK) == 0

    row = lambda a: a.reshape(1, -1)
    wr_t = w_router.T.astype(F32)
    wr_hi = wr_t.astype(BF16)
    wr = jnp.concatenate([wr_hi, (wr_t - wr_hi.astype(F32)).astype(BF16)], axis=0)
    mixer_weights = (row(g_mix), w_in.astype(BF16), row(b_gate), w_pool_group.astype(BF16),
                     row(pool_scale), w_pool_out.astype(BF16), conv_w, w_conv_out.astype(BF16),
                     w_o.astype(BF16), row(g_ffn), wr, router_bias.astype(F32).reshape(N_EXPERTS, 1),
                     s_gate.astype(BF16), s_up.astype(BF16), s_down.astype(BF16))
    x2d = x.reshape(t_tok, d)
    e_ids = jnp.arange(N_EXPERTS, dtype=jnp.int32)
    n_rows = t_split * TOP_K + n_pad
    n_blk = n_rows // ROW_BLOCK
    blk_row0 = jnp.arange(n_blk, dtype=jnp.int32) * ROW_BLOCK
    j_pad = jnp.arange(n_pad, dtype=jnp.int32)

    out = None
    for sp in range(N_SPLITS):
        tok0 = sp * t_split
        xres, h2p, idx_t, wsel_t, rank_t, counts = _mixer_router(x2d, tok0, t_split, s,
                                                                 *mixer_weights)

        counts = counts.reshape(N_EXPERTS)
        padded = (counts + ROW_BLOCK - 1) // ROW_BLOCK * ROW_BLOCK
        pad_end = jnp.cumsum(padded)
        pad_start = pad_end - padded
        n_used = (pad_end[-1:] // ROW_BLOCK).astype(jnp.int32)
        blk_e = jnp.minimum(jnp.sum(pad_end[None, :] <= blk_row0[:, None], axis=1),
                            N_EXPERTS - 1).astype(jnp.int32)
        later_used = jnp.logical_and(e_ids[None, :] > e_ids[:, None], padded[None, :] > 0)
        next_e = jnp.min(jnp.where(later_used, e_ids[None, :], N_EXPERTS), axis=1)
        next_e = jnp.where(next_e < N_EXPERTS, next_e, -1).astype(jnp.int32)
        gap = padded - counts
        gap_before = jnp.cumsum(gap) - gap
        zero_rows = (j_pad + jnp.sum(jnp.where(gap_before[None, :] <= j_pad[:, None],
                                               counts[None, :], 0), axis=1)).astype(jnp.int32)

        dest_t = _dest(idx_t, rank_t, pad_start.reshape(N_EXPERTS, 1).astype(jnp.int32))
        xs = _dispatch(h2p, dest_t, zero_rows, n_rows)
        ys = _experts(blk_e, next_e, n_used, xs, e_gate, e_up, e_down)
        for c in range(N_COMB_CHUNKS):
            yg = _regroup(ys, dest_t[:, c * t_chunk:(c + 1) * t_chunk])
            out = _combine(yg, wsel_t, xres, row(g_final), out, c * t_chunk, tok0 + c * t_chunk,
                           t_tok)
    return out.reshape(b, s, d)
```

```python
import functools

import jax
import jax.numpy as jnp
from jax import lax
from jax.experimental import pallas as pl
from jax.experimental.pallas import tpu as pltpu
from jax.experimental.pallas import tpu_sc as plsc

D_MODEL = 1024
HALF = D_MODEL // 2
POOL_WIDTH = 512
N_POOL_GROUPS = 4
POOL_GROUP = 128
POOL_WINDOWS = (2, 4, 8, 16)
CONV_WIDTH = 512
N_EXPERTS = 64
TOP_K = 8
EXPERT_HIDDEN = 256
SHARED_HIDDEN = 256
ROUTED_SCALE = 2.5
EPS = 1e-6

POOL_HALO = 16
CONV_HALO = 8
TM_MIX = 512
TM_DEST = 2048
PLAN_LANES = 2048
ROW_BLOCK = 512
ROW_RING = 8
TM_COMB = 512
N_SPLITS = 1
N_COMB_CHUNKS = 4
VMEM_LIMIT = 56 * 1024 * 1024

SC_CORES = 2
SC_SUBCORES = 16
SC_WORKERS = SC_CORES * SC_SUBCORES
SC_CHUNK = 64

BF16 = jnp.bfloat16
F32 = jnp.float32
U32 = jnp.uint32


def _rms(x, g):
    r = lax.rsqrt(jnp.mean(x * x, axis=-1, keepdims=True) + EPS)
    return (x * r) * g


def _dot(a, b):
    return jnp.dot(a, b, preferred_element_type=F32)


def _sigmoid(z):
    return 0.5 * jnp.tanh(0.5 * z) + 0.5


def _pack_pair(a, b):
    ra = lax.bitcast_convert_type(a.astype(BF16).astype(F32), U32)
    rb = lax.bitcast_convert_type(b.astype(BF16).astype(F32), U32)
    return ra | (rb >> 16)


def _unpack_pair(w):
    a = lax.bitcast_convert_type(w & jnp.uint32(0xFFFF0000), F32)
    b = lax.bitcast_convert_type(w << 16, F32)
    return a, b


def _mixer_router_kernel(n_seq_tiles,
                         x_ref, g_mix_ref, w_in_ref, b_gate_ref, w_grp_ref, pool_scale_ref,
                         w_po_ref, conv_w_ref, w_co_ref, w_o_ref, g_ffn_ref,
                         wr_ref, rbias_ref, s_gate_ref, s_up_ref, s_down_ref,
                         xres_ref, h2p_ref, idx_ref, wsel_ref, rank_ref, counts_ref,
                         ext_pool, ext_conv, cnt_carry, tri):
    tm = x_ref.shape[0]
    i = pl.program_id(0)
    st = i % n_seq_tiles

    @pl.when(i == 0)
    def _():
        r = lax.broadcasted_iota(jnp.int32, (tm, tm), 0)
        c = lax.broadcasted_iota(jnp.int32, (tm, tm), 1)
        tri[...] = (r < c).astype(BF16)
        cnt_carry[...] = jnp.zeros_like(cnt_carry)

    @pl.when(st == 0)
    def _():
        ext_pool[0:POOL_HALO, :] = jnp.zeros((POOL_HALO, POOL_WIDTH), F32)
        ext_conv[0:CONV_HALO, :] = jnp.zeros((CONV_HALO, CONV_WIDTH), F32)

    x = x_ref[...]
    hb = _rms(x, g_mix_ref[...]).astype(BF16)

    o0 = POOL_WIDTH
    o1 = o0 + CONV_WIDTH
    o2 = o1 + CONV_WIDTH
    o3 = o2 + CONV_WIDTH

    u = _dot(hb, w_in_ref[:, 0:o0])
    ext_pool[POOL_HALO:POOL_HALO + tm, :] = u
    gc = _dot(hb, w_in_ref[:, o1:o2])
    v = _dot(hb, w_in_ref[:, o2:o3])
    pre_a = _dot(hb, w_in_ref[:, o3:o3 + D_MODEL])

    t_glob = st * tm + lax.broadcasted_iota(jnp.int32, (tm, 1), 0)
    mixed = []
    for gi, w in enumerate(POOL_WINDOWS):
        cols = slice(gi * POOL_GROUP, (gi + 1) * POOL_GROUP)
        ug = u[:, cols]
        acc = ug
        for j in range(1, w):
            acc = acc + ext_pool[POOL_HALO - j:POOL_HALO - j + tm, cols]
        cnt = jnp.minimum(t_glob + 1, w).astype(F32)
        pooled = acc * (1.0 / cnt) - ug
        mixed.append(_dot(pooled.astype(BF16), w_grp_ref[gi]))
    ext_pool[0:POOL_HALO, :] = ext_pool[tm:tm + POOL_HALO, :]
    pre_b = _dot(hb, w_in_ref[:, o3 + D_MODEL:o3 + 2 * D_MODEL])
    gb = _dot(hb, w_in_ref[:, o0:o1])

    cv = gc * v
    ext_conv[CONV_HALO:CONV_HALO + tm, :] = cv
    conv = (ext_conv[CONV_HALO - 2:CONV_HALO - 2 + tm, :] * conv_w_ref[0:1, :]
            + ext_conv[CONV_HALO - 1:CONV_HALO - 1 + tm, :] * conv_w_ref[1:2, :]
            + cv * conv_w_ref[2:3, :])
    ext_conv[0:CONV_HALO, :] = ext_conv[tm:tm + CONV_HALO, :]
    branch_b = _dot((gb * conv).astype(BF16), w_co_ref[...])
    mixed = jnp.concatenate(mixed, axis=1) * pool_scale_ref[...]
    branch_a = _dot(mixed.astype(BF16), w_po_ref[...])

    merged = (_sigmoid(pre_a + b_gate_ref[:, 0:D_MODEL]) * branch_a
              + _sigmoid(pre_b + b_gate_ref[:, D_MODEL:2 * D_MODEL]) * branch_b)
    x1 = x + _dot(merged.astype(BF16), w_o_ref[...])

    h2 = _rms(x1, g_ffn_ref[...])
    h2p_ref[...] = _pack_pair(h2[:, 0:HALF], h2[:, HALF:D_MODEL])
    h2b = h2.astype(BF16)

    nt = (((1,), (1,)), ((), ()))
    parts = lax.dot_general(wr_ref[...], h2b, nt, preferred_element_type=F32)
    logits = parts[0:N_EXPERTS, :] + parts[N_EXPERTS:2 * N_EXPERTS, :]
    sg = _dot(h2b, s_gate_ref[...])
    su = _dot(h2b, s_up_ref[...])
    scores = jax.nn.sigmoid(logits)
    sel = scores + rbias_ref[...]
    eidx = lax.broadcasted_iota(jnp.int32, (N_EXPERTS, tm), 0).astype(F32)
    e_rows, w_rows = [], []
    mask = jnp.zeros((N_EXPERTS, tm), F32)
    for _ in range(TOP_K):
        m = jnp.max(sel, axis=0, keepdims=True)
        ek = jnp.min(jnp.where(sel == m, eidx, float(N_EXPERTS)), axis=0, keepdims=True)
        oh = eidx == ek
        w_rows.append(jnp.sum(jnp.where(oh, scores, 0.0), axis=0, keepdims=True))
        e_rows.append(ek)
        mask = mask + oh.astype(F32)
        sel = jnp.where(oh, -jnp.inf, sel)

    shared = _dot((sg * _sigmoid(sg) * su).astype(BF16), s_down_ref[...])
    xres_ref[...] = x1 + shared

    wsum = w_rows[0]
    for k in range(1, TOP_K):
        wsum = wsum + w_rows[k]

    before = _dot(mask.astype(BF16), tri[...]) + cnt_carry[...]
    for k in range(TOP_K):
        oh = eidx == e_rows[k]
        rank_ref[k:k + 1, :] = jnp.sum(jnp.where(oh, before, 0.0), axis=0,
                                       keepdims=True).astype(jnp.int32)
        idx_ref[k:k + 1, :] = e_rows[k].astype(jnp.int32)
        wsel_ref[k:k + 1, :] = w_rows[k] / wsum * ROUTED_SCALE
    total = cnt_carry[...] + jnp.sum(mask, axis=1, keepdims=True)
    cnt_carry[...] = total
    counts_ref[...] = total.astype(jnp.int32)


def _mixer_router(x2d, tok0, t_tok, seq_len, g_mix, w_in, b_gate, w_grp, pool_scale, w_po, conv_w,
                  w_co, w_o, g_ffn, wr, rbias, s_gate, s_up, s_down):
    tm = TM_MIX
    n_seq_tiles = seq_len // tm
    off = tok0 // tm
    const = lambda shape: pl.BlockSpec(shape, lambda i: (0,) * len(shape),
                                       pipeline_mode=pl.Buffered(1))
    x_blk = pl.BlockSpec((tm, D_MODEL), lambda i: (i + off, 0))
    row_blk = pl.BlockSpec((tm, D_MODEL), lambda i: (i, 0))
    half_blk = pl.BlockSpec((tm, HALF), lambda i: (i, 0))
    slot_blk = pl.BlockSpec((TOP_K, tm), lambda i: (0, i))
    return pl.pallas_call(
        functools.partial(_mixer_router_kernel, n_seq_tiles),
        grid=(t_tok // tm,),
        in_specs=[x_blk, const(g_mix.shape), const(w_in.shape), const(b_gate.shape),
                  const(w_grp.shape), const(pool_scale.shape), const(w_po.shape),
                  const(conv_w.shape), const(w_co.shape), const(w_o.shape), const(g_ffn.shape),
                  const(wr.shape), const(rbias.shape),
                  const(s_gate.shape), const(s_up.shape), const(s_down.shape)],
        out_specs=[row_blk, half_blk, slot_blk, slot_blk, slot_blk,
                   pl.BlockSpec((N_EXPERTS, 1), lambda i: (0, 0))],
        out_shape=[jax.ShapeDtypeStruct((t_tok, D_MODEL), F32),
                   jax.ShapeDtypeStruct((t_tok, HALF), U32),
                   jax.ShapeDtypeStruct((TOP_K, t_tok), jnp.int32),
                   jax.ShapeDtypeStruct((TOP_K, t_tok), F32),
                   jax.ShapeDtypeStruct((TOP_K, t_tok), jnp.int32),
                   jax.ShapeDtypeStruct((N_EXPERTS, 1), jnp.int32)],
        scratch_shapes=[pltpu.VMEM((POOL_HALO + tm, POOL_WIDTH), F32),
                        pltpu.VMEM((CONV_HALO + tm, CONV_WIDTH), F32),
                        pltpu.VMEM((N_EXPERTS, 1), F32),
                        pltpu.VMEM((tm, tm), BF16)],
        compiler_params=pltpu.CompilerParams(dimension_semantics=("arbitrary",),
                                             vmem_limit_bytes=VMEM_LIMIT),
        name="mixer_router",
    )(x2d, g_mix, w_in, b_gate, w_grp, pool_scale, w_po, conv_w, w_co, w_o, g_ffn,
      wr, rbias, s_gate, s_up, s_down)


def _plan_kernel(n_blk, n_pad, counts_ref, idx_ref, rank_ref,
                 dest_ref, blk_e_ref, next_e_ref, n_used_ref, zero_rows_ref, pad_start):
    e_n = N_EXPERTS

    @pl.when(pl.program_id(0) == 0)
    def _():
        sub = lax.broadcasted_iota(jnp.int32, (e_n, e_n), 0)
        lane = lax.broadcasted_iota(jnp.int32, (e_n, e_n), 1)
        c_col = counts_ref[...]
        p_col = ((c_col + (ROW_BLOCK - 1)) // ROW_BLOCK) * ROW_BLOCK
        c_f = c_col.astype(F32)
        p_f = p_col.astype(F32)
        gap_f = p_f - c_f
        to_row = lambda col: jnp.sum(jnp.where(sub == lane, col, 0.0), axis=0, keepdims=True)
        p_row = to_row(p_f)
        gap_row = to_row(gap_f)
        pad_end_col = jnp.sum(jnp.where(lane <= sub, p_row, 0.0), axis=1, keepdims=True)
        gap_before_col = jnp.sum(jnp.where(lane < sub, gap_row, 0.0), axis=1, keepdims=True)
        pad_start[...] = pad_end_col - p_f
        pad_end_last = jnp.sum(p_row, axis=1, keepdims=True)
        n_used_ref[...] = jnp.broadcast_to(pad_end_last * (1.0 / ROW_BLOCK),
                                           n_used_ref.shape).astype(jnp.int32)
        row0 = (lax.broadcasted_iota(jnp.int32, (e_n, blk_e_ref.shape[1]), 1)
                * ROW_BLOCK).astype(F32)
        owner = jnp.sum(jnp.where(pad_end_col <= row0, 1.0, 0.0), axis=0, keepdims=True)
        blk_e_ref[...] = jnp.minimum(owner, float(e_n - 1)).astype(jnp.int32)
        used_later = jnp.logical_and(sub > lane, p_f > 0.0)
        nxt = jnp.min(jnp.where(used_later, sub, e_n), axis=0, keepdims=True)
        next_e_ref[...] = jnp.full(next_e_ref.shape, -1, jnp.int32)
        next_e_ref[:, 0:e_n] = jnp.where(nxt < e_n, nxt, -1)
        for j0 in range(0, n_pad, PLAN_LANES):
            j = (j0 + lax.broadcasted_iota(jnp.int32, (e_n, PLAN_LANES), 1)).astype(F32)
            before = jnp.sum(jnp.where(gap_before_col <= j, c_f, 0.0), axis=0, keepdims=True)
            zero_rows_ref[:, j0:j0 + PLAN_LANES] = (j[0:1, :] + before).astype(jnp.int32)

    tm = idx_ref.shape[1]
    eidx = lax.broadcasted_iota(jnp.int32, (e_n, tm), 0)
    ps = pad_start[...]
    for k in range(TOP_K):
        oh = eidx == idx_ref[k:k + 1, :]
        start = jnp.sum(jnp.where(oh, ps, 0.0), axis=0, keepdims=True)
        dest_ref[k:k + 1, :] = start.astype(jnp.int32) + rank_ref[k:k + 1, :]


def _plan(counts, idx_t, rank_t, n_blk, n_pad):
    t_tok = idx_t.shape[1]
    lanes = 128
    n_blk_p = -(-n_blk // lanes) * lanes
    slot_blk = pl.BlockSpec((TOP_K, TM_DEST), lambda i: (0, i))
    whole = lambda n: pl.BlockSpec((1, n), lambda i: (0, 0))
    dest_t, blk_e, next_e, n_used, zero_rows = pl.pallas_call(
        functools.partial(_plan_kernel, n_blk, n_pad),
        grid=(t_tok // TM_DEST,),
        in_specs=[pl.BlockSpec((N_EXPERTS, 1), lambda i: (0, 0)), slot_blk, slot_blk],
        out_specs=[slot_blk, whole(n_blk_p), whole(lanes), whole(lanes), whole(n_pad)],
        out_shape=[jax.ShapeDtypeStruct((TOP_K, t_tok), jnp.int32),
                   jax.ShapeDtypeStruct((1, n_blk_p), jnp.int32),
                   jax.ShapeDtypeStruct((1, lanes), jnp.int32),
                   jax.ShapeDtypeStruct((1, lanes), jnp.int32),
                   jax.ShapeDtypeStruct((1, n_pad), jnp.int32)],
        scratch_shapes=[pltpu.VMEM((N_EXPERTS, 1), F32)],
        compiler_params=pltpu.CompilerParams(dimension_semantics=("arbitrary",)),
        name="plan",
    )(counts, idx_t, rank_t)
    return (dest_t, blk_e.reshape(-1), next_e.reshape(-1), n_used.reshape(-1),
            zero_rows.reshape(-1))


def _sc_mesh():
    return plsc.VectorSubcoreMesh(core_axis_name="c", subcore_axis_name="s")


def _sc_worker_id():
    return lax.axis_index("s") * SC_CORES + lax.axis_index("c")


def _dispatch(h2p, dest_t, zero_rows, n_rows):
    t_tok, width = h2p.shape
    per_w = t_tok // SC_WORKERS
    n_chunks = per_w // SC_CHUNK
    z_chunks = zero_rows.shape[0] // (SC_WORKERS * SC_CHUNK)
    dest_w = (dest_t.reshape(TOP_K, SC_WORKERS, n_chunks, SC_CHUNK)
              .transpose(1, 2, 0, 3).reshape(SC_WORKERS * n_chunks * TOP_K, SC_CHUNK))
    zero_w = zero_rows.reshape(SC_WORKERS * z_chunks, SC_CHUNK)

    @functools.partial(
        pl.kernel, mesh=_sc_mesh(),
        out_type=jax.ShapeDtypeStruct((n_rows, width), h2p.dtype),
        scratch_types=[pltpu.VMEM((n_chunks * TOP_K, SC_CHUNK), jnp.int32),
                       pltpu.VMEM((z_chunks, SC_CHUNK), jnp.int32),
                       pltpu.VMEM((2, SC_CHUNK, width), h2p.dtype),
                       pltpu.VMEM((SC_CHUNK, width), h2p.dtype),
                       pltpu.SemaphoreType.DMA, pltpu.SemaphoreType.DMA,
                       pltpu.SemaphoreType.DMA],
        name="dispatch",
    )
    def k(h2p_hbm, dest_hbm, zidx_hbm, zsrc_hbm, xs_hbm, idx_v, zidx_v, rows_v, zero_v,
          gsem, wsem, zsem):
        wid = _sc_worker_id()
        base = wid * per_w
        pltpu.sync_copy(dest_hbm.at[pl.ds(wid * n_chunks * TOP_K, n_chunks * TOP_K)], idx_v)
        pltpu.sync_copy(zidx_hbm.at[pl.ds(wid * z_chunks, z_chunks)], zidx_v)
        pltpu.sync_copy(zsrc_hbm, zero_v)

        def zput(j):
            return pltpu.make_async_copy(zero_v, xs_hbm.at[zidx_v.at[j]], zsem)

        for j in range(z_chunks):
            zput(j).start()

        def get(j, slot):
            return pltpu.make_async_copy(h2p_hbm.at[pl.ds(base + j * SC_CHUNK, SC_CHUNK)],
                                         rows_v.at[slot], gsem)

        def put(j, slot, kk):
            return pltpu.make_async_copy(rows_v.at[slot], xs_hbm.at[idx_v.at[j * TOP_K + kk]], wsem)

        get(0, 0).start()

        @pl.loop(0, n_chunks, step=2)
        def _(j):
            for b in range(2):
                jj = j + b
                get(jj, b).wait()

                @pl.when(jj >= 1)
                def _():
                    for kk in range(TOP_K):
                        put(jj - 1, 1 - b, kk).wait()

                @pl.when(jj + 1 < n_chunks)
                def _():
                    get(jj + 1, 1 - b).start()
                for kk in range(TOP_K):
                    put(jj, b, kk).start()

        for kk in range(TOP_K):
            put(n_chunks - 1, (n_chunks - 1) % 2, kk).wait()
        for j in range(z_chunks):
            zput(j).wait()

    return k(h2p, dest_w, zero_w, jnp.zeros((SC_CHUNK, width), h2p.dtype))


def _regroup(ys, dest_t):
    width = ys.shape[1]
    n_slots, t_tok = dest_t.shape
    m_rows = n_slots * t_tok
    per_w = m_rows // SC_WORKERS
    n_chunks = per_w // SC_CHUNK
    idx2 = dest_t.reshape(SC_WORKERS * n_chunks, SC_CHUNK)

    @functools.partial(
        pl.kernel, mesh=_sc_mesh(),
        out_type=jax.ShapeDtypeStruct((m_rows, width), ys.dtype),
        scratch_types=[pltpu.VMEM((n_chunks, SC_CHUNK), jnp.int32),
                       pltpu.VMEM((2, SC_CHUNK, width), ys.dtype),
                       pltpu.SemaphoreType.DMA, pltpu.SemaphoreType.DMA],
        name="regroup",
    )
    def k(ys_hbm, idx_hbm, out_hbm, idx_v, rows_v, gsem, wsem):
        wid = _sc_worker_id()
        base = wid * per_w
        pltpu.sync_copy(idx_hbm.at[pl.ds(wid * n_chunks, n_chunks)], idx_v)

        def get(j, slot):
            return pltpu.make_async_copy(ys_hbm.at[idx_v.at[j]], rows_v.at[slot], gsem)

        def put(j, slot):
            return pltpu.make_async_copy(rows_v.at[slot],
                                         out_hbm.at[pl.ds(base + j * SC_CHUNK, SC_CHUNK)], wsem)

        get(0, 0).start()

        @pl.loop(0, n_chunks, step=2)
        def _(j):
            for b in range(2):
                jj = j + b
                get(jj, b).wait()

                @pl.when(jj >= 1)
                def _():
                    put(jj - 1, 1 - b).wait()

                @pl.when(jj + 1 < n_chunks)
                def _():
                    get(jj + 1, 1 - b).start()
                put(jj, b).start()

        put(n_chunks - 1, (n_chunks - 1) % 2).wait()

    return k(ys, idx2).reshape(n_slots, t_tok, width)


def _experts_kernel(n_blk, blk_e_ref, next_e_ref, n_used_ref,
                    xs_hbm, eg_hbm, eu_hbm, ed_hbm, ys_hbm,
                    xbuf, ybuf, hid, stg_g, stg_u, stg_d, wg, wu, wd, xsem, ysem, wsem):
    n = n_used_ref[0]

    def ring(b):
        return jnp.bitwise_and(b, ROW_RING - 1)

    def x_copy(b):
        return pltpu.make_async_copy(xs_hbm.at[pl.ds(pl.multiple_of(b * ROW_BLOCK, ROW_BLOCK),
                                                     ROW_BLOCK)], xbuf.at[ring(b)], xsem.at[ring(b)])

    def y_copy(b):
        return pltpu.make_async_copy(ybuf.at[ring(b)],
                                     ys_hbm.at[pl.ds(pl.multiple_of(b * ROW_BLOCK, ROW_BLOCK),
                                                     ROW_BLOCK)], ysem.at[ring(b)])

    def w_copies(e):
        return (pltpu.make_async_copy(eg_hbm.at[e], stg_g, wsem.at[0]),
                pltpu.make_async_copy(eu_hbm.at[e], stg_u, wsem.at[1]),
                pltpu.make_async_copy(ed_hbm.at[e], stg_d, wsem.at[2]))

    def switch_expert(e, wslot):
        for cp in w_copies(e):
            cp.wait()
        wg[wslot] = stg_g[...].astype(BF16)
        wu[wslot] = stg_u[...].astype(BF16)
        wd[wslot] = stg_d[...].astype(BF16)
        nxt = next_e_ref[e]

        @pl.when(nxt >= 0)
        def _():
            for cp in w_copies(nxt):
                cp.start()

    def gate_up(b, wslot):
        xa, xb = _unpack_pair(xbuf[ring(b)])
        xb16 = jnp.concatenate([xa, xb], axis=1).astype(BF16)
        g = _dot(xb16, wg[wslot])
        up = _dot(xb16, wu[wslot])
        hid[jnp.bitwise_and(b, 1)] = (g * _sigmoid(g) * up).astype(BF16)

    def down(b, wslot):
        y = _dot(hid[jnp.bitwise_and(b, 1)], wd[wslot])
        ybuf[ring(b)] = _pack_pair(y[:, 0:HALF], y[:, HALF:D_MODEL])

    e0 = blk_e_ref[0]
    for cp in w_copies(e0):
        cp.start()
    for j in range(ROW_RING):
        @pl.when(j < n)
        def _():
            x_copy(j).start()
    switch_expert(e0, 0)
    x_copy(0).wait()
    gate_up(0, 0)

    def body(b, wslot_prev):
        e = blk_e_ref[b]
        first = e != blk_e_ref[b - 1]
        wslot = jnp.where(first, 1 - wslot_prev, wslot_prev)

        @pl.when(first)
        def _():
            switch_expert(e, wslot)

        x_copy(b).wait()

        @pl.when(b + ROW_RING - 1 < n)
        def _():
            x_copy(b + ROW_RING - 1).start()

        @pl.when(b >= ROW_RING + 1)
        def _():
            y_copy(b - 1 - ROW_RING).wait()

        down(b - 1, wslot_prev)
        gate_up(b, wslot)
        y_copy(b - 1).start()
        return wslot

    wslot_last = lax.fori_loop(1, n, body, jnp.int32(0))

    last = n - 1

    @pl.when(last >= ROW_RING)
    def _():
        y_copy(last - ROW_RING).wait()
    down(last, wslot_last)
    y_copy(last).start()
    for j in range(ROW_RING - 1, -1, -1):
        @pl.when(last - j >= 0)
        def _():
            y_copy(last - j).wait()

    ybuf[0] = jnp.zeros((ROW_BLOCK, HALF), U32)

    def zero_tail(b, c):
        cp = pltpu.make_async_copy(ybuf.at[0],
                                   ys_hbm.at[pl.ds(pl.multiple_of(b * ROW_BLOCK, ROW_BLOCK),
                                                   ROW_BLOCK)], ysem.at[0])
        cp.start()
        cp.wait()
        return c
    lax.fori_loop(n, n_blk, zero_tail, 0)


def _experts(blk_e, next_e, n_used, xs, e_gate, e_up, e_down):
    n_rows = xs.shape[0]
    n_blk = n_rows // ROW_BLOCK
    any_spec = pl.BlockSpec(memory_space=pl.ANY)
    grid_spec = pltpu.PrefetchScalarGridSpec(
        num_scalar_prefetch=3,
        grid=(1,),
        in_specs=[any_spec, any_spec, any_spec, any_spec],
        out_specs=any_spec,
        scratch_shapes=[pltpu.VMEM((ROW_RING, ROW_BLOCK, HALF), U32),
                        pltpu.VMEM((ROW_RING, ROW_BLOCK, HALF), U32),
                        pltpu.VMEM((2, ROW_BLOCK, EXPERT_HIDDEN), BF16),
                        pltpu.VMEM((D_MODEL, EXPERT_HIDDEN), F32),
                        pltpu.VMEM((D_MODEL, EXPERT_HIDDEN), F32),
                        pltpu.VMEM((EXPERT_HIDDEN, D_MODEL), F32),
                        pltpu.VMEM((2, D_MODEL, EXPERT_HIDDEN), BF16),
                        pltpu.VMEM((2, D_MODEL, EXPERT_HIDDEN), BF16),
                        pltpu.VMEM((2, EXPERT_HIDDEN, D_MODEL), BF16),
                        pltpu.SemaphoreType.DMA((ROW_RING,)),
                        pltpu.SemaphoreType.DMA((ROW_RING,)),
                        pltpu.SemaphoreType.DMA((3,))],
    )
    return pl.pallas_call(
        functools.partial(_experts_kernel, n_blk),
        grid_spec=grid_spec,
        out_shape=jax.ShapeDtypeStruct((n_rows, HALF), U32),
        compiler_params=pltpu.CompilerParams(dimension_semantics=("arbitrary",)),
        name="experts",
    )(blk_e, next_e, n_used, xs, e_gate, e_up, e_down)


def _combine_kernel(yg_ref, wsel_ref, xres_ref, g_final_ref, *rest):
    out_ref = rest[-1]
    xres = xres_ref[...]
    acc_a = xres[:, 0:HALF]
    acc_b = xres[:, HALF:D_MODEL]
    wsel = wsel_ref[...].T
    for k in range(TOP_K):
        ya, yb = _unpack_pair(yg_ref[k])
        wk = wsel[:, k:k + 1]
        acc_a = acc_a + ya * wk
        acc_b = acc_b + yb * wk
    out_ref[...] = _rms(jnp.concatenate([acc_a, acc_b], axis=1), g_final_ref[...])


def _combine(yg, wsel_t, xres, g_final, out_prev, tok_local, tok_out, t_out):
    tm = TM_COMB
    steps = yg.shape[1] // tm
    off = tok_local // tm
    off_out = tok_out // tm
    in_specs = [pl.BlockSpec((TOP_K, tm, HALF), lambda i: (0, i, 0)),
                pl.BlockSpec((TOP_K, tm), lambda i: (0, i + off)),
                pl.BlockSpec((tm, D_MODEL), lambda i: (i + off, 0)),
                pl.BlockSpec((1, D_MODEL), lambda i: (0, 0))]
    args = [yg, wsel_t, xres, g_final]
    aliases = {}
    if out_prev is not None:
        in_specs.append(pl.BlockSpec(memory_space=pl.ANY))
        args.append(out_prev)
        aliases = {4: 0}
    return pl.pallas_call(
        _combine_kernel,
        grid=(steps,),
        in_specs=in_specs,
        out_specs=pl.BlockSpec((tm, D_MODEL), lambda i: (i + off_out, 0)),
        out_shape=jax.ShapeDtypeStruct((t_out, D_MODEL), F32),
        input_output_aliases=aliases,
        compiler_params=pltpu.CompilerParams(dimension_semantics=("arbitrary",),
                                             vmem_limit_bytes=VMEM_LIMIT),
        name="combine",
    )(*args)


def kernel(x, g_mix, w_in, b_gate, w_pool_group, pool_scale, w_pool_out, conv_w, w_conv_out, w_o,
           g_ffn, w_router, router_bias, e_gate, e_up, e_down, s_gate, s_up, s_down, g_final):
    b, s, d = x.shape
    t_tok = b * s
    t_split = t_tok // N_SPLITS
    t_chunk = t_split // N_COMB_CHUNKS
    n_pad = N_EXPERTS * ROW_BLOCK
    assert d == D_MODEL and s % TM_MIX == 0 and TM_MIX >= POOL_HALO
    assert b % N_SPLITS == 0 and t_split % TM_DEST == 0 and t_chunk % TM_COMB == 0
    assert (t_chunk * TOP_K) % (2 * SC_WORKERS * SC_CHUNK) == 0
    assert t_split % (2 * SC_WORKERS * SC_CHUNK) == 0 and n_pad % (SC_WORKERS * SC_CHUNK) == 0
    assert n_pad % PLAN_LANES == 0

    row = lambda a: a.reshape(1, -1)
    wr_t = w_router.T.astype(F32)
    wr_hi = wr_t.astype(BF16)
    wr = jnp.concatenate([wr_hi, (wr_t - wr_hi.astype(F32)).astype(BF16)], axis=0)
    mixer_weights = (row(g_mix), w_in.astype(BF16), row(b_gate), w_pool_group.astype(BF16),
                     row(pool_scale), w_pool_out.astype(BF16), conv_w, w_conv_out.astype(BF16),
                     w_o.astype(BF16), row(g_ffn), wr, router_bias.astype(F32).reshape(N_EXPERTS, 1),
                     s_gate.astype(BF16), s_up.astype(BF16), s_down.astype(BF16))
    x2d = x.reshape(t_tok, d)
    n_rows = t_split * TOP_K + n_pad
    n_blk = n_rows // ROW_BLOCK

    out = None
    for sp in range(N_SPLITS):
        tok0 = sp * t_split
        xres, h2p, idx_t, wsel_t, rank_t, counts = _mixer_router(x2d, tok0, t_split, s,
                                                                 *mixer_weights)

        dest_t, blk_e, next_e, n_used, zero_rows = _plan(counts, idx_t, rank_t, n_blk, n_pad)
        xs = _dispatch(h2p, dest_t, zero_rows, n_rows)
        ys = _experts(blk_e, next_e, n_used, xs, e_gate, e_up, e_down)
        for c in range(N_COMB_CHUNKS):
            yg = _regroup(ys, dest_t[:, c * t_chunk:(c + 1) * t_chunk])
            out = _combine(yg, wsel_t, xres, row(g_final), out, c * t_chunk, tok0 + c * t_chunk,
                           t_tok)
    return out.reshape(b, s, d)
```

```python
import functools

import jax
import jax.numpy as jnp
from jax import lax
from jax.experimental import pallas as pl
from jax.experimental.pallas import tpu as pltpu
from jax.experimental.pallas import tpu_sc as plsc

D_MODEL = 1024
HALF = D_MODEL // 2
POOL_WIDTH = 512
N_POOL_GROUPS = 4
POOL_GROUP = 128
POOL_WINDOWS = (2, 4, 8, 16)
CONV_WIDTH = 512
N_EXPERTS = 64
TOP_K = 8
EXPERT_HIDDEN = 256
SHARED_HIDDEN = 256
ROUTED_SCALE = 2.5
EPS = 1e-6

POOL_HALO = 16
CONV_HALO = 8
TM_MIX = 512
W_IN_CHUNK = 128
TM_DEST = 2048
PLAN_LANES = 2048
ROW_BLOCK = 512
ROW_RING = 8
TM_COMB = 512
N_SPLITS = 1
N_COMB_CHUNKS = 4
VMEM_LIMIT = 56 * 1024 * 1024

SC_CORES = 2
SC_SUBCORES = 16
SC_WORKERS = SC_CORES * SC_SUBCORES
SC_CHUNK = 64

BF16 = jnp.bfloat16
F32 = jnp.float32
U32 = jnp.uint32


def _rms(x, g):
    r = lax.rsqrt(jnp.mean(x * x, axis=-1, keepdims=True) + EPS)
    return (x * r) * g


def _dot(a, b):
    return jnp.dot(a, b, preferred_element_type=F32)


def _sigmoid(z):
    return 0.5 * jnp.tanh(0.5 * z) + 0.5


def _pack_pair(a, b):
    ra = lax.bitcast_convert_type(a.astype(BF16).astype(F32), U32)
    rb = lax.bitcast_convert_type(b.astype(BF16).astype(F32), U32)
    return ra | (rb >> 16)


def _unpack_pair(w):
    a = lax.bitcast_convert_type(w & jnp.uint32(0xFFFF0000), F32)
    b = lax.bitcast_convert_type(w << 16, F32)
    return a, b


def _load_weights_bf16(w_in_hbm, w_grp_hbm, w_po_hbm, w_co_hbm, w_o_hbm, s_gate_hbm, s_up_hbm,
                       s_down_hbm, w_in_ref, w_grp_ref, w_po_ref, w_co_ref, w_o_ref, s_gate_ref,
                       s_up_ref, s_down_ref, stg_in, stg_sq, stg_sh, stg_grp, wsem):
    copy = lambda src, dst, k: pltpu.make_async_copy(src, dst, wsem.at[k])
    rows = stg_in.shape[1]
    n_in = w_in_hbm.shape[0] // rows
    half = w_po_hbm.shape[0]
    c_in = [copy(w_in_hbm.at[pl.ds(j * rows, rows)], stg_in.at[j % 2], j % 2) for j in range(n_in)]
    c_o = copy(w_o_hbm, stg_sq, 2)
    c_sg = copy(s_gate_hbm, stg_sh.at[0], 3)
    c_su = copy(s_up_hbm, stg_sh.at[1], 4)
    c_grp = copy(w_grp_hbm, stg_grp, 5)
    c_po = copy(w_po_hbm, stg_sq.at[0:half], 2)
    c_co = copy(w_co_hbm, stg_sq.at[half:2 * half], 6)
    c_sd = copy(s_down_hbm, stg_sq.at[0:s_down_hbm.shape[0]], 2)
    for cp in (c_in[0], c_in[1], c_o, c_sg, c_su, c_grp):
        cp.start()
    for j in range(n_in):
        c_in[j].wait()
        w_in_ref[j * rows:(j + 1) * rows, :] = stg_in[j % 2].astype(BF16)
        if j + 2 < n_in:
            c_in[j + 2].start()
    c_o.wait()
    w_o_ref[...] = stg_sq[...].astype(BF16)
    c_po.start()
    c_co.start()
    c_sg.wait()
    s_gate_ref[...] = stg_sh[0].astype(BF16)
    c_su.wait()
    s_up_ref[...] = stg_sh[1].astype(BF16)
    c_grp.wait()
    w_grp_ref[...] = stg_grp[...].astype(BF16)
    c_po.wait()
    w_po_ref[...] = stg_sq[0:half, :].astype(BF16)
    c_co.wait()
    w_co_ref[...] = stg_sq[half:2 * half, :].astype(BF16)
    c_sd.start()
    c_sd.wait()
    s_down_ref[...] = stg_sq[0:s_down_hbm.shape[0], :].astype(BF16)


def _mixer_router_kernel(n_seq_tiles,
                         x_ref, g_mix_ref, w_in_hbm, b_gate_ref, w_grp_hbm, pool_scale_ref,
                         w_po_hbm, conv_w_ref, w_co_hbm, w_o_hbm, g_ffn_ref,
                         wr_ref, rbias_ref, s_gate_hbm, s_up_hbm, s_down_hbm,
                         xres_ref, h2p_ref, idx_ref, wsel_ref, rank_ref, counts_ref,
                         ext_pool, ext_conv, cnt_carry, tri,
                         w_in_ref, w_grp_ref, w_po_ref, w_co_ref, w_o_ref, s_gate_ref, s_up_ref,
                         s_down_ref, stg_in, stg_sq, stg_sh, stg_grp, wsem):
    tm = x_ref.shape[0]
    i = pl.program_id(0)
    st = i % n_seq_tiles

    @pl.when(i == 0)
    def _():
        _load_weights_bf16(w_in_hbm, w_grp_hbm, w_po_hbm, w_co_hbm, w_o_hbm, s_gate_hbm, s_up_hbm,
                           s_down_hbm, w_in_ref, w_grp_ref, w_po_ref, w_co_ref, w_o_ref,
                           s_gate_ref, s_up_ref, s_down_ref, stg_in, stg_sq, stg_sh, stg_grp, wsem)
        r = lax.broadcasted_iota(jnp.int32, (tm, tm), 0)
        c = lax.broadcasted_iota(jnp.int32, (tm, tm), 1)
        tri[...] = (r < c).astype(BF16)
        cnt_carry[...] = jnp.zeros_like(cnt_carry)

    @pl.when(st == 0)
    def _():
        ext_pool[0:POOL_HALO, :] = jnp.zeros((POOL_HALO, POOL_WIDTH), F32)
        ext_conv[0:CONV_HALO, :] = jnp.zeros((CONV_HALO, CONV_WIDTH), F32)

    x = x_ref[...]
    hb = _rms(x, g_mix_ref[...]).astype(BF16)

    o0 = POOL_WIDTH
    o1 = o0 + CONV_WIDTH
    o2 = o1 + CONV_WIDTH
    o3 = o2 + CONV_WIDTH

    u = _dot(hb, w_in_ref[:, 0:o0])
    ext_pool[POOL_HALO:POOL_HALO + tm, :] = u
    gc = _dot(hb, w_in_ref[:, o1:o2])
    v = _dot(hb, w_in_ref[:, o2:o3])
    pre_a = _dot(hb, w_in_ref[:, o3:o3 + D_MODEL])

    t_glob = st * tm + lax.broadcasted_iota(jnp.int32, (tm, 1), 0)
    mixed = []
    for gi, w in enumerate(POOL_WINDOWS):
        cols = slice(gi * POOL_GROUP, (gi + 1) * POOL_GROUP)
        ug = u[:, cols]
        acc = ug
        for j in range(1, w):
            acc = acc + ext_pool[POOL_HALO - j:POOL_HALO - j + tm, cols]
        cnt = jnp.minimum(t_glob + 1, w).astype(F32)
        pooled = acc * (1.0 / cnt) - ug
        mixed.append(_dot(pooled.astype(BF16), w_grp_ref[gi]))
    ext_pool[0:POOL_HALO, :] = ext_pool[tm:tm + POOL_HALO, :]
    pre_b = _dot(hb, w_in_ref[:, o3 + D_MODEL:o3 + 2 * D_MODEL])
    gb = _dot(hb, w_in_ref[:, o0:o1])

    cv = gc * v
    ext_conv[CONV_HALO:CONV_HALO + tm, :] = cv
    conv = (ext_conv[CONV_HALO - 2:CONV_HALO - 2 + tm, :] * conv_w_ref[0:1, :]
            + ext_conv[CONV_HALO - 1:CONV_HALO - 1 + tm, :] * conv_w_ref[1:2, :]
            + cv * conv_w_ref[2:3, :])
    ext_conv[0:CONV_HALO, :] = ext_conv[tm:tm + CONV_HALO, :]
    branch_b = _dot((gb * conv).astype(BF16), w_co_ref[...])
    mixed = jnp.concatenate(mixed, axis=1) * pool_scale_ref[...]
    branch_a = _dot(mixed.astype(BF16), w_po_ref[...])

    merged = (_sigmoid(pre_a + b_gate_ref[:, 0:D_MODEL]) * branch_a
              + _sigmoid(pre_b + b_gate_ref[:, D_MODEL:2 * D_MODEL]) * branch_b)
    x1 = x + _dot(merged.astype(BF16), w_o_ref[...])

    h2 = _rms(x1, g_ffn_ref[...])
    h2p_ref[...] = _pack_pair(h2[:, 0:HALF], h2[:, HALF:D_MODEL])
    h2b = h2.astype(BF16)

    nt = (((1,), (1,)), ((), ()))
    parts = lax.dot_general(wr_ref[...], h2b, nt, preferred_element_type=F32)
    logits = parts[0:N_EXPERTS, :] + parts[N_EXPERTS:2 * N_EXPERTS, :]
    sg = _dot(h2b, s_gate_ref[...])
    su = _dot(h2b, s_up_ref[...])
    scores = jax.nn.sigmoid(logits)
    sel = scores + rbias_ref[...]
    eidx = lax.broadcasted_iota(jnp.int32, (N_EXPERTS, tm), 0).astype(F32)
    e_rows, w_rows = [], []
    mask = jnp.zeros((N_EXPERTS, tm), F32)
    for _ in range(TOP_K):
        m = jnp.max(sel, axis=0, keepdims=True)
        ek = jnp.min(jnp.where(sel == m, eidx, float(N_EXPERTS)), axis=0, keepdims=True)
        oh = eidx == ek
        w_rows.append(jnp.sum(jnp.where(oh, scores, 0.0), axis=0, keepdims=True))
        e_rows.append(ek)
        mask = mask + oh.astype(F32)
        sel = jnp.where(oh, -jnp.inf, sel)

    shared = _dot((sg * _sigmoid(sg) * su).astype(BF16), s_down_ref[...])
    xres_ref[...] = x1 + shared

    wsum = w_rows[0]
    for k in range(1, TOP_K):
        wsum = wsum + w_rows[k]

    before = _dot(mask.astype(BF16), tri[...]) + cnt_carry[...]
    for k in range(TOP_K):
        oh = eidx == e_rows[k]
        rank_ref[k:k + 1, :] = jnp.sum(jnp.where(oh, before, 0.0), axis=0,
                                       keepdims=True).astype(jnp.int32)
        idx_ref[k:k + 1, :] = e_rows[k].astype(jnp.int32)
        wsel_ref[k:k + 1, :] = w_rows[k] / wsum * ROUTED_SCALE
    total = cnt_carry[...] + jnp.sum(mask, axis=1, keepdims=True)
    cnt_carry[...] = total
    counts_ref[...] = total.astype(jnp.int32)


def _mixer_router(x2d, tok0, t_tok, seq_len, g_mix, w_in, b_gate, w_grp, pool_scale, w_po, conv_w,
                  w_co, w_o, g_ffn, wr, rbias, s_gate, s_up, s_down):
    tm = TM_MIX
    n_seq_tiles = seq_len // tm
    off = tok0 // tm
    const = lambda shape: pl.BlockSpec(shape, lambda i: (0,) * len(shape),
                                       pipeline_mode=pl.Buffered(1))
    hbm = pl.BlockSpec(memory_space=pl.ANY)
    x_blk = pl.BlockSpec((tm, D_MODEL), lambda i: (i + off, 0))
    row_blk = pl.BlockSpec((tm, D_MODEL), lambda i: (i, 0))
    half_blk = pl.BlockSpec((tm, HALF), lambda i: (i, 0))
    slot_blk = pl.BlockSpec((TOP_K, tm), lambda i: (0, i))
    return pl.pallas_call(
        functools.partial(_mixer_router_kernel, n_seq_tiles),
        grid=(t_tok // tm,),
        in_specs=[x_blk, const(g_mix.shape), hbm, const(b_gate.shape),
                  hbm, const(pool_scale.shape), hbm,
                  const(conv_w.shape), hbm, hbm, const(g_ffn.shape),
                  const(wr.shape), const(rbias.shape), hbm, hbm, hbm],
        out_specs=[row_blk, half_blk, slot_blk, slot_blk, slot_blk,
                   pl.BlockSpec((N_EXPERTS, 1), lambda i: (0, 0))],
        out_shape=[jax.ShapeDtypeStruct((t_tok, D_MODEL), F32),
                   jax.ShapeDtypeStruct((t_tok, HALF), U32),
                   jax.ShapeDtypeStruct((TOP_K, t_tok), jnp.int32),
                   jax.ShapeDtypeStruct((TOP_K, t_tok), F32),
                   jax.ShapeDtypeStruct((TOP_K, t_tok), jnp.int32),
                   jax.ShapeDtypeStruct((N_EXPERTS, 1), jnp.int32)],
        scratch_shapes=[pltpu.VMEM((POOL_HALO + tm, POOL_WIDTH), F32),
                        pltpu.VMEM((CONV_HALO + tm, CONV_WIDTH), F32),
                        pltpu.VMEM((N_EXPERTS, 1), F32),
                        pltpu.VMEM((tm, tm), BF16)]
                       + [pltpu.VMEM(w.shape, BF16)
                          for w in (w_in, w_grp, w_po, w_co, w_o, s_gate, s_up, s_down)]
                       + [pltpu.VMEM((2, W_IN_CHUNK, w_in.shape[1]), F32),
                          pltpu.VMEM(w_o.shape, F32),
                          pltpu.VMEM((2,) + s_gate.shape, F32),
                          pltpu.VMEM(w_grp.shape, F32),
                          pltpu.SemaphoreType.DMA((7,))],
        compiler_params=pltpu.CompilerParams(dimension_semantics=("arbitrary",),
                                             vmem_limit_bytes=VMEM_LIMIT),
        name="mixer_router",
    )(x2d, g_mix, w_in, b_gate, w_grp, pool_scale, w_po, conv_w, w_co, w_o, g_ffn,
      wr, rbias, s_gate, s_up, s_down)


def _plan_kernel(n_blk, n_pad, counts_ref, idx_ref, rank_ref,
                 dest_ref, blk_e_ref, next_e_ref, n_used_ref, zero_rows_ref, pad_start):
    e_n = N_EXPERTS

    @pl.when(pl.program_id(0) == 0)
    def _():
        sub = lax.broadcasted_iota(jnp.int32, (e_n, e_n), 0)
        lane = lax.broadcasted_iota(jnp.int32, (e_n, e_n), 1)
        c_col = counts_ref[...]
        p_col = ((c_col + (ROW_BLOCK - 1)) // ROW_BLOCK) * ROW_BLOCK
        c_f = c_col.astype(F32)
        p_f = p_col.astype(F32)
        gap_f = p_f - c_f
        to_row = lambda col: jnp.sum(jnp.where(sub == lane, col, 0.0), axis=0, keepdims=True)
        p_row = to_row(p_f)
        gap_row = to_row(gap_f)
        pad_end_col = jnp.sum(jnp.where(lane <= sub, p_row, 0.0), axis=1, keepdims=True)
        gap_before_col = jnp.sum(jnp.where(lane < sub, gap_row, 0.0), axis=1, keepdims=True)
        pad_start[...] = pad_end_col - p_f
        pad_end_last = jnp.sum(p_row, axis=1, keepdims=True)
        n_used_ref[...] = jnp.broadcast_to(pad_end_last * (1.0 / ROW_BLOCK),
                                           n_used_ref.shape).astype(jnp.int32)
        row0 = (lax.broadcasted_iota(jnp.int32, (e_n, blk_e_ref.shape[1]), 1)
                * ROW_BLOCK).astype(F32)
        owner = jnp.sum(jnp.where(pad_end_col <= row0, 1.0, 0.0), axis=0, keepdims=True)
        blk_e_ref[...] = jnp.minimum(owner, float(e_n - 1)).astype(jnp.int32)
        used_later = jnp.logical_and(sub > lane, p_f > 0.0)
        nxt = jnp.min(jnp.where(used_later, sub, e_n), axis=0, keepdims=True)
        next_e_ref[...] = jnp.full(next_e_ref.shape, -1, jnp.int32)
        next_e_ref[:, 0:e_n] = jnp.where(nxt < e_n, nxt, -1)
        for j0 in range(0, n_pad, PLAN_LANES):
            j = (j0 + lax.broadcasted_iota(jnp.int32, (e_n, PLAN_LANES), 1)).astype(F32)
            before = jnp.sum(jnp.where(gap_before_col <= j, c_f, 0.0), axis=0, keepdims=True)
            zero_rows_ref[:, j0:j0 + PLAN_LANES] = (j[0:1, :] + before).astype(jnp.int32)

    tm = idx_ref.shape[1]
    eidx = lax.broadcasted_iota(jnp.int32, (e_n, tm), 0)
    ps = pad_start[...]
    for k in range(TOP_K):
        oh = eidx == idx_ref[k:k + 1, :]
        start = jnp.sum(jnp.where(oh, ps, 0.0), axis=0, keepdims=True)
        dest_ref[k:k + 1, :] = start.astype(jnp.int32) + rank_ref[k:k + 1, :]


def _plan(counts, idx_t, rank_t, n_blk, n_pad):
    t_tok = idx_t.shape[1]
    lanes = 128
    n_blk_p = -(-n_blk // lanes) * lanes
    slot_blk = pl.BlockSpec((TOP_K, TM_DEST), lambda i: (0, i))
    whole = lambda n: pl.BlockSpec((1, n), lambda i: (0, 0))
    dest_t, blk_e, next_e, n_used, zero_rows = pl.pallas_call(
        functools.partial(_plan_kernel, n_blk, n_pad),
        grid=(t_tok // TM_DEST,),
        in_specs=[pl.BlockSpec((N_EXPERTS, 1), lambda i: (0, 0)), slot_blk, slot_blk],
        out_specs=[slot_blk, whole(n_blk_p), whole(lanes), whole(lanes), whole(n_pad)],
        out_shape=[jax.ShapeDtypeStruct((TOP_K, t_tok), jnp.int32),
                   jax.ShapeDtypeStruct((1, n_blk_p), jnp.int32),
                   jax.ShapeDtypeStruct((1, lanes), jnp.int32),
                   jax.ShapeDtypeStruct((1, lanes), jnp.int32),
                   jax.ShapeDtypeStruct((1, n_pad), jnp.int32)],
        scratch_shapes=[pltpu.VMEM((N_EXPERTS, 1), F32)],
        compiler_params=pltpu.CompilerParams(dimension_semantics=("arbitrary",)),
        name="plan",
    )(counts, idx_t, rank_t)
    return (dest_t, blk_e.reshape(-1), next_e.reshape(-1), n_used.reshape(-1),
            zero_rows.reshape(-1))


def _sc_mesh():
    return plsc.VectorSubcoreMesh(core_axis_name="c", subcore_axis_name="s")


def _sc_worker_id():
    return lax.axis_index("s") * SC_CORES + lax.axis_index("c")


def _dispatch(h2p, dest_t, zero_rows, n_rows):
    t_tok, width = h2p.shape
    per_w = t_tok // SC_WORKERS
    n_chunks = per_w // SC_CHUNK
    z_chunks = zero_rows.shape[0] // (SC_WORKERS * SC_CHUNK)
    dest_w = (dest_t.reshape(TOP_K, SC_WORKERS, n_chunks, SC_CHUNK)
              .transpose(1, 2, 0, 3).reshape(SC_WORKERS * n_chunks * TOP_K, SC_CHUNK))
    zero_w = zero_rows.reshape(SC_WORKERS * z_chunks, SC_CHUNK)

    @functools.partial(
        pl.kernel, mesh=_sc_mesh(),
        out_type=jax.ShapeDtypeStruct((n_rows, width), h2p.dtype),
        scratch_types=[pltpu.VMEM((n_chunks * TOP_K, SC_CHUNK), jnp.int32),
                       pltpu.VMEM((z_chunks, SC_CHUNK), jnp.int32),
                       pltpu.VMEM((2, SC_CHUNK, width), h2p.dtype),
                       pltpu.VMEM((SC_CHUNK, width), h2p.dtype),
                       pltpu.SemaphoreType.DMA, pltpu.SemaphoreType.DMA,
                       pltpu.SemaphoreType.DMA],
        name="dispatch",
    )
    def k(h2p_hbm, dest_hbm, zidx_hbm, zsrc_hbm, xs_hbm, idx_v, zidx_v, rows_v, zero_v,
          gsem, wsem, zsem):
        wid = _sc_worker_id()
        base = wid * per_w
        pltpu.sync_copy(dest_hbm.at[pl.ds(wid * n_chunks * TOP_K, n_chunks * TOP_K)], idx_v)
        pltpu.sync_copy(zidx_hbm.at[pl.ds(wid * z_chunks, z_chunks)], zidx_v)
        pltpu.sync_copy(zsrc_hbm, zero_v)

        def zput(j):
            return pltpu.make_async_copy(zero_v, xs_hbm.at[zidx_v.at[j]], zsem)

        for j in range(z_chunks):
            zput(j).start()

        def get(j, slot):
            return pltpu.make_async_copy(h2p_hbm.at[pl.ds(base + j * SC_CHUNK, SC_CHUNK)],
                                         rows_v.at[slot], gsem)

        def put(j, slot, kk):
            return pltpu.make_async_copy(rows_v.at[slot], xs_hbm.at[idx_v.at[j * TOP_K + kk]], wsem)

        get(0, 0).start()

        @pl.loop(0, n_chunks, step=2)
        def _(j):
            for b in range(2):
                jj = j + b
                get(jj, b).wait()

                @pl.when(jj >= 1)
                def _():
                    for kk in range(TOP_K):
                        put(jj - 1, 1 - b, kk).wait()

                @pl.when(jj + 1 < n_chunks)
                def _():
                    get(jj + 1, 1 - b).start()
                for kk in range(TOP_K):
                    put(jj, b, kk).start()

        for kk in range(TOP_K):
            put(n_chunks - 1, (n_chunks - 1) % 2, kk).wait()
        for j in range(z_chunks):
            zput(j).wait()

    return k(h2p, dest_w, zero_w, jnp.zeros((SC_CHUNK, width), h2p.dtype))


def _regroup(ys, dest_t):
    width = ys.shape[1]
    n_slots, t_tok = dest_t.shape
    m_rows = n_slots * t_tok
    per_w = m_rows // SC_WORKERS
    n_chunks = per_w // SC_CHUNK
    idx2 = dest_t.reshape(SC_WORKERS * n_chunks, SC_CHUNK)

    @functools.partial(
        pl.kernel, mesh=_sc_mesh(),
        out_type=jax.ShapeDtypeStruct((m_rows, width), ys.dtype),
        scratch_types=[pltpu.VMEM((n_chunks, SC_CHUNK), jnp.int32),
                       pltpu.VMEM((2, SC_CHUNK, width), ys.dtype),
                       pltpu.SemaphoreType.DMA, pltpu.SemaphoreType.DMA],
        name="regroup",
    )
    def k(ys_hbm, idx_hbm, out_hbm, idx_v, rows_v, gsem, wsem):
        wid = _sc_worker_id()
        base = wid * per_w
        pltpu.sync_copy(idx_hbm.at[pl.ds(wid * n_chunks, n_chunks)], idx_v)

        def get(j, slot):
            return pltpu.make_async_copy(ys_hbm.at[idx_v.at[j]], rows_v.at[slot], gsem)

        def put(j, slot):
            return pltpu.make_async_copy(rows_v.at[slot],
                                         out_hbm.at[pl.ds(base + j * SC_CHUNK, SC_CHUNK)], wsem)

        get(0, 0).start()

        @pl.loop(0, n_chunks, step=2)
        def _(j):
            for b in range(2):
                jj = j + b
                get(jj, b).wait()

                @pl.when(jj >= 1)
                def _():
                    put(jj - 1, 1 - b).wait()

                @pl.when(jj + 1 < n_chunks)
                def _():
                    get(jj + 1, 1 - b).start()
                put(jj, b).start()

        put(n_chunks - 1, (n_chunks - 1) % 2).wait()

    return k(ys, idx2).reshape(n_slots, t_tok, width)


def _experts_kernel(n_blk, blk_e_ref, next_e_ref, n_used_ref,
                    xs_hbm, eg_hbm, eu_hbm, ed_hbm, ys_hbm,
                    xbuf, ybuf, hid, stg_g, stg_u, stg_d, wg, wu, wd, xsem, ysem, wsem):
    n = n_used_ref[0]

    def ring(b):
        return jnp.bitwise_and(b, ROW_RING - 1)

    def x_copy(b):
        return pltpu.make_async_copy(xs_hbm.at[pl.ds(pl.multiple_of(b * ROW_BLOCK, ROW_BLOCK),
                                                     ROW_BLOCK)], xbuf.at[ring(b)], xsem.at[ring(b)])

    def y_copy(b):
        return pltpu.make_async_copy(ybuf.at[ring(b)],
                                     ys_hbm.at[pl.ds(pl.multiple_of(b * ROW_BLOCK, ROW_BLOCK),
                                                     ROW_BLOCK)], ysem.at[ring(b)])

    def w_copies(e):
        return (pltpu.make_async_copy(eg_hbm.at[e], stg_g, wsem.at[0]),
                pltpu.make_async_copy(eu_hbm.at[e], stg_u, wsem.at[1]),
                pltpu.make_async_copy(ed_hbm.at[e], stg_d, wsem.at[2]))

    def switch_expert(e, wslot):
        for cp in w_copies(e):
            cp.wait()
        wg[wslot] = stg_g[...].astype(BF16)
        wu[wslot] = stg_u[...].astype(BF16)
        wd[wslot] = stg_d[...].astype(BF16)
        nxt = next_e_ref[e]

        @pl.when(nxt >= 0)
        def _():
            for cp in w_copies(nxt):
                cp.start()

    def gate_up(b, wslot):
        xa, xb = _unpack_pair(xbuf[ring(b)])
        xb16 = jnp.concatenate([xa, xb], axis=1).astype(BF16)
        g = _dot(xb16, wg[wslot])
        up = _dot(xb16, wu[wslot])
        hid[jnp.bitwise_and(b, 1)] = (g * _sigmoid(g) * up).astype(BF16)

    def down(b, wslot):
        y = _dot(hid[jnp.bitwise_and(b, 1)], wd[wslot])
        ybuf[ring(b)] = _pack_pair(y[:, 0:HALF], y[:, HALF:D_MODEL])

    e0 = blk_e_ref[0]
    for cp in w_copies(e0):
        cp.start()
    for j in range(ROW_RING):
        @pl.when(j < n)
        def _():
            x_copy(j).start()
    switch_expert(e0, 0)
    x_copy(0).wait()
    gate_up(0, 0)

    def body(b, wslot_prev):
        e = blk_e_ref[b]
        first = e != blk_e_ref[b - 1]
        wslot = jnp.where(first, 1 - wslot_prev, wslot_prev)

        @pl.when(first)
        def _():
            switch_expert(e, wslot)

        x_copy(b).wait()

        @pl.when(b + ROW_RING - 1 < n)
        def _():
            x_copy(b + ROW_RING - 1).start()

        @pl.when(b >= ROW_RING + 1)
        def _():
            y_copy(b - 1 - ROW_RING).wait()

        down(b - 1, wslot_prev)
        gate_up(b, wslot)
        y_copy(b - 1).start()
        return wslot

    wslot_last = lax.fori_loop(1, n, body, jnp.int32(0))

    last = n - 1

    @pl.when(last >= ROW_RING)
    def _():
        y_copy(last - ROW_RING).wait()
    down(last, wslot_last)
    y_copy(last).start()
    for j in range(ROW_RING - 1, -1, -1):
        @pl.when(last - j >= 0)
        def _():
            y_copy(last - j).wait()

    ybuf[0] = jnp.zeros((ROW_BLOCK, HALF), U32)

    def zero_tail(b, c):
        cp = pltpu.make_async_copy(ybuf.at[0],
                                   ys_hbm.at[pl.ds(pl.multiple_of(b * ROW_BLOCK, ROW_BLOCK),
                                                   ROW_BLOCK)], ysem.at[0])
        cp.start()
        cp.wait()
        return c
    lax.fori_loop(n, n_blk, zero_tail, 0)


def _experts(blk_e, next_e, n_used, xs, e_gate, e_up, e_down):
    n_rows = xs.shape[0]
    n_blk = n_rows // ROW_BLOCK
    any_spec = pl.BlockSpec(memory_space=pl.ANY)
    grid_spec = pltpu.PrefetchScalarGridSpec(
        num_scalar_prefetch=3,
        grid=(1,),
        in_specs=[any_spec, any_spec, any_spec, any_spec],
        out_specs=any_spec,
        scratch_shapes=[pltpu.VMEM((ROW_RING, ROW_BLOCK, HALF), U32),
                        pltpu.VMEM((ROW_RING, ROW_BLOCK, HALF), U32),
                        pltpu.VMEM((2, ROW_BLOCK, EXPERT_HIDDEN), BF16),
                        pltpu.VMEM((D_MODEL, EXPERT_HIDDEN), F32),
                        pltpu.VMEM((D_MODEL, EXPERT_HIDDEN), F32),
                        pltpu.VMEM((EXPERT_HIDDEN, D_MODEL), F32),
                        pltpu.VMEM((2, D_MODEL, EXPERT_HIDDEN), BF16),
                        pltpu.VMEM((2, D_MODEL, EXPERT_HIDDEN), BF16),
                        pltpu.VMEM((2, EXPERT_HIDDEN, D_MODEL), BF16),
                        pltpu.SemaphoreType.DMA((ROW_RING,)),
                        pltpu.SemaphoreType.DMA((ROW_RING,)),
                        pltpu.SemaphoreType.DMA((3,))],
    )
    return pl.pallas_call(
        functools.partial(_experts_kernel, n_blk),
        grid_spec=grid_spec,
        out_shape=jax.ShapeDtypeStruct((n_rows, HALF), U32),
        compiler_params=pltpu.CompilerParams(dimension_semantics=("arbitrary",)),
        name="experts",
    )(blk_e, next_e, n_used, xs, e_gate, e_up, e_down)


def _combine_kernel(yg_ref, wsel_ref, xres_ref, g_final_ref, *rest):
    out_ref = rest[-1]
    xres = xres_ref[...]
    acc_a = xres[:, 0:HALF]
    acc_b = xres[:, HALF:D_MODEL]
    wsel = wsel_ref[...].T
    for k in range(TOP_K):
        ya, yb = _unpack_pair(yg_ref[k])
        wk = wsel[:, k:k + 1]
        acc_a = acc_a + ya * wk
        acc_b = acc_b + yb * wk
    out_ref[...] = _rms(jnp.concatenate([acc_a, acc_b], axis=1), g_final_ref[...])


def _combine(yg, wsel_t, xres, g_final, out_prev, tok_local, tok_out, t_out):
    tm = TM_COMB
    steps = yg.shape[1] // tm
    off = tok_local // tm
    off_out = tok_out // tm
    in_specs = [pl.BlockSpec((TOP_K, tm, HALF), lambda i: (0, i, 0)),
                pl.BlockSpec((TOP_K, tm), lambda i: (0, i + off)),
                pl.BlockSpec((tm, D_MODEL), lambda i: (i + off, 0)),
                pl.BlockSpec((1, D_MODEL), lambda i: (0, 0))]
    args = [yg, wsel_t, xres, g_final]
    aliases = {}
    if out_prev is not None:
        in_specs.append(pl.BlockSpec(memory_space=pl.ANY))
        args.append(out_prev)
        aliases = {4: 0}
    return pl.pallas_call(
        _combine_kernel,
        grid=(steps,),
        in_specs=in_specs,
        out_specs=pl.BlockSpec((tm, D_MODEL), lambda i: (i + off_out, 0)),
        out_shape=jax.ShapeDtypeStruct((t_out, D_MODEL), F32),
        input_output_aliases=aliases,
        compiler_params=pltpu.CompilerParams(dimension_semantics=("arbitrary",),
                                             vmem_limit_bytes=VMEM_LIMIT),
        name="combine",
    )(*args)


def kernel(x, g_mix, w_in, b_gate, w_pool_group, pool_scale, w_pool_out, conv_w, w_conv_out, w_o,
           g_ffn, w_router, router_bias, e_gate, e_up, e_down, s_gate, s_up, s_down, g_final):
    b, s, d = x.shape
    t_tok = b * s
    t_split = t_tok // N_SPLITS
    t_chunk = t_split // N_COMB_CHUNKS
    n_pad = N_EXPERTS * ROW_BLOCK
    assert d == D_MODEL and s % TM_MIX == 0 and TM_MIX >= POOL_HALO
    assert b % N_SPLITS == 0 and t_split % TM_DEST == 0 and t_chunk % TM_COMB == 0
    assert (t_chunk * TOP_K) % (2 * SC_WORKERS * SC_CHUNK) == 0
    assert t_split % (2 * SC_WORKERS * SC_CHUNK) == 0 and n_pad % (SC_WORKERS * SC_CHUNK) == 0
    assert n_pad % PLAN_LANES == 0

    row = lambda a: a.reshape(1, -1)
    wr_t = w_router.T.astype(F32)
    wr_hi = wr_t.astype(BF16)
    wr = jnp.concatenate([wr_hi, (wr_t - wr_hi.astype(F32)).astype(BF16)], axis=0)
    mixer_weights = (row(g_mix), w_in, row(b_gate), w_pool_group, row(pool_scale), w_pool_out,
                     conv_w, w_conv_out, w_o, row(g_ffn), wr,
                     router_bias.astype(F32).reshape(N_EXPERTS, 1), s_gate, s_up, s_down)
    x2d = x.reshape(t_tok, d)
    n_rows = t_split * TOP_K + n_pad
    n_blk = n_rows // ROW_BLOCK

    out = None
    for sp in range(N_SPLITS):
        tok0 = sp * t_split
        xres, h2p, idx_t, wsel_t, rank_t, counts = _mixer_router(x2d, tok0, t_split, s,
                                                                 *mixer_weights)

        dest_t, blk_e, next_e, n_used, zero_rows = _plan(counts, idx_t, rank_t, n_blk, n_pad)
        xs = _dispatch(h2p, dest_t, zero_rows, n_rows)
        ys = _experts(blk_e, next_e, n_used, xs, e_gate, e_up, e_down)
        for c in range(N_COMB_CHUNKS):
            yg = _regroup(ys, dest_t[:, c * t_chunk:(c + 1) * t_chunk])
            out = _combine(yg, wsel_t, xres, row(g_final), out, c * t_chunk, tok0 + c * t_chunk,
                           t_tok)
    return out.reshape(b, s, d)
```

```python
import functools

import jax
import jax.numpy as jnp
from jax import lax
from jax.experimental import pallas as pl
from jax.experimental.pallas import tpu as pltpu
from jax.experimental.pallas import tpu_sc as plsc

D_MODEL = 1024
HALF = D_MODEL // 2
POOL_WIDTH = 512
N_POOL_GROUPS = 4
POOL_GROUP = 128
POOL_WINDOWS = (2, 4, 8, 16)
CONV_WIDTH = 512
N_EXPERTS = 64
TOP_K = 8
EXPERT_HIDDEN = 256
SHARED_HIDDEN = 256
ROUTED_SCALE = 2.5
EPS = 1e-6

POOL_HALO = 16
CONV_HALO = 8
TM_MIX = 512
W_IN_CHUNK = 128
TM_DEST = 2048
PLAN_LANES = 2048
ROW_BLOCK = 512
ROW_RING = 8
TM_NORM = 512
VMEM_LIMIT = 56 * 1024 * 1024

SC_CORES = 2
SC_SUBCORES = 16
SC_WORKERS = SC_CORES * SC_SUBCORES
SC_LANES = 16
SC_CHUNK = 64
SC_SUM_GROUP = 4
SC_SUM_RING = 4

BF16 = jnp.bfloat16
F32 = jnp.float32
U32 = jnp.uint32


def _rms(x, g):
    r = lax.rsqrt(jnp.mean(x * x, axis=-1, keepdims=True) + EPS)
    return (x * r) * g


def _dot(a, b):
    return jnp.dot(a, b, preferred_element_type=F32)


def _sigmoid(z):
    return 0.5 * jnp.tanh(0.5 * z) + 0.5


def _pack_pair(a, b):
    ra = lax.bitcast_convert_type(a.astype(BF16).astype(F32), U32)
    rb = lax.bitcast_convert_type(b.astype(BF16).astype(F32), U32)
    return ra | (rb >> 16)


def _unpack_pair(w):
    a = lax.bitcast_convert_type(w & jnp.uint32(0xFFFF0000), F32)
    b = lax.bitcast_convert_type(w << 16, F32)
    return a, b


def _load_weights_bf16(w_in_hbm, w_grp_hbm, w_po_hbm, w_co_hbm, w_o_hbm, s_gate_hbm, s_up_hbm,
                       s_down_hbm, w_in_ref, w_grp_ref, w_po_ref, w_co_ref, w_o_ref, s_gate_ref,
                       s_up_ref, s_down_ref, stg_in, stg_sq, stg_sh, stg_grp, wsem):
    copy = lambda src, dst, k: pltpu.make_async_copy(src, dst, wsem.at[k])
    rows = stg_in.shape[1]
    n_in = w_in_hbm.shape[0] // rows
    half = w_po_hbm.shape[0]
    c_in = [copy(w_in_hbm.at[pl.ds(j * rows, rows)], stg_in.at[j % 2], j % 2) for j in range(n_in)]
    c_o = copy(w_o_hbm, stg_sq, 2)
    c_sg = copy(s_gate_hbm, stg_sh.at[0], 3)
    c_su = copy(s_up_hbm, stg_sh.at[1], 4)
    c_grp = copy(w_grp_hbm, stg_grp, 5)
    c_po = copy(w_po_hbm, stg_sq.at[0:half], 2)
    c_co = copy(w_co_hbm, stg_sq.at[half:2 * half], 6)
    c_sd = copy(s_down_hbm, stg_sq.at[0:s_down_hbm.shape[0]], 2)
    for cp in (c_in[0], c_in[1], c_o, c_sg, c_su, c_grp):
        cp.start()
    for j in range(n_in):
        c_in[j].wait()
        w_in_ref[j * rows:(j + 1) * rows, :] = stg_in[j % 2].astype(BF16)
        if j + 2 < n_in:
            c_in[j + 2].start()
    c_o.wait()
    w_o_ref[...] = stg_sq[...].astype(BF16)
    c_po.start()
    c_co.start()
    c_sg.wait()
    s_gate_ref[...] = stg_sh[0].astype(BF16)
    c_su.wait()
    s_up_ref[...] = stg_sh[1].astype(BF16)
    c_grp.wait()
    w_grp_ref[...] = stg_grp[...].astype(BF16)
    c_po.wait()
    w_po_ref[...] = stg_sq[0:half, :].astype(BF16)
    c_co.wait()
    w_co_ref[...] = stg_sq[half:2 * half, :].astype(BF16)
    c_sd.start()
    c_sd.wait()
    s_down_ref[...] = stg_sq[0:s_down_hbm.shape[0], :].astype(BF16)


def _mixer_router_kernel(n_seq_tiles,
                         x_ref, g_mix_ref, w_in_hbm, b_gate_ref, w_grp_hbm, pool_scale_ref,
                         w_po_hbm, conv_w_ref, w_co_hbm, w_o_hbm, g_ffn_ref,
                         wr_ref, rbias_ref, s_gate_hbm, s_up_hbm, s_down_hbm,
                         xres_ref, h2p_ref, idx_ref, wsel_ref, rank_ref, counts_ref,
                         ext_pool, ext_conv, cnt_carry, tri,
                         w_in_ref, w_grp_ref, w_po_ref, w_co_ref, w_o_ref, s_gate_ref, s_up_ref,
                         s_down_ref, stg_in, stg_sq, stg_sh, stg_grp, wsem):
    tm = x_ref.shape[0]
    i = pl.program_id(0)
    st = i % n_seq_tiles

    @pl.when(i == 0)
    def _():
        _load_weights_bf16(w_in_hbm, w_grp_hbm, w_po_hbm, w_co_hbm, w_o_hbm, s_gate_hbm, s_up_hbm,
                           s_down_hbm, w_in_ref, w_grp_ref, w_po_ref, w_co_ref, w_o_ref,
                           s_gate_ref, s_up_ref, s_down_ref, stg_in, stg_sq, stg_sh, stg_grp, wsem)
        r = lax.broadcasted_iota(jnp.int32, (tm, tm), 0)
        c = lax.broadcasted_iota(jnp.int32, (tm, tm), 1)
        tri[...] = (r < c).astype(BF16)
        cnt_carry[...] = jnp.zeros_like(cnt_carry)

    @pl.when(st == 0)
    def _():
        ext_pool[0:POOL_HALO, :] = jnp.zeros((POOL_HALO, POOL_WIDTH), F32)
        ext_conv[0:CONV_HALO, :] = jnp.zeros((CONV_HALO, CONV_WIDTH), F32)

    x = x_ref[...]
    hb = _rms(x, g_mix_ref[...]).astype(BF16)

    o0 = POOL_WIDTH
    o1 = o0 + CONV_WIDTH
    o2 = o1 + CONV_WIDTH
    o3 = o2 + CONV_WIDTH

    u = _dot(hb, w_in_ref[:, 0:o0])
    ext_pool[POOL_HALO:POOL_HALO + tm, :] = u
    gc = _dot(hb, w_in_ref[:, o1:o2])
    v = _dot(hb, w_in_ref[:, o2:o3])
    pre_a = _dot(hb, w_in_ref[:, o3:o3 + D_MODEL])

    t_glob = st * tm + lax.broadcasted_iota(jnp.int32, (tm, 1), 0)
    mixed = []
    for gi, w in enumerate(POOL_WINDOWS):
        cols = slice(gi * POOL_GROUP, (gi + 1) * POOL_GROUP)
        ug = u[:, cols]
        acc = ug
        for j in range(1, w):
            acc = acc + ext_pool[POOL_HALO - j:POOL_HALO - j + tm, cols]
        cnt = jnp.minimum(t_glob + 1, w).astype(F32)
        pooled = acc * (1.0 / cnt) - ug
        mixed.append(_dot(pooled.astype(BF16), w_grp_ref[gi]))
    ext_pool[0:POOL_HALO, :] = ext_pool[tm:tm + POOL_HALO, :]
    pre_b = _dot(hb, w_in_ref[:, o3 + D_MODEL:o3 + 2 * D_MODEL])
    gb = _dot(hb, w_in_ref[:, o0:o1])

    cv = gc * v
    ext_conv[CONV_HALO:CONV_HALO + tm, :] = cv
    conv = (ext_conv[CONV_HALO - 2:CONV_HALO - 2 + tm, :] * conv_w_ref[0:1, :]
            + ext_conv[CONV_HALO - 1:CONV_HALO - 1 + tm, :] * conv_w_ref[1:2, :]
            + cv * conv_w_ref[2:3, :])
    ext_conv[0:CONV_HALO, :] = ext_conv[tm:tm + CONV_HALO, :]
    branch_b = _dot((gb * conv).astype(BF16), w_co_ref[...])
    mixed = jnp.concatenate(mixed, axis=1) * pool_scale_ref[...]
    branch_a = _dot(mixed.astype(BF16), w_po_ref[...])

    merged = (_sigmoid(pre_a + b_gate_ref[:, 0:D_MODEL]) * branch_a
              + _sigmoid(pre_b + b_gate_ref[:, D_MODEL:2 * D_MODEL]) * branch_b)
    x1 = x + _dot(merged.astype(BF16), w_o_ref[...])

    h2 = _rms(x1, g_ffn_ref[...])
    h2p_ref[...] = _pack_pair(h2[:, 0:HALF], h2[:, HALF:D_MODEL])
    h2b = h2.astype(BF16)

    nt = (((1,), (1,)), ((), ()))
    parts = lax.dot_general(wr_ref[...], h2b, nt, preferred_element_type=F32)
    logits = parts[0:N_EXPERTS, :] + parts[N_EXPERTS:2 * N_EXPERTS, :]
    sg = _dot(h2b, s_gate_ref[...])
    su = _dot(h2b, s_up_ref[...])
    scores = jax.nn.sigmoid(logits)
    sel = scores + rbias_ref[...]
    eidx = lax.broadcasted_iota(jnp.int32, (N_EXPERTS, tm), 0).astype(F32)
    e_rows, w_rows = [], []
    mask = jnp.zeros((N_EXPERTS, tm), F32)
    for _ in range(TOP_K):
        m = jnp.max(sel, axis=0, keepdims=True)
        ek = jnp.min(jnp.where(sel == m, eidx, float(N_EXPERTS)), axis=0, keepdims=True)
        oh = eidx == ek
        w_rows.append(jnp.sum(jnp.where(oh, scores, 0.0), axis=0, keepdims=True))
        e_rows.append(ek)
        mask = mask + oh.astype(F32)
        sel = jnp.where(oh, -jnp.inf, sel)

    shared = _dot((sg * _sigmoid(sg) * su).astype(BF16), s_down_ref[...])
    xres_ref[...] = x1 + shared

    wsum = w_rows[0]
    for k in range(1, TOP_K):
        wsum = wsum + w_rows[k]

    before = _dot(mask.astype(BF16), tri[...]) + cnt_carry[...]
    for k in range(TOP_K):
        oh = eidx == e_rows[k]
        rank_ref[k:k + 1, :] = jnp.sum(jnp.where(oh, before, 0.0), axis=0,
                                       keepdims=True).astype(jnp.int32)
        idx_ref[k:k + 1, :] = e_rows[k].astype(jnp.int32)
        wsel_ref[k:k + 1, :] = w_rows[k] / wsum * ROUTED_SCALE
    total = cnt_carry[...] + jnp.sum(mask, axis=1, keepdims=True)
    cnt_carry[...] = total
    counts_ref[...] = total.astype(jnp.int32)


def _mixer_router(x2d, tok0, t_tok, seq_len, g_mix, w_in, b_gate, w_grp, pool_scale, w_po, conv_w,
                  w_co, w_o, g_ffn, wr, rbias, s_gate, s_up, s_down):
    tm = TM_MIX
    n_seq_tiles = seq_len // tm
    off = tok0 // tm
    const = lambda shape: pl.BlockSpec(shape, lambda i: (0,) * len(shape),
                                       pipeline_mode=pl.Buffered(1))
    hbm = pl.BlockSpec(memory_space=pl.ANY)
    x_blk = pl.BlockSpec((tm, D_MODEL), lambda i: (i + off, 0))
    row_blk = pl.BlockSpec((tm, D_MODEL), lambda i: (i, 0))
    half_blk = pl.BlockSpec((tm, HALF), lambda i: (i, 0))
    slot_blk = pl.BlockSpec((TOP_K, tm), lambda i: (0, i))
    return pl.pallas_call(
        functools.partial(_mixer_router_kernel, n_seq_tiles),
        grid=(t_tok // tm,),
        in_specs=[x_blk, const(g_mix.shape), hbm, const(b_gate.shape),
                  hbm, const(pool_scale.shape), hbm,
                  const(conv_w.shape), hbm, hbm, const(g_ffn.shape),
                  const(wr.shape), const(rbias.shape), hbm, hbm, hbm],
        out_specs=[row_blk, half_blk, slot_blk, slot_blk, slot_blk,
                   pl.BlockSpec((N_EXPERTS, 1), lambda i: (0, 0))],
        out_shape=[jax.ShapeDtypeStruct((t_tok, D_MODEL), F32),
                   jax.ShapeDtypeStruct((t_tok, HALF), U32),
                   jax.ShapeDtypeStruct((TOP_K, t_tok), jnp.int32),
                   jax.ShapeDtypeStruct((TOP_K, t_tok), F32),
                   jax.ShapeDtypeStruct((TOP_K, t_tok), jnp.int32),
                   jax.ShapeDtypeStruct((N_EXPERTS, 1), jnp.int32)],
        scratch_shapes=[pltpu.VMEM((POOL_HALO + tm, POOL_WIDTH), F32),
                        pltpu.VMEM((CONV_HALO + tm, CONV_WIDTH), F32),
                        pltpu.VMEM((N_EXPERTS, 1), F32),
                        pltpu.VMEM((tm, tm), BF16)]
                       + [pltpu.VMEM(w.shape, BF16)
                          for w in (w_in, w_grp, w_po, w_co, w_o, s_gate, s_up, s_down)]
                       + [pltpu.VMEM((2, W_IN_CHUNK, w_in.shape[1]), F32),
                          pltpu.VMEM(w_o.shape, F32),
                          pltpu.VMEM((2,) + s_gate.shape, F32),
                          pltpu.VMEM(w_grp.shape, F32),
                          pltpu.SemaphoreType.DMA((7,))],
        compiler_params=pltpu.CompilerParams(dimension_semantics=("arbitrary",),
                                             vmem_limit_bytes=VMEM_LIMIT),
        name="mixer_router",
    )(x2d, g_mix, w_in, b_gate, w_grp, pool_scale, w_po, conv_w, w_co, w_o, g_ffn,
      wr, rbias, s_gate, s_up, s_down)


def _plan_kernel(n_blk, n_pad, counts_ref, idx_ref, rank_ref,
                 dest_ref, blk_e_ref, next_e_ref, n_used_ref, zero_rows_ref, pad_start):
    e_n = N_EXPERTS

    @pl.when(pl.program_id(0) == 0)
    def _():
        sub = lax.broadcasted_iota(jnp.int32, (e_n, e_n), 0)
        lane = lax.broadcasted_iota(jnp.int32, (e_n, e_n), 1)
        c_col = counts_ref[...]
        p_col = ((c_col + (ROW_BLOCK - 1)) // ROW_BLOCK) * ROW_BLOCK
        c_f = c_col.astype(F32)
        p_f = p_col.astype(F32)
        gap_f = p_f - c_f
        to_row = lambda col: jnp.sum(jnp.where(sub == lane, col, 0.0), axis=0, keepdims=True)
        p_row = to_row(p_f)
        gap_row = to_row(gap_f)
        pad_end_col = jnp.sum(jnp.where(lane <= sub, p_row, 0.0), axis=1, keepdims=True)
        gap_before_col = jnp.sum(jnp.where(lane < sub, gap_row, 0.0), axis=1, keepdims=True)
        pad_start[...] = pad_end_col - p_f
        pad_end_last = jnp.sum(p_row, axis=1, keepdims=True)
        n_used_ref[...] = jnp.broadcast_to(pad_end_last * (1.0 / ROW_BLOCK),
                                           n_used_ref.shape).astype(jnp.int32)
        row0 = (lax.broadcasted_iota(jnp.int32, (e_n, blk_e_ref.shape[1]), 1)
                * ROW_BLOCK).astype(F32)
        owner = jnp.sum(jnp.where(pad_end_col <= row0, 1.0, 0.0), axis=0, keepdims=True)
        blk_e_ref[...] = jnp.minimum(owner, float(e_n - 1)).astype(jnp.int32)
        used_later = jnp.logical_and(sub > lane, p_f > 0.0)
        nxt = jnp.min(jnp.where(used_later, sub, e_n), axis=0, keepdims=True)
        next_e_ref[...] = jnp.full(next_e_ref.shape, -1, jnp.int32)
        next_e_ref[:, 0:e_n] = jnp.where(nxt < e_n, nxt, -1)
        for j0 in range(0, n_pad, PLAN_LANES):
            j = (j0 + lax.broadcasted_iota(jnp.int32, (e_n, PLAN_LANES), 1)).astype(F32)
            before = jnp.sum(jnp.where(gap_before_col <= j, c_f, 0.0), axis=0, keepdims=True)
            zero_rows_ref[:, j0:j0 + PLAN_LANES] = (j[0:1, :] + before).astype(jnp.int32)

    tm = idx_ref.shape[1]
    eidx = lax.broadcasted_iota(jnp.int32, (e_n, tm), 0)
    ps = pad_start[...]
    for k in range(TOP_K):
        oh = eidx == idx_ref[k:k + 1, :]
        start = jnp.sum(jnp.where(oh, ps, 0.0), axis=0, keepdims=True)
        dest_ref[k:k + 1, :] = start.astype(jnp.int32) + rank_ref[k:k + 1, :]


def _plan(counts, idx_t, rank_t, n_blk, n_pad):
    t_tok = idx_t.shape[1]
    lanes = 128
    n_blk_p = -(-n_blk // lanes) * lanes
    slot_blk = pl.BlockSpec((TOP_K, TM_DEST), lambda i: (0, i))
    whole = lambda n: pl.BlockSpec((1, n), lambda i: (0, 0))
    dest_t, blk_e, next_e, n_used, zero_rows = pl.pallas_call(
        functools.partial(_plan_kernel, n_blk, n_pad),
        grid=(t_tok // TM_DEST,),
        in_specs=[pl.BlockSpec((N_EXPERTS, 1), lambda i: (0, 0)), slot_blk, slot_blk],
        out_specs=[slot_blk, whole(n_blk_p), whole(lanes), whole(lanes), whole(n_pad)],
        out_shape=[jax.ShapeDtypeStruct((TOP_K, t_tok), jnp.int32),
                   jax.ShapeDtypeStruct((1, n_blk_p), jnp.int32),
                   jax.ShapeDtypeStruct((1, lanes), jnp.int32),
                   jax.ShapeDtypeStruct((1, lanes), jnp.int32),
                   jax.ShapeDtypeStruct((1, n_pad), jnp.int32)],
        scratch_shapes=[pltpu.VMEM((N_EXPERTS, 1), F32)],
        compiler_params=pltpu.CompilerParams(dimension_semantics=("arbitrary",)),
        name="plan",
    )(counts, idx_t, rank_t)
    return (dest_t, blk_e.reshape(-1), next_e.reshape(-1), n_used.reshape(-1),
            zero_rows.reshape(-1))


def _sc_mesh():
    return plsc.VectorSubcoreMesh(core_axis_name="c", subcore_axis_name="s")


def _sc_worker_id():
    return lax.axis_index("s") * SC_CORES + lax.axis_index("c")


def _dispatch(h2p, dest_t, zero_rows, n_rows):
    t_tok, width = h2p.shape
    per_w = t_tok // SC_WORKERS
    n_chunks = per_w // SC_CHUNK
    z_chunks = zero_rows.shape[0] // (SC_WORKERS * SC_CHUNK)
    dest_w = (dest_t.reshape(TOP_K, SC_WORKERS, n_chunks, SC_CHUNK)
              .transpose(1, 2, 0, 3).reshape(SC_WORKERS * n_chunks * TOP_K, SC_CHUNK))
    zero_w = zero_rows.reshape(SC_WORKERS * z_chunks, SC_CHUNK)

    @functools.partial(
        pl.kernel, mesh=_sc_mesh(),
        out_type=jax.ShapeDtypeStruct((n_rows, width), h2p.dtype),
        scratch_types=[pltpu.VMEM((n_chunks * TOP_K, SC_CHUNK), jnp.int32),
                       pltpu.VMEM((z_chunks, SC_CHUNK), jnp.int32),
                       pltpu.VMEM((2, SC_CHUNK, width), h2p.dtype),
                       pltpu.VMEM((SC_CHUNK, width), h2p.dtype),
                       pltpu.SemaphoreType.DMA, pltpu.SemaphoreType.DMA,
                       pltpu.SemaphoreType.DMA],
        name="dispatch",
    )
    def k(h2p_hbm, dest_hbm, zidx_hbm, zsrc_hbm, xs_hbm, idx_v, zidx_v, rows_v, zero_v,
          gsem, wsem, zsem):
        wid = _sc_worker_id()
        base = wid * per_w
        pltpu.sync_copy(dest_hbm.at[pl.ds(wid * n_chunks * TOP_K, n_chunks * TOP_K)], idx_v)
        pltpu.sync_copy(zidx_hbm.at[pl.ds(wid * z_chunks, z_chunks)], zidx_v)
        pltpu.sync_copy(zsrc_hbm, zero_v)

        def zput(j):
            return pltpu.make_async_copy(zero_v, xs_hbm.at[zidx_v.at[j]], zsem)

        for j in range(z_chunks):
            zput(j).start()

        def get(j, slot):
            return pltpu.make_async_copy(h2p_hbm.at[pl.ds(base + j * SC_CHUNK, SC_CHUNK)],
                                         rows_v.at[slot], gsem)

        def put(j, slot, kk):
            return pltpu.make_async_copy(rows_v.at[slot], xs_hbm.at[idx_v.at[j * TOP_K + kk]], wsem)

        get(0, 0).start()

        @pl.loop(0, n_chunks, step=2)
        def _(j):
            for b in range(2):
                jj = j + b
                get(jj, b).wait()

                @pl.when(jj >= 1)
                def _():
                    for kk in range(TOP_K):
                        put(jj - 1, 1 - b, kk).wait()

                @pl.when(jj + 1 < n_chunks)
                def _():
                    get(jj + 1, 1 - b).start()
                for kk in range(TOP_K):
                    put(jj, b, kk).start()

        for kk in range(TOP_K):
            put(n_chunks - 1, (n_chunks - 1) % 2, kk).wait()
        for j in range(z_chunks):
            zput(j).wait()

    return k(h2p, dest_w, zero_w, jnp.zeros((SC_CHUNK, width), h2p.dtype))


def _regroup_sum(ys, dest_t, wsel_t, xres):
    n_slots, t_tok = dest_t.shape
    width = ys.shape[1]
    d_out = xres.shape[1]
    grp = SC_SUM_GROUP
    ring = SC_SUM_RING
    per_w = t_tok // SC_WORKERS
    n_sub = per_w // grp
    rows = n_slots * grp
    arrange = lambda a: (a.reshape(n_slots, SC_WORKERS, n_sub, grp).transpose(1, 2, 0, 3)
                         .reshape(SC_WORKERS * n_sub, rows))
    idx2 = arrange(dest_t)
    w2 = arrange(wsel_t).reshape(SC_WORKERS, n_sub * rows)

    @functools.partial(
        pl.kernel, mesh=_sc_mesh(),
        out_type=jax.ShapeDtypeStruct((t_tok, d_out), F32),
        scratch_types=[pltpu.VMEM((n_sub, rows), jnp.int32),
                       pltpu.VMEM((n_sub * rows,), F32),
                       pltpu.VMEM((ring, rows, width), ys.dtype),
                       pltpu.VMEM((ring, grp, d_out), F32),
                       pltpu.SemaphoreType.DMA((ring,)), pltpu.SemaphoreType.DMA((ring,)),
                       pltpu.SemaphoreType.DMA((ring,))],
        compiler_params=pltpu.CompilerParams(needs_layout_passes=False),
        name="regroup_sum",
    )
    def k(ys_hbm, idx_hbm, w_hbm, xres_hbm, out_hbm, idx_v, w_v, rows_v, acc_v, gsem, xsem, psem):
        wid = _sc_worker_id()
        base = wid * per_w
        pltpu.sync_copy(idx_hbm.at[pl.ds(wid * n_sub, n_sub)], idx_v)
        pltpu.sync_copy(w_hbm.at[wid], w_v)

        def get(j, slot):
            return pltpu.make_async_copy(ys_hbm.at[idx_v.at[j]], rows_v.at[slot], gsem.at[slot])

        def xload(j, slot):
            return pltpu.make_async_copy(xres_hbm.at[pl.ds(base + j * grp, grp)], acc_v.at[slot],
                                         xsem.at[slot])

        def put(j, slot):
            return pltpu.make_async_copy(acc_v.at[slot], out_hbm.at[pl.ds(base + j * grp, grp)],
                                         psem.at[slot])

        def accumulate(j, slot):
            @pl.loop(0, grp)
            def _(g):
                wk = [plsc.load_gather(w_v, [jnp.full((SC_LANES,), j * rows + kk * grp + g,
                                                      jnp.int32)]) for kk in range(n_slots)]

                @plsc.parallel_loop(0, width // SC_LANES, unroll=4)
                def _(v):
                    cols_a = pl.ds(v * SC_LANES, SC_LANES)
                    cols_b = pl.ds(width + v * SC_LANES, SC_LANES)
                    acc_a = acc_v[slot, g, cols_a]
                    acc_b = acc_v[slot, g, cols_b]
                    for kk in range(n_slots):
                        ya, yb = _unpack_pair(rows_v[slot, kk * grp + g, cols_a])
                        acc_a = acc_a + ya * wk[kk]
                        acc_b = acc_b + yb * wk[kk]
                    acc_v[slot, g, cols_a] = acc_a
                    acc_v[slot, g, cols_b] = acc_b

        for p in range(ring - 1):
            get(p, p).start()
        xload(0, 0).start()

        @pl.loop(0, n_sub, step=ring)
        def _(j):
            for b in range(ring):
                jj = j + b
                nb = (b + 1) % ring
                get(jj, b).wait()

                @pl.when(jj + ring - 1 < n_sub)
                def _():
                    get(jj + ring - 1, (b + ring - 1) % ring).start()
                xload(jj, b).wait()

                @pl.when(jj + 1 < n_sub)
                def _():
                    @pl.when(jj + 1 >= ring)
                    def _():
                        put(jj + 1 - ring, nb).wait()
                    xload(jj + 1, nb).start()
                accumulate(jj, b)
                put(jj, b).start()

        for p in range(ring):
            put(n_sub - ring + p, p).wait()

    return k(ys, idx2, w2, xres)


def _experts_kernel(n_blk, blk_e_ref, next_e_ref, n_used_ref,
                    xs_hbm, eg_hbm, eu_hbm, ed_hbm, ys_hbm,
                    xbuf, ybuf, hid, stg_g, stg_u, stg_d, wg, wu, wd, xsem, ysem, wsem):
    n = n_used_ref[0]

    def ring(b):
        return jnp.bitwise_and(b, ROW_RING - 1)

    def x_copy(b):
        return pltpu.make_async_copy(xs_hbm.at[pl.ds(pl.multiple_of(b * ROW_BLOCK, ROW_BLOCK),
                                                     ROW_BLOCK)], xbuf.at[ring(b)], xsem.at[ring(b)])

    def y_copy(b):
        return pltpu.make_async_copy(ybuf.at[ring(b)],
                                     ys_hbm.at[pl.ds(pl.multiple_of(b * ROW_BLOCK, ROW_BLOCK),
                                                     ROW_BLOCK)], ysem.at[ring(b)])

    def w_copies(e):
        return (pltpu.make_async_copy(eg_hbm.at[e], stg_g, wsem.at[0]),
                pltpu.make_async_copy(eu_hbm.at[e], stg_u, wsem.at[1]),
                pltpu.make_async_copy(ed_hbm.at[e], stg_d, wsem.at[2]))

    def switch_expert(e, wslot):
        for cp in w_copies(e):
            cp.wait()
        wg[wslot] = stg_g[...].astype(BF16)
        wu[wslot] = stg_u[...].astype(BF16)
        wd[wslot] = stg_d[...].astype(BF16)
        nxt = next_e_ref[e]

        @pl.when(nxt >= 0)
        def _():
            for cp in w_copies(nxt):
                cp.start()

    def gate_up(b, wslot):
        xa, xb = _unpack_pair(xbuf[ring(b)])
        xb16 = jnp.concatenate([xa, xb], axis=1).astype(BF16)
        g = _dot(xb16, wg[wslot])
        up = _dot(xb16, wu[wslot])
        hid[jnp.bitwise_and(b, 1)] = (g * _sigmoid(g) * up).astype(BF16)

    def down(b, wslot):
        y = _dot(hid[jnp.bitwise_and(b, 1)], wd[wslot])
        ybuf[ring(b)] = _pack_pair(y[:, 0:HALF], y[:, HALF:D_MODEL])

    e0 = blk_e_ref[0]
    for cp in w_copies(e0):
        cp.start()
    for j in range(ROW_RING):
        @pl.when(j < n)
        def _():
            x_copy(j).start()
    switch_expert(e0, 0)
    x_copy(0).wait()
    gate_up(0, 0)

    def body(b, wslot_prev):
        e = blk_e_ref[b]
        first = e != blk_e_ref[b - 1]
        wslot = jnp.where(first, 1 - wslot_prev, wslot_prev)

        @pl.when(first)
        def _():
            switch_expert(e, wslot)

        x_copy(b).wait()

        @pl.when(b + ROW_RING - 1 < n)
        def _():
            x_copy(b + ROW_RING - 1).start()

        @pl.when(b >= ROW_RING + 1)
        def _():
            y_copy(b - 1 - ROW_RING).wait()

        down(b - 1, wslot_prev)
        gate_up(b, wslot)
        y_copy(b - 1).start()
        return wslot

    wslot_last = lax.fori_loop(1, n, body, jnp.int32(0))

    last = n - 1

    @pl.when(last >= ROW_RING)
    def _():
        y_copy(last - ROW_RING).wait()
    down(last, wslot_last)
    y_copy(last).start()
    for j in range(ROW_RING - 1, -1, -1):
        @pl.when(last - j >= 0)
        def _():
            y_copy(last - j).wait()

    ybuf[0] = jnp.zeros((ROW_BLOCK, HALF), U32)

    def zero_tail(b, c):
        cp = pltpu.make_async_copy(ybuf.at[0],
                                   ys_hbm.at[pl.ds(pl.multiple_of(b * ROW_BLOCK, ROW_BLOCK),
                                                   ROW_BLOCK)], ysem.at[0])
        cp.start()
        cp.wait()
        return c
    lax.fori_loop(n, n_blk, zero_tail, 0)


def _experts(blk_e, next_e, n_used, xs, e_gate, e_up, e_down):
    n_rows = xs.shape[0]
    n_blk = n_rows // ROW_BLOCK
    any_spec = pl.BlockSpec(memory_space=pl.ANY)
    grid_spec = pltpu.PrefetchScalarGridSpec(
        num_scalar_prefetch=3,
        grid=(1,),
        in_specs=[any_spec, any_spec, any_spec, any_spec],
        out_specs=any_spec,
        scratch_shapes=[pltpu.VMEM((ROW_RING, ROW_BLOCK, HALF), U32),
                        pltpu.VMEM((ROW_RING, ROW_BLOCK, HALF), U32),
                        pltpu.VMEM((2, ROW_BLOCK, EXPERT_HIDDEN), BF16),
                        pltpu.VMEM((D_MODEL, EXPERT_HIDDEN), F32),
                        pltpu.VMEM((D_MODEL, EXPERT_HIDDEN), F32),
                        pltpu.VMEM((EXPERT_HIDDEN, D_MODEL), F32),
                        pltpu.VMEM((2, D_MODEL, EXPERT_HIDDEN), BF16),
                        pltpu.VMEM((2, D_MODEL, EXPERT_HIDDEN), BF16),
                        pltpu.VMEM((2, EXPERT_HIDDEN, D_MODEL), BF16),
                        pltpu.SemaphoreType.DMA((ROW_RING,)),
                        pltpu.SemaphoreType.DMA((ROW_RING,)),
                        pltpu.SemaphoreType.DMA((3,))],
    )
    return pl.pallas_call(
        functools.partial(_experts_kernel, n_blk),
        grid_spec=grid_spec,
        out_shape=jax.ShapeDtypeStruct((n_rows, HALF), U32),
        compiler_params=pltpu.CompilerParams(dimension_semantics=("arbitrary",)),
        name="experts",
    )(blk_e, next_e, n_used, xs, e_gate, e_up, e_down)


def _final_norm_kernel(x_ref, g_ref, out_ref):
    out_ref[...] = _rms(x_ref[...], g_ref[...])


def _final_norm(xsum, g_final):
    t_tok = xsum.shape[0]
    blk = pl.BlockSpec((TM_NORM, D_MODEL), lambda i: (i, 0))
    return pl.pallas_call(
        _final_norm_kernel,
        grid=(t_tok // TM_NORM,),
        in_specs=[blk, pl.BlockSpec((1, D_MODEL), lambda i: (0, 0))],
        out_specs=blk,
        out_shape=jax.ShapeDtypeStruct((t_tok, D_MODEL), F32),
        compiler_params=pltpu.CompilerParams(dimension_semantics=("arbitrary",)),
        name="final_norm",
    )(xsum, g_final)


def kernel(x, g_mix, w_in, b_gate, w_pool_group, pool_scale, w_pool_out, conv_w, w_conv_out, w_o,
           g_ffn, w_router, router_bias, e_gate, e_up, e_down, s_gate, s_up, s_down, g_final):
    b, s, d = x.shape
    t_tok = b * s
    n_pad = N_EXPERTS * ROW_BLOCK
    n_rows = t_tok * TOP_K + n_pad
    n_blk = n_rows // ROW_BLOCK
    assert d == D_MODEL and s % TM_MIX == 0 and TM_MIX >= POOL_HALO
    assert t_tok % TM_DEST == 0 and t_tok % TM_NORM == 0 and n_pad % PLAN_LANES == 0
    assert t_tok % (2 * SC_WORKERS * SC_CHUNK) == 0 and n_pad % (SC_WORKERS * SC_CHUNK) == 0
    assert t_tok % (SC_WORKERS * SC_SUM_GROUP * SC_SUM_RING) == 0

    row = lambda a: a.reshape(1, -1)
    wr_t = w_router.T.astype(F32)
    wr_hi = wr_t.astype(BF16)
    wr = jnp.concatenate([wr_hi, (wr_t - wr_hi.astype(F32)).astype(BF16)], axis=0)

    xres, h2p, idx_t, wsel_t, rank_t, counts = _mixer_router(
        x.reshape(t_tok, d), 0, t_tok, s, row(g_mix), w_in, row(b_gate), w_pool_group,
        row(pool_scale), w_pool_out, conv_w, w_conv_out, w_o, row(g_ffn), wr,
        router_bias.astype(F32).reshape(N_EXPERTS, 1), s_gate, s_up, s_down)
    dest_t, blk_e, next_e, n_used, zero_rows = _plan(counts, idx_t, rank_t, n_blk, n_pad)
    xs = _dispatch(h2p, dest_t, zero_rows, n_rows)
    ys = _experts(blk_e, next_e, n_used, xs, e_gate, e_up, e_down)
    xsum = _regroup_sum(ys, dest_t, wsel_t, xres)
    return _final_norm(xsum, row(g_final)).reshape(b, s, d)
```

```python
import functools

import jax
import jax.numpy as jnp
from jax import lax
from jax.experimental import pallas as pl
from jax.experimental.pallas import tpu as pltpu
from jax.experimental.pallas import tpu_sc as plsc

D_MODEL = 1024
HALF = D_MODEL // 2
POOL_WIDTH = 512
N_POOL_GROUPS = 4
POOL_GROUP = 128
POOL_WINDOWS = (2, 4, 8, 16)
CONV_WIDTH = 512
N_EXPERTS = 64
TOP_K = 8
EXPERT_HIDDEN = 256
SHARED_HIDDEN = 256
ROUTED_SCALE = 2.5
EPS = 1e-6

POOL_HALO = 16
CONV_HALO = 8
TM_MIX = 512
W_IN_CHUNK = 128
TM_DEST = 2048
PLAN_LANES = 2048
ROW_BLOCK = 512
ROW_RING = 8
TM_NORM = 512
VMEM_LIMIT = 56 * 1024 * 1024

SC_CORES = 2
SC_SUBCORES = 16
SC_WORKERS = SC_CORES * SC_SUBCORES
SC_LANES = 16
SC_CHUNK = 64
SC_SUM_GROUP = 8
SC_SUM_ROW_RING = 2
SC_SUM_ACC_RING = 4

BF16 = jnp.bfloat16
F32 = jnp.float32
U32 = jnp.uint32


def _rms(x, g):
    r = lax.rsqrt(jnp.mean(x * x, axis=-1, keepdims=True) + EPS)
    return (x * r) * g


def _dot(a, b):
    return jnp.dot(a, b, preferred_element_type=F32)


def _sigmoid(z):
    return 0.5 * jnp.tanh(0.5 * z) + 0.5


def _pack_pair(a, b):
    ra = lax.bitcast_convert_type(a.astype(BF16).astype(F32), U32)
    rb = lax.bitcast_convert_type(b.astype(BF16).astype(F32), U32)
    return ra | (rb >> 16)


def _unpack_pair(w):
    a = lax.bitcast_convert_type(w & jnp.uint32(0xFFFF0000), F32)
    b = lax.bitcast_convert_type(w << 16, F32)
    return a, b


def _load_weights_bf16(w_in_hbm, w_grp_hbm, w_po_hbm, w_co_hbm, w_o_hbm, s_gate_hbm, s_up_hbm,
                       s_down_hbm, w_in_ref, w_grp_ref, w_po_ref, w_co_ref, w_o_ref, s_gate_ref,
                       s_up_ref, s_down_ref, stg_in, stg_sq, stg_sh, stg_grp, wsem):
    copy = lambda src, dst, k: pltpu.make_async_copy(src, dst, wsem.at[k])
    rows = stg_in.shape[1]
    n_in = w_in_hbm.shape[0] // rows
    half = w_po_hbm.shape[0]
    c_in = [copy(w_in_hbm.at[pl.ds(j * rows, rows)], stg_in.at[j % 2], j % 2) for j in range(n_in)]
    c_o = copy(w_o_hbm, stg_sq, 2)
    c_sg = copy(s_gate_hbm, stg_sh.at[0], 3)
    c_su = copy(s_up_hbm, stg_sh.at[1], 4)
    c_grp = copy(w_grp_hbm, stg_grp, 5)
    c_po = copy(w_po_hbm, stg_sq.at[0:half], 2)
    c_co = copy(w_co_hbm, stg_sq.at[half:2 * half], 6)
    c_sd = copy(s_down_hbm, stg_sq.at[0:s_down_hbm.shape[0]], 2)
    for cp in (c_in[0], c_in[1], c_o, c_sg, c_su, c_grp):
        cp.start()
    for j in range(n_in):
        c_in[j].wait()
        w_in_ref[j * rows:(j + 1) * rows, :] = stg_in[j % 2].astype(BF16)
        if j + 2 < n_in:
            c_in[j + 2].start()
    c_o.wait()
    w_o_ref[...] = stg_sq[...].astype(BF16)
    c_po.start()
    c_co.start()
    c_sg.wait()
    s_gate_ref[...] = stg_sh[0].astype(BF16)
    c_su.wait()
    s_up_ref[...] = stg_sh[1].astype(BF16)
    c_grp.wait()
    w_grp_ref[...] = stg_grp[...].astype(BF16)
    c_po.wait()
    w_po_ref[...] = stg_sq[0:half, :].astype(BF16)
    c_co.wait()
    w_co_ref[...] = stg_sq[half:2 * half, :].astype(BF16)
    c_sd.start()
    c_sd.wait()
    s_down_ref[...] = stg_sq[0:s_down_hbm.shape[0], :].astype(BF16)


def _mixer_router_kernel(n_seq_tiles,
                         x_ref, g_mix_ref, w_in_hbm, b_gate_ref, w_grp_hbm, pool_scale_ref,
                         w_po_hbm, conv_w_ref, w_co_hbm, w_o_hbm, g_ffn_ref,
                         wr_ref, rbias_ref, s_gate_hbm, s_up_hbm, s_down_hbm,
                         xres_ref, h2p_ref, idx_ref, wsel_ref, rank_ref, counts_ref,
                         ext_pool, ext_conv, cnt_carry, tri,
                         w_in_ref, w_grp_ref, w_po_ref, w_co_ref, w_o_ref, s_gate_ref, s_up_ref,
                         s_down_ref, stg_in, stg_sq, stg_sh, stg_grp, wsem):
    tm = x_ref.shape[0]
    i = pl.program_id(0)
    st = i % n_seq_tiles

    @pl.when(i == 0)
    def _():
        _load_weights_bf16(w_in_hbm, w_grp_hbm, w_po_hbm, w_co_hbm, w_o_hbm, s_gate_hbm, s_up_hbm,
                           s_down_hbm, w_in_ref, w_grp_ref, w_po_ref, w_co_ref, w_o_ref,
                           s_gate_ref, s_up_ref, s_down_ref, stg_in, stg_sq, stg_sh, stg_grp, wsem)
        r = lax.broadcasted_iota(jnp.int32, (tm, tm), 0)
        c = lax.broadcasted_iota(jnp.int32, (tm, tm), 1)
        tri[...] = (r < c).astype(BF16)
        cnt_carry[...] = jnp.zeros_like(cnt_carry)

    @pl.when(st == 0)
    def _():
        ext_pool[0:POOL_HALO, :] = jnp.zeros((POOL_HALO, POOL_WIDTH), F32)
        ext_conv[0:CONV_HALO, :] = jnp.zeros((CONV_HALO, CONV_WIDTH), F32)

    x = x_ref[...]
    hb = _rms(x, g_mix_ref[...]).astype(BF16)

    o0 = POOL_WIDTH
    o1 = o0 + CONV_WIDTH
    o2 = o1 + CONV_WIDTH
    o3 = o2 + CONV_WIDTH

    u = _dot(hb, w_in_ref[:, 0:o0])
    ext_pool[POOL_HALO:POOL_HALO + tm, :] = u
    gc = _dot(hb, w_in_ref[:, o1:o2])
    v = _dot(hb, w_in_ref[:, o2:o3])
    pre_a = _dot(hb, w_in_ref[:, o3:o3 + D_MODEL])

    t_glob = st * tm + lax.broadcasted_iota(jnp.int32, (tm, 1), 0)
    mixed = []
    for gi, w in enumerate(POOL_WINDOWS):
        cols = slice(gi * POOL_GROUP, (gi + 1) * POOL_GROUP)
        ug = u[:, cols]
        acc = ug
        for j in range(1, w):
            acc = acc + ext_pool[POOL_HALO - j:POOL_HALO - j + tm, cols]
        cnt = jnp.minimum(t_glob + 1, w).astype(F32)
        pooled = acc * (1.0 / cnt) - ug
        mixed.append(_dot(pooled.astype(BF16), w_grp_ref[gi]))
    ext_pool[0:POOL_HALO, :] = ext_pool[tm:tm + POOL_HALO, :]
    pre_b = _dot(hb, w_in_ref[:, o3 + D_MODEL:o3 + 2 * D_MODEL])
    gb = _dot(hb, w_in_ref[:, o0:o1])

    cv = gc * v
    ext_conv[CONV_HALO:CONV_HALO + tm, :] = cv
    conv = (ext_conv[CONV_HALO - 2:CONV_HALO - 2 + tm, :] * conv_w_ref[0:1, :]
            + ext_conv[CONV_HALO - 1:CONV_HALO - 1 + tm, :] * conv_w_ref[1:2, :]
            + cv * conv_w_ref[2:3, :])
    ext_conv[0:CONV_HALO, :] = ext_conv[tm:tm + CONV_HALO, :]
    branch_b = _dot((gb * conv).astype(BF16), w_co_ref[...])
    mixed = jnp.concatenate(mixed, axis=1) * pool_scale_ref[...]
    branch_a = _dot(mixed.astype(BF16), w_po_ref[...])

    merged = (_sigmoid(pre_a + b_gate_ref[:, 0:D_MODEL]) * branch_a
              + _sigmoid(pre_b + b_gate_ref[:, D_MODEL:2 * D_MODEL]) * branch_b)
    x1 = x + _dot(merged.astype(BF16), w_o_ref[...])

    h2 = _rms(x1, g_ffn_ref[...])
    h2p_ref[...] = _pack_pair(h2[:, 0:HALF], h2[:, HALF:D_MODEL])
    h2b = h2.astype(BF16)

    nt = (((1,), (1,)), ((), ()))
    parts = lax.dot_general(wr_ref[...], h2b, nt, preferred_element_type=F32)
    logits = parts[0:N_EXPERTS, :] + parts[N_EXPERTS:2 * N_EXPERTS, :]
    sg = _dot(h2b, s_gate_ref[...])
    su = _dot(h2b, s_up_ref[...])
    scores = jax.nn.sigmoid(logits)
    sel = scores + rbias_ref[...]
    eidx = lax.broadcasted_iota(jnp.int32, (N_EXPERTS, tm), 0).astype(F32)
    e_rows, w_rows = [], []
    mask = jnp.zeros((N_EXPERTS, tm), F32)
    for _ in range(TOP_K):
        m = jnp.max(sel, axis=0, keepdims=True)
        ek = jnp.min(jnp.where(sel == m, eidx, float(N_EXPERTS)), axis=0, keepdims=True)
        oh = eidx == ek
        w_rows.append(jnp.sum(jnp.where(oh, scores, 0.0), axis=0, keepdims=True))
        e_rows.append(ek)
        mask = mask + oh.astype(F32)
        sel = jnp.where(oh, -jnp.inf, sel)

    shared = _dot((sg * _sigmoid(sg) * su).astype(BF16), s_down_ref[...])
    xres_ref[...] = x1 + shared

    wsum = w_rows[0]
    for k in range(1, TOP_K):
        wsum = wsum + w_rows[k]

    before = _dot(mask.astype(BF16), tri[...]) + cnt_carry[...]
    for k in range(TOP_K):
        oh = eidx == e_rows[k]
        rank_ref[k:k + 1, :] = jnp.sum(jnp.where(oh, before, 0.0), axis=0,
                                       keepdims=True).astype(jnp.int32)
        idx_ref[k:k + 1, :] = e_rows[k].astype(jnp.int32)
        wsel_ref[k:k + 1, :] = w_rows[k] / wsum * ROUTED_SCALE
    total = cnt_carry[...] + jnp.sum(mask, axis=1, keepdims=True)
    cnt_carry[...] = total
    counts_ref[...] = total.astype(jnp.int32)


def _mixer_router(x2d, tok0, t_tok, seq_len, g_mix, w_in, b_gate, w_grp, pool_scale, w_po, conv_w,
                  w_co, w_o, g_ffn, wr, rbias, s_gate, s_up, s_down):
    tm = TM_MIX
    n_seq_tiles = seq_len // tm
    off = tok0 // tm
    const = lambda shape: pl.BlockSpec(shape, lambda i: (0,) * len(shape),
                                       pipeline_mode=pl.Buffered(1))
    hbm = pl.BlockSpec(memory_space=pl.ANY)
    x_blk = pl.BlockSpec((tm, D_MODEL), lambda i: (i + off, 0))
    row_blk = pl.BlockSpec((tm, D_MODEL), lambda i: (i, 0))
    half_blk = pl.BlockSpec((tm, HALF), lambda i: (i, 0))
    slot_blk = pl.BlockSpec((TOP_K, tm), lambda i: (0, i))
    return pl.pallas_call(
        functools.partial(_mixer_router_kernel, n_seq_tiles),
        grid=(t_tok // tm,),
        in_specs=[x_blk, const(g_mix.shape), hbm, const(b_gate.shape),
                  hbm, const(pool_scale.shape), hbm,
                  const(conv_w.shape), hbm, hbm, const(g_ffn.shape),
                  const(wr.shape), const(rbias.shape), hbm, hbm, hbm],
        out_specs=[row_blk, half_blk, slot_blk, slot_blk, slot_blk,
                   pl.BlockSpec((N_EXPERTS, 1), lambda i: (0, 0))],
        out_shape=[jax.ShapeDtypeStruct((t_tok, D_MODEL), F32),
                   jax.ShapeDtypeStruct((t_tok, HALF), U32),
                   jax.ShapeDtypeStruct((TOP_K, t_tok), jnp.int32),
                   jax.ShapeDtypeStruct((TOP_K, t_tok), F32),
                   jax.ShapeDtypeStruct((TOP_K, t_tok), jnp.int32),
                   jax.ShapeDtypeStruct((N_EXPERTS, 1), jnp.int32)],
        scratch_shapes=[pltpu.VMEM((POOL_HALO + tm, POOL_WIDTH), F32),
                        pltpu.VMEM((CONV_HALO + tm, CONV_WIDTH), F32),
                        pltpu.VMEM((N_EXPERTS, 1), F32),
                        pltpu.VMEM((tm, tm), BF16)]
                       + [pltpu.VMEM(w.shape, BF16)
                          for w in (w_in, w_grp, w_po, w_co, w_o, s_gate, s_up, s_down)]
                       + [pltpu.VMEM((2, W_IN_CHUNK, w_in.shape[1]), F32),
                          pltpu.VMEM(w_o.shape, F32),
                          pltpu.VMEM((2,) + s_gate.shape, F32),
                          pltpu.VMEM(w_grp.shape, F32),
                          pltpu.SemaphoreType.DMA((7,))],
        compiler_params=pltpu.CompilerParams(dimension_semantics=("arbitrary",),
                                             vmem_limit_bytes=VMEM_LIMIT),
        name="mixer_router",
    )(x2d, g_mix, w_in, b_gate, w_grp, pool_scale, w_po, conv_w, w_co, w_o, g_ffn,
      wr, rbias, s_gate, s_up, s_down)


def _plan_kernel(n_blk, n_pad, counts_ref, idx_ref, rank_ref,
                 dest_ref, blk_e_ref, next_e_ref, n_used_ref, zero_rows_ref, pad_start):
    e_n = N_EXPERTS

    @pl.when(pl.program_id(0) == 0)
    def _():
        sub = lax.broadcasted_iota(jnp.int32, (e_n, e_n), 0)
        lane = lax.broadcasted_iota(jnp.int32, (e_n, e_n), 1)
        c_col = counts_ref[...]
        p_col = ((c_col + (ROW_BLOCK - 1)) // ROW_BLOCK) * ROW_BLOCK
        c_f = c_col.astype(F32)
        p_f = p_col.astype(F32)
        gap_f = p_f - c_f
        to_row = lambda col: jnp.sum(jnp.where(sub == lane, col, 0.0), axis=0, keepdims=True)
        p_row = to_row(p_f)
        gap_row = to_row(gap_f)
        pad_end_col = jnp.sum(jnp.where(lane <= sub, p_row, 0.0), axis=1, keepdims=True)
        gap_before_col = jnp.sum(jnp.where(lane < sub, gap_row, 0.0), axis=1, keepdims=True)
        pad_start[...] = pad_end_col - p_f
        pad_end_last = jnp.sum(p_row, axis=1, keepdims=True)
        n_used_ref[...] = jnp.broadcast_to(pad_end_last * (1.0 / ROW_BLOCK),
                                           n_used_ref.shape).astype(jnp.int32)
        row0 = (lax.broadcasted_iota(jnp.int32, (e_n, blk_e_ref.shape[1]), 1)
                * ROW_BLOCK).astype(F32)
        owner = jnp.sum(jnp.where(pad_end_col <= row0, 1.0, 0.0), axis=0, keepdims=True)
        blk_e_ref[...] = jnp.minimum(owner, float(e_n - 1)).astype(jnp.int32)
        used_later = jnp.logical_and(sub > lane, p_f > 0.0)
        nxt = jnp.min(jnp.where(used_later, sub, e_n), axis=0, keepdims=True)
        next_e_ref[...] = jnp.full(next_e_ref.shape, -1, jnp.int32)
        next_e_ref[:, 0:e_n] = jnp.where(nxt < e_n, nxt, -1)
        for j0 in range(0, n_pad, PLAN_LANES):
            j = (j0 + lax.broadcasted_iota(jnp.int32, (e_n, PLAN_LANES), 1)).astype(F32)
            before = jnp.sum(jnp.where(gap_before_col <= j, c_f, 0.0), axis=0, keepdims=True)
            zero_rows_ref[:, j0:j0 + PLAN_LANES] = (j[0:1, :] + before).astype(jnp.int32)

    tm = idx_ref.shape[1]
    eidx = lax.broadcasted_iota(jnp.int32, (e_n, tm), 0)
    ps = pad_start[...]
    for k in range(TOP_K):
        oh = eidx == idx_ref[k:k + 1, :]
        start = jnp.sum(jnp.where(oh, ps, 0.0), axis=0, keepdims=True)
        dest_ref[k:k + 1, :] = start.astype(jnp.int32) + rank_ref[k:k + 1, :]


def _plan(counts, idx_t, rank_t, n_blk, n_pad):
    t_tok = idx_t.shape[1]
    lanes = 128
    n_blk_p = -(-n_blk // lanes) * lanes
    slot_blk = pl.BlockSpec((TOP_K, TM_DEST), lambda i: (0, i))
    whole = lambda n: pl.BlockSpec((1, n), lambda i: (0, 0))
    dest_t, blk_e, next_e, n_used, zero_rows = pl.pallas_call(
        functools.partial(_plan_kernel, n_blk, n_pad),
        grid=(t_tok // TM_DEST,),
        in_specs=[pl.BlockSpec((N_EXPERTS, 1), lambda i: (0, 0)), slot_blk, slot_blk],
        out_specs=[slot_blk, whole(n_blk_p), whole(lanes), whole(lanes), whole(n_pad)],
        out_shape=[jax.ShapeDtypeStruct((TOP_K, t_tok), jnp.int32),
                   jax.ShapeDtypeStruct((1, n_blk_p), jnp.int32),
                   jax.ShapeDtypeStruct((1, lanes), jnp.int32),
                   jax.ShapeDtypeStruct((1, lanes), jnp.int32),
                   jax.ShapeDtypeStruct((1, n_pad), jnp.int32)],
        scratch_shapes=[pltpu.VMEM((N_EXPERTS, 1), F32)],
        compiler_params=pltpu.CompilerParams(dimension_semantics=("arbitrary",)),
        name="plan",
    )(counts, idx_t, rank_t)
    return (dest_t, blk_e.reshape(-1), next_e.reshape(-1), n_used.reshape(-1),
            zero_rows.reshape(-1))


def _sc_mesh():
    return plsc.VectorSubcoreMesh(core_axis_name="c", subcore_axis_name="s")


def _sc_worker_id():
    return lax.axis_index("s") * SC_CORES + lax.axis_index("c")


def _dispatch(h2p, dest_t, zero_rows, n_rows):
    t_tok, width = h2p.shape
    per_w = t_tok // SC_WORKERS
    n_chunks = per_w // SC_CHUNK
    z_chunks = zero_rows.shape[0] // (SC_WORKERS * SC_CHUNK)
    dest_w = (dest_t.reshape(TOP_K, SC_WORKERS, n_chunks, SC_CHUNK)
              .transpose(1, 2, 0, 3).reshape(SC_WORKERS * n_chunks * TOP_K, SC_CHUNK))
    zero_w = zero_rows.reshape(SC_WORKERS * z_chunks, SC_CHUNK)

    @functools.partial(
        pl.kernel, mesh=_sc_mesh(),
        out_type=jax.ShapeDtypeStruct((n_rows, width), h2p.dtype),
        scratch_types=[pltpu.VMEM((n_chunks * TOP_K, SC_CHUNK), jnp.int32),
                       pltpu.VMEM((z_chunks, SC_CHUNK), jnp.int32),
                       pltpu.VMEM((2, SC_CHUNK, width), h2p.dtype),
                       pltpu.VMEM((SC_CHUNK, width), h2p.dtype),
                       pltpu.SemaphoreType.DMA, pltpu.SemaphoreType.DMA,
                       pltpu.SemaphoreType.DMA],
        name="dispatch",
    )
    def k(h2p_hbm, dest_hbm, zidx_hbm, zsrc_hbm, xs_hbm, idx_v, zidx_v, rows_v, zero_v,
          gsem, wsem, zsem):
        wid = _sc_worker_id()
        base = wid * per_w
        pltpu.sync_copy(dest_hbm.at[pl.ds(wid * n_chunks * TOP_K, n_chunks * TOP_K)], idx_v)
        pltpu.sync_copy(zidx_hbm.at[pl.ds(wid * z_chunks, z_chunks)], zidx_v)
        pltpu.sync_copy(zsrc_hbm, zero_v)

        def zput(j):
            return pltpu.make_async_copy(zero_v, xs_hbm.at[zidx_v.at[j]], zsem)

        for j in range(z_chunks):
            zput(j).start()

        def get(j, slot):
            return pltpu.make_async_copy(h2p_hbm.at[pl.ds(base + j * SC_CHUNK, SC_CHUNK)],
                                         rows_v.at[slot], gsem)

        def put(j, slot, kk):
            return pltpu.make_async_copy(rows_v.at[slot], xs_hbm.at[idx_v.at[j * TOP_K + kk]], wsem)

        get(0, 0).start()

        @pl.loop(0, n_chunks, step=2)
        def _(j):
            for b in range(2):
                jj = j + b
                get(jj, b).wait()

                @pl.when(jj >= 1)
                def _():
                    for kk in range(TOP_K):
                        put(jj - 1, 1 - b, kk).wait()

                @pl.when(jj + 1 < n_chunks)
                def _():
                    get(jj + 1, 1 - b).start()
                for kk in range(TOP_K):
                    put(jj, b, kk).start()

        for kk in range(TOP_K):
            put(n_chunks - 1, (n_chunks - 1) % 2, kk).wait()
        for j in range(z_chunks):
            zput(j).wait()

    return k(h2p, dest_w, zero_w, jnp.zeros((SC_CHUNK, width), h2p.dtype))


def _regroup_sum(ys, dest_t, wsel_t, xres):
    n_slots, t_tok = dest_t.shape
    width = ys.shape[1]
    d_out = xres.shape[1]
    grp = SC_SUM_GROUP
    r_rows = SC_SUM_ROW_RING
    r_acc = SC_SUM_ACC_RING
    per_w = t_tok // SC_WORKERS
    n_sub = per_w // grp

    @functools.partial(
        pl.kernel, mesh=_sc_mesh(),
        out_type=jax.ShapeDtypeStruct((t_tok, d_out), F32),
        scratch_types=[pltpu.VMEM((n_slots, per_w), jnp.int32),
                       pltpu.VMEM((n_slots, per_w), F32),
                       pltpu.VMEM((r_rows, n_slots * grp, width), ys.dtype),
                       pltpu.VMEM((r_acc, grp, d_out), F32),
                       pltpu.SemaphoreType.DMA((r_rows,)), pltpu.SemaphoreType.DMA((r_acc,)),
                       pltpu.SemaphoreType.DMA((r_acc,))],
        compiler_params=pltpu.CompilerParams(needs_layout_passes=False),
        name="regroup_sum",
    )
    def k(ys_hbm, idx_hbm, w_hbm, xres_hbm, out_hbm, idx_v, w_v, rows_v, acc_v, gsem, xsem, psem):
        wid = _sc_worker_id()
        base = wid * per_w
        pltpu.sync_copy(idx_hbm.at[:, pl.ds(base, per_w)], idx_v)
        pltpu.sync_copy(w_hbm.at[:, pl.ds(base, per_w)], w_v)

        def gets(j, slot):
            return [pltpu.make_async_copy(ys_hbm.at[idx_v.at[kk, pl.ds(j * grp, grp)]],
                                          rows_v.at[slot, pl.ds(kk * grp, grp)], gsem.at[slot])
                    for kk in range(n_slots)]

        def xload(j, slot):
            return pltpu.make_async_copy(xres_hbm.at[pl.ds(base + j * grp, grp)], acc_v.at[slot],
                                         xsem.at[slot])

        def put(j, slot):
            return pltpu.make_async_copy(acc_v.at[slot], out_hbm.at[pl.ds(base + j * grp, grp)],
                                         psem.at[slot])

        def accumulate(j, rslot, aslot):
            @pl.loop(0, grp)
            def _(g):
                pos = jnp.full((SC_LANES,), j * grp + g, jnp.int32)
                wk = [plsc.load_gather(w_v, [jnp.full((SC_LANES,), kk, jnp.int32), pos])
                      for kk in range(n_slots)]

                @plsc.parallel_loop(0, width // SC_LANES, unroll=4)
                def _(v):
                    cols_a = pl.ds(v * SC_LANES, SC_LANES)
                    cols_b = pl.ds(width + v * SC_LANES, SC_LANES)
                    acc_a = acc_v[aslot, g, cols_a]
                    acc_b = acc_v[aslot, g, cols_b]
                    for kk in range(n_slots):
                        ya, yb = _unpack_pair(rows_v[rslot, kk * grp + g, cols_a])
                        acc_a = acc_a + ya * wk[kk]
                        acc_b = acc_b + yb * wk[kk]
                    acc_v[aslot, g, cols_a] = acc_a
                    acc_v[aslot, g, cols_b] = acc_b

        for cp in gets(0, 0):
            cp.start()
        xload(0, 0).start()

        @pl.loop(0, n_sub, step=r_acc)
        def _(j):
            for b in range(r_acc):
                jj = j + b
                rs = b % r_rows
                na = (b + 1) % r_acc
                for cp in gets(jj, rs):
                    cp.wait()

                @pl.when(jj + 1 < n_sub)
                def _():
                    for cp in gets(jj + 1, (b + 1) % r_rows):
                        cp.start()
                xload(jj, b).wait()

                @pl.when(jj + 1 < n_sub)
                def _():
                    @pl.when(jj + 1 >= r_acc)
                    def _():
                        put(jj + 1 - r_acc, na).wait()
                    xload(jj + 1, na).start()
                accumulate(jj, rs, b)
                put(jj, b).start()

        for p in range(r_acc):
            put(n_sub - r_acc + p, p).wait()

    return k(ys, dest_t, wsel_t, xres)


def _experts_kernel(n_blk, blk_e_ref, next_e_ref, n_used_ref,
                    xs_hbm, eg_hbm, eu_hbm, ed_hbm, ys_hbm,
                    xbuf, ybuf, hid, stg_g, stg_u, stg_d, wg, wu, wd, xsem, ysem, wsem):
    n = n_used_ref[0]

    def ring(b):
        return jnp.bitwise_and(b, ROW_RING - 1)

    def x_copy(b):
        return pltpu.make_async_copy(xs_hbm.at[pl.ds(pl.multiple_of(b * ROW_BLOCK, ROW_BLOCK),
                                                     ROW_BLOCK)], xbuf.at[ring(b)], xsem.at[ring(b)])

    def y_copy(b):
        return pltpu.make_async_copy(ybuf.at[ring(b)],
                                     ys_hbm.at[pl.ds(pl.multiple_of(b * ROW_BLOCK, ROW_BLOCK),
                                                     ROW_BLOCK)], ysem.at[ring(b)])

    def w_copies(e):
        return (pltpu.make_async_copy(eg_hbm.at[e], stg_g, wsem.at[0]),
                pltpu.make_async_copy(eu_hbm.at[e], stg_u, wsem.at[1]),
                pltpu.make_async_copy(ed_hbm.at[e], stg_d, wsem.at[2]))

    def switch_expert(e, wslot):
        for cp in w_copies(e):
            cp.wait()
        wg[wslot] = stg_g[...].astype(BF16)
        wu[wslot] = stg_u[...].astype(BF16)
        wd[wslot] = stg_d[...].astype(BF16)
        nxt = next_e_ref[e]

        @pl.when(nxt >= 0)
        def _():
            for cp in w_copies(nxt):
                cp.start()

    def gate_up(b, wslot):
        xa, xb = _unpack_pair(xbuf[ring(b)])
        xb16 = jnp.concatenate([xa, xb], axis=1).astype(BF16)
        g = _dot(xb16, wg[wslot])
        up = _dot(xb16, wu[wslot])
        hid[jnp.bitwise_and(b, 1)] = (g * _sigmoid(g) * up).astype(BF16)

    def down(b, wslot):
        y = _dot(hid[jnp.bitwise_and(b, 1)], wd[wslot])
        ybuf[ring(b)] = _pack_pair(y[:, 0:HALF], y[:, HALF:D_MODEL])

    e0 = blk_e_ref[0]
    for cp in w_copies(e0):
        cp.start()
    for j in range(ROW_RING):
        @pl.when(j < n)
        def _():
            x_copy(j).start()
    switch_expert(e0, 0)
    x_copy(0).wait()
    gate_up(0, 0)

    def body(b, wslot_prev):
        e = blk_e_ref[b]
        first = e != blk_e_ref[b - 1]
        wslot = jnp.where(first, 1 - wslot_prev, wslot_prev)

        @pl.when(first)
        def _():
            switch_expert(e, wslot)

        x_copy(b).wait()

        @pl.when(b + ROW_RING - 1 < n)
        def _():
            x_copy(b + ROW_RING - 1).start()

        @pl.when(b >= ROW_RING + 1)
        def _():
            y_copy(b - 1 - ROW_RING).wait()

        down(b - 1, wslot_prev)
        gate_up(b, wslot)
        y_copy(b - 1).start()
        return wslot

    wslot_last = lax.fori_loop(1, n, body, jnp.int32(0))

    last = n - 1

    @pl.when(last >= ROW_RING)
    def _():
        y_copy(last - ROW_RING).wait()
    down(last, wslot_last)
    y_copy(last).start()
    for j in range(ROW_RING - 1, -1, -1):
        @pl.when(last - j >= 0)
        def _():
            y_copy(last - j).wait()

    ybuf[0] = jnp.zeros((ROW_BLOCK, HALF), U32)

    def zero_tail(b, c):
        cp = pltpu.make_async_copy(ybuf.at[0],
                                   ys_hbm.at[pl.ds(pl.multiple_of(b * ROW_BLOCK, ROW_BLOCK),
                                                   ROW_BLOCK)], ysem.at[0])
        cp.start()
        cp.wait()
        return c
    lax.fori_loop(n, n_blk, zero_tail, 0)


def _experts(blk_e, next_e, n_used, xs, e_gate, e_up, e_down):
    n_rows = xs.shape[0]
    n_blk = n_rows // ROW_BLOCK
    any_spec = pl.BlockSpec(memory_space=pl.ANY)
    grid_spec = pltpu.PrefetchScalarGridSpec(
        num_scalar_prefetch=3,
        grid=(1,),
        in_specs=[any_spec, any_spec, any_spec, any_spec],
        out_specs=any_spec,
        scratch_shapes=[pltpu.VMEM((ROW_RING, ROW_BLOCK, HALF), U32),
                        pltpu.VMEM((ROW_RING, ROW_BLOCK, HALF), U32),
                        pltpu.VMEM((2, ROW_BLOCK, EXPERT_HIDDEN), BF16),
                        pltpu.VMEM((D_MODEL, EXPERT_HIDDEN), F32),
                        pltpu.VMEM((D_MODEL, EXPERT_HIDDEN), F32),
                        pltpu.VMEM((EXPERT_HIDDEN, D_MODEL), F32),
                        pltpu.VMEM((2, D_MODEL, EXPERT_HIDDEN), BF16),
                        pltpu.VMEM((2, D_MODEL, EXPERT_HIDDEN), BF16),
                        pltpu.VMEM((2, EXPERT_HIDDEN, D_MODEL), BF16),
                        pltpu.SemaphoreType.DMA((ROW_RING,)),
                        pltpu.SemaphoreType.DMA((ROW_RING,)),
                        pltpu.SemaphoreType.DMA((3,))],
    )
    return pl.pallas_call(
        functools.partial(_experts_kernel, n_blk),
        grid_spec=grid_spec,
        out_shape=jax.ShapeDtypeStruct((n_rows, HALF), U32),
        compiler_params=pltpu.CompilerParams(dimension_semantics=("arbitrary",)),
        name="experts",
    )(blk_e, next_e, n_used, xs, e_gate, e_up, e_down)


def _final_norm_kernel(x_ref, g_ref, out_ref):
    out_ref[...] = _rms(x_ref[...], g_ref[...])


def _final_norm(xsum, g_final):
    t_tok = xsum.shape[0]
    blk = pl.BlockSpec((TM_NORM, D_MODEL), lambda i: (i, 0))
    return pl.pallas_call(
        _final_norm_kernel,
        grid=(t_tok // TM_NORM,),
        in_specs=[blk, pl.BlockSpec((1, D_MODEL), lambda i: (0, 0))],
        out_specs=blk,
        out_shape=jax.ShapeDtypeStruct((t_tok, D_MODEL), F32),
        compiler_params=pltpu.CompilerParams(dimension_semantics=("arbitrary",)),
        name="final_norm",
    )(xsum, g_final)


def kernel(x, g_mix, w_in, b_gate, w_pool_group, pool_scale, w_pool_out, conv_w, w_conv_out, w_o,
           g_ffn, w_router, router_bias, e_gate, e_up, e_down, s_gate, s_up, s_down, g_final):
    b, s, d = x.shape
    t_tok = b * s
    n_pad = N_EXPERTS * ROW_BLOCK
    n_rows = t_tok * TOP_K + n_pad
    n_blk = n_rows // ROW_BLOCK
    assert d == D_MODEL and s % TM_MIX == 0 and TM_MIX >= POOL_HALO
    assert t_tok % TM_DEST == 0 and t_tok % TM_NORM == 0 and n_pad % PLAN_LANES == 0
    assert t_tok % (2 * SC_WORKERS * SC_CHUNK) == 0 and n_pad % (SC_WORKERS * SC_CHUNK) == 0
    assert t_tok % (SC_WORKERS * SC_SUM_GROUP * SC_SUM_ACC_RING) == 0

    row = lambda a: a.reshape(1, -1)
    wr_t = w_router.T.astype(F32)
    wr_hi = wr_t.astype(BF16)
    wr = jnp.concatenate([wr_hi, (wr_t - wr_hi.astype(F32)).astype(BF16)], axis=0)

    xres, h2p, idx_t, wsel_t, rank_t, counts = _mixer_router(
        x.reshape(t_tok, d), 0, t_tok, s, row(g_mix), w_in, row(b_gate), w_pool_group,
        row(pool_scale), w_pool_out, conv_w, w_conv_out, w_o, row(g_ffn), wr,
        router_bias.astype(F32).reshape(N_EXPERTS, 1), s_gate, s_up, s_down)
    dest_t, blk_e, next_e, n_used, zero_rows = _plan(counts, idx_t, rank_t, n_blk, n_pad)
    xs = _dispatch(h2p, dest_t, zero_rows, n_rows)
    ys = _experts(blk_e, next_e, n_used, xs, e_gate, e_up, e_down)
    xsum = _regroup_sum(ys, dest_t, wsel_t, xres)
    return _final_norm(xsum, row(g_final)).reshape(b, s, d)
```

```python
import functools

import jax
import jax.numpy as jnp
from jax import lax
from jax.experimental import pallas as pl
from jax.experimental.pallas import tpu as pltpu
from jax.experimental.pallas import tpu_sc as plsc

D_MODEL = 1024
HALF = D_MODEL // 2
POOL_WIDTH = 512
N_POOL_GROUPS = 4
POOL_GROUP = 128
POOL_WINDOWS = (2, 4, 8, 16)
CONV_WIDTH = 512
N_EXPERTS = 64
TOP_K = 8
EXPERT_HIDDEN = 256
SHARED_HIDDEN = 256
ROUTED_SCALE = 2.5
EPS = 1e-6

POOL_HALO = 16
CONV_HALO = 8
TM_MIX = 512
W_IN_CHUNK = 128
TM_DEST = 2048
PLAN_LANES = 2048
ROW_BLOCK = 512
ROW_RING = 16
TM_NORM = 512
VMEM_LIMIT = 56 * 1024 * 1024

SC_CORES = 2
SC_SUBCORES = 16
SC_WORKERS = SC_CORES * SC_SUBCORES
SC_LANES = 16
SC_CHUNK = 64
SC_SUM_GROUP = 8
SC_SUM_ROW_RING = 2
SC_SUM_ACC_RING = 4

BF16 = jnp.bfloat16
F32 = jnp.float32
U32 = jnp.uint32


def _rms(x, g):
    r = lax.rsqrt(jnp.mean(x * x, axis=-1, keepdims=True) + EPS)
    return (x * r) * g


def _dot(a, b):
    return jnp.dot(a, b, preferred_element_type=F32)


def _sigmoid(z):
    return 0.5 * jnp.tanh(0.5 * z) + 0.5


def _pack_pair(a, b):
    ra = lax.bitcast_convert_type(a.astype(BF16).astype(F32), U32)
    rb = lax.bitcast_convert_type(b.astype(BF16).astype(F32), U32)
    return ra | (rb >> 16)


def _unpack_pair(w):
    a = lax.bitcast_convert_type(w & jnp.uint32(0xFFFF0000), F32)
    b = lax.bitcast_convert_type(w << 16, F32)
    return a, b


def _load_weights_bf16(w_in_hbm, w_grp_hbm, w_po_hbm, w_co_hbm, w_o_hbm, s_gate_hbm, s_up_hbm,
                       s_down_hbm, w_in_ref, w_grp_ref, w_po_ref, w_co_ref, w_o_ref, s_gate_ref,
                       s_up_ref, s_down_ref, stg_in, stg_sq, stg_sh, stg_grp, wsem):
    copy = lambda src, dst, k: pltpu.make_async_copy(src, dst, wsem.at[k])
    rows = stg_in.shape[1]
    n_in = w_in_hbm.shape[0] // rows
    half = w_po_hbm.shape[0]
    c_in = [copy(w_in_hbm.at[pl.ds(j * rows, rows)], stg_in.at[j % 2], j % 2) for j in range(n_in)]
    c_o = copy(w_o_hbm, stg_sq, 2)
    c_sg = copy(s_gate_hbm, stg_sh.at[0], 3)
    c_su = copy(s_up_hbm, stg_sh.at[1], 4)
    c_grp = copy(w_grp_hbm, stg_grp, 5)
    c_po = copy(w_po_hbm, stg_sq.at[0:half], 2)
    c_co = copy(w_co_hbm, stg_sq.at[half:2 * half], 6)
    c_sd = copy(s_down_hbm, stg_sq.at[0:s_down_hbm.shape[0]], 2)
    for cp in (c_in[0], c_in[1], c_o, c_sg, c_su, c_grp):
        cp.start()
    for j in range(n_in):
        c_in[j].wait()
        w_in_ref[j * rows:(j + 1) * rows, :] = stg_in[j % 2].astype(BF16)
        if j + 2 < n_in:
            c_in[j + 2].start()
    c_o.wait()
    w_o_ref[...] = stg_sq[...].astype(BF16)
    c_po.start()
    c_co.start()
    c_sg.wait()
    s_gate_ref[...] = stg_sh[0].astype(BF16)
    c_su.wait()
    s_up_ref[...] = stg_sh[1].astype(BF16)
    c_grp.wait()
    w_grp_ref[...] = stg_grp[...].astype(BF16)
    c_po.wait()
    w_po_ref[...] = stg_sq[0:half, :].astype(BF16)
    c_co.wait()
    w_co_ref[...] = stg_sq[half:2 * half, :].astype(BF16)
    c_sd.start()
    c_sd.wait()
    s_down_ref[...] = stg_sq[0:s_down_hbm.shape[0], :].astype(BF16)


def _mixer_router_kernel(n_seq_tiles,
                         x_ref, g_mix_ref, w_in_hbm, b_gate_ref, w_grp_hbm, pool_scale_ref,
                         w_po_hbm, conv_w_ref, w_co_hbm, w_o_hbm, g_ffn_ref,
                         wr_ref, rbias_ref, s_gate_hbm, s_up_hbm, s_down_hbm,
                         xres_ref, h2p_ref, idx_ref, wsel_ref, rank_ref, counts_ref,
                         ext_pool, ext_conv, cnt_carry, tri,
                         w_in_ref, w_grp_ref, w_po_ref, w_co_ref, w_o_ref, s_gate_ref, s_up_ref,
                         s_down_ref, stg_in, stg_sq, stg_sh, stg_grp, wsem):
    tm = x_ref.shape[0]
    i = pl.program_id(0)
    st = i % n_seq_tiles

    @pl.when(i == 0)
    def _():
        _load_weights_bf16(w_in_hbm, w_grp_hbm, w_po_hbm, w_co_hbm, w_o_hbm, s_gate_hbm, s_up_hbm,
                           s_down_hbm, w_in_ref, w_grp_ref, w_po_ref, w_co_ref, w_o_ref,
                           s_gate_ref, s_up_ref, s_down_ref, stg_in, stg_sq, stg_sh, stg_grp, wsem)
        r = lax.broadcasted_iota(jnp.int32, (tm, tm), 0)
        c = lax.broadcasted_iota(jnp.int32, (tm, tm), 1)
        tri[...] = (r < c).astype(BF16)
        cnt_carry[...] = jnp.zeros_like(cnt_carry)

    @pl.when(st == 0)
    def _():
        ext_pool[0:POOL_HALO, :] = jnp.zeros((POOL_HALO, POOL_WIDTH), F32)
        ext_conv[0:CONV_HALO, :] = jnp.zeros((CONV_HALO, CONV_WIDTH), F32)

    x = x_ref[...]
    hb = _rms(x, g_mix_ref[...]).astype(BF16)

    o0 = POOL_WIDTH
    o1 = o0 + CONV_WIDTH
    o2 = o1 + CONV_WIDTH
    o3 = o2 + CONV_WIDTH

    u = _dot(hb, w_in_ref[:, 0:o0])
    ext_pool[POOL_HALO:POOL_HALO + tm, :] = u
    gc = _dot(hb, w_in_ref[:, o1:o2])
    v = _dot(hb, w_in_ref[:, o2:o3])
    pre_a = _dot(hb, w_in_ref[:, o3:o3 + D_MODEL])

    t_glob = st * tm + lax.broadcasted_iota(jnp.int32, (tm, 1), 0)
    mixed = []
    for gi, w in enumerate(POOL_WINDOWS):
        cols = slice(gi * POOL_GROUP, (gi + 1) * POOL_GROUP)
        ug = u[:, cols]
        acc = ug
        for j in range(1, w):
            acc = acc + ext_pool[POOL_HALO - j:POOL_HALO - j + tm, cols]
        cnt = jnp.minimum(t_glob + 1, w).astype(F32)
        pooled = acc * (1.0 / cnt) - ug
        mixed.append(_dot(pooled.astype(BF16), w_grp_ref[gi]))
    ext_pool[0:POOL_HALO, :] = ext_pool[tm:tm + POOL_HALO, :]
    pre_b = _dot(hb, w_in_ref[:, o3 + D_MODEL:o3 + 2 * D_MODEL])
    gb = _dot(hb, w_in_ref[:, o0:o1])

    cv = gc * v
    ext_conv[CONV_HALO:CONV_HALO + tm, :] = cv
    conv = (ext_conv[CONV_HALO - 2:CONV_HALO - 2 + tm, :] * conv_w_ref[0:1, :]
            + ext_conv[CONV_HALO - 1:CONV_HALO - 1 + tm, :] * conv_w_ref[1:2, :]
            + cv * conv_w_ref[2:3, :])
    ext_conv[0:CONV_HALO, :] = ext_conv[tm:tm + CONV_HALO, :]
    branch_b = _dot((gb * conv).astype(BF16), w_co_ref[...])
    mixed = jnp.concatenate(mixed, axis=1) * pool_scale_ref[...]
    branch_a = _dot(mixed.astype(BF16), w_po_ref[...])

    merged = (_sigmoid(pre_a + b_gate_ref[:, 0:D_MODEL]) * branch_a
              + _sigmoid(pre_b + b_gate_ref[:, D_MODEL:2 * D_MODEL]) * branch_b)
    x1 = x + _dot(merged.astype(BF16), w_o_ref[...])

    h2 = _rms(x1, g_ffn_ref[...])
    h2p_ref[...] = _pack_pair(h2[:, 0:HALF], h2[:, HALF:D_MODEL])
    h2b = h2.astype(BF16)

    nt = (((1,), (1,)), ((), ()))
    parts = lax.dot_general(wr_ref[...], h2b, nt, preferred_element_type=F32)
    logits = parts[0:N_EXPERTS, :] + parts[N_EXPERTS:2 * N_EXPERTS, :]
    sg = _dot(h2b, s_gate_ref[...])
    su = _dot(h2b, s_up_ref[...])
    scores = jax.nn.sigmoid(logits)
    sel = scores + rbias_ref[...]
    eidx = lax.broadcasted_iota(jnp.int32, (N_EXPERTS, tm), 0).astype(F32)
    e_rows, w_rows = [], []
    mask = jnp.zeros((N_EXPERTS, tm), F32)
    for _ in range(TOP_K):
        m = jnp.max(sel, axis=0, keepdims=True)
        ek = jnp.min(jnp.where(sel == m, eidx, float(N_EXPERTS)), axis=0, keepdims=True)
        oh = eidx == ek
        w_rows.append(jnp.sum(jnp.where(oh, scores, 0.0), axis=0, keepdims=True))
        e_rows.append(ek)
        mask = mask + oh.astype(F32)
        sel = jnp.where(oh, -jnp.inf, sel)

    shared = _dot((sg * _sigmoid(sg) * su).astype(BF16), s_down_ref[...])
    xres_ref[...] = x1 + shared

    wsum = w_rows[0]
    for k in range(1, TOP_K):
        wsum = wsum + w_rows[k]

    before = _dot(mask.astype(BF16), tri[...]) + cnt_carry[...]
    for k in range(TOP_K):
        oh = eidx == e_rows[k]
        rank_ref[k:k + 1, :] = jnp.sum(jnp.where(oh, before, 0.0), axis=0,
                                       keepdims=True).astype(jnp.int32)
        idx_ref[k:k + 1, :] = e_rows[k].astype(jnp.int32)
        wsel_ref[k:k + 1, :] = w_rows[k] / wsum * ROUTED_SCALE
    total = cnt_carry[...] + jnp.sum(mask, axis=1, keepdims=True)
    cnt_carry[...] = total
    counts_ref[...] = total.astype(jnp.int32)


def _mixer_router(x2d, tok0, t_tok, seq_len, g_mix, w_in, b_gate, w_grp, pool_scale, w_po, conv_w,
                  w_co, w_o, g_ffn, wr, rbias, s_gate, s_up, s_down):
    tm = TM_MIX
    n_seq_tiles = seq_len // tm
    off = tok0 // tm
    const = lambda shape: pl.BlockSpec(shape, lambda i: (0,) * len(shape),
                                       pipeline_mode=pl.Buffered(1))
    hbm = pl.BlockSpec(memory_space=pl.ANY)
    x_blk = pl.BlockSpec((tm, D_MODEL), lambda i: (i + off, 0))
    row_blk = pl.BlockSpec((tm, D_MODEL), lambda i: (i, 0))
    half_blk = pl.BlockSpec((tm, HALF), lambda i: (i, 0))
    slot_blk = pl.BlockSpec((TOP_K, tm), lambda i: (0, i))
    return pl.pallas_call(
        functools.partial(_mixer_router_kernel, n_seq_tiles),
        grid=(t_tok // tm,),
        in_specs=[x_blk, const(g_mix.shape), hbm, const(b_gate.shape),
                  hbm, const(pool_scale.shape), hbm,
                  const(conv_w.shape), hbm, hbm, const(g_ffn.shape),
                  const(wr.shape), const(rbias.shape), hbm, hbm, hbm],
        out_specs=[row_blk, half_blk, slot_blk, slot_blk, slot_blk,
                   pl.BlockSpec((N_EXPERTS, 1), lambda i: (0, 0))],
        out_shape=[jax.ShapeDtypeStruct((t_tok, D_MODEL), F32),
                   jax.ShapeDtypeStruct((t_tok, HALF), U32),
                   jax.ShapeDtypeStruct((TOP_K, t_tok), jnp.int32),
                   jax.ShapeDtypeStruct((TOP_K, t_tok), F32),
                   jax.ShapeDtypeStruct((TOP_K, t_tok), jnp.int32),
                   jax.ShapeDtypeStruct((N_EXPERTS, 1), jnp.int32)],
        scratch_shapes=[pltpu.VMEM((POOL_HALO + tm, POOL_WIDTH), F32),
                        pltpu.VMEM((CONV_HALO + tm, CONV_WIDTH), F32),
                        pltpu.VMEM((N_EXPERTS, 1), F32),
                        pltpu.VMEM((tm, tm), BF16)]
                       + [pltpu.VMEM(w.shape, BF16)
                          for w in (w_in, w_grp, w_po, w_co, w_o, s_gate, s_up, s_down)]
                       + [pltpu.VMEM((2, W_IN_CHUNK, w_in.shape[1]), F32),
                          pltpu.VMEM(w_o.shape, F32),
                          pltpu.VMEM((2,) + s_gate.shape, F32),
                          pltpu.VMEM(w_grp.shape, F32),
                          pltpu.SemaphoreType.DMA((7,))],
        compiler_params=pltpu.CompilerParams(dimension_semantics=("arbitrary",),
                                             vmem_limit_bytes=VMEM_LIMIT),
        name="mixer_router",
    )(x2d, g_mix, w_in, b_gate, w_grp, pool_scale, w_po, conv_w, w_co, w_o, g_ffn,
      wr, rbias, s_gate, s_up, s_down)


def _plan_kernel(n_blk, n_pad, counts_ref, idx_ref, rank_ref,
                 dest_ref, blk_e_ref, next_e_ref, n_used_ref, zero_rows_ref, pad_start):
    e_n = N_EXPERTS

    @pl.when(pl.program_id(0) == 0)
    def _():
        sub = lax.broadcasted_iota(jnp.int32, (e_n, e_n), 0)
        lane = lax.broadcasted_iota(jnp.int32, (e_n, e_n), 1)
        c_col = counts_ref[...]
        p_col = ((c_col + (ROW_BLOCK - 1)) // ROW_BLOCK) * ROW_BLOCK
        c_f = c_col.astype(F32)
        p_f = p_col.astype(F32)
        gap_f = p_f - c_f
        to_row = lambda col: jnp.sum(jnp.where(sub == lane, col, 0.0), axis=0, keepdims=True)
        p_row = to_row(p_f)
        gap_row = to_row(gap_f)
        pad_end_col = jnp.sum(jnp.where(lane <= sub, p_row, 0.0), axis=1, keepdims=True)
        gap_before_col = jnp.sum(jnp.where(lane < sub, gap_row, 0.0), axis=1, keepdims=True)
        pad_start[...] = pad_end_col - p_f
        pad_end_last = jnp.sum(p_row, axis=1, keepdims=True)
        n_used_ref[...] = jnp.broadcast_to(pad_end_last * (1.0 / ROW_BLOCK),
                                           n_used_ref.shape).astype(jnp.int32)
        row0 = (lax.broadcasted_iota(jnp.int32, (e_n, blk_e_ref.shape[1]), 1)
                * ROW_BLOCK).astype(F32)
        owner = jnp.sum(jnp.where(pad_end_col <= row0, 1.0, 0.0), axis=0, keepdims=True)
        blk_e_ref[...] = jnp.minimum(owner, float(e_n - 1)).astype(jnp.int32)
        used_later = jnp.logical_and(sub > lane, p_f > 0.0)
        nxt = jnp.min(jnp.where(used_later, sub, e_n), axis=0, keepdims=True)
        next_e_ref[...] = jnp.full(next_e_ref.shape, -1, jnp.int32)
        next_e_ref[:, 0:e_n] = jnp.where(nxt < e_n, nxt, -1)
        for j0 in range(0, n_pad, PLAN_LANES):
            j = (j0 + lax.broadcasted_iota(jnp.int32, (e_n, PLAN_LANES), 1)).astype(F32)
            before = jnp.sum(jnp.where(gap_before_col <= j, c_f, 0.0), axis=0, keepdims=True)
            zero_rows_ref[:, j0:j0 + PLAN_LANES] = (j[0:1, :] + before).astype(jnp.int32)

    tm = idx_ref.shape[1]
    eidx = lax.broadcasted_iota(jnp.int32, (e_n, tm), 0)
    ps = pad_start[...]
    for k in range(TOP_K):
        oh = eidx == idx_ref[k:k + 1, :]
        start = jnp.sum(jnp.where(oh, ps, 0.0), axis=0, keepdims=True)
        dest_ref[k:k + 1, :] = start.astype(jnp.int32) + rank_ref[k:k + 1, :]


def _plan(counts, idx_t, rank_t, n_blk, n_pad):
    t_tok = idx_t.shape[1]
    lanes = 128
    n_blk_p = -(-n_blk // lanes) * lanes
    slot_blk = pl.BlockSpec((TOP_K, TM_DEST), lambda i: (0, i))
    whole = lambda n: pl.BlockSpec((1, n), lambda i: (0, 0))
    dest_t, blk_e, next_e, n_used, zero_rows = pl.pallas_call(
        functools.partial(_plan_kernel, n_blk, n_pad),
        grid=(t_tok // TM_DEST,),
        in_specs=[pl.BlockSpec((N_EXPERTS, 1), lambda i: (0, 0)), slot_blk, slot_blk],
        out_specs=[slot_blk, whole(n_blk_p), whole(lanes), whole(lanes), whole(n_pad)],
        out_shape=[jax.ShapeDtypeStruct((TOP_K, t_tok), jnp.int32),
                   jax.ShapeDtypeStruct((1, n_blk_p), jnp.int32),
                   jax.ShapeDtypeStruct((1, lanes), jnp.int32),
                   jax.ShapeDtypeStruct((1, lanes), jnp.int32),
                   jax.ShapeDtypeStruct((1, n_pad), jnp.int32)],
        scratch_shapes=[pltpu.VMEM((N_EXPERTS, 1), F32)],
        compiler_params=pltpu.CompilerParams(dimension_semantics=("arbitrary",)),
        name="plan",
    )(counts, idx_t, rank_t)
    return (dest_t, blk_e.reshape(-1), next_e.reshape(-1), n_used.reshape(-1),
            zero_rows.reshape(-1))


def _sc_mesh():
    return plsc.VectorSubcoreMesh(core_axis_name="c", subcore_axis_name="s")


def _sc_worker_id():
    return lax.axis_index("s") * SC_CORES + lax.axis_index("c")


def _dispatch(h2p, dest_t, zero_rows, n_rows):
    t_tok, width = h2p.shape
    per_w = t_tok // SC_WORKERS
    n_chunks = per_w // SC_CHUNK
    z_chunks = zero_rows.shape[0] // (SC_WORKERS * SC_CHUNK)
    dest_w = (dest_t.reshape(TOP_K, SC_WORKERS, n_chunks, SC_CHUNK)
              .transpose(1, 2, 0, 3).reshape(SC_WORKERS * n_chunks * TOP_K, SC_CHUNK))
    zero_w = zero_rows.reshape(SC_WORKERS * z_chunks, SC_CHUNK)

    @functools.partial(
        pl.kernel, mesh=_sc_mesh(),
        out_type=jax.ShapeDtypeStruct((n_rows, width), h2p.dtype),
        scratch_types=[pltpu.VMEM((n_chunks * TOP_K, SC_CHUNK), jnp.int32),
                       pltpu.VMEM((z_chunks, SC_CHUNK), jnp.int32),
                       pltpu.VMEM((2, SC_CHUNK, width), h2p.dtype),
                       pltpu.VMEM((SC_CHUNK, width), h2p.dtype),
                       pltpu.SemaphoreType.DMA, pltpu.SemaphoreType.DMA,
                       pltpu.SemaphoreType.DMA],
        name="dispatch",
    )
    def k(h2p_hbm, dest_hbm, zidx_hbm, zsrc_hbm, xs_hbm, idx_v, zidx_v, rows_v, zero_v,
          gsem, wsem, zsem):
        wid = _sc_worker_id()
        base = wid * per_w
        pltpu.sync_copy(dest_hbm.at[pl.ds(wid * n_chunks * TOP_K, n_chunks * TOP_K)], idx_v)
        pltpu.sync_copy(zidx_hbm.at[pl.ds(wid * z_chunks, z_chunks)], zidx_v)
        pltpu.sync_copy(zsrc_hbm, zero_v)

        def zput(j):
            return pltpu.make_async_copy(zero_v, xs_hbm.at[zidx_v.at[j]], zsem)

        for j in range(z_chunks):
            zput(j).start()

        def get(j, slot):
            return pltpu.make_async_copy(h2p_hbm.at[pl.ds(base + j * SC_CHUNK, SC_CHUNK)],
                                         rows_v.at[slot], gsem)

        def put(j, slot, kk):
            return pltpu.make_async_copy(rows_v.at[slot], xs_hbm.at[idx_v.at[j * TOP_K + kk]], wsem)

        get(0, 0).start()

        @pl.loop(0, n_chunks, step=2)
        def _(j):
            for b in range(2):
                jj = j + b
                get(jj, b).wait()

                @pl.when(jj >= 1)
                def _():
                    for kk in range(TOP_K):
                        put(jj - 1, 1 - b, kk).wait()

                @pl.when(jj + 1 < n_chunks)
                def _():
                    get(jj + 1, 1 - b).start()
                for kk in range(TOP_K):
                    put(jj, b, kk).start()

        for kk in range(TOP_K):
            put(n_chunks - 1, (n_chunks - 1) % 2, kk).wait()
        for j in range(z_chunks):
            zput(j).wait()

    return k(h2p, dest_w, zero_w, jnp.zeros((SC_CHUNK, width), h2p.dtype))


def _regroup_sum(ys, dest_t, wsel_t, xres):
    n_slots, t_tok = dest_t.shape
    width = ys.shape[1]
    d_out = xres.shape[1]
    grp = SC_SUM_GROUP
    r_rows = SC_SUM_ROW_RING
    r_acc = SC_SUM_ACC_RING
    per_w = t_tok // SC_WORKERS
    n_sub = per_w // grp

    @functools.partial(
        pl.kernel, mesh=_sc_mesh(),
        out_type=jax.ShapeDtypeStruct((t_tok, d_out), F32),
        scratch_types=[pltpu.VMEM((n_slots, per_w), jnp.int32),
                       pltpu.VMEM((n_slots, per_w), F32),
                       pltpu.VMEM((r_rows, n_slots * grp, width), ys.dtype),
                       pltpu.VMEM((r_acc, grp, d_out), F32),
                       pltpu.SemaphoreType.DMA((r_rows,)), pltpu.SemaphoreType.DMA((r_acc,)),
                       pltpu.SemaphoreType.DMA((r_acc,))],
        compiler_params=pltpu.CompilerParams(needs_layout_passes=False),
        name="regroup_sum",
    )
    def k(ys_hbm, idx_hbm, w_hbm, xres_hbm, out_hbm, idx_v, w_v, rows_v, acc_v, gsem, xsem, psem):
        wid = _sc_worker_id()
        base = wid * per_w
        pltpu.sync_copy(idx_hbm.at[:, pl.ds(base, per_w)], idx_v)
        pltpu.sync_copy(w_hbm.at[:, pl.ds(base, per_w)], w_v)

        def gets(j, slot):
            return [pltpu.make_async_copy(ys_hbm.at[idx_v.at[kk, pl.ds(j * grp, grp)]],
                                          rows_v.at[slot, pl.ds(kk * grp, grp)], gsem.at[slot])
                    for kk in range(n_slots)]

        def xload(j, slot):
            return pltpu.make_async_copy(xres_hbm.at[pl.ds(base + j * grp, grp)], acc_v.at[slot],
                                         xsem.at[slot])

        def put(j, slot):
            return pltpu.make_async_copy(acc_v.at[slot], out_hbm.at[pl.ds(base + j * grp, grp)],
                                         psem.at[slot])

        def accumulate(j, rslot, aslot):
            @pl.loop(0, grp)
            def _(g):
                pos = jnp.full((SC_LANES,), j * grp + g, jnp.int32)
                wk = [plsc.load_gather(w_v, [jnp.full((SC_LANES,), kk, jnp.int32), pos])
                      for kk in range(n_slots)]

                @plsc.parallel_loop(0, width // SC_LANES, unroll=4)
                def _(v):
                    cols_a = pl.ds(v * SC_LANES, SC_LANES)
                    cols_b = pl.ds(width + v * SC_LANES, SC_LANES)
                    acc_a = acc_v[aslot, g, cols_a]
                    acc_b = acc_v[aslot, g, cols_b]
                    for kk in range(n_slots):
                        ya, yb = _unpack_pair(rows_v[rslot, kk * grp + g, cols_a])
                        acc_a = acc_a + ya * wk[kk]
                        acc_b = acc_b + yb * wk[kk]
                    acc_v[aslot, g, cols_a] = acc_a
                    acc_v[aslot, g, cols_b] = acc_b

        for cp in gets(0, 0):
            cp.start()
        xload(0, 0).start()

        @pl.loop(0, n_sub, step=r_acc)
        def _(j):
            for b in range(r_acc):
                jj = j + b
                rs = b % r_rows
                na = (b + 1) % r_acc
                for cp in gets(jj, rs):
                    cp.wait()

                @pl.when(jj + 1 < n_sub)
                def _():
                    for cp in gets(jj + 1, (b + 1) % r_rows):
                        cp.start()
                xload(jj, b).wait()

                @pl.when(jj + 1 < n_sub)
                def _():
                    @pl.when(jj + 1 >= r_acc)
                    def _():
                        put(jj + 1 - r_acc, na).wait()
                    xload(jj + 1, na).start()
                accumulate(jj, rs, b)
                put(jj, b).start()

        for p in range(r_acc):
            put(n_sub - r_acc + p, p).wait()

    return k(ys, dest_t, wsel_t, xres)


def _experts_kernel(n_blk, blk_e_ref, next_e_ref, n_used_ref,
                    xs_hbm, eg_hbm, eu_hbm, ed_hbm, ys_hbm,
                    xbuf, ybuf, hid, stg_g, stg_u, stg_d, wg, wu, wd, xsem, ysem, wsem):
    n = n_used_ref[0]

    def ring(b):
        return jnp.bitwise_and(b, ROW_RING - 1)

    def x_copy(b):
        return pltpu.make_async_copy(xs_hbm.at[pl.ds(pl.multiple_of(b * ROW_BLOCK, ROW_BLOCK),
                                                     ROW_BLOCK)], xbuf.at[ring(b)], xsem.at[ring(b)])

    def y_copy(b):
        return pltpu.make_async_copy(ybuf.at[ring(b)],
                                     ys_hbm.at[pl.ds(pl.multiple_of(b * ROW_BLOCK, ROW_BLOCK),
                                                     ROW_BLOCK)], ysem.at[ring(b)])

    def w_copies(e):
        return (pltpu.make_async_copy(eg_hbm.at[e], stg_g, wsem.at[0]),
                pltpu.make_async_copy(eu_hbm.at[e], stg_u, wsem.at[1]),
                pltpu.make_async_copy(ed_hbm.at[e], stg_d, wsem.at[2]))

    def switch_expert(e, wslot):
        for cp in w_copies(e):
            cp.wait()
        wg[wslot] = stg_g[...].astype(BF16)
        wu[wslot] = stg_u[...].astype(BF16)
        wd[wslot] = stg_d[...].astype(BF16)
        nxt = next_e_ref[e]

        @pl.when(nxt >= 0)
        def _():
            for cp in w_copies(nxt):
                cp.start()

    def gate_up(b, wslot):
        xa, xb = _unpack_pair(xbuf[ring(b)])
        xb16 = jnp.concatenate([xa, xb], axis=1).astype(BF16)
        g = _dot(xb16, wg[wslot])
        up = _dot(xb16, wu[wslot])
        hid[jnp.bitwise_and(b, 1)] = (g * _sigmoid(g) * up).astype(BF16)

    def down(b, wslot):
        y = _dot(hid[jnp.bitwise_and(b, 1)], wd[wslot])
        ybuf[ring(b)] = _pack_pair(y[:, 0:HALF], y[:, HALF:D_MODEL])

    e0 = blk_e_ref[0]
    for cp in w_copies(e0):
        cp.start()
    for j in range(ROW_RING):
        @pl.when(j < n)
        def _():
            x_copy(j).start()
    switch_expert(e0, 0)
    x_copy(0).wait()
    gate_up(0, 0)

    def body(b, wslot_prev):
        e = blk_e_ref[b]
        first = e != blk_e_ref[b - 1]
        wslot = jnp.where(first, 1 - wslot_prev, wslot_prev)

        @pl.when(first)
        def _():
            switch_expert(e, wslot)

        x_copy(b).wait()

        @pl.when(b + ROW_RING - 1 < n)
        def _():
            x_copy(b + ROW_RING - 1).start()

        @pl.when(b >= ROW_RING + 1)
        def _():
            y_copy(b - 1 - ROW_RING).wait()

        down(b - 1, wslot_prev)
        gate_up(b, wslot)
        y_copy(b - 1).start()
        return wslot

    wslot_last = lax.fori_loop(1, n, body, jnp.int32(0))

    last = n - 1

    @pl.when(last >= ROW_RING)
    def _():
        y_copy(last - ROW_RING).wait()
    down(last, wslot_last)
    y_copy(last).start()
    for j in range(ROW_RING - 1, -1, -1):
        @pl.when(last - j >= 0)
        def _():
            y_copy(last - j).wait()

    ybuf[0] = jnp.zeros((ROW_BLOCK, HALF), U32)

    def zero_tail(b, c):
        cp = pltpu.make_async_copy(ybuf.at[0],
                                   ys_hbm.at[pl.ds(pl.multiple_of(b * ROW_BLOCK, ROW_BLOCK),
                                                   ROW_BLOCK)], ysem.at[0])
        cp.start()
        cp.wait()
        return c
    lax.fori_loop(n, n_blk, zero_tail, 0)


def _experts(blk_e, next_e, n_used, xs, e_gate, e_up, e_down):
    n_rows = xs.shape[0]
    n_blk = n_rows // ROW_BLOCK
    any_spec = pl.BlockSpec(memory_space=pl.ANY)
    grid_spec = pltpu.PrefetchScalarGridSpec(
        num_scalar_prefetch=3,
        grid=(1,),
        in_specs=[any_spec, any_spec, any_spec, any_spec],
        out_specs=any_spec,
        scratch_shapes=[pltpu.VMEM((ROW_RING, ROW_BLOCK, HALF), U32),
                        pltpu.VMEM((ROW_RING, ROW_BLOCK, HALF), U32),
                        pltpu.VMEM((2, ROW_BLOCK, EXPERT_HIDDEN), BF16),
                        pltpu.VMEM((D_MODEL, EXPERT_HIDDEN), F32),
                        pltpu.VMEM((D_MODEL, EXPERT_HIDDEN), F32),
                        pltpu.VMEM((EXPERT_HIDDEN, D_MODEL), F32),
                        pltpu.VMEM((2, D_MODEL, EXPERT_HIDDEN), BF16),
                        pltpu.VMEM((2, D_MODEL, EXPERT_HIDDEN), BF16),
                        pltpu.VMEM((2, EXPERT_HIDDEN, D_MODEL), BF16),
                        pltpu.SemaphoreType.DMA((ROW_RING,)),
                        pltpu.SemaphoreType.DMA((ROW_RING,)),
                        pltpu.SemaphoreType.DMA((3,))],
    )
    return pl.pallas_call(
        functools.partial(_experts_kernel, n_blk),
        grid_spec=grid_spec,
        out_shape=jax.ShapeDtypeStruct((n_rows, HALF), U32),
        compiler_params=pltpu.CompilerParams(dimension_semantics=("arbitrary",)),
        name="experts",
    )(blk_e, next_e, n_used, xs, e_gate, e_up, e_down)


def _final_norm_kernel(x_ref, g_ref, out_ref):
    out_ref[...] = _rms(x_ref[...], g_ref[...])


def _final_norm(xsum, g_final):
    t_tok = xsum.shape[0]
    blk = pl.BlockSpec((TM_NORM, D_MODEL), lambda i: (i, 0))
    return pl.pallas_call(
        _final_norm_kernel,
        grid=(t_tok // TM_NORM,),
        in_specs=[blk, pl.BlockSpec((1, D_MODEL), lambda i: (0, 0))],
        out_specs=blk,
        out_shape=jax.ShapeDtypeStruct((t_tok, D_MODEL), F32),
        compiler_params=pltpu.CompilerParams(dimension_semantics=("arbitrary",)),
        name="final_norm",
    )(xsum, g_final)


def kernel(x, g_mix, w_in, b_gate, w_pool_group, pool_scale, w_pool_out, conv_w, w_conv_out, w_o,
           g_ffn, w_router, router_bias, e_gate, e_up, e_down, s_gate, s_up, s_down, g_final):
    b, s, d = x.shape
    t_tok = b * s
    n_pad = N_EXPERTS * ROW_BLOCK
    n_rows = t_tok * TOP_K + n_pad
    n_blk = n_rows // ROW_BLOCK
    assert d == D_MODEL and s % TM_MIX == 0 and TM_MIX >= POOL_HALO
    assert t_tok % TM_DEST == 0 and t_tok % TM_NORM == 0 and n_pad % PLAN_LANES == 0
    assert t_tok % (2 * SC_WORKERS * SC_CHUNK) == 0 and n_pad % (SC_WORKERS * SC_CHUNK) == 0
    assert t_tok % (SC_WORKERS * SC_SUM_GROUP * SC_SUM_ACC_RING) == 0

    row = lambda a: a.reshape(1, -1)
    wr_t = w_router.T.astype(F32)
    wr_hi = wr_t.astype(BF16)
    wr = jnp.concatenate([wr_hi, (wr_t - wr_hi.astype(F32)).astype(BF16)], axis=0)

    xres, h2p, idx_t, wsel_t, rank_t, counts = _mixer_router(
        x.reshape(t_tok, d), 0, t_tok, s, row(g_mix), w_in, row(b_gate), w_pool_group,
        row(pool_scale), w_pool_out, conv_w, w_conv_out, w_o, row(g_ffn), wr,
        router_bias.astype(F32).reshape(N_EXPERTS, 1), s_gate, s_up, s_down)
    dest_t, blk_e, next_e, n_used, zero_rows = _plan(counts, idx_t, rank_t, n_blk, n_pad)
    xs = _dispatch(h2p, dest_t, zero_rows, n_rows)
    ys = _experts(blk_e, next_e, n_used, xs, e_gate, e_up, e_down)
    xsum = _regroup_sum(ys, dest_t, wsel_t, xres)
    return _final_norm(xsum, row(g_final)).reshape(b, s, d)
```

```python
import functools

import jax
import jax.numpy as jnp
from jax import lax
from jax.experimental import pallas as pl
from jax.experimental.pallas import tpu as pltpu
from jax.experimental.pallas import tpu_sc as plsc

D_MODEL = 1024
HALF = D_MODEL // 2
POOL_WIDTH = 512
N_POOL_GROUPS = 4
POOL_GROUP = 128
POOL_WINDOWS = (2, 4, 8, 16)
CONV_WIDTH = 512
N_EXPERTS = 64
TOP_K = 8
EXPERT_HIDDEN = 256
SHARED_HIDDEN = 256
ROUTED_SCALE = 2.5
EPS = 1e-6

POOL_HALO = 16
CONV_HALO = 8
TM_MIX = 512
W_IN_CHUNK = 128
TM_DEST = 2048
PLAN_LANES = 2048
ROW_BLOCK = 512
ROW_RING = 8
TM_NORM = 512
N_SPLITS = 2
VMEM_LIMIT = 56 * 1024 * 1024

SC_CORES = 2
SC_SUBCORES = 16
SC_WORKERS = SC_CORES * SC_SUBCORES
SC_LANES = 16
SC_CHUNK = 64
SC_SUM_GROUP = 8
SC_SUM_ROW_RING = 2
SC_SUM_ACC_RING = 4

BF16 = jnp.bfloat16
F32 = jnp.float32
U32 = jnp.uint32


def _rms(x, g):
    r = lax.rsqrt(jnp.mean(x * x, axis=-1, keepdims=True) + EPS)
    return (x * r) * g


def _dot(a, b):
    return jnp.dot(a, b, preferred_element_type=F32)


def _sigmoid(z):
    return 0.5 * jnp.tanh(0.5 * z) + 0.5


def _pack_pair(a, b):
    ra = lax.bitcast_convert_type(a.astype(BF16).astype(F32), U32)
    rb = lax.bitcast_convert_type(b.astype(BF16).astype(F32), U32)
    return ra | (rb >> 16)


def _unpack_pair(w):
    a = lax.bitcast_convert_type(w & jnp.uint32(0xFFFF0000), F32)
    b = lax.bitcast_convert_type(w << 16, F32)
    return a, b


def _load_weights_bf16(w_in_hbm, w_grp_hbm, w_po_hbm, w_co_hbm, w_o_hbm, s_gate_hbm, s_up_hbm,
                       s_down_hbm, w_in_ref, w_grp_ref, w_po_ref, w_co_ref, w_o_ref, s_gate_ref,
                       s_up_ref, s_down_ref, stg_in, stg_sq, stg_sh, stg_grp, wsem):
    copy = lambda src, dst, k: pltpu.make_async_copy(src, dst, wsem.at[k])
    rows = stg_in.shape[1]
    n_in = w_in_hbm.shape[0] // rows
    half = w_po_hbm.shape[0]
    c_in = [copy(w_in_hbm.at[pl.ds(j * rows, rows)], stg_in.at[j % 2], j % 2) for j in range(n_in)]
    c_o = copy(w_o_hbm, stg_sq, 2)
    c_sg = copy(s_gate_hbm, stg_sh.at[0], 3)
    c_su = copy(s_up_hbm, stg_sh.at[1], 4)
    c_grp = copy(w_grp_hbm, stg_grp, 5)
    c_po = copy(w_po_hbm, stg_sq.at[0:half], 2)
    c_co = copy(w_co_hbm, stg_sq.at[half:2 * half], 6)
    c_sd = copy(s_down_hbm, stg_sq.at[0:s_down_hbm.shape[0]], 2)
    for cp in (c_in[0], c_in[1], c_o, c_sg, c_su, c_grp):
        cp.start()
    for j in range(n_in):
        c_in[j].wait()
        w_in_ref[j * rows:(j + 1) * rows, :] = stg_in[j % 2].astype(BF16)
        if j + 2 < n_in:
            c_in[j + 2].start()
    c_o.wait()
    w_o_ref[...] = stg_sq[...].astype(BF16)
    c_po.start()
    c_co.start()
    c_sg.wait()
    s_gate_ref[...] = stg_sh[0].astype(BF16)
    c_su.wait()
    s_up_ref[...] = stg_sh[1].astype(BF16)
    c_grp.wait()
    w_grp_ref[...] = stg_grp[...].astype(BF16)
    c_po.wait()
    w_po_ref[...] = stg_sq[0:half, :].astype(BF16)
    c_co.wait()
    w_co_ref[...] = stg_sq[half:2 * half, :].astype(BF16)
    c_sd.start()
    c_sd.wait()
    s_down_ref[...] = stg_sq[0:s_down_hbm.shape[0], :].astype(BF16)


def _mixer_router_kernel(n_seq_tiles,
                         x_ref, g_mix_ref, w_in_hbm, b_gate_ref, w_grp_hbm, pool_scale_ref,
                         w_po_hbm, conv_w_ref, w_co_hbm, w_o_hbm, g_ffn_ref,
                         wr_ref, rbias_ref, s_gate_hbm, s_up_hbm, s_down_hbm,
                         xres_ref, h2p_ref, idx_ref, wsel_ref, rank_ref, counts_ref,
                         ext_pool, ext_conv, cnt_carry, tri,
                         w_in_ref, w_grp_ref, w_po_ref, w_co_ref, w_o_ref, s_gate_ref, s_up_ref,
                         s_down_ref, stg_in, stg_sq, stg_sh, stg_grp, wsem):
    tm = x_ref.shape[0]
    i = pl.program_id(0)
    st = i % n_seq_tiles

    @pl.when(i == 0)
    def _():
        _load_weights_bf16(w_in_hbm, w_grp_hbm, w_po_hbm, w_co_hbm, w_o_hbm, s_gate_hbm, s_up_hbm,
                           s_down_hbm, w_in_ref, w_grp_ref, w_po_ref, w_co_ref, w_o_ref,
                           s_gate_ref, s_up_ref, s_down_ref, stg_in, stg_sq, stg_sh, stg_grp, wsem)
        r = lax.broadcasted_iota(jnp.int32, (tm, tm), 0)
        c = lax.broadcasted_iota(jnp.int32, (tm, tm), 1)
        tri[...] = (r < c).astype(BF16)
        cnt_carry[...] = jnp.zeros_like(cnt_carry)

    @pl.when(st == 0)
    def _():
        ext_pool[0:POOL_HALO, :] = jnp.zeros((POOL_HALO, POOL_WIDTH), F32)
        ext_conv[0:CONV_HALO, :] = jnp.zeros((CONV_HALO, CONV_WIDTH), F32)

    x = x_ref[...]
    hb = _rms(x, g_mix_ref[...]).astype(BF16)

    o0 = POOL_WIDTH
    o1 = o0 + CONV_WIDTH
    o2 = o1 + CONV_WIDTH
    o3 = o2 + CONV_WIDTH

    u = _dot(hb, w_in_ref[:, 0:o0])
    ext_pool[POOL_HALO:POOL_HALO + tm, :] = u
    gc = _dot(hb, w_in_ref[:, o1:o2])
    v = _dot(hb, w_in_ref[:, o2:o3])
    pre_a = _dot(hb, w_in_ref[:, o3:o3 + D_MODEL])

    t_glob = st * tm + lax.broadcasted_iota(jnp.int32, (tm, 1), 0)
    mixed = []
    for gi, w in enumerate(POOL_WINDOWS):
        cols = slice(gi * POOL_GROUP, (gi + 1) * POOL_GROUP)
        ug = u[:, cols]
        acc = ug
        for j in range(1, w):
            acc = acc + ext_pool[POOL_HALO - j:POOL_HALO - j + tm, cols]
        cnt = jnp.minimum(t_glob + 1, w).astype(F32)
        pooled = acc * (1.0 / cnt) - ug
        mixed.append(_dot(pooled.astype(BF16), w_grp_ref[gi]))
    ext_pool[0:POOL_HALO, :] = ext_pool[tm:tm + POOL_HALO, :]
    pre_b = _dot(hb, w_in_ref[:, o3 + D_MODEL:o3 + 2 * D_MODEL])
    gb = _dot(hb, w_in_ref[:, o0:o1])

    cv = gc * v
    ext_conv[CONV_HALO:CONV_HALO + tm, :] = cv
    conv = (ext_conv[CONV_HALO - 2:CONV_HALO - 2 + tm, :] * conv_w_ref[0:1, :]
            + ext_conv[CONV_HALO - 1:CONV_HALO - 1 + tm, :] * conv_w_ref[1:2, :]
            + cv * conv_w_ref[2:3, :])
    ext_conv[0:CONV_HALO, :] = ext_conv[tm:tm + CONV_HALO, :]
    branch_b = _dot((gb * conv).astype(BF16), w_co_ref[...])
    mixed = jnp.concatenate(mixed, axis=1) * pool_scale_ref[...]
    branch_a = _dot(mixed.astype(BF16), w_po_ref[...])

    merged = (_sigmoid(pre_a + b_gate_ref[:, 0:D_MODEL]) * branch_a
              + _sigmoid(pre_b + b_gate_ref[:, D_MODEL:2 * D_MODEL]) * branch_b)
    x1 = x + _dot(merged.astype(BF16), w_o_ref[...])

    h2 = _rms(x1, g_ffn_ref[...])
    h2p_ref[...] = _pack_pair(h2[:, 0:HALF], h2[:, HALF:D_MODEL])
    h2b = h2.astype(BF16)

    nt = (((1,), (1,)), ((), ()))
    parts = lax.dot_general(wr_ref[...], h2b, nt, preferred_element_type=F32)
    logits = parts[0:N_EXPERTS, :] + parts[N_EXPERTS:2 * N_EXPERTS, :]
    sg = _dot(h2b, s_gate_ref[...])
    su = _dot(h2b, s_up_ref[...])
    scores = jax.nn.sigmoid(logits)
    sel = scores + rbias_ref[...]
    eidx = lax.broadcasted_iota(jnp.int32, (N_EXPERTS, tm), 0).astype(F32)
    e_rows, w_rows = [], []
    mask = jnp.zeros((N_EXPERTS, tm), F32)
    for _ in range(TOP_K):
        m = jnp.max(sel, axis=0, keepdims=True)
        ek = jnp.min(jnp.where(sel == m, eidx, float(N_EXPERTS)), axis=0, keepdims=True)
        oh = eidx == ek
        w_rows.append(jnp.sum(jnp.where(oh, scores, 0.0), axis=0, keepdims=True))
        e_rows.append(ek)
        mask = mask + oh.astype(F32)
        sel = jnp.where(oh, -jnp.inf, sel)

    shared = _dot((sg * _sigmoid(sg) * su).astype(BF16), s_down_ref[...])
    xres_ref[...] = x1 + shared

    wsum = w_rows[0]
    for k in range(1, TOP_K):
        wsum = wsum + w_rows[k]

    before = _dot(mask.astype(BF16), tri[...]) + cnt_carry[...]
    for k in range(TOP_K):
        oh = eidx == e_rows[k]
        rank_ref[k:k + 1, :] = jnp.sum(jnp.where(oh, before, 0.0), axis=0,
                                       keepdims=True).astype(jnp.int32)
        idx_ref[k:k + 1, :] = e_rows[k].astype(jnp.int32)
        wsel_ref[k:k + 1, :] = w_rows[k] / wsum * ROUTED_SCALE
    total = cnt_carry[...] + jnp.sum(mask, axis=1, keepdims=True)
    cnt_carry[...] = total
    counts_ref[...] = total.astype(jnp.int32)


def _mixer_router(x2d, tok0, t_tok, seq_len, g_mix, w_in, b_gate, w_grp, pool_scale, w_po, conv_w,
                  w_co, w_o, g_ffn, wr, rbias, s_gate, s_up, s_down):
    tm = TM_MIX
    n_seq_tiles = seq_len // tm
    off = tok0 // tm
    const = lambda shape: pl.BlockSpec(shape, lambda i: (0,) * len(shape),
                                       pipeline_mode=pl.Buffered(1))
    hbm = pl.BlockSpec(memory_space=pl.ANY)
    x_blk = pl.BlockSpec((tm, D_MODEL), lambda i: (i + off, 0))
    row_blk = pl.BlockSpec((tm, D_MODEL), lambda i: (i, 0))
    half_blk = pl.BlockSpec((tm, HALF), lambda i: (i, 0))
    slot_blk = pl.BlockSpec((TOP_K, tm), lambda i: (0, i))
    return pl.pallas_call(
        functools.partial(_mixer_router_kernel, n_seq_tiles),
        grid=(t_tok // tm,),
        in_specs=[x_blk, const(g_mix.shape), hbm, const(b_gate.shape),
                  hbm, const(pool_scale.shape), hbm,
                  const(conv_w.shape), hbm, hbm, const(g_ffn.shape),
                  const(wr.shape), const(rbias.shape), hbm, hbm, hbm],
        out_specs=[row_blk, half_blk, slot_blk, slot_blk, slot_blk,
                   pl.BlockSpec((N_EXPERTS, 1), lambda i: (0, 0))],
        out_shape=[jax.ShapeDtypeStruct((t_tok, D_MODEL), F32),
                   jax.ShapeDtypeStruct((t_tok, HALF), U32),
                   jax.ShapeDtypeStruct((TOP_K, t_tok), jnp.int32),
                   jax.ShapeDtypeStruct((TOP_K, t_tok), F32),
                   jax.ShapeDtypeStruct((TOP_K, t_tok), jnp.int32),
                   jax.ShapeDtypeStruct((N_EXPERTS, 1), jnp.int32)],
        scratch_shapes=[pltpu.VMEM((POOL_HALO + tm, POOL_WIDTH), F32),
                        pltpu.VMEM((CONV_HALO + tm, CONV_WIDTH), F32),
                        pltpu.VMEM((N_EXPERTS, 1), F32),
                        pltpu.VMEM((tm, tm), BF16)]
                       + [pltpu.VMEM(w.shape, BF16)
                          for w in (w_in, w_grp, w_po, w_co, w_o, s_gate, s_up, s_down)]
                       + [pltpu.VMEM((2, W_IN_CHUNK, w_in.shape[1]), F32),
                          pltpu.VMEM(w_o.shape, F32),
                          pltpu.VMEM((2,) + s_gate.shape, F32),
                          pltpu.VMEM(w_grp.shape, F32),
                          pltpu.SemaphoreType.DMA((7,))],
        compiler_params=pltpu.CompilerParams(dimension_semantics=("arbitrary",),
                                             vmem_limit_bytes=VMEM_LIMIT),
        name="mixer_router",
    )(x2d, g_mix, w_in, b_gate, w_grp, pool_scale, w_po, conv_w, w_co, w_o, g_ffn,
      wr, rbias, s_gate, s_up, s_down)


def _plan_kernel(n_blk, n_pad, counts_ref, idx_ref, rank_ref,
                 dest_ref, blk_e_ref, next_e_ref, n_used_ref, zero_rows_ref, pad_start):
    e_n = N_EXPERTS

    @pl.when(pl.program_id(0) == 0)
    def _():
        sub = lax.broadcasted_iota(jnp.int32, (e_n, e_n), 0)
        lane = lax.broadcasted_iota(jnp.int32, (e_n, e_n), 1)
        c_col = counts_ref[...]
        p_col = ((c_col + (ROW_BLOCK - 1)) // ROW_BLOCK) * ROW_BLOCK
        c_f = c_col.astype(F32)
        p_f = p_col.astype(F32)
        gap_f = p_f - c_f
        to_row = lambda col: jnp.sum(jnp.where(sub == lane, col, 0.0), axis=0, keepdims=True)
        p_row = to_row(p_f)
        gap_row = to_row(gap_f)
        pad_end_col = jnp.sum(jnp.where(lane <= sub, p_row, 0.0), axis=1, keepdims=True)
        gap_before_col = jnp.sum(jnp.where(lane < sub, gap_row, 0.0), axis=1, keepdims=True)
        pad_start[...] = pad_end_col - p_f
        pad_end_last = jnp.sum(p_row, axis=1, keepdims=True)
        n_used_ref[...] = jnp.broadcast_to(pad_end_last * (1.0 / ROW_BLOCK),
                                           n_used_ref.shape).astype(jnp.int32)
        row0 = (lax.broadcasted_iota(jnp.int32, (e_n, blk_e_ref.shape[1]), 1)
                * ROW_BLOCK).astype(F32)
        owner = jnp.sum(jnp.where(pad_end_col <= row0, 1.0, 0.0), axis=0, keepdims=True)
        blk_e_ref[...] = jnp.minimum(owner, float(e_n - 1)).astype(jnp.int32)
        used_later = jnp.logical_and(sub > lane, p_f > 0.0)
        nxt = jnp.min(jnp.where(used_later, sub, e_n), axis=0, keepdims=True)
        next_e_ref[...] = jnp.full(next_e_ref.shape, -1, jnp.int32)
        next_e_ref[:, 0:e_n] = jnp.where(nxt < e_n, nxt, -1)
        for j0 in range(0, n_pad, PLAN_LANES):
            j = (j0 + lax.broadcasted_iota(jnp.int32, (e_n, PLAN_LANES), 1)).astype(F32)
            before = jnp.sum(jnp.where(gap_before_col <= j, c_f, 0.0), axis=0, keepdims=True)
            zero_rows_ref[:, j0:j0 + PLAN_LANES] = (j[0:1, :] + before).astype(jnp.int32)

    tm = idx_ref.shape[1]
    eidx = lax.broadcasted_iota(jnp.int32, (e_n, tm), 0)
    ps = pad_start[...]
    for k in range(TOP_K):
        oh = eidx == idx_ref[k:k + 1, :]
        start = jnp.sum(jnp.where(oh, ps, 0.0), axis=0, keepdims=True)
        dest_ref[k:k + 1, :] = start.astype(jnp.int32) + rank_ref[k:k + 1, :]


def _plan(counts, idx_t, rank_t, n_blk, n_pad):
    t_tok = idx_t.shape[1]
    lanes = 128
    n_blk_p = -(-n_blk // lanes) * lanes
    slot_blk = pl.BlockSpec((TOP_K, TM_DEST), lambda i: (0, i))
    whole = lambda n: pl.BlockSpec((1, n), lambda i: (0, 0))
    dest_t, blk_e, next_e, n_used, zero_rows = pl.pallas_call(
        functools.partial(_plan_kernel, n_blk, n_pad),
        grid=(t_tok // TM_DEST,),
        in_specs=[pl.BlockSpec((N_EXPERTS, 1), lambda i: (0, 0)), slot_blk, slot_blk],
        out_specs=[slot_blk, whole(n_blk_p), whole(lanes), whole(lanes), whole(n_pad)],
        out_shape=[jax.ShapeDtypeStruct((TOP_K, t_tok), jnp.int32),
                   jax.ShapeDtypeStruct((1, n_blk_p), jnp.int32),
                   jax.ShapeDtypeStruct((1, lanes), jnp.int32),
                   jax.ShapeDtypeStruct((1, lanes), jnp.int32),
                   jax.ShapeDtypeStruct((1, n_pad), jnp.int32)],
        scratch_shapes=[pltpu.VMEM((N_EXPERTS, 1), F32)],
        compiler_params=pltpu.CompilerParams(dimension_semantics=("arbitrary",)),
        name="plan",
    )(counts, idx_t, rank_t)
    return (dest_t, blk_e.reshape(-1), next_e.reshape(-1), n_used.reshape(-1),
            zero_rows.reshape(-1))


def _sc_mesh():
    return plsc.VectorSubcoreMesh(core_axis_name="c", subcore_axis_name="s")


def _sc_worker_id():
    return lax.axis_index("s") * SC_CORES + lax.axis_index("c")


def _dispatch(h2p, dest_t, zero_rows, n_rows):
    t_tok, width = h2p.shape
    per_w = t_tok // SC_WORKERS
    n_chunks = per_w // SC_CHUNK
    z_chunks = zero_rows.shape[0] // (SC_WORKERS * SC_CHUNK)
    dest_w = (dest_t.reshape(TOP_K, SC_WORKERS, n_chunks, SC_CHUNK)
              .transpose(1, 2, 0, 3).reshape(SC_WORKERS * n_chunks * TOP_K, SC_CHUNK))
    zero_w = zero_rows.reshape(SC_WORKERS * z_chunks, SC_CHUNK)

    @functools.partial(
        pl.kernel, mesh=_sc_mesh(),
        out_type=jax.ShapeDtypeStruct((n_rows, width), h2p.dtype),
        scratch_types=[pltpu.VMEM((n_chunks * TOP_K, SC_CHUNK), jnp.int32),
                       pltpu.VMEM((z_chunks, SC_CHUNK), jnp.int32),
                       pltpu.VMEM((2, SC_CHUNK, width), h2p.dtype),
                       pltpu.VMEM((SC_CHUNK, width), h2p.dtype),
                       pltpu.SemaphoreType.DMA, pltpu.SemaphoreType.DMA,
                       pltpu.SemaphoreType.DMA],
        name="dispatch",
    )
    def k(h2p_hbm, dest_hbm, zidx_hbm, zsrc_hbm, xs_hbm, idx_v, zidx_v, rows_v, zero_v,
          gsem, wsem, zsem):
        wid = _sc_worker_id()
        base = wid * per_w
        pltpu.sync_copy(dest_hbm.at[pl.ds(wid * n_chunks * TOP_K, n_chunks * TOP_K)], idx_v)
        pltpu.sync_copy(zidx_hbm.at[pl.ds(wid * z_chunks, z_chunks)], zidx_v)
        pltpu.sync_copy(zsrc_hbm, zero_v)

        def zput(j):
            return pltpu.make_async_copy(zero_v, xs_hbm.at[zidx_v.at[j]], zsem)

        for j in range(z_chunks):
            zput(j).start()

        def get(j, slot):
            return pltpu.make_async_copy(h2p_hbm.at[pl.ds(base + j * SC_CHUNK, SC_CHUNK)],
                                         rows_v.at[slot], gsem)

        def put(j, slot, kk):
            return pltpu.make_async_copy(rows_v.at[slot], xs_hbm.at[idx_v.at[j * TOP_K + kk]], wsem)

        get(0, 0).start()

        @pl.loop(0, n_chunks, step=2)
        def _(j):
            for b in range(2):
                jj = j + b
                get(jj, b).wait()

                @pl.when(jj >= 1)
                def _():
                    for kk in range(TOP_K):
                        put(jj - 1, 1 - b, kk).wait()

                @pl.when(jj + 1 < n_chunks)
                def _():
                    get(jj + 1, 1 - b).start()
                for kk in range(TOP_K):
                    put(jj, b, kk).start()

        for kk in range(TOP_K):
            put(n_chunks - 1, (n_chunks - 1) % 2, kk).wait()
        for j in range(z_chunks):
            zput(j).wait()

    return k(h2p, dest_w, zero_w, jnp.zeros((SC_CHUNK, width), h2p.dtype))


def _regroup_sum(ys, dest_t, wsel_t, xres):
    n_slots, t_tok = dest_t.shape
    width = ys.shape[1]
    d_out = xres.shape[1]
    grp = SC_SUM_GROUP
    r_rows = SC_SUM_ROW_RING
    r_acc = SC_SUM_ACC_RING
    per_w = t_tok // SC_WORKERS
    n_sub = per_w // grp

    @functools.partial(
        pl.kernel, mesh=_sc_mesh(),
        out_type=jax.ShapeDtypeStruct((t_tok, d_out), F32),
        scratch_types=[pltpu.VMEM((n_slots, per_w), jnp.int32),
                       pltpu.VMEM((n_slots, per_w), F32),
                       pltpu.VMEM((r_rows, n_slots * grp, width), ys.dtype),
                       pltpu.VMEM((r_acc, grp, d_out), F32),
                       pltpu.SemaphoreType.DMA((r_rows,)), pltpu.SemaphoreType.DMA((r_acc,)),
                       pltpu.SemaphoreType.DMA((r_acc,))],
        compiler_params=pltpu.CompilerParams(needs_layout_passes=False),
        name="regroup_sum",
    )
    def k(ys_hbm, idx_hbm, w_hbm, xres_hbm, out_hbm, idx_v, w_v, rows_v, acc_v, gsem, xsem, psem):
        wid = _sc_worker_id()
        base = wid * per_w
        pltpu.sync_copy(idx_hbm.at[:, pl.ds(base, per_w)], idx_v)
        pltpu.sync_copy(w_hbm.at[:, pl.ds(base, per_w)], w_v)

        def gets(j, slot):
            return [pltpu.make_async_copy(ys_hbm.at[idx_v.at[kk, pl.ds(j * grp, grp)]],
                                          rows_v.at[slot, pl.ds(kk * grp, grp)], gsem.at[slot])
                    for kk in range(n_slots)]

        def xload(j, slot):
            return pltpu.make_async_copy(xres_hbm.at[pl.ds(base + j * grp, grp)], acc_v.at[slot],
                                         xsem.at[slot])

        def put(j, slot):
            return pltpu.make_async_copy(acc_v.at[slot], out_hbm.at[pl.ds(base + j * grp, grp)],
                                         psem.at[slot])

        def accumulate(j, rslot, aslot):
            @pl.loop(0, grp)
            def _(g):
                pos = jnp.full((SC_LANES,), j * grp + g, jnp.int32)
                wk = [plsc.load_gather(w_v, [jnp.full((SC_LANES,), kk, jnp.int32), pos])
                      for kk in range(n_slots)]

                @plsc.parallel_loop(0, width // SC_LANES, unroll=4)
                def _(v):
                    cols_a = pl.ds(v * SC_LANES, SC_LANES)
                    cols_b = pl.ds(width + v * SC_LANES, SC_LANES)
                    acc_a = acc_v[aslot, g, cols_a]
                    acc_b = acc_v[aslot, g, cols_b]
                    for kk in range(n_slots):
                        ya, yb = _unpack_pair(rows_v[rslot, kk * grp + g, cols_a])
                        acc_a = acc_a + ya * wk[kk]
                        acc_b = acc_b + yb * wk[kk]
                    acc_v[aslot, g, cols_a] = acc_a
                    acc_v[aslot, g, cols_b] = acc_b

        for cp in gets(0, 0):
            cp.start()
        xload(0, 0).start()

        @pl.loop(0, n_sub, step=r_acc)
        def _(j):
            for b in range(r_acc):
                jj = j + b
                rs = b % r_rows
                na = (b + 1) % r_acc
                for cp in gets(jj, rs):
                    cp.wait()

                @pl.when(jj + 1 < n_sub)
                def _():
                    for cp in gets(jj + 1, (b + 1) % r_rows):
                        cp.start()
                xload(jj, b).wait()

                @pl.when(jj + 1 < n_sub)
                def _():
                    @pl.when(jj + 1 >= r_acc)
                    def _():
                        put(jj + 1 - r_acc, na).wait()
                    xload(jj + 1, na).start()
                accumulate(jj, rs, b)
                put(jj, b).start()

        for p in range(r_acc):
            put(n_sub - r_acc + p, p).wait()

    return k(ys, dest_t, wsel_t, xres)


def _experts_kernel(n_blk, blk_e_ref, next_e_ref, n_used_ref,
                    xs_hbm, eg_hbm, eu_hbm, ed_hbm, ys_hbm,
                    xbuf, ybuf, hid, stg_g, stg_u, stg_d, wg, wu, wd, xsem, ysem, wsem):
    n = n_used_ref[0]

    def ring(b):
        return jnp.bitwise_and(b, ROW_RING - 1)

    def x_copy(b):
        return pltpu.make_async_copy(xs_hbm.at[pl.ds(pl.multiple_of(b * ROW_BLOCK, ROW_BLOCK),
                                                     ROW_BLOCK)], xbuf.at[ring(b)], xsem.at[ring(b)])

    def y_copy(b):
        return pltpu.make_async_copy(ybuf.at[ring(b)],
                                     ys_hbm.at[pl.ds(pl.multiple_of(b * ROW_BLOCK, ROW_BLOCK),
                                                     ROW_BLOCK)], ysem.at[ring(b)])

    def w_copies(e):
        return (pltpu.make_async_copy(eg_hbm.at[e], stg_g, wsem.at[0]),
                pltpu.make_async_copy(eu_hbm.at[e], stg_u, wsem.at[1]),
                pltpu.make_async_copy(ed_hbm.at[e], stg_d, wsem.at[2]))

    def switch_expert(e, wslot):
        for cp in w_copies(e):
            cp.wait()
        wg[wslot] = stg_g[...].astype(BF16)
        wu[wslot] = stg_u[...].astype(BF16)
        wd[wslot] = stg_d[...].astype(BF16)
        nxt = next_e_ref[e]

        @pl.when(nxt >= 0)
        def _():
            for cp in w_copies(nxt):
                cp.start()

    def gate_up(b, wslot):
        xa, xb = _unpack_pair(xbuf[ring(b)])
        xb16 = jnp.concatenate([xa, xb], axis=1).astype(BF16)
        g = _dot(xb16, wg[wslot])
        up = _dot(xb16, wu[wslot])
        hid[jnp.bitwise_and(b, 1)] = (g * _sigmoid(g) * up).astype(BF16)

    def down(b, wslot):
        y = _dot(hid[jnp.bitwise_and(b, 1)], wd[wslot])
        ybuf[ring(b)] = _pack_pair(y[:, 0:HALF], y[:, HALF:D_MODEL])

    e0 = blk_e_ref[0]
    for cp in w_copies(e0):
        cp.start()
    for j in range(ROW_RING):
        @pl.when(j < n)
        def _():
            x_copy(j).start()
    switch_expert(e0, 0)
    x_copy(0).wait()
    gate_up(0, 0)

    def body(b, wslot_prev):
        e = blk_e_ref[b]
        first = e != blk_e_ref[b - 1]
        wslot = jnp.where(first, 1 - wslot_prev, wslot_prev)

        @pl.when(first)
        def _():
            switch_expert(e, wslot)

        x_copy(b).wait()

        @pl.when(b + ROW_RING - 1 < n)
        def _():
            x_copy(b + ROW_RING - 1).start()

        @pl.when(b >= ROW_RING + 1)
        def _():
            y_copy(b - 1 - ROW_RING).wait()

        down(b - 1, wslot_prev)
        gate_up(b, wslot)
        y_copy(b - 1).start()
        return wslot

    wslot_last = lax.fori_loop(1, n, body, jnp.int32(0))

    last = n - 1

    @pl.when(last >= ROW_RING)
    def _():
        y_copy(last - ROW_RING).wait()
    down(last, wslot_last)
    y_copy(last).start()
    for j in range(ROW_RING - 1, -1, -1):
        @pl.when(last - j >= 0)
        def _():
            y_copy(last - j).wait()

    ybuf[0] = jnp.zeros((ROW_BLOCK, HALF), U32)

    def zero_tail(b, c):
        cp = pltpu.make_async_copy(ybuf.at[0],
                                   ys_hbm.at[pl.ds(pl.multiple_of(b * ROW_BLOCK, ROW_BLOCK),
                                                   ROW_BLOCK)], ysem.at[0])
        cp.start()
        cp.wait()
        return c
    lax.fori_loop(n, n_blk, zero_tail, 0)


def _experts(blk_e, next_e, n_used, xs, e_gate, e_up, e_down):
    n_rows = xs.shape[0]
    n_blk = n_rows // ROW_BLOCK
    any_spec = pl.BlockSpec(memory_space=pl.ANY)
    grid_spec = pltpu.PrefetchScalarGridSpec(
        num_scalar_prefetch=3,
        grid=(1,),
        in_specs=[any_spec, any_spec, any_spec, any_spec],
        out_specs=any_spec,
        scratch_shapes=[pltpu.VMEM((ROW_RING, ROW_BLOCK, HALF), U32),
                        pltpu.VMEM((ROW_RING, ROW_BLOCK, HALF), U32),
                        pltpu.VMEM((2, ROW_BLOCK, EXPERT_HIDDEN), BF16),
                        pltpu.VMEM((D_MODEL, EXPERT_HIDDEN), F32),
                        pltpu.VMEM((D_MODEL, EXPERT_HIDDEN), F32),
                        pltpu.VMEM((EXPERT_HIDDEN, D_MODEL), F32),
                        pltpu.VMEM((2, D_MODEL, EXPERT_HIDDEN), BF16),
                        pltpu.VMEM((2, D_MODEL, EXPERT_HIDDEN), BF16),
                        pltpu.VMEM((2, EXPERT_HIDDEN, D_MODEL), BF16),
                        pltpu.SemaphoreType.DMA((ROW_RING,)),
                        pltpu.SemaphoreType.DMA((ROW_RING,)),
                        pltpu.SemaphoreType.DMA((3,))],
    )
    return pl.pallas_call(
        functools.partial(_experts_kernel, n_blk),
        grid_spec=grid_spec,
        out_shape=jax.ShapeDtypeStruct((n_rows, HALF), U32),
        compiler_params=pltpu.CompilerParams(dimension_semantics=("arbitrary",)),
        name="experts",
    )(blk_e, next_e, n_used, xs, e_gate, e_up, e_down)


def _final_norm_kernel(x_ref, g_ref, *rest):
    out_ref = rest[-1]
    out_ref[...] = _rms(x_ref[...], g_ref[...])


def _final_norm(xsum, g_final, out_prev, tok0, t_out):
    off = tok0 // TM_NORM
    in_specs = [pl.BlockSpec((TM_NORM, D_MODEL), lambda i: (i, 0)),
                pl.BlockSpec((1, D_MODEL), lambda i: (0, 0))]
    args = [xsum, g_final]
    aliases = {}
    if out_prev is not None:
        in_specs.append(pl.BlockSpec(memory_space=pl.ANY))
        args.append(out_prev)
        aliases = {2: 0}
    return pl.pallas_call(
        _final_norm_kernel,
        grid=(xsum.shape[0] // TM_NORM,),
        in_specs=in_specs,
        out_specs=pl.BlockSpec((TM_NORM, D_MODEL), lambda i: (i + off, 0)),
        out_shape=jax.ShapeDtypeStruct((t_out, D_MODEL), F32),
        input_output_aliases=aliases,
        compiler_params=pltpu.CompilerParams(dimension_semantics=("arbitrary",)),
        name="final_norm",
    )(*args)


def kernel(x, g_mix, w_in, b_gate, w_pool_group, pool_scale, w_pool_out, conv_w, w_conv_out, w_o,
           g_ffn, w_router, router_bias, e_gate, e_up, e_down, s_gate, s_up, s_down, g_final):
    b, s, d = x.shape
    t_tok = b * s
    t_split = t_tok // N_SPLITS
    n_pad = N_EXPERTS * ROW_BLOCK
    n_rows = t_split * TOP_K + n_pad
    n_blk = n_rows // ROW_BLOCK
    assert d == D_MODEL and s % TM_MIX == 0 and TM_MIX >= POOL_HALO and b % N_SPLITS == 0
    assert t_split % TM_DEST == 0 and t_split % TM_NORM == 0 and n_pad % PLAN_LANES == 0
    assert t_split % (2 * SC_WORKERS * SC_CHUNK) == 0 and n_pad % (SC_WORKERS * SC_CHUNK) == 0
    assert t_split % (SC_WORKERS * SC_SUM_GROUP * SC_SUM_ACC_RING) == 0

    row = lambda a: a.reshape(1, -1)
    wr_t = w_router.T.astype(F32)
    wr_hi = wr_t.astype(BF16)
    wr = jnp.concatenate([wr_hi, (wr_t - wr_hi.astype(F32)).astype(BF16)], axis=0)
    x2d = x.reshape(t_tok, d)

    out = None
    for sp in range(N_SPLITS):
        tok0 = sp * t_split
        xres, h2p, idx_t, wsel_t, rank_t, counts = _mixer_router(
            x2d, tok0, t_split, s, row(g_mix), w_in, row(b_gate), w_pool_group,
            row(pool_scale), w_pool_out, conv_w, w_conv_out, w_o, row(g_ffn), wr,
            router_bias.astype(F32).reshape(N_EXPERTS, 1), s_gate, s_up, s_down)
        dest_t, blk_e, next_e, n_used, zero_rows = _plan(counts, idx_t, rank_t, n_blk, n_pad)
        xs = _dispatch(h2p, dest_t, zero_rows, n_rows)
        ys = _experts(blk_e, next_e, n_used, xs, e_gate, e_up, e_down)
        xsum = _regroup_sum(ys, dest_t, wsel_t, xres)
        out = _final_norm(xsum, row(g_final), out, tok0, t_tok)
    return out.reshape(b, s, d)
```

```python
import functools

import jax
import jax.numpy as jnp
from jax import lax
from jax.experimental import pallas as pl
from jax.experimental.pallas import tpu as pltpu
from jax.experimental.pallas import tpu_sc as plsc

D_MODEL = 1024
HALF = D_MODEL // 2
POOL_WIDTH = 512
N_POOL_GROUPS = 4
POOL_GROUP = 128
POOL_WINDOWS = (2, 4, 8, 16)
CONV_WIDTH = 512
N_EXPERTS = 64
TOP_K = 8
EXPERT_HIDDEN = 256
SHARED_HIDDEN = 256
ROUTED_SCALE = 2.5
EPS = 1e-6

POOL_HALO = 16
CONV_HALO = 8
TM_MIX = 512
W_IN_CHUNK = 128
TM_DEST = 2048
PLAN_LANES = 2048
ROW_BLOCK = 512
ROW_RING = 8
TM_NORM = 1024
VMEM_LIMIT = 56 * 1024 * 1024

SC_CORES = 2
SC_SUBCORES = 16
SC_WORKERS = SC_CORES * SC_SUBCORES
SC_LANES = 16
SC_CHUNK = 64
SC_SUM_GROUP = 8
SC_SUM_ROW_RING = 2
SC_SUM_ACC_RING = 4

BF16 = jnp.bfloat16
F32 = jnp.float32
U32 = jnp.uint32


def _rms(x, g):
    r = lax.rsqrt(jnp.mean(x * x, axis=-1, keepdims=True) + EPS)
    return (x * r) * g


def _dot(a, b):
    return jnp.dot(a, b, preferred_element_type=F32)


def _sigmoid(z):
    return 0.5 * jnp.tanh(0.5 * z) + 0.5


def _pack_pair(a, b):
    ra = lax.bitcast_convert_type(a.astype(BF16).astype(F32), U32)
    rb = lax.bitcast_convert_type(b.astype(BF16).astype(F32), U32)
    return ra | (rb >> 16)


def _unpack_pair(w):
    a = lax.bitcast_convert_type(w & jnp.uint32(0xFFFF0000), F32)
    b = lax.bitcast_convert_type(w << 16, F32)
    return a, b


def _load_weights_bf16(w_in_hbm, w_grp_hbm, w_po_hbm, w_co_hbm, w_o_hbm, s_gate_hbm, s_up_hbm,
                       s_down_hbm, w_in_ref, w_grp_ref, w_po_ref, w_co_ref, w_o_ref, s_gate_ref,
                       s_up_ref, s_down_ref, stg_in, stg_sq, stg_sh, stg_grp, wsem):
    copy = lambda src, dst, k: pltpu.make_async_copy(src, dst, wsem.at[k])
    rows = stg_in.shape[1]
    n_in = w_in_hbm.shape[0] // rows
    half = w_po_hbm.shape[0]
    c_in = [copy(w_in_hbm.at[pl.ds(j * rows, rows)], stg_in.at[j % 2], j % 2) for j in range(n_in)]
    c_o = copy(w_o_hbm, stg_sq, 2)
    c_sg = copy(s_gate_hbm, stg_sh.at[0], 3)
    c_su = copy(s_up_hbm, stg_sh.at[1], 4)
    c_grp = copy(w_grp_hbm, stg_grp, 5)
    c_po = copy(w_po_hbm, stg_sq.at[0:half], 2)
    c_co = copy(w_co_hbm, stg_sq.at[half:2 * half], 6)
    c_sd = copy(s_down_hbm, stg_sq.at[0:s_down_hbm.shape[0]], 2)
    for cp in (c_in[0], c_in[1], c_o, c_sg, c_su, c_grp):
        cp.start()
    for j in range(n_in):
        c_in[j].wait()
        w_in_ref[j * rows:(j + 1) * rows, :] = stg_in[j % 2].astype(BF16)
        if j + 2 < n_in:
            c_in[j + 2].start()
    c_o.wait()
    w_o_ref[...] = stg_sq[...].astype(BF16)
    c_po.start()
    c_co.start()
    c_sg.wait()
    s_gate_ref[...] = stg_sh[0].astype(BF16)
    c_su.wait()
    s_up_ref[...] = stg_sh[1].astype(BF16)
    c_grp.wait()
    w_grp_ref[...] = stg_grp[...].astype(BF16)
    c_po.wait()
    w_po_ref[...] = stg_sq[0:half, :].astype(BF16)
    c_co.wait()
    w_co_ref[...] = stg_sq[half:2 * half, :].astype(BF16)
    c_sd.start()
    c_sd.wait()
    s_down_ref[...] = stg_sq[0:s_down_hbm.shape[0], :].astype(BF16)


def _mixer_router_kernel(n_seq_tiles,
                         x_ref, g_mix_ref, w_in_hbm, b_gate_ref, w_grp_hbm, pool_scale_ref,
                         w_po_hbm, conv_w_ref, w_co_hbm, w_o_hbm, g_ffn_ref,
                         wr_ref, rbias_ref, s_gate_hbm, s_up_hbm, s_down_hbm,
                         xres_ref, h2p_ref, idx_ref, wsel_ref, rank_ref, counts_ref,
                         ext_pool, ext_conv, cnt_carry, tri,
                         w_in_ref, w_grp_ref, w_po_ref, w_co_ref, w_o_ref, s_gate_ref, s_up_ref,
                         s_down_ref, stg_in, stg_sq, stg_sh, stg_grp, wsem):
    tm = x_ref.shape[0]
    i = pl.program_id(0)
    st = i % n_seq_tiles

    @pl.when(i == 0)
    def _():
        _load_weights_bf16(w_in_hbm, w_grp_hbm, w_po_hbm, w_co_hbm, w_o_hbm, s_gate_hbm, s_up_hbm,
                           s_down_hbm, w_in_ref, w_grp_ref, w_po_ref, w_co_ref, w_o_ref,
                           s_gate_ref, s_up_ref, s_down_ref, stg_in, stg_sq, stg_sh, stg_grp, wsem)
        r = lax.broadcasted_iota(jnp.int32, (tm, tm), 0)
        c = lax.broadcasted_iota(jnp.int32, (tm, tm), 1)
        tri[...] = (r < c).astype(BF16)
        cnt_carry[...] = jnp.zeros_like(cnt_carry)

    @pl.when(st == 0)
    def _():
        ext_pool[0:POOL_HALO, :] = jnp.zeros((POOL_HALO, POOL_WIDTH), F32)
        ext_conv[0:CONV_HALO, :] = jnp.zeros((CONV_HALO, CONV_WIDTH), F32)

    x = x_ref[...]
    hb = _rms(x, g_mix_ref[...]).astype(BF16)

    o0 = POOL_WIDTH
    o1 = o0 + CONV_WIDTH
    o2 = o1 + CONV_WIDTH
    o3 = o2 + CONV_WIDTH

    u = _dot(hb, w_in_ref[:, 0:o0])
    ext_pool[POOL_HALO:POOL_HALO + tm, :] = u
    gc = _dot(hb, w_in_ref[:, o1:o2])
    v = _dot(hb, w_in_ref[:, o2:o3])
    pre_a = _dot(hb, w_in_ref[:, o3:o3 + D_MODEL])

    t_glob = st * tm + lax.broadcasted_iota(jnp.int32, (tm, 1), 0)
    mixed = []
    for gi, w in enumerate(POOL_WINDOWS):
        cols = slice(gi * POOL_GROUP, (gi + 1) * POOL_GROUP)
        ug = u[:, cols]
        acc = ug
        for j in range(1, w):
            acc = acc + ext_pool[POOL_HALO - j:POOL_HALO - j + tm, cols]
        cnt = jnp.minimum(t_glob + 1, w).astype(F32)
        pooled = acc * (1.0 / cnt) - ug
        mixed.append(_dot(pooled.astype(BF16), w_grp_ref[gi]))
    ext_pool[0:POOL_HALO, :] = ext_pool[tm:tm + POOL_HALO, :]
    pre_b = _dot(hb, w_in_ref[:, o3 + D_MODEL:o3 + 2 * D_MODEL])
    gb = _dot(hb, w_in_ref[:, o0:o1])

    cv = gc * v
    ext_conv[CONV_HALO:CONV_HALO + tm, :] = cv
    conv = (ext_conv[CONV_HALO - 2:CONV_HALO - 2 + tm, :] * conv_w_ref[0:1, :]
            + ext_conv[CONV_HALO - 1:CONV_HALO - 1 + tm, :] * conv_w_ref[1:2, :]
            + cv * conv_w_ref[2:3, :])
    ext_conv[0:CONV_HALO, :] = ext_conv[tm:tm + CONV_HALO, :]
    branch_b = _dot((gb * conv).astype(BF16), w_co_ref[...])
    mixed = jnp.concatenate(mixed, axis=1) * pool_scale_ref[...]
    branch_a = _dot(mixed.astype(BF16), w_po_ref[...])

    merged = (_sigmoid(pre_a + b_gate_ref[:, 0:D_MODEL]) * branch_a
              + _sigmoid(pre_b + b_gate_ref[:, D_MODEL:2 * D_MODEL]) * branch_b)
    x1 = x + _dot(merged.astype(BF16), w_o_ref[...])

    h2 = _rms(x1, g_ffn_ref[...])
    h2p_ref[...] = _pack_pair(h2[:, 0:HALF], h2[:, HALF:D_MODEL])
    h2b = h2.astype(BF16)

    nt = (((1,), (1,)), ((), ()))
    parts = lax.dot_general(wr_ref[...], h2b, nt, preferred_element_type=F32)
    logits = parts[0:N_EXPERTS, :] + parts[N_EXPERTS:2 * N_EXPERTS, :]
    sg = _dot(h2b, s_gate_ref[...])
    su = _dot(h2b, s_up_ref[...])
    scores = jax.nn.sigmoid(logits)
    sel = scores + rbias_ref[...]
    eidx = lax.broadcasted_iota(jnp.int32, (N_EXPERTS, tm), 0).astype(F32)
    e_rows, w_rows = [], []
    mask = jnp.zeros((N_EXPERTS, tm), F32)
    for _ in range(TOP_K):
        m = jnp.max(sel, axis=0, keepdims=True)
        ek = jnp.min(jnp.where(sel == m, eidx, float(N_EXPERTS)), axis=0, keepdims=True)
        oh = eidx == ek
        w_rows.append(jnp.sum(jnp.where(oh, scores, 0.0), axis=0, keepdims=True))
        e_rows.append(ek)
        mask = mask + oh.astype(F32)
        sel = jnp.where(oh, -jnp.inf, sel)

    shared = _dot((sg * _sigmoid(sg) * su).astype(BF16), s_down_ref[...])
    xres_ref[...] = x1 + shared

    wsum = w_rows[0]
    for k in range(1, TOP_K):
        wsum = wsum + w_rows[k]

    before = _dot(mask.astype(BF16), tri[...]) + cnt_carry[...]
    for k in range(TOP_K):
        oh = eidx == e_rows[k]
        rank_ref[k:k + 1, :] = jnp.sum(jnp.where(oh, before, 0.0), axis=0,
                                       keepdims=True).astype(jnp.int32)
        idx_ref[k:k + 1, :] = e_rows[k].astype(jnp.int32)
        wsel_ref[k:k + 1, :] = w_rows[k] / wsum * ROUTED_SCALE
    total = cnt_carry[...] + jnp.sum(mask, axis=1, keepdims=True)
    cnt_carry[...] = total
    counts_ref[...] = total.astype(jnp.int32)


def _mixer_router(x2d, tok0, t_tok, seq_len, g_mix, w_in, b_gate, w_grp, pool_scale, w_po, conv_w,
                  w_co, w_o, g_ffn, wr, rbias, s_gate, s_up, s_down):
    tm = TM_MIX
    n_seq_tiles = seq_len // tm
    off = tok0 // tm
    const = lambda shape: pl.BlockSpec(shape, lambda i: (0,) * len(shape),
                                       pipeline_mode=pl.Buffered(1))
    hbm = pl.BlockSpec(memory_space=pl.ANY)
    x_blk = pl.BlockSpec((tm, D_MODEL), lambda i: (i + off, 0))
    row_blk = pl.BlockSpec((tm, D_MODEL), lambda i: (i, 0))
    half_blk = pl.BlockSpec((tm, HALF), lambda i: (i, 0))
    slot_blk = pl.BlockSpec((TOP_K, tm), lambda i: (0, i))
    return pl.pallas_call(
        functools.partial(_mixer_router_kernel, n_seq_tiles),
        grid=(t_tok // tm,),
        in_specs=[x_blk, const(g_mix.shape), hbm, const(b_gate.shape),
                  hbm, const(pool_scale.shape), hbm,
                  const(conv_w.shape), hbm, hbm, const(g_ffn.shape),
                  const(wr.shape), const(rbias.shape), hbm, hbm, hbm],
        out_specs=[row_blk, half_blk, slot_blk, slot_blk, slot_blk,
                   pl.BlockSpec((N_EXPERTS, 1), lambda i: (0, 0))],
        out_shape=[jax.ShapeDtypeStruct((t_tok, D_MODEL), F32),
                   jax.ShapeDtypeStruct((t_tok, HALF), U32),
                   jax.ShapeDtypeStruct((TOP_K, t_tok), jnp.int32),
                   jax.ShapeDtypeStruct((TOP_K, t_tok), F32),
                   jax.ShapeDtypeStruct((TOP_K, t_tok), jnp.int32),
                   jax.ShapeDtypeStruct((N_EXPERTS, 1), jnp.int32)],
        scratch_shapes=[pltpu.VMEM((POOL_HALO + tm, POOL_WIDTH), F32),
                        pltpu.VMEM((CONV_HALO + tm, CONV_WIDTH), F32),
                        pltpu.VMEM((N_EXPERTS, 1), F32),
                        pltpu.VMEM((tm, tm), BF16)]
                       + [pltpu.VMEM(w.shape, BF16)
                          for w in (w_in, w_grp, w_po, w_co, w_o, s_gate, s_up, s_down)]
                       + [pltpu.VMEM((2, W_IN_CHUNK, w_in.shape[1]), F32),
                          pltpu.VMEM(w_o.shape, F32),
                          pltpu.VMEM((2,) + s_gate.shape, F32),
                          pltpu.VMEM(w_grp.shape, F32),
                          pltpu.SemaphoreType.DMA((7,))],
        compiler_params=pltpu.CompilerParams(dimension_semantics=("arbitrary",),
                                             vmem_limit_bytes=VMEM_LIMIT),
        name="mixer_router",
    )(x2d, g_mix, w_in, b_gate, w_grp, pool_scale, w_po, conv_w, w_co, w_o, g_ffn,
      wr, rbias, s_gate, s_up, s_down)


def _plan_kernel(n_blk, n_pad, counts_ref, idx_ref, rank_ref,
                 dest_ref, blk_e_ref, next_e_ref, n_used_ref, zero_rows_ref, pad_start):
    e_n = N_EXPERTS

    @pl.when(pl.program_id(0) == 0)
    def _():
        sub = lax.broadcasted_iota(jnp.int32, (e_n, e_n), 0)
        lane = lax.broadcasted_iota(jnp.int32, (e_n, e_n), 1)
        c_col = counts_ref[...]
        p_col = ((c_col + (ROW_BLOCK - 1)) // ROW_BLOCK) * ROW_BLOCK
        c_f = c_col.astype(F32)
        p_f = p_col.astype(F32)
        gap_f = p_f - c_f
        to_row = lambda col: jnp.sum(jnp.where(sub == lane, col, 0.0), axis=0, keepdims=True)
        p_row = to_row(p_f)
        gap_row = to_row(gap_f)
        pad_end_col = jnp.sum(jnp.where(lane <= sub, p_row, 0.0), axis=1, keepdims=True)
        gap_before_col = jnp.sum(jnp.where(lane < sub, gap_row, 0.0), axis=1, keepdims=True)
        pad_start[...] = pad_end_col - p_f
        pad_end_last = jnp.sum(p_row, axis=1, keepdims=True)
        n_used_ref[...] = jnp.broadcast_to(pad_end_last * (1.0 / ROW_BLOCK),
                                           n_used_ref.shape).astype(jnp.int32)
        row0 = (lax.broadcasted_iota(jnp.int32, (e_n, blk_e_ref.shape[1]), 1)
                * ROW_BLOCK).astype(F32)
        owner = jnp.sum(jnp.where(pad_end_col <= row0, 1.0, 0.0), axis=0, keepdims=True)
        blk_e_ref[...] = jnp.minimum(owner, float(e_n - 1)).astype(jnp.int32)
        used_later = jnp.logical_and(sub > lane, p_f > 0.0)
        nxt = jnp.min(jnp.where(used_later, sub, e_n), axis=0, keepdims=True)
        next_e_ref[...] = jnp.full(next_e_ref.shape, -1, jnp.int32)
        next_e_ref[:, 0:e_n] = jnp.where(nxt < e_n, nxt, -1)
        for j0 in range(0, n_pad, PLAN_LANES):
            j = (j0 + lax.broadcasted_iota(jnp.int32, (e_n, PLAN_LANES), 1)).astype(F32)
            before = jnp.sum(jnp.where(gap_before_col <= j, c_f, 0.0), axis=0, keepdims=True)
            zero_rows_ref[:, j0:j0 + PLAN_LANES] = (j[0:1, :] + before).astype(jnp.int32)

    tm = idx_ref.shape[1]
    eidx = lax.broadcasted_iota(jnp.int32, (e_n, tm), 0)
    ps = pad_start[...]
    for k in range(TOP_K):
        oh = eidx == idx_ref[k:k + 1, :]
        start = jnp.sum(jnp.where(oh, ps, 0.0), axis=0, keepdims=True)
        dest_ref[k:k + 1, :] = start.astype(jnp.int32) + rank_ref[k:k + 1, :]


def _plan(counts, idx_t, rank_t, n_blk, n_pad):
    t_tok = idx_t.shape[1]
    lanes = 128
    n_blk_p = -(-n_blk // lanes) * lanes
    slot_blk = pl.BlockSpec((TOP_K, TM_DEST), lambda i: (0, i))
    whole = lambda n: pl.BlockSpec((1, n), lambda i: (0, 0))
    dest_t, blk_e, next_e, n_used, zero_rows = pl.pallas_call(
        functools.partial(_plan_kernel, n_blk, n_pad),
        grid=(t_tok // TM_DEST,),
        in_specs=[pl.BlockSpec((N_EXPERTS, 1), lambda i: (0, 0)), slot_blk, slot_blk],
        out_specs=[slot_blk, whole(n_blk_p), whole(lanes), whole(lanes), whole(n_pad)],
        out_shape=[jax.ShapeDtypeStruct((TOP_K, t_tok), jnp.int32),
                   jax.ShapeDtypeStruct((1, n_blk_p), jnp.int32),
                   jax.ShapeDtypeStruct((1, lanes), jnp.int32),
                   jax.ShapeDtypeStruct((1, lanes), jnp.int32),
                   jax.ShapeDtypeStruct((1, n_pad), jnp.int32)],
        scratch_shapes=[pltpu.VMEM((N_EXPERTS, 1), F32)],
        compiler_params=pltpu.CompilerParams(dimension_semantics=("arbitrary",)),
        name="plan",
    )(counts, idx_t, rank_t)
    return (dest_t, blk_e.reshape(-1), next_e.reshape(-1), n_used.reshape(-1),
            zero_rows.reshape(-1))


def _sc_mesh():
    return plsc.VectorSubcoreMesh(core_axis_name="c", subcore_axis_name="s")


def _sc_worker_id():
    return lax.axis_index("s") * SC_CORES + lax.axis_index("c")


def _dispatch(h2p, dest_t, zero_rows, n_rows):
    t_tok, width = h2p.shape
    per_w = t_tok // SC_WORKERS
    n_chunks = per_w // SC_CHUNK
    z_chunks = zero_rows.shape[0] // (SC_WORKERS * SC_CHUNK)
    dest_w = (dest_t.reshape(TOP_K, SC_WORKERS, n_chunks, SC_CHUNK)
              .transpose(1, 2, 0, 3).reshape(SC_WORKERS * n_chunks * TOP_K, SC_CHUNK))
    zero_w = zero_rows.reshape(SC_WORKERS * z_chunks, SC_CHUNK)

    @functools.partial(
        pl.kernel, mesh=_sc_mesh(),
        out_type=jax.ShapeDtypeStruct((n_rows, width), h2p.dtype),
        scratch_types=[pltpu.VMEM((n_chunks * TOP_K, SC_CHUNK), jnp.int32),
                       pltpu.VMEM((z_chunks, SC_CHUNK), jnp.int32),
                       pltpu.VMEM((2, SC_CHUNK, width), h2p.dtype),
                       pltpu.VMEM((SC_CHUNK, width), h2p.dtype),
                       pltpu.SemaphoreType.DMA, pltpu.SemaphoreType.DMA,
                       pltpu.SemaphoreType.DMA],
        name="dispatch",
    )
    def k(h2p_hbm, dest_hbm, zidx_hbm, zsrc_hbm, xs_hbm, idx_v, zidx_v, rows_v, zero_v,
          gsem, wsem, zsem):
        wid = _sc_worker_id()
        base = wid * per_w
        pltpu.sync_copy(dest_hbm.at[pl.ds(wid * n_chunks * TOP_K, n_chunks * TOP_K)], idx_v)
        pltpu.sync_copy(zidx_hbm.at[pl.ds(wid * z_chunks, z_chunks)], zidx_v)
        pltpu.sync_copy(zsrc_hbm, zero_v)

        def zput(j):
            return pltpu.make_async_copy(zero_v, xs_hbm.at[zidx_v.at[j]], zsem)

        for j in range(z_chunks):
            zput(j).start()

        def get(j, slot):
            return pltpu.make_async_copy(h2p_hbm.at[pl.ds(base + j * SC_CHUNK, SC_CHUNK)],
                                         rows_v.at[slot], gsem)

        def put(j, slot, kk):
            return pltpu.make_async_copy(rows_v.at[slot], xs_hbm.at[idx_v.at[j * TOP_K + kk]], wsem)

        get(0, 0).start()

        @pl.loop(0, n_chunks, step=2)
        def _(j):
            for b in range(2):
                jj = j + b
                get(jj, b).wait()

                @pl.when(jj >= 1)
                def _():
                    for kk in range(TOP_K):
                        put(jj - 1, 1 - b, kk).wait()

                @pl.when(jj + 1 < n_chunks)
                def _():
                    get(jj + 1, 1 - b).start()
                for kk in range(TOP_K):
                    put(jj, b, kk).start()

        for kk in range(TOP_K):
            put(n_chunks - 1, (n_chunks - 1) % 2, kk).wait()
        for j in range(z_chunks):
            zput(j).wait()

    return k(h2p, dest_w, zero_w, jnp.zeros((SC_CHUNK, width), h2p.dtype))


def _regroup_sum(ys, dest_t, wsel_t, xres):
    n_slots, t_tok = dest_t.shape
    width = ys.shape[1]
    d_out = xres.shape[1]
    grp = SC_SUM_GROUP
    r_rows = SC_SUM_ROW_RING
    r_acc = SC_SUM_ACC_RING
    per_w = t_tok // SC_WORKERS
    n_sub = per_w // grp

    @functools.partial(
        pl.kernel, mesh=_sc_mesh(),
        out_type=jax.ShapeDtypeStruct((t_tok, d_out), F32),
        scratch_types=[pltpu.VMEM((n_slots, per_w), jnp.int32),
                       pltpu.VMEM((n_slots, per_w), F32),
                       pltpu.VMEM((r_rows, n_slots * grp, width), ys.dtype),
                       pltpu.VMEM((r_acc, grp, d_out), F32),
                       pltpu.SemaphoreType.DMA((r_rows,)), pltpu.SemaphoreType.DMA((r_acc,)),
                       pltpu.SemaphoreType.DMA((r_acc,))],
        compiler_params=pltpu.CompilerParams(needs_layout_passes=False),
        name="regroup_sum",
    )
    def k(ys_hbm, idx_hbm, w_hbm, xres_hbm, out_hbm, idx_v, w_v, rows_v, acc_v, gsem, xsem, psem):
        wid = _sc_worker_id()
        base = wid * per_w
        pltpu.sync_copy(idx_hbm.at[:, pl.ds(base, per_w)], idx_v)
        pltpu.sync_copy(w_hbm.at[:, pl.ds(base, per_w)], w_v)

        def gets(j, slot):
            return [pltpu.make_async_copy(ys_hbm.at[idx_v.at[kk, pl.ds(j * grp, grp)]],
                                          rows_v.at[slot, pl.ds(kk * grp, grp)], gsem.at[slot])
                    for kk in range(n_slots)]

        def xload(j, slot):
            return pltpu.make_async_copy(xres_hbm.at[pl.ds(base + j * grp, grp)], acc_v.at[slot],
                                         xsem.at[slot])

        def put(j, slot):
            return pltpu.make_async_copy(acc_v.at[slot], out_hbm.at[pl.ds(base + j * grp, grp)],
                                         psem.at[slot])

        def accumulate(j, rslot, aslot):
            @pl.loop(0, grp)
            def _(g):
                pos = jnp.full((SC_LANES,), j * grp + g, jnp.int32)
                wk = [plsc.load_gather(w_v, [jnp.full((SC_LANES,), kk, jnp.int32), pos])
                      for kk in range(n_slots)]

                @plsc.parallel_loop(0, width // SC_LANES, unroll=4)
                def _(v):
                    cols_a = pl.ds(v * SC_LANES, SC_LANES)
                    cols_b = pl.ds(width + v * SC_LANES, SC_LANES)
                    acc_a = acc_v[aslot, g, cols_a]
                    acc_b = acc_v[aslot, g, cols_b]
                    for kk in range(n_slots):
                        ya, yb = _unpack_pair(rows_v[rslot, kk * grp + g, cols_a])
                        acc_a = acc_a + ya * wk[kk]
                        acc_b = acc_b + yb * wk[kk]
                    acc_v[aslot, g, cols_a] = acc_a
                    acc_v[aslot, g, cols_b] = acc_b

        for cp in gets(0, 0):
            cp.start()
        xload(0, 0).start()

        @pl.loop(0, n_sub, step=r_acc)
        def _(j):
            for b in range(r_acc):
                jj = j + b
                rs = b % r_rows
                na = (b + 1) % r_acc
                for cp in gets(jj, rs):
                    cp.wait()

                @pl.when(jj + 1 < n_sub)
                def _():
                    for cp in gets(jj + 1, (b + 1) % r_rows):
                        cp.start()
                xload(jj, b).wait()

                @pl.when(jj + 1 < n_sub)
                def _():
                    @pl.when(jj + 1 >= r_acc)
                    def _():
                        put(jj + 1 - r_acc, na).wait()
                    xload(jj + 1, na).start()
                accumulate(jj, rs, b)
                put(jj, b).start()

        for p in range(r_acc):
            put(n_sub - r_acc + p, p).wait()

    return k(ys, dest_t, wsel_t, xres)


def _experts_kernel(n_blk, blk_e_ref, next_e_ref, n_used_ref,
                    xs_hbm, eg_hbm, eu_hbm, ed_hbm, ys_hbm,
                    xbuf, ybuf, hid, stg_g, stg_u, stg_d, wg, wu, wd, xsem, ysem, wsem):
    n = n_used_ref[0]

    def ring(b):
        return jnp.bitwise_and(b, ROW_RING - 1)

    def x_copy(b):
        return pltpu.make_async_copy(xs_hbm.at[pl.ds(pl.multiple_of(b * ROW_BLOCK, ROW_BLOCK),
                                                     ROW_BLOCK)], xbuf.at[ring(b)], xsem.at[ring(b)])

    def y_copy(b):
        return pltpu.make_async_copy(ybuf.at[ring(b)],
                                     ys_hbm.at[pl.ds(pl.multiple_of(b * ROW_BLOCK, ROW_BLOCK),
                                                     ROW_BLOCK)], ysem.at[ring(b)])

    def w_copies(e):
        return (pltpu.make_async_copy(eg_hbm.at[e], stg_g, wsem.at[0]),
                pltpu.make_async_copy(eu_hbm.at[e], stg_u, wsem.at[1]),
                pltpu.make_async_copy(ed_hbm.at[e], stg_d, wsem.at[2]))

    def switch_expert(e, wslot):
        for cp in w_copies(e):
            cp.wait()
        wg[wslot] = stg_g[...].astype(BF16)
        wu[wslot] = stg_u[...].astype(BF16)
        wd[wslot] = stg_d[...].astype(BF16)
        nxt = next_e_ref[e]

        @pl.when(nxt >= 0)
        def _():
            for cp in w_copies(nxt):
                cp.start()

    def gate_up(b, wslot):
        xa, xb = _unpack_pair(xbuf[ring(b)])
        xb16 = jnp.concatenate([xa, xb], axis=1).astype(BF16)
        g = _dot(xb16, wg[wslot])
        up = _dot(xb16, wu[wslot])
        hid[jnp.bitwise_and(b, 1)] = (g * _sigmoid(g) * up).astype(BF16)

    def down(b, wslot):
        y = _dot(hid[jnp.bitwise_and(b, 1)], wd[wslot])
        ybuf[ring(b)] = _pack_pair(y[:, 0:HALF], y[:, HALF:D_MODEL])

    e0 = blk_e_ref[0]
    for cp in w_copies(e0):
        cp.start()
    for j in range(ROW_RING):
        @pl.when(j < n)
        def _():
            x_copy(j).start()
    switch_expert(e0, 0)
    x_copy(0).wait()
    gate_up(0, 0)

    def body(b, wslot_prev):
        e = blk_e_ref[b]
        first = e != blk_e_ref[b - 1]
        wslot = jnp.where(first, 1 - wslot_prev, wslot_prev)

        @pl.when(first)
        def _():
            switch_expert(e, wslot)

        x_copy(b).wait()

        @pl.when(b + ROW_RING - 1 < n)
        def _():
            x_copy(b + ROW_RING - 1).start()

        @pl.when(b >= ROW_RING + 1)
        def _():
            y_copy(b - 1 - ROW_RING).wait()

        down(b - 1, wslot_prev)
        gate_up(b, wslot)
        y_copy(b - 1).start()
        return wslot

    wslot_last = lax.fori_loop(1, n, body, jnp.int32(0))

    last = n - 1

    @pl.when(last >= ROW_RING)
    def _():
        y_copy(last - ROW_RING).wait()
    down(last, wslot_last)
    y_copy(last).start()
    for j in range(ROW_RING - 1, -1, -1):
        @pl.when(last - j >= 0)
        def _():
            y_copy(last - j).wait()

    ybuf[0] = jnp.zeros((ROW_BLOCK, HALF), U32)

    def zero_tail(b, c):
        cp = pltpu.make_async_copy(ybuf.at[0],
                                   ys_hbm.at[pl.ds(pl.multiple_of(b * ROW_BLOCK, ROW_BLOCK),
                                                   ROW_BLOCK)], ysem.at[0])
        cp.start()
        cp.wait()
        return c
    lax.fori_loop(n, n_blk, zero_tail, 0)


def _experts(blk_e, next_e, n_used, xs, e_gate, e_up, e_down):
    n_rows = xs.shape[0]
    n_blk = n_rows // ROW_BLOCK
    any_spec = pl.BlockSpec(memory_space=pl.ANY)
    grid_spec = pltpu.PrefetchScalarGridSpec(
        num_scalar_prefetch=3,
        grid=(1,),
        in_specs=[any_spec, any_spec, any_spec, any_spec],
        out_specs=any_spec,
        scratch_shapes=[pltpu.VMEM((ROW_RING, ROW_BLOCK, HALF), U32),
                        pltpu.VMEM((ROW_RING, ROW_BLOCK, HALF), U32),
                        pltpu.VMEM((2, ROW_BLOCK, EXPERT_HIDDEN), BF16),
                        pltpu.VMEM((D_MODEL, EXPERT_HIDDEN), F32),
                        pltpu.VMEM((D_MODEL, EXPERT_HIDDEN), F32),
                        pltpu.VMEM((EXPERT_HIDDEN, D_MODEL), F32),
                        pltpu.VMEM((2, D_MODEL, EXPERT_HIDDEN), BF16),
                        pltpu.VMEM((2, D_MODEL, EXPERT_HIDDEN), BF16),
                        pltpu.VMEM((2, EXPERT_HIDDEN, D_MODEL), BF16),
                        pltpu.SemaphoreType.DMA((ROW_RING,)),
                        pltpu.SemaphoreType.DMA((ROW_RING,)),
                        pltpu.SemaphoreType.DMA((3,))],
    )
    return pl.pallas_call(
        functools.partial(_experts_kernel, n_blk),
        grid_spec=grid_spec,
        out_shape=jax.ShapeDtypeStruct((n_rows, HALF), U32),
        compiler_params=pltpu.CompilerParams(dimension_semantics=("arbitrary",)),
        name="experts",
    )(blk_e, next_e, n_used, xs, e_gate, e_up, e_down)


def _final_norm_kernel(x_ref, g_ref, out_ref):
    out_ref[...] = _rms(x_ref[...], g_ref[...])


def _final_norm(xsum, g_final):
    t_tok = xsum.shape[0]
    blk = pl.BlockSpec((TM_NORM, D_MODEL), lambda i: (i, 0))
    return pl.pallas_call(
        _final_norm_kernel,
        grid=(t_tok // TM_NORM,),
        in_specs=[blk, pl.BlockSpec((1, D_MODEL), lambda i: (0, 0))],
        out_specs=blk,
        out_shape=jax.ShapeDtypeStruct((t_tok, D_MODEL), F32),
        compiler_params=pltpu.CompilerParams(dimension_semantics=("arbitrary",)),
        name="final_norm",
    )(xsum, g_final)


def kernel(x, g_mix, w_in, b_gate, w_pool_group, pool_scale, w_pool_out, conv_w, w_conv_out, w_o,
           g_ffn, w_router, router_bias, e_gate, e_up, e_down, s_gate, s_up, s_down, g_final):
    b, s, d = x.shape
    t_tok = b * s
    n_pad = N_EXPERTS * ROW_BLOCK
    n_rows = t_tok * TOP_K + n_pad
    n_blk = n_rows // ROW_BLOCK
    assert d == D_MODEL and s % TM_MIX == 0 and TM_MIX >= POOL_HALO
    assert t_tok % TM_DEST == 0 and t_tok % TM_NORM == 0 and n_pad % PLAN_LANES == 0
    assert t_tok % (2 * SC_WORKERS * SC_CHUNK) == 0 and n_pad % (SC_WORKERS * SC_CHUNK) == 0
    assert t_tok % (SC_WORKERS * SC_SUM_GROUP * SC_SUM_ACC_RING) == 0

    row = lambda a: a.reshape(1, -1)
    wr_t = w_router.T.astype(F32)
    wr_hi = wr_t.astype(BF16)
    wr = jnp.concatenate([wr_hi, (wr_t - wr_hi.astype(F32)).astype(BF16)], axis=0)

    xres, h2p, idx_t, wsel_t, rank_t, counts = _mixer_router(
        x.reshape(t_tok, d), 0, t_tok, s, row(g_mix), w_in, row(b_gate), w_pool_group,
        row(pool_scale), w_pool_out, conv_w, w_conv_out, w_o, row(g_ffn), wr,
        router_bias.astype(F32).reshape(N_EXPERTS, 1), s_gate, s_up, s_down)
    dest_t, blk_e, next_e, n_used, zero_rows = _plan(counts, idx_t, rank_t, n_blk, n_pad)
    xs = _dispatch(h2p, dest_t, zero_rows, n_rows)
    ys = _experts(blk_e, next_e, n_used, xs, e_gate, e_up, e_down)
    xsum = _regroup_sum(ys, dest_t, wsel_t, xres)
    return _final_norm(xsum, row(g_final)).reshape(b, s, d)
```

```python
import functools

import jax
import jax.numpy as jnp
from jax import lax
from jax.experimental import pallas as pl
from jax.experimental.pallas import tpu as pltpu
from jax.experimental.pallas import tpu_sc as plsc

D_MODEL = 1024
HALF = D_MODEL // 2
POOL_WIDTH = 512
N_POOL_GROUPS = 4
POOL_GROUP = 128
POOL_WINDOWS = (2, 4, 8, 16)
CONV_WIDTH = 512
N_EXPERTS = 64
TOP_K = 8
EXPERT_HIDDEN = 256
SHARED_HIDDEN = 256
ROUTED_SCALE = 2.5
EPS = 1e-6

POOL_HALO = 16
CONV_HALO = 8
TM_MIX = 512
W_IN_CHUNK = 128
TM_DEST = 2048
PLAN_LANES = 2048
ROW_BLOCK = 512
ROW_RING = 8
TM_NORM = 1024
VMEM_LIMIT = 56 * 1024 * 1024

SC_CORES = 2
SC_SUBCORES = 16
SC_WORKERS = SC_CORES * SC_SUBCORES
SC_LANES = 16
SC_CHUNK = 64
SC_SUM_GROUP = 8
SC_SUM_ROW_RING = 2
SC_SUM_ACC_RING = 4

BF16 = jnp.bfloat16
F32 = jnp.float32
U32 = jnp.uint32


def _rms(x, g):
    r = lax.rsqrt(jnp.mean(x * x, axis=-1, keepdims=True) + EPS)
    return (x * r) * g


def _dot(a, b):
    return jnp.dot(a, b, preferred_element_type=F32)


def _sigmoid(z):
    return 0.5 * jnp.tanh(0.5 * z) + 0.5


def _pack_pair(a, b):
    ra = lax.bitcast_convert_type(a.astype(BF16).astype(F32), U32)
    rb = lax.bitcast_convert_type(b.astype(BF16).astype(F32), U32)
    return ra | (rb >> 16)


def _unpack_pair(w):
    a = lax.bitcast_convert_type(w & jnp.uint32(0xFFFF0000), F32)
    b = lax.bitcast_convert_type(w << 16, F32)
    return a, b


def _load_weights_bf16(w_in_hbm, w_grp_hbm, w_po_hbm, w_co_hbm, w_o_hbm, s_gate_hbm, s_up_hbm,
                       s_down_hbm, w_in_ref, w_grp_ref, w_po_ref, w_co_ref, w_o_ref, s_gate_ref,
                       s_up_ref, s_down_ref, stg_in, stg_sq, stg_sh, stg_grp, wsem):
    copy = lambda src, dst, k: pltpu.make_async_copy(src, dst, wsem.at[k])
    rows = stg_in.shape[1]
    n_in = w_in_hbm.shape[0] // rows
    half = w_po_hbm.shape[0]
    c_in = [copy(w_in_hbm.at[pl.ds(j * rows, rows)], stg_in.at[j % 2], j % 2) for j in range(n_in)]
    c_o = copy(w_o_hbm, stg_sq, 2)
    c_sg = copy(s_gate_hbm, stg_sh.at[0], 3)
    c_su = copy(s_up_hbm, stg_sh.at[1], 4)
    c_grp = copy(w_grp_hbm, stg_grp, 5)
    c_po = copy(w_po_hbm, stg_sq.at[0:half], 2)
    c_co = copy(w_co_hbm, stg_sq.at[half:2 * half], 6)
    c_sd = copy(s_down_hbm, stg_sq.at[0:s_down_hbm.shape[0]], 2)
    for cp in (c_in[0], c_in[1], c_o, c_sg, c_su, c_grp):
        cp.start()
    for j in range(n_in):
        c_in[j].wait()
        w_in_ref[j * rows:(j + 1) * rows, :] = stg_in[j % 2].astype(BF16)
        if j + 2 < n_in:
            c_in[j + 2].start()
    c_o.wait()
    w_o_ref[...] = stg_sq[...].astype(BF16)
    c_po.start()
    c_co.start()
    c_sg.wait()
    s_gate_ref[...] = stg_sh[0].astype(BF16)
    c_su.wait()
    s_up_ref[...] = stg_sh[1].astype(BF16)
    c_grp.wait()
    w_grp_ref[...] = stg_grp[...].astype(BF16)
    c_po.wait()
    w_po_ref[...] = stg_sq[0:half, :].astype(BF16)
    c_co.wait()
    w_co_ref[...] = stg_sq[half:2 * half, :].astype(BF16)
    c_sd.start()
    c_sd.wait()
    s_down_ref[...] = stg_sq[0:s_down_hbm.shape[0], :].astype(BF16)


def _mixer_router_kernel(n_seq_tiles,
                         x_ref, g_mix_ref, w_in_hbm, b_gate_ref, w_grp_hbm, pool_scale_ref,
                         w_po_hbm, conv_w_ref, w_co_hbm, w_o_hbm, g_ffn_ref,
                         wr_ref, rbias_ref, s_gate_hbm, s_up_hbm, s_down_hbm,
                         xres_ref, h2p_ref, idx_ref, wsel_ref, rank_ref, counts_ref,
                         ext_pool, ext_conv, cnt_carry, tri,
                         w_in_ref, w_grp_ref, w_po_ref, w_co_ref, w_o_ref, s_gate_ref, s_up_ref,
                         s_down_ref, stg_in, stg_sq, stg_sh, stg_grp, wsem):
    tm = x_ref.shape[0]
    i = pl.program_id(0)
    st = i % n_seq_tiles

    @pl.when(i == 0)
    def _():
        _load_weights_bf16(w_in_hbm, w_grp_hbm, w_po_hbm, w_co_hbm, w_o_hbm, s_gate_hbm, s_up_hbm,
                           s_down_hbm, w_in_ref, w_grp_ref, w_po_ref, w_co_ref, w_o_ref,
                           s_gate_ref, s_up_ref, s_down_ref, stg_in, stg_sq, stg_sh, stg_grp, wsem)
        r = lax.broadcasted_iota(jnp.int32, (tm, tm), 0)
        c = lax.broadcasted_iota(jnp.int32, (tm, tm), 1)
        tri[...] = (r < c).astype(BF16)
        cnt_carry[...] = jnp.zeros_like(cnt_carry)

    @pl.when(st == 0)
    def _():
        ext_pool[0:POOL_HALO, :] = jnp.zeros((POOL_HALO, POOL_WIDTH), F32)
        ext_conv[0:CONV_HALO, :] = jnp.zeros((CONV_HALO, CONV_WIDTH), F32)

    x = x_ref[...]
    hb = _rms(x, g_mix_ref[...]).astype(BF16)

    o0 = POOL_WIDTH
    o1 = o0 + CONV_WIDTH
    o2 = o1 + CONV_WIDTH
    o3 = o2 + CONV_WIDTH

    u = _dot(hb, w_in_ref[:, 0:o0])
    ext_pool[POOL_HALO:POOL_HALO + tm, :] = u
    gc = _dot(hb, w_in_ref[:, o1:o2])
    v = _dot(hb, w_in_ref[:, o2:o3])
    pre_a = _dot(hb, w_in_ref[:, o3:o3 + D_MODEL])

    t_glob = st * tm + lax.broadcasted_iota(jnp.int32, (tm, 1), 0)
    mixed = []
    for gi, w in enumerate(POOL_WINDOWS):
        cols = slice(gi * POOL_GROUP, (gi + 1) * POOL_GROUP)
        ug = u[:, cols]
        acc = ug
        for j in range(1, w):
            acc = acc + ext_pool[POOL_HALO - j:POOL_HALO - j + tm, cols]
        cnt = jnp.minimum(t_glob + 1, w).astype(F32)
        pooled = acc * (1.0 / cnt) - ug
        mixed.append(_dot(pooled.astype(BF16), w_grp_ref[gi]))
    ext_pool[0:POOL_HALO, :] = ext_pool[tm:tm + POOL_HALO, :]
    pre_b = _dot(hb, w_in_ref[:, o3 + D_MODEL:o3 + 2 * D_MODEL])
    gb = _dot(hb, w_in_ref[:, o0:o1])

    cv = gc * v
    ext_conv[CONV_HALO:CONV_HALO + tm, :] = cv
    conv = (ext_conv[CONV_HALO - 2:CONV_HALO - 2 + tm, :] * conv_w_ref[0:1, :]
            + ext_conv[CONV_HALO - 1:CONV_HALO - 1 + tm, :] * conv_w_ref[1:2, :]
            + cv * conv_w_ref[2:3, :])
    ext_conv[0:CONV_HALO, :] = ext_conv[tm:tm + CONV_HALO, :]
    branch_b = _dot((gb * conv).astype(BF16), w_co_ref[...])
    mixed = jnp.concatenate(mixed, axis=1) * pool_scale_ref[...]
    branch_a = _dot(mixed.astype(BF16), w_po_ref[...])

    merged = (_sigmoid(pre_a + b_gate_ref[:, 0:D_MODEL]) * branch_a
              + _sigmoid(pre_b + b_gate_ref[:, D_MODEL:2 * D_MODEL]) * branch_b)
    x1 = x + _dot(merged.astype(BF16), w_o_ref[...])

    h2 = _rms(x1, g_ffn_ref[...])
    h2p_ref[...] = _pack_pair(h2[:, 0:HALF], h2[:, HALF:D_MODEL])
    h2b = h2.astype(BF16)

    nt = (((1,), (1,)), ((), ()))
    parts = lax.dot_general(wr_ref[...], h2b, nt, preferred_element_type=F32)
    logits = parts[0:N_EXPERTS, :] + parts[N_EXPERTS:2 * N_EXPERTS, :]
    sg = _dot(h2b, s_gate_ref[...])
    su = _dot(h2b, s_up_ref[...])
    scores = jax.nn.sigmoid(logits)
    sel = scores + rbias_ref[...]
    eidx = lax.broadcasted_iota(jnp.int32, (N_EXPERTS, tm), 0).astype(F32)
    e_rows, w_rows = [], []
    mask = jnp.zeros((N_EXPERTS, tm), F32)
    for _ in range(TOP_K):
        m = jnp.max(sel, axis=0, keepdims=True)
        ek = jnp.min(jnp.where(sel == m, eidx, float(N_EXPERTS)), axis=0, keepdims=True)
        oh = eidx == ek
        w_rows.append(jnp.sum(jnp.where(oh, scores, 0.0), axis=0, keepdims=True))
        e_rows.append(ek)
        mask = mask + oh.astype(F32)
        sel = jnp.where(oh, -jnp.inf, sel)

    shared = _dot((sg * _sigmoid(sg) * su).astype(BF16), s_down_ref[...])
    xres_ref[...] = x1 + shared

    wsum = w_rows[0]
    for k in range(1, TOP_K):
        wsum = wsum + w_rows[k]

    before = _dot(mask.astype(BF16), tri[...]) + cnt_carry[...]
    for k in range(TOP_K):
        oh = eidx == e_rows[k]
        rank_ref[k:k + 1, :] = jnp.sum(jnp.where(oh, before, 0.0), axis=0,
                                       keepdims=True).astype(jnp.int32)
        idx_ref[k:k + 1, :] = e_rows[k].astype(jnp.int32)
        wsel_ref[k:k + 1, :] = w_rows[k] / wsum * ROUTED_SCALE
    total = cnt_carry[...] + jnp.sum(mask, axis=1, keepdims=True)
    cnt_carry[...] = total
    counts_ref[...] = total.astype(jnp.int32)


def _mixer_router(x2d, tok0, t_tok, seq_len, g_mix, w_in, b_gate, w_grp, pool_scale, w_po, conv_w,
                  w_co, w_o, g_ffn, wr, rbias, s_gate, s_up, s_down):
    tm = TM_MIX
    n_seq_tiles = seq_len // tm
    off = tok0 // tm
    const = lambda shape: pl.BlockSpec(shape, lambda i: (0,) * len(shape),
                                       pipeline_mode=pl.Buffered(1))
    hbm = pl.BlockSpec(memory_space=pl.ANY)
    x_blk = pl.BlockSpec((tm, D_MODEL), lambda i: (i + off, 0))
    row_blk = pl.BlockSpec((tm, D_MODEL), lambda i: (i, 0))
    half_blk = pl.BlockSpec((tm, HALF), lambda i: (i, 0))
    slot_blk = pl.BlockSpec((TOP_K, tm), lambda i: (0, i))
    return pl.pallas_call(
        functools.partial(_mixer_router_kernel, n_seq_tiles),
        grid=(t_tok // tm,),
        in_specs=[x_blk, const(g_mix.shape), hbm, const(b_gate.shape),
                  hbm, const(pool_scale.shape), hbm,
                  const(conv_w.shape), hbm, hbm, const(g_ffn.shape),
                  const(wr.shape), const(rbias.shape), hbm, hbm, hbm],
        out_specs=[row_blk, half_blk, slot_blk, slot_blk, slot_blk,
                   pl.BlockSpec((N_EXPERTS, 1), lambda i: (0, 0))],
        out_shape=[jax.ShapeDtypeStruct((t_tok, D_MODEL), F32),
                   jax.ShapeDtypeStruct((t_tok, HALF), U32),
                   jax.ShapeDtypeStruct((TOP_K, t_tok), jnp.int32),
                   jax.ShapeDtypeStruct((TOP_K, t_tok), F32),
                   jax.ShapeDtypeStruct((TOP_K, t_tok), jnp.int32),
                   jax.ShapeDtypeStruct((N_EXPERTS, 1), jnp.int32)],
        scratch_shapes=[pltpu.VMEM((POOL_HALO + tm, POOL_WIDTH), F32),
                        pltpu.VMEM((CONV_HALO + tm, CONV_WIDTH), F32),
                        pltpu.VMEM((N_EXPERTS, 1), F32),
                        pltpu.VMEM((tm, tm), BF16)]
                       + [pltpu.VMEM(w.shape, BF16)
                          for w in (w_in, w_grp, w_po, w_co, w_o, s_gate, s_up, s_down)]
                       + [pltpu.VMEM((2, W_IN_CHUNK, w_in.shape[1]), F32),
                          pltpu.VMEM(w_o.shape, F32),
                          pltpu.VMEM((2,) + s_gate.shape, F32),
                          pltpu.VMEM(w_grp.shape, F32),
                          pltpu.SemaphoreType.DMA((7,))],
        compiler_params=pltpu.CompilerParams(dimension_semantics=("arbitrary",),
                                             vmem_limit_bytes=VMEM_LIMIT),
        name="mixer_router",
    )(x2d, g_mix, w_in, b_gate, w_grp, pool_scale, w_po, conv_w, w_co, w_o, g_ffn,
      wr, rbias, s_gate, s_up, s_down)


def _plan_kernel(n_blk, n_pad, counts_ref, idx_ref, rank_ref,
                 dest_ref, blk_e_ref, next_e_ref, n_used_ref, n_zero_ref, zero_rows_ref, pad_start):
    e_n = N_EXPERTS

    @pl.when(pl.program_id(0) == 0)
    def _():
        sub = lax.broadcasted_iota(jnp.int32, (e_n, e_n), 0)
        lane = lax.broadcasted_iota(jnp.int32, (e_n, e_n), 1)
        c_col = counts_ref[...]
        p_col = ((c_col + (ROW_BLOCK - 1)) // ROW_BLOCK) * ROW_BLOCK
        c_f = c_col.astype(F32)
        p_f = p_col.astype(F32)
        gap_f = p_f - c_f
        to_row = lambda col: jnp.sum(jnp.where(sub == lane, col, 0.0), axis=0, keepdims=True)
        p_row = to_row(p_f)
        gap_row = to_row(gap_f)
        pad_end_col = jnp.sum(jnp.where(lane <= sub, p_row, 0.0), axis=1, keepdims=True)
        gap_before_col = jnp.sum(jnp.where(lane < sub, gap_row, 0.0), axis=1, keepdims=True)
        pad_start[...] = pad_end_col - p_f
        pad_end_last = jnp.sum(p_row, axis=1, keepdims=True)
        n_used_ref[...] = jnp.broadcast_to(pad_end_last * (1.0 / ROW_BLOCK),
                                           n_used_ref.shape).astype(jnp.int32)
        gap_total = jnp.sum(gap_row, axis=1, keepdims=True)
        n_zero_ref[...] = jnp.broadcast_to(jnp.floor((gap_total + (SC_CHUNK - 1))
                                                     * (1.0 / SC_CHUNK)),
                                           n_zero_ref.shape).astype(jnp.int32)
        row0 = (lax.broadcasted_iota(jnp.int32, (e_n, blk_e_ref.shape[1]), 1)
                * ROW_BLOCK).astype(F32)
        owner = jnp.sum(jnp.where(pad_end_col <= row0, 1.0, 0.0), axis=0, keepdims=True)
        blk_e_ref[...] = jnp.minimum(owner, float(e_n - 1)).astype(jnp.int32)
        used_later = jnp.logical_and(sub > lane, p_f > 0.0)
        nxt = jnp.min(jnp.where(used_later, sub, e_n), axis=0, keepdims=True)
        next_e_ref[...] = jnp.full(next_e_ref.shape, -1, jnp.int32)
        next_e_ref[:, 0:e_n] = jnp.where(nxt < e_n, nxt, -1)
        for j0 in range(0, n_pad, PLAN_LANES):
            j = (j0 + lax.broadcasted_iota(jnp.int32, (e_n, PLAN_LANES), 1)).astype(F32)
            before = jnp.sum(jnp.where(gap_before_col <= j, c_f, 0.0), axis=0, keepdims=True)
            zero_rows_ref[:, j0:j0 + PLAN_LANES] = (j[0:1, :] + before).astype(jnp.int32)

    tm = idx_ref.shape[1]
    eidx = lax.broadcasted_iota(jnp.int32, (e_n, tm), 0)
    ps = pad_start[...]
    for k in range(TOP_K):
        oh = eidx == idx_ref[k:k + 1, :]
        start = jnp.sum(jnp.where(oh, ps, 0.0), axis=0, keepdims=True)
        dest_ref[k:k + 1, :] = start.astype(jnp.int32) + rank_ref[k:k + 1, :]


def _plan(counts, idx_t, rank_t, n_blk, n_pad):
    t_tok = idx_t.shape[1]
    lanes = 128
    n_blk_p = -(-n_blk // lanes) * lanes
    slot_blk = pl.BlockSpec((TOP_K, TM_DEST), lambda i: (0, i))
    whole = lambda n: pl.BlockSpec((1, n), lambda i: (0, 0))
    dest_t, blk_e, next_e, n_used, n_zero, zero_rows = pl.pallas_call(
        functools.partial(_plan_kernel, n_blk, n_pad),
        grid=(t_tok // TM_DEST,),
        in_specs=[pl.BlockSpec((N_EXPERTS, 1), lambda i: (0, 0)), slot_blk, slot_blk],
        out_specs=[slot_blk, whole(n_blk_p), whole(lanes), whole(lanes), whole(lanes),
                   whole(n_pad)],
        out_shape=[jax.ShapeDtypeStruct((TOP_K, t_tok), jnp.int32),
                   jax.ShapeDtypeStruct((1, n_blk_p), jnp.int32),
                   jax.ShapeDtypeStruct((1, lanes), jnp.int32),
                   jax.ShapeDtypeStruct((1, lanes), jnp.int32),
                   jax.ShapeDtypeStruct((1, lanes), jnp.int32),
                   jax.ShapeDtypeStruct((1, n_pad), jnp.int32)],
        scratch_shapes=[pltpu.VMEM((N_EXPERTS, 1), F32)],
        compiler_params=pltpu.CompilerParams(dimension_semantics=("arbitrary",)),
        name="plan",
    )(counts, idx_t, rank_t)
    return (dest_t, blk_e.reshape(-1), next_e.reshape(-1), n_used.reshape(-1), n_zero.reshape(-1),
            zero_rows.reshape(-1))


def _sc_mesh():
    return plsc.VectorSubcoreMesh(core_axis_name="c", subcore_axis_name="s")


def _sc_worker_id():
    return lax.axis_index("s") * SC_CORES + lax.axis_index("c")


def _dispatch(h2p, dest_t, zero_rows, n_zero, n_rows):
    t_tok, width = h2p.shape
    per_w = t_tok // SC_WORKERS
    n_chunks = per_w // SC_CHUNK
    z_chunks = zero_rows.shape[0] // (SC_WORKERS * SC_CHUNK)
    dest_w = (dest_t.reshape(TOP_K, SC_WORKERS, n_chunks, SC_CHUNK)
              .transpose(1, 2, 0, 3).reshape(SC_WORKERS * n_chunks * TOP_K, SC_CHUNK))
    zero_w = zero_rows.reshape(z_chunks * SC_WORKERS, SC_CHUNK)

    @functools.partial(
        pl.kernel, mesh=_sc_mesh(),
        out_type=jax.ShapeDtypeStruct((n_rows, width), h2p.dtype),
        scratch_types=[pltpu.VMEM((n_chunks * TOP_K, SC_CHUNK), jnp.int32),
                       pltpu.VMEM((z_chunks, SC_CHUNK), jnp.int32),
                       pltpu.VMEM((SC_LANES,), jnp.int32),
                       pltpu.VMEM((2, SC_CHUNK, width), h2p.dtype),
                       pltpu.VMEM((SC_CHUNK, width), h2p.dtype),
                       pltpu.SemaphoreType.DMA, pltpu.SemaphoreType.DMA,
                       pltpu.SemaphoreType.DMA, pltpu.SemaphoreType.DMA],
        compiler_params=pltpu.CompilerParams(needs_layout_passes=False),
        name="dispatch",
    )
    def k(h2p_hbm, dest_hbm, zidx_hbm, nz_hbm, zsrc_hbm, xs_hbm, idx_v, zidx_v, nz_v, rows_v, zero_v,
          gsem, wsem, zsem, isem):
        wid = _sc_worker_id()
        base = wid * per_w
        zidx_copies = [pltpu.make_async_copy(zidx_hbm.at[pl.ds(j * SC_WORKERS + wid, 1)],
                                             zidx_v.at[pl.ds(j, 1)], isem) for j in range(z_chunks)]
        for cp in zidx_copies:
            cp.start()
        pltpu.sync_copy(dest_hbm.at[pl.ds(wid * n_chunks * TOP_K, n_chunks * TOP_K)], idx_v)
        pltpu.sync_copy(nz_hbm.at[pl.ds(0, SC_LANES)], nz_v)
        pltpu.sync_copy(zsrc_hbm, zero_v)
        for cp in zidx_copies:
            cp.wait()
        n_zero_chunks = jnp.max(nz_v[...])

        def zput(j):
            return pltpu.make_async_copy(zero_v, xs_hbm.at[zidx_v.at[j]], zsem)

        for j in range(z_chunks):
            @pl.when(j * SC_WORKERS + wid < n_zero_chunks)
            def _():
                zput(j).start()

        def get(j, slot):
            return pltpu.make_async_copy(h2p_hbm.at[pl.ds(base + j * SC_CHUNK, SC_CHUNK)],
                                         rows_v.at[slot], gsem)

        def put(j, slot, kk):
            return pltpu.make_async_copy(rows_v.at[slot], xs_hbm.at[idx_v.at[j * TOP_K + kk]], wsem)

        get(0, 0).start()

        @pl.loop(0, n_chunks, step=2)
        def _(j):
            for b in range(2):
                jj = j + b
                get(jj, b).wait()

                @pl.when(jj >= 1)
                def _():
                    for kk in range(TOP_K):
                        put(jj - 1, 1 - b, kk).wait()

                @pl.when(jj + 1 < n_chunks)
                def _():
                    get(jj + 1, 1 - b).start()
                for kk in range(TOP_K):
                    put(jj, b, kk).start()

        for kk in range(TOP_K):
            put(n_chunks - 1, (n_chunks - 1) % 2, kk).wait()
        for j in range(z_chunks):
            @pl.when(j * SC_WORKERS + wid < n_zero_chunks)
            def _():
                zput(j).wait()

    return k(h2p, dest_w, zero_w, n_zero, jnp.zeros((SC_CHUNK, width), h2p.dtype))


def _regroup_sum(ys, dest_t, wsel_t, xres):
    n_slots, t_tok = dest_t.shape
    width = ys.shape[1]
    d_out = xres.shape[1]
    grp = SC_SUM_GROUP
    r_rows = SC_SUM_ROW_RING
    r_acc = SC_SUM_ACC_RING
    per_w = t_tok // SC_WORKERS
    n_sub = per_w // grp

    @functools.partial(
        pl.kernel, mesh=_sc_mesh(),
        out_type=jax.ShapeDtypeStruct((t_tok, d_out), F32),
        scratch_types=[pltpu.VMEM((n_slots, per_w), jnp.int32),
                       pltpu.VMEM((n_slots, per_w), F32),
                       pltpu.VMEM((r_rows, n_slots * grp, width), ys.dtype),
                       pltpu.VMEM((r_acc, grp, d_out), F32),
                       pltpu.SemaphoreType.DMA((r_rows,)), pltpu.SemaphoreType.DMA((r_acc,)),
                       pltpu.SemaphoreType.DMA((r_acc,))],
        compiler_params=pltpu.CompilerParams(needs_layout_passes=False),
        name="regroup_sum",
    )
    def k(ys_hbm, idx_hbm, w_hbm, xres_hbm, out_hbm, idx_v, w_v, rows_v, acc_v, gsem, xsem, psem):
        wid = _sc_worker_id()
        base = wid * per_w
        pltpu.sync_copy(idx_hbm.at[:, pl.ds(base, per_w)], idx_v)
        pltpu.sync_copy(w_hbm.at[:, pl.ds(base, per_w)], w_v)

        def gets(j, slot):
            return [pltpu.make_async_copy(ys_hbm.at[idx_v.at[kk, pl.ds(j * grp, grp)]],
                                          rows_v.at[slot, pl.ds(kk * grp, grp)], gsem.at[slot])
                    for kk in range(n_slots)]

        def xload(j, slot):
            return pltpu.make_async_copy(xres_hbm.at[pl.ds(base + j * grp, grp)], acc_v.at[slot],
                                         xsem.at[slot])

        def put(j, slot):
            return pltpu.make_async_copy(acc_v.at[slot], out_hbm.at[pl.ds(base + j * grp, grp)],
                                         psem.at[slot])

        def accumulate(j, rslot, aslot):
            @pl.loop(0, grp)
            def _(g):
                pos = jnp.full((SC_LANES,), j * grp + g, jnp.int32)
                wk = [plsc.load_gather(w_v, [jnp.full((SC_LANES,), kk, jnp.int32), pos])
                      for kk in range(n_slots)]

                @plsc.parallel_loop(0, width // SC_LANES, unroll=4)
                def _(v):
                    cols_a = pl.ds(v * SC_LANES, SC_LANES)
                    cols_b = pl.ds(width + v * SC_LANES, SC_LANES)
                    acc_a = acc_v[aslot, g, cols_a]
                    acc_b = acc_v[aslot, g, cols_b]
                    for kk in range(n_slots):
                        ya, yb = _unpack_pair(rows_v[rslot, kk * grp + g, cols_a])
                        acc_a = acc_a + ya * wk[kk]
                        acc_b = acc_b + yb * wk[kk]
                    acc_v[aslot, g, cols_a] = acc_a
                    acc_v[aslot, g, cols_b] = acc_b

        for cp in gets(0, 0):
            cp.start()
        xload(0, 0).start()

        @pl.loop(0, n_sub, step=r_acc)
        def _(j):
            for b in range(r_acc):
                jj = j + b
                rs = b % r_rows
                na = (b + 1) % r_acc
                for cp in gets(jj, rs):
                    cp.wait()

                @pl.when(jj + 1 < n_sub)
                def _():
                    for cp in gets(jj + 1, (b + 1) % r_rows):
                        cp.start()
                xload(jj, b).wait()

                @pl.when(jj + 1 < n_sub)
                def _():
                    @pl.when(jj + 1 >= r_acc)
                    def _():
                        put(jj + 1 - r_acc, na).wait()
                    xload(jj + 1, na).start()
                accumulate(jj, rs, b)
                put(jj, b).start()

        for p in range(r_acc):
            put(n_sub - r_acc + p, p).wait()

    return k(ys, dest_t, wsel_t, xres)


def _experts_kernel(n_blk, blk_e_ref, next_e_ref, n_used_ref,
                    xs_hbm, eg_hbm, eu_hbm, ed_hbm, ys_hbm,
                    xbuf, ybuf, hid, stg_g, stg_u, stg_d, wg, wu, wd, xsem, ysem, wsem):
    n = n_used_ref[0]

    def ring(b):
        return jnp.bitwise_and(b, ROW_RING - 1)

    def x_copy(b):
        return pltpu.make_async_copy(xs_hbm.at[pl.ds(pl.multiple_of(b * ROW_BLOCK, ROW_BLOCK),
                                                     ROW_BLOCK)], xbuf.at[ring(b)], xsem.at[ring(b)])

    def y_copy(b):
        return pltpu.make_async_copy(ybuf.at[ring(b)],
                                     ys_hbm.at[pl.ds(pl.multiple_of(b * ROW_BLOCK, ROW_BLOCK),
                                                     ROW_BLOCK)], ysem.at[ring(b)])

    def w_copies(e):
        return (pltpu.make_async_copy(eg_hbm.at[e], stg_g, wsem.at[0]),
                pltpu.make_async_copy(eu_hbm.at[e], stg_u, wsem.at[1]),
                pltpu.make_async_copy(ed_hbm.at[e], stg_d, wsem.at[2]))

    def switch_expert(e, wslot):
        for cp in w_copies(e):
            cp.wait()
        wg[wslot] = stg_g[...].astype(BF16)
        wu[wslot] = stg_u[...].astype(BF16)
        wd[wslot] = stg_d[...].astype(BF16)
        nxt = next_e_ref[e]

        @pl.when(nxt >= 0)
        def _():
            for cp in w_copies(nxt):
                cp.start()

    def gate_up(b, wslot):
        xa, xb = _unpack_pair(xbuf[ring(b)])
        xb16 = jnp.concatenate([xa, xb], axis=1).astype(BF16)
        g = _dot(xb16, wg[wslot])
        up = _dot(xb16, wu[wslot])
        hid[jnp.bitwise_and(b, 1)] = (g * _sigmoid(g) * up).astype(BF16)

    def down(b, wslot):
        y = _dot(hid[jnp.bitwise_and(b, 1)], wd[wslot])
        ybuf[ring(b)] = _pack_pair(y[:, 0:HALF], y[:, HALF:D_MODEL])

    e0 = blk_e_ref[0]
    for cp in w_copies(e0):
        cp.start()
    for j in range(ROW_RING):
        @pl.when(j < n)
        def _():
            x_copy(j).start()
    switch_expert(e0, 0)
    x_copy(0).wait()
    gate_up(0, 0)

    def body(b, wslot_prev):
        e = blk_e_ref[b]
        first = e != blk_e_ref[b - 1]
        wslot = jnp.where(first, 1 - wslot_prev, wslot_prev)

        @pl.when(first)
        def _():
            switch_expert(e, wslot)

        x_copy(b).wait()

        @pl.when(b + ROW_RING - 1 < n)
        def _():
            x_copy(b + ROW_RING - 1).start()

        @pl.when(b >= ROW_RING + 1)
        def _():
            y_copy(b - 1 - ROW_RING).wait()

        down(b - 1, wslot_prev)
        gate_up(b, wslot)
        y_copy(b - 1).start()
        return wslot

    wslot_last = lax.fori_loop(1, n, body, jnp.int32(0))

    last = n - 1

    @pl.when(last >= ROW_RING)
    def _():
        y_copy(last - ROW_RING).wait()
    down(last, wslot_last)
    y_copy(last).start()
    for j in range(ROW_RING - 1, -1, -1):
        @pl.when(last - j >= 0)
        def _():
            y_copy(last - j).wait()

    ybuf[0] = jnp.zeros((ROW_BLOCK, HALF), U32)

    def zero_tail(b, c):
        cp = pltpu.make_async_copy(ybuf.at[0],
                                   ys_hbm.at[pl.ds(pl.multiple_of(b * ROW_BLOCK, ROW_BLOCK),
                                                   ROW_BLOCK)], ysem.at[0])
        cp.start()
        cp.wait()
        return c
    lax.fori_loop(n, n_blk, zero_tail, 0)


def _experts(blk_e, next_e, n_used, xs, e_gate, e_up, e_down):
    n_rows = xs.shape[0]
    n_blk = n_rows // ROW_BLOCK
    any_spec = pl.BlockSpec(memory_space=pl.ANY)
    grid_spec = pltpu.PrefetchScalarGridSpec(
        num_scalar_prefetch=3,
        grid=(1,),
        in_specs=[any_spec, any_spec, any_spec, any_spec],
        out_specs=any_spec,
        scratch_shapes=[pltpu.VMEM((ROW_RING, ROW_BLOCK, HALF), U32),
                        pltpu.VMEM((ROW_RING, ROW_BLOCK, HALF), U32),
                        pltpu.VMEM((2, ROW_BLOCK, EXPERT_HIDDEN), BF16),
                        pltpu.VMEM((D_MODEL, EXPERT_HIDDEN), F32),
                        pltpu.VMEM((D_MODEL, EXPERT_HIDDEN), F32),
                        pltpu.VMEM((EXPERT_HIDDEN, D_MODEL), F32),
                        pltpu.VMEM((2, D_MODEL, EXPERT_HIDDEN), BF16),
                        pltpu.VMEM((2, D_MODEL, EXPERT_HIDDEN), BF16),
                        pltpu.VMEM((2, EXPERT_HIDDEN, D_MODEL), BF16),
                        pltpu.SemaphoreType.DMA((ROW_RING,)),
                        pltpu.SemaphoreType.DMA((ROW_RING,)),
                        pltpu.SemaphoreType.DMA((3,))],
    )
    return pl.pallas_call(
        functools.partial(_experts_kernel, n_blk),
        grid_spec=grid_spec,
        out_shape=jax.ShapeDtypeStruct((n_rows, HALF), U32),
        compiler_params=pltpu.CompilerParams(dimension_semantics=("arbitrary",)),
        name="experts",
    )(blk_e, next_e, n_used, xs, e_gate, e_up, e_down)


def _final_norm_kernel(x_ref, g_ref, out_ref):
    out_ref[...] = _rms(x_ref[...], g_ref[...])


def _final_norm(xsum, g_final):
    t_tok = xsum.shape[0]
    blk = pl.BlockSpec((TM_NORM, D_MODEL), lambda i: (i, 0))
    return pl.pallas_call(
        _final_norm_kernel,
        grid=(t_tok // TM_NORM,),
        in_specs=[blk, pl.BlockSpec((1, D_MODEL), lambda i: (0, 0))],
        out_specs=blk,
        out_shape=jax.ShapeDtypeStruct((t_tok, D_MODEL), F32),
        compiler_params=pltpu.CompilerParams(dimension_semantics=("arbitrary",)),
        name="final_norm",
    )(xsum, g_final)


def kernel(x, g_mix, w_in, b_gate, w_pool_group, pool_scale, w_pool_out, conv_w, w_conv_out, w_o,
           g_ffn, w_router, router_bias, e_gate, e_up, e_down, s_gate, s_up, s_down, g_final):
    b, s, d = x.shape
    t_tok = b * s
    n_pad = N_EXPERTS * ROW_BLOCK
    n_rows = t_tok * TOP_K + n_pad
    n_blk = n_rows // ROW_BLOCK
    assert d == D_MODEL and s % TM_MIX == 0 and TM_MIX >= POOL_HALO
    assert t_tok % TM_DEST == 0 and t_tok % TM_NORM == 0 and n_pad % PLAN_LANES == 0
    assert t_tok % (2 * SC_WORKERS * SC_CHUNK) == 0 and n_pad % (SC_WORKERS * SC_CHUNK) == 0
    assert t_tok % (SC_WORKERS * SC_SUM_GROUP * SC_SUM_ACC_RING) == 0

    row = lambda a: a.reshape(1, -1)
    wr_t = w_router.T.astype(F32)
    wr_hi = wr_t.astype(BF16)
    wr = jnp.concatenate([wr_hi, (wr_t - wr_hi.astype(F32)).astype(BF16)], axis=0)

    xres, h2p, idx_t, wsel_t, rank_t, counts = _mixer_router(
        x.reshape(t_tok, d), 0, t_tok, s, row(g_mix), w_in, row(b_gate), w_pool_group,
        row(pool_scale), w_pool_out, conv_w, w_conv_out, w_o, row(g_ffn), wr,
        router_bias.astype(F32).reshape(N_EXPERTS, 1), s_gate, s_up, s_down)
    dest_t, blk_e, next_e, n_used, n_zero, zero_rows = _plan(counts, idx_t, rank_t, n_blk, n_pad)
    xs = _dispatch(h2p, dest_t, zero_rows, n_zero, n_rows)
    ys = _experts(blk_e, next_e, n_used, xs, e_gate, e_up, e_down)
    xsum = _regroup_sum(ys, dest_t, wsel_t, xres)
    return _final_norm(xsum, row(g_final)).reshape(b, s, d)
```

```python
import functools

import jax
import jax.numpy as jnp
from jax import lax
from jax.experimental import pallas as pl
from jax.experimental.pallas import tpu as pltpu
from jax.experimental.pallas import tpu_sc as plsc

D_MODEL = 1024
HALF = D_MODEL // 2
POOL_WIDTH = 512
N_POOL_GROUPS = 4
POOL_GROUP = 128
POOL_WINDOWS = (2, 4, 8, 16)
CONV_WIDTH = 512
N_EXPERTS = 64
TOP_K = 8
EXPERT_HIDDEN = 256
SHARED_HIDDEN = 256
ROUTED_SCALE = 2.5
EPS = 1e-6

POOL_HALO = 16
CONV_HALO = 8
TM_MIX = 512
W_IN_CHUNK = 128
TM_DEST = 2048
PLAN_LANES = 2048
ROW_BLOCK = 512
ROW_RING = 8
TM_NORM = 1024
VMEM_LIMIT = 56 * 1024 * 1024

SC_CORES = 2
SC_SUBCORES = 16
SC_WORKERS = SC_CORES * SC_SUBCORES
SC_LANES = 16
SC_CHUNK = 64
SC_SUM_GROUP = 8
SC_SUM_ROW_RING = 2
SC_SUM_ACC_RING = 4

BF16 = jnp.bfloat16
F32 = jnp.float32
U32 = jnp.uint32


def _rms(x, g):
    r = lax.rsqrt(jnp.mean(x * x, axis=-1, keepdims=True) + EPS)
    return (x * r) * g


def _dot(a, b):
    return jnp.dot(a, b, preferred_element_type=F32)


def _sigmoid(z):
    return 0.5 * jnp.tanh(0.5 * z) + 0.5


def _pack_pair(a, b):
    ra = lax.bitcast_convert_type(a.astype(BF16).astype(F32), U32)
    rb = lax.bitcast_convert_type(b.astype(BF16).astype(F32), U32)
    return ra | (rb >> 16)


def _unpack_pair(w):
    a = lax.bitcast_convert_type(w & jnp.uint32(0xFFFF0000), F32)
    b = lax.bitcast_convert_type(w << 16, F32)
    return a, b


def _load_weights_bf16(w_in_hbm, w_grp_hbm, w_po_hbm, w_co_hbm, w_o_hbm, s_gate_hbm, s_up_hbm,
                       s_down_hbm, w_in_ref, w_grp_ref, w_po_ref, w_co_ref, w_o_ref, s_gate_ref,
                       s_up_ref, s_down_ref, stg_in, stg_sq, stg_sh, stg_grp, wsem):
    copy = lambda src, dst, k: pltpu.make_async_copy(src, dst, wsem.at[k])
    rows = stg_in.shape[1]
    n_in = w_in_hbm.shape[0] // rows
    half = w_po_hbm.shape[0]
    c_in = [copy(w_in_hbm.at[pl.ds(j * rows, rows)], stg_in.at[j % 2], j % 2) for j in range(n_in)]
    c_o = copy(w_o_hbm, stg_sq, 2)
    c_sg = copy(s_gate_hbm, stg_sh.at[0], 3)
    c_su = copy(s_up_hbm, stg_sh.at[1], 4)
    c_grp = copy(w_grp_hbm, stg_grp, 5)
    c_po = copy(w_po_hbm, stg_sq.at[0:half], 2)
    c_co = copy(w_co_hbm, stg_sq.at[half:2 * half], 6)
    c_sd = copy(s_down_hbm, stg_sq.at[0:s_down_hbm.shape[0]], 2)
    for cp in (c_in[0], c_in[1], c_o, c_sg, c_su, c_grp):
        cp.start()
    for j in range(n_in):
        c_in[j].wait()
        w_in_ref[j * rows:(j + 1) * rows, :] = stg_in[j % 2].astype(BF16)
        if j + 2 < n_in:
            c_in[j + 2].start()
    c_o.wait()
    w_o_ref[...] = stg_sq[...].astype(BF16)
    c_po.start()
    c_co.start()
    c_sg.wait()
    s_gate_ref[...] = stg_sh[0].astype(BF16)
    c_su.wait()
    s_up_ref[...] = stg_sh[1].astype(BF16)
    c_grp.wait()
    w_grp_ref[...] = stg_grp[...].astype(BF16)
    c_po.wait()
    w_po_ref[...] = stg_sq[0:half, :].astype(BF16)
    c_co.wait()
    w_co_ref[...] = stg_sq[half:2 * half, :].astype(BF16)
    c_sd.start()
    c_sd.wait()
    s_down_ref[...] = stg_sq[0:s_down_hbm.shape[0], :].astype(BF16)


def _mixer_router_kernel(n_seq_tiles,
                         x_ref, g_mix_ref, w_in_hbm, b_gate_ref, w_grp_hbm, pool_scale_ref,
                         w_po_hbm, conv_w_ref, w_co_hbm, w_o_hbm, g_ffn_ref,
                         wr_ref, rbias_ref, s_gate_hbm, s_up_hbm, s_down_hbm,
                         xres_ref, h2p_ref, idx_ref, wsel_ref, rank_ref, counts_ref,
                         ext_pool, ext_conv, cnt_carry, tri,
                         w_in_ref, w_grp_ref, w_po_ref, w_co_ref, w_o_ref, s_gate_ref, s_up_ref,
                         s_down_ref, stg_in, stg_sq, stg_sh, stg_grp, wsem):
    tm = x_ref.shape[0]
    i = pl.program_id(0)
    st = i % n_seq_tiles

    @pl.when(i == 0)
    def _():
        _load_weights_bf16(w_in_hbm, w_grp_hbm, w_po_hbm, w_co_hbm, w_o_hbm, s_gate_hbm, s_up_hbm,
                           s_down_hbm, w_in_ref, w_grp_ref, w_po_ref, w_co_ref, w_o_ref,
                           s_gate_ref, s_up_ref, s_down_ref, stg_in, stg_sq, stg_sh, stg_grp, wsem)
        r = lax.broadcasted_iota(jnp.int32, (tm, tm), 0)
        c = lax.broadcasted_iota(jnp.int32, (tm, tm), 1)
        tri[...] = (r < c).astype(BF16)
        cnt_carry[...] = jnp.zeros_like(cnt_carry)

    @pl.when(st == 0)
    def _():
        ext_pool[0:POOL_HALO, :] = jnp.zeros((POOL_HALO, POOL_WIDTH), F32)
        ext_conv[0:CONV_HALO, :] = jnp.zeros((CONV_HALO, CONV_WIDTH), F32)

    x = x_ref[...]
    hb = _rms(x, g_mix_ref[...]).astype(BF16)

    o0 = POOL_WIDTH
    o1 = o0 + CONV_WIDTH
    o2 = o1 + CONV_WIDTH
    o3 = o2 + CONV_WIDTH

    u = _dot(hb, w_in_ref[:, 0:o0])
    ext_pool[POOL_HALO:POOL_HALO + tm, :] = u
    gc = _dot(hb, w_in_ref[:, o1:o2])
    v = _dot(hb, w_in_ref[:, o2:o3])
    pre_a = _dot(hb, w_in_ref[:, o3:o3 + D_MODEL])

    t_glob = st * tm + lax.broadcasted_iota(jnp.int32, (tm, 1), 0)
    mixed = []
    for gi, w in enumerate(POOL_WINDOWS):
        cols = slice(gi * POOL_GROUP, (gi + 1) * POOL_GROUP)
        ug = u[:, cols]
        acc = ug
        for j in range(1, w):
            acc = acc + ext_pool[POOL_HALO - j:POOL_HALO - j + tm, cols]
        cnt = jnp.minimum(t_glob + 1, w).astype(F32)
        pooled = acc * (1.0 / cnt) - ug
        mixed.append(_dot(pooled.astype(BF16), w_grp_ref[gi]))
    ext_pool[0:POOL_HALO, :] = ext_pool[tm:tm + POOL_HALO, :]
    pre_b = _dot(hb, w_in_ref[:, o3 + D_MODEL:o3 + 2 * D_MODEL])
    gb = _dot(hb, w_in_ref[:, o0:o1])

    cv = gc * v
    ext_conv[CONV_HALO:CONV_HALO + tm, :] = cv
    conv = (ext_conv[CONV_HALO - 2:CONV_HALO - 2 + tm, :] * conv_w_ref[0:1, :]
            + ext_conv[CONV_HALO - 1:CONV_HALO - 1 + tm, :] * conv_w_ref[1:2, :]
            + cv * conv_w_ref[2:3, :])
    ext_conv[0:CONV_HALO, :] = ext_conv[tm:tm + CONV_HALO, :]
    branch_b = _dot((gb * conv).astype(BF16), w_co_ref[...])
    mixed = jnp.concatenate(mixed, axis=1) * pool_scale_ref[...]
    branch_a = _dot(mixed.astype(BF16), w_po_ref[...])

    merged = (_sigmoid(pre_a + b_gate_ref[:, 0:D_MODEL]) * branch_a
              + _sigmoid(pre_b + b_gate_ref[:, D_MODEL:2 * D_MODEL]) * branch_b)
    x1 = x + _dot(merged.astype(BF16), w_o_ref[...])

    h2 = _rms(x1, g_ffn_ref[...])
    h2p_ref[...] = _pack_pair(h2[:, 0:HALF], h2[:, HALF:D_MODEL])
    h2b = h2.astype(BF16)

    nt = (((1,), (1,)), ((), ()))
    parts = lax.dot_general(wr_ref[...], h2b, nt, preferred_element_type=F32)
    logits = parts[0:N_EXPERTS, :] + parts[N_EXPERTS:2 * N_EXPERTS, :]
    sg = _dot(h2b, s_gate_ref[...])
    su = _dot(h2b, s_up_ref[...])
    scores = jax.nn.sigmoid(logits)
    sel = scores + rbias_ref[...]
    eidx = lax.broadcasted_iota(jnp.int32, (N_EXPERTS, tm), 0).astype(F32)
    e_rows, w_rows = [], []
    mask = jnp.zeros((N_EXPERTS, tm), F32)
    for _ in range(TOP_K):
        m = jnp.max(sel, axis=0, keepdims=True)
        ek = jnp.min(jnp.where(sel == m, eidx, float(N_EXPERTS)), axis=0, keepdims=True)
        oh = eidx == ek
        w_rows.append(jnp.sum(jnp.where(oh, scores, 0.0), axis=0, keepdims=True))
        e_rows.append(ek)
        mask = mask + oh.astype(F32)
        sel = jnp.where(oh, -jnp.inf, sel)

    shared = _dot((sg * _sigmoid(sg) * su).astype(BF16), s_down_ref[...])
    xres_ref[...] = x1 + shared

    wsum = w_rows[0]
    for k in range(1, TOP_K):
        wsum = wsum + w_rows[k]

    before = _dot(mask.astype(BF16), tri[...]) + cnt_carry[...]
    for k in range(TOP_K):
        oh = eidx == e_rows[k]
        rank_ref[k:k + 1, :] = jnp.sum(jnp.where(oh, before, 0.0), axis=0,
                                       keepdims=True).astype(jnp.int32)
        idx_ref[k:k + 1, :] = e_rows[k].astype(jnp.int32)
        wsel_ref[k:k + 1, :] = w_rows[k] / wsum * ROUTED_SCALE
    total = cnt_carry[...] + jnp.sum(mask, axis=1, keepdims=True)
    cnt_carry[...] = total
    counts_ref[...] = total.astype(jnp.int32)


def _mixer_router(x2d, seq_len, g_mix, w_in, b_gate, w_grp, pool_scale, w_po, conv_w,
                  w_co, w_o, g_ffn, wr, rbias, s_gate, s_up, s_down):
    t_tok = x2d.shape[0]
    tm = TM_MIX
    n_seq_tiles = seq_len // tm
    const = lambda shape: pl.BlockSpec(shape, lambda i: (0,) * len(shape),
                                       pipeline_mode=pl.Buffered(1))
    hbm = pl.BlockSpec(memory_space=pl.ANY)
    row_blk = pl.BlockSpec((tm, D_MODEL), lambda i: (i, 0))
    half_blk = pl.BlockSpec((tm, HALF), lambda i: (i, 0))
    slot_blk = pl.BlockSpec((TOP_K, tm), lambda i: (0, i))
    return pl.pallas_call(
        functools.partial(_mixer_router_kernel, n_seq_tiles),
        grid=(t_tok // tm,),
        in_specs=[row_blk, const(g_mix.shape), hbm, const(b_gate.shape),
                  hbm, const(pool_scale.shape), hbm,
                  const(conv_w.shape), hbm, hbm, const(g_ffn.shape),
                  const(wr.shape), const(rbias.shape), hbm, hbm, hbm],
        out_specs=[row_blk, half_blk, slot_blk, slot_blk, slot_blk,
                   pl.BlockSpec((N_EXPERTS, 1), lambda i: (0, 0))],
        out_shape=[jax.ShapeDtypeStruct((t_tok, D_MODEL), F32),
                   jax.ShapeDtypeStruct((t_tok, HALF), U32),
                   jax.ShapeDtypeStruct((TOP_K, t_tok), jnp.int32),
                   jax.ShapeDtypeStruct((TOP_K, t_tok), F32),
                   jax.ShapeDtypeStruct((TOP_K, t_tok), jnp.int32),
                   jax.ShapeDtypeStruct((N_EXPERTS, 1), jnp.int32)],
        scratch_shapes=[pltpu.VMEM((POOL_HALO + tm, POOL_WIDTH), F32),
                        pltpu.VMEM((CONV_HALO + tm, CONV_WIDTH), F32),
                        pltpu.VMEM((N_EXPERTS, 1), F32),
                        pltpu.VMEM((tm, tm), BF16)]
                       + [pltpu.VMEM(w.shape, BF16)
                          for w in (w_in, w_grp, w_po, w_co, w_o, s_gate, s_up, s_down)]
                       + [pltpu.VMEM((2, W_IN_CHUNK, w_in.shape[1]), F32),
                          pltpu.VMEM(w_o.shape, F32),
                          pltpu.VMEM((2,) + s_gate.shape, F32),
                          pltpu.VMEM(w_grp.shape, F32),
                          pltpu.SemaphoreType.DMA((7,))],
        compiler_params=pltpu.CompilerParams(dimension_semantics=("arbitrary",),
                                             vmem_limit_bytes=VMEM_LIMIT),
        name="mixer_router",
    )(x2d, g_mix, w_in, b_gate, w_grp, pool_scale, w_po, conv_w, w_co, w_o, g_ffn,
      wr, rbias, s_gate, s_up, s_down)


def _plan_kernel(n_blk, n_pad, counts_ref, idx_ref, rank_ref,
                 dest_ref, blk_e_ref, next_e_ref, n_used_ref, n_zero_ref, zero_rows_ref, pad_start):
    e_n = N_EXPERTS

    @pl.when(pl.program_id(0) == 0)
    def _():
        sub = lax.broadcasted_iota(jnp.int32, (e_n, e_n), 0)
        lane = lax.broadcasted_iota(jnp.int32, (e_n, e_n), 1)
        c_col = counts_ref[...]
        p_col = ((c_col + (ROW_BLOCK - 1)) // ROW_BLOCK) * ROW_BLOCK
        c_f = c_col.astype(F32)
        p_f = p_col.astype(F32)
        gap_f = p_f - c_f
        to_row = lambda col: jnp.sum(jnp.where(sub == lane, col, 0.0), axis=0, keepdims=True)
        p_row = to_row(p_f)
        gap_row = to_row(gap_f)
        pad_end_col = jnp.sum(jnp.where(lane <= sub, p_row, 0.0), axis=1, keepdims=True)
        gap_before_col = jnp.sum(jnp.where(lane < sub, gap_row, 0.0), axis=1, keepdims=True)
        pad_start[...] = pad_end_col - p_f
        pad_end_last = jnp.sum(p_row, axis=1, keepdims=True)
        n_used_ref[...] = jnp.broadcast_to(pad_end_last * (1.0 / ROW_BLOCK),
                                           n_used_ref.shape).astype(jnp.int32)
        gap_total = jnp.sum(gap_row, axis=1, keepdims=True)
        n_zero_ref[...] = jnp.broadcast_to(jnp.floor((gap_total + (SC_CHUNK - 1))
                                                     * (1.0 / SC_CHUNK)),
                                           n_zero_ref.shape).astype(jnp.int32)
        row0 = (lax.broadcasted_iota(jnp.int32, (e_n, blk_e_ref.shape[1]), 1)
                * ROW_BLOCK).astype(F32)
        owner = jnp.sum(jnp.where(pad_end_col <= row0, 1.0, 0.0), axis=0, keepdims=True)
        blk_e_ref[...] = jnp.minimum(owner, float(e_n - 1)).astype(jnp.int32)
        used_later = jnp.logical_and(sub > lane, p_f > 0.0)
        nxt = jnp.min(jnp.where(used_later, sub, e_n), axis=0, keepdims=True)
        next_e_ref[...] = jnp.full(next_e_ref.shape, -1, jnp.int32)
        next_e_ref[:, 0:e_n] = jnp.where(nxt < e_n, nxt, -1)
        for j0 in range(0, n_pad, PLAN_LANES):
            j = (j0 + lax.broadcasted_iota(jnp.int32, (e_n, PLAN_LANES), 1)).astype(F32)
            before = jnp.sum(jnp.where(gap_before_col <= j, c_f, 0.0), axis=0, keepdims=True)
            zero_rows_ref[:, j0:j0 + PLAN_LANES] = (j[0:1, :] + before).astype(jnp.int32)

    tm = idx_ref.shape[1]
    eidx = lax.broadcasted_iota(jnp.int32, (e_n, tm), 0)
    ps = pad_start[...]
    for k in range(TOP_K):
        oh = eidx == idx_ref[k:k + 1, :]
        start = jnp.sum(jnp.where(oh, ps, 0.0), axis=0, keepdims=True)
        dest_ref[k:k + 1, :] = start.astype(jnp.int32) + rank_ref[k:k + 1, :]


def _plan(counts, idx_t, rank_t, n_blk, n_pad):
    t_tok = idx_t.shape[1]
    lanes = 128
    n_blk_p = -(-n_blk // lanes) * lanes
    slot_blk = pl.BlockSpec((TOP_K, TM_DEST), lambda i: (0, i))
    whole = lambda n: pl.BlockSpec((1, n), lambda i: (0, 0))
    dest_t, blk_e, next_e, n_used, n_zero, zero_rows = pl.pallas_call(
        functools.partial(_plan_kernel, n_blk, n_pad),
        grid=(t_tok // TM_DEST,),
        in_specs=[pl.BlockSpec((N_EXPERTS, 1), lambda i: (0, 0)), slot_blk, slot_blk],
        out_specs=[slot_blk, whole(n_blk_p), whole(lanes), whole(lanes), whole(lanes),
                   whole(n_pad)],
        out_shape=[jax.ShapeDtypeStruct((TOP_K, t_tok), jnp.int32),
                   jax.ShapeDtypeStruct((1, n_blk_p), jnp.int32),
                   jax.ShapeDtypeStruct((1, lanes), jnp.int32),
                   jax.ShapeDtypeStruct((1, lanes), jnp.int32),
                   jax.ShapeDtypeStruct((1, lanes), jnp.int32),
                   jax.ShapeDtypeStruct((1, n_pad), jnp.int32)],
        scratch_shapes=[pltpu.VMEM((N_EXPERTS, 1), F32)],
        compiler_params=pltpu.CompilerParams(dimension_semantics=("arbitrary",)),
        name="plan",
    )(counts, idx_t, rank_t)
    return (dest_t, blk_e.reshape(-1), next_e.reshape(-1), n_used.reshape(-1), n_zero.reshape(-1),
            zero_rows.reshape(-1))


def _sc_mesh():
    return plsc.VectorSubcoreMesh(core_axis_name="c", subcore_axis_name="s")


def _sc_worker_id():
    return lax.axis_index("s") * SC_CORES + lax.axis_index("c")


def _dispatch(h2p, dest_t, zero_rows, n_zero, n_rows):
    t_tok, width = h2p.shape
    per_w = t_tok // SC_WORKERS
    n_chunks = per_w // SC_CHUNK
    z_chunks = zero_rows.shape[0] // (SC_WORKERS * SC_CHUNK)
    dest_w = (dest_t.reshape(TOP_K, SC_WORKERS, n_chunks, SC_CHUNK)
              .transpose(1, 2, 0, 3).reshape(SC_WORKERS * n_chunks * TOP_K, SC_CHUNK))
    zero_w = zero_rows.reshape(z_chunks * SC_WORKERS, SC_CHUNK)

    @functools.partial(
        pl.kernel, mesh=_sc_mesh(),
        out_type=jax.ShapeDtypeStruct((n_rows, width), h2p.dtype),
        scratch_types=[pltpu.VMEM((n_chunks * TOP_K, SC_CHUNK), jnp.int32),
                       pltpu.VMEM((z_chunks, SC_CHUNK), jnp.int32),
                       pltpu.VMEM((SC_LANES,), jnp.int32),
                       pltpu.VMEM((2, SC_CHUNK, width), h2p.dtype),
                       pltpu.VMEM((SC_CHUNK, width), h2p.dtype),
                       pltpu.SemaphoreType.DMA, pltpu.SemaphoreType.DMA,
                       pltpu.SemaphoreType.DMA, pltpu.SemaphoreType.DMA],
        compiler_params=pltpu.CompilerParams(needs_layout_passes=False),
        name="dispatch",
    )
    def k(h2p_hbm, dest_hbm, zidx_hbm, nz_hbm, zsrc_hbm, xs_hbm, idx_v, zidx_v, nz_v, rows_v, zero_v,
          gsem, wsem, zsem, isem):
        wid = _sc_worker_id()
        base = wid * per_w
        zidx_copies = [pltpu.make_async_copy(zidx_hbm.at[pl.ds(j * SC_WORKERS + wid, 1)],
                                             zidx_v.at[pl.ds(j, 1)], isem) for j in range(z_chunks)]
        for cp in zidx_copies:
            cp.start()
        pltpu.sync_copy(dest_hbm.at[pl.ds(wid * n_chunks * TOP_K, n_chunks * TOP_K)], idx_v)
        pltpu.sync_copy(nz_hbm.at[pl.ds(0, SC_LANES)], nz_v)
        pltpu.sync_copy(zsrc_hbm, zero_v)
        for cp in zidx_copies:
            cp.wait()
        n_zero_chunks = jnp.max(nz_v[...])

        def zput(j):
            return pltpu.make_async_copy(zero_v, xs_hbm.at[zidx_v.at[j]], zsem)

        for j in range(z_chunks):
            @pl.when(j * SC_WORKERS + wid < n_zero_chunks)
            def _():
                zput(j).start()

        def get(j, slot):
            return pltpu.make_async_copy(h2p_hbm.at[pl.ds(base + j * SC_CHUNK, SC_CHUNK)],
                                         rows_v.at[slot], gsem)

        def put(j, slot, kk):
            return pltpu.make_async_copy(rows_v.at[slot], xs_hbm.at[idx_v.at[j * TOP_K + kk]], wsem)

        get(0, 0).start()

        @pl.loop(0, n_chunks, step=2)
        def _(j):
            for b in range(2):
                jj = j + b
                get(jj, b).wait()

                @pl.when(jj >= 1)
                def _():
                    for kk in range(TOP_K):
                        put(jj - 1, 1 - b, kk).wait()

                @pl.when(jj + 1 < n_chunks)
                def _():
                    get(jj + 1, 1 - b).start()
                for kk in range(TOP_K):
                    put(jj, b, kk).start()

        for kk in range(TOP_K):
            put(n_chunks - 1, (n_chunks - 1) % 2, kk).wait()
        for j in range(z_chunks):
            @pl.when(j * SC_WORKERS + wid < n_zero_chunks)
            def _():
                zput(j).wait()

    return k(h2p, dest_w, zero_w, n_zero, jnp.zeros((SC_CHUNK, width), h2p.dtype))


def _regroup_sum(ys, dest_t, wsel_t, xres):
    n_slots, t_tok = dest_t.shape
    width = ys.shape[1]
    d_out = xres.shape[1]
    grp = SC_SUM_GROUP
    r_rows = SC_SUM_ROW_RING
    r_acc = SC_SUM_ACC_RING
    per_w = t_tok // SC_WORKERS
    n_sub = per_w // grp

    @functools.partial(
        pl.kernel, mesh=_sc_mesh(),
        out_type=jax.ShapeDtypeStruct((t_tok, d_out), F32),
        scratch_types=[pltpu.VMEM((n_slots, per_w), jnp.int32),
                       pltpu.VMEM((n_slots, per_w), F32),
                       pltpu.VMEM((r_rows, n_slots * grp, width), ys.dtype),
                       pltpu.VMEM((r_acc, grp, d_out), F32),
                       pltpu.SemaphoreType.DMA((r_rows,)), pltpu.SemaphoreType.DMA((r_acc,)),
                       pltpu.SemaphoreType.DMA((r_acc,))],
        compiler_params=pltpu.CompilerParams(needs_layout_passes=False),
        name="regroup_sum",
    )
    def k(ys_hbm, idx_hbm, w_hbm, xres_hbm, out_hbm, idx_v, w_v, rows_v, acc_v, gsem, xsem, psem):
        wid = _sc_worker_id()
        base = wid * per_w
        pltpu.sync_copy(idx_hbm.at[:, pl.ds(base, per_w)], idx_v)
        pltpu.sync_copy(w_hbm.at[:, pl.ds(base, per_w)], w_v)

        def gets(j, slot):
            return [pltpu.make_async_copy(ys_hbm.at[idx_v.at[kk, pl.ds(j * grp, grp)]],
                                          rows_v.at[slot, pl.ds(kk * grp, grp)], gsem.at[slot])
                    for kk in range(n_slots)]

        def xload(j, slot):
            return pltpu.make_async_copy(xres_hbm.at[pl.ds(base + j * grp, grp)], acc_v.at[slot],
                                         xsem.at[slot])

        def put(j, slot):
            return pltpu.make_async_copy(acc_v.at[slot], out_hbm.at[pl.ds(base + j * grp, grp)],
                                         psem.at[slot])

        def accumulate(j, rslot, aslot):
            @pl.loop(0, grp)
            def _(g):
                pos = jnp.full((SC_LANES,), j * grp + g, jnp.int32)
                wk = [plsc.load_gather(w_v, [jnp.full((SC_LANES,), kk, jnp.int32), pos])
                      for kk in range(n_slots)]

                @plsc.parallel_loop(0, width // SC_LANES)
                def _(v):
                    cols_a = pl.ds(v * SC_LANES, SC_LANES)
                    cols_b = pl.ds(width + v * SC_LANES, SC_LANES)
                    acc_a = acc_v[aslot, g, cols_a]
                    acc_b = acc_v[aslot, g, cols_b]
                    for kk in range(n_slots):
                        ya, yb = _unpack_pair(rows_v[rslot, kk * grp + g, cols_a])
                        acc_a = acc_a + ya * wk[kk]
                        acc_b = acc_b + yb * wk[kk]
                    acc_v[aslot, g, cols_a] = acc_a
                    acc_v[aslot, g, cols_b] = acc_b

        for cp in gets(0, 0):
            cp.start()
        xload(0, 0).start()

        @pl.loop(0, n_sub, step=r_acc)
        def _(j):
            for b in range(r_acc):
                jj = j + b
                rs = b % r_rows
                na = (b + 1) % r_acc
                for cp in gets(jj, rs):
                    cp.wait()

                @pl.when(jj + 1 < n_sub)
                def _():
                    for cp in gets(jj + 1, (b + 1) % r_rows):
                        cp.start()
                xload(jj, b).wait()

                @pl.when(jj + 1 < n_sub)
                def _():
                    @pl.when(jj + 1 >= r_acc)
                    def _():
                        put(jj + 1 - r_acc, na).wait()
                    xload(jj + 1, na).start()
                accumulate(jj, rs, b)
                put(jj, b).start()

        for p in range(r_acc):
            put(n_sub - r_acc + p, p).wait()

    return k(ys, dest_t, wsel_t, xres)


def _experts_kernel(n_blk, blk_e_ref, next_e_ref, n_used_ref,
                    xs_hbm, eg_hbm, eu_hbm, ed_hbm, ys_hbm,
                    xbuf, ybuf, hid, stg_g, stg_u, stg_d, wg, wu, wd, xsem, ysem, wsem):
    n = n_used_ref[0]

    def ring(b):
        return jnp.bitwise_and(b, ROW_RING - 1)

    def x_copy(b):
        return pltpu.make_async_copy(xs_hbm.at[pl.ds(pl.multiple_of(b * ROW_BLOCK, ROW_BLOCK),
                                                     ROW_BLOCK)], xbuf.at[ring(b)], xsem.at[ring(b)])

    def y_copy(b):
        return pltpu.make_async_copy(ybuf.at[ring(b)],
                                     ys_hbm.at[pl.ds(pl.multiple_of(b * ROW_BLOCK, ROW_BLOCK),
                                                     ROW_BLOCK)], ysem.at[ring(b)])

    def w_copies(e):
        return (pltpu.make_async_copy(eg_hbm.at[e], stg_g, wsem.at[0]),
                pltpu.make_async_copy(eu_hbm.at[e], stg_u, wsem.at[1]),
                pltpu.make_async_copy(ed_hbm.at[e], stg_d, wsem.at[2]))

    def switch_expert(e, wslot):
        for cp in w_copies(e):
            cp.wait()
        wg[wslot] = stg_g[...].astype(BF16)
        wu[wslot] = stg_u[...].astype(BF16)
        wd[wslot] = stg_d[...].astype(BF16)
        nxt = next_e_ref[e]

        @pl.when(nxt >= 0)
        def _():
            for cp in w_copies(nxt):
                cp.start()

    def gate_up(b, wslot):
        xa, xb = _unpack_pair(xbuf[ring(b)])
        xb16 = jnp.concatenate([xa, xb], axis=1).astype(BF16)
        g = _dot(xb16, wg[wslot])
        up = _dot(xb16, wu[wslot])
        hid[jnp.bitwise_and(b, 1)] = (g * _sigmoid(g) * up).astype(BF16)

    def down(b, wslot):
        y = _dot(hid[jnp.bitwise_and(b, 1)], wd[wslot])
        ybuf[ring(b)] = _pack_pair(y[:, 0:HALF], y[:, HALF:D_MODEL])

    e0 = blk_e_ref[0]
    for cp in w_copies(e0):
        cp.start()
    for j in range(ROW_RING):
        @pl.when(j < n)
        def _():
            x_copy(j).start()
    switch_expert(e0, 0)
    x_copy(0).wait()
    gate_up(0, 0)

    def body(b, wslot_prev):
        e = blk_e_ref[b]
        first = e != blk_e_ref[b - 1]
        wslot = jnp.where(first, 1 - wslot_prev, wslot_prev)

        @pl.when(first)
        def _():
            switch_expert(e, wslot)

        x_copy(b).wait()

        @pl.when(b + ROW_RING - 1 < n)
        def _():
            x_copy(b + ROW_RING - 1).start()

        @pl.when(b >= ROW_RING + 1)
        def _():
            y_copy(b - 1 - ROW_RING).wait()

        down(b - 1, wslot_prev)
        gate_up(b, wslot)
        y_copy(b - 1).start()
        return wslot

    wslot_last = lax.fori_loop(1, n, body, jnp.int32(0))

    last = n - 1

    @pl.when(last >= ROW_RING)
    def _():
        y_copy(last - ROW_RING).wait()
    down(last, wslot_last)
    y_copy(last).start()
    for j in range(ROW_RING - 1, -1, -1):
        @pl.when(last - j >= 0)
        def _():
            y_copy(last - j).wait()

    ybuf[0] = jnp.zeros((ROW_BLOCK, HALF), U32)

    def zero_tail(b, c):
        cp = pltpu.make_async_copy(ybuf.at[0],
                                   ys_hbm.at[pl.ds(pl.multiple_of(b * ROW_BLOCK, ROW_BLOCK),
                                                   ROW_BLOCK)], ysem.at[0])
        cp.start()
        cp.wait()
        return c
    lax.fori_loop(n, n_blk, zero_tail, 0)


def _experts(blk_e, next_e, n_used, xs, e_gate, e_up, e_down):
    n_rows = xs.shape[0]
    n_blk = n_rows // ROW_BLOCK
    any_spec = pl.BlockSpec(memory_space=pl.ANY)
    grid_spec = pltpu.PrefetchScalarGridSpec(
        num_scalar_prefetch=3,
        grid=(1,),
        in_specs=[any_spec, any_spec, any_spec, any_spec],
        out_specs=any_spec,
        scratch_shapes=[pltpu.VMEM((ROW_RING, ROW_BLOCK, HALF), U32),
                        pltpu.VMEM((ROW_RING, ROW_BLOCK, HALF), U32),
                        pltpu.VMEM((2, ROW_BLOCK, EXPERT_HIDDEN), BF16),
                        pltpu.VMEM((D_MODEL, EXPERT_HIDDEN), F32),
                        pltpu.VMEM((D_MODEL, EXPERT_HIDDEN), F32),
                        pltpu.VMEM((EXPERT_HIDDEN, D_MODEL), F32),
                        pltpu.VMEM((2, D_MODEL, EXPERT_HIDDEN), BF16),
                        pltpu.VMEM((2, D_MODEL, EXPERT_HIDDEN), BF16),
                        pltpu.VMEM((2, EXPERT_HIDDEN, D_MODEL), BF16),
                        pltpu.SemaphoreType.DMA((ROW_RING,)),
                        pltpu.SemaphoreType.DMA((ROW_RING,)),
                        pltpu.SemaphoreType.DMA((3,))],
    )
    return pl.pallas_call(
        functools.partial(_experts_kernel, n_blk),
        grid_spec=grid_spec,
        out_shape=jax.ShapeDtypeStruct((n_rows, HALF), U32),
        compiler_params=pltpu.CompilerParams(dimension_semantics=("arbitrary",)),
        name="experts",
    )(blk_e, next_e, n_used, xs, e_gate, e_up, e_down)


def _final_norm_kernel(x_ref, g_ref, out_ref):
    out_ref[...] = _rms(x_ref[...], g_ref[...])


def _final_norm(xsum, g_final):
    t_tok = xsum.shape[0]
    blk = pl.BlockSpec((TM_NORM, D_MODEL), lambda i: (i, 0))
    return pl.pallas_call(
        _final_norm_kernel,
        grid=(t_tok // TM_NORM,),
        in_specs=[blk, pl.BlockSpec((1, D_MODEL), lambda i: (0, 0))],
        out_specs=blk,
        out_shape=jax.ShapeDtypeStruct((t_tok, D_MODEL), F32),
        compiler_params=pltpu.CompilerParams(dimension_semantics=("arbitrary",)),
        name="final_norm",
    )(xsum, g_final)


def kernel(x, g_mix, w_in, b_gate, w_pool_group, pool_scale, w_pool_out, conv_w, w_conv_out, w_o,
           g_ffn, w_router, router_bias, e_gate, e_up, e_down, s_gate, s_up, s_down, g_final):
    b, s, d = x.shape
    t_tok = b * s
    n_pad = N_EXPERTS * ROW_BLOCK
    n_rows = t_tok * TOP_K + n_pad
    n_blk = n_rows // ROW_BLOCK
    assert d == D_MODEL and s % TM_MIX == 0 and TM_MIX >= POOL_HALO
    assert t_tok % TM_DEST == 0 and t_tok % TM_NORM == 0 and n_pad % PLAN_LANES == 0
    assert t_tok % (2 * SC_WORKERS * SC_CHUNK) == 0 and n_pad % (SC_WORKERS * SC_CHUNK) == 0
    assert t_tok % (SC_WORKERS * SC_SUM_GROUP * SC_SUM_ACC_RING) == 0

    row = lambda a: a.reshape(1, -1)
    wr_t = w_router.T.astype(F32)
    wr_hi = wr_t.astype(BF16)
    wr = jnp.concatenate([wr_hi, (wr_t - wr_hi.astype(F32)).astype(BF16)], axis=0)

    xres, h2p, idx_t, wsel_t, rank_t, counts = _mixer_router(
        x.reshape(t_tok, d), s, row(g_mix), w_in, row(b_gate), w_pool_group,
        row(pool_scale), w_pool_out, conv_w, w_conv_out, w_o, row(g_ffn), wr,
        router_bias.astype(F32).reshape(N_EXPERTS, 1), s_gate, s_up, s_down)
    dest_t, blk_e, next_e, n_used, n_zero, zero_rows = _plan(counts, idx_t, rank_t, n_blk, n_pad)
    xs = _dispatch(h2p, dest_t, zero_rows, n_zero, n_rows)
    ys = _experts(blk_e, next_e, n_used, xs, e_gate, e_up, e_down)
    xsum = _regroup_sum(ys, dest_t, wsel_t, xres)
    return _final_norm(xsum, row(g_final)).reshape(b, s, d)
```

```python
import functools

import jax
import jax.numpy as jnp
from jax import lax
from jax.experimental import pallas as pl
from jax.experimental.pallas import tpu as pltpu
from jax.experimental.pallas import tpu_sc as plsc

D_MODEL = 1024
HALF = D_MODEL // 2
POOL_WIDTH = 512
N_POOL_GROUPS = 4
POOL_GROUP = 128
POOL_WINDOWS = (2, 4, 8, 16)
CONV_WIDTH = 512
N_EXPERTS = 64
TOP_K = 8
EXPERT_HIDDEN = 256
SHARED_HIDDEN = 256
ROUTED_SCALE = 2.5
EPS = 1e-6

POOL_HALO = 16
CONV_HALO = 8
TM_MIX = 512
W_IN_CHUNK = 128
TM_DEST = 2048
PLAN_LANES = 2048
ROW_BLOCK = 512
ROW_RING = 8
TM_NORM = 1024
VMEM_LIMIT = 56 * 1024 * 1024

SC_CORES = 2
SC_SUBCORES = 16
SC_WORKERS = SC_CORES * SC_SUBCORES
SC_LANES = 16
SC_CHUNK = 64
SC_SUM_GROUP = 8
SC_SUM_ROW_RING = 2
SC_SUM_ACC_RING = 4

BF16 = jnp.bfloat16
F32 = jnp.float32
U32 = jnp.uint32


def _rms(x, g):
    r = lax.rsqrt(jnp.mean(x * x, axis=-1, keepdims=True) + EPS)
    return (x * r) * g


def _dot(a, b):
    return jnp.dot(a, b, preferred_element_type=F32)


def _sigmoid(z):
    return 0.5 * jnp.tanh(0.5 * z) + 0.5


def _pack_pair(a, b):
    ra = lax.bitcast_convert_type(a.astype(BF16).astype(F32), U32)
    rb = lax.bitcast_convert_type(b.astype(BF16).astype(F32), U32)
    return ra | (rb >> 16)


def _unpack_pair(w):
    a = lax.bitcast_convert_type(w & jnp.uint32(0xFFFF0000), F32)
    b = lax.bitcast_convert_type(w << 16, F32)
    return a, b


def _load_weights_bf16(w_in_hbm, w_grp_hbm, w_po_hbm, w_co_hbm, w_o_hbm, s_gate_hbm, s_up_hbm,
                       s_down_hbm, w_in_ref, w_grp_ref, w_po_ref, w_co_ref, w_o_ref, s_gate_ref,
                       s_up_ref, s_down_ref, stg_in, stg_sq, stg_sh, stg_grp, wsem):
    copy = lambda src, dst, k: pltpu.make_async_copy(src, dst, wsem.at[k])
    rows = stg_in.shape[1]
    n_in = w_in_hbm.shape[0] // rows
    half = w_po_hbm.shape[0]
    c_in = [copy(w_in_hbm.at[pl.ds(j * rows, rows)], stg_in.at[j % 2], j % 2) for j in range(n_in)]
    c_o = copy(w_o_hbm, stg_sq, 2)
    c_sg = copy(s_gate_hbm, stg_sh.at[0], 3)
    c_su = copy(s_up_hbm, stg_sh.at[1], 4)
    c_grp = copy(w_grp_hbm, stg_grp, 5)
    c_po = copy(w_po_hbm, stg_sq.at[0:half], 2)
    c_co = copy(w_co_hbm, stg_sq.at[half:2 * half], 6)
    c_sd = copy(s_down_hbm, stg_sq.at[0:s_down_hbm.shape[0]], 2)
    for cp in (c_in[0], c_in[1], c_o, c_sg, c_su, c_grp):
        cp.start()
    for j in range(n_in):
        c_in[j].wait()
        w_in_ref[j * rows:(j + 1) * rows, :] = stg_in[j % 2].astype(BF16)
        if j + 2 < n_in:
            c_in[j + 2].start()
    c_o.wait()
    w_o_ref[...] = stg_sq[...].astype(BF16)
    c_po.start()
    c_co.start()
    c_sg.wait()
    s_gate_ref[...] = stg_sh[0].astype(BF16)
    c_su.wait()
    s_up_ref[...] = stg_sh[1].astype(BF16)
    c_grp.wait()
    w_grp_ref[...] = stg_grp[...].astype(BF16)
    c_po.wait()
    w_po_ref[...] = stg_sq[0:half, :].astype(BF16)
    c_co.wait()
    w_co_ref[...] = stg_sq[half:2 * half, :].astype(BF16)
    c_sd.start()
    c_sd.wait()
    s_down_ref[...] = stg_sq[0:s_down_hbm.shape[0], :].astype(BF16)


def _mixer_router_kernel(n_seq_tiles,
                         x_ref, g_mix_ref, w_in_hbm, b_gate_ref, w_grp_hbm, pool_scale_ref,
                         w_po_hbm, conv_w_ref, w_co_hbm, w_o_hbm, g_ffn_ref,
                         wr_ref, rbias_ref, s_gate_hbm, s_up_hbm, s_down_hbm,
                         xres_ref, h2p_ref, idx_ref, wsel_ref, rank_ref, counts_ref,
                         ext_pool, ext_conv, cnt_carry, tri,
                         w_in_ref, w_grp_ref, w_po_ref, w_co_ref, w_o_ref, s_gate_ref, s_up_ref,
                         s_down_ref, stg_in, stg_sq, stg_sh, stg_grp, wsem):
    tm = x_ref.shape[0]
    i = pl.program_id(0)
    st = i % n_seq_tiles

    @pl.when(i == 0)
    def _():
        _load_weights_bf16(w_in_hbm, w_grp_hbm, w_po_hbm, w_co_hbm, w_o_hbm, s_gate_hbm, s_up_hbm,
                           s_down_hbm, w_in_ref, w_grp_ref, w_po_ref, w_co_ref, w_o_ref,
                           s_gate_ref, s_up_ref, s_down_ref, stg_in, stg_sq, stg_sh, stg_grp, wsem)
        r = lax.broadcasted_iota(jnp.int32, (tm, tm), 0)
        c = lax.broadcasted_iota(jnp.int32, (tm, tm), 1)
        tri[...] = (r < c).astype(BF16)
        cnt_carry[...] = jnp.zeros_like(cnt_carry)

    @pl.when(st == 0)
    def _():
        ext_pool[0:POOL_HALO, :] = jnp.zeros((POOL_HALO, POOL_WIDTH), F32)
        ext_conv[0:CONV_HALO, :] = jnp.zeros((CONV_HALO, CONV_WIDTH), F32)

    x = x_ref[...]
    hb = _rms(x, g_mix_ref[...]).astype(BF16)

    o0 = POOL_WIDTH
    o1 = o0 + CONV_WIDTH
    o2 = o1 + CONV_WIDTH
    o3 = o2 + CONV_WIDTH

    u = _dot(hb, w_in_ref[:, 0:o0])
    ext_pool[POOL_HALO:POOL_HALO + tm, :] = u
    gc = _dot(hb, w_in_ref[:, o1:o2])
    v = _dot(hb, w_in_ref[:, o2:o3])
    pre_a = _dot(hb, w_in_ref[:, o3:o3 + D_MODEL])

    t_glob = st * tm + lax.broadcasted_iota(jnp.int32, (tm, 1), 0)
    mixed = []
    for gi, w in enumerate(POOL_WINDOWS):
        cols = slice(gi * POOL_GROUP, (gi + 1) * POOL_GROUP)
        ug = u[:, cols]
        acc = ug
        for j in range(1, w):
            acc = acc + ext_pool[POOL_HALO - j:POOL_HALO - j + tm, cols]
        cnt = jnp.minimum(t_glob + 1, w).astype(F32)
        pooled = acc * (1.0 / cnt) - ug
        mixed.append(_dot(pooled.astype(BF16), w_grp_ref[gi]))
    ext_pool[0:POOL_HALO, :] = ext_pool[tm:tm + POOL_HALO, :]
    pre_b = _dot(hb, w_in_ref[:, o3 + D_MODEL:o3 + 2 * D_MODEL])
    gb = _dot(hb, w_in_ref[:, o0:o1])

    cv = gc * v
    ext_conv[CONV_HALO:CONV_HALO + tm, :] = cv
    conv = (ext_conv[CONV_HALO - 2:CONV_HALO - 2 + tm, :] * conv_w_ref[0:1, :]
            + ext_conv[CONV_HALO - 1:CONV_HALO - 1 + tm, :] * conv_w_ref[1:2, :]
            + cv * conv_w_ref[2:3, :])
    ext_conv[0:CONV_HALO, :] = ext_conv[tm:tm + CONV_HALO, :]
    branch_b = _dot((gb * conv).astype(BF16), w_co_ref[...])
    mixed = jnp.concatenate(mixed, axis=1) * pool_scale_ref[...]
    branch_a = _dot(mixed.astype(BF16), w_po_ref[...])

    merged = (_sigmoid(pre_a + b_gate_ref[:, 0:D_MODEL]) * branch_a
              + _sigmoid(pre_b + b_gate_ref[:, D_MODEL:2 * D_MODEL]) * branch_b)
    x1 = x + _dot(merged.astype(BF16), w_o_ref[...])

    h2 = _rms(x1, g_ffn_ref[...])
    h2p_ref[...] = _pack_pair(h2[:, 0:HALF], h2[:, HALF:D_MODEL])
    h2b = h2.astype(BF16)

    nt = (((1,), (1,)), ((), ()))
    parts = lax.dot_general(wr_ref[...], h2b, nt, preferred_element_type=F32)
    logits = parts[0:N_EXPERTS, :] + parts[N_EXPERTS:2 * N_EXPERTS, :]
    sg = _dot(h2b, s_gate_ref[...])
    su = _dot(h2b, s_up_ref[...])
    scores = jax.nn.sigmoid(logits)
    sel = scores + rbias_ref[...]
    eidx = lax.broadcasted_iota(jnp.int32, (N_EXPERTS, tm), 0).astype(F32)
    e_rows, w_rows = [], []
    mask = jnp.zeros((N_EXPERTS, tm), F32)
    for _ in range(TOP_K):
        m = jnp.max(sel, axis=0, keepdims=True)
        ek = jnp.min(jnp.where(sel == m, eidx, float(N_EXPERTS)), axis=0, keepdims=True)
        oh = eidx == ek
        w_rows.append(jnp.sum(jnp.where(oh, scores, 0.0), axis=0, keepdims=True))
        e_rows.append(ek)
        mask = mask + oh.astype(F32)
        sel = jnp.where(oh, -jnp.inf, sel)

    shared = _dot((sg * _sigmoid(sg) * su).astype(BF16), s_down_ref[...])
    xres_ref[...] = x1 + shared

    wsum = w_rows[0]
    for k in range(1, TOP_K):
        wsum = wsum + w_rows[k]

    before = _dot(mask.astype(BF16), tri[...]) + cnt_carry[...]
    for k in range(TOP_K):
        oh = eidx == e_rows[k]
        rank_ref[k:k + 1, :] = jnp.sum(jnp.where(oh, before, 0.0), axis=0,
                                       keepdims=True).astype(jnp.int32)
        idx_ref[k:k + 1, :] = e_rows[k].astype(jnp.int32)
        wsel_ref[k:k + 1, :] = w_rows[k] / wsum * ROUTED_SCALE
    total = cnt_carry[...] + jnp.sum(mask, axis=1, keepdims=True)
    cnt_carry[...] = total
    counts_ref[...] = total.astype(jnp.int32)


def _mixer_router(x2d, seq_len, g_mix, w_in, b_gate, w_grp, pool_scale, w_po, conv_w,
                  w_co, w_o, g_ffn, wr, rbias, s_gate, s_up, s_down):
    t_tok = x2d.shape[0]
    tm = TM_MIX
    n_seq_tiles = seq_len // tm
    const = lambda shape: pl.BlockSpec(shape, lambda i: (0,) * len(shape),
                                       pipeline_mode=pl.Buffered(1))
    hbm = pl.BlockSpec(memory_space=pl.ANY)
    row_blk = pl.BlockSpec((tm, D_MODEL), lambda i: (i, 0))
    half_blk = pl.BlockSpec((tm, HALF), lambda i: (i, 0))
    slot_blk = pl.BlockSpec((TOP_K, tm), lambda i: (0, i))
    return pl.pallas_call(
        functools.partial(_mixer_router_kernel, n_seq_tiles),
        grid=(t_tok // tm,),
        in_specs=[row_blk, const(g_mix.shape), hbm, const(b_gate.shape),
                  hbm, const(pool_scale.shape), hbm,
                  const(conv_w.shape), hbm, hbm, const(g_ffn.shape),
                  const(wr.shape), const(rbias.shape), hbm, hbm, hbm],
        out_specs=[row_blk, half_blk, slot_blk, slot_blk, slot_blk,
                   pl.BlockSpec((N_EXPERTS, 1), lambda i: (0, 0))],
        out_shape=[jax.ShapeDtypeStruct((t_tok, D_MODEL), F32),
                   jax.ShapeDtypeStruct((t_tok, HALF), U32),
                   jax.ShapeDtypeStruct((TOP_K, t_tok), jnp.int32),
                   jax.ShapeDtypeStruct((TOP_K, t_tok), F32),
                   jax.ShapeDtypeStruct((TOP_K, t_tok), jnp.int32),
                   jax.ShapeDtypeStruct((N_EXPERTS, 1), jnp.int32)],
        scratch_shapes=[pltpu.VMEM((POOL_HALO + tm, POOL_WIDTH), F32),
                        pltpu.VMEM((CONV_HALO + tm, CONV_WIDTH), F32),
                        pltpu.VMEM((N_EXPERTS, 1), F32),
                        pltpu.VMEM((tm, tm), BF16)]
                       + [pltpu.VMEM(w.shape, BF16)
                          for w in (w_in, w_grp, w_po, w_co, w_o, s_gate, s_up, s_down)]
                       + [pltpu.VMEM((2, W_IN_CHUNK, w_in.shape[1]), F32),
                          pltpu.VMEM(w_o.shape, F32),
                          pltpu.VMEM((2,) + s_gate.shape, F32),
                          pltpu.VMEM(w_grp.shape, F32),
                          pltpu.SemaphoreType.DMA((7,))],
        compiler_params=pltpu.CompilerParams(dimension_semantics=("arbitrary",),
                                             vmem_limit_bytes=VMEM_LIMIT),
        name="mixer_router",
    )(x2d, g_mix, w_in, b_gate, w_grp, pool_scale, w_po, conv_w, w_co, w_o, g_ffn,
      wr, rbias, s_gate, s_up, s_down)


def _plan_kernel(n_blk, n_pad, counts_ref, idx_ref, rank_ref,
                 dest_ref, blk_e_ref, next_e_ref, n_used_ref, n_zero_ref, zero_rows_ref, pad_start):
    e_n = N_EXPERTS

    @pl.when(pl.program_id(0) == 0)
    def _():
        sub = lax.broadcasted_iota(jnp.int32, (e_n, e_n), 0)
        lane = lax.broadcasted_iota(jnp.int32, (e_n, e_n), 1)
        c_col = counts_ref[...]
        p_col = ((c_col + (ROW_BLOCK - 1)) // ROW_BLOCK) * ROW_BLOCK
        c_f = c_col.astype(F32)
        p_f = p_col.astype(F32)
        gap_f = p_f - c_f
        to_row = lambda col: jnp.sum(jnp.where(sub == lane, col, 0.0), axis=0, keepdims=True)
        p_row = to_row(p_f)
        gap_row = to_row(gap_f)
        pad_end_col = jnp.sum(jnp.where(lane <= sub, p_row, 0.0), axis=1, keepdims=True)
        gap_before_col = jnp.sum(jnp.where(lane < sub, gap_row, 0.0), axis=1, keepdims=True)
        pad_start[...] = pad_end_col - p_f
        pad_end_last = jnp.sum(p_row, axis=1, keepdims=True)
        n_used_ref[...] = jnp.broadcast_to(pad_end_last * (1.0 / ROW_BLOCK),
                                           n_used_ref.shape).astype(jnp.int32)
        gap_total = jnp.sum(gap_row, axis=1, keepdims=True)
        n_zero_ref[...] = jnp.broadcast_to(jnp.floor((gap_total + (SC_CHUNK - 1))
                                                     * (1.0 / SC_CHUNK)),
                                           n_zero_ref.shape).astype(jnp.int32)
        row0 = (lax.broadcasted_iota(jnp.int32, (e_n, blk_e_ref.shape[1]), 1)
                * ROW_BLOCK).astype(F32)
        owner = jnp.sum(jnp.where(pad_end_col <= row0, 1.0, 0.0), axis=0, keepdims=True)
        blk_e_ref[...] = jnp.minimum(owner, float(e_n - 1)).astype(jnp.int32)
        used_later = jnp.logical_and(sub > lane, p_f > 0.0)
        nxt = jnp.min(jnp.where(used_later, sub, e_n), axis=0, keepdims=True)
        next_e_ref[...] = jnp.full(next_e_ref.shape, -1, jnp.int32)
        next_e_ref[:, 0:e_n] = jnp.where(nxt < e_n, nxt, -1)
        for j0 in range(0, n_pad, PLAN_LANES):
            j = (j0 + lax.broadcasted_iota(jnp.int32, (e_n, PLAN_LANES), 1)).astype(F32)
            before = jnp.sum(jnp.where(gap_before_col <= j, c_f, 0.0), axis=0, keepdims=True)
            zero_rows_ref[:, j0:j0 + PLAN_LANES] = (j[0:1, :] + before).astype(jnp.int32)

    tm = idx_ref.shape[1]
    eidx = lax.broadcasted_iota(jnp.int32, (e_n, tm), 0)
    ps = pad_start[...]
    for k in range(TOP_K):
        oh = eidx == idx_ref[k:k + 1, :]
        start = jnp.sum(jnp.where(oh, ps, 0.0), axis=0, keepdims=True)
        dest_ref[k:k + 1, :] = start.astype(jnp.int32) + rank_ref[k:k + 1, :]


def _plan(counts, idx_t, rank_t, n_blk, n_pad):
    t_tok = idx_t.shape[1]
    lanes = 128
    n_blk_p = -(-n_blk // lanes) * lanes
    slot_blk = pl.BlockSpec((TOP_K, TM_DEST), lambda i: (0, i))
    whole = lambda n: pl.BlockSpec((1, n), lambda i: (0, 0))
    dest_t, blk_e, next_e, n_used, n_zero, zero_rows = pl.pallas_call(
        functools.partial(_plan_kernel, n_blk, n_pad),
        grid=(t_tok // TM_DEST,),
        in_specs=[pl.BlockSpec((N_EXPERTS, 1), lambda i: (0, 0)), slot_blk, slot_blk],
        out_specs=[slot_blk, whole(n_blk_p), whole(lanes), whole(lanes), whole(lanes),
                   whole(n_pad)],
        out_shape=[jax.ShapeDtypeStruct((TOP_K, t_tok), jnp.int32),
                   jax.ShapeDtypeStruct((1, n_blk_p), jnp.int32),
                   jax.ShapeDtypeStruct((1, lanes), jnp.int32),
                   jax.ShapeDtypeStruct((1, lanes), jnp.int32),
                   jax.ShapeDtypeStruct((1, lanes), jnp.int32),
                   jax.ShapeDtypeStruct((1, n_pad), jnp.int32)],
        scratch_shapes=[pltpu.VMEM((N_EXPERTS, 1), F32)],
        compiler_params=pltpu.CompilerParams(dimension_semantics=("arbitrary",)),
        name="plan",
    )(counts, idx_t, rank_t)
    return (dest_t, blk_e.reshape(-1), next_e.reshape(-1), n_used.reshape(-1), n_zero.reshape(-1),
            zero_rows.reshape(-1))


def _sc_mesh():
    return plsc.VectorSubcoreMesh(core_axis_name="c", subcore_axis_name="s")


def _sc_worker_id():
    return lax.axis_index("s") * SC_CORES + lax.axis_index("c")


def _dispatch(h2p, dest_t, zero_rows, n_zero, n_rows):
    t_tok, width = h2p.shape
    per_w = t_tok // SC_WORKERS
    n_chunks = per_w // SC_CHUNK
    z_chunks = zero_rows.shape[0] // (SC_WORKERS * SC_CHUNK)
    dest_w = (dest_t.reshape(TOP_K, SC_WORKERS, n_chunks, SC_CHUNK)
              .transpose(1, 2, 0, 3).reshape(SC_WORKERS * n_chunks * TOP_K, SC_CHUNK))
    zero_w = zero_rows.reshape(z_chunks * SC_WORKERS, SC_CHUNK)

    @functools.partial(
        pl.kernel, mesh=_sc_mesh(),
        out_type=jax.ShapeDtypeStruct((n_rows, width), h2p.dtype),
        scratch_types=[pltpu.VMEM((n_chunks * TOP_K, SC_CHUNK), jnp.int32),
                       pltpu.VMEM((z_chunks, SC_CHUNK), jnp.int32),
                       pltpu.VMEM((SC_LANES,), jnp.int32),
                       pltpu.VMEM((2, SC_CHUNK, width), h2p.dtype),
                       pltpu.VMEM((SC_CHUNK, width), h2p.dtype),
                       pltpu.SemaphoreType.DMA, pltpu.SemaphoreType.DMA,
                       pltpu.SemaphoreType.DMA, pltpu.SemaphoreType.DMA],
        compiler_params=pltpu.CompilerParams(needs_layout_passes=False),
        name="dispatch",
    )
    def k(h2p_hbm, dest_hbm, zidx_hbm, nz_hbm, zsrc_hbm, xs_hbm, idx_v, zidx_v, nz_v, rows_v, zero_v,
          gsem, wsem, zsem, isem):
        wid = _sc_worker_id()
        base = wid * per_w
        setup = [pltpu.make_async_copy(zidx_hbm.at[pl.ds(j * SC_WORKERS + wid, 1)],
                                       zidx_v.at[pl.ds(j, 1)], isem) for j in range(z_chunks)]
        setup += [pltpu.make_async_copy(dest_hbm.at[pl.ds(wid * n_chunks * TOP_K, n_chunks * TOP_K)],
                                        idx_v, isem),
                  pltpu.make_async_copy(nz_hbm.at[pl.ds(0, SC_LANES)], nz_v, isem),
                  pltpu.make_async_copy(zsrc_hbm, zero_v, isem)]
        for cp in setup:
            cp.start()
        for cp in setup:
            cp.wait()
        n_zero_chunks = jnp.max(nz_v[...])

        def zput(j):
            return pltpu.make_async_copy(zero_v, xs_hbm.at[zidx_v.at[j]], zsem)

        for j in range(z_chunks):
            @pl.when(j * SC_WORKERS + wid < n_zero_chunks)
            def _():
                zput(j).start()

        def get(j, slot):
            return pltpu.make_async_copy(h2p_hbm.at[pl.ds(base + j * SC_CHUNK, SC_CHUNK)],
                                         rows_v.at[slot], gsem)

        def put(j, slot, kk):
            return pltpu.make_async_copy(rows_v.at[slot], xs_hbm.at[idx_v.at[j * TOP_K + kk]], wsem)

        get(0, 0).start()

        @pl.loop(0, n_chunks, step=2)
        def _(j):
            for b in range(2):
                jj = j + b
                get(jj, b).wait()

                @pl.when(jj >= 1)
                def _():
                    for kk in range(TOP_K):
                        put(jj - 1, 1 - b, kk).wait()

                @pl.when(jj + 1 < n_chunks)
                def _():
                    get(jj + 1, 1 - b).start()
                for kk in range(TOP_K):
                    put(jj, b, kk).start()

        for kk in range(TOP_K):
            put(n_chunks - 1, (n_chunks - 1) % 2, kk).wait()
        for j in range(z_chunks):
            @pl.when(j * SC_WORKERS + wid < n_zero_chunks)
            def _():
                zput(j).wait()

    return k(h2p, dest_w, zero_w, n_zero, jnp.zeros((SC_CHUNK, width), h2p.dtype))


def _regroup_sum(ys, dest_t, wsel_t, xres):
    n_slots, t_tok = dest_t.shape
    width = ys.shape[1]
    d_out = xres.shape[1]
    grp = SC_SUM_GROUP
    r_rows = SC_SUM_ROW_RING
    r_acc = SC_SUM_ACC_RING
    per_w = t_tok // SC_WORKERS
    n_sub = per_w // grp

    @functools.partial(
        pl.kernel, mesh=_sc_mesh(),
        out_type=jax.ShapeDtypeStruct((t_tok, d_out), F32),
        scratch_types=[pltpu.VMEM((n_slots, per_w), jnp.int32),
                       pltpu.VMEM((n_slots, per_w), F32),
                       pltpu.VMEM((r_rows, n_slots * grp, width), ys.dtype),
                       pltpu.VMEM((r_acc, grp, d_out), F32),
                       pltpu.SemaphoreType.DMA((r_rows,)), pltpu.SemaphoreType.DMA((r_acc,)),
                       pltpu.SemaphoreType.DMA((r_acc,))],
        compiler_params=pltpu.CompilerParams(needs_layout_passes=False),
        name="regroup_sum",
    )
    def k(ys_hbm, idx_hbm, w_hbm, xres_hbm, out_hbm, idx_v, w_v, rows_v, acc_v, gsem, xsem, psem):
        wid = _sc_worker_id()
        base = wid * per_w
        setup = [pltpu.make_async_copy(idx_hbm.at[:, pl.ds(base, per_w)], idx_v, psem.at[0]),
                 pltpu.make_async_copy(w_hbm.at[:, pl.ds(base, per_w)], w_v, psem.at[1])]
        for cp in setup:
            cp.start()

        def gets(j, slot):
            return [pltpu.make_async_copy(ys_hbm.at[idx_v.at[kk, pl.ds(j * grp, grp)]],
                                          rows_v.at[slot, pl.ds(kk * grp, grp)], gsem.at[slot])
                    for kk in range(n_slots)]

        def xload(j, slot):
            return pltpu.make_async_copy(xres_hbm.at[pl.ds(base + j * grp, grp)], acc_v.at[slot],
                                         xsem.at[slot])

        def put(j, slot):
            return pltpu.make_async_copy(acc_v.at[slot], out_hbm.at[pl.ds(base + j * grp, grp)],
                                         psem.at[slot])

        def accumulate(j, rslot, aslot):
            @pl.loop(0, grp)
            def _(g):
                pos = jnp.full((SC_LANES,), j * grp + g, jnp.int32)
                wk = [plsc.load_gather(w_v, [jnp.full((SC_LANES,), kk, jnp.int32), pos])
                      for kk in range(n_slots)]

                @plsc.parallel_loop(0, width // SC_LANES)
                def _(v):
                    cols_a = pl.ds(v * SC_LANES, SC_LANES)
                    cols_b = pl.ds(width + v * SC_LANES, SC_LANES)
                    acc_a = acc_v[aslot, g, cols_a]
                    acc_b = acc_v[aslot, g, cols_b]
                    for kk in range(n_slots):
                        ya, yb = _unpack_pair(rows_v[rslot, kk * grp + g, cols_a])
                        acc_a = acc_a + ya * wk[kk]
                        acc_b = acc_b + yb * wk[kk]
                    acc_v[aslot, g, cols_a] = acc_a
                    acc_v[aslot, g, cols_b] = acc_b

        xload(0, 0).start()
        for cp in setup:
            cp.wait()
        for cp in gets(0, 0):
            cp.start()

        @pl.loop(0, n_sub, step=r_acc)
        def _(j):
            for b in range(r_acc):
                jj = j + b
                rs = b % r_rows
                na = (b + 1) % r_acc
                for cp in gets(jj, rs):
                    cp.wait()

                @pl.when(jj + 1 < n_sub)
                def _():
                    for cp in gets(jj + 1, (b + 1) % r_rows):
                        cp.start()
                xload(jj, b).wait()

                @pl.when(jj + 1 < n_sub)
                def _():
                    @pl.when(jj + 1 >= r_acc)
                    def _():
                        put(jj + 1 - r_acc, na).wait()
                    xload(jj + 1, na).start()
                accumulate(jj, rs, b)
                put(jj, b).start()

        for p in range(r_acc):
            put(n_sub - r_acc + p, p).wait()

    return k(ys, dest_t, wsel_t, xres)


def _experts_kernel(n_blk, blk_e_ref, next_e_ref, n_used_ref,
                    xs_hbm, eg_hbm, eu_hbm, ed_hbm, ys_hbm,
                    xbuf, ybuf, hid, stg_g, stg_u, stg_d, wg, wu, wd, xsem, ysem, wsem):
    n = n_used_ref[0]

    def ring(b):
        return jnp.bitwise_and(b, ROW_RING - 1)

    def x_copy(b):
        return pltpu.make_async_copy(xs_hbm.at[pl.ds(pl.multiple_of(b * ROW_BLOCK, ROW_BLOCK),
                                                     ROW_BLOCK)], xbuf.at[ring(b)], xsem.at[ring(b)])

    def y_copy(b):
        return pltpu.make_async_copy(ybuf.at[ring(b)],
                                     ys_hbm.at[pl.ds(pl.multiple_of(b * ROW_BLOCK, ROW_BLOCK),
                                                     ROW_BLOCK)], ysem.at[ring(b)])

    def w_copies(e):
        return (pltpu.make_async_copy(eg_hbm.at[e], stg_g, wsem.at[0]),
                pltpu.make_async_copy(eu_hbm.at[e], stg_u, wsem.at[1]),
                pltpu.make_async_copy(ed_hbm.at[e], stg_d, wsem.at[2]))

    def switch_expert(e, wslot):
        for cp in w_copies(e):
            cp.wait()
        wg[wslot] = stg_g[...].astype(BF16)
        wu[wslot] = stg_u[...].astype(BF16)
        wd[wslot] = stg_d[...].astype(BF16)
        nxt = next_e_ref[e]

        @pl.when(nxt >= 0)
        def _():
            for cp in w_copies(nxt):
                cp.start()

    def gate_up(b, wslot):
        xa, xb = _unpack_pair(xbuf[ring(b)])
        xb16 = jnp.concatenate([xa, xb], axis=1).astype(BF16)
        g = _dot(xb16, wg[wslot])
        up = _dot(xb16, wu[wslot])
        hid[jnp.bitwise_and(b, 1)] = (g * _sigmoid(g) * up).astype(BF16)

    def down(b, wslot):
        y = _dot(hid[jnp.bitwise_and(b, 1)], wd[wslot])
        ybuf[ring(b)] = _pack_pair(y[:, 0:HALF], y[:, HALF:D_MODEL])

    e0 = blk_e_ref[0]
    for cp in w_copies(e0):
        cp.start()
    for j in range(ROW_RING):
        @pl.when(j < n)
        def _():
            x_copy(j).start()
    switch_expert(e0, 0)
    x_copy(0).wait()
    gate_up(0, 0)

    def body(b, wslot_prev):
        e = blk_e_ref[b]
        first = e != blk_e_ref[b - 1]
        wslot = jnp.where(first, 1 - wslot_prev, wslot_prev)

        @pl.when(first)
        def _():
            switch_expert(e, wslot)

        x_copy(b).wait()

        @pl.when(b + ROW_RING - 1 < n)
        def _():
            x_copy(b + ROW_RING - 1).start()

        @pl.when(b >= ROW_RING + 1)
        def _():
            y_copy(b - 1 - ROW_RING).wait()

        down(b - 1, wslot_prev)
        gate_up(b, wslot)
        y_copy(b - 1).start()
        return wslot

    wslot_last = lax.fori_loop(1, n, body, jnp.int32(0))

    last = n - 1

    @pl.when(last >= ROW_RING)
    def _():
        y_copy(last - ROW_RING).wait()
    down(last, wslot_last)
    y_copy(last).start()
    for j in range(ROW_RING - 1, -1, -1):
        @pl.when(last - j >= 0)
        def _():
            y_copy(last - j).wait()

    ybuf[0] = jnp.zeros((ROW_BLOCK, HALF), U32)

    def zero_tail(b, c):
        cp = pltpu.make_async_copy(ybuf.at[0],
                                   ys_hbm.at[pl.ds(pl.multiple_of(b * ROW_BLOCK, ROW_BLOCK),
                                                   ROW_BLOCK)], ysem.at[0])
        cp.start()
        cp.wait()
        return c
    lax.fori_loop(n, n_blk, zero_tail, 0)


def _experts(blk_e, next_e, n_used, xs, e_gate, e_up, e_down):
    n_rows = xs.shape[0]
    n_blk = n_rows // ROW_BLOCK
    any_spec = pl.BlockSpec(memory_space=pl.ANY)
    grid_spec = pltpu.PrefetchScalarGridSpec(
        num_scalar_prefetch=3,
        grid=(1,),
        in_specs=[any_spec, any_spec, any_spec, any_spec],
        out_specs=any_spec,
        scratch_shapes=[pltpu.VMEM((ROW_RING, ROW_BLOCK, HALF), U32),
                        pltpu.VMEM((ROW_RING, ROW_BLOCK, HALF), U32),
                        pltpu.VMEM((2, ROW_BLOCK, EXPERT_HIDDEN), BF16),
                        pltpu.VMEM((D_MODEL, EXPERT_HIDDEN), F32),
                        pltpu.VMEM((D_MODEL, EXPERT_HIDDEN), F32),
                        pltpu.VMEM((EXPERT_HIDDEN, D_MODEL), F32),
                        pltpu.VMEM((2, D_MODEL, EXPERT_HIDDEN), BF16),
                        pltpu.VMEM((2, D_MODEL, EXPERT_HIDDEN), BF16),
                        pltpu.VMEM((2, EXPERT_HIDDEN, D_MODEL), BF16),
                        pltpu.SemaphoreType.DMA((ROW_RING,)),
                        pltpu.SemaphoreType.DMA((ROW_RING,)),
                        pltpu.SemaphoreType.DMA((3,))],
    )
    return pl.pallas_call(
        functools.partial(_experts_kernel, n_blk),
        grid_spec=grid_spec,
        out_shape=jax.ShapeDtypeStruct((n_rows, HALF), U32),
        compiler_params=pltpu.CompilerParams(dimension_semantics=("arbitrary",)),
        name="experts",
    )(blk_e, next_e, n_used, xs, e_gate, e_up, e_down)


def _final_norm_kernel(x_ref, g_ref, out_ref):
    out_ref[...] = _rms(x_ref[...], g_ref[...])


def _final_norm(xsum, g_final):
    t_tok = xsum.shape[0]
    blk = pl.BlockSpec((TM_NORM, D_MODEL), lambda i: (i, 0))
    return pl.pallas_call(
        _final_norm_kernel,
        grid=(t_tok // TM_NORM,),
        in_specs=[blk, pl.BlockSpec((1, D_MODEL), lambda i: (0, 0))],
        out_specs=blk,
        out_shape=jax.ShapeDtypeStruct((t_tok, D_MODEL), F32),
        compiler_params=pltpu.CompilerParams(dimension_semantics=("arbitrary",)),
        name="final_norm",
    )(xsum, g_final)


def kernel(x, g_mix, w_in, b_gate, w_pool_group, pool_scale, w_pool_out, conv_w, w_conv_out, w_o,
           g_ffn, w_router, router_bias, e_gate, e_up, e_down, s_gate, s_up, s_down, g_final):
    b, s, d = x.shape
    t_tok = b * s
    n_pad = N_EXPERTS * ROW_BLOCK
    n_rows = t_tok * TOP_K + n_pad
    n_blk = n_rows // ROW_BLOCK
    assert d == D_MODEL and s % TM_MIX == 0 and TM_MIX >= POOL_HALO
    assert t_tok % TM_DEST == 0 and t_tok % TM_NORM == 0 and n_pad % PLAN_LANES == 0
    assert t_tok % (2 * SC_WORKERS * SC_CHUNK) == 0 and n_pad % (SC_WORKERS * SC_CHUNK) == 0
    assert t_tok % (SC_WORKERS * SC_SUM_GROUP * SC_SUM_ACC_RING) == 0

    row = lambda a: a.reshape(1, -1)
    wr_t = w_router.T.astype(F32)
    wr_hi = wr_t.astype(BF16)
    wr = jnp.concatenate([wr_hi, (wr_t - wr_hi.astype(F32)).astype(BF16)], axis=0)

    xres, h2p, idx_t, wsel_t, rank_t, counts = _mixer_router(
        x.reshape(t_tok, d), s, row(g_mix), w_in, row(b_gate), w_pool_group,
        row(pool_scale), w_pool_out, conv_w, w_conv_out, w_o, row(g_ffn), wr,
        router_bias.astype(F32).reshape(N_EXPERTS, 1), s_gate, s_up, s_down)
    dest_t, blk_e, next_e, n_used, n_zero, zero_rows = _plan(counts, idx_t, rank_t, n_blk, n_pad)
    xs = _dispatch(h2p, dest_t, zero_rows, n_zero, n_rows)
    ys = _experts(blk_e, next_e, n_used, xs, e_gate, e_up, e_down)
    xsum = _regroup_sum(ys, dest_t, wsel_t, xres)
    return _final_norm(xsum, row(g_final)).reshape(b, s, d)
```

```python
import functools

import jax
import jax.numpy as jnp
from jax import lax
from jax.experimental import pallas as pl
from jax.experimental.pallas import tpu as pltpu
from jax.experimental.pallas import tpu_sc as plsc

D_MODEL = 1024
HALF = D_MODEL // 2
POOL_WIDTH = 512
N_POOL_GROUPS = 4
POOL_GROUP = 128
POOL_WINDOWS = (2, 4, 8, 16)
CONV_WIDTH = 512
N_EXPERTS = 64
TOP_K = 8
EXPERT_HIDDEN = 256
SHARED_HIDDEN = 256
ROUTED_SCALE = 2.5
EPS = 1e-6

POOL_HALO = 16
CONV_HALO = 8
TM_MIX = 512
W_IN_CHUNK = 128
TM_DEST = 2048
PLAN_LANES = 2048
ROW_BLOCK = 512
ROW_RING = 8
TM_NORM = 1024
N_SPLITS = 2
VMEM_LIMIT = 56 * 1024 * 1024

SC_CORES = 2
SC_SUBCORES = 16
SC_WORKERS = SC_CORES * SC_SUBCORES
SC_LANES = 16
SC_CHUNK = 64
SC_SUM_GROUP = 8
SC_SUM_ROW_RING = 2
SC_SUM_ACC_RING = 4

BF16 = jnp.bfloat16
F32 = jnp.float32
U32 = jnp.uint32


def _rms(x, g):
    r = lax.rsqrt(jnp.mean(x * x, axis=-1, keepdims=True) + EPS)
    return (x * r) * g


def _dot(a, b):
    return jnp.dot(a, b, preferred_element_type=F32)


def _sigmoid(z):
    return 0.5 * jnp.tanh(0.5 * z) + 0.5


def _pack_pair(a, b):
    ra = lax.bitcast_convert_type(a.astype(BF16).astype(F32), U32)
    rb = lax.bitcast_convert_type(b.astype(BF16).astype(F32), U32)
    return ra | (rb >> 16)


def _unpack_pair(w):
    a = lax.bitcast_convert_type(w & jnp.uint32(0xFFFF0000), F32)
    b = lax.bitcast_convert_type(w << 16, F32)
    return a, b


def _load_weights_bf16(w_in_hbm, w_grp_hbm, w_po_hbm, w_co_hbm, w_o_hbm, s_gate_hbm, s_up_hbm,
                       s_down_hbm, w_in_ref, w_grp_ref, w_po_ref, w_co_ref, w_o_ref, s_gate_ref,
                       s_up_ref, s_down_ref, stg_in, stg_sq, stg_sh, stg_grp, wsem):
    copy = lambda src, dst, k: pltpu.make_async_copy(src, dst, wsem.at[k])
    rows = stg_in.shape[1]
    n_in = w_in_hbm.shape[0] // rows
    half = w_po_hbm.shape[0]
    c_in = [copy(w_in_hbm.at[pl.ds(j * rows, rows)], stg_in.at[j % 2], j % 2) for j in range(n_in)]
    c_o = copy(w_o_hbm, stg_sq, 2)
    c_sg = copy(s_gate_hbm, stg_sh.at[0], 3)
    c_su = copy(s_up_hbm, stg_sh.at[1], 4)
    c_grp = copy(w_grp_hbm, stg_grp, 5)
    c_po = copy(w_po_hbm, stg_sq.at[0:half], 2)
    c_co = copy(w_co_hbm, stg_sq.at[half:2 * half], 6)
    c_sd = copy(s_down_hbm, stg_sq.at[0:s_down_hbm.shape[0]], 2)
    for cp in (c_in[0], c_in[1], c_o, c_sg, c_su, c_grp):
        cp.start()
    for j in range(n_in):
        c_in[j].wait()
        w_in_ref[j * rows:(j + 1) * rows, :] = stg_in[j % 2].astype(BF16)
        if j + 2 < n_in:
            c_in[j + 2].start()
    c_o.wait()
    w_o_ref[...] = stg_sq[...].astype(BF16)
    c_po.start()
    c_co.start()
    c_sg.wait()
    s_gate_ref[...] = stg_sh[0].astype(BF16)
    c_su.wait()
    s_up_ref[...] = stg_sh[1].astype(BF16)
    c_grp.wait()
    w_grp_ref[...] = stg_grp[...].astype(BF16)
    c_po.wait()
    w_po_ref[...] = stg_sq[0:half, :].astype(BF16)
    c_co.wait()
    w_co_ref[...] = stg_sq[half:2 * half, :].astype(BF16)
    c_sd.start()
    c_sd.wait()
    s_down_ref[...] = stg_sq[0:s_down_hbm.shape[0], :].astype(BF16)


def _mixer_router_kernel(n_seq_tiles,
                         x_ref, g_mix_ref, w_in_hbm, b_gate_ref, w_grp_hbm, pool_scale_ref,
                         w_po_hbm, conv_w_ref, w_co_hbm, w_o_hbm, g_ffn_ref,
                         wr_ref, rbias_ref, s_gate_hbm, s_up_hbm, s_down_hbm,
                         xres_ref, h2p_ref, idx_ref, wsel_ref, rank_ref, counts_ref,
                         ext_pool, ext_conv, cnt_carry, tri,
                         w_in_ref, w_grp_ref, w_po_ref, w_co_ref, w_o_ref, s_gate_ref, s_up_ref,
                         s_down_ref, stg_in, stg_sq, stg_sh, stg_grp, wsem):
    tm = x_ref.shape[0]
    i = pl.program_id(0)
    st = i % n_seq_tiles

    @pl.when(i == 0)
    def _():
        _load_weights_bf16(w_in_hbm, w_grp_hbm, w_po_hbm, w_co_hbm, w_o_hbm, s_gate_hbm, s_up_hbm,
                           s_down_hbm, w_in_ref, w_grp_ref, w_po_ref, w_co_ref, w_o_ref,
                           s_gate_ref, s_up_ref, s_down_ref, stg_in, stg_sq, stg_sh, stg_grp, wsem)
        r = lax.broadcasted_iota(jnp.int32, (tm, tm), 0)
        c = lax.broadcasted_iota(jnp.int32, (tm, tm), 1)
        tri[...] = (r < c).astype(BF16)
        cnt_carry[...] = jnp.zeros_like(cnt_carry)

    @pl.when(st == 0)
    def _():
        ext_pool[0:POOL_HALO, :] = jnp.zeros((POOL_HALO, POOL_WIDTH), F32)
        ext_conv[0:CONV_HALO, :] = jnp.zeros((CONV_HALO, CONV_WIDTH), F32)

    x = x_ref[...]
    hb = _rms(x, g_mix_ref[...]).astype(BF16)

    o0 = POOL_WIDTH
    o1 = o0 + CONV_WIDTH
    o2 = o1 + CONV_WIDTH
    o3 = o2 + CONV_WIDTH

    u = _dot(hb, w_in_ref[:, 0:o0])
    ext_pool[POOL_HALO:POOL_HALO + tm, :] = u
    gc = _dot(hb, w_in_ref[:, o1:o2])
    v = _dot(hb, w_in_ref[:, o2:o3])
    pre_a = _dot(hb, w_in_ref[:, o3:o3 + D_MODEL])

    t_glob = st * tm + lax.broadcasted_iota(jnp.int32, (tm, 1), 0)
    mixed = []
    for gi, w in enumerate(POOL_WINDOWS):
        cols = slice(gi * POOL_GROUP, (gi + 1) * POOL_GROUP)
        ug = u[:, cols]
        acc = ug
        for j in range(1, w):
            acc = acc + ext_pool[POOL_HALO - j:POOL_HALO - j + tm, cols]
        cnt = jnp.minimum(t_glob + 1, w).astype(F32)
        pooled = acc * (1.0 / cnt) - ug
        mixed.append(_dot(pooled.astype(BF16), w_grp_ref[gi]))
    ext_pool[0:POOL_HALO, :] = ext_pool[tm:tm + POOL_HALO, :]
    pre_b = _dot(hb, w_in_ref[:, o3 + D_MODEL:o3 + 2 * D_MODEL])
    gb = _dot(hb, w_in_ref[:, o0:o1])

    cv = gc * v
    ext_conv[CONV_HALO:CONV_HALO + tm, :] = cv
    conv = (ext_conv[CONV_HALO - 2:CONV_HALO - 2 + tm, :] * conv_w_ref[0:1, :]
            + ext_conv[CONV_HALO - 1:CONV_HALO - 1 + tm, :] * conv_w_ref[1:2, :]
            + cv * conv_w_ref[2:3, :])
    ext_conv[0:CONV_HALO, :] = ext_conv[tm:tm + CONV_HALO, :]
    branch_b = _dot((gb * conv).astype(BF16), w_co_ref[...])
    mixed = jnp.concatenate(mixed, axis=1) * pool_scale_ref[...]
    branch_a = _dot(mixed.astype(BF16), w_po_ref[...])

    merged = (_sigmoid(pre_a + b_gate_ref[:, 0:D_MODEL]) * branch_a
              + _sigmoid(pre_b + b_gate_ref[:, D_MODEL:2 * D_MODEL]) * branch_b)
    x1 = x + _dot(merged.astype(BF16), w_o_ref[...])

    h2 = _rms(x1, g_ffn_ref[...])
    h2p_ref[...] = _pack_pair(h2[:, 0:HALF], h2[:, HALF:D_MODEL])
    h2b = h2.astype(BF16)

    nt = (((1,), (1,)), ((), ()))
    parts = lax.dot_general(wr_ref[...], h2b, nt, preferred_element_type=F32)
    logits = parts[0:N_EXPERTS, :] + parts[N_EXPERTS:2 * N_EXPERTS, :]
    sg = _dot(h2b, s_gate_ref[...])
    su = _dot(h2b, s_up_ref[...])
    scores = jax.nn.sigmoid(logits)
    sel = scores + rbias_ref[...]
    eidx = lax.broadcasted_iota(jnp.int32, (N_EXPERTS, tm), 0).astype(F32)
    e_rows, w_rows = [], []
    mask = jnp.zeros((N_EXPERTS, tm), F32)
    for _ in range(TOP_K):
        m = jnp.max(sel, axis=0, keepdims=True)
        ek = jnp.min(jnp.where(sel == m, eidx, float(N_EXPERTS)), axis=0, keepdims=True)
        oh = eidx == ek
        w_rows.append(jnp.sum(jnp.where(oh, scores, 0.0), axis=0, keepdims=True))
        e_rows.append(ek)
        mask = mask + oh.astype(F32)
        sel = jnp.where(oh, -jnp.inf, sel)

    shared = _dot((sg * _sigmoid(sg) * su).astype(BF16), s_down_ref[...])
    xres_ref[...] = x1 + shared

    wsum = w_rows[0]
    for k in range(1, TOP_K):
        wsum = wsum + w_rows[k]

    before = _dot(mask.astype(BF16), tri[...]) + cnt_carry[...]
    for k in range(TOP_K):
        oh = eidx == e_rows[k]
        rank_ref[k:k + 1, :] = jnp.sum(jnp.where(oh, before, 0.0), axis=0,
                                       keepdims=True).astype(jnp.int32)
        idx_ref[k:k + 1, :] = e_rows[k].astype(jnp.int32)
        wsel_ref[k:k + 1, :] = w_rows[k] / wsum * ROUTED_SCALE
    total = cnt_carry[...] + jnp.sum(mask, axis=1, keepdims=True)
    cnt_carry[...] = total
    counts_ref[...] = total.astype(jnp.int32)


def _mixer_router(x2d, tok0, t_tok, seq_len, g_mix, w_in, b_gate, w_grp, pool_scale, w_po, conv_w,
                  w_co, w_o, g_ffn, wr, rbias, s_gate, s_up, s_down):
    tm = TM_MIX
    n_seq_tiles = seq_len // tm
    off = tok0 // tm
    x_blk = pl.BlockSpec((tm, D_MODEL), lambda i: (i + off, 0))
    const = lambda shape: pl.BlockSpec(shape, lambda i: (0,) * len(shape),
                                       pipeline_mode=pl.Buffered(1))
    hbm = pl.BlockSpec(memory_space=pl.ANY)
    row_blk = pl.BlockSpec((tm, D_MODEL), lambda i: (i, 0))
    half_blk = pl.BlockSpec((tm, HALF), lambda i: (i, 0))
    slot_blk = pl.BlockSpec((TOP_K, tm), lambda i: (0, i))
    return pl.pallas_call(
        functools.partial(_mixer_router_kernel, n_seq_tiles),
        grid=(t_tok // tm,),
        in_specs=[x_blk, const(g_mix.shape), hbm, const(b_gate.shape),
                  hbm, const(pool_scale.shape), hbm,
                  const(conv_w.shape), hbm, hbm, const(g_ffn.shape),
                  const(wr.shape), const(rbias.shape), hbm, hbm, hbm],
        out_specs=[row_blk, half_blk, slot_blk, slot_blk, slot_blk,
                   pl.BlockSpec((N_EXPERTS, 1), lambda i: (0, 0))],
        out_shape=[jax.ShapeDtypeStruct((t_tok, D_MODEL), F32),
                   jax.ShapeDtypeStruct((t_tok, HALF), U32),
                   jax.ShapeDtypeStruct((TOP_K, t_tok), jnp.int32),
                   jax.ShapeDtypeStruct((TOP_K, t_tok), F32),
                   jax.ShapeDtypeStruct((TOP_K, t_tok), jnp.int32),
                   jax.ShapeDtypeStruct((N_EXPERTS, 1), jnp.int32)],
        scratch_shapes=[pltpu.VMEM((POOL_HALO + tm, POOL_WIDTH), F32),
                        pltpu.VMEM((CONV_HALO + tm, CONV_WIDTH), F32),
                        pltpu.VMEM((N_EXPERTS, 1), F32),
                        pltpu.VMEM((tm, tm), BF16)]
                       + [pltpu.VMEM(w.shape, BF16)
                          for w in (w_in, w_grp, w_po, w_co, w_o, s_gate, s_up, s_down)]
                       + [pltpu.VMEM((2, W_IN_CHUNK, w_in.shape[1]), F32),
                          pltpu.VMEM(w_o.shape, F32),
                          pltpu.VMEM((2,) + s_gate.shape, F32),
                          pltpu.VMEM(w_grp.shape, F32),
                          pltpu.SemaphoreType.DMA((7,))],
        compiler_params=pltpu.CompilerParams(dimension_semantics=("arbitrary",),
                                             vmem_limit_bytes=VMEM_LIMIT),
        name="mixer_router",
    )(x2d, g_mix, w_in, b_gate, w_grp, pool_scale, w_po, conv_w, w_co, w_o, g_ffn,
      wr, rbias, s_gate, s_up, s_down)


def _plan_kernel(n_blk, n_pad, counts_ref, idx_ref, rank_ref,
                 dest_ref, blk_e_ref, next_e_ref, n_used_ref, n_zero_ref, zero_rows_ref, pad_start):
    e_n = N_EXPERTS

    @pl.when(pl.program_id(0) == 0)
    def _():
        sub = lax.broadcasted_iota(jnp.int32, (e_n, e_n), 0)
        lane = lax.broadcasted_iota(jnp.int32, (e_n, e_n), 1)
        c_col = counts_ref[...]
        p_col = ((c_col + (ROW_BLOCK - 1)) // ROW_BLOCK) * ROW_BLOCK
        c_f = c_col.astype(F32)
        p_f = p_col.astype(F32)
        gap_f = p_f - c_f
        to_row = lambda col: jnp.sum(jnp.where(sub == lane, col, 0.0), axis=0, keepdims=True)
        p_row = to_row(p_f)
        gap_row = to_row(gap_f)
        pad_end_col = jnp.sum(jnp.where(lane <= sub, p_row, 0.0), axis=1, keepdims=True)
        gap_before_col = jnp.sum(jnp.where(lane < sub, gap_row, 0.0), axis=1, keepdims=True)
        pad_start[...] = pad_end_col - p_f
        pad_end_last = jnp.sum(p_row, axis=1, keepdims=True)
        n_used_ref[...] = jnp.broadcast_to(pad_end_last * (1.0 / ROW_BLOCK),
                                           n_used_ref.shape).astype(jnp.int32)
        gap_total = jnp.sum(gap_row, axis=1, keepdims=True)
        n_zero_ref[...] = jnp.broadcast_to(jnp.floor((gap_total + (SC_CHUNK - 1))
                                                     * (1.0 / SC_CHUNK)),
                                           n_zero_ref.shape).astype(jnp.int32)
        row0 = (lax.broadcasted_iota(jnp.int32, (e_n, blk_e_ref.shape[1]), 1)
                * ROW_BLOCK).astype(F32)
        owner = jnp.sum(jnp.where(pad_end_col <= row0, 1.0, 0.0), axis=0, keepdims=True)
        blk_e_ref[...] = jnp.minimum(owner, float(e_n - 1)).astype(jnp.int32)
        used_later = jnp.logical_and(sub > lane, p_f > 0.0)
        nxt = jnp.min(jnp.where(used_later, sub, e_n), axis=0, keepdims=True)
        next_e_ref[...] = jnp.full(next_e_ref.shape, -1, jnp.int32)
        next_e_ref[:, 0:e_n] = jnp.where(nxt < e_n, nxt, -1)
        for j0 in range(0, n_pad, PLAN_LANES):
            j = (j0 + lax.broadcasted_iota(jnp.int32, (e_n, PLAN_LANES), 1)).astype(F32)
            before = jnp.sum(jnp.where(gap_before_col <= j, c_f, 0.0), axis=0, keepdims=True)
            zero_rows_ref[:, j0:j0 + PLAN_LANES] = (j[0:1, :] + before).astype(jnp.int32)

    tm = idx_ref.shape[1]
    eidx = lax.broadcasted_iota(jnp.int32, (e_n, tm), 0)
    ps = pad_start[...]
    for k in range(TOP_K):
        oh = eidx == idx_ref[k:k + 1, :]
        start = jnp.sum(jnp.where(oh, ps, 0.0), axis=0, keepdims=True)
        dest_ref[k:k + 1, :] = start.astype(jnp.int32) + rank_ref[k:k + 1, :]


def _plan(counts, idx_t, rank_t, n_blk, n_pad):
    t_tok = idx_t.shape[1]
    lanes = 128
    n_blk_p = -(-n_blk // lanes) * lanes
    slot_blk = pl.BlockSpec((TOP_K, TM_DEST), lambda i: (0, i))
    whole = lambda n: pl.BlockSpec((1, n), lambda i: (0, 0))
    dest_t, blk_e, next_e, n_used, n_zero, zero_rows = pl.pallas_call(
        functools.partial(_plan_kernel, n_blk, n_pad),
        grid=(t_tok // TM_DEST,),
        in_specs=[pl.BlockSpec((N_EXPERTS, 1), lambda i: (0, 0)), slot_blk, slot_blk],
        out_specs=[slot_blk, whole(n_blk_p), whole(lanes), whole(lanes), whole(lanes),
                   whole(n_pad)],
        out_shape=[jax.ShapeDtypeStruct((TOP_K, t_tok), jnp.int32),
                   jax.ShapeDtypeStruct((1, n_blk_p), jnp.int32),
                   jax.ShapeDtypeStruct((1, lanes), jnp.int32),
                   jax.ShapeDtypeStruct((1, lanes), jnp.int32),
                   jax.ShapeDtypeStruct((1, lanes), jnp.int32),
                   jax.ShapeDtypeStruct((1, n_pad), jnp.int32)],
        scratch_shapes=[pltpu.VMEM((N_EXPERTS, 1), F32)],
        compiler_params=pltpu.CompilerParams(dimension_semantics=("arbitrary",)),
        name="plan",
    )(counts, idx_t, rank_t)
    return (dest_t, blk_e.reshape(-1), next_e.reshape(-1), n_used.reshape(-1), n_zero.reshape(-1),
            zero_rows.reshape(-1))


def _sc_mesh():
    return plsc.VectorSubcoreMesh(core_axis_name="c", subcore_axis_name="s")


def _sc_worker_id():
    return lax.axis_index("s") * SC_CORES + lax.axis_index("c")


def _dispatch(h2p, dest_t, zero_rows, n_zero, n_rows):
    t_tok, width = h2p.shape
    per_w = t_tok // SC_WORKERS
    n_chunks = per_w // SC_CHUNK
    z_chunks = zero_rows.shape[0] // (SC_WORKERS * SC_CHUNK)
    dest_w = (dest_t.reshape(TOP_K, SC_WORKERS, n_chunks, SC_CHUNK)
              .transpose(1, 2, 0, 3).reshape(SC_WORKERS * n_chunks * TOP_K, SC_CHUNK))
    zero_w = zero_rows.reshape(z_chunks * SC_WORKERS, SC_CHUNK)

    @functools.partial(
        pl.kernel, mesh=_sc_mesh(),
        out_type=jax.ShapeDtypeStruct((n_rows, width), h2p.dtype),
        scratch_types=[pltpu.VMEM((n_chunks * TOP_K, SC_CHUNK), jnp.int32),
                       pltpu.VMEM((z_chunks, SC_CHUNK), jnp.int32),
                       pltpu.VMEM((SC_LANES,), jnp.int32),
                       pltpu.VMEM((2, SC_CHUNK, width), h2p.dtype),
                       pltpu.VMEM((SC_CHUNK, width), h2p.dtype),
                       pltpu.SemaphoreType.DMA, pltpu.SemaphoreType.DMA,
                       pltpu.SemaphoreType.DMA, pltpu.SemaphoreType.DMA],
        compiler_params=pltpu.CompilerParams(needs_layout_passes=False),
        name="dispatch",
    )
    def k(h2p_hbm, dest_hbm, zidx_hbm, nz_hbm, zsrc_hbm, xs_hbm, idx_v, zidx_v, nz_v, rows_v, zero_v,
          gsem, wsem, zsem, isem):
        wid = _sc_worker_id()
        base = wid * per_w
        setup = [pltpu.make_async_copy(zidx_hbm.at[pl.ds(j * SC_WORKERS + wid, 1)],
                                       zidx_v.at[pl.ds(j, 1)], isem) for j in range(z_chunks)]
        setup += [pltpu.make_async_copy(dest_hbm.at[pl.ds(wid * n_chunks * TOP_K, n_chunks * TOP_K)],
                                        idx_v, isem),
                  pltpu.make_async_copy(nz_hbm.at[pl.ds(0, SC_LANES)], nz_v, isem),
                  pltpu.make_async_copy(zsrc_hbm, zero_v, isem)]
        for cp in setup:
            cp.start()
        for cp in setup:
            cp.wait()
        n_zero_chunks = jnp.max(nz_v[...])

        def zput(j):
            return pltpu.make_async_copy(zero_v, xs_hbm.at[zidx_v.at[j]], zsem)

        for j in range(z_chunks):
            @pl.when(j * SC_WORKERS + wid < n_zero_chunks)
            def _():
                zput(j).start()

        def get(j, slot):
            return pltpu.make_async_copy(h2p_hbm.at[pl.ds(base + j * SC_CHUNK, SC_CHUNK)],
                                         rows_v.at[slot], gsem)

        def put(j, slot, kk):
            return pltpu.make_async_copy(rows_v.at[slot], xs_hbm.at[idx_v.at[j * TOP_K + kk]], wsem)

        get(0, 0).start()

        @pl.loop(0, n_chunks, step=2)
        def _(j):
            for b in range(2):
                jj = j + b
                get(jj, b).wait()

                @pl.when(jj >= 1)
                def _():
                    for kk in range(TOP_K):
                        put(jj - 1, 1 - b, kk).wait()

                @pl.when(jj + 1 < n_chunks)
                def _():
                    get(jj + 1, 1 - b).start()
                for kk in range(TOP_K):
                    put(jj, b, kk).start()

        for kk in range(TOP_K):
            put(n_chunks - 1, (n_chunks - 1) % 2, kk).wait()
        for j in range(z_chunks):
            @pl.when(j * SC_WORKERS + wid < n_zero_chunks)
            def _():
                zput(j).wait()

    return k(h2p, dest_w, zero_w, n_zero, jnp.zeros((SC_CHUNK, width), h2p.dtype))


def _regroup_sum(ys, dest_t, wsel_t, xres):
    n_slots, t_tok = dest_t.shape
    width = ys.shape[1]
    d_out = xres.shape[1]
    grp = SC_SUM_GROUP
    r_rows = SC_SUM_ROW_RING
    r_acc = SC_SUM_ACC_RING
    per_w = t_tok // SC_WORKERS
    n_sub = per_w // grp

    @functools.partial(
        pl.kernel, mesh=_sc_mesh(),
        out_type=jax.ShapeDtypeStruct((t_tok, d_out), F32),
        scratch_types=[pltpu.VMEM((n_slots, per_w), jnp.int32),
                       pltpu.VMEM((n_slots, per_w), F32),
                       pltpu.VMEM((r_rows, n_slots * grp, width), ys.dtype),
                       pltpu.VMEM((r_acc, grp, d_out), F32),
                       pltpu.SemaphoreType.DMA((r_rows,)), pltpu.SemaphoreType.DMA((r_acc,)),
                       pltpu.SemaphoreType.DMA((r_acc,))],
        compiler_params=pltpu.CompilerParams(needs_layout_passes=False),
        name="regroup_sum",
    )
    def k(ys_hbm, idx_hbm, w_hbm, xres_hbm, out_hbm, idx_v, w_v, rows_v, acc_v, gsem, xsem, psem):
        wid = _sc_worker_id()
        base = wid * per_w
        setup = [pltpu.make_async_copy(idx_hbm.at[:, pl.ds(base, per_w)], idx_v, psem.at[0]),
                 pltpu.make_async_copy(w_hbm.at[:, pl.ds(base, per_w)], w_v, psem.at[1])]
        for cp in setup:
            cp.start()

        def gets(j, slot):
            return [pltpu.make_async_copy(ys_hbm.at[idx_v.at[kk, pl.ds(j * grp, grp)]],
                                          rows_v.at[slot, pl.ds(kk * grp, grp)], gsem.at[slot])
                    for kk in range(n_slots)]

        def xload(j, slot):
            return pltpu.make_async_copy(xres_hbm.at[pl.ds(base + j * grp, grp)], acc_v.at[slot],
                                         xsem.at[slot])

        def put(j, slot):
            return pltpu.make_async_copy(acc_v.at[slot], out_hbm.at[pl.ds(base + j * grp, grp)],
                                         psem.at[slot])

        def accumulate(j, rslot, aslot):
            @pl.loop(0, grp)
            def _(g):
                pos = jnp.full((SC_LANES,), j * grp + g, jnp.int32)
                wk = [plsc.load_gather(w_v, [jnp.full((SC_LANES,), kk, jnp.int32), pos])
                      for kk in range(n_slots)]

                @plsc.parallel_loop(0, width // SC_LANES)
                def _(v):
                    cols_a = pl.ds(v * SC_LANES, SC_LANES)
                    cols_b = pl.ds(width + v * SC_LANES, SC_LANES)
                    acc_a = acc_v[aslot, g, cols_a]
                    acc_b = acc_v[aslot, g, cols_b]
                    for kk in range(n_slots):
                        ya, yb = _unpack_pair(rows_v[rslot, kk * grp + g, cols_a])
                        acc_a = acc_a + ya * wk[kk]
                        acc_b = acc_b + yb * wk[kk]
                    acc_v[aslot, g, cols_a] = acc_a
                    acc_v[aslot, g, cols_b] = acc_b

        xload(0, 0).start()
        for cp in setup:
            cp.wait()
        for cp in gets(0, 0):
            cp.start()

        @pl.loop(0, n_sub, step=r_acc)
        def _(j):
            for b in range(r_acc):
                jj = j + b
                rs = b % r_rows
                na = (b + 1) % r_acc
                for cp in gets(jj, rs):
                    cp.wait()

                @pl.when(jj + 1 < n_sub)
                def _():
                    for cp in gets(jj + 1, (b + 1) % r_rows):
                        cp.start()
                xload(jj, b).wait()

                @pl.when(jj + 1 < n_sub)
                def _():
                    @pl.when(jj + 1 >= r_acc)
                    def _():
                        put(jj + 1 - r_acc, na).wait()
                    xload(jj + 1, na).start()
                accumulate(jj, rs, b)
                put(jj, b).start()

        for p in range(r_acc):
            put(n_sub - r_acc + p, p).wait()

    return k(ys, dest_t, wsel_t, xres)


def _experts_kernel(n_blk, blk_e_ref, next_e_ref, n_used_ref,
                    xs_hbm, eg_hbm, eu_hbm, ed_hbm, ys_hbm,
                    xbuf, ybuf, hid, stg_g, stg_u, stg_d, wg, wu, wd, xsem, ysem, wsem):
    n = n_used_ref[0]

    def ring(b):
        return jnp.bitwise_and(b, ROW_RING - 1)

    def x_copy(b):
        return pltpu.make_async_copy(xs_hbm.at[pl.ds(pl.multiple_of(b * ROW_BLOCK, ROW_BLOCK),
                                                     ROW_BLOCK)], xbuf.at[ring(b)], xsem.at[ring(b)])

    def y_copy(b):
        return pltpu.make_async_copy(ybuf.at[ring(b)],
                                     ys_hbm.at[pl.ds(pl.multiple_of(b * ROW_BLOCK, ROW_BLOCK),
                                                     ROW_BLOCK)], ysem.at[ring(b)])

    def w_copies(e):
        return (pltpu.make_async_copy(eg_hbm.at[e], stg_g, wsem.at[0]),
                pltpu.make_async_copy(eu_hbm.at[e], stg_u, wsem.at[1]),
                pltpu.make_async_copy(ed_hbm.at[e], stg_d, wsem.at[2]))

    def switch_expert(e, wslot):
        for cp in w_copies(e):
            cp.wait()
        wg[wslot] = stg_g[...].astype(BF16)
        wu[wslot] = stg_u[...].astype(BF16)
        wd[wslot] = stg_d[...].astype(BF16)
        nxt = next_e_ref[e]

        @pl.when(nxt >= 0)
        def _():
            for cp in w_copies(nxt):
                cp.start(priority=1)

    def gate_up(b, wslot):
        xa, xb = _unpack_pair(xbuf[ring(b)])
        xb16 = jnp.concatenate([xa, xb], axis=1).astype(BF16)
        g = _dot(xb16, wg[wslot])
        up = _dot(xb16, wu[wslot])
        hid[jnp.bitwise_and(b, 1)] = (g * _sigmoid(g) * up).astype(BF16)

    def down(b, wslot):
        y = _dot(hid[jnp.bitwise_and(b, 1)], wd[wslot])
        ybuf[ring(b)] = _pack_pair(y[:, 0:HALF], y[:, HALF:D_MODEL])

    e0 = blk_e_ref[0]
    for cp in w_copies(e0):
        cp.start()
    for j in range(ROW_RING):
        @pl.when(j < n)
        def _():
            x_copy(j).start(priority=1)
    switch_expert(e0, 0)
    x_copy(0).wait()
    gate_up(0, 0)

    def body(b, wslot_prev):
        e = blk_e_ref[b]
        first = e != blk_e_ref[b - 1]
        wslot = jnp.where(first, 1 - wslot_prev, wslot_prev)

        @pl.when(first)
        def _():
            switch_expert(e, wslot)

        x_copy(b).wait()

        @pl.when(b + ROW_RING - 1 < n)
        def _():
            x_copy(b + ROW_RING - 1).start(priority=1)

        @pl.when(b >= ROW_RING + 1)
        def _():
            y_copy(b - 1 - ROW_RING).wait()

        down(b - 1, wslot_prev)
        gate_up(b, wslot)
        y_copy(b - 1).start(priority=1)
        return wslot

    wslot_last = lax.fori_loop(1, n, body, jnp.int32(0))

    last = n - 1

    @pl.when(last >= ROW_RING)
    def _():
        y_copy(last - ROW_RING).wait()
    down(last, wslot_last)
    y_copy(last).start(priority=1)
    for j in range(ROW_RING - 1, -1, -1):
        @pl.when(last - j >= 0)
        def _():
            y_copy(last - j).wait()

    ybuf[0] = jnp.zeros((ROW_BLOCK, HALF), U32)

    def zero_tail(b, c):
        cp = pltpu.make_async_copy(ybuf.at[0],
                                   ys_hbm.at[pl.ds(pl.multiple_of(b * ROW_BLOCK, ROW_BLOCK),
                                                   ROW_BLOCK)], ysem.at[0])
        cp.start()
        cp.wait()
        return c
    lax.fori_loop(n, n_blk, zero_tail, 0)


def _experts(blk_e, next_e, n_used, xs, e_gate, e_up, e_down):
    n_rows = xs.shape[0]
    n_blk = n_rows // ROW_BLOCK
    any_spec = pl.BlockSpec(memory_space=pl.ANY)
    grid_spec = pltpu.PrefetchScalarGridSpec(
        num_scalar_prefetch=3,
        grid=(1,),
        in_specs=[any_spec, any_spec, any_spec, any_spec],
        out_specs=any_spec,
        scratch_shapes=[pltpu.VMEM((ROW_RING, ROW_BLOCK, HALF), U32),
                        pltpu.VMEM((ROW_RING, ROW_BLOCK, HALF), U32),
                        pltpu.VMEM((2, ROW_BLOCK, EXPERT_HIDDEN), BF16),
                        pltpu.VMEM((D_MODEL, EXPERT_HIDDEN), F32),
                        pltpu.VMEM((D_MODEL, EXPERT_HIDDEN), F32),
                        pltpu.VMEM((EXPERT_HIDDEN, D_MODEL), F32),
                        pltpu.VMEM((2, D_MODEL, EXPERT_HIDDEN), BF16),
                        pltpu.VMEM((2, D_MODEL, EXPERT_HIDDEN), BF16),
                        pltpu.VMEM((2, EXPERT_HIDDEN, D_MODEL), BF16),
                        pltpu.SemaphoreType.DMA((ROW_RING,)),
                        pltpu.SemaphoreType.DMA((ROW_RING,)),
                        pltpu.SemaphoreType.DMA((3,))],
    )
    return pl.pallas_call(
        functools.partial(_experts_kernel, n_blk),
        grid_spec=grid_spec,
        out_shape=jax.ShapeDtypeStruct((n_rows, HALF), U32),
        compiler_params=pltpu.CompilerParams(dimension_semantics=("arbitrary",)),
        name="experts",
    )(blk_e, next_e, n_used, xs, e_gate, e_up, e_down)


def _final_norm_kernel(x_ref, g_ref, *rest):
    out_ref = rest[-1]
    out_ref[...] = _rms(x_ref[...], g_ref[...])


def _final_norm(xsum, g_final, out_prev, tok0, t_out):
    off = tok0 // TM_NORM
    in_specs = [pl.BlockSpec((TM_NORM, D_MODEL), lambda i: (i, 0)),
                pl.BlockSpec((1, D_MODEL), lambda i: (0, 0))]
    args = [xsum, g_final]
    aliases = {}
    if out_prev is not None:
        in_specs.append(pl.BlockSpec(memory_space=pl.ANY))
        args.append(out_prev)
        aliases = {2: 0}
    return pl.pallas_call(
        _final_norm_kernel,
        grid=(xsum.shape[0] // TM_NORM,),
        in_specs=in_specs,
        out_specs=pl.BlockSpec((TM_NORM, D_MODEL), lambda i: (i + off, 0)),
        out_shape=jax.ShapeDtypeStruct((t_out, D_MODEL), F32),
        input_output_aliases=aliases,
        compiler_params=pltpu.CompilerParams(dimension_semantics=("arbitrary",)),
        name="final_norm",
    )(*args)


def kernel(x, g_mix, w_in, b_gate, w_pool_group, pool_scale, w_pool_out, conv_w, w_conv_out, w_o,
           g_ffn, w_router, router_bias, e_gate, e_up, e_down, s_gate, s_up, s_down, g_final):
    b, s, d = x.shape
    t_tok = b * s
    t_split = t_tok // N_SPLITS
    n_pad = N_EXPERTS * ROW_BLOCK
    n_rows = t_split * TOP_K + n_pad
    n_blk = n_rows // ROW_BLOCK
    assert d == D_MODEL and s % TM_MIX == 0 and TM_MIX >= POOL_HALO and b % N_SPLITS == 0
    assert t_split % TM_DEST == 0 and t_split % TM_NORM == 0 and n_pad % PLAN_LANES == 0
    assert t_split % (2 * SC_WORKERS * SC_CHUNK) == 0 and n_pad % (SC_WORKERS * SC_CHUNK) == 0
    assert t_split % (SC_WORKERS * SC_SUM_GROUP * SC_SUM_ACC_RING) == 0

    row = lambda a: a.reshape(1, -1)
    wr_t = w_router.T.astype(F32)
    wr_hi = wr_t.astype(BF16)
    wr = jnp.concatenate([wr_hi, (wr_t - wr_hi.astype(F32)).astype(BF16)], axis=0)
    x2d = x.reshape(t_tok, d)

    out = None
    for sp in range(N_SPLITS):
        tok0 = sp * t_split
        xres, h2p, idx_t, wsel_t, rank_t, counts = _mixer_router(
            x2d, tok0, t_split, s, row(g_mix), w_in, row(b_gate), w_pool_group,
            row(pool_scale), w_pool_out, conv_w, w_conv_out, w_o, row(g_ffn), wr,
            router_bias.astype(F32).reshape(N_EXPERTS, 1), s_gate, s_up, s_down)
        dest_t, blk_e, next_e, n_used, n_zero, zero_rows = _plan(counts, idx_t, rank_t, n_blk, n_pad)
        xs = _dispatch(h2p, dest_t, zero_rows, n_zero, n_rows)
        ys = _experts(blk_e, next_e, n_used, xs, e_gate, e_up, e_down)
        xsum = _regroup_sum(ys, dest_t, wsel_t, xres)
        out = _final_norm(xsum, row(g_final), out, tok0, t_tok)
    return out.reshape(b, s, d)
```

```python
import functools

import jax
import jax.numpy as jnp
from jax import lax
from jax.experimental import pallas as pl
from jax.experimental.pallas import tpu as pltpu
from jax.experimental.pallas import tpu_sc as plsc

D_MODEL = 1024
HALF = D_MODEL // 2
POOL_WIDTH = 512
N_POOL_GROUPS = 4
POOL_GROUP = 128
POOL_WINDOWS = (2, 4, 8, 16)
CONV_WIDTH = 512
N_EXPERTS = 64
TOP_K = 8
EXPERT_HIDDEN = 256
SHARED_HIDDEN = 256
ROUTED_SCALE = 2.5
EPS = 1e-6

POOL_HALO = 16
CONV_HALO = 8
TM_MIX = 512
W_IN_CHUNK = 128
TM_DEST = 2048
PLAN_LANES = 2048
ROW_BLOCK = 512
ROW_RING = 8
TM_NORM = 1024
VMEM_LIMIT = 56 * 1024 * 1024

SC_CORES = 2
SC_SUBCORES = 16
SC_WORKERS = SC_CORES * SC_SUBCORES
SC_LANES = 16
SC_CHUNK = 64
SC_SUM_GROUP = 8
SC_SUM_ROW_RING = 2
SC_SUM_ACC_RING = 4

BF16 = jnp.bfloat16
F32 = jnp.float32
U32 = jnp.uint32


def _rms(x, g):
    r = lax.rsqrt(jnp.mean(x * x, axis=-1, keepdims=True) + EPS)
    return (x * r) * g


def _dot(a, b):
    return jnp.dot(a, b, preferred_element_type=F32)


def _sigmoid(z):
    return 0.5 * jnp.tanh(0.5 * z) + 0.5


def _pack_pair(a, b):
    ra = lax.bitcast_convert_type(a.astype(BF16).astype(F32), U32)
    rb = lax.bitcast_convert_type(b.astype(BF16).astype(F32), U32)
    return ra | (rb >> 16)


def _unpack_pair(w):
    a = lax.bitcast_convert_type(w & jnp.uint32(0xFFFF0000), F32)
    b = lax.bitcast_convert_type(w << 16, F32)
    return a, b


def _load_weights_bf16(w_in_hbm, w_grp_hbm, w_po_hbm, w_co_hbm, w_o_hbm, s_gate_hbm, s_up_hbm,
                       s_down_hbm, w_in_ref, w_grp_ref, w_po_ref, w_co_ref, w_o_ref, s_gate_ref,
                       s_up_ref, s_down_ref, stg_in, stg_sq, stg_sh, stg_grp, wsem):
    copy = lambda src, dst, k: pltpu.make_async_copy(src, dst, wsem.at[k])
    rows = stg_in.shape[1]
    n_in = w_in_hbm.shape[0] // rows
    half = w_po_hbm.shape[0]
    c_in = [copy(w_in_hbm.at[pl.ds(j * rows, rows)], stg_in.at[j % 2], j % 2) for j in range(n_in)]
    c_o = copy(w_o_hbm, stg_sq, 2)
    c_sg = copy(s_gate_hbm, stg_sh.at[0], 3)
    c_su = copy(s_up_hbm, stg_sh.at[1], 4)
    c_grp = copy(w_grp_hbm, stg_grp, 5)
    c_po = copy(w_po_hbm, stg_sq.at[0:half], 2)
    c_co = copy(w_co_hbm, stg_sq.at[half:2 * half], 6)
    c_sd = copy(s_down_hbm, stg_sq.at[0:s_down_hbm.shape[0]], 2)
    for cp in (c_in[0], c_in[1], c_o, c_sg, c_su, c_grp):
        cp.start()
    for j in range(n_in):
        c_in[j].wait()
        w_in_ref[j * rows:(j + 1) * rows, :] = stg_in[j % 2].astype(BF16)
        if j + 2 < n_in:
            c_in[j + 2].start()
    c_o.wait()
    w_o_ref[...] = stg_sq[...].astype(BF16)
    c_po.start()
    c_co.start()
    c_sg.wait()
    s_gate_ref[...] = stg_sh[0].astype(BF16)
    c_su.wait()
    s_up_ref[...] = stg_sh[1].astype(BF16)
    c_grp.wait()
    w_grp_ref[...] = stg_grp[...].astype(BF16)
    c_po.wait()
    w_po_ref[...] = stg_sq[0:half, :].astype(BF16)
    c_co.wait()
    w_co_ref[...] = stg_sq[half:2 * half, :].astype(BF16)
    c_sd.start()
    c_sd.wait()
    s_down_ref[...] = stg_sq[0:s_down_hbm.shape[0], :].astype(BF16)


def _mixer_router_kernel(n_seq_tiles,
                         x_ref, g_mix_ref, w_in_hbm, b_gate_ref, w_grp_hbm, pool_scale_ref,
                         w_po_hbm, conv_w_ref, w_co_hbm, w_o_hbm, g_ffn_ref,
                         wr_ref, rbias_ref, s_gate_hbm, s_up_hbm, s_down_hbm,
                         xres_ref, h2p_ref, idx_ref, wsel_ref, rank_ref, counts_ref,
                         ext_pool, ext_conv, cnt_carry, tri,
                         w_in_ref, w_grp_ref, w_po_ref, w_co_ref, w_o_ref, s_gate_ref, s_up_ref,
                         s_down_ref, stg_in, stg_sq, stg_sh, stg_grp, wsem):
    tm = x_ref.shape[0]
    i = pl.program_id(0)
    st = i % n_seq_tiles

    @pl.when(i == 0)
    def _():
        _load_weights_bf16(w_in_hbm, w_grp_hbm, w_po_hbm, w_co_hbm, w_o_hbm, s_gate_hbm, s_up_hbm,
                           s_down_hbm, w_in_ref, w_grp_ref, w_po_ref, w_co_ref, w_o_ref,
                           s_gate_ref, s_up_ref, s_down_ref, stg_in, stg_sq, stg_sh, stg_grp, wsem)
        r = lax.broadcasted_iota(jnp.int32, (tm, tm), 0)
        c = lax.broadcasted_iota(jnp.int32, (tm, tm), 1)
        tri[...] = (r < c).astype(BF16)
        cnt_carry[...] = jnp.zeros_like(cnt_carry)

    @pl.when(st == 0)
    def _():
        ext_pool[0:POOL_HALO, :] = jnp.zeros((POOL_HALO, POOL_WIDTH), F32)
        ext_conv[0:CONV_HALO, :] = jnp.zeros((CONV_HALO, CONV_WIDTH), F32)

    x = x_ref[...]
    hb = _rms(x, g_mix_ref[...]).astype(BF16)

    o0 = POOL_WIDTH
    o1 = o0 + CONV_WIDTH
    o2 = o1 + CONV_WIDTH
    o3 = o2 + CONV_WIDTH

    u = _dot(hb, w_in_ref[:, 0:o0])
    ext_pool[POOL_HALO:POOL_HALO + tm, :] = u
    gc = _dot(hb, w_in_ref[:, o1:o2])
    v = _dot(hb, w_in_ref[:, o2:o3])
    pre_a = _dot(hb, w_in_ref[:, o3:o3 + D_MODEL])

    t_glob = st * tm + lax.broadcasted_iota(jnp.int32, (tm, 1), 0)
    mixed = []
    for gi, w in enumerate(POOL_WINDOWS):
        cols = slice(gi * POOL_GROUP, (gi + 1) * POOL_GROUP)
        ug = u[:, cols]
        acc = ug
        for j in range(1, w):
            acc = acc + ext_pool[POOL_HALO - j:POOL_HALO - j + tm, cols]
        cnt = jnp.minimum(t_glob + 1, w).astype(F32)
        pooled = acc * (1.0 / cnt) - ug
        mixed.append(_dot(pooled.astype(BF16), w_grp_ref[gi]))
    ext_pool[0:POOL_HALO, :] = ext_pool[tm:tm + POOL_HALO, :]
    pre_b = _dot(hb, w_in_ref[:, o3 + D_MODEL:o3 + 2 * D_MODEL])
    gb = _dot(hb, w_in_ref[:, o0:o1])

    cv = gc * v
    ext_conv[CONV_HALO:CONV_HALO + tm, :] = cv
    conv = (ext_conv[CONV_HALO - 2:CONV_HALO - 2 + tm, :] * conv_w_ref[0:1, :]
            + ext_conv[CONV_HALO - 1:CONV_HALO - 1 + tm, :] * conv_w_ref[1:2, :]
            + cv * conv_w_ref[2:3, :])
    ext_conv[0:CONV_HALO, :] = ext_conv[tm:tm + CONV_HALO, :]
    branch_b = _dot((gb * conv).astype(BF16), w_co_ref[...])
    mixed = jnp.concatenate(mixed, axis=1) * pool_scale_ref[...]
    branch_a = _dot(mixed.astype(BF16), w_po_ref[...])

    merged = (_sigmoid(pre_a + b_gate_ref[:, 0:D_MODEL]) * branch_a
              + _sigmoid(pre_b + b_gate_ref[:, D_MODEL:2 * D_MODEL]) * branch_b)
    x1 = x + _dot(merged.astype(BF16), w_o_ref[...])

    h2 = _rms(x1, g_ffn_ref[...])
    h2p_ref[...] = _pack_pair(h2[:, 0:HALF], h2[:, HALF:D_MODEL])
    h2b = h2.astype(BF16)

    nt = (((1,), (1,)), ((), ()))
    parts = lax.dot_general(wr_ref[...], h2b, nt, preferred_element_type=F32)
    logits = parts[0:N_EXPERTS, :] + parts[N_EXPERTS:2 * N_EXPERTS, :]
    sg = _dot(h2b, s_gate_ref[...])
    su = _dot(h2b, s_up_ref[...])
    scores = jax.nn.sigmoid(logits)
    sel = scores + rbias_ref[...]
    eidx = lax.broadcasted_iota(jnp.int32, (N_EXPERTS, tm), 0).astype(F32)
    e_rows, w_rows = [], []
    mask = jnp.zeros((N_EXPERTS, tm), F32)
    for _ in range(TOP_K):
        m = jnp.max(sel, axis=0, keepdims=True)
        ek = jnp.min(jnp.where(sel == m, eidx, float(N_EXPERTS)), axis=0, keepdims=True)
        oh = eidx == ek
        w_rows.append(jnp.sum(jnp.where(oh, scores, 0.0), axis=0, keepdims=True))
        e_rows.append(ek)
        mask = mask + oh.astype(F32)
        sel = jnp.where(oh, -jnp.inf, sel)

    shared = _dot((sg * _sigmoid(sg) * su).astype(BF16), s_down_ref[...])
    xres_ref[...] = x1 + shared

    wsum = w_rows[0]
    for k in range(1, TOP_K):
        wsum = wsum + w_rows[k]

    before = _dot(mask.astype(BF16), tri[...]) + cnt_carry[...]
    for k in range(TOP_K):
        oh = eidx == e_rows[k]
        rank_ref[k:k + 1, :] = jnp.sum(jnp.where(oh, before, 0.0), axis=0,
                                       keepdims=True).astype(jnp.int32)
        idx_ref[k:k + 1, :] = e_rows[k].astype(jnp.int32)
        wsel_ref[k:k + 1, :] = w_rows[k] / wsum * ROUTED_SCALE
    total = cnt_carry[...] + jnp.sum(mask, axis=1, keepdims=True)
    cnt_carry[...] = total
    counts_ref[...] = total.astype(jnp.int32)


def _mixer_router(x2d, seq_len, g_mix, w_in, b_gate, w_grp, pool_scale, w_po, conv_w,
                  w_co, w_o, g_ffn, wr, rbias, s_gate, s_up, s_down):
    t_tok = x2d.shape[0]
    tm = TM_MIX
    n_seq_tiles = seq_len // tm
    const = lambda shape: pl.BlockSpec(shape, lambda i: (0,) * len(shape),
                                       pipeline_mode=pl.Buffered(1))
    hbm = pl.BlockSpec(memory_space=pl.ANY)
    row_blk = pl.BlockSpec((tm, D_MODEL), lambda i: (i, 0))
    half_blk = pl.BlockSpec((tm, HALF), lambda i: (i, 0))
    slot_blk = pl.BlockSpec((TOP_K, tm), lambda i: (0, i))
    return pl.pallas_call(
        functools.partial(_mixer_router_kernel, n_seq_tiles),
        grid=(t_tok // tm,),
        in_specs=[row_blk, const(g_mix.shape), hbm, const(b_gate.shape),
                  hbm, const(pool_scale.shape), hbm,
                  const(conv_w.shape), hbm, hbm, const(g_ffn.shape),
                  const(wr.shape), const(rbias.shape), hbm, hbm, hbm],
        out_specs=[row_blk, half_blk, slot_blk, slot_blk, slot_blk,
                   pl.BlockSpec((N_EXPERTS, 1), lambda i: (0, 0))],
        out_shape=[jax.ShapeDtypeStruct((t_tok, D_MODEL), F32),
                   jax.ShapeDtypeStruct((t_tok, HALF), U32),
                   jax.ShapeDtypeStruct((TOP_K, t_tok), jnp.int32),
                   jax.ShapeDtypeStruct((TOP_K, t_tok), F32),
                   jax.ShapeDtypeStruct((TOP_K, t_tok), jnp.int32),
                   jax.ShapeDtypeStruct((N_EXPERTS, 1), jnp.int32)],
        scratch_shapes=[pltpu.VMEM((POOL_HALO + tm, POOL_WIDTH), F32),
                        pltpu.VMEM((CONV_HALO + tm, CONV_WIDTH), F32),
                        pltpu.VMEM((N_EXPERTS, 1), F32),
                        pltpu.VMEM((tm, tm), BF16)]
                       + [pltpu.VMEM(w.shape, BF16)
                          for w in (w_in, w_grp, w_po, w_co, w_o, s_gate, s_up, s_down)]
                       + [pltpu.VMEM((2, W_IN_CHUNK, w_in.shape[1]), F32),
                          pltpu.VMEM(w_o.shape, F32),
                          pltpu.VMEM((2,) + s_gate.shape, F32),
                          pltpu.VMEM(w_grp.shape, F32),
                          pltpu.SemaphoreType.DMA((7,))],
        compiler_params=pltpu.CompilerParams(dimension_semantics=("arbitrary",),
                                             vmem_limit_bytes=VMEM_LIMIT),
        name="mixer_router",
    )(x2d, g_mix, w_in, b_gate, w_grp, pool_scale, w_po, conv_w, w_co, w_o, g_ffn,
      wr, rbias, s_gate, s_up, s_down)


def _plan_kernel(n_blk, n_pad, counts_ref, idx_ref, rank_ref,
                 dest_ref, blk_e_ref, next_e_ref, n_used_ref, n_zero_ref, zero_rows_ref, pad_start):
    e_n = N_EXPERTS

    @pl.when(pl.program_id(0) == 0)
    def _():
        sub = lax.broadcasted_iota(jnp.int32, (e_n, e_n), 0)
        lane = lax.broadcasted_iota(jnp.int32, (e_n, e_n), 1)
        c_col = counts_ref[...]
        p_col = ((c_col + (ROW_BLOCK - 1)) // ROW_BLOCK) * ROW_BLOCK
        c_f = c_col.astype(F32)
        p_f = p_col.astype(F32)
        gap_f = p_f - c_f
        to_row = lambda col: jnp.sum(jnp.where(sub == lane, col, 0.0), axis=0, keepdims=True)
        p_row = to_row(p_f)
        gap_row = to_row(gap_f)
        pad_end_col = jnp.sum(jnp.where(lane <= sub, p_row, 0.0), axis=1, keepdims=True)
        gap_before_col = jnp.sum(jnp.where(lane < sub, gap_row, 0.0), axis=1, keepdims=True)
        pad_start[...] = pad_end_col - p_f
        pad_end_last = jnp.sum(p_row, axis=1, keepdims=True)
        n_used_ref[...] = jnp.broadcast_to(pad_end_last * (1.0 / ROW_BLOCK),
                                           n_used_ref.shape).astype(jnp.int32)
        gap_total = jnp.sum(gap_row, axis=1, keepdims=True)
        n_zero_ref[...] = jnp.broadcast_to(jnp.floor((gap_total + (SC_CHUNK - 1))
                                                     * (1.0 / SC_CHUNK)),
                                           n_zero_ref.shape).astype(jnp.int32)
        row0 = (lax.broadcasted_iota(jnp.int32, (e_n, blk_e_ref.shape[1]), 1)
                * ROW_BLOCK).astype(F32)
        owner = jnp.sum(jnp.where(pad_end_col <= row0, 1.0, 0.0), axis=0, keepdims=True)
        blk_e_ref[...] = jnp.minimum(owner, float(e_n - 1)).astype(jnp.int32)
        used_later = jnp.logical_and(sub > lane, p_f > 0.0)
        nxt = jnp.min(jnp.where(used_later, sub, e_n), axis=0, keepdims=True)
        next_e_ref[...] = jnp.full(next_e_ref.shape, -1, jnp.int32)
        next_e_ref[:, 0:e_n] = jnp.where(nxt < e_n, nxt, -1)
        for j0 in range(0, n_pad, PLAN_LANES):
            j = (j0 + lax.broadcasted_iota(jnp.int32, (e_n, PLAN_LANES), 1)).astype(F32)
            before = jnp.sum(jnp.where(gap_before_col <= j, c_f, 0.0), axis=0, keepdims=True)
            zero_rows_ref[:, j0:j0 + PLAN_LANES] = (j[0:1, :] + before).astype(jnp.int32)

    tm = idx_ref.shape[1]
    eidx = lax.broadcasted_iota(jnp.int32, (e_n, tm), 0)
    ps = pad_start[...]
    for k in range(TOP_K):
        oh = eidx == idx_ref[k:k + 1, :]
        start = jnp.sum(jnp.where(oh, ps, 0.0), axis=0, keepdims=True)
        dest_ref[k:k + 1, :] = start.astype(jnp.int32) + rank_ref[k:k + 1, :]


def _plan(counts, idx_t, rank_t, n_blk, n_pad):
    t_tok = idx_t.shape[1]
    lanes = 128
    n_blk_p = -(-n_blk // lanes) * lanes
    slot_blk = pl.BlockSpec((TOP_K, TM_DEST), lambda i: (0, i))
    whole = lambda n: pl.BlockSpec((1, n), lambda i: (0, 0))
    dest_t, blk_e, next_e, n_used, n_zero, zero_rows = pl.pallas_call(
        functools.partial(_plan_kernel, n_blk, n_pad),
        grid=(t_tok // TM_DEST,),
        in_specs=[pl.BlockSpec((N_EXPERTS, 1), lambda i: (0, 0)), slot_blk, slot_blk],
        out_specs=[slot_blk, whole(n_blk_p), whole(lanes), whole(lanes), whole(lanes),
                   whole(n_pad)],
        out_shape=[jax.ShapeDtypeStruct((TOP_K, t_tok), jnp.int32),
                   jax.ShapeDtypeStruct((1, n_blk_p), jnp.int32),
                   jax.ShapeDtypeStruct((1, lanes), jnp.int32),
                   jax.ShapeDtypeStruct((1, lanes), jnp.int32),
                   jax.ShapeDtypeStruct((1, lanes), jnp.int32),
                   jax.ShapeDtypeStruct((1, n_pad), jnp.int32)],
        scratch_shapes=[pltpu.VMEM((N_EXPERTS, 1), F32)],
        compiler_params=pltpu.CompilerParams(dimension_semantics=("arbitrary",)),
        name="plan",
    )(counts, idx_t, rank_t)
    return (dest_t, blk_e.reshape(-1), next_e.reshape(-1), n_used.reshape(-1), n_zero.reshape(-1),
            zero_rows.reshape(-1))


def _sc_mesh():
    return plsc.VectorSubcoreMesh(core_axis_name="c", subcore_axis_name="s")


def _sc_worker_id():
    return lax.axis_index("s") * SC_CORES + lax.axis_index("c")


def _dispatch(h2p, dest_t, zero_rows, n_zero, n_rows):
    t_tok, width = h2p.shape
    per_w = t_tok // SC_WORKERS
    n_chunks = per_w // SC_CHUNK
    z_chunks = zero_rows.shape[0] // (SC_WORKERS * SC_CHUNK)
    dest_w = (dest_t.reshape(TOP_K, SC_WORKERS, n_chunks, SC_CHUNK)
              .transpose(1, 2, 0, 3).reshape(SC_WORKERS * n_chunks * TOP_K, SC_CHUNK))
    zero_w = zero_rows.reshape(z_chunks * SC_WORKERS, SC_CHUNK)

    @functools.partial(
        pl.kernel, mesh=_sc_mesh(),
        out_type=jax.ShapeDtypeStruct((n_rows, width), h2p.dtype),
        scratch_types=[pltpu.VMEM((n_chunks * TOP_K, SC_CHUNK), jnp.int32),
                       pltpu.VMEM((z_chunks, SC_CHUNK), jnp.int32),
                       pltpu.VMEM((SC_LANES,), jnp.int32),
                       pltpu.VMEM((2, SC_CHUNK, width), h2p.dtype),
                       pltpu.VMEM((SC_CHUNK, width), h2p.dtype),
                       pltpu.SemaphoreType.DMA, pltpu.SemaphoreType.DMA((2,)),
                       pltpu.SemaphoreType.DMA, pltpu.SemaphoreType.DMA],
        compiler_params=pltpu.CompilerParams(needs_layout_passes=False),
        name="dispatch",
    )
    def k(h2p_hbm, dest_hbm, zidx_hbm, nz_hbm, zsrc_hbm, xs_hbm, idx_v, zidx_v, nz_v, rows_v, zero_v,
          gsem, wsem, zsem, isem):
        wid = _sc_worker_id()
        base = wid * per_w
        setup = [pltpu.make_async_copy(zidx_hbm.at[pl.ds(j * SC_WORKERS + wid, 1)],
                                       zidx_v.at[pl.ds(j, 1)], isem) for j in range(z_chunks)]
        setup += [pltpu.make_async_copy(dest_hbm.at[pl.ds(wid * n_chunks * TOP_K, n_chunks * TOP_K)],
                                        idx_v, isem),
                  pltpu.make_async_copy(nz_hbm.at[pl.ds(0, SC_LANES)], nz_v, isem),
                  pltpu.make_async_copy(zsrc_hbm, zero_v, isem)]
        for cp in setup:
            cp.start()
        for cp in setup:
            cp.wait()
        n_zero_chunks = jnp.max(nz_v[...])

        def zput(j):
            return pltpu.make_async_copy(zero_v, xs_hbm.at[zidx_v.at[j]], zsem)

        for j in range(z_chunks):
            @pl.when(j * SC_WORKERS + wid < n_zero_chunks)
            def _():
                zput(j).start()

        def get(j, slot):
            return pltpu.make_async_copy(h2p_hbm.at[pl.ds(base + j * SC_CHUNK, SC_CHUNK)],
                                         rows_v.at[slot], gsem)

        def put(j, slot, kk):
            return pltpu.make_async_copy(rows_v.at[slot], xs_hbm.at[idx_v.at[j * TOP_K + kk]],
                                         wsem.at[slot])

        get(0, 0).start()

        @pl.loop(0, n_chunks, step=2)
        def _(j):
            for b in range(2):
                jj = j + b
                get(jj, b).wait()
                for kk in range(TOP_K):
                    put(jj, b, kk).start()

                @pl.when(jj >= 1)
                def _():
                    for kk in range(TOP_K):
                        put(jj - 1, 1 - b, kk).wait()

                @pl.when(jj + 1 < n_chunks)
                def _():
                    get(jj + 1, 1 - b).start()

        for kk in range(TOP_K):
            put(n_chunks - 1, (n_chunks - 1) % 2, kk).wait()
        for j in range(z_chunks):
            @pl.when(j * SC_WORKERS + wid < n_zero_chunks)
            def _():
                zput(j).wait()

    return k(h2p, dest_w, zero_w, n_zero, jnp.zeros((SC_CHUNK, width), h2p.dtype))


def _regroup_sum(ys, dest_t, wsel_t, xres):
    n_slots, t_tok = dest_t.shape
    width = ys.shape[1]
    d_out = xres.shape[1]
    grp = SC_SUM_GROUP
    r_rows = SC_SUM_ROW_RING
    r_acc = SC_SUM_ACC_RING
    per_w = t_tok // SC_WORKERS
    n_sub = per_w // grp

    @functools.partial(
        pl.kernel, mesh=_sc_mesh(),
        out_type=jax.ShapeDtypeStruct((t_tok, d_out), F32),
        scratch_types=[pltpu.VMEM((n_slots, per_w), jnp.int32),
                       pltpu.VMEM((n_slots, per_w), F32),
                       pltpu.VMEM((r_rows, n_slots * grp, width), ys.dtype),
                       pltpu.VMEM((r_acc, grp, d_out), F32),
                       pltpu.SemaphoreType.DMA((r_rows,)), pltpu.SemaphoreType.DMA((r_acc,)),
                       pltpu.SemaphoreType.DMA((r_acc,))],
        compiler_params=pltpu.CompilerParams(needs_layout_passes=False),
        name="regroup_sum",
    )
    def k(ys_hbm, idx_hbm, w_hbm, xres_hbm, out_hbm, idx_v, w_v, rows_v, acc_v, gsem, xsem, psem):
        wid = _sc_worker_id()
        base = wid * per_w
        setup = [pltpu.make_async_copy(idx_hbm.at[:, pl.ds(base, per_w)], idx_v, psem.at[0]),
                 pltpu.make_async_copy(w_hbm.at[:, pl.ds(base, per_w)], w_v, psem.at[1])]
        for cp in setup:
            cp.start()

        def gets(j, slot):
            return [pltpu.make_async_copy(ys_hbm.at[idx_v.at[kk, pl.ds(j * grp, grp)]],
                                          rows_v.at[slot, pl.ds(kk * grp, grp)], gsem.at[slot])
                    for kk in range(n_slots)]

        def xload(j, slot):
            return pltpu.make_async_copy(xres_hbm.at[pl.ds(base + j * grp, grp)], acc_v.at[slot],
                                         xsem.at[slot])

        def put(j, slot):
            return pltpu.make_async_copy(acc_v.at[slot], out_hbm.at[pl.ds(base + j * grp, grp)],
                                         psem.at[slot])

        def accumulate(j, rslot, aslot):
            @pl.loop(0, grp)
            def _(g):
                pos = jnp.full((SC_LANES,), j * grp + g, jnp.int32)
                wk = [plsc.load_gather(w_v, [jnp.full((SC_LANES,), kk, jnp.int32), pos])
                      for kk in range(n_slots)]

                @plsc.parallel_loop(0, width // SC_LANES)
                def _(v):
                    cols_a = pl.ds(v * SC_LANES, SC_LANES)
                    cols_b = pl.ds(width + v * SC_LANES, SC_LANES)
                    acc_a = acc_v[aslot, g, cols_a]
                    acc_b = acc_v[aslot, g, cols_b]
                    for kk in range(n_slots):
                        ya, yb = _unpack_pair(rows_v[rslot, kk * grp + g, cols_a])
                        acc_a = acc_a + ya * wk[kk]
                        acc_b = acc_b + yb * wk[kk]
                    acc_v[aslot, g, cols_a] = acc_a
                    acc_v[aslot, g, cols_b] = acc_b

        xload(0, 0).start()
        for cp in setup:
            cp.wait()
        for cp in gets(0, 0):
            cp.start()

        @pl.loop(0, n_sub, step=r_acc)
        def _(j):
            for b in range(r_acc):
                jj = j + b
                rs = b % r_rows
                na = (b + 1) % r_acc
                for cp in gets(jj, rs):
                    cp.wait()

                @pl.when(jj + 1 < n_sub)
                def _():
                    for cp in gets(jj + 1, (b + 1) % r_rows):
                        cp.start()
                xload(jj, b).wait()

                @pl.when(jj + 1 < n_sub)
                def _():
                    @pl.when(jj + 1 >= r_acc)
                    def _():
                        put(jj + 1 - r_acc, na).wait()
                    xload(jj + 1, na).start()
                accumulate(jj, rs, b)
                put(jj, b).start()

        for p in range(r_acc):
            put(n_sub - r_acc + p, p).wait()

    return k(ys, dest_t, wsel_t, xres)


def _experts_kernel(n_blk, blk_e_ref, next_e_ref, n_used_ref,
                    xs_hbm, eg_hbm, eu_hbm, ed_hbm, ys_hbm,
                    xbuf, ybuf, hid, stg_g, stg_u, stg_d, wg, wu, wd, xsem, ysem, wsem):
    n = n_used_ref[0]

    def ring(b):
        return jnp.bitwise_and(b, ROW_RING - 1)

    def x_copy(b):
        return pltpu.make_async_copy(xs_hbm.at[pl.ds(pl.multiple_of(b * ROW_BLOCK, ROW_BLOCK),
                                                     ROW_BLOCK)], xbuf.at[ring(b)], xsem.at[ring(b)])

    def y_copy(b):
        return pltpu.make_async_copy(ybuf.at[ring(b)],
                                     ys_hbm.at[pl.ds(pl.multiple_of(b * ROW_BLOCK, ROW_BLOCK),
                                                     ROW_BLOCK)], ysem.at[ring(b)])

    def w_copies(e):
        return (pltpu.make_async_copy(eg_hbm.at[e], stg_g, wsem.at[0]),
                pltpu.make_async_copy(eu_hbm.at[e], stg_u, wsem.at[1]),
                pltpu.make_async_copy(ed_hbm.at[e], stg_d, wsem.at[2]))

    def switch_expert(e, wslot):
        for cp in w_copies(e):
            cp.wait()
        wg[wslot] = stg_g[...].astype(BF16)
        wu[wslot] = stg_u[...].astype(BF16)
        wd[wslot] = stg_d[...].astype(BF16)
        nxt = next_e_ref[e]

        @pl.when(nxt >= 0)
        def _():
            for cp in w_copies(nxt):
                cp.start()

    def gate_up(b, wslot):
        xa, xb = _unpack_pair(xbuf[ring(b)])
        xb16 = jnp.concatenate([xa, xb], axis=1).astype(BF16)
        g = _dot(xb16, wg[wslot])
        up = _dot(xb16, wu[wslot])
        hid[jnp.bitwise_and(b, 1)] = (g * _sigmoid(g) * up).astype(BF16)

    def down(b, wslot):
        y = _dot(hid[jnp.bitwise_and(b, 1)], wd[wslot])
        ybuf[ring(b)] = _pack_pair(y[:, 0:HALF], y[:, HALF:D_MODEL])

    e0 = blk_e_ref[0]
    for cp in w_copies(e0):
        cp.start()
    for j in range(ROW_RING):
        @pl.when(j < n)
        def _():
            x_copy(j).start()
    switch_expert(e0, 0)
    x_copy(0).wait()
    gate_up(0, 0)

    def body(b, wslot_prev):
        e = blk_e_ref[b]
        first = e != blk_e_ref[b - 1]
        wslot = jnp.where(first, 1 - wslot_prev, wslot_prev)

        @pl.when(first)
        def _():
            switch_expert(e, wslot)

        x_copy(b).wait()

        @pl.when(b + ROW_RING - 1 < n)
        def _():
            x_copy(b + ROW_RING - 1).start()

        @pl.when(b >= ROW_RING + 1)
        def _():
            y_copy(b - 1 - ROW_RING).wait()

        down(b - 1, wslot_prev)
        gate_up(b, wslot)
        y_copy(b - 1).start()
        return wslot

    wslot_last = lax.fori_loop(1, n, body, jnp.int32(0))

    last = n - 1

    @pl.when(last >= ROW_RING)
    def _():
        y_copy(last - ROW_RING).wait()
    down(last, wslot_last)
    y_copy(last).start()
    for j in range(ROW_RING - 1, -1, -1):
        @pl.when(last - j >= 0)
        def _():
            y_copy(last - j).wait()

    ybuf[0] = jnp.zeros((ROW_BLOCK, HALF), U32)

    def zero_tail(b, c):
        cp = pltpu.make_async_copy(ybuf.at[0],
                                   ys_hbm.at[pl.ds(pl.multiple_of(b * ROW_BLOCK, ROW_BLOCK),
                                                   ROW_BLOCK)], ysem.at[0])
        cp.start()
        cp.wait()
        return c
    lax.fori_loop(n, n_blk, zero_tail, 0)


def _experts(blk_e, next_e, n_used, xs, e_gate, e_up, e_down):
    n_rows = xs.shape[0]
    n_blk = n_rows // ROW_BLOCK
    any_spec = pl.BlockSpec(memory_space=pl.ANY)
    grid_spec = pltpu.PrefetchScalarGridSpec(
        num_scalar_prefetch=3,
        grid=(1,),
        in_specs=[any_spec, any_spec, any_spec, any_spec],
        out_specs=any_spec,
        scratch_shapes=[pltpu.VMEM((ROW_RING, ROW_BLOCK, HALF), U32),
                        pltpu.VMEM((ROW_RING, ROW_BLOCK, HALF), U32),
                        pltpu.VMEM((2, ROW_BLOCK, EXPERT_HIDDEN), BF16),
                        pltpu.VMEM((D_MODEL, EXPERT_HIDDEN), F32),
                        pltpu.VMEM((D_MODEL, EXPERT_HIDDEN), F32),
                        pltpu.VMEM((EXPERT_HIDDEN, D_MODEL), F32),
                        pltpu.VMEM((2, D_MODEL, EXPERT_HIDDEN), BF16),
                        pltpu.VMEM((2, D_MODEL, EXPERT_HIDDEN), BF16),
                        pltpu.VMEM((2, EXPERT_HIDDEN, D_MODEL), BF16),
                        pltpu.SemaphoreType.DMA((ROW_RING,)),
                        pltpu.SemaphoreType.DMA((ROW_RING,)),
                        pltpu.SemaphoreType.DMA((3,))],
    )
    return pl.pallas_call(
        functools.partial(_experts_kernel, n_blk),
        grid_spec=grid_spec,
        out_shape=jax.ShapeDtypeStruct((n_rows, HALF), U32),
        compiler_params=pltpu.CompilerParams(dimension_semantics=("arbitrary",)),
        name="experts",
    )(blk_e, next_e, n_used, xs, e_gate, e_up, e_down)


def _final_norm_kernel(x_ref, g_ref, out_ref):
    out_ref[...] = _rms(x_ref[...], g_ref[...])


def _final_norm(xsum, g_final):
    t_tok = xsum.shape[0]
    blk = pl.BlockSpec((TM_NORM, D_MODEL), lambda i: (i, 0))
    return pl.pallas_call(
        _final_norm_kernel,
        grid=(t_tok // TM_NORM,),
        in_specs=[blk, pl.BlockSpec((1, D_MODEL), lambda i: (0, 0))],
        out_specs=blk,
        out_shape=jax.ShapeDtypeStruct((t_tok, D_MODEL), F32),
        compiler_params=pltpu.CompilerParams(dimension_semantics=("arbitrary",)),
        name="final_norm",
    )(xsum, g_final)


def kernel(x, g_mix, w_in, b_gate, w_pool_group, pool_scale, w_pool_out, conv_w, w_conv_out, w_o,
           g_ffn, w_router, router_bias, e_gate, e_up, e_down, s_gate, s_up, s_down, g_final):
    b, s, d = x.shape
    t_tok = b * s
    n_pad = N_EXPERTS * ROW_BLOCK
    n_rows = t_tok * TOP_K + n_pad
    n_blk = n_rows // ROW_BLOCK
    assert d == D_MODEL and s % TM_MIX == 0 and TM_MIX >= POOL_HALO
    assert t_tok % TM_DEST == 0 and t_tok % TM_NORM == 0 and n_pad % PLAN_LANES == 0
    assert t_tok % (2 * SC_WORKERS * SC_CHUNK) == 0 and n_pad % (SC_WORKERS * SC_CHUNK) == 0
    assert t_tok % (SC_WORKERS * SC_SUM_GROUP * SC_SUM_ACC_RING) == 0

    row = lambda a: a.reshape(1, -1)
    wr_t = w_router.T.astype(F32)
    wr_hi = wr_t.astype(BF16)
    wr = jnp.concatenate([wr_hi, (wr_t - wr_hi.astype(F32)).astype(BF16)], axis=0)

    xres, h2p, idx_t, wsel_t, rank_t, counts = _mixer_router(
        x.reshape(t_tok, d), s, row(g_mix), w_in, row(b_gate), w_pool_group,
        row(pool_scale), w_pool_out, conv_w, w_conv_out, w_o, row(g_ffn), wr,
        router_bias.astype(F32).reshape(N_EXPERTS, 1), s_gate, s_up, s_down)
    dest_t, blk_e, next_e, n_used, n_zero, zero_rows = _plan(counts, idx_t, rank_t, n_blk, n_pad)
    xs = _dispatch(h2p, dest_t, zero_rows, n_zero, n_rows)
    ys = _experts(blk_e, next_e, n_used, xs, e_gate, e_up, e_down)
    xsum = _regroup_sum(ys, dest_t, wsel_t, xres)
    return _final_norm(xsum, row(g_final)).reshape(b, s, d)
```

```python
import functools

import jax
import jax.numpy as jnp
from jax import lax
from jax.experimental import pallas as pl
from jax.experimental.pallas import tpu as pltpu
from jax.experimental.pallas import tpu_sc as plsc

D_MODEL = 1024
HALF = D_MODEL // 2
POOL_WIDTH = 512
N_POOL_GROUPS = 4
POOL_GROUP = 128
POOL_WINDOWS = (2, 4, 8, 16)
CONV_WIDTH = 512
N_EXPERTS = 64
TOP_K = 8
EXPERT_HIDDEN = 256
SHARED_HIDDEN = 256
ROUTED_SCALE = 2.5
EPS = 1e-6

POOL_HALO = 16
CONV_HALO = 8
TM_MIX = 512
W_IN_CHUNK = 128
TM_DEST = 2048
PLAN_LANES = 2048
ROW_BLOCK = 512
ROW_RING = 8
TM_NORM = 1024
VMEM_LIMIT = 56 * 1024 * 1024

SC_CORES = 2
SC_SUBCORES = 16
SC_WORKERS = SC_CORES * SC_SUBCORES
SC_LANES = 16
SC_CHUNK = 32
SC_SUM_GROUP = 8
SC_SUM_ROW_RING = 2
SC_SUM_ACC_RING = 4

BF16 = jnp.bfloat16
F32 = jnp.float32
U32 = jnp.uint32


def _rms(x, g):
    r = lax.rsqrt(jnp.mean(x * x, axis=-1, keepdims=True) + EPS)
    return (x * r) * g


def _dot(a, b):
    return jnp.dot(a, b, preferred_element_type=F32)


def _sigmoid(z):
    return 0.5 * jnp.tanh(0.5 * z) + 0.5


def _pack_pair(a, b):
    ra = lax.bitcast_convert_type(a.astype(BF16).astype(F32), U32)
    rb = lax.bitcast_convert_type(b.astype(BF16).astype(F32), U32)
    return ra | (rb >> 16)


def _unpack_pair(w):
    a = lax.bitcast_convert_type(w & jnp.uint32(0xFFFF0000), F32)
    b = lax.bitcast_convert_type(w << 16, F32)
    return a, b


def _load_weights_bf16(w_in_hbm, w_grp_hbm, w_po_hbm, w_co_hbm, w_o_hbm, s_gate_hbm, s_up_hbm,
                       s_down_hbm, w_in_ref, w_grp_ref, w_po_ref, w_co_ref, w_o_ref, s_gate_ref,
                       s_up_ref, s_down_ref, stg_in, stg_sq, stg_sh, stg_grp, wsem):
    copy = lambda src, dst, k: pltpu.make_async_copy(src, dst, wsem.at[k])
    rows = stg_in.shape[1]
    n_in = w_in_hbm.shape[0] // rows
    half = w_po_hbm.shape[0]
    c_in = [copy(w_in_hbm.at[pl.ds(j * rows, rows)], stg_in.at[j % 2], j % 2) for j in range(n_in)]
    c_o = copy(w_o_hbm, stg_sq, 2)
    c_sg = copy(s_gate_hbm, stg_sh.at[0], 3)
    c_su = copy(s_up_hbm, stg_sh.at[1], 4)
    c_grp = copy(w_grp_hbm, stg_grp, 5)
    c_po = copy(w_po_hbm, stg_sq.at[0:half], 2)
    c_co = copy(w_co_hbm, stg_sq.at[half:2 * half], 6)
    c_sd = copy(s_down_hbm, stg_sq.at[0:s_down_hbm.shape[0]], 2)
    for cp in (c_in[0], c_in[1], c_o, c_sg, c_su, c_grp):
        cp.start()
    for j in range(n_in):
        c_in[j].wait()
        w_in_ref[j * rows:(j + 1) * rows, :] = stg_in[j % 2].astype(BF16)
        if j + 2 < n_in:
            c_in[j + 2].start()
    c_o.wait()
    w_o_ref[...] = stg_sq[...].astype(BF16)
    c_po.start()
    c_co.start()
    c_sg.wait()
    s_gate_ref[...] = stg_sh[0].astype(BF16)
    c_su.wait()
    s_up_ref[...] = stg_sh[1].astype(BF16)
    c_grp.wait()
    w_grp_ref[...] = stg_grp[...].astype(BF16)
    c_po.wait()
    w_po_ref[...] = stg_sq[0:half, :].astype(BF16)
    c_co.wait()
    w_co_ref[...] = stg_sq[half:2 * half, :].astype(BF16)
    c_sd.start()
    c_sd.wait()
    s_down_ref[...] = stg_sq[0:s_down_hbm.shape[0], :].astype(BF16)


def _mixer_router_kernel(n_seq_tiles,
                         x_ref, g_mix_ref, w_in_hbm, b_gate_ref, w_grp_hbm, pool_scale_ref,
                         w_po_hbm, conv_w_ref, w_co_hbm, w_o_hbm, g_ffn_ref,
                         wr_ref, rbias_ref, s_gate_hbm, s_up_hbm, s_down_hbm,
                         xres_ref, h2p_ref, idx_ref, wsel_ref, rank_ref, counts_ref,
                         ext_pool, ext_conv, cnt_carry, tri,
                         w_in_ref, w_grp_ref, w_po_ref, w_co_ref, w_o_ref, s_gate_ref, s_up_ref,
                         s_down_ref, stg_in, stg_sq, stg_sh, stg_grp, wsem):
    tm = x_ref.shape[0]
    i = pl.program_id(0)
    st = i % n_seq_tiles

    @pl.when(i == 0)
    def _():
        _load_weights_bf16(w_in_hbm, w_grp_hbm, w_po_hbm, w_co_hbm, w_o_hbm, s_gate_hbm, s_up_hbm,
                           s_down_hbm, w_in_ref, w_grp_ref, w_po_ref, w_co_ref, w_o_ref,
                           s_gate_ref, s_up_ref, s_down_ref, stg_in, stg_sq, stg_sh, stg_grp, wsem)
        r = lax.broadcasted_iota(jnp.int32, (tm, tm), 0)
        c = lax.broadcasted_iota(jnp.int32, (tm, tm), 1)
        tri[...] = (r < c).astype(BF16)
        cnt_carry[...] = jnp.zeros_like(cnt_carry)

    @pl.when(st == 0)
    def _():
        ext_pool[0:POOL_HALO, :] = jnp.zeros((POOL_HALO, POOL_WIDTH), F32)
        ext_conv[0:CONV_HALO, :] = jnp.zeros((CONV_HALO, CONV_WIDTH), F32)

    x = x_ref[...]
    hb = _rms(x, g_mix_ref[...]).astype(BF16)

    o0 = POOL_WIDTH
    o1 = o0 + CONV_WIDTH
    o2 = o1 + CONV_WIDTH
    o3 = o2 + CONV_WIDTH

    u = _dot(hb, w_in_ref[:, 0:o0])
    ext_pool[POOL_HALO:POOL_HALO + tm, :] = u
    gc = _dot(hb, w_in_ref[:, o1:o2])
    v = _dot(hb, w_in_ref[:, o2:o3])
    pre_a = _dot(hb, w_in_ref[:, o3:o3 + D_MODEL])

    t_glob = st * tm + lax.broadcasted_iota(jnp.int32, (tm, 1), 0)
    mixed = []
    for gi, w in enumerate(POOL_WINDOWS):
        cols = slice(gi * POOL_GROUP, (gi + 1) * POOL_GROUP)
        ug = u[:, cols]
        acc = ug
        for j in range(1, w):
            acc = acc + ext_pool[POOL_HALO - j:POOL_HALO - j + tm, cols]
        cnt = jnp.minimum(t_glob + 1, w).astype(F32)
        pooled = acc * (1.0 / cnt) - ug
        mixed.append(_dot(pooled.astype(BF16), w_grp_ref[gi]))
    ext_pool[0:POOL_HALO, :] = ext_pool[tm:tm + POOL_HALO, :]
    pre_b = _dot(hb, w_in_ref[:, o3 + D_MODEL:o3 + 2 * D_MODEL])
    gb = _dot(hb, w_in_ref[:, o0:o1])

    cv = gc * v
    ext_conv[CONV_HALO:CONV_HALO + tm, :] = cv
    conv = (ext_conv[CONV_HALO - 2:CONV_HALO - 2 + tm, :] * conv_w_ref[0:1, :]
            + ext_conv[CONV_HALO - 1:CONV_HALO - 1 + tm, :] * conv_w_ref[1:2, :]
            + cv * conv_w_ref[2:3, :])
    ext_conv[0:CONV_HALO, :] = ext_conv[tm:tm + CONV_HALO, :]
    branch_b = _dot((gb * conv).astype(BF16), w_co_ref[...])
    mixed = jnp.concatenate(mixed, axis=1) * pool_scale_ref[...]
    branch_a = _dot(mixed.astype(BF16), w_po_ref[...])

    merged = (_sigmoid(pre_a + b_gate_ref[:, 0:D_MODEL]) * branch_a
              + _sigmoid(pre_b + b_gate_ref[:, D_MODEL:2 * D_MODEL]) * branch_b)
    x1 = x + _dot(merged.astype(BF16), w_o_ref[...])

    h2 = _rms(x1, g_ffn_ref[...])
    h2p_ref[...] = _pack_pair(h2[:, 0:HALF], h2[:, HALF:D_MODEL])
    h2b = h2.astype(BF16)

    nt = (((1,), (1,)), ((), ()))
    parts = lax.dot_general(wr_ref[...], h2b, nt, preferred_element_type=F32)
    logits = parts[0:N_EXPERTS, :] + parts[N_EXPERTS:2 * N_EXPERTS, :]
    sg = _dot(h2b, s_gate_ref[...])
    su = _dot(h2b, s_up_ref[...])
    scores = jax.nn.sigmoid(logits)
    sel = scores + rbias_ref[...]
    eidx = lax.broadcasted_iota(jnp.int32, (N_EXPERTS, tm), 0).astype(F32)
    e_rows, w_rows = [], []
    mask = jnp.zeros((N_EXPERTS, tm), F32)
    for _ in range(TOP_K):
        m = jnp.max(sel, axis=0, keepdims=True)
        ek = jnp.min(jnp.where(sel == m, eidx, float(N_EXPERTS)), axis=0, keepdims=True)
        oh = eidx == ek
        w_rows.append(jnp.sum(jnp.where(oh, scores, 0.0), axis=0, keepdims=True))
        e_rows.append(ek)
        mask = mask + oh.astype(F32)
        sel = jnp.where(oh, -jnp.inf, sel)

    shared = _dot((sg * _sigmoid(sg) * su).astype(BF16), s_down_ref[...])
    xres_ref[...] = x1 + shared

    wsum = w_rows[0]
    for k in range(1, TOP_K):
        wsum = wsum + w_rows[k]

    before = _dot(mask.astype(BF16), tri[...]) + cnt_carry[...]
    for k in range(TOP_K):
        oh = eidx == e_rows[k]
        rank_ref[k:k + 1, :] = jnp.sum(jnp.where(oh, before, 0.0), axis=0,
                                       keepdims=True).astype(jnp.int32)
        idx_ref[k:k + 1, :] = e_rows[k].astype(jnp.int32)
        wsel_ref[k:k + 1, :] = w_rows[k] / wsum * ROUTED_SCALE
    total = cnt_carry[...] + jnp.sum(mask, axis=1, keepdims=True)
    cnt_carry[...] = total
    counts_ref[...] = total.astype(jnp.int32)


def _mixer_router(x2d, seq_len, g_mix, w_in, b_gate, w_grp, pool_scale, w_po, conv_w,
                  w_co, w_o, g_ffn, wr, rbias, s_gate, s_up, s_down):
    t_tok = x2d.shape[0]
    tm = TM_MIX
    n_seq_tiles = seq_len // tm
    const = lambda shape: pl.BlockSpec(shape, lambda i: (0,) * len(shape),
                                       pipeline_mode=pl.Buffered(1))
    hbm = pl.BlockSpec(memory_space=pl.ANY)
    row_blk = pl.BlockSpec((tm, D_MODEL), lambda i: (i, 0))
    half_blk = pl.BlockSpec((tm, HALF), lambda i: (i, 0))
    slot_blk = pl.BlockSpec((TOP_K, tm), lambda i: (0, i))
    return pl.pallas_call(
        functools.partial(_mixer_router_kernel, n_seq_tiles),
        grid=(t_tok // tm,),
        in_specs=[row_blk, const(g_mix.shape), hbm, const(b_gate.shape),
                  hbm, const(pool_scale.shape), hbm,
                  const(conv_w.shape), hbm, hbm, const(g_ffn.shape),
                  const(wr.shape), const(rbias.shape), hbm, hbm, hbm],
        out_specs=[row_blk, half_blk, slot_blk, slot_blk, slot_blk,
                   pl.BlockSpec((N_EXPERTS, 1), lambda i: (0, 0))],
        out_shape=[jax.ShapeDtypeStruct((t_tok, D_MODEL), F32),
                   jax.ShapeDtypeStruct((t_tok, HALF), U32),
                   jax.ShapeDtypeStruct((TOP_K, t_tok), jnp.int32),
                   jax.ShapeDtypeStruct((TOP_K, t_tok), F32),
                   jax.ShapeDtypeStruct((TOP_K, t_tok), jnp.int32),
                   jax.ShapeDtypeStruct((N_EXPERTS, 1), jnp.int32)],
        scratch_shapes=[pltpu.VMEM((POOL_HALO + tm, POOL_WIDTH), F32),
                        pltpu.VMEM((CONV_HALO + tm, CONV_WIDTH), F32),
                        pltpu.VMEM((N_EXPERTS, 1), F32),
                        pltpu.VMEM((tm, tm), BF16)]
                       + [pltpu.VMEM(w.shape, BF16)
                          for w in (w_in, w_grp, w_po, w_co, w_o, s_gate, s_up, s_down)]
                       + [pltpu.VMEM((2, W_IN_CHUNK, w_in.shape[1]), F32),
                          pltpu.VMEM(w_o.shape, F32),
                          pltpu.VMEM((2,) + s_gate.shape, F32),
                          pltpu.VMEM(w_grp.shape, F32),
                          pltpu.SemaphoreType.DMA((7,))],
        compiler_params=pltpu.CompilerParams(dimension_semantics=("arbitrary",),
                                             vmem_limit_bytes=VMEM_LIMIT),
        name="mixer_router",
    )(x2d, g_mix, w_in, b_gate, w_grp, pool_scale, w_po, conv_w, w_co, w_o, g_ffn,
      wr, rbias, s_gate, s_up, s_down)


def _plan_kernel(n_blk, n_pad, counts_ref, idx_ref, rank_ref,
                 dest_ref, blk_e_ref, next_e_ref, n_used_ref, n_zero_ref, zero_rows_ref, pad_start):
    e_n = N_EXPERTS

    @pl.when(pl.program_id(0) == 0)
    def _():
        sub = lax.broadcasted_iota(jnp.int32, (e_n, e_n), 0)
        lane = lax.broadcasted_iota(jnp.int32, (e_n, e_n), 1)
        c_col = counts_ref[...]
        p_col = ((c_col + (ROW_BLOCK - 1)) // ROW_BLOCK) * ROW_BLOCK
        c_f = c_col.astype(F32)
        p_f = p_col.astype(F32)
        gap_f = p_f - c_f
        to_row = lambda col: jnp.sum(jnp.where(sub == lane, col, 0.0), axis=0, keepdims=True)
        p_row = to_row(p_f)
        gap_row = to_row(gap_f)
        pad_end_col = jnp.sum(jnp.where(lane <= sub, p_row, 0.0), axis=1, keepdims=True)
        gap_before_col = jnp.sum(jnp.where(lane < sub, gap_row, 0.0), axis=1, keepdims=True)
        pad_start[...] = pad_end_col - p_f
        pad_end_last = jnp.sum(p_row, axis=1, keepdims=True)
        n_used_ref[...] = jnp.broadcast_to(pad_end_last * (1.0 / ROW_BLOCK),
                                           n_used_ref.shape).astype(jnp.int32)
        gap_total = jnp.sum(gap_row, axis=1, keepdims=True)
        n_zero_ref[...] = jnp.broadcast_to(jnp.floor((gap_total + (SC_CHUNK - 1))
                                                     * (1.0 / SC_CHUNK)),
                                           n_zero_ref.shape).astype(jnp.int32)
        row0 = (lax.broadcasted_iota(jnp.int32, (e_n, blk_e_ref.shape[1]), 1)
                * ROW_BLOCK).astype(F32)
        owner = jnp.sum(jnp.where(pad_end_col <= row0, 1.0, 0.0), axis=0, keepdims=True)
        blk_e_ref[...] = jnp.minimum(owner, float(e_n - 1)).astype(jnp.int32)
        used_later = jnp.logical_and(sub > lane, p_f > 0.0)
        nxt = jnp.min(jnp.where(used_later, sub, e_n), axis=0, keepdims=True)
        next_e_ref[...] = jnp.full(next_e_ref.shape, -1, jnp.int32)
        next_e_ref[:, 0:e_n] = jnp.where(nxt < e_n, nxt, -1)
        for j0 in range(0, n_pad, PLAN_LANES):
            j = (j0 + lax.broadcasted_iota(jnp.int32, (e_n, PLAN_LANES), 1)).astype(F32)
            before = jnp.sum(jnp.where(gap_before_col <= j, c_f, 0.0), axis=0, keepdims=True)
            zero_rows_ref[:, j0:j0 + PLAN_LANES] = (j[0:1, :] + before).astype(jnp.int32)

    tm = idx_ref.shape[1]
    eidx = lax.broadcasted_iota(jnp.int32, (e_n, tm), 0)
    ps = pad_start[...]
    for k in range(TOP_K):
        oh = eidx == idx_ref[k:k + 1, :]
        start = jnp.sum(jnp.where(oh, ps, 0.0), axis=0, keepdims=True)
        dest_ref[k:k + 1, :] = start.astype(jnp.int32) + rank_ref[k:k + 1, :]


def _plan(counts, idx_t, rank_t, n_blk, n_pad):
    t_tok = idx_t.shape[1]
    lanes = 128
    n_blk_p = -(-n_blk // lanes) * lanes
    slot_blk = pl.BlockSpec((TOP_K, TM_DEST), lambda i: (0, i))
    whole = lambda n: pl.BlockSpec((1, n), lambda i: (0, 0))
    dest_t, blk_e, next_e, n_used, n_zero, zero_rows = pl.pallas_call(
        functools.partial(_plan_kernel, n_blk, n_pad),
        grid=(t_tok // TM_DEST,),
        in_specs=[pl.BlockSpec((N_EXPERTS, 1), lambda i: (0, 0)), slot_blk, slot_blk],
        out_specs=[slot_blk, whole(n_blk_p), whole(lanes), whole(lanes), whole(lanes),
                   whole(n_pad)],
        out_shape=[jax.ShapeDtypeStruct((TOP_K, t_tok), jnp.int32),
                   jax.ShapeDtypeStruct((1, n_blk_p), jnp.int32),
                   jax.ShapeDtypeStruct((1, lanes), jnp.int32),
                   jax.ShapeDtypeStruct((1, lanes), jnp.int32),
                   jax.ShapeDtypeStruct((1, lanes), jnp.int32),
                   jax.ShapeDtypeStruct((1, n_pad), jnp.int32)],
        scratch_shapes=[pltpu.VMEM((N_EXPERTS, 1), F32)],
        compiler_params=pltpu.CompilerParams(dimension_semantics=("arbitrary",)),
        name="plan",
    )(counts, idx_t, rank_t)
    return (dest_t, blk_e.reshape(-1), next_e.reshape(-1), n_used.reshape(-1), n_zero.reshape(-1),
            zero_rows.reshape(-1))


def _sc_mesh():
    return plsc.VectorSubcoreMesh(core_axis_name="c", subcore_axis_name="s")


def _sc_worker_id():
    return lax.axis_index("s") * SC_CORES + lax.axis_index("c")


def _dispatch(h2p, dest_t, zero_rows, n_zero, n_rows):
    t_tok, width = h2p.shape
    per_w = t_tok // SC_WORKERS
    n_chunks = per_w // SC_CHUNK
    z_chunks = zero_rows.shape[0] // (SC_WORKERS * SC_CHUNK)
    dest_w = (dest_t.reshape(TOP_K, SC_WORKERS, n_chunks, SC_CHUNK)
              .transpose(1, 2, 0, 3).reshape(SC_WORKERS * n_chunks * TOP_K, SC_CHUNK))
    zero_w = zero_rows.reshape(z_chunks * SC_WORKERS, SC_CHUNK)

    @functools.partial(
        pl.kernel, mesh=_sc_mesh(),
        out_type=jax.ShapeDtypeStruct((n_rows, width), h2p.dtype),
        scratch_types=[pltpu.VMEM((n_chunks * TOP_K, SC_CHUNK), jnp.int32),
                       pltpu.VMEM((z_chunks, SC_CHUNK), jnp.int32),
                       pltpu.VMEM((SC_LANES,), jnp.int32),
                       pltpu.VMEM((2, SC_CHUNK, width), h2p.dtype),
                       pltpu.VMEM((SC_CHUNK, width), h2p.dtype),
                       pltpu.SemaphoreType.DMA, pltpu.SemaphoreType.DMA,
                       pltpu.SemaphoreType.DMA, pltpu.SemaphoreType.DMA],
        compiler_params=pltpu.CompilerParams(needs_layout_passes=False),
        name="dispatch",
    )
    def k(h2p_hbm, dest_hbm, zidx_hbm, nz_hbm, zsrc_hbm, xs_hbm, idx_v, zidx_v, nz_v, rows_v, zero_v,
          gsem, wsem, zsem, isem):
        wid = _sc_worker_id()
        base = wid * per_w
        setup = [pltpu.make_async_copy(zidx_hbm.at[pl.ds(j * SC_WORKERS + wid, 1)],
                                       zidx_v.at[pl.ds(j, 1)], isem) for j in range(z_chunks)]
        setup += [pltpu.make_async_copy(dest_hbm.at[pl.ds(wid * n_chunks * TOP_K, n_chunks * TOP_K)],
                                        idx_v, isem),
                  pltpu.make_async_copy(nz_hbm.at[pl.ds(0, SC_LANES)], nz_v, isem),
                  pltpu.make_async_copy(zsrc_hbm, zero_v, isem)]
        for cp in setup:
            cp.start()
        for cp in setup:
            cp.wait()
        n_zero_chunks = jnp.max(nz_v[...])

        def zput(j):
            return pltpu.make_async_copy(zero_v, xs_hbm.at[zidx_v.at[j]], zsem)

        for j in range(z_chunks):
            @pl.when(j * SC_WORKERS + wid < n_zero_chunks)
            def _():
                zput(j).start()

        def get(j, slot):
            return pltpu.make_async_copy(h2p_hbm.at[pl.ds(base + j * SC_CHUNK, SC_CHUNK)],
                                         rows_v.at[slot], gsem)

        def put(j, slot, kk):
            return pltpu.make_async_copy(rows_v.at[slot], xs_hbm.at[idx_v.at[j * TOP_K + kk]], wsem)

        get(0, 0).start()

        @pl.loop(0, n_chunks, step=2)
        def _(j):
            for b in range(2):
                jj = j + b
                get(jj, b).wait()

                @pl.when(jj >= 1)
                def _():
                    for kk in range(TOP_K):
                        put(jj - 1, 1 - b, kk).wait()

                @pl.when(jj + 1 < n_chunks)
                def _():
                    get(jj + 1, 1 - b).start()
                for kk in range(TOP_K):
                    put(jj, b, kk).start()

        for kk in range(TOP_K):
            put(n_chunks - 1, (n_chunks - 1) % 2, kk).wait()
        for j in range(z_chunks):
            @pl.when(j * SC_WORKERS + wid < n_zero_chunks)
            def _():
                zput(j).wait()

    return k(h2p, dest_w, zero_w, n_zero, jnp.zeros((SC_CHUNK, width), h2p.dtype))


def _regroup_sum(ys, dest_t, wsel_t, xres):
    n_slots, t_tok = dest_t.shape
    width = ys.shape[1]
    d_out = xres.shape[1]
    grp = SC_SUM_GROUP
    r_rows = SC_SUM_ROW_RING
    r_acc = SC_SUM_ACC_RING
    per_w = t_tok // SC_WORKERS
    n_sub = per_w // grp

    @functools.partial(
        pl.kernel, mesh=_sc_mesh(),
        out_type=jax.ShapeDtypeStruct((t_tok, d_out), F32),
        scratch_types=[pltpu.VMEM((n_slots, per_w), jnp.int32),
                       pltpu.VMEM((n_slots, per_w), F32),
                       pltpu.VMEM((r_rows, n_slots * grp, width), ys.dtype),
                       pltpu.VMEM((r_acc, grp, d_out), F32),
                       pltpu.SemaphoreType.DMA((r_rows,)), pltpu.SemaphoreType.DMA((r_acc,)),
                       pltpu.SemaphoreType.DMA((r_acc,))],
        compiler_params=pltpu.CompilerParams(needs_layout_passes=False),
        name="regroup_sum",
    )
    def k(ys_hbm, idx_hbm, w_hbm, xres_hbm, out_hbm, idx_v, w_v, rows_v, acc_v, gsem, xsem, psem):
        wid = _sc_worker_id()
        base = wid * per_w
        setup = [pltpu.make_async_copy(idx_hbm.at[:, pl.ds(base, per_w)], idx_v, psem.at[0]),
                 pltpu.make_async_copy(w_hbm.at[:, pl.ds(base, per_w)], w_v, psem.at[1])]
        for cp in setup:
            cp.start()

        def gets(j, slot):
            return [pltpu.make_async_copy(ys_hbm.at[idx_v.at[kk, pl.ds(j * grp, grp)]],
                                          rows_v.at[slot, pl.ds(kk * grp, grp)], gsem.at[slot])
                    for kk in range(n_slots)]

        def xload(j, slot):
            return pltpu.make_async_copy(xres_hbm.at[pl.ds(base + j * grp, grp)], acc_v.at[slot],
                                         xsem.at[slot])

        def put(j, slot):
            return pltpu.make_async_copy(acc_v.at[slot], out_hbm.at[pl.ds(base + j * grp, grp)],
                                         psem.at[slot])

        def accumulate(j, rslot, aslot):
            @pl.loop(0, grp)
            def _(g):
                pos = jnp.full((SC_LANES,), j * grp + g, jnp.int32)
                wk = [plsc.load_gather(w_v, [jnp.full((SC_LANES,), kk, jnp.int32), pos])
                      for kk in range(n_slots)]

                @plsc.parallel_loop(0, width // SC_LANES)
                def _(v):
                    cols_a = pl.ds(v * SC_LANES, SC_LANES)
                    cols_b = pl.ds(width + v * SC_LANES, SC_LANES)
                    acc_a = acc_v[aslot, g, cols_a]
                    acc_b = acc_v[aslot, g, cols_b]
                    for kk in range(n_slots):
                        ya, yb = _unpack_pair(rows_v[rslot, kk * grp + g, cols_a])
                        acc_a = acc_a + ya * wk[kk]
                        acc_b = acc_b + yb * wk[kk]
                    acc_v[aslot, g, cols_a] = acc_a
                    acc_v[aslot, g, cols_b] = acc_b

        xload(0, 0).start()
        for cp in setup:
            cp.wait()
        for cp in gets(0, 0):
            cp.start()

        @pl.loop(0, n_sub, step=r_acc)
        def _(j):
            for b in range(r_acc):
                jj = j + b
                rs = b % r_rows
                na = (b + 1) % r_acc
                for cp in gets(jj, rs):
                    cp.wait()

                @pl.when(jj + 1 < n_sub)
                def _():
                    for cp in gets(jj + 1, (b + 1) % r_rows):
                        cp.start()
                xload(jj, b).wait()

                @pl.when(jj + 1 < n_sub)
                def _():
                    @pl.when(jj + 1 >= r_acc)
                    def _():
                        put(jj + 1 - r_acc, na).wait()
                    xload(jj + 1, na).start()
                accumulate(jj, rs, b)
                put(jj, b).start()

        for p in range(r_acc):
            put(n_sub - r_acc + p, p).wait()

    return k(ys, dest_t, wsel_t, xres)


def _experts_kernel(n_blk, blk_e_ref, next_e_ref, n_used_ref,
                    xs_hbm, eg_hbm, eu_hbm, ed_hbm, ys_hbm,
                    xbuf, ybuf, hid, stg_g, stg_u, stg_d, wg, wu, wd, xsem, ysem, wsem):
    n = n_used_ref[0]

    def ring(b):
        return jnp.bitwise_and(b, ROW_RING - 1)

    def x_copy(b):
        return pltpu.make_async_copy(xs_hbm.at[pl.ds(pl.multiple_of(b * ROW_BLOCK, ROW_BLOCK),
                                                     ROW_BLOCK)], xbuf.at[ring(b)], xsem.at[ring(b)])

    def y_copy(b):
        return pltpu.make_async_copy(ybuf.at[ring(b)],
                                     ys_hbm.at[pl.ds(pl.multiple_of(b * ROW_BLOCK, ROW_BLOCK),
                                                     ROW_BLOCK)], ysem.at[ring(b)])

    def w_copies(e):
        return (pltpu.make_async_copy(eg_hbm.at[e], stg_g, wsem.at[0]),
                pltpu.make_async_copy(eu_hbm.at[e], stg_u, wsem.at[1]),
                pltpu.make_async_copy(ed_hbm.at[e], stg_d, wsem.at[2]))

    def switch_expert(e, wslot):
        for cp in w_copies(e):
            cp.wait()
        wg[wslot] = stg_g[...].astype(BF16)
        wu[wslot] = stg_u[...].astype(BF16)
        wd[wslot] = stg_d[...].astype(BF16)
        nxt = next_e_ref[e]

        @pl.when(nxt >= 0)
        def _():
            for cp in w_copies(nxt):
                cp.start()

    def gate_up(b, wslot):
        xa, xb = _unpack_pair(xbuf[ring(b)])
        xb16 = jnp.concatenate([xa, xb], axis=1).astype(BF16)
        g = _dot(xb16, wg[wslot])
        up = _dot(xb16, wu[wslot])
        hid[jnp.bitwise_and(b, 1)] = (g * _sigmoid(g) * up).astype(BF16)

    def down(b, wslot):
        y = _dot(hid[jnp.bitwise_and(b, 1)], wd[wslot])
        ybuf[ring(b)] = _pack_pair(y[:, 0:HALF], y[:, HALF:D_MODEL])

    e0 = blk_e_ref[0]
    for cp in w_copies(e0):
        cp.start()
    for j in range(ROW_RING):
        @pl.when(j < n)
        def _():
            x_copy(j).start()
    switch_expert(e0, 0)
    x_copy(0).wait()
    gate_up(0, 0)

    def body(b, wslot_prev):
        e = blk_e_ref[b]
        first = e != blk_e_ref[b - 1]
        wslot = jnp.where(first, 1 - wslot_prev, wslot_prev)

        @pl.when(first)
        def _():
            switch_expert(e, wslot)

        x_copy(b).wait()

        @pl.when(b + ROW_RING - 1 < n)
        def _():
            x_copy(b + ROW_RING - 1).start()

        @pl.when(b >= ROW_RING + 1)
        def _():
            y_copy(b - 1 - ROW_RING).wait()

        down(b - 1, wslot_prev)
        gate_up(b, wslot)
        y_copy(b - 1).start()
        return wslot

    wslot_last = lax.fori_loop(1, n, body, jnp.int32(0))

    last = n - 1

    @pl.when(last >= ROW_RING)
    def _():
        y_copy(last - ROW_RING).wait()
    down(last, wslot_last)
    y_copy(last).start()
    for j in range(ROW_RING - 1, -1, -1):
        @pl.when(last - j >= 0)
        def _():
            y_copy(last - j).wait()

    ybuf[0] = jnp.zeros((ROW_BLOCK, HALF), U32)

    def zero_tail(b, c):
        cp = pltpu.make_async_copy(ybuf.at[0],
                                   ys_hbm.at[pl.ds(pl.multiple_of(b * ROW_BLOCK, ROW_BLOCK),
                                                   ROW_BLOCK)], ysem.at[0])
        cp.start()
        cp.wait()
        return c
    lax.fori_loop(n, n_blk, zero_tail, 0)


def _experts(blk_e, next_e, n_used, xs, e_gate, e_up, e_down):
    n_rows = xs.shape[0]
    n_blk = n_rows // ROW_BLOCK
    any_spec = pl.BlockSpec(memory_space=pl.ANY)
    grid_spec = pltpu.PrefetchScalarGridSpec(
        num_scalar_prefetch=3,
        grid=(1,),
        in_specs=[any_spec, any_spec, any_spec, any_spec],
        out_specs=any_spec,
        scratch_shapes=[pltpu.VMEM((ROW_RING, ROW_BLOCK, HALF), U32),
                        pltpu.VMEM((ROW_RING, ROW_BLOCK, HALF), U32),
                        pltpu.VMEM((2, ROW_BLOCK, EXPERT_HIDDEN), BF16),
                        pltpu.VMEM((D_MODEL, EXPERT_HIDDEN), F32),
                        pltpu.VMEM((D_MODEL, EXPERT_HIDDEN), F32),
                        pltpu.VMEM((EXPERT_HIDDEN, D_MODEL), F32),
                        pltpu.VMEM((2, D_MODEL, EXPERT_HIDDEN), BF16),
                        pltpu.VMEM((2, D_MODEL, EXPERT_HIDDEN), BF16),
                        pltpu.VMEM((2, EXPERT_HIDDEN, D_MODEL), BF16),
                        pltpu.SemaphoreType.DMA((ROW_RING,)),
                        pltpu.SemaphoreType.DMA((ROW_RING,)),
                        pltpu.SemaphoreType.DMA((3,))],
    )
    return pl.pallas_call(
        functools.partial(_experts_kernel, n_blk),
        grid_spec=grid_spec,
        out_shape=jax.ShapeDtypeStruct((n_rows, HALF), U32),
        compiler_params=pltpu.CompilerParams(dimension_semantics=("arbitrary",)),
        name="experts",
    )(blk_e, next_e, n_used, xs, e_gate, e_up, e_down)


def _final_norm_kernel(x_ref, g_ref, out_ref):
    out_ref[...] = _rms(x_ref[...], g_ref[...])


def _final_norm(xsum, g_final):
    t_tok = xsum.shape[0]
    blk = pl.BlockSpec((TM_NORM, D_MODEL), lambda i: (i, 0))
    return pl.pallas_call(
        _final_norm_kernel,
        grid=(t_tok // TM_NORM,),
        in_specs=[blk, pl.BlockSpec((1, D_MODEL), lambda i: (0, 0))],
        out_specs=blk,
        out_shape=jax.ShapeDtypeStruct((t_tok, D_MODEL), F32),
        compiler_params=pltpu.CompilerParams(dimension_semantics=("arbitrary",)),
        name="final_norm",
    )(xsum, g_final)


def kernel(x, g_mix, w_in, b_gate, w_pool_group, pool_scale, w_pool_out, conv_w, w_conv_out, w_o,
           g_ffn, w_router, router_bias, e_gate, e_up, e_down, s_gate, s_up, s_down, g_final):
    b, s, d = x.shape
    t_tok = b * s
    n_pad = N_EXPERTS * ROW_BLOCK
    n_rows = t_tok * TOP_K + n_pad
    n_blk = n_rows // ROW_BLOCK
    assert d == D_MODEL and s % TM_MIX == 0 and TM_MIX >= POOL_HALO
    assert t_tok % TM_DEST == 0 and t_tok % TM_NORM == 0 and n_pad % PLAN_LANES == 0
    assert t_tok % (2 * SC_WORKERS * SC_CHUNK) == 0 and n_pad % (SC_WORKERS * SC_CHUNK) == 0
    assert t_tok % (SC_WORKERS * SC_SUM_GROUP * SC_SUM_ACC_RING) == 0

    row = lambda a: a.reshape(1, -1)
    wr_t = w_router.T.astype(F32)
    wr_hi = wr_t.astype(BF16)
    wr = jnp.concatenate([wr_hi, (wr_t - wr_hi.astype(F32)).astype(BF16)], axis=0)

    xres, h2p, idx_t, wsel_t, rank_t, counts = _mixer_router(
        x.reshape(t_tok, d), s, row(g_mix), w_in, row(b_gate), w_pool_group,
        row(pool_scale), w_pool_out, conv_w, w_conv_out, w_o, row(g_ffn), wr,
        router_bias.astype(F32).reshape(N_EXPERTS, 1), s_gate, s_up, s_down)
    dest_t, blk_e, next_e, n_used, n_zero, zero_rows = _plan(counts, idx_t, rank_t, n_blk, n_pad)
    xs = _dispatch(h2p, dest_t, zero_rows, n_zero, n_rows)
    ys = _experts(blk_e, next_e, n_used, xs, e_gate, e_up, e_down)
    xsum = _regroup_sum(ys, dest_t, wsel_t, xres)
    return _final_norm(xsum, row(g_final)).reshape(b, s, d)
```

```python
import functools

import jax
import jax.numpy as jnp
from jax import lax
from jax.experimental import pallas as pl
from jax.experimental.pallas import tpu as pltpu
from jax.experimental.pallas import tpu_sc as plsc

D_MODEL = 1024
HALF = D_MODEL // 2
POOL_WIDTH = 512
N_POOL_GROUPS = 4
POOL_GROUP = 128
POOL_WINDOWS = (2, 4, 8, 16)
CONV_WIDTH = 512
N_EXPERTS = 64
TOP_K = 8
EXPERT_HIDDEN = 256
SHARED_HIDDEN = 256
ROUTED_SCALE = 2.5
EPS = 1e-6

POOL_HALO = 16
CONV_HALO = 8
TM_MIX = 512
W_IN_CHUNK = 128
TM_DEST = 4096
PLAN_LANES = 2048
ROW_BLOCK = 512
ROW_RING = 8
TM_NORM = 2048
VMEM_LIMIT = 56 * 1024 * 1024

SC_CORES = 2
SC_SUBCORES = 16
SC_WORKERS = SC_CORES * SC_SUBCORES
SC_LANES = 16
SC_CHUNK = 64
SC_SUM_GROUP = 8
SC_SUM_ROW_RING = 2
SC_SUM_ACC_RING = 4

BF16 = jnp.bfloat16
F32 = jnp.float32
U32 = jnp.uint32


def _rms(x, g):
    r = lax.rsqrt(jnp.mean(x * x, axis=-1, keepdims=True) + EPS)
    return (x * r) * g


def _dot(a, b):
    return jnp.dot(a, b, preferred_element_type=F32)


def _sigmoid(z):
    return 0.5 * jnp.tanh(0.5 * z) + 0.5


def _pack_pair(a, b):
    ra = lax.bitcast_convert_type(a.astype(BF16).astype(F32), U32)
    rb = lax.bitcast_convert_type(b.astype(BF16).astype(F32), U32)
    return ra | (rb >> 16)


def _unpack_pair(w):
    a = lax.bitcast_convert_type(w & jnp.uint32(0xFFFF0000), F32)
    b = lax.bitcast_convert_type(w << 16, F32)
    return a, b


def _load_weights_bf16(w_in_hbm, w_grp_hbm, w_po_hbm, w_co_hbm, w_o_hbm, s_gate_hbm, s_up_hbm,
                       s_down_hbm, w_in_ref, w_grp_ref, w_po_ref, w_co_ref, w_o_ref, s_gate_ref,
                       s_up_ref, s_down_ref, stg_in, stg_sq, stg_sh, stg_grp, wsem):
    copy = lambda src, dst, k: pltpu.make_async_copy(src, dst, wsem.at[k])
    rows = stg_in.shape[1]
    n_in = w_in_hbm.shape[0] // rows
    half = w_po_hbm.shape[0]
    c_in = [copy(w_in_hbm.at[pl.ds(j * rows, rows)], stg_in.at[j % 2], j % 2) for j in range(n_in)]
    c_o = copy(w_o_hbm, stg_sq, 2)
    c_sg = copy(s_gate_hbm, stg_sh.at[0], 3)
    c_su = copy(s_up_hbm, stg_sh.at[1], 4)
    c_grp = copy(w_grp_hbm, stg_grp, 5)
    c_po = copy(w_po_hbm, stg_sq.at[0:half], 2)
    c_co = copy(w_co_hbm, stg_sq.at[half:2 * half], 6)
    c_sd = copy(s_down_hbm, stg_sq.at[0:s_down_hbm.shape[0]], 2)
    for cp in (c_in[0], c_in[1], c_o, c_sg, c_su, c_grp):
        cp.start()
    for j in range(n_in):
        c_in[j].wait()
        w_in_ref[j * rows:(j + 1) * rows, :] = stg_in[j % 2].astype(BF16)
        if j + 2 < n_in:
            c_in[j + 2].start()
    c_o.wait()
    w_o_ref[...] = stg_sq[...].astype(BF16)
    c_po.start()
    c_co.start()
    c_sg.wait()
    s_gate_ref[...] = stg_sh[0].astype(BF16)
    c_su.wait()
    s_up_ref[...] = stg_sh[1].astype(BF16)
    c_grp.wait()
    w_grp_ref[...] = stg_grp[...].astype(BF16)
    c_po.wait()
    w_po_ref[...] = stg_sq[0:half, :].astype(BF16)
    c_co.wait()
    w_co_ref[...] = stg_sq[half:2 * half, :].astype(BF16)
    c_sd.start()
    c_sd.wait()
    s_down_ref[...] = stg_sq[0:s_down_hbm.shape[0], :].astype(BF16)


def _mixer_router_kernel(n_seq_tiles,
                         x_ref, g_mix_ref, w_in_hbm, b_gate_ref, w_grp_hbm, pool_scale_ref,
                         w_po_hbm, conv_w_ref, w_co_hbm, w_o_hbm, g_ffn_ref,
                         wr_ref, rbias_ref, s_gate_hbm, s_up_hbm, s_down_hbm,
                         xres_ref, h2p_ref, idx_ref, wsel_ref, rank_ref, counts_ref,
                         ext_pool, ext_conv, cnt_carry, tri,
                         w_in_ref, w_grp_ref, w_po_ref, w_co_ref, w_o_ref, s_gate_ref, s_up_ref,
                         s_down_ref, stg_in, stg_sq, stg_sh, stg_grp, wsem):
    tm = x_ref.shape[0]
    i = pl.program_id(0)
    st = i % n_seq_tiles

    @pl.when(i == 0)
    def _():
        _load_weights_bf16(w_in_hbm, w_grp_hbm, w_po_hbm, w_co_hbm, w_o_hbm, s_gate_hbm, s_up_hbm,
                           s_down_hbm, w_in_ref, w_grp_ref, w_po_ref, w_co_ref, w_o_ref,
                           s_gate_ref, s_up_ref, s_down_ref, stg_in, stg_sq, stg_sh, stg_grp, wsem)
        r = lax.broadcasted_iota(jnp.int32, (tm, tm), 0)
        c = lax.broadcasted_iota(jnp.int32, (tm, tm), 1)
        tri[...] = (r < c).astype(BF16)
        cnt_carry[...] = jnp.zeros_like(cnt_carry)

    @pl.when(st == 0)
    def _():
        ext_pool[0:POOL_HALO, :] = jnp.zeros((POOL_HALO, POOL_WIDTH), F32)
        ext_conv[0:CONV_HALO, :] = jnp.zeros((CONV_HALO, CONV_WIDTH), F32)

    x = x_ref[...]
    hb = _rms(x, g_mix_ref[...]).astype(BF16)

    o0 = POOL_WIDTH
    o1 = o0 + CONV_WIDTH
    o2 = o1 + CONV_WIDTH
    o3 = o2 + CONV_WIDTH

    u = _dot(hb, w_in_ref[:, 0:o0])
    ext_pool[POOL_HALO:POOL_HALO + tm, :] = u
    gc = _dot(hb, w_in_ref[:, o1:o2])
    v = _dot(hb, w_in_ref[:, o2:o3])
    pre_a = _dot(hb, w_in_ref[:, o3:o3 + D_MODEL])

    t_glob = st * tm + lax.broadcasted_iota(jnp.int32, (tm, 1), 0)
    mixed = []
    for gi, w in enumerate(POOL_WINDOWS):
        cols = slice(gi * POOL_GROUP, (gi + 1) * POOL_GROUP)
        ug = u[:, cols]
        acc = ug
        for j in range(1, w):
            acc = acc + ext_pool[POOL_HALO - j:POOL_HALO - j + tm, cols]
        cnt = jnp.minimum(t_glob + 1, w).astype(F32)
        pooled = acc * (1.0 / cnt) - ug
        mixed.append(_dot(pooled.astype(BF16), w_grp_ref[gi]))
    ext_pool[0:POOL_HALO, :] = ext_pool[tm:tm + POOL_HALO, :]
    pre_b = _dot(hb, w_in_ref[:, o3 + D_MODEL:o3 + 2 * D_MODEL])
    gb = _dot(hb, w_in_ref[:, o0:o1])

    cv = gc * v
    ext_conv[CONV_HALO:CONV_HALO + tm, :] = cv
    conv = (ext_conv[CONV_HALO - 2:CONV_HALO - 2 + tm, :] * conv_w_ref[0:1, :]
            + ext_conv[CONV_HALO - 1:CONV_HALO - 1 + tm, :] * conv_w_ref[1:2, :]
            + cv * conv_w_ref[2:3, :])
    ext_conv[0:CONV_HALO, :] = ext_conv[tm:tm + CONV_HALO, :]
    branch_b = _dot((gb * conv).astype(BF16), w_co_ref[...])
    mixed = jnp.concatenate(mixed, axis=1) * pool_scale_ref[...]
    branch_a = _dot(mixed.astype(BF16), w_po_ref[...])

    merged = (_sigmoid(pre_a + b_gate_ref[:, 0:D_MODEL]) * branch_a
              + _sigmoid(pre_b + b_gate_ref[:, D_MODEL:2 * D_MODEL]) * branch_b)
    x1 = x + _dot(merged.astype(BF16), w_o_ref[...])

    h2 = _rms(x1, g_ffn_ref[...])
    h2p_ref[...] = _pack_pair(h2[:, 0:HALF], h2[:, HALF:D_MODEL])
    h2b = h2.astype(BF16)

    nt = (((1,), (1,)), ((), ()))
    parts = lax.dot_general(wr_ref[...], h2b, nt, preferred_element_type=F32)
    logits = parts[0:N_EXPERTS, :] + parts[N_EXPERTS:2 * N_EXPERTS, :]
    sg = _dot(h2b, s_gate_ref[...])
    su = _dot(h2b, s_up_ref[...])
    scores = jax.nn.sigmoid(logits)
    sel = scores + rbias_ref[...]
    eidx = lax.broadcasted_iota(jnp.int32, (N_EXPERTS, tm), 0).astype(F32)
    e_rows, w_rows = [], []
    mask = jnp.zeros((N_EXPERTS, tm), F32)
    for _ in range(TOP_K):
        m = jnp.max(sel, axis=0, keepdims=True)
        ek = jnp.min(jnp.where(sel == m, eidx, float(N_EXPERTS)), axis=0, keepdims=True)
        oh = eidx == ek
        w_rows.append(jnp.sum(jnp.where(oh, scores, 0.0), axis=0, keepdims=True))
        e_rows.append(ek)
        mask = mask + oh.astype(F32)
        sel = jnp.where(oh, -jnp.inf, sel)

    shared = _dot((sg * _sigmoid(sg) * su).astype(BF16), s_down_ref[...])
    xres_ref[...] = x1 + shared

    wsum = w_rows[0]
    for k in range(1, TOP_K):
        wsum = wsum + w_rows[k]

    before = _dot(mask.astype(BF16), tri[...]) + cnt_carry[...]
    for k in range(TOP_K):
        oh = eidx == e_rows[k]
        rank_ref[k:k + 1, :] = jnp.sum(jnp.where(oh, before, 0.0), axis=0,
                                       keepdims=True).astype(jnp.int32)
        idx_ref[k:k + 1, :] = e_rows[k].astype(jnp.int32)
        wsel_ref[k:k + 1, :] = w_rows[k] / wsum * ROUTED_SCALE
    total = cnt_carry[...] + jnp.sum(mask, axis=1, keepdims=True)
    cnt_carry[...] = total
    counts_ref[...] = total.astype(jnp.int32)


def _mixer_router(x2d, seq_len, g_mix, w_in, b_gate, w_grp, pool_scale, w_po, conv_w,
                  w_co, w_o, g_ffn, wr, rbias, s_gate, s_up, s_down):
    t_tok = x2d.shape[0]
    tm = TM_MIX
    n_seq_tiles = seq_len // tm
    const = lambda shape: pl.BlockSpec(shape, lambda i: (0,) * len(shape),
                                       pipeline_mode=pl.Buffered(1))
    hbm = pl.BlockSpec(memory_space=pl.ANY)
    row_blk = pl.BlockSpec((tm, D_MODEL), lambda i: (i, 0))
    half_blk = pl.BlockSpec((tm, HALF), lambda i: (i, 0))
    slot_blk = pl.BlockSpec((TOP_K, tm), lambda i: (0, i))
    return pl.pallas_call(
        functools.partial(_mixer_router_kernel, n_seq_tiles),
        grid=(t_tok // tm,),
        in_specs=[row_blk, const(g_mix.shape), hbm, const(b_gate.shape),
                  hbm, const(pool_scale.shape), hbm,
                  const(conv_w.shape), hbm, hbm, const(g_ffn.shape),
                  const(wr.shape), const(rbias.shape), hbm, hbm, hbm],
        out_specs=[row_blk, half_blk, slot_blk, slot_blk, slot_blk,
                   pl.BlockSpec((N_EXPERTS, 1), lambda i: (0, 0))],
        out_shape=[jax.ShapeDtypeStruct((t_tok, D_MODEL), F32),
                   jax.ShapeDtypeStruct((t_tok, HALF), U32),
                   jax.ShapeDtypeStruct((TOP_K, t_tok), jnp.int32),
                   jax.ShapeDtypeStruct((TOP_K, t_tok), F32),
                   jax.ShapeDtypeStruct((TOP_K, t_tok), jnp.int32),
                   jax.ShapeDtypeStruct((N_EXPERTS, 1), jnp.int32)],
        scratch_shapes=[pltpu.VMEM((POOL_HALO + tm, POOL_WIDTH), F32),
                        pltpu.VMEM((CONV_HALO + tm, CONV_WIDTH), F32),
                        pltpu.VMEM((N_EXPERTS, 1), F32),
                        pltpu.VMEM((tm, tm), BF16)]
                       + [pltpu.VMEM(w.shape, BF16)
                          for w in (w_in, w_grp, w_po, w_co, w_o, s_gate, s_up, s_down)]
                       + [pltpu.VMEM((2, W_IN_CHUNK, w_in.shape[1]), F32),
                          pltpu.VMEM(w_o.shape, F32),
                          pltpu.VMEM((2,) + s_gate.shape, F32),
                          pltpu.VMEM(w_grp.shape, F32),
                          pltpu.SemaphoreType.DMA((7,))],
        compiler_params=pltpu.CompilerParams(dimension_semantics=("arbitrary",),
                                             vmem_limit_bytes=VMEM_LIMIT),
        name="mixer_router",
    )(x2d, g_mix, w_in, b_gate, w_grp, pool_scale, w_po, conv_w, w_co, w_o, g_ffn,
      wr, rbias, s_gate, s_up, s_down)


def _plan_kernel(n_blk, n_pad, counts_ref, idx_ref, rank_ref,
                 dest_ref, blk_e_ref, next_e_ref, n_used_ref, n_zero_ref, zero_rows_ref, pad_start):
    e_n = N_EXPERTS

    @pl.when(pl.program_id(0) == 0)
    def _():
        sub = lax.broadcasted_iota(jnp.int32, (e_n, e_n), 0)
        lane = lax.broadcasted_iota(jnp.int32, (e_n, e_n), 1)
        c_col = counts_ref[...]
        p_col = ((c_col + (ROW_BLOCK - 1)) // ROW_BLOCK) * ROW_BLOCK
        c_f = c_col.astype(F32)
        p_f = p_col.astype(F32)
        gap_f = p_f - c_f
        to_row = lambda col: jnp.sum(jnp.where(sub == lane, col, 0.0), axis=0, keepdims=True)
        p_row = to_row(p_f)
        gap_row = to_row(gap_f)
        pad_end_col = jnp.sum(jnp.where(lane <= sub, p_row, 0.0), axis=1, keepdims=True)
        gap_before_col = jnp.sum(jnp.where(lane < sub, gap_row, 0.0), axis=1, keepdims=True)
        pad_start[...] = pad_end_col - p_f
        pad_end_last = jnp.sum(p_row, axis=1, keepdims=True)
        n_used_ref[...] = jnp.broadcast_to(pad_end_last * (1.0 / ROW_BLOCK),
                                           n_used_ref.shape).astype(jnp.int32)
        gap_total = jnp.sum(gap_row, axis=1, keepdims=True)
        n_zero_ref[...] = jnp.broadcast_to(jnp.floor((gap_total + (SC_CHUNK - 1))
                                                     * (1.0 / SC_CHUNK)),
                                           n_zero_ref.shape).astype(jnp.int32)
        row0 = (lax.broadcasted_iota(jnp.int32, (e_n, blk_e_ref.shape[1]), 1)
                * ROW_BLOCK).astype(F32)
        owner = jnp.sum(jnp.where(pad_end_col <= row0, 1.0, 0.0), axis=0, keepdims=True)
        blk_e_ref[...] = jnp.minimum(owner, float(e_n - 1)).astype(jnp.int32)
        used_later = jnp.logical_and(sub > lane, p_f > 0.0)
        nxt = jnp.min(jnp.where(used_later, sub, e_n), axis=0, keepdims=True)
        next_e_ref[...] = jnp.full(next_e_ref.shape, -1, jnp.int32)
        next_e_ref[:, 0:e_n] = jnp.where(nxt < e_n, nxt, -1)
        for j0 in range(0, n_pad, PLAN_LANES):
            j = (j0 + lax.broadcasted_iota(jnp.int32, (e_n, PLAN_LANES), 1)).astype(F32)
            before = jnp.sum(jnp.where(gap_before_col <= j, c_f, 0.0), axis=0, keepdims=True)
            zero_rows_ref[:, j0:j0 + PLAN_LANES] = (j[0:1, :] + before).astype(jnp.int32)

    tm = idx_ref.shape[1]
    eidx = lax.broadcasted_iota(jnp.int32, (e_n, tm), 0)
    ps = pad_start[...]
    for k in range(TOP_K):
        oh = eidx == idx_ref[k:k + 1, :]
        start = jnp.sum(jnp.where(oh, ps, 0.0), axis=0, keepdims=True)
        dest_ref[k:k + 1, :] = start.astype(jnp.int32) + rank_ref[k:k + 1, :]


def _plan(counts, idx_t, rank_t, n_blk, n_pad):
    t_tok = idx_t.shape[1]
    lanes = 128
    n_blk_p = -(-n_blk // lanes) * lanes
    slot_blk = pl.BlockSpec((TOP_K, TM_DEST), lambda i: (0, i))
    whole = lambda n: pl.BlockSpec((1, n), lambda i: (0, 0))
    dest_t, blk_e, next_e, n_used, n_zero, zero_rows = pl.pallas_call(
        functools.partial(_plan_kernel, n_blk, n_pad),
        grid=(t_tok // TM_DEST,),
        in_specs=[pl.BlockSpec((N_EXPERTS, 1), lambda i: (0, 0)), slot_blk, slot_blk],
        out_specs=[slot_blk, whole(n_blk_p), whole(lanes), whole(lanes), whole(lanes),
                   whole(n_pad)],
        out_shape=[jax.ShapeDtypeStruct((TOP_K, t_tok), jnp.int32),
                   jax.ShapeDtypeStruct((1, n_blk_p), jnp.int32),
                   jax.ShapeDtypeStruct((1, lanes), jnp.int32),
                   jax.ShapeDtypeStruct((1, lanes), jnp.int32),
                   jax.ShapeDtypeStruct((1, lanes), jnp.int32),
                   jax.ShapeDtypeStruct((1, n_pad), jnp.int32)],
        scratch_shapes=[pltpu.VMEM((N_EXPERTS, 1), F32)],
        compiler_params=pltpu.CompilerParams(dimension_semantics=("arbitrary",)),
        name="plan",
    )(counts, idx_t, rank_t)
    return (dest_t, blk_e.reshape(-1), next_e.reshape(-1), n_used.reshape(-1), n_zero.reshape(-1),
            zero_rows.reshape(-1))


def _sc_mesh():
    return plsc.VectorSubcoreMesh(core_axis_name="c", subcore_axis_name="s")


def _sc_worker_id():
    return lax.axis_index("s") * SC_CORES + lax.axis_index("c")


def _dispatch(h2p, dest_t, zero_rows, n_zero, n_rows):
    t_tok, width = h2p.shape
    per_w = t_tok // SC_WORKERS
    n_chunks = per_w // SC_CHUNK
    z_chunks = zero_rows.shape[0] // (SC_WORKERS * SC_CHUNK)
    dest_w = (dest_t.reshape(TOP_K, SC_WORKERS, n_chunks, SC_CHUNK)
              .transpose(1, 2, 0, 3).reshape(SC_WORKERS * n_chunks * TOP_K, SC_CHUNK))
    zero_w = zero_rows.reshape(z_chunks * SC_WORKERS, SC_CHUNK)

    @functools.partial(
        pl.kernel, mesh=_sc_mesh(),
        out_type=jax.ShapeDtypeStruct((n_rows, width), h2p.dtype),
        scratch_types=[pltpu.VMEM((n_chunks * TOP_K, SC_CHUNK), jnp.int32),
                       pltpu.VMEM((z_chunks, SC_CHUNK), jnp.int32),
                       pltpu.VMEM((SC_LANES,), jnp.int32),
                       pltpu.VMEM((2, SC_CHUNK, width), h2p.dtype),
                       pltpu.VMEM((SC_CHUNK, width), h2p.dtype),
                       pltpu.SemaphoreType.DMA, pltpu.SemaphoreType.DMA,
                       pltpu.SemaphoreType.DMA, pltpu.SemaphoreType.DMA],
        compiler_params=pltpu.CompilerParams(needs_layout_passes=False),
        name="dispatch",
    )
    def k(h2p_hbm, dest_hbm, zidx_hbm, nz_hbm, zsrc_hbm, xs_hbm, idx_v, zidx_v, nz_v, rows_v, zero_v,
          gsem, wsem, zsem, isem):
        wid = _sc_worker_id()
        base = wid * per_w
        setup = [pltpu.make_async_copy(zidx_hbm.at[pl.ds(j * SC_WORKERS + wid, 1)],
                                       zidx_v.at[pl.ds(j, 1)], isem) for j in range(z_chunks)]
        setup += [pltpu.make_async_copy(dest_hbm.at[pl.ds(wid * n_chunks * TOP_K, n_chunks * TOP_K)],
                                        idx_v, isem),
                  pltpu.make_async_copy(nz_hbm.at[pl.ds(0, SC_LANES)], nz_v, isem),
                  pltpu.make_async_copy(zsrc_hbm, zero_v, isem)]
        for cp in setup:
            cp.start()
        for cp in setup:
            cp.wait()
        n_zero_chunks = jnp.max(nz_v[...])

        def zput(j):
            return pltpu.make_async_copy(zero_v, xs_hbm.at[zidx_v.at[j]], zsem)

        for j in range(z_chunks):
            @pl.when(j * SC_WORKERS + wid < n_zero_chunks)
            def _():
                zput(j).start()

        def get(j, slot):
            return pltpu.make_async_copy(h2p_hbm.at[pl.ds(base + j * SC_CHUNK, SC_CHUNK)],
                                         rows_v.at[slot], gsem)

        def put(j, slot, kk):
            return pltpu.make_async_copy(rows_v.at[slot], xs_hbm.at[idx_v.at[j * TOP_K + kk]], wsem)

        get(0, 0).start()

        @pl.loop(0, n_chunks, step=2)
        def _(j):
            for b in range(2):
                jj = j + b
                get(jj, b).wait()

                @pl.when(jj >= 1)
                def _():
                    for kk in range(TOP_K):
                        put(jj - 1, 1 - b, kk).wait()

                @pl.when(jj + 1 < n_chunks)
                def _():
                    get(jj + 1, 1 - b).start()
                for kk in range(TOP_K):
                    put(jj, b, kk).start()

        for kk in range(TOP_K):
            put(n_chunks - 1, (n_chunks - 1) % 2, kk).wait()
        for j in range(z_chunks):
            @pl.when(j * SC_WORKERS + wid < n_zero_chunks)
            def _():
                zput(j).wait()

    return k(h2p, dest_w, zero_w, n_zero, jnp.zeros((SC_CHUNK, width), h2p.dtype))


def _regroup_sum(ys, dest_t, wsel_t, xres):
    n_slots, t_tok = dest_t.shape
    width = ys.shape[1]
    d_out = xres.shape[1]
    grp = SC_SUM_GROUP
    r_rows = SC_SUM_ROW_RING
    r_acc = SC_SUM_ACC_RING
    per_w = t_tok // SC_WORKERS
    n_sub = per_w // grp

    @functools.partial(
        pl.kernel, mesh=_sc_mesh(),
        out_type=jax.ShapeDtypeStruct((t_tok, d_out), F32),
        scratch_types=[pltpu.VMEM((n_slots, per_w), jnp.int32),
                       pltpu.VMEM((n_slots, per_w), F32),
                       pltpu.VMEM((r_rows, n_slots * grp, width), ys.dtype),
                       pltpu.VMEM((r_acc, grp, d_out), F32),
                       pltpu.SemaphoreType.DMA((r_rows,)), pltpu.SemaphoreType.DMA((r_acc,)),
                       pltpu.SemaphoreType.DMA((r_acc,))],
        compiler_params=pltpu.CompilerParams(needs_layout_passes=False),
        name="regroup_sum",
    )
    def k(ys_hbm, idx_hbm, w_hbm, xres_hbm, out_hbm, idx_v, w_v, rows_v, acc_v, gsem, xsem, psem):
        wid = _sc_worker_id()
        base = wid * per_w
        setup = [pltpu.make_async_copy(idx_hbm.at[:, pl.ds(base, per_w)], idx_v, psem.at[0]),
                 pltpu.make_async_copy(w_hbm.at[:, pl.ds(base, per_w)], w_v, psem.at[1])]
        for cp in setup:
            cp.start()

        def gets(j, slot):
            return [pltpu.make_async_copy(ys_hbm.at[idx_v.at[kk, pl.ds(j * grp, grp)]],
                                          rows_v.at[slot, pl.ds(kk * grp, grp)], gsem.at[slot])
                    for kk in range(n_slots)]

        def xload(j, slot):
            return pltpu.make_async_copy(xres_hbm.at[pl.ds(base + j * grp, grp)], acc_v.at[slot],
                                         xsem.at[slot])

        def put(j, slot):
            return pltpu.make_async_copy(acc_v.at[slot], out_hbm.at[pl.ds(base + j * grp, grp)],
                                         psem.at[slot])

        def accumulate(j, rslot, aslot):
            @pl.loop(0, grp)
            def _(g):
                pos = jnp.full((SC_LANES,), j * grp + g, jnp.int32)
                wk = [plsc.load_gather(w_v, [jnp.full((SC_LANES,), kk, jnp.int32), pos])
                      for kk in range(n_slots)]

                @plsc.parallel_loop(0, width // SC_LANES)
                def _(v):
                    cols_a = pl.ds(v * SC_LANES, SC_LANES)
                    cols_b = pl.ds(width + v * SC_LANES, SC_LANES)
                    acc_a = acc_v[aslot, g, cols_a]
                    acc_b = acc_v[aslot, g, cols_b]
                    for kk in range(n_slots):
                        ya, yb = _unpack_pair(rows_v[rslot, kk * grp + g, cols_a])
                        acc_a = acc_a + ya * wk[kk]
                        acc_b = acc_b + yb * wk[kk]
                    acc_v[aslot, g, cols_a] = acc_a
                    acc_v[aslot, g, cols_b] = acc_b

        xload(0, 0).start()
        for cp in setup:
            cp.wait()
        for cp in gets(0, 0):
            cp.start()

        @pl.loop(0, n_sub, step=r_acc)
        def _(j):
            for b in range(r_acc):
                jj = j + b
                rs = b % r_rows
                na = (b + 1) % r_acc
                for cp in gets(jj, rs):
                    cp.wait()

                @pl.when(jj + 1 < n_sub)
                def _():
                    for cp in gets(jj + 1, (b + 1) % r_rows):
                        cp.start()
                xload(jj, b).wait()

                @pl.when(jj + 1 < n_sub)
                def _():
                    @pl.when(jj + 1 >= r_acc)
                    def _():
                        put(jj + 1 - r_acc, na).wait()
                    xload(jj + 1, na).start()
                accumulate(jj, rs, b)
                put(jj, b).start()

        for p in range(r_acc):
            put(n_sub - r_acc + p, p).wait()

    return k(ys, dest_t, wsel_t, xres)


def _experts_kernel(n_blk, blk_e_ref, next_e_ref, n_used_ref,
                    xs_hbm, eg_hbm, eu_hbm, ed_hbm, ys_hbm,
                    xbuf, ybuf, hid, stg_g, stg_u, stg_d, wg, wu, wd, xsem, ysem, wsem):
    n = n_used_ref[0]

    def ring(b):
        return jnp.bitwise_and(b, ROW_RING - 1)

    def x_copy(b):
        return pltpu.make_async_copy(xs_hbm.at[pl.ds(pl.multiple_of(b * ROW_BLOCK, ROW_BLOCK),
                                                     ROW_BLOCK)], xbuf.at[ring(b)], xsem.at[ring(b)])

    def y_copy(b):
        return pltpu.make_async_copy(ybuf.at[ring(b)],
                                     ys_hbm.at[pl.ds(pl.multiple_of(b * ROW_BLOCK, ROW_BLOCK),
                                                     ROW_BLOCK)], ysem.at[ring(b)])

    def w_copies(e):
        return (pltpu.make_async_copy(eg_hbm.at[e], stg_g, wsem.at[0]),
                pltpu.make_async_copy(eu_hbm.at[e], stg_u, wsem.at[1]),
                pltpu.make_async_copy(ed_hbm.at[e], stg_d, wsem.at[2]))

    def switch_expert(e, wslot):
        for cp in w_copies(e):
            cp.wait()
        wg[wslot] = stg_g[...].astype(BF16)
        wu[wslot] = stg_u[...].astype(BF16)
        wd[wslot] = stg_d[...].astype(BF16)
        nxt = next_e_ref[e]

        @pl.when(nxt >= 0)
        def _():
            for cp in w_copies(nxt):
                cp.start()

    def gate_up(b, wslot):
        xa, xb = _unpack_pair(xbuf[ring(b)])
        xb16 = jnp.concatenate([xa, xb], axis=1).astype(BF16)
        g = _dot(xb16, wg[wslot])
        up = _dot(xb16, wu[wslot])
        hid[jnp.bitwise_and(b, 1)] = (g * _sigmoid(g) * up).astype(BF16)

    def down(b, wslot):
        y = _dot(hid[jnp.bitwise_and(b, 1)], wd[wslot])
        ybuf[ring(b)] = _pack_pair(y[:, 0:HALF], y[:, HALF:D_MODEL])

    e0 = blk_e_ref[0]
    for cp in w_copies(e0):
        cp.start()
    for j in range(ROW_RING):
        @pl.when(j < n)
        def _():
            x_copy(j).start()
    switch_expert(e0, 0)
    x_copy(0).wait()
    gate_up(0, 0)

    def body(b, wslot_prev):
        e = blk_e_ref[b]
        first = e != blk_e_ref[b - 1]
        wslot = jnp.where(first, 1 - wslot_prev, wslot_prev)

        @pl.when(first)
        def _():
            switch_expert(e, wslot)

        x_copy(b).wait()

        @pl.when(b + ROW_RING - 1 < n)
        def _():
            x_copy(b + ROW_RING - 1).start()

        @pl.when(b >= ROW_RING + 1)
        def _():
            y_copy(b - 1 - ROW_RING).wait()

        down(b - 1, wslot_prev)
        gate_up(b, wslot)
        y_copy(b - 1).start()
        return wslot

    wslot_last = lax.fori_loop(1, n, body, jnp.int32(0))

    last = n - 1

    @pl.when(last >= ROW_RING)
    def _():
        y_copy(last - ROW_RING).wait()
    down(last, wslot_last)
    y_copy(last).start()
    for j in range(ROW_RING - 1, -1, -1):
        @pl.when(last - j >= 0)
        def _():
            y_copy(last - j).wait()

    ybuf[0] = jnp.zeros((ROW_BLOCK, HALF), U32)

    def zero_tail(b, c):
        cp = pltpu.make_async_copy(ybuf.at[0],
                                   ys_hbm.at[pl.ds(pl.multiple_of(b * ROW_BLOCK, ROW_BLOCK),
                                                   ROW_BLOCK)], ysem.at[0])
        cp.start()
        cp.wait()
        return c
    lax.fori_loop(n, n_blk, zero_tail, 0)


def _experts(blk_e, next_e, n_used, xs, e_gate, e_up, e_down):
    n_rows = xs.shape[0]
    n_blk = n_rows // ROW_BLOCK
    any_spec = pl.BlockSpec(memory_space=pl.ANY)
    grid_spec = pltpu.PrefetchScalarGridSpec(
        num_scalar_prefetch=3,
        grid=(1,),
        in_specs=[any_spec, any_spec, any_spec, any_spec],
        out_specs=any_spec,
        scratch_shapes=[pltpu.VMEM((ROW_RING, ROW_BLOCK, HALF), U32),
                        pltpu.VMEM((ROW_RING, ROW_BLOCK, HALF), U32),
                        pltpu.VMEM((2, ROW_BLOCK, EXPERT_HIDDEN), BF16),
                        pltpu.VMEM((D_MODEL, EXPERT_HIDDEN), F32),
                        pltpu.VMEM((D_MODEL, EXPERT_HIDDEN), F32),
                        pltpu.VMEM((EXPERT_HIDDEN, D_MODEL), F32),
                        pltpu.VMEM((2, D_MODEL, EXPERT_HIDDEN), BF16),
                        pltpu.VMEM((2, D_MODEL, EXPERT_HIDDEN), BF16),
                        pltpu.VMEM((2, EXPERT_HIDDEN, D_MODEL), BF16),
                        pltpu.SemaphoreType.DMA((ROW_RING,)),
                        pltpu.SemaphoreType.DMA((ROW_RING,)),
                        pltpu.SemaphoreType.DMA((3,))],
    )
    return pl.pallas_call(
        functools.partial(_experts_kernel, n_blk),
        grid_spec=grid_spec,
        out_shape=jax.ShapeDtypeStruct((n_rows, HALF), U32),
        compiler_params=pltpu.CompilerParams(dimension_semantics=("arbitrary",)),
        name="experts",
    )(blk_e, next_e, n_used, xs, e_gate, e_up, e_down)


def _final_norm_kernel(x_ref, g_ref, out_ref):
    out_ref[...] = _rms(x_ref[...], g_ref[...])


def _final_norm(xsum, g_final):
    t_tok = xsum.shape[0]
    blk = pl.BlockSpec((TM_NORM, D_MODEL), lambda i: (i, 0))
    return pl.pallas_call(
        _final_norm_kernel,
        grid=(t_tok // TM_NORM,),
        in_specs=[blk, pl.BlockSpec((1, D_MODEL), lambda i: (0, 0))],
        out_specs=blk,
        out_shape=jax.ShapeDtypeStruct((t_tok, D_MODEL), F32),
        compiler_params=pltpu.CompilerParams(dimension_semantics=("arbitrary",)),
        name="final_norm",
    )(xsum, g_final)


def kernel(x, g_mix, w_in, b_gate, w_pool_group, pool_scale, w_pool_out, conv_w, w_conv_out, w_o,
           g_ffn, w_router, router_bias, e_gate, e_up, e_down, s_gate, s_up, s_down, g_final):
    b, s, d = x.shape
    t_tok = b * s
    n_pad = N_EXPERTS * ROW_BLOCK
    n_rows = t_tok * TOP_K + n_pad
    n_blk = n_rows // ROW_BLOCK
    assert d == D_MODEL and s % TM_MIX == 0 and TM_MIX >= POOL_HALO
    assert t_tok % TM_DEST == 0 and t_tok % TM_NORM == 0 and n_pad % PLAN_LANES == 0
    assert t_tok % (2 * SC_WORKERS * SC_CHUNK) == 0 and n_pad % (SC_WORKERS * SC_CHUNK) == 0
    assert t_tok % (SC_WORKERS * SC_SUM_GROUP * SC_SUM_ACC_RING) == 0

    row = lambda a: a.reshape(1, -1)
    wr_t = w_router.T.astype(F32)
    wr_hi = wr_t.astype(BF16)
    wr = jnp.concatenate([wr_hi, (wr_t - wr_hi.astype(F32)).astype(BF16)], axis=0)

    xres, h2p, idx_t, wsel_t, rank_t, counts = _mixer_router(
        x.reshape(t_tok, d), s, row(g_mix), w_in, row(b_gate), w_pool_group,
        row(pool_scale), w_pool_out, conv_w, w_conv_out, w_o, row(g_ffn), wr,
        router_bias.astype(F32).reshape(N_EXPERTS, 1), s_gate, s_up, s_down)
    dest_t, blk_e, next_e, n_used, n_zero, zero_rows = _plan(counts, idx_t, rank_t, n_blk, n_pad)
    xs = _dispatch(h2p, dest_t, zero_rows, n_zero, n_rows)
    ys = _experts(blk_e, next_e, n_used, xs, e_gate, e_up, e_down)
    xsum = _regroup_sum(ys, dest_t, wsel_t, xres)
    return _final_norm(xsum, row(g_final)).reshape(b, s, d)
```

```python
import functools

import jax
import jax.numpy as jnp
from jax import lax
from jax.experimental import pallas as pl
from jax.experimental.pallas import tpu as pltpu
from jax.experimental.pallas import tpu_sc as plsc

D_MODEL = 1024
HALF = D_MODEL // 2
POOL_WIDTH = 512
N_POOL_GROUPS = 4
POOL_GROUP = 128
POOL_WINDOWS = (2, 4, 8, 16)
CONV_WIDTH = 512
N_EXPERTS = 64
TOP_K = 8
EXPERT_HIDDEN = 256
SHARED_HIDDEN = 256
ROUTED_SCALE = 2.5
EPS = 1e-6

POOL_HALO = 16
CONV_HALO = 8
TM_MIX = 512
W_IN_CHUNK = 128
TM_DEST = 4096
PLAN_LANES = 2048
ROW_BLOCK = 512
ROW_RING = 8
TM_NORM = 2048
TM_SHARED = 1024
VMEM_LIMIT = 56 * 1024 * 1024

SC_CORES = 2
SC_SUBCORES = 16
SC_WORKERS = SC_CORES * SC_SUBCORES
SC_LANES = 16
SC_CHUNK = 64
SC_SUM_GROUP = 8
SC_SUM_ROW_RING = 2
SC_SUM_ACC_RING = 4

BF16 = jnp.bfloat16
F32 = jnp.float32
U32 = jnp.uint32


def _rms(x, g):
    r = lax.rsqrt(jnp.mean(x * x, axis=-1, keepdims=True) + EPS)
    return (x * r) * g


def _dot(a, b):
    return jnp.dot(a, b, preferred_element_type=F32)


def _sigmoid(z):
    return 0.5 * jnp.tanh(0.5 * z) + 0.5


def _pack_pair(a, b):
    ra = lax.bitcast_convert_type(a.astype(BF16).astype(F32), U32)
    rb = lax.bitcast_convert_type(b.astype(BF16).astype(F32), U32)
    return ra | (rb >> 16)


def _unpack_pair(w):
    a = lax.bitcast_convert_type(w & jnp.uint32(0xFFFF0000), F32)
    b = lax.bitcast_convert_type(w << 16, F32)
    return a, b


def _load_weights_bf16(w_in_hbm, w_grp_hbm, w_po_hbm, w_co_hbm, w_o_hbm, s_gate_hbm, s_up_hbm,
                       s_down_hbm, w_in_ref, w_grp_ref, w_po_ref, w_co_ref, w_o_ref, s_gate_ref,
                       s_up_ref, s_down_ref, stg_in, stg_sq, stg_sh, stg_grp, wsem):
    copy = lambda src, dst, k: pltpu.make_async_copy(src, dst, wsem.at[k])
    rows = stg_in.shape[1]
    n_in = w_in_hbm.shape[0] // rows
    half = w_po_hbm.shape[0]
    c_in = [copy(w_in_hbm.at[pl.ds(j * rows, rows)], stg_in.at[j % 2], j % 2) for j in range(n_in)]
    c_o = copy(w_o_hbm, stg_sq, 2)
    c_sg = copy(s_gate_hbm, stg_sh.at[0], 3)
    c_su = copy(s_up_hbm, stg_sh.at[1], 4)
    c_grp = copy(w_grp_hbm, stg_grp, 5)
    c_po = copy(w_po_hbm, stg_sq.at[0:half], 2)
    c_co = copy(w_co_hbm, stg_sq.at[half:2 * half], 6)
    c_sd = copy(s_down_hbm, stg_sq.at[0:s_down_hbm.shape[0]], 2)
    for cp in (c_in[0], c_in[1], c_o, c_sg, c_su, c_grp):
        cp.start()
    for j in range(n_in):
        c_in[j].wait()
        w_in_ref[j * rows:(j + 1) * rows, :] = stg_in[j % 2].astype(BF16)
        if j + 2 < n_in:
            c_in[j + 2].start()
    c_o.wait()
    w_o_ref[...] = stg_sq[...].astype(BF16)
    c_po.start()
    c_co.start()
    c_sg.wait()
    s_gate_ref[...] = stg_sh[0].astype(BF16)
    c_su.wait()
    s_up_ref[...] = stg_sh[1].astype(BF16)
    c_grp.wait()
    w_grp_ref[...] = stg_grp[...].astype(BF16)
    c_po.wait()
    w_po_ref[...] = stg_sq[0:half, :].astype(BF16)
    c_co.wait()
    w_co_ref[...] = stg_sq[half:2 * half, :].astype(BF16)
    c_sd.start()
    c_sd.wait()
    s_down_ref[...] = stg_sq[0:s_down_hbm.shape[0], :].astype(BF16)


def _mixer_router_kernel(n_seq_tiles,
                         x_ref, g_mix_ref, w_in_hbm, b_gate_ref, w_grp_hbm, pool_scale_ref,
                         w_po_hbm, conv_w_ref, w_co_hbm, w_o_hbm, g_ffn_ref,
                         wr_ref, rbias_ref, s_gate_hbm, s_up_hbm, s_down_hbm,
                         xres_ref, h2p_ref, idx_ref, wsel_ref, rank_ref, counts_ref,
                         ext_pool, ext_conv, cnt_carry, tri,
                         w_in_ref, w_grp_ref, w_po_ref, w_co_ref, w_o_ref, s_gate_ref, s_up_ref,
                         s_down_ref, stg_in, stg_sq, stg_sh, stg_grp, wsem):
    tm = x_ref.shape[0]
    i = pl.program_id(0)
    st = i % n_seq_tiles

    @pl.when(i == 0)
    def _():
        _load_weights_bf16(w_in_hbm, w_grp_hbm, w_po_hbm, w_co_hbm, w_o_hbm, s_gate_hbm, s_up_hbm,
                           s_down_hbm, w_in_ref, w_grp_ref, w_po_ref, w_co_ref, w_o_ref,
                           s_gate_ref, s_up_ref, s_down_ref, stg_in, stg_sq, stg_sh, stg_grp, wsem)
        r = lax.broadcasted_iota(jnp.int32, (tm, tm), 0)
        c = lax.broadcasted_iota(jnp.int32, (tm, tm), 1)
        tri[...] = (r < c).astype(BF16)
        cnt_carry[...] = jnp.zeros_like(cnt_carry)

    @pl.when(st == 0)
    def _():
        ext_pool[0:POOL_HALO, :] = jnp.zeros((POOL_HALO, POOL_WIDTH), F32)
        ext_conv[0:CONV_HALO, :] = jnp.zeros((CONV_HALO, CONV_WIDTH), F32)

    x = x_ref[...]
    hb = _rms(x, g_mix_ref[...]).astype(BF16)

    o0 = POOL_WIDTH
    o1 = o0 + CONV_WIDTH
    o2 = o1 + CONV_WIDTH
    o3 = o2 + CONV_WIDTH

    u = _dot(hb, w_in_ref[:, 0:o0])
    ext_pool[POOL_HALO:POOL_HALO + tm, :] = u
    gc = _dot(hb, w_in_ref[:, o1:o2])
    v = _dot(hb, w_in_ref[:, o2:o3])
    pre_a = _dot(hb, w_in_ref[:, o3:o3 + D_MODEL])

    t_glob = st * tm + lax.broadcasted_iota(jnp.int32, (tm, 1), 0)
    mixed = []
    for gi, w in enumerate(POOL_WINDOWS):
        cols = slice(gi * POOL_GROUP, (gi + 1) * POOL_GROUP)
        ug = u[:, cols]
        acc = ug
        for j in range(1, w):
            acc = acc + ext_pool[POOL_HALO - j:POOL_HALO - j + tm, cols]
        cnt = jnp.minimum(t_glob + 1, w).astype(F32)
        pooled = acc * (1.0 / cnt) - ug
        mixed.append(_dot(pooled.astype(BF16), w_grp_ref[gi]))
    ext_pool[0:POOL_HALO, :] = ext_pool[tm:tm + POOL_HALO, :]
    pre_b = _dot(hb, w_in_ref[:, o3 + D_MODEL:o3 + 2 * D_MODEL])
    gb = _dot(hb, w_in_ref[:, o0:o1])

    cv = gc * v
    ext_conv[CONV_HALO:CONV_HALO + tm, :] = cv
    conv = (ext_conv[CONV_HALO - 2:CONV_HALO - 2 + tm, :] * conv_w_ref[0:1, :]
            + ext_conv[CONV_HALO - 1:CONV_HALO - 1 + tm, :] * conv_w_ref[1:2, :]
            + cv * conv_w_ref[2:3, :])
    ext_conv[0:CONV_HALO, :] = ext_conv[tm:tm + CONV_HALO, :]
    branch_b = _dot((gb * conv).astype(BF16), w_co_ref[...])
    mixed = jnp.concatenate(mixed, axis=1) * pool_scale_ref[...]
    branch_a = _dot(mixed.astype(BF16), w_po_ref[...])

    merged = (_sigmoid(pre_a + b_gate_ref[:, 0:D_MODEL]) * branch_a
              + _sigmoid(pre_b + b_gate_ref[:, D_MODEL:2 * D_MODEL]) * branch_b)
    x1 = x + _dot(merged.astype(BF16), w_o_ref[...])

    h2 = _rms(x1, g_ffn_ref[...])
    h2p_ref[...] = _pack_pair(h2[:, 0:HALF], h2[:, HALF:D_MODEL])
    h2b = h2.astype(BF16)

    nt = (((1,), (1,)), ((), ()))
    parts = lax.dot_general(wr_ref[...], h2b, nt, preferred_element_type=F32)
    logits = parts[0:N_EXPERTS, :] + parts[N_EXPERTS:2 * N_EXPERTS, :]
    xres_ref[...] = x1
    scores = jax.nn.sigmoid(logits)
    sel = scores + rbias_ref[...]
    eidx = lax.broadcasted_iota(jnp.int32, (N_EXPERTS, tm), 0).astype(F32)
    e_rows, w_rows = [], []
    mask = jnp.zeros((N_EXPERTS, tm), F32)
    for _ in range(TOP_K):
        m = jnp.max(sel, axis=0, keepdims=True)
        ek = jnp.min(jnp.where(sel == m, eidx, float(N_EXPERTS)), axis=0, keepdims=True)
        oh = eidx == ek
        w_rows.append(jnp.sum(jnp.where(oh, scores, 0.0), axis=0, keepdims=True))
        e_rows.append(ek)
        mask = mask + oh.astype(F32)
        sel = jnp.where(oh, -jnp.inf, sel)

    wsum = w_rows[0]
    for k in range(1, TOP_K):
        wsum = wsum + w_rows[k]

    before = _dot(mask.astype(BF16), tri[...]) + cnt_carry[...]
    for k in range(TOP_K):
        oh = eidx == e_rows[k]
        rank_ref[k:k + 1, :] = jnp.sum(jnp.where(oh, before, 0.0), axis=0,
                                       keepdims=True).astype(jnp.int32)
        idx_ref[k:k + 1, :] = e_rows[k].astype(jnp.int32)
        wsel_ref[k:k + 1, :] = w_rows[k] / wsum * ROUTED_SCALE
    total = cnt_carry[...] + jnp.sum(mask, axis=1, keepdims=True)
    cnt_carry[...] = total
    counts_ref[...] = total.astype(jnp.int32)


def _mixer_router(x2d, seq_len, g_mix, w_in, b_gate, w_grp, pool_scale, w_po, conv_w,
                  w_co, w_o, g_ffn, wr, rbias, s_gate, s_up, s_down):
    t_tok = x2d.shape[0]
    tm = TM_MIX
    n_seq_tiles = seq_len // tm
    const = lambda shape: pl.BlockSpec(shape, lambda i: (0,) * len(shape),
                                       pipeline_mode=pl.Buffered(1))
    hbm = pl.BlockSpec(memory_space=pl.ANY)
    row_blk = pl.BlockSpec((tm, D_MODEL), lambda i: (i, 0))
    half_blk = pl.BlockSpec((tm, HALF), lambda i: (i, 0))
    slot_blk = pl.BlockSpec((TOP_K, tm), lambda i: (0, i))
    return pl.pallas_call(
        functools.partial(_mixer_router_kernel, n_seq_tiles),
        grid=(t_tok // tm,),
        in_specs=[row_blk, const(g_mix.shape), hbm, const(b_gate.shape),
                  hbm, const(pool_scale.shape), hbm,
                  const(conv_w.shape), hbm, hbm, const(g_ffn.shape),
                  const(wr.shape), const(rbias.shape), hbm, hbm, hbm],
        out_specs=[row_blk, half_blk, slot_blk, slot_blk, slot_blk,
                   pl.BlockSpec((N_EXPERTS, 1), lambda i: (0, 0))],
        out_shape=[jax.ShapeDtypeStruct((t_tok, D_MODEL), F32),
                   jax.ShapeDtypeStruct((t_tok, HALF), U32),
                   jax.ShapeDtypeStruct((TOP_K, t_tok), jnp.int32),
                   jax.ShapeDtypeStruct((TOP_K, t_tok), F32),
                   jax.ShapeDtypeStruct((TOP_K, t_tok), jnp.int32),
                   jax.ShapeDtypeStruct((N_EXPERTS, 1), jnp.int32)],
        scratch_shapes=[pltpu.VMEM((POOL_HALO + tm, POOL_WIDTH), F32),
                        pltpu.VMEM((CONV_HALO + tm, CONV_WIDTH), F32),
                        pltpu.VMEM((N_EXPERTS, 1), F32),
                        pltpu.VMEM((tm, tm), BF16)]
                       + [pltpu.VMEM(w.shape, BF16)
                          for w in (w_in, w_grp, w_po, w_co, w_o, s_gate, s_up, s_down)]
                       + [pltpu.VMEM((2, W_IN_CHUNK, w_in.shape[1]), F32),
                          pltpu.VMEM(w_o.shape, F32),
                          pltpu.VMEM((2,) + s_gate.shape, F32),
                          pltpu.VMEM(w_grp.shape, F32),
                          pltpu.SemaphoreType.DMA((7,))],
        compiler_params=pltpu.CompilerParams(dimension_semantics=("arbitrary",),
                                             vmem_limit_bytes=VMEM_LIMIT),
        name="mixer_router",
    )(x2d, g_mix, w_in, b_gate, w_grp, pool_scale, w_po, conv_w, w_co, w_o, g_ffn,
      wr, rbias, s_gate, s_up, s_down)


def _plan_kernel(n_blk, n_pad, counts_ref, idx_ref, rank_ref,
                 dest_ref, blk_e_ref, next_e_ref, n_used_ref, n_zero_ref, zero_rows_ref, pad_start):
    e_n = N_EXPERTS

    @pl.when(pl.program_id(0) == 0)
    def _():
        sub = lax.broadcasted_iota(jnp.int32, (e_n, e_n), 0)
        lane = lax.broadcasted_iota(jnp.int32, (e_n, e_n), 1)
        c_col = counts_ref[...]
        p_col = ((c_col + (ROW_BLOCK - 1)) // ROW_BLOCK) * ROW_BLOCK
        c_f = c_col.astype(F32)
        p_f = p_col.astype(F32)
        gap_f = p_f - c_f
        to_row = lambda col: jnp.sum(jnp.where(sub == lane, col, 0.0), axis=0, keepdims=True)
        p_row = to_row(p_f)
        gap_row = to_row(gap_f)
        pad_end_col = jnp.sum(jnp.where(lane <= sub, p_row, 0.0), axis=1, keepdims=True)
        gap_before_col = jnp.sum(jnp.where(lane < sub, gap_row, 0.0), axis=1, keepdims=True)
        pad_start[...] = pad_end_col - p_f
        pad_end_last = jnp.sum(p_row, axis=1, keepdims=True)
        n_used_ref[...] = jnp.broadcast_to(pad_end_last * (1.0 / ROW_BLOCK),
                                           n_used_ref.shape).astype(jnp.int32)
        gap_total = jnp.sum(gap_row, axis=1, keepdims=True)
        n_zero_ref[...] = jnp.broadcast_to(jnp.floor((gap_total + (SC_CHUNK - 1))
                                                     * (1.0 / SC_CHUNK)),
                                           n_zero_ref.shape).astype(jnp.int32)
        row0 = (lax.broadcasted_iota(jnp.int32, (e_n, blk_e_ref.shape[1]), 1)
                * ROW_BLOCK).astype(F32)
        owner = jnp.sum(jnp.where(pad_end_col <= row0, 1.0, 0.0), axis=0, keepdims=True)
        blk_e_ref[...] = jnp.minimum(owner, float(e_n - 1)).astype(jnp.int32)
        used_later = jnp.logical_and(sub > lane, p_f > 0.0)
        nxt = jnp.min(jnp.where(used_later, sub, e_n), axis=0, keepdims=True)
        next_e_ref[...] = jnp.full(next_e_ref.shape, -1, jnp.int32)
        next_e_ref[:, 0:e_n] = jnp.where(nxt < e_n, nxt, -1)
        for j0 in range(0, n_pad, PLAN_LANES):
            j = (j0 + lax.broadcasted_iota(jnp.int32, (e_n, PLAN_LANES), 1)).astype(F32)
            before = jnp.sum(jnp.where(gap_before_col <= j, c_f, 0.0), axis=0, keepdims=True)
            zero_rows_ref[:, j0:j0 + PLAN_LANES] = (j[0:1, :] + before).astype(jnp.int32)

    tm = idx_ref.shape[1]
    eidx = lax.broadcasted_iota(jnp.int32, (e_n, tm), 0)
    ps = pad_start[...]
    for k in range(TOP_K):
        oh = eidx == idx_ref[k:k + 1, :]
        start = jnp.sum(jnp.where(oh, ps, 0.0), axis=0, keepdims=True)
        dest_ref[k:k + 1, :] = start.astype(jnp.int32) + rank_ref[k:k + 1, :]


def _plan(counts, idx_t, rank_t, n_blk, n_pad):
    t_tok = idx_t.shape[1]
    lanes = 128
    n_blk_p = -(-n_blk // lanes) * lanes
    slot_blk = pl.BlockSpec((TOP_K, TM_DEST), lambda i: (0, i))
    whole = lambda n: pl.BlockSpec((1, n), lambda i: (0, 0))
    dest_t, blk_e, next_e, n_used, n_zero, zero_rows = pl.pallas_call(
        functools.partial(_plan_kernel, n_blk, n_pad),
        grid=(t_tok // TM_DEST,),
        in_specs=[pl.BlockSpec((N_EXPERTS, 1), lambda i: (0, 0)), slot_blk, slot_blk],
        out_specs=[slot_blk, whole(n_blk_p), whole(lanes), whole(lanes), whole(lanes),
                   whole(n_pad)],
        out_shape=[jax.ShapeDtypeStruct((TOP_K, t_tok), jnp.int32),
                   jax.ShapeDtypeStruct((1, n_blk_p), jnp.int32),
                   jax.ShapeDtypeStruct((1, lanes), jnp.int32),
                   jax.ShapeDtypeStruct((1, lanes), jnp.int32),
                   jax.ShapeDtypeStruct((1, lanes), jnp.int32),
                   jax.ShapeDtypeStruct((1, n_pad), jnp.int32)],
        scratch_shapes=[pltpu.VMEM((N_EXPERTS, 1), F32)],
        compiler_params=pltpu.CompilerParams(dimension_semantics=("arbitrary",)),
        name="plan",
    )(counts, idx_t, rank_t)
    return (dest_t, blk_e.reshape(-1), next_e.reshape(-1), n_used.reshape(-1), n_zero.reshape(-1),
            zero_rows.reshape(-1))


def _sc_mesh():
    return plsc.VectorSubcoreMesh(core_axis_name="c", subcore_axis_name="s")


def _sc_worker_id():
    return lax.axis_index("s") * SC_CORES + lax.axis_index("c")


def _dispatch(h2p, dest_t, zero_rows, n_zero, n_rows):
    t_tok, width = h2p.shape
    per_w = t_tok // SC_WORKERS
    n_chunks = per_w // SC_CHUNK
    z_chunks = zero_rows.shape[0] // (SC_WORKERS * SC_CHUNK)
    dest_w = (dest_t.reshape(TOP_K, SC_WORKERS, n_chunks, SC_CHUNK)
              .transpose(1, 2, 0, 3).reshape(SC_WORKERS * n_chunks * TOP_K, SC_CHUNK))
    zero_w = zero_rows.reshape(z_chunks * SC_WORKERS, SC_CHUNK)

    @functools.partial(
        pl.kernel, mesh=_sc_mesh(),
        out_type=jax.ShapeDtypeStruct((n_rows, width), h2p.dtype),
        scratch_types=[pltpu.VMEM((n_chunks * TOP_K, SC_CHUNK), jnp.int32),
                       pltpu.VMEM((z_chunks, SC_CHUNK), jnp.int32),
                       pltpu.VMEM((SC_LANES,), jnp.int32),
                       pltpu.VMEM((2, SC_CHUNK, width), h2p.dtype),
                       pltpu.VMEM((SC_CHUNK, width), h2p.dtype),
                       pltpu.SemaphoreType.DMA, pltpu.SemaphoreType.DMA,
                       pltpu.SemaphoreType.DMA, pltpu.SemaphoreType.DMA],
        compiler_params=pltpu.CompilerParams(needs_layout_passes=False),
        name="dispatch",
    )
    def k(h2p_hbm, dest_hbm, zidx_hbm, nz_hbm, zsrc_hbm, xs_hbm, idx_v, zidx_v, nz_v, rows_v, zero_v,
          gsem, wsem, zsem, isem):
        wid = _sc_worker_id()
        base = wid * per_w
        setup = [pltpu.make_async_copy(zidx_hbm.at[pl.ds(j * SC_WORKERS + wid, 1)],
                                       zidx_v.at[pl.ds(j, 1)], isem) for j in range(z_chunks)]
        setup += [pltpu.make_async_copy(dest_hbm.at[pl.ds(wid * n_chunks * TOP_K, n_chunks * TOP_K)],
                                        idx_v, isem),
                  pltpu.make_async_copy(nz_hbm.at[pl.ds(0, SC_LANES)], nz_v, isem),
                  pltpu.make_async_copy(zsrc_hbm, zero_v, isem)]
        for cp in setup:
            cp.start()
        for cp in setup:
            cp.wait()
        n_zero_chunks = jnp.max(nz_v[...])

        def zput(j):
            return pltpu.make_async_copy(zero_v, xs_hbm.at[zidx_v.at[j]], zsem)

        for j in range(z_chunks):
            @pl.when(j * SC_WORKERS + wid < n_zero_chunks)
            def _():
                zput(j).start()

        def get(j, slot):
            return pltpu.make_async_copy(h2p_hbm.at[pl.ds(base + j * SC_CHUNK, SC_CHUNK)],
                                         rows_v.at[slot], gsem)

        def put(j, slot, kk):
            return pltpu.make_async_copy(rows_v.at[slot], xs_hbm.at[idx_v.at[j * TOP_K + kk]], wsem)

        get(0, 0).start()

        @pl.loop(0, n_chunks, step=2)
        def _(j):
            for b in range(2):
                jj = j + b
                get(jj, b).wait()

                @pl.when(jj >= 1)
                def _():
                    for kk in range(TOP_K):
                        put(jj - 1, 1 - b, kk).wait()

                @pl.when(jj + 1 < n_chunks)
                def _():
                    get(jj + 1, 1 - b).start()
                for kk in range(TOP_K):
                    put(jj, b, kk).start()

        for kk in range(TOP_K):
            put(n_chunks - 1, (n_chunks - 1) % 2, kk).wait()
        for j in range(z_chunks):
            @pl.when(j * SC_WORKERS + wid < n_zero_chunks)
            def _():
                zput(j).wait()

    return k(h2p, dest_w, zero_w, n_zero, jnp.zeros((SC_CHUNK, width), h2p.dtype))


def _regroup_sum(ys, dest_t, wsel_t, xres):
    n_slots, t_tok = dest_t.shape
    width = ys.shape[1]
    d_out = xres.shape[1]
    grp = SC_SUM_GROUP
    r_rows = SC_SUM_ROW_RING
    r_acc = SC_SUM_ACC_RING
    per_w = t_tok // SC_WORKERS
    n_sub = per_w // grp

    @functools.partial(
        pl.kernel, mesh=_sc_mesh(),
        out_type=jax.ShapeDtypeStruct((t_tok, d_out), F32),
        scratch_types=[pltpu.VMEM((n_slots, per_w), jnp.int32),
                       pltpu.VMEM((n_slots, per_w), F32),
                       pltpu.VMEM((r_rows, n_slots * grp, width), ys.dtype),
                       pltpu.VMEM((r_acc, grp, d_out), F32),
                       pltpu.SemaphoreType.DMA((r_rows,)), pltpu.SemaphoreType.DMA((r_acc,)),
                       pltpu.SemaphoreType.DMA((r_acc,))],
        compiler_params=pltpu.CompilerParams(needs_layout_passes=False),
        name="regroup_sum",
    )
    def k(ys_hbm, idx_hbm, w_hbm, xres_hbm, out_hbm, idx_v, w_v, rows_v, acc_v, gsem, xsem, psem):
        wid = _sc_worker_id()
        base = wid * per_w
        setup = [pltpu.make_async_copy(idx_hbm.at[:, pl.ds(base, per_w)], idx_v, psem.at[0]),
                 pltpu.make_async_copy(w_hbm.at[:, pl.ds(base, per_w)], w_v, psem.at[1])]
        for cp in setup:
            cp.start()

        def gets(j, slot):
            return [pltpu.make_async_copy(ys_hbm.at[idx_v.at[kk, pl.ds(j * grp, grp)]],
                                          rows_v.at[slot, pl.ds(kk * grp, grp)], gsem.at[slot])
                    for kk in range(n_slots)]

        def xload(j, slot):
            return pltpu.make_async_copy(xres_hbm.at[pl.ds(base + j * grp, grp)], acc_v.at[slot],
                                         xsem.at[slot])

        def put(j, slot):
            return pltpu.make_async_copy(acc_v.at[slot], out_hbm.at[pl.ds(base + j * grp, grp)],
                                         psem.at[slot])

        def accumulate(j, rslot, aslot):
            @pl.loop(0, grp)
            def _(g):
                pos = jnp.full((SC_LANES,), j * grp + g, jnp.int32)
                wk = [plsc.load_gather(w_v, [jnp.full((SC_LANES,), kk, jnp.int32), pos])
                      for kk in range(n_slots)]

                @plsc.parallel_loop(0, width // SC_LANES)
                def _(v):
                    cols_a = pl.ds(v * SC_LANES, SC_LANES)
                    cols_b = pl.ds(width + v * SC_LANES, SC_LANES)
                    acc_a = acc_v[aslot, g, cols_a]
                    acc_b = acc_v[aslot, g, cols_b]
                    for kk in range(n_slots):
                        ya, yb = _unpack_pair(rows_v[rslot, kk * grp + g, cols_a])
                        acc_a = acc_a + ya * wk[kk]
                        acc_b = acc_b + yb * wk[kk]
                    acc_v[aslot, g, cols_a] = acc_a
                    acc_v[aslot, g, cols_b] = acc_b

        xload(0, 0).start()
        for cp in setup:
            cp.wait()
        for cp in gets(0, 0):
            cp.start()

        @pl.loop(0, n_sub, step=r_acc)
        def _(j):
            for b in range(r_acc):
                jj = j + b
                rs = b % r_rows
                na = (b + 1) % r_acc
                for cp in gets(jj, rs):
                    cp.wait()

                @pl.when(jj + 1 < n_sub)
                def _():
                    for cp in gets(jj + 1, (b + 1) % r_rows):
                        cp.start()
                xload(jj, b).wait()

                @pl.when(jj + 1 < n_sub)
                def _():
                    @pl.when(jj + 1 >= r_acc)
                    def _():
                        put(jj + 1 - r_acc, na).wait()
                    xload(jj + 1, na).start()
                accumulate(jj, rs, b)
                put(jj, b).start()

        for p in range(r_acc):
            put(n_sub - r_acc + p, p).wait()

    return k(ys, dest_t, wsel_t, xres)


def _experts_kernel(n_blk, blk_e_ref, next_e_ref, n_used_ref,
                    xs_hbm, eg_hbm, eu_hbm, ed_hbm, ys_hbm,
                    xbuf, ybuf, hid, stg_g, stg_u, stg_d, wg, wu, wd, xsem, ysem, wsem):
    n = n_used_ref[0]

    def ring(b):
        return jnp.bitwise_and(b, ROW_RING - 1)

    def x_copy(b):
        return pltpu.make_async_copy(xs_hbm.at[pl.ds(pl.multiple_of(b * ROW_BLOCK, ROW_BLOCK),
                                                     ROW_BLOCK)], xbuf.at[ring(b)], xsem.at[ring(b)])

    def y_copy(b):
        return pltpu.make_async_copy(ybuf.at[ring(b)],
                                     ys_hbm.at[pl.ds(pl.multiple_of(b * ROW_BLOCK, ROW_BLOCK),
                                                     ROW_BLOCK)], ysem.at[ring(b)])

    def w_copies(e):
        return (pltpu.make_async_copy(eg_hbm.at[e], stg_g, wsem.at[0]),
                pltpu.make_async_copy(eu_hbm.at[e], stg_u, wsem.at[1]),
                pltpu.make_async_copy(ed_hbm.at[e], stg_d, wsem.at[2]))

    def switch_expert(e, wslot):
        for cp in w_copies(e):
            cp.wait()
        wg[wslot] = stg_g[...].astype(BF16)
        wu[wslot] = stg_u[...].astype(BF16)
        wd[wslot] = stg_d[...].astype(BF16)
        nxt = next_e_ref[e]

        @pl.when(nxt >= 0)
        def _():
            for cp in w_copies(nxt):
                cp.start()

    def gate_up(b, wslot):
        xa, xb = _unpack_pair(xbuf[ring(b)])
        xb16 = jnp.concatenate([xa, xb], axis=1).astype(BF16)
        g = _dot(xb16, wg[wslot])
        up = _dot(xb16, wu[wslot])
        hid[jnp.bitwise_and(b, 1)] = (g * _sigmoid(g) * up).astype(BF16)

    def down(b, wslot):
        y = _dot(hid[jnp.bitwise_and(b, 1)], wd[wslot])
        ybuf[ring(b)] = _pack_pair(y[:, 0:HALF], y[:, HALF:D_MODEL])

    e0 = blk_e_ref[0]
    for cp in w_copies(e0):
        cp.start()
    for j in range(ROW_RING):
        @pl.when(j < n)
        def _():
            x_copy(j).start()
    switch_expert(e0, 0)
    x_copy(0).wait()
    gate_up(0, 0)

    def body(b, wslot_prev):
        e = blk_e_ref[b]
        first = e != blk_e_ref[b - 1]
        wslot = jnp.where(first, 1 - wslot_prev, wslot_prev)

        @pl.when(first)
        def _():
            switch_expert(e, wslot)

        x_copy(b).wait()

        @pl.when(b + ROW_RING - 1 < n)
        def _():
            x_copy(b + ROW_RING - 1).start()

        @pl.when(b >= ROW_RING + 1)
        def _():
            y_copy(b - 1 - ROW_RING).wait()

        down(b - 1, wslot_prev)
        gate_up(b, wslot)
        y_copy(b - 1).start()
        return wslot

    wslot_last = lax.fori_loop(1, n, body, jnp.int32(0))

    last = n - 1

    @pl.when(last >= ROW_RING)
    def _():
        y_copy(last - ROW_RING).wait()
    down(last, wslot_last)
    y_copy(last).start()
    for j in range(ROW_RING - 1, -1, -1):
        @pl.when(last - j >= 0)
        def _():
            y_copy(last - j).wait()

    ybuf[0] = jnp.zeros((ROW_BLOCK, HALF), U32)

    def zero_tail(b, c):
        cp = pltpu.make_async_copy(ybuf.at[0],
                                   ys_hbm.at[pl.ds(pl.multiple_of(b * ROW_BLOCK, ROW_BLOCK),
                                                   ROW_BLOCK)], ysem.at[0])
        cp.start()
        cp.wait()
        return c
    lax.fori_loop(n, n_blk, zero_tail, 0)


def _experts(blk_e, next_e, n_used, xs, e_gate, e_up, e_down):
    n_rows = xs.shape[0]
    n_blk = n_rows // ROW_BLOCK
    any_spec = pl.BlockSpec(memory_space=pl.ANY)
    grid_spec = pltpu.PrefetchScalarGridSpec(
        num_scalar_prefetch=3,
        grid=(1,),
        in_specs=[any_spec, any_spec, any_spec, any_spec],
        out_specs=any_spec,
        scratch_shapes=[pltpu.VMEM((ROW_RING, ROW_BLOCK, HALF), U32),
                        pltpu.VMEM((ROW_RING, ROW_BLOCK, HALF), U32),
                        pltpu.VMEM((2, ROW_BLOCK, EXPERT_HIDDEN), BF16),
                        pltpu.VMEM((D_MODEL, EXPERT_HIDDEN), F32),
                        pltpu.VMEM((D_MODEL, EXPERT_HIDDEN), F32),
                        pltpu.VMEM((EXPERT_HIDDEN, D_MODEL), F32),
                        pltpu.VMEM((2, D_MODEL, EXPERT_HIDDEN), BF16),
                        pltpu.VMEM((2, D_MODEL, EXPERT_HIDDEN), BF16),
                        pltpu.VMEM((2, EXPERT_HIDDEN, D_MODEL), BF16),
                        pltpu.SemaphoreType.DMA((ROW_RING,)),
                        pltpu.SemaphoreType.DMA((ROW_RING,)),
                        pltpu.SemaphoreType.DMA((3,))],
    )
    return pl.pallas_call(
        functools.partial(_experts_kernel, n_blk),
        grid_spec=grid_spec,
        out_shape=jax.ShapeDtypeStruct((n_rows, HALF), U32),
        compiler_params=pltpu.CompilerParams(dimension_semantics=("arbitrary",)),
        name="experts",
    )(blk_e, next_e, n_used, xs, e_gate, e_up, e_down)


def _shared_expert_kernel(h2p_ref, x1_ref, sg_ref, su_ref, sd_ref, out_ref):
    xa, xb = _unpack_pair(h2p_ref[...])
    h2b = jnp.concatenate([xa, xb], axis=1).astype(BF16)
    sg = _dot(h2b, sg_ref[...].astype(BF16))
    su = _dot(h2b, su_ref[...].astype(BF16))
    shared = _dot((sg * _sigmoid(sg) * su).astype(BF16), sd_ref[...].astype(BF16))
    out_ref[...] = x1_ref[...] + shared


def _shared_expert(h2p, x1, s_gate, s_up, s_down):
    t_tok = x1.shape[0]
    tm = TM_SHARED
    const = lambda a: pl.BlockSpec(a.shape, lambda i: (0, 0))
    return pl.pallas_call(
        _shared_expert_kernel,
        grid=(t_tok // tm,),
        in_specs=[pl.BlockSpec((tm, HALF), lambda i: (i, 0)),
                  pl.BlockSpec((tm, D_MODEL), lambda i: (i, 0)),
                  const(s_gate), const(s_up), const(s_down)],
        out_specs=pl.BlockSpec((tm, D_MODEL), lambda i: (i, 0)),
        out_shape=jax.ShapeDtypeStruct((t_tok, D_MODEL), F32),
        compiler_params=pltpu.CompilerParams(dimension_semantics=("arbitrary",)),
        name="shared_expert",
    )(h2p, x1, s_gate, s_up, s_down)


def _final_norm_kernel(x_ref, g_ref, out_ref):
    out_ref[...] = _rms(x_ref[...], g_ref[...])


def _final_norm(xsum, g_final):
    t_tok = xsum.shape[0]
    blk = pl.BlockSpec((TM_NORM, D_MODEL), lambda i: (i, 0))
    return pl.pallas_call(
        _final_norm_kernel,
        grid=(t_tok // TM_NORM,),
        in_specs=[blk, pl.BlockSpec((1, D_MODEL), lambda i: (0, 0))],
        out_specs=blk,
        out_shape=jax.ShapeDtypeStruct((t_tok, D_MODEL), F32),
        compiler_params=pltpu.CompilerParams(dimension_semantics=("arbitrary",)),
        name="final_norm",
    )(xsum, g_final)


def kernel(x, g_mix, w_in, b_gate, w_pool_group, pool_scale, w_pool_out, conv_w, w_conv_out, w_o,
           g_ffn, w_router, router_bias, e_gate, e_up, e_down, s_gate, s_up, s_down, g_final):
    b, s, d = x.shape
    t_tok = b * s
    n_pad = N_EXPERTS * ROW_BLOCK
    n_rows = t_tok * TOP_K + n_pad
    n_blk = n_rows // ROW_BLOCK
    assert d == D_MODEL and s % TM_MIX == 0 and TM_MIX >= POOL_HALO
    assert t_tok % TM_DEST == 0 and t_tok % TM_NORM == 0 and n_pad % PLAN_LANES == 0
    assert t_tok % (2 * SC_WORKERS * SC_CHUNK) == 0 and n_pad % (SC_WORKERS * SC_CHUNK) == 0
    assert t_tok % (SC_WORKERS * SC_SUM_GROUP * SC_SUM_ACC_RING) == 0

    row = lambda a: a.reshape(1, -1)
    wr_t = w_router.T.astype(F32)
    wr_hi = wr_t.astype(BF16)
    wr = jnp.concatenate([wr_hi, (wr_t - wr_hi.astype(F32)).astype(BF16)], axis=0)

    xres, h2p, idx_t, wsel_t, rank_t, counts = _mixer_router(
        x.reshape(t_tok, d), s, row(g_mix), w_in, row(b_gate), w_pool_group,
        row(pool_scale), w_pool_out, conv_w, w_conv_out, w_o, row(g_ffn), wr,
        router_bias.astype(F32).reshape(N_EXPERTS, 1), s_gate, s_up, s_down)
    dest_t, blk_e, next_e, n_used, n_zero, zero_rows = _plan(counts, idx_t, rank_t, n_blk, n_pad)
    xs = _dispatch(h2p, dest_t, zero_rows, n_zero, n_rows)
    xres = _shared_expert(h2p, xres, s_gate, s_up, s_down)
    ys = _experts(blk_e, next_e, n_used, xs, e_gate, e_up, e_down)
    xsum = _regroup_sum(ys, dest_t, wsel_t, xres)
    return _final_norm(xsum, row(g_final)).reshape(b, s, d)
```

```python
import functools

import jax
import jax.numpy as jnp
from jax import lax
from jax.experimental import pallas as pl
from jax.experimental.pallas import tpu as pltpu
from jax.experimental.pallas import tpu_sc as plsc

D_MODEL = 1024
HALF = D_MODEL // 2
POOL_WIDTH = 512
N_POOL_GROUPS = 4
POOL_GROUP = 128
POOL_WINDOWS = (2, 4, 8, 16)
CONV_WIDTH = 512
N_EXPERTS = 64
TOP_K = 8
EXPERT_HIDDEN = 256
SHARED_HIDDEN = 256
ROUTED_SCALE = 2.5
EPS = 1e-6

POOL_HALO = 16
CONV_HALO = 8
TM_MIX = 512
W_IN_CHUNK = 128
TM_DEST = 4096
PLAN_LANES = 2048
ROW_BLOCK = 512
ROW_RING = 8
VMEM_LIMIT = 56 * 1024 * 1024

SC_CORES = 2
SC_SUBCORES = 16
SC_WORKERS = SC_CORES * SC_SUBCORES
SC_LANES = 16
SC_CHUNK = 64
SC_SUM_GROUP = 8
SC_SUM_ROW_RING = 2
SC_SUM_ACC_RING = 4

BF16 = jnp.bfloat16
F32 = jnp.float32
U32 = jnp.uint32


def _rms(x, g):
    r = lax.rsqrt(jnp.mean(x * x, axis=-1, keepdims=True) + EPS)
    return (x * r) * g


def _dot(a, b):
    return jnp.dot(a, b, preferred_element_type=F32)


def _sigmoid(z):
    return 0.5 * jnp.tanh(0.5 * z) + 0.5


def _pack_pair(a, b):
    ra = lax.bitcast_convert_type(a.astype(BF16).astype(F32), U32)
    rb = lax.bitcast_convert_type(b.astype(BF16).astype(F32), U32)
    return ra | (rb >> 16)


def _unpack_pair(w):
    a = lax.bitcast_convert_type(w & jnp.uint32(0xFFFF0000), F32)
    b = lax.bitcast_convert_type(w << 16, F32)
    return a, b


def _load_weights_bf16(w_in_hbm, w_grp_hbm, w_po_hbm, w_co_hbm, w_o_hbm, s_gate_hbm, s_up_hbm,
                       s_down_hbm, w_in_ref, w_grp_ref, w_po_ref, w_co_ref, w_o_ref, s_gate_ref,
                       s_up_ref, s_down_ref, stg_in, stg_sq, stg_sh, stg_grp, wsem):
    copy = lambda src, dst, k: pltpu.make_async_copy(src, dst, wsem.at[k])
    rows = stg_in.shape[1]
    n_in = w_in_hbm.shape[0] // rows
    half = w_po_hbm.shape[0]
    c_in = [copy(w_in_hbm.at[pl.ds(j * rows, rows)], stg_in.at[j % 2], j % 2) for j in range(n_in)]
    c_o = copy(w_o_hbm, stg_sq, 2)
    c_sg = copy(s_gate_hbm, stg_sh.at[0], 3)
    c_su = copy(s_up_hbm, stg_sh.at[1], 4)
    c_grp = copy(w_grp_hbm, stg_grp, 5)
    c_po = copy(w_po_hbm, stg_sq.at[0:half], 2)
    c_co = copy(w_co_hbm, stg_sq.at[half:2 * half], 6)
    c_sd = copy(s_down_hbm, stg_sq.at[0:s_down_hbm.shape[0]], 2)
    for cp in (c_in[0], c_in[1], c_o, c_sg, c_su, c_grp):
        cp.start()
    for j in range(n_in):
        c_in[j].wait()
        w_in_ref[j * rows:(j + 1) * rows, :] = stg_in[j % 2].astype(BF16)
        if j + 2 < n_in:
            c_in[j + 2].start()
    c_o.wait()
    w_o_ref[...] = stg_sq[...].astype(BF16)
    c_po.start()
    c_co.start()
    c_sg.wait()
    s_gate_ref[...] = stg_sh[0].astype(BF16)
    c_su.wait()
    s_up_ref[...] = stg_sh[1].astype(BF16)
    c_grp.wait()
    w_grp_ref[...] = stg_grp[...].astype(BF16)
    c_po.wait()
    w_po_ref[...] = stg_sq[0:half, :].astype(BF16)
    c_co.wait()
    w_co_ref[...] = stg_sq[half:2 * half, :].astype(BF16)
    c_sd.start()
    c_sd.wait()
    s_down_ref[...] = stg_sq[0:s_down_hbm.shape[0], :].astype(BF16)


def _mixer_router_kernel(n_seq_tiles,
                         x_ref, g_mix_ref, w_in_hbm, b_gate_ref, w_grp_hbm, pool_scale_ref,
                         w_po_hbm, conv_w_ref, w_co_hbm, w_o_hbm, g_ffn_ref,
                         wr_ref, rbias_ref, s_gate_hbm, s_up_hbm, s_down_hbm,
                         xres_ref, h2p_ref, idx_ref, wsel_ref, rank_ref, counts_ref,
                         ext_pool, ext_conv, cnt_carry, tri,
                         w_in_ref, w_grp_ref, w_po_ref, w_co_ref, w_o_ref, s_gate_ref, s_up_ref,
                         s_down_ref, stg_in, stg_sq, stg_sh, stg_grp, wsem):
    tm = x_ref.shape[0]
    i = pl.program_id(0)
    st = i % n_seq_tiles

    @pl.when(i == 0)
    def _():
        _load_weights_bf16(w_in_hbm, w_grp_hbm, w_po_hbm, w_co_hbm, w_o_hbm, s_gate_hbm, s_up_hbm,
                           s_down_hbm, w_in_ref, w_grp_ref, w_po_ref, w_co_ref, w_o_ref,
                           s_gate_ref, s_up_ref, s_down_ref, stg_in, stg_sq, stg_sh, stg_grp, wsem)
        r = lax.broadcasted_iota(jnp.int32, (tm, tm), 0)
        c = lax.broadcasted_iota(jnp.int32, (tm, tm), 1)
        tri[...] = (r < c).astype(BF16)
        cnt_carry[...] = jnp.zeros_like(cnt_carry)

    @pl.when(st == 0)
    def _():
        ext_pool[0:POOL_HALO, :] = jnp.zeros((POOL_HALO, POOL_WIDTH), F32)
        ext_conv[0:CONV_HALO, :] = jnp.zeros((CONV_HALO, CONV_WIDTH), F32)

    x = x_ref[...]
    hb = _rms(x, g_mix_ref[...]).astype(BF16)

    o0 = POOL_WIDTH
    o1 = o0 + CONV_WIDTH
    o2 = o1 + CONV_WIDTH
    o3 = o2 + CONV_WIDTH

    u = _dot(hb, w_in_ref[:, 0:o0])
    ext_pool[POOL_HALO:POOL_HALO + tm, :] = u
    gc = _dot(hb, w_in_ref[:, o1:o2])
    v = _dot(hb, w_in_ref[:, o2:o3])
    pre_a = _dot(hb, w_in_ref[:, o3:o3 + D_MODEL])

    t_glob = st * tm + lax.broadcasted_iota(jnp.int32, (tm, 1), 0)
    mixed = []
    for gi, w in enumerate(POOL_WINDOWS):
        cols = slice(gi * POOL_GROUP, (gi + 1) * POOL_GROUP)
        ug = u[:, cols]
        acc = ug
        for j in range(1, w):
            acc = acc + ext_pool[POOL_HALO - j:POOL_HALO - j + tm, cols]
        cnt = jnp.minimum(t_glob + 1, w).astype(F32)
        pooled = acc * (1.0 / cnt) - ug
        mixed.append(_dot(pooled.astype(BF16), w_grp_ref[gi]))
    ext_pool[0:POOL_HALO, :] = ext_pool[tm:tm + POOL_HALO, :]
    pre_b = _dot(hb, w_in_ref[:, o3 + D_MODEL:o3 + 2 * D_MODEL])
    gb = _dot(hb, w_in_ref[:, o0:o1])

    cv = gc * v
    ext_conv[CONV_HALO:CONV_HALO + tm, :] = cv
    conv = (ext_conv[CONV_HALO - 2:CONV_HALO - 2 + tm, :] * conv_w_ref[0:1, :]
            + ext_conv[CONV_HALO - 1:CONV_HALO - 1 + tm, :] * conv_w_ref[1:2, :]
            + cv * conv_w_ref[2:3, :])
    ext_conv[0:CONV_HALO, :] = ext_conv[tm:tm + CONV_HALO, :]
    branch_b = _dot((gb * conv).astype(BF16), w_co_ref[...])
    mixed = jnp.concatenate(mixed, axis=1) * pool_scale_ref[...]
    branch_a = _dot(mixed.astype(BF16), w_po_ref[...])

    merged = (_sigmoid(pre_a + b_gate_ref[:, 0:D_MODEL]) * branch_a
              + _sigmoid(pre_b + b_gate_ref[:, D_MODEL:2 * D_MODEL]) * branch_b)
    x1 = x + _dot(merged.astype(BF16), w_o_ref[...])

    h2 = _rms(x1, g_ffn_ref[...])
    h2p_ref[...] = _pack_pair(h2[:, 0:HALF], h2[:, HALF:D_MODEL])
    h2b = h2.astype(BF16)

    nt = (((1,), (1,)), ((), ()))
    parts = lax.dot_general(wr_ref[...], h2b, nt, preferred_element_type=F32)
    logits = parts[0:N_EXPERTS, :] + parts[N_EXPERTS:2 * N_EXPERTS, :]
    sg = _dot(h2b, s_gate_ref[...])
    su = _dot(h2b, s_up_ref[...])
    scores = jax.nn.sigmoid(logits)
    sel = scores + rbias_ref[...]
    eidx = lax.broadcasted_iota(jnp.int32, (N_EXPERTS, tm), 0).astype(F32)
    e_rows, w_rows = [], []
    mask = jnp.zeros((N_EXPERTS, tm), F32)
    for _ in range(TOP_K):
        m = jnp.max(sel, axis=0, keepdims=True)
        ek = jnp.min(jnp.where(sel == m, eidx, float(N_EXPERTS)), axis=0, keepdims=True)
        oh = eidx == ek
        w_rows.append(jnp.sum(jnp.where(oh, scores, 0.0), axis=0, keepdims=True))
        e_rows.append(ek)
        mask = mask + oh.astype(F32)
        sel = jnp.where(oh, -jnp.inf, sel)

    shared = _dot((sg * _sigmoid(sg) * su).astype(BF16), s_down_ref[...])
    xres_ref[...] = x1 + shared

    wsum = w_rows[0]
    for k in range(1, TOP_K):
        wsum = wsum + w_rows[k]

    before = _dot(mask.astype(BF16), tri[...]) + cnt_carry[...]
    for k in range(TOP_K):
        oh = eidx == e_rows[k]
        rank_ref[k:k + 1, :] = jnp.sum(jnp.where(oh, before, 0.0), axis=0,
                                       keepdims=True).astype(jnp.int32)
        idx_ref[k:k + 1, :] = e_rows[k].astype(jnp.int32)
        wsel_ref[k:k + 1, :] = w_rows[k] / wsum * ROUTED_SCALE
    total = cnt_carry[...] + jnp.sum(mask, axis=1, keepdims=True)
    cnt_carry[...] = total
    counts_ref[...] = total.astype(jnp.int32)


def _mixer_router(x2d, seq_len, g_mix, w_in, b_gate, w_grp, pool_scale, w_po, conv_w,
                  w_co, w_o, g_ffn, wr, rbias, s_gate, s_up, s_down):
    t_tok = x2d.shape[0]
    tm = TM_MIX
    n_seq_tiles = seq_len // tm
    const = lambda shape: pl.BlockSpec(shape, lambda i: (0,) * len(shape),
                                       pipeline_mode=pl.Buffered(1))
    hbm = pl.BlockSpec(memory_space=pl.ANY)
    row_blk = pl.BlockSpec((tm, D_MODEL), lambda i: (i, 0))
    half_blk = pl.BlockSpec((tm, HALF), lambda i: (i, 0))
    slot_blk = pl.BlockSpec((TOP_K, tm), lambda i: (0, i))
    return pl.pallas_call(
        functools.partial(_mixer_router_kernel, n_seq_tiles),
        grid=(t_tok // tm,),
        in_specs=[row_blk, const(g_mix.shape), hbm, const(b_gate.shape),
                  hbm, const(pool_scale.shape), hbm,
                  const(conv_w.shape), hbm, hbm, const(g_ffn.shape),
                  const(wr.shape), const(rbias.shape), hbm, hbm, hbm],
        out_specs=[row_blk, half_blk, slot_blk, slot_blk, slot_blk,
                   pl.BlockSpec((N_EXPERTS, 1), lambda i: (0, 0))],
        out_shape=[jax.ShapeDtypeStruct((t_tok, D_MODEL), F32),
                   jax.ShapeDtypeStruct((t_tok, HALF), U32),
                   jax.ShapeDtypeStruct((TOP_K, t_tok), jnp.int32),
                   jax.ShapeDtypeStruct((TOP_K, t_tok), F32),
                   jax.ShapeDtypeStruct((TOP_K, t_tok), jnp.int32),
                   jax.ShapeDtypeStruct((N_EXPERTS, 1), jnp.int32)],
        scratch_shapes=[pltpu.VMEM((POOL_HALO + tm, POOL_WIDTH), F32),
                        pltpu.VMEM((CONV_HALO + tm, CONV_WIDTH), F32),
                        pltpu.VMEM((N_EXPERTS, 1), F32),
                        pltpu.VMEM((tm, tm), BF16)]
                       + [pltpu.VMEM(w.shape, BF16)
                          for w in (w_in, w_grp, w_po, w_co, w_o, s_gate, s_up, s_down)]
                       + [pltpu.VMEM((2, W_IN_CHUNK, w_in.shape[1]), F32),
                          pltpu.VMEM(w_o.shape, F32),
                          pltpu.VMEM((2,) + s_gate.shape, F32),
                          pltpu.VMEM(w_grp.shape, F32),
                          pltpu.SemaphoreType.DMA((7,))],
        compiler_params=pltpu.CompilerParams(dimension_semantics=("arbitrary",),
                                             vmem_limit_bytes=VMEM_LIMIT),
        name="mixer_router",
    )(x2d, g_mix, w_in, b_gate, w_grp, pool_scale, w_po, conv_w, w_co, w_o, g_ffn,
      wr, rbias, s_gate, s_up, s_down)


def _plan_kernel(n_blk, n_pad, counts_ref, idx_ref, rank_ref,
                 dest_ref, blk_e_ref, next_e_ref, n_used_ref, n_zero_ref, zero_rows_ref, pad_start):
    e_n = N_EXPERTS

    @pl.when(pl.program_id(0) == 0)
    def _():
        sub = lax.broadcasted_iota(jnp.int32, (e_n, e_n), 0)
        lane = lax.broadcasted_iota(jnp.int32, (e_n, e_n), 1)
        c_col = counts_ref[...]
        p_col = ((c_col + (ROW_BLOCK - 1)) // ROW_BLOCK) * ROW_BLOCK
        c_f = c_col.astype(F32)
        p_f = p_col.astype(F32)
        gap_f = p_f - c_f
        to_row = lambda col: jnp.sum(jnp.where(sub == lane, col, 0.0), axis=0, keepdims=True)
        p_row = to_row(p_f)
        gap_row = to_row(gap_f)
        pad_end_col = jnp.sum(jnp.where(lane <= sub, p_row, 0.0), axis=1, keepdims=True)
        gap_before_col = jnp.sum(jnp.where(lane < sub, gap_row, 0.0), axis=1, keepdims=True)
        pad_start[...] = pad_end_col - p_f
        pad_end_last = jnp.sum(p_row, axis=1, keepdims=True)
        n_used_ref[...] = jnp.broadcast_to(pad_end_last * (1.0 / ROW_BLOCK),
                                           n_used_ref.shape).astype(jnp.int32)
        gap_total = jnp.sum(gap_row, axis=1, keepdims=True)
        n_zero_ref[...] = jnp.broadcast_to(jnp.floor((gap_total + (SC_CHUNK - 1))
                                                     * (1.0 / SC_CHUNK)),
                                           n_zero_ref.shape).astype(jnp.int32)
        row0 = (lax.broadcasted_iota(jnp.int32, (e_n, blk_e_ref.shape[1]), 1)
                * ROW_BLOCK).astype(F32)
        owner = jnp.sum(jnp.where(pad_end_col <= row0, 1.0, 0.0), axis=0, keepdims=True)
        blk_e_ref[...] = jnp.minimum(owner, float(e_n - 1)).astype(jnp.int32)
        used_later = jnp.logical_and(sub > lane, p_f > 0.0)
        nxt = jnp.min(jnp.where(used_later, sub, e_n), axis=0, keepdims=True)
        next_e_ref[...] = jnp.full(next_e_ref.shape, -1, jnp.int32)
        next_e_ref[:, 0:e_n] = jnp.where(nxt < e_n, nxt, -1)
        for j0 in range(0, n_pad, PLAN_LANES):
            j = (j0 + lax.broadcasted_iota(jnp.int32, (e_n, PLAN_LANES), 1)).astype(F32)
            before = jnp.sum(jnp.where(gap_before_col <= j, c_f, 0.0), axis=0, keepdims=True)
            zero_rows_ref[:, j0:j0 + PLAN_LANES] = (j[0:1, :] + before).astype(jnp.int32)

    tm = idx_ref.shape[1]
    eidx = lax.broadcasted_iota(jnp.int32, (e_n, tm), 0)
    ps = pad_start[...]
    for k in range(TOP_K):
        oh = eidx == idx_ref[k:k + 1, :]
        start = jnp.sum(jnp.where(oh, ps, 0.0), axis=0, keepdims=True)
        dest_ref[k:k + 1, :] = start.astype(jnp.int32) + rank_ref[k:k + 1, :]


def _plan(counts, idx_t, rank_t, n_blk, n_pad):
    t_tok = idx_t.shape[1]
    lanes = 128
    n_blk_p = -(-n_blk // lanes) * lanes
    slot_blk = pl.BlockSpec((TOP_K, TM_DEST), lambda i: (0, i))
    whole = lambda n: pl.BlockSpec((1, n), lambda i: (0, 0))
    dest_t, blk_e, next_e, n_used, n_zero, zero_rows = pl.pallas_call(
        functools.partial(_plan_kernel, n_blk, n_pad),
        grid=(t_tok // TM_DEST,),
        in_specs=[pl.BlockSpec((N_EXPERTS, 1), lambda i: (0, 0)), slot_blk, slot_blk],
        out_specs=[slot_blk, whole(n_blk_p), whole(lanes), whole(lanes), whole(lanes),
                   whole(n_pad)],
        out_shape=[jax.ShapeDtypeStruct((TOP_K, t_tok), jnp.int32),
                   jax.ShapeDtypeStruct((1, n_blk_p), jnp.int32),
                   jax.ShapeDtypeStruct((1, lanes), jnp.int32),
                   jax.ShapeDtypeStruct((1, lanes), jnp.int32),
                   jax.ShapeDtypeStruct((1, lanes), jnp.int32),
                   jax.ShapeDtypeStruct((1, n_pad), jnp.int32)],
        scratch_shapes=[pltpu.VMEM((N_EXPERTS, 1), F32)],
        compiler_params=pltpu.CompilerParams(dimension_semantics=("arbitrary",)),
        name="plan",
    )(counts, idx_t, rank_t)
    return (dest_t, blk_e.reshape(-1), next_e.reshape(-1), n_used.reshape(-1), n_zero.reshape(-1),
            zero_rows.reshape(-1))


def _rsqrt_newton(x):
    seed = jnp.int32(0x5F3759DF) - (lax.bitcast_convert_type(x, jnp.int32) >> 1)
    y = lax.bitcast_convert_type(seed, F32)
    for _ in range(3):
        y = y * (1.5 - (0.5 * x) * (y * y))
    return y


def _sc_mesh():
    return plsc.VectorSubcoreMesh(core_axis_name="c", subcore_axis_name="s")


def _sc_worker_id():
    return lax.axis_index("s") * SC_CORES + lax.axis_index("c")


def _dispatch(h2p, dest_t, zero_rows, n_zero, n_rows):
    t_tok, width = h2p.shape
    per_w = t_tok // SC_WORKERS
    n_chunks = per_w // SC_CHUNK
    z_chunks = zero_rows.shape[0] // (SC_WORKERS * SC_CHUNK)
    dest_w = (dest_t.reshape(TOP_K, SC_WORKERS, n_chunks, SC_CHUNK)
              .transpose(1, 2, 0, 3).reshape(SC_WORKERS * n_chunks * TOP_K, SC_CHUNK))
    zero_w = zero_rows.reshape(z_chunks * SC_WORKERS, SC_CHUNK)

    @functools.partial(
        pl.kernel, mesh=_sc_mesh(),
        out_type=jax.ShapeDtypeStruct((n_rows, width), h2p.dtype),
        scratch_types=[pltpu.VMEM((n_chunks * TOP_K, SC_CHUNK), jnp.int32),
                       pltpu.VMEM((z_chunks, SC_CHUNK), jnp.int32),
                       pltpu.VMEM((SC_LANES,), jnp.int32),
                       pltpu.VMEM((2, SC_CHUNK, width), h2p.dtype),
                       pltpu.VMEM((SC_CHUNK, width), h2p.dtype),
                       pltpu.SemaphoreType.DMA, pltpu.SemaphoreType.DMA,
                       pltpu.SemaphoreType.DMA, pltpu.SemaphoreType.DMA],
        compiler_params=pltpu.CompilerParams(needs_layout_passes=False),
        name="dispatch",
    )
    def k(h2p_hbm, dest_hbm, zidx_hbm, nz_hbm, zsrc_hbm, xs_hbm, idx_v, zidx_v, nz_v, rows_v, zero_v,
          gsem, wsem, zsem, isem):
        wid = _sc_worker_id()
        base = wid * per_w
        setup = [pltpu.make_async_copy(zidx_hbm.at[pl.ds(j * SC_WORKERS + wid, 1)],
                                       zidx_v.at[pl.ds(j, 1)], isem) for j in range(z_chunks)]
        setup += [pltpu.make_async_copy(dest_hbm.at[pl.ds(wid * n_chunks * TOP_K, n_chunks * TOP_K)],
                                        idx_v, isem),
                  pltpu.make_async_copy(nz_hbm.at[pl.ds(0, SC_LANES)], nz_v, isem),
                  pltpu.make_async_copy(zsrc_hbm, zero_v, isem)]
        for cp in setup:
            cp.start()
        for cp in setup:
            cp.wait()
        n_zero_chunks = jnp.max(nz_v[...])

        def zput(j):
            return pltpu.make_async_copy(zero_v, xs_hbm.at[zidx_v.at[j]], zsem)

        for j in range(z_chunks):
            @pl.when(j * SC_WORKERS + wid < n_zero_chunks)
            def _():
                zput(j).start()

        def get(j, slot):
            return pltpu.make_async_copy(h2p_hbm.at[pl.ds(base + j * SC_CHUNK, SC_CHUNK)],
                                         rows_v.at[slot], gsem)

        def put(j, slot, kk):
            return pltpu.make_async_copy(rows_v.at[slot], xs_hbm.at[idx_v.at[j * TOP_K + kk]], wsem)

        get(0, 0).start()

        @pl.loop(0, n_chunks, step=2)
        def _(j):
            for b in range(2):
                jj = j + b
                get(jj, b).wait()

                @pl.when(jj >= 1)
                def _():
                    for kk in range(TOP_K):
                        put(jj - 1, 1 - b, kk).wait()

                @pl.when(jj + 1 < n_chunks)
                def _():
                    get(jj + 1, 1 - b).start()
                for kk in range(TOP_K):
                    put(jj, b, kk).start()

        for kk in range(TOP_K):
            put(n_chunks - 1, (n_chunks - 1) % 2, kk).wait()
        for j in range(z_chunks):
            @pl.when(j * SC_WORKERS + wid < n_zero_chunks)
            def _():
                zput(j).wait()

    return k(h2p, dest_w, zero_w, n_zero, jnp.zeros((SC_CHUNK, width), h2p.dtype))


def _regroup_sum(ys, dest_t, wsel_t, xres, g_out):
    n_slots, t_tok = dest_t.shape
    width = ys.shape[1]
    d_out = xres.shape[1]
    grp = SC_SUM_GROUP
    r_rows = SC_SUM_ROW_RING
    r_acc = SC_SUM_ACC_RING
    per_w = t_tok // SC_WORKERS
    n_sub = per_w // grp

    @functools.partial(
        pl.kernel, mesh=_sc_mesh(),
        out_type=jax.ShapeDtypeStruct((t_tok, d_out), F32),
        scratch_types=[pltpu.VMEM((n_slots, per_w), jnp.int32),
                       pltpu.VMEM((n_slots, per_w), F32),
                       pltpu.VMEM((r_rows, n_slots * grp, width), ys.dtype),
                       pltpu.VMEM((r_acc, grp, d_out), F32),
                       pltpu.VMEM((d_out,), F32),
                       pltpu.SemaphoreType.DMA((r_rows,)), pltpu.SemaphoreType.DMA((r_acc,)),
                       pltpu.SemaphoreType.DMA((r_acc,))],
        compiler_params=pltpu.CompilerParams(needs_layout_passes=False),
        name="regroup_sum",
    )
    def k(ys_hbm, idx_hbm, w_hbm, xres_hbm, g_hbm, out_hbm, idx_v, w_v, rows_v, acc_v, g_v,
          gsem, xsem, psem):
        wid = _sc_worker_id()
        base = wid * per_w
        setup = [pltpu.make_async_copy(idx_hbm.at[:, pl.ds(base, per_w)], idx_v, psem.at[0]),
                 pltpu.make_async_copy(w_hbm.at[:, pl.ds(base, per_w)], w_v, psem.at[1]),
                 pltpu.make_async_copy(g_hbm, g_v, psem.at[2])]
        for cp in setup:
            cp.start()

        def gets(j, slot):
            return [pltpu.make_async_copy(ys_hbm.at[idx_v.at[kk, pl.ds(j * grp, grp)]],
                                          rows_v.at[slot, pl.ds(kk * grp, grp)], gsem.at[slot])
                    for kk in range(n_slots)]

        def xload(j, slot):
            return pltpu.make_async_copy(xres_hbm.at[pl.ds(base + j * grp, grp)], acc_v.at[slot],
                                         xsem.at[slot])

        def put(j, slot):
            return pltpu.make_async_copy(acc_v.at[slot], out_hbm.at[pl.ds(base + j * grp, grp)],
                                         psem.at[slot])

        def accumulate(j, rslot, aslot):
            @pl.loop(0, grp)
            def _(g):
                pos = jnp.full((SC_LANES,), j * grp + g, jnp.int32)
                wk = [plsc.load_gather(w_v, [jnp.full((SC_LANES,), kk, jnp.int32), pos])
                      for kk in range(n_slots)]

                zero = jnp.zeros((SC_LANES,), F32)

                @plsc.parallel_loop(0, width // SC_LANES, carry=(zero, zero))
                def ssq(v, carry):
                    cols_a = pl.ds(v * SC_LANES, SC_LANES)
                    cols_b = pl.ds(width + v * SC_LANES, SC_LANES)
                    acc_a = acc_v[aslot, g, cols_a]
                    acc_b = acc_v[aslot, g, cols_b]
                    for kk in range(n_slots):
                        ya, yb = _unpack_pair(rows_v[rslot, kk * grp + g, cols_a])
                        acc_a = acc_a + ya * wk[kk]
                        acc_b = acc_b + yb * wk[kk]
                    acc_v[aslot, g, cols_a] = acc_a
                    acc_v[aslot, g, cols_b] = acc_b
                    return carry[0] + acc_a * acc_a, carry[1] + acc_b * acc_b

                mean = jnp.full((SC_LANES,), jnp.sum(ssq[0] + ssq[1]), F32) * (1.0 / d_out)
                r = _rsqrt_newton(mean + EPS)

                @plsc.parallel_loop(0, d_out // SC_LANES)
                def _(v):
                    cols = pl.ds(v * SC_LANES, SC_LANES)
                    acc_v[aslot, g, cols] = (acc_v[aslot, g, cols] * r) * g_v[cols]

        xload(0, 0).start()
        for cp in setup:
            cp.wait()
        for cp in gets(0, 0):
            cp.start()

        @pl.loop(0, n_sub, step=r_acc)
        def _(j):
            for b in range(r_acc):
                jj = j + b
                rs = b % r_rows
                na = (b + 1) % r_acc
                for cp in gets(jj, rs):
                    cp.wait()

                @pl.when(jj + 1 < n_sub)
                def _():
                    for cp in gets(jj + 1, (b + 1) % r_rows):
                        cp.start()
                xload(jj, b).wait()

                @pl.when(jj + 1 < n_sub)
                def _():
                    @pl.when(jj + 1 >= r_acc)
                    def _():
                        put(jj + 1 - r_acc, na).wait()
                    xload(jj + 1, na).start()
                accumulate(jj, rs, b)
                put(jj, b).start()

        for p in range(r_acc):
            put(n_sub - r_acc + p, p).wait()

    return k(ys, dest_t, wsel_t, xres, g_out)


def _experts_kernel(n_blk, blk_e_ref, next_e_ref, n_used_ref,
                    xs_hbm, eg_hbm, eu_hbm, ed_hbm, ys_hbm,
                    xbuf, ybuf, hid, stg_g, stg_u, stg_d, wg, wu, wd, zbuf, xsem, ysem, wsem, zsem):
    n = n_used_ref[0]

    def ring(b):
        return jnp.bitwise_and(b, ROW_RING - 1)

    def x_copy(b):
        return pltpu.make_async_copy(xs_hbm.at[pl.ds(pl.multiple_of(b * ROW_BLOCK, ROW_BLOCK),
                                                     ROW_BLOCK)], xbuf.at[ring(b)], xsem.at[ring(b)])

    def y_copy(b):
        return pltpu.make_async_copy(ybuf.at[ring(b)],
                                     ys_hbm.at[pl.ds(pl.multiple_of(b * ROW_BLOCK, ROW_BLOCK),
                                                     ROW_BLOCK)], ysem.at[ring(b)])

    def w_copies(e):
        return (pltpu.make_async_copy(eg_hbm.at[e], stg_g, wsem.at[0]),
                pltpu.make_async_copy(eu_hbm.at[e], stg_u, wsem.at[1]),
                pltpu.make_async_copy(ed_hbm.at[e], stg_d, wsem.at[2]))

    def switch_expert(e, wslot):
        for cp in w_copies(e):
            cp.wait()
        wg[wslot] = stg_g[...].astype(BF16)
        wu[wslot] = stg_u[...].astype(BF16)
        wd[wslot] = stg_d[...].astype(BF16)
        nxt = next_e_ref[e]

        @pl.when(nxt >= 0)
        def _():
            for cp in w_copies(nxt):
                cp.start()

    def gate_up(b, wslot):
        xa, xb = _unpack_pair(xbuf[ring(b)])
        xb16 = jnp.concatenate([xa, xb], axis=1).astype(BF16)
        g = _dot(xb16, wg[wslot])
        up = _dot(xb16, wu[wslot])
        hid[jnp.bitwise_and(b, 1)] = (g * _sigmoid(g) * up).astype(BF16)

    def down(b, wslot):
        y = _dot(hid[jnp.bitwise_and(b, 1)], wd[wslot])
        ybuf[ring(b)] = _pack_pair(y[:, 0:HALF], y[:, HALF:D_MODEL])

    def z_copy(b):
        return pltpu.make_async_copy(zbuf, ys_hbm.at[pl.ds(pl.multiple_of(b * ROW_BLOCK, ROW_BLOCK),
                                                           ROW_BLOCK)], zsem)

    zbuf[...] = jnp.zeros_like(zbuf)

    def z_start(b, c):
        z_copy(b).start()
        return c
    lax.fori_loop(n, n_blk, z_start, 0)

    e0 = blk_e_ref[0]
    for cp in w_copies(e0):
        cp.start()
    for j in range(ROW_RING):
        @pl.when(j < n)
        def _():
            x_copy(j).start()
    switch_expert(e0, 0)
    x_copy(0).wait()
    gate_up(0, 0)

    def body(b, wslot_prev):
        e = blk_e_ref[b]
        first = e != blk_e_ref[b - 1]
        wslot = jnp.where(first, 1 - wslot_prev, wslot_prev)

        @pl.when(first)
        def _():
            switch_expert(e, wslot)

        x_copy(b).wait()

        @pl.when(b + ROW_RING - 1 < n)
        def _():
            x_copy(b + ROW_RING - 1).start()

        @pl.when(b >= ROW_RING + 1)
        def _():
            y_copy(b - 1 - ROW_RING).wait()

        down(b - 1, wslot_prev)
        gate_up(b, wslot)
        y_copy(b - 1).start()
        return wslot

    wslot_last = lax.fori_loop(1, n, body, jnp.int32(0))

    last = n - 1

    @pl.when(last >= ROW_RING)
    def _():
        y_copy(last - ROW_RING).wait()
    down(last, wslot_last)
    y_copy(last).start()
    for j in range(ROW_RING - 1, -1, -1):
        @pl.when(last - j >= 0)
        def _():
            y_copy(last - j).wait()

    def z_wait(b, c):
        z_copy(b).wait()
        return c
    lax.fori_loop(n, n_blk, z_wait, 0)


def _experts(blk_e, next_e, n_used, xs, e_gate, e_up, e_down):
    n_rows = xs.shape[0]
    n_blk = n_rows // ROW_BLOCK
    any_spec = pl.BlockSpec(memory_space=pl.ANY)
    grid_spec = pltpu.PrefetchScalarGridSpec(
        num_scalar_prefetch=3,
        grid=(1,),
        in_specs=[any_spec, any_spec, any_spec, any_spec],
        out_specs=any_spec,
        scratch_shapes=[pltpu.VMEM((ROW_RING, ROW_BLOCK, HALF), U32),
                        pltpu.VMEM((ROW_RING, ROW_BLOCK, HALF), U32),
                        pltpu.VMEM((2, ROW_BLOCK, EXPERT_HIDDEN), BF16),
                        pltpu.VMEM((D_MODEL, EXPERT_HIDDEN), F32),
                        pltpu.VMEM((D_MODEL, EXPERT_HIDDEN), F32),
                        pltpu.VMEM((EXPERT_HIDDEN, D_MODEL), F32),
                        pltpu.VMEM((2, D_MODEL, EXPERT_HIDDEN), BF16),
                        pltpu.VMEM((2, D_MODEL, EXPERT_HIDDEN), BF16),
                        pltpu.VMEM((2, EXPERT_HIDDEN, D_MODEL), BF16),
                        pltpu.VMEM((ROW_BLOCK, HALF), U32),
                        pltpu.SemaphoreType.DMA((ROW_RING,)),
                        pltpu.SemaphoreType.DMA((ROW_RING,)),
                        pltpu.SemaphoreType.DMA((3,)),
                        pltpu.SemaphoreType.DMA],
    )
    return pl.pallas_call(
        functools.partial(_experts_kernel, n_blk),
        grid_spec=grid_spec,
        out_shape=jax.ShapeDtypeStruct((n_rows, HALF), U32),
        compiler_params=pltpu.CompilerParams(dimension_semantics=("arbitrary",)),
        name="experts",
    )(blk_e, next_e, n_used, xs, e_gate, e_up, e_down)


def kernel(x, g_mix, w_in, b_gate, w_pool_group, pool_scale, w_pool_out, conv_w, w_conv_out, w_o,
           g_ffn, w_router, router_bias, e_gate, e_up, e_down, s_gate, s_up, s_down, g_final):
    b, s, d = x.shape
    t_tok = b * s
    n_pad = N_EXPERTS * ROW_BLOCK
    n_rows = t_tok * TOP_K + n_pad
    n_blk = n_rows // ROW_BLOCK
    assert d == D_MODEL and s % TM_MIX == 0 and TM_MIX >= POOL_HALO
    assert t_tok % TM_DEST == 0 and n_pad % PLAN_LANES == 0
    assert t_tok % (2 * SC_WORKERS * SC_CHUNK) == 0 and n_pad % (SC_WORKERS * SC_CHUNK) == 0
    assert t_tok % (SC_WORKERS * SC_SUM_GROUP * SC_SUM_ACC_RING) == 0

    row = lambda a: a.reshape(1, -1)
    wr_t = w_router.T.astype(F32)
    wr_hi = wr_t.astype(BF16)
    wr = jnp.concatenate([wr_hi, (wr_t - wr_hi.astype(F32)).astype(BF16)], axis=0)

    xres, h2p, idx_t, wsel_t, rank_t, counts = _mixer_router(
        x.reshape(t_tok, d), s, row(g_mix), w_in, row(b_gate), w_pool_group,
        row(pool_scale), w_pool_out, conv_w, w_conv_out, w_o, row(g_ffn), wr,
        router_bias.astype(F32).reshape(N_EXPERTS, 1), s_gate, s_up, s_down)
    dest_t, blk_e, next_e, n_used, n_zero, zero_rows = _plan(counts, idx_t, rank_t, n_blk, n_pad)
    xs = _dispatch(h2p, dest_t, zero_rows, n_zero, n_rows)
    ys = _experts(blk_e, next_e, n_used, xs, e_gate, e_up, e_down)
    out = _regroup_sum(ys, dest_t, wsel_t, xres, g_final.astype(F32).reshape(-1))
    return out.reshape(b, s, d)
```

```python
import functools

import jax
import jax.numpy as jnp
from jax import lax
from jax.experimental import pallas as pl
from jax.experimental.pallas import tpu as pltpu
from jax.experimental.pallas import tpu_sc as plsc

D_MODEL = 1024
HALF = D_MODEL // 2
POOL_WIDTH = 512
N_POOL_GROUPS = 4
POOL_GROUP = 128
POOL_WINDOWS = (2, 4, 8, 16)
CONV_WIDTH = 512
N_EXPERTS = 64
TOP_K = 8
EXPERT_HIDDEN = 256
SHARED_HIDDEN = 256
ROUTED_SCALE = 2.5
EPS = 1e-6

POOL_HALO = 16
CONV_HALO = 8
TM_MIX = 512
W_IN_CHUNK = 128
TM_DEST = 4096
PLAN_LANES = 2048
ROW_BLOCK = 512
ROW_RING = 8
TM_NORM = 2048
VMEM_LIMIT = 56 * 1024 * 1024

SC_CORES = 2
SC_SUBCORES = 16
SC_WORKERS = SC_CORES * SC_SUBCORES
SC_LANES = 16
SC_CHUNK = 64
SC_SUM_GROUP = 8
SC_SUM_ROW_RING = 2
SC_SUM_ACC_RING = 4

BF16 = jnp.bfloat16
F32 = jnp.float32
U32 = jnp.uint32


def _rms(x, g):
    r = lax.rsqrt(jnp.mean(x * x, axis=-1, keepdims=True) + EPS)
    return (x * r) * g


def _dot(a, b):
    return jnp.dot(a, b, preferred_element_type=F32)


def _sigmoid(z):
    return 0.5 * jnp.tanh(0.5 * z) + 0.5


def _pack_pair(a, b):
    ra = lax.bitcast_convert_type(a.astype(BF16).astype(F32), U32)
    rb = lax.bitcast_convert_type(b.astype(BF16).astype(F32), U32)
    return ra | (rb >> 16)


def _unpack_pair(w):
    a = lax.bitcast_convert_type(w & jnp.uint32(0xFFFF0000), F32)
    b = lax.bitcast_convert_type(w << 16, F32)
    return a, b


def _load_weights_bf16(w_in_hbm, w_grp_hbm, w_po_hbm, w_co_hbm, w_o_hbm, s_gate_hbm, s_up_hbm,
                       s_down_hbm, w_in_ref, w_grp_ref, w_po_ref, w_co_ref, w_o_ref, s_gate_ref,
                       s_up_ref, s_down_ref, stg_in, stg_sq, stg_sh, stg_grp, wsem):
    copy = lambda src, dst, k: pltpu.make_async_copy(src, dst, wsem.at[k])
    rows = stg_in.shape[1]
    n_in = w_in_hbm.shape[0] // rows
    half = w_po_hbm.shape[0]
    c_in = [copy(w_in_hbm.at[pl.ds(j * rows, rows)], stg_in.at[j % 2], j % 2) for j in range(n_in)]
    c_o = copy(w_o_hbm, stg_sq, 2)
    c_sg = copy(s_gate_hbm, stg_sh.at[0], 3)
    c_su = copy(s_up_hbm, stg_sh.at[1], 4)
    c_grp = copy(w_grp_hbm, stg_grp, 5)
    c_po = copy(w_po_hbm, stg_sq.at[0:half], 2)
    c_co = copy(w_co_hbm, stg_sq.at[half:2 * half], 6)
    c_sd = copy(s_down_hbm, stg_sq.at[0:s_down_hbm.shape[0]], 2)
    for cp in (c_in[0], c_in[1], c_o, c_sg, c_su, c_grp):
        cp.start()
    for j in range(n_in):
        c_in[j].wait()
        w_in_ref[j * rows:(j + 1) * rows, :] = stg_in[j % 2].astype(BF16)
        if j + 2 < n_in:
            c_in[j + 2].start()
    c_o.wait()
    w_o_ref[...] = stg_sq[...].astype(BF16)
    c_po.start()
    c_co.start()
    c_sg.wait()
    s_gate_ref[...] = stg_sh[0].astype(BF16)
    c_su.wait()
    s_up_ref[...] = stg_sh[1].astype(BF16)
    c_grp.wait()
    w_grp_ref[...] = stg_grp[...].astype(BF16)
    c_po.wait()
    w_po_ref[...] = stg_sq[0:half, :].astype(BF16)
    c_co.wait()
    w_co_ref[...] = stg_sq[half:2 * half, :].astype(BF16)
    c_sd.start()
    c_sd.wait()
    s_down_ref[...] = stg_sq[0:s_down_hbm.shape[0], :].astype(BF16)


def _mixer_router_kernel(n_seq_tiles,
                         x_ref, g_mix_ref, w_in_hbm, b_gate_ref, w_grp_hbm, pool_scale_ref,
                         w_po_hbm, conv_w_ref, w_co_hbm, w_o_hbm, g_ffn_ref,
                         wr_ref, rbias_ref, s_gate_hbm, s_up_hbm, s_down_hbm,
                         xres_ref, h2p_ref, idx_ref, wsel_ref, rank_ref, counts_ref,
                         ext_pool, ext_conv, cnt_carry, tri,
                         w_in_ref, w_grp_ref, w_po_ref, w_co_ref, w_o_ref, s_gate_ref, s_up_ref,
                         s_down_ref, stg_in, stg_sq, stg_sh, stg_grp, wsem):
    tm = x_ref.shape[0]
    i = pl.program_id(0)
    st = i % n_seq_tiles

    @pl.when(i == 0)
    def _():
        _load_weights_bf16(w_in_hbm, w_grp_hbm, w_po_hbm, w_co_hbm, w_o_hbm, s_gate_hbm, s_up_hbm,
                           s_down_hbm, w_in_ref, w_grp_ref, w_po_ref, w_co_ref, w_o_ref,
                           s_gate_ref, s_up_ref, s_down_ref, stg_in, stg_sq, stg_sh, stg_grp, wsem)
        r = lax.broadcasted_iota(jnp.int32, (tm, tm), 0)
        c = lax.broadcasted_iota(jnp.int32, (tm, tm), 1)
        tri[...] = (r < c).astype(BF16)
        cnt_carry[...] = jnp.zeros_like(cnt_carry)

    @pl.when(st == 0)
    def _():
        ext_pool[0:POOL_HALO, :] = jnp.zeros((POOL_HALO, POOL_WIDTH), F32)
        ext_conv[0:CONV_HALO, :] = jnp.zeros((CONV_HALO, CONV_WIDTH), F32)

    x = x_ref[...]
    hb = _rms(x, g_mix_ref[...]).astype(BF16)

    o0 = POOL_WIDTH
    o1 = o0 + CONV_WIDTH
    o2 = o1 + CONV_WIDTH
    o3 = o2 + CONV_WIDTH

    u = _dot(hb, w_in_ref[:, 0:o0])
    ext_pool[POOL_HALO:POOL_HALO + tm, :] = u
    gc = _dot(hb, w_in_ref[:, o1:o2])
    v = _dot(hb, w_in_ref[:, o2:o3])
    pre_a = _dot(hb, w_in_ref[:, o3:o3 + D_MODEL])

    t_glob = st * tm + lax.broadcasted_iota(jnp.int32, (tm, 1), 0)
    mixed = []
    for gi, w in enumerate(POOL_WINDOWS):
        cols = slice(gi * POOL_GROUP, (gi + 1) * POOL_GROUP)
        ug = u[:, cols]
        acc = ug
        for j in range(1, w):
            acc = acc + ext_pool[POOL_HALO - j:POOL_HALO - j + tm, cols]
        cnt = jnp.minimum(t_glob + 1, w).astype(F32)
        pooled = acc * (1.0 / cnt) - ug
        mixed.append(_dot(pooled.astype(BF16), w_grp_ref[gi]))
    ext_pool[0:POOL_HALO, :] = ext_pool[tm:tm + POOL_HALO, :]
    pre_b = _dot(hb, w_in_ref[:, o3 + D_MODEL:o3 + 2 * D_MODEL])
    gb = _dot(hb, w_in_ref[:, o0:o1])

    cv = gc * v
    ext_conv[CONV_HALO:CONV_HALO + tm, :] = cv
    conv = (ext_conv[CONV_HALO - 2:CONV_HALO - 2 + tm, :] * conv_w_ref[0:1, :]
            + ext_conv[CONV_HALO - 1:CONV_HALO - 1 + tm, :] * conv_w_ref[1:2, :]
            + cv * conv_w_ref[2:3, :])
    ext_conv[0:CONV_HALO, :] = ext_conv[tm:tm + CONV_HALO, :]
    branch_b = _dot((gb * conv).astype(BF16), w_co_ref[...])
    mixed = jnp.concatenate(mixed, axis=1) * pool_scale_ref[...]
    branch_a = _dot(mixed.astype(BF16), w_po_ref[...])

    merged = (_sigmoid(pre_a + b_gate_ref[:, 0:D_MODEL]) * branch_a
              + _sigmoid(pre_b + b_gate_ref[:, D_MODEL:2 * D_MODEL]) * branch_b)
    x1 = x + _dot(merged.astype(BF16), w_o_ref[...])

    h2 = _rms(x1, g_ffn_ref[...])
    h2p_ref[...] = _pack_pair(h2[:, 0:HALF], h2[:, HALF:D_MODEL])
    h2b = h2.astype(BF16)

    nt = (((1,), (1,)), ((), ()))
    parts = lax.dot_general(wr_ref[...], h2b, nt, preferred_element_type=F32)
    logits = parts[0:N_EXPERTS, :] + parts[N_EXPERTS:2 * N_EXPERTS, :]
    sg = _dot(h2b, s_gate_ref[...])
    su = _dot(h2b, s_up_ref[...])
    scores = jax.nn.sigmoid(logits)
    sel = scores + rbias_ref[...]
    eidx = lax.broadcasted_iota(jnp.int32, (N_EXPERTS, tm), 0).astype(F32)
    e_rows, w_rows = [], []
    mask = jnp.zeros((N_EXPERTS, tm), F32)
    for _ in range(TOP_K):
        m = jnp.max(sel, axis=0, keepdims=True)
        ek = jnp.min(jnp.where(sel == m, eidx, float(N_EXPERTS)), axis=0, keepdims=True)
        oh = eidx == ek
        w_rows.append(jnp.sum(jnp.where(oh, scores, 0.0), axis=0, keepdims=True))
        e_rows.append(ek)
        mask = mask + oh.astype(F32)
        sel = jnp.where(oh, -jnp.inf, sel)

    shared = _dot((sg * _sigmoid(sg) * su).astype(BF16), s_down_ref[...])
    xres_ref[...] = x1 + shared

    wsum = w_rows[0]
    for k in range(1, TOP_K):
        wsum = wsum + w_rows[k]

    before = _dot(mask.astype(BF16), tri[...]) + cnt_carry[...]
    for k in range(TOP_K):
        oh = eidx == e_rows[k]
        rank_ref[k:k + 1, :] = jnp.sum(jnp.where(oh, before, 0.0), axis=0,
                                       keepdims=True).astype(jnp.int32)
        idx_ref[k:k + 1, :] = e_rows[k].astype(jnp.int32)
        wsel_ref[k:k + 1, :] = w_rows[k] / wsum * ROUTED_SCALE
    total = cnt_carry[...] + jnp.sum(mask, axis=1, keepdims=True)
    cnt_carry[...] = total
    counts_ref[...] = total.astype(jnp.int32)


def _mixer_router(x2d, seq_len, g_mix, w_in, b_gate, w_grp, pool_scale, w_po, conv_w,
                  w_co, w_o, g_ffn, wr, rbias, s_gate, s_up, s_down):
    t_tok = x2d.shape[0]
    tm = TM_MIX
    n_seq_tiles = seq_len // tm
    const = lambda shape: pl.BlockSpec(shape, lambda i: (0,) * len(shape),
                                       pipeline_mode=pl.Buffered(1))
    hbm = pl.BlockSpec(memory_space=pl.ANY)
    row_blk = pl.BlockSpec((tm, D_MODEL), lambda i: (i, 0))
    half_blk = pl.BlockSpec((tm, HALF), lambda i: (i, 0))
    slot_blk = pl.BlockSpec((TOP_K, tm), lambda i: (0, i))
    return pl.pallas_call(
        functools.partial(_mixer_router_kernel, n_seq_tiles),
        grid=(t_tok // tm,),
        in_specs=[row_blk, const(g_mix.shape), hbm, const(b_gate.shape),
                  hbm, const(pool_scale.shape), hbm,
                  const(conv_w.shape), hbm, hbm, const(g_ffn.shape),
                  const(wr.shape), const(rbias.shape), hbm, hbm, hbm],
        out_specs=[row_blk, half_blk, slot_blk, slot_blk, slot_blk,
                   pl.BlockSpec((N_EXPERTS, 1), lambda i: (0, 0))],
        out_shape=[jax.ShapeDtypeStruct((t_tok, D_MODEL), F32),
                   jax.ShapeDtypeStruct((t_tok, HALF), U32),
                   jax.ShapeDtypeStruct((TOP_K, t_tok), jnp.int32),
                   jax.ShapeDtypeStruct((TOP_K, t_tok), F32),
                   jax.ShapeDtypeStruct((TOP_K, t_tok), jnp.int32),
                   jax.ShapeDtypeStruct((N_EXPERTS, 1), jnp.int32)],
        scratch_shapes=[pltpu.VMEM((POOL_HALO + tm, POOL_WIDTH), F32),
                        pltpu.VMEM((CONV_HALO + tm, CONV_WIDTH), F32),
                        pltpu.VMEM((N_EXPERTS, 1), F32),
                        pltpu.VMEM((tm, tm), BF16)]
                       + [pltpu.VMEM(w.shape, BF16)
                          for w in (w_in, w_grp, w_po, w_co, w_o, s_gate, s_up, s_down)]
                       + [pltpu.VMEM((2, W_IN_CHUNK, w_in.shape[1]), F32),
                          pltpu.VMEM(w_o.shape, F32),
                          pltpu.VMEM((2,) + s_gate.shape, F32),
                          pltpu.VMEM(w_grp.shape, F32),
                          pltpu.SemaphoreType.DMA((7,))],
        compiler_params=pltpu.CompilerParams(dimension_semantics=("arbitrary",),
                                             vmem_limit_bytes=VMEM_LIMIT),
        name="mixer_router",
    )(x2d, g_mix, w_in, b_gate, w_grp, pool_scale, w_po, conv_w, w_co, w_o, g_ffn,
      wr, rbias, s_gate, s_up, s_down)


def _plan_kernel(n_blk, n_pad, counts_ref, idx_ref, rank_ref,
                 dest_ref, blk_e_ref, next_e_ref, n_used_ref, n_zero_ref, zero_rows_ref, pad_start):
    e_n = N_EXPERTS

    @pl.when(pl.program_id(0) == 0)
    def _():
        sub = lax.broadcasted_iota(jnp.int32, (e_n, e_n), 0)
        lane = lax.broadcasted_iota(jnp.int32, (e_n, e_n), 1)
        c_col = counts_ref[...]
        p_col = ((c_col + (ROW_BLOCK - 1)) // ROW_BLOCK) * ROW_BLOCK
        c_f = c_col.astype(F32)
        p_f = p_col.astype(F32)
        gap_f = p_f - c_f
        to_row = lambda col: jnp.sum(jnp.where(sub == lane, col, 0.0), axis=0, keepdims=True)
        p_row = to_row(p_f)
        gap_row = to_row(gap_f)
        pad_end_col = jnp.sum(jnp.where(lane <= sub, p_row, 0.0), axis=1, keepdims=True)
        gap_before_col = jnp.sum(jnp.where(lane < sub, gap_row, 0.0), axis=1, keepdims=True)
        pad_start[...] = pad_end_col - p_f
        pad_end_last = jnp.sum(p_row, axis=1, keepdims=True)
        n_used_ref[...] = jnp.broadcast_to(pad_end_last * (1.0 / ROW_BLOCK),
                                           n_used_ref.shape).astype(jnp.int32)
        gap_total = jnp.sum(gap_row, axis=1, keepdims=True)
        n_zero_ref[...] = jnp.broadcast_to(jnp.floor((gap_total + (SC_CHUNK - 1))
                                                     * (1.0 / SC_CHUNK)),
                                           n_zero_ref.shape).astype(jnp.int32)
        row0 = (lax.broadcasted_iota(jnp.int32, (e_n, blk_e_ref.shape[1]), 1)
                * ROW_BLOCK).astype(F32)
        owner = jnp.sum(jnp.where(pad_end_col <= row0, 1.0, 0.0), axis=0, keepdims=True)
        blk_e_ref[...] = jnp.minimum(owner, float(e_n - 1)).astype(jnp.int32)
        used_later = jnp.logical_and(sub > lane, p_f > 0.0)
        nxt = jnp.min(jnp.where(used_later, sub, e_n), axis=0, keepdims=True)
        next_e_ref[...] = jnp.full(next_e_ref.shape, -1, jnp.int32)
        next_e_ref[:, 0:e_n] = jnp.where(nxt < e_n, nxt, -1)
        for j0 in range(0, n_pad, PLAN_LANES):
            j = (j0 + lax.broadcasted_iota(jnp.int32, (e_n, PLAN_LANES), 1)).astype(F32)
            before = jnp.sum(jnp.where(gap_before_col <= j, c_f, 0.0), axis=0, keepdims=True)
            zero_rows_ref[:, j0:j0 + PLAN_LANES] = (j[0:1, :] + before).astype(jnp.int32)

    tm = idx_ref.shape[1]
    eidx = lax.broadcasted_iota(jnp.int32, (e_n, tm), 0)
    ps = pad_start[...]
    for k in range(TOP_K):
        oh = eidx == idx_ref[k:k + 1, :]
        start = jnp.sum(jnp.where(oh, ps, 0.0), axis=0, keepdims=True)
        dest_ref[k:k + 1, :] = start.astype(jnp.int32) + rank_ref[k:k + 1, :]


def _plan(counts, idx_t, rank_t, n_blk, n_pad):
    t_tok = idx_t.shape[1]
    lanes = 128
    n_blk_p = -(-n_blk // lanes) * lanes
    slot_blk = pl.BlockSpec((TOP_K, TM_DEST), lambda i: (0, i))
    whole = lambda n: pl.BlockSpec((1, n), lambda i: (0, 0))
    dest_t, blk_e, next_e, n_used, n_zero, zero_rows = pl.pallas_call(
        functools.partial(_plan_kernel, n_blk, n_pad),
        grid=(t_tok // TM_DEST,),
        in_specs=[pl.BlockSpec((N_EXPERTS, 1), lambda i: (0, 0)), slot_blk, slot_blk],
        out_specs=[slot_blk, whole(n_blk_p), whole(lanes), whole(lanes), whole(lanes),
                   whole(n_pad)],
        out_shape=[jax.ShapeDtypeStruct((TOP_K, t_tok), jnp.int32),
                   jax.ShapeDtypeStruct((1, n_blk_p), jnp.int32),
                   jax.ShapeDtypeStruct((1, lanes), jnp.int32),
                   jax.ShapeDtypeStruct((1, lanes), jnp.int32),
                   jax.ShapeDtypeStruct((1, lanes), jnp.int32),
                   jax.ShapeDtypeStruct((1, n_pad), jnp.int32)],
        scratch_shapes=[pltpu.VMEM((N_EXPERTS, 1), F32)],
        compiler_params=pltpu.CompilerParams(dimension_semantics=("arbitrary",)),
        name="plan",
    )(counts, idx_t, rank_t)
    return (dest_t, blk_e.reshape(-1), next_e.reshape(-1), n_used.reshape(-1), n_zero.reshape(-1),
            zero_rows.reshape(-1))


def _sc_mesh():
    return plsc.VectorSubcoreMesh(core_axis_name="c", subcore_axis_name="s")


def _sc_worker_id():
    return lax.axis_index("s") * SC_CORES + lax.axis_index("c")


def _dispatch(h2p, dest_t, zero_rows, n_zero, n_rows):
    t_tok, width = h2p.shape
    per_w = t_tok // SC_WORKERS
    n_chunks = per_w // SC_CHUNK
    z_chunks = zero_rows.shape[0] // (SC_WORKERS * SC_CHUNK)
    dest_w = (dest_t.reshape(TOP_K, SC_WORKERS, n_chunks, SC_CHUNK)
              .transpose(1, 2, 0, 3).reshape(SC_WORKERS * n_chunks * TOP_K, SC_CHUNK))
    zero_w = zero_rows.reshape(z_chunks * SC_WORKERS, SC_CHUNK)

    @functools.partial(
        pl.kernel, mesh=_sc_mesh(),
        out_type=jax.ShapeDtypeStruct((n_rows, width), h2p.dtype),
        scratch_types=[pltpu.VMEM((n_chunks * TOP_K, SC_CHUNK), jnp.int32),
                       pltpu.VMEM((z_chunks, SC_CHUNK), jnp.int32),
                       pltpu.VMEM((SC_LANES,), jnp.int32),
                       pltpu.VMEM((2, SC_CHUNK, width), h2p.dtype),
                       pltpu.VMEM((SC_CHUNK, width), h2p.dtype),
                       pltpu.SemaphoreType.DMA, pltpu.SemaphoreType.DMA,
                       pltpu.SemaphoreType.DMA, pltpu.SemaphoreType.DMA],
        compiler_params=pltpu.CompilerParams(needs_layout_passes=False),
        name="dispatch",
    )
    def k(h2p_hbm, dest_hbm, zidx_hbm, nz_hbm, zsrc_hbm, xs_hbm, idx_v, zidx_v, nz_v, rows_v, zero_v,
          gsem, wsem, zsem, isem):
        wid = _sc_worker_id()
        base = wid * per_w
        setup = [pltpu.make_async_copy(zidx_hbm.at[pl.ds(j * SC_WORKERS + wid, 1)],
                                       zidx_v.at[pl.ds(j, 1)], isem) for j in range(z_chunks)]
        setup += [pltpu.make_async_copy(dest_hbm.at[pl.ds(wid * n_chunks * TOP_K, n_chunks * TOP_K)],
                                        idx_v, isem),
                  pltpu.make_async_copy(nz_hbm.at[pl.ds(0, SC_LANES)], nz_v, isem),
                  pltpu.make_async_copy(zsrc_hbm, zero_v, isem)]
        for cp in setup:
            cp.start()
        for cp in setup:
            cp.wait()
        n_zero_chunks = jnp.max(nz_v[...])

        def zput(j):
            return pltpu.make_async_copy(zero_v, xs_hbm.at[zidx_v.at[j]], zsem)

        for j in range(z_chunks):
            @pl.when(j * SC_WORKERS + wid < n_zero_chunks)
            def _():
                zput(j).start()

        def get(j, slot):
            return pltpu.make_async_copy(h2p_hbm.at[pl.ds(base + j * SC_CHUNK, SC_CHUNK)],
                                         rows_v.at[slot], gsem)

        def put(j, slot, kk):
            return pltpu.make_async_copy(rows_v.at[slot], xs_hbm.at[idx_v.at[j * TOP_K + kk]], wsem)

        get(0, 0).start()

        @pl.loop(0, n_chunks, step=2)
        def _(j):
            for b in range(2):
                jj = j + b
                get(jj, b).wait()

                @pl.when(jj >= 1)
                def _():
                    for kk in range(TOP_K):
                        put(jj - 1, 1 - b, kk).wait()

                @pl.when(jj + 1 < n_chunks)
                def _():
                    get(jj + 1, 1 - b).start()
                for kk in range(TOP_K):
                    put(jj, b, kk).start()

        for kk in range(TOP_K):
            put(n_chunks - 1, (n_chunks - 1) % 2, kk).wait()
        for j in range(z_chunks):
            @pl.when(j * SC_WORKERS + wid < n_zero_chunks)
            def _():
                zput(j).wait()

    return k(h2p, dest_w, zero_w, n_zero, jnp.zeros((SC_CHUNK, width), h2p.dtype))


def _regroup_sum(ys, dest_t, wsel_t, xres):
    n_slots, t_tok = dest_t.shape
    width = ys.shape[1]
    d_out = xres.shape[1]
    grp = SC_SUM_GROUP
    r_rows = SC_SUM_ROW_RING
    r_acc = SC_SUM_ACC_RING
    per_w = t_tok // SC_WORKERS
    n_sub = per_w // grp

    @functools.partial(
        pl.kernel, mesh=_sc_mesh(),
        out_type=jax.ShapeDtypeStruct((t_tok, d_out), F32),
        scratch_types=[pltpu.VMEM((n_slots, per_w), jnp.int32),
                       pltpu.VMEM((n_slots, per_w), F32),
                       pltpu.VMEM((r_rows, n_slots * grp, width), ys.dtype),
                       pltpu.VMEM((r_acc, grp, d_out), F32),
                       pltpu.SemaphoreType.DMA((r_rows,)), pltpu.SemaphoreType.DMA((r_acc,)),
                       pltpu.SemaphoreType.DMA((r_acc,))],
        compiler_params=pltpu.CompilerParams(needs_layout_passes=False),
        name="regroup_sum",
    )
    def k(ys_hbm, idx_hbm, w_hbm, xres_hbm, out_hbm, idx_v, w_v, rows_v, acc_v, gsem, xsem, psem):
        wid = _sc_worker_id()
        base = wid * per_w
        setup = [pltpu.make_async_copy(idx_hbm.at[:, pl.ds(base, per_w)], idx_v, psem.at[0]),
                 pltpu.make_async_copy(w_hbm.at[:, pl.ds(base, per_w)], w_v, psem.at[1])]
        for cp in setup:
            cp.start()

        def gets(j, slot):
            return [pltpu.make_async_copy(ys_hbm.at[idx_v.at[kk, pl.ds(j * grp, grp)]],
                                          rows_v.at[slot, pl.ds(kk * grp, grp)], gsem.at[slot])
                    for kk in range(n_slots)]

        def xload(j, slot):
            return pltpu.make_async_copy(xres_hbm.at[pl.ds(base + j * grp, grp)], acc_v.at[slot],
                                         xsem.at[slot])

        def put(j, slot):
            return pltpu.make_async_copy(acc_v.at[slot], out_hbm.at[pl.ds(base + j * grp, grp)],
                                         psem.at[slot])

        def accumulate(j, rslot, aslot):
            @pl.loop(0, grp)
            def _(g):
                pos = jnp.full((SC_LANES,), j * grp + g, jnp.int32)
                wk = [plsc.load_gather(w_v, [jnp.full((SC_LANES,), kk, jnp.int32), pos])
                      for kk in range(n_slots)]

                @plsc.parallel_loop(0, width // SC_LANES)
                def _(v):
                    cols_a = pl.ds(v * SC_LANES, SC_LANES)
                    cols_b = pl.ds(width + v * SC_LANES, SC_LANES)
                    acc_a = acc_v[aslot, g, cols_a]
                    acc_b = acc_v[aslot, g, cols_b]
                    for kk in range(n_slots):
                        ya, yb = _unpack_pair(rows_v[rslot, kk * grp + g, cols_a])
                        acc_a = acc_a + ya * wk[kk]
                        acc_b = acc_b + yb * wk[kk]
                    acc_v[aslot, g, cols_a] = acc_a
                    acc_v[aslot, g, cols_b] = acc_b

        xload(0, 0).start()
        for cp in setup:
            cp.wait()
        for cp in gets(0, 0):
            cp.start()

        @pl.loop(0, n_sub, step=r_acc)
        def _(j):
            for b in range(r_acc):
                jj = j + b
                rs = b % r_rows
                na = (b + 1) % r_acc
                for cp in gets(jj, rs):
                    cp.wait()

                @pl.when(jj + 1 < n_sub)
                def _():
                    for cp in gets(jj + 1, (b + 1) % r_rows):
                        cp.start()
                xload(jj, b).wait()

                @pl.when(jj + 1 < n_sub)
                def _():
                    @pl.when(jj + 1 >= r_acc)
                    def _():
                        put(jj + 1 - r_acc, na).wait()
                    xload(jj + 1, na).start()
                accumulate(jj, rs, b)
                put(jj, b).start()

        for p in range(r_acc):
            put(n_sub - r_acc + p, p).wait()

    return k(ys, dest_t, wsel_t, xres)


def _experts_kernel(n_blk, blk_e_ref, next_e_ref, n_used_ref,
                    xs_hbm, eg_hbm, eu_hbm, ed_hbm, ys_hbm,
                    xbuf, ybuf, hid, stg_g, stg_u, stg_d, wg, wu, wd, zbuf, xsem, ysem, wsem, zsem):
    n = n_used_ref[0]

    def ring(b):
        return jnp.bitwise_and(b, ROW_RING - 1)

    def x_copy(b):
        return pltpu.make_async_copy(xs_hbm.at[pl.ds(pl.multiple_of(b * ROW_BLOCK, ROW_BLOCK),
                                                     ROW_BLOCK)], xbuf.at[ring(b)], xsem.at[ring(b)])

    def y_copy(b):
        return pltpu.make_async_copy(ybuf.at[ring(b)],
                                     ys_hbm.at[pl.ds(pl.multiple_of(b * ROW_BLOCK, ROW_BLOCK),
                                                     ROW_BLOCK)], ysem.at[ring(b)])

    def w_copies(e):
        return (pltpu.make_async_copy(eg_hbm.at[e], stg_g, wsem.at[0]),
                pltpu.make_async_copy(eu_hbm.at[e], stg_u, wsem.at[1]),
                pltpu.make_async_copy(ed_hbm.at[e], stg_d, wsem.at[2]))

    def switch_expert(e, wslot):
        for cp in w_copies(e):
            cp.wait()
        wg[wslot] = stg_g[...].astype(BF16)
        wu[wslot] = stg_u[...].astype(BF16)
        wd[wslot] = stg_d[...].astype(BF16)
        nxt = next_e_ref[e]

        @pl.when(nxt >= 0)
        def _():
            for cp in w_copies(nxt):
                cp.start(priority=1)

    def gate_up(b, wslot):
        xa, xb = _unpack_pair(xbuf[ring(b)])
        xb16 = jnp.concatenate([xa, xb], axis=1).astype(BF16)
        g = _dot(xb16, wg[wslot])
        up = _dot(xb16, wu[wslot])
        hid[jnp.bitwise_and(b, 1)] = (g * _sigmoid(g) * up).astype(BF16)

    def down(b, wslot):
        y = _dot(hid[jnp.bitwise_and(b, 1)], wd[wslot])
        ybuf[ring(b)] = _pack_pair(y[:, 0:HALF], y[:, HALF:D_MODEL])

    def z_copy(b):
        return pltpu.make_async_copy(zbuf, ys_hbm.at[pl.ds(pl.multiple_of(b * ROW_BLOCK, ROW_BLOCK),
                                                           ROW_BLOCK)], zsem)

    def z_start(b, c):
        z_copy(b).start(priority=1)
        return c

    e0 = blk_e_ref[0]
    for cp in w_copies(e0):
        cp.start(priority=1)
    for j in range(ROW_RING):
        @pl.when(j < n)
        def _():
            x_copy(j).start()
    zbuf[...] = jnp.zeros_like(zbuf)
    lax.fori_loop(n, n_blk, z_start, 0)
    switch_expert(e0, 0)
    x_copy(0).wait()
    gate_up(0, 0)

    def body(b, wslot_prev):
        e = blk_e_ref[b]
        first = e != blk_e_ref[b - 1]
        wslot = jnp.where(first, 1 - wslot_prev, wslot_prev)

        @pl.when(first)
        def _():
            switch_expert(e, wslot)

        x_copy(b).wait()

        @pl.when(b + ROW_RING - 1 < n)
        def _():
            x_copy(b + ROW_RING - 1).start()

        @pl.when(b >= ROW_RING + 1)
        def _():
            y_copy(b - 1 - ROW_RING).wait()

        down(b - 1, wslot_prev)
        gate_up(b, wslot)
        y_copy(b - 1).start()
        return wslot

    wslot_last = lax.fori_loop(1, n, body, jnp.int32(0))

    last = n - 1

    @pl.when(last >= ROW_RING)
    def _():
        y_copy(last - ROW_RING).wait()
    down(last, wslot_last)
    y_copy(last).start()
    for j in range(ROW_RING - 1, -1, -1):
        @pl.when(last - j >= 0)
        def _():
            y_copy(last - j).wait()

    def z_wait(b, c):
        z_copy(b).wait()
        return c
    lax.fori_loop(n, n_blk, z_wait, 0)


def _experts(blk_e, next_e, n_used, xs, e_gate, e_up, e_down):
    n_rows = xs.shape[0]
    n_blk = n_rows // ROW_BLOCK
    any_spec = pl.BlockSpec(memory_space=pl.ANY)
    grid_spec = pltpu.PrefetchScalarGridSpec(
        num_scalar_prefetch=3,
        grid=(1,),
        in_specs=[any_spec, any_spec, any_spec, any_spec],
        out_specs=any_spec,
        scratch_shapes=[pltpu.VMEM((ROW_RING, ROW_BLOCK, HALF), U32),
                        pltpu.VMEM((ROW_RING, ROW_BLOCK, HALF), U32),
                        pltpu.VMEM((2, ROW_BLOCK, EXPERT_HIDDEN), BF16),
                        pltpu.VMEM((D_MODEL, EXPERT_HIDDEN), F32),
                        pltpu.VMEM((D_MODEL, EXPERT_HIDDEN), F32),
                        pltpu.VMEM((EXPERT_HIDDEN, D_MODEL), F32),
                        pltpu.VMEM((2, D_MODEL, EXPERT_HIDDEN), BF16),
                        pltpu.VMEM((2, D_MODEL, EXPERT_HIDDEN), BF16),
                        pltpu.VMEM((2, EXPERT_HIDDEN, D_MODEL), BF16),
                        pltpu.VMEM((ROW_BLOCK, HALF), U32),
                        pltpu.SemaphoreType.DMA((ROW_RING,)),
                        pltpu.SemaphoreType.DMA((ROW_RING,)),
                        pltpu.SemaphoreType.DMA((3,)),
                        pltpu.SemaphoreType.DMA],
    )
    return pl.pallas_call(
        functools.partial(_experts_kernel, n_blk),
        grid_spec=grid_spec,
        out_shape=jax.ShapeDtypeStruct((n_rows, HALF), U32),
        compiler_params=pltpu.CompilerParams(dimension_semantics=("arbitrary",)),
        name="experts",
    )(blk_e, next_e, n_used, xs, e_gate, e_up, e_down)


def _final_norm_kernel(x_ref, g_ref, out_ref):
    out_ref[...] = _rms(x_ref[...], g_ref[...])


def _final_norm(xsum, g_final):
    t_tok = xsum.shape[0]
    blk = pl.BlockSpec((TM_NORM, D_MODEL), lambda i: (i, 0))
    return pl.pallas_call(
        _final_norm_kernel,
        grid=(t_tok // TM_NORM,),
        in_specs=[blk, pl.BlockSpec((1, D_MODEL), lambda i: (0, 0))],
        out_specs=blk,
        out_shape=jax.ShapeDtypeStruct((t_tok, D_MODEL), F32),
        compiler_params=pltpu.CompilerParams(dimension_semantics=("arbitrary",)),
        name="final_norm",
    )(xsum, g_final)


def kernel(x, g_mix, w_in, b_gate, w_pool_group, pool_scale, w_pool_out, conv_w, w_conv_out, w_o,
           g_ffn, w_router, router_bias, e_gate, e_up, e_down, s_gate, s_up, s_down, g_final):
    b, s, d = x.shape
    t_tok = b * s
    n_pad = N_EXPERTS * ROW_BLOCK
    n_rows = t_tok * TOP_K + n_pad
    n_blk = n_rows // ROW_BLOCK
    assert d == D_MODEL and s % TM_MIX == 0 and TM_MIX >= POOL_HALO
    assert t_tok % TM_DEST == 0 and t_tok % TM_NORM == 0 and n_pad % PLAN_LANES == 0
    assert t_tok % (2 * SC_WORKERS * SC_CHUNK) == 0 and n_pad % (SC_WORKERS * SC_CHUNK) == 0
    assert t_tok % (SC_WORKERS * SC_SUM_GROUP * SC_SUM_ACC_RING) == 0

    row = lambda a: a.reshape(1, -1)
    wr_t = w_router.T.astype(F32)
    wr_hi = wr_t.astype(BF16)
    wr = jnp.concatenate([wr_hi, (wr_t - wr_hi.astype(F32)).astype(BF16)], axis=0)

    xres, h2p, idx_t, wsel_t, rank_t, counts = _mixer_router(
        x.reshape(t_tok, d), s, row(g_mix), w_in, row(b_gate), w_pool_group,
        row(pool_scale), w_pool_out, conv_w, w_conv_out, w_o, row(g_ffn), wr,
        router_bias.astype(F32).reshape(N_EXPERTS, 1), s_gate, s_up, s_down)
    dest_t, blk_e, next_e, n_used, n_zero, zero_rows = _plan(counts, idx_t, rank_t, n_blk, n_pad)
    xs = _dispatch(h2p, dest_t, zero_rows, n_zero, n_rows)
    ys = _experts(blk_e, next_e, n_used, xs, e_gate, e_up, e_down)
    xsum = _regroup_sum(ys, dest_t, wsel_t, xres)
    return _final_norm(xsum, row(g_final)).reshape(b, s, d)
```

```python
import functools

import jax
import jax.numpy as jnp
from jax import lax
from jax.experimental import pallas as pl
from jax.experimental.pallas import tpu as pltpu
from jax.experimental.pallas import tpu_sc as plsc

D_MODEL = 1024
HALF = D_MODEL // 2
POOL_WIDTH = 512
N_POOL_GROUPS = 4
POOL_GROUP = 128
POOL_WINDOWS = (2, 4, 8, 16)
CONV_WIDTH = 512
N_EXPERTS = 64
TOP_K = 8
EXPERT_HIDDEN = 256
SHARED_HIDDEN = 256
ROUTED_SCALE = 2.5
EPS = 1e-6

POOL_HALO = 16
CONV_HALO = 8
TM_MIX = 512
W_IN_CHUNK = 128
TM_DEST = 4096
PLAN_LANES = 2048
ROW_BLOCK = 512
ROW_RING = 8
TM_NORM = 2048
VMEM_LIMIT = 56 * 1024 * 1024

SC_CORES = 2
SC_SUBCORES = 16
SC_WORKERS = SC_CORES * SC_SUBCORES
SC_LANES = 16
SC_CHUNK = 64
SC_SUM_GROUP = 8
SC_SUM_ROW_RING = 2
SC_SUM_ACC_RING = 4

BF16 = jnp.bfloat16
F32 = jnp.float32
U32 = jnp.uint32


def _rms(x, g):
    r = lax.rsqrt(jnp.mean(x * x, axis=-1, keepdims=True) + EPS)
    return (x * r) * g


def _dot(a, b):
    return jnp.dot(a, b, preferred_element_type=F32)


def _sigmoid(z):
    return 0.5 * jnp.tanh(0.5 * z) + 0.5


def _pack_pair(a, b):
    ra = lax.bitcast_convert_type(a.astype(BF16).astype(F32), U32)
    rb = lax.bitcast_convert_type(b.astype(BF16).astype(F32), U32)
    return ra | (rb >> 16)


def _unpack_pair(w):
    a = lax.bitcast_convert_type(w & jnp.uint32(0xFFFF0000), F32)
    b = lax.bitcast_convert_type(w << 16, F32)
    return a, b


def _load_weights_bf16(w_in_hbm, w_grp_hbm, w_po_hbm, w_co_hbm, w_o_hbm, s_gate_hbm, s_up_hbm,
                       s_down_hbm, w_in_ref, w_grp_ref, w_po_ref, w_co_ref, w_o_ref, s_gate_ref,
                       s_up_ref, s_down_ref, stg_in, stg_sq, stg_sh, stg_grp, wsem):
    copy = lambda src, dst, k: pltpu.make_async_copy(src, dst, wsem.at[k])
    rows = stg_in.shape[1]
    n_in = w_in_hbm.shape[0] // rows
    half = w_po_hbm.shape[0]
    c_in = [copy(w_in_hbm.at[pl.ds(j * rows, rows)], stg_in.at[j % 2], j % 2) for j in range(n_in)]
    c_o = copy(w_o_hbm, stg_sq, 2)
    c_sg = copy(s_gate_hbm, stg_sh.at[0], 3)
    c_su = copy(s_up_hbm, stg_sh.at[1], 4)
    c_grp = copy(w_grp_hbm, stg_grp, 5)
    c_po = copy(w_po_hbm, stg_sq.at[0:half], 2)
    c_co = copy(w_co_hbm, stg_sq.at[half:2 * half], 6)
    c_sd = copy(s_down_hbm, stg_sq.at[0:s_down_hbm.shape[0]], 2)
    for cp in (c_in[0], c_in[1], c_o, c_sg, c_su, c_grp):
        cp.start()
    for j in range(n_in):
        c_in[j].wait()
        w_in_ref[j * rows:(j + 1) * rows, :] = stg_in[j % 2].astype(BF16)
        if j + 2 < n_in:
            c_in[j + 2].start()
    c_o.wait()
    w_o_ref[...] = stg_sq[...].astype(BF16)
    c_po.start()
    c_co.start()
    c_sg.wait()
    s_gate_ref[...] = stg_sh[0].astype(BF16)
    c_su.wait()
    s_up_ref[...] = stg_sh[1].astype(BF16)
    c_grp.wait()
    w_grp_ref[...] = stg_grp[...].astype(BF16)
    c_po.wait()
    w_po_ref[...] = stg_sq[0:half, :].astype(BF16)
    c_co.wait()
    w_co_ref[...] = stg_sq[half:2 * half, :].astype(BF16)
    c_sd.start()
    c_sd.wait()
    s_down_ref[...] = stg_sq[0:s_down_hbm.shape[0], :].astype(BF16)


def _mixer_router_kernel(n_seq_tiles,
                         x_ref, g_mix_ref, w_in_hbm, b_gate_ref, w_grp_hbm, pool_scale_ref,
                         w_po_hbm, conv_w_ref, w_co_hbm, w_o_hbm, g_ffn_ref,
                         wr_ref, rbias_ref, s_gate_hbm, s_up_hbm, s_down_hbm,
                         xres_ref, h2p_ref, idx_ref, wsel_ref, rank_ref, counts_ref,
                         ext_pool, ext_conv, cnt_carry, tri,
                         w_in_ref, w_grp_ref, w_po_ref, w_co_ref, w_o_ref, s_gate_ref, s_up_ref,
                         s_down_ref, stg_in, stg_sq, stg_sh, stg_grp, wsem):
    tm = x_ref.shape[0]
    i = pl.program_id(0)
    st = i % n_seq_tiles

    @pl.when(i == 0)
    def _():
        _load_weights_bf16(w_in_hbm, w_grp_hbm, w_po_hbm, w_co_hbm, w_o_hbm, s_gate_hbm, s_up_hbm,
                           s_down_hbm, w_in_ref, w_grp_ref, w_po_ref, w_co_ref, w_o_ref,
                           s_gate_ref, s_up_ref, s_down_ref, stg_in, stg_sq, stg_sh, stg_grp, wsem)
        r = lax.broadcasted_iota(jnp.int32, (tm, tm), 0)
        c = lax.broadcasted_iota(jnp.int32, (tm, tm), 1)
        tri[...] = (r < c).astype(BF16)
        cnt_carry[...] = jnp.zeros_like(cnt_carry)

    @pl.when(st == 0)
    def _():
        ext_pool[0:POOL_HALO, :] = jnp.zeros((POOL_HALO, POOL_WIDTH), F32)
        ext_conv[0:CONV_HALO, :] = jnp.zeros((CONV_HALO, CONV_WIDTH), F32)

    x = x_ref[...]
    hb = _rms(x, g_mix_ref[...]).astype(BF16)

    o0 = POOL_WIDTH
    o1 = o0 + CONV_WIDTH
    o2 = o1 + CONV_WIDTH
    o3 = o2 + CONV_WIDTH

    u = _dot(hb, w_in_ref[:, 0:o0])
    ext_pool[POOL_HALO:POOL_HALO + tm, :] = u
    gc = _dot(hb, w_in_ref[:, o1:o2])
    v = _dot(hb, w_in_ref[:, o2:o3])
    pre_a = _dot(hb, w_in_ref[:, o3:o3 + D_MODEL])

    t_glob = st * tm + lax.broadcasted_iota(jnp.int32, (tm, 1), 0)
    mixed = []
    for gi, w in enumerate(POOL_WINDOWS):
        cols = slice(gi * POOL_GROUP, (gi + 1) * POOL_GROUP)
        ug = u[:, cols]
        acc = ug
        for j in range(1, w):
            acc = acc + ext_pool[POOL_HALO - j:POOL_HALO - j + tm, cols]
        cnt = jnp.minimum(t_glob + 1, w).astype(F32)
        pooled = acc * (1.0 / cnt) - ug
        mixed.append(_dot(pooled.astype(BF16), w_grp_ref[gi]))
    ext_pool[0:POOL_HALO, :] = ext_pool[tm:tm + POOL_HALO, :]
    pre_b = _dot(hb, w_in_ref[:, o3 + D_MODEL:o3 + 2 * D_MODEL])
    gb = _dot(hb, w_in_ref[:, o0:o1])

    cv = gc * v
    ext_conv[CONV_HALO:CONV_HALO + tm, :] = cv
    conv = (ext_conv[CONV_HALO - 2:CONV_HALO - 2 + tm, :] * conv_w_ref[0:1, :]
            + ext_conv[CONV_HALO - 1:CONV_HALO - 1 + tm, :] * conv_w_ref[1:2, :]
            + cv * conv_w_ref[2:3, :])
    ext_conv[0:CONV_HALO, :] = ext_conv[tm:tm + CONV_HALO, :]
    branch_b = _dot((gb * conv).astype(BF16), w_co_ref[...])
    mixed = jnp.concatenate(mixed, axis=1) * pool_scale_ref[...]
    branch_a = _dot(mixed.astype(BF16), w_po_ref[...])

    merged = (_sigmoid(pre_a + b_gate_ref[:, 0:D_MODEL]) * branch_a
              + _sigmoid(pre_b + b_gate_ref[:, D_MODEL:2 * D_MODEL]) * branch_b)
    x1 = x + _dot(merged.astype(BF16), w_o_ref[...])

    h2 = _rms(x1, g_ffn_ref[...])
    h2p_ref[...] = _pack_pair(h2[:, 0:HALF], h2[:, HALF:D_MODEL])
    h2b = h2.astype(BF16)

    nt = (((1,), (1,)), ((), ()))
    parts = lax.dot_general(wr_ref[...], h2b, nt, preferred_element_type=F32)
    logits = parts[0:N_EXPERTS, :] + parts[N_EXPERTS:2 * N_EXPERTS, :]
    sg = _dot(h2b, s_gate_ref[...])
    su = _dot(h2b, s_up_ref[...])
    scores = jax.nn.sigmoid(logits)
    sel = scores + rbias_ref[...]
    eidx = lax.broadcasted_iota(jnp.int32, (N_EXPERTS, tm), 0).astype(F32)
    e_rows, w_rows = [], []
    mask = jnp.zeros((N_EXPERTS, tm), F32)
    for _ in range(TOP_K):
        m = jnp.max(sel, axis=0, keepdims=True)
        ek = jnp.min(jnp.where(sel == m, eidx, float(N_EXPERTS)), axis=0, keepdims=True)
        oh = eidx == ek
        w_rows.append(jnp.sum(jnp.where(oh, scores, 0.0), axis=0, keepdims=True))
        e_rows.append(ek)
        mask = mask + oh.astype(F32)
        sel = jnp.where(oh, -jnp.inf, sel)

    shared = _dot((sg * _sigmoid(sg) * su).astype(BF16), s_down_ref[...])
    xres_ref[...] = x1 + shared

    wsum = w_rows[0]
    for k in range(1, TOP_K):
        wsum = wsum + w_rows[k]

    before = _dot(mask.astype(BF16), tri[...]) + cnt_carry[...]
    for k in range(TOP_K):
        oh = eidx == e_rows[k]
        rank_ref[k:k + 1, :] = jnp.sum(jnp.where(oh, before, 0.0), axis=0,
                                       keepdims=True).astype(jnp.int32)
        idx_ref[k:k + 1, :] = e_rows[k].astype(jnp.int32)
        wsel_ref[k:k + 1, :] = w_rows[k] / wsum * ROUTED_SCALE
    total = cnt_carry[...] + jnp.sum(mask, axis=1, keepdims=True)
    cnt_carry[...] = total
    counts_ref[...] = total.astype(jnp.int32)


def _mixer_router(x2d, seq_len, g_mix, w_in, b_gate, w_grp, pool_scale, w_po, conv_w,
                  w_co, w_o, g_ffn, wr, rbias, s_gate, s_up, s_down):
    t_tok = x2d.shape[0]
    tm = TM_MIX
    n_seq_tiles = seq_len // tm
    const = lambda shape: pl.BlockSpec(shape, lambda i: (0,) * len(shape),
                                       pipeline_mode=pl.Buffered(1))
    hbm = pl.BlockSpec(memory_space=pl.ANY)
    row_blk = pl.BlockSpec((tm, D_MODEL), lambda i: (i, 0))
    half_blk = pl.BlockSpec((tm, HALF), lambda i: (i, 0))
    slot_blk = pl.BlockSpec((TOP_K, tm), lambda i: (0, i))
    return pl.pallas_call(
        functools.partial(_mixer_router_kernel, n_seq_tiles),
        grid=(t_tok // tm,),
        in_specs=[row_blk, const(g_mix.shape), hbm, const(b_gate.shape),
                  hbm, const(pool_scale.shape), hbm,
                  const(conv_w.shape), hbm, hbm, const(g_ffn.shape),
                  const(wr.shape), const(rbias.shape), hbm, hbm, hbm],
        out_specs=[row_blk, half_blk, slot_blk, slot_blk, slot_blk,
                   pl.BlockSpec((N_EXPERTS, 1), lambda i: (0, 0))],
        out_shape=[jax.ShapeDtypeStruct((t_tok, D_MODEL), F32),
                   jax.ShapeDtypeStruct((t_tok, HALF), U32),
                   jax.ShapeDtypeStruct((TOP_K, t_tok), jnp.int32),
                   jax.ShapeDtypeStruct((TOP_K, t_tok), F32),
                   jax.ShapeDtypeStruct((TOP_K, t_tok), jnp.int32),
                   jax.ShapeDtypeStruct((N_EXPERTS, 1), jnp.int32)],
        scratch_shapes=[pltpu.VMEM((POOL_HALO + tm, POOL_WIDTH), F32),
                        pltpu.VMEM((CONV_HALO + tm, CONV_WIDTH), F32),
                        pltpu.VMEM((N_EXPERTS, 1), F32),
                        pltpu.VMEM((tm, tm), BF16)]
                       + [pltpu.VMEM(w.shape, BF16)
                          for w in (w_in, w_grp, w_po, w_co, w_o, s_gate, s_up, s_down)]
                       + [pltpu.VMEM((2, W_IN_CHUNK, w_in.shape[1]), F32),
                          pltpu.VMEM(w_o.shape, F32),
                          pltpu.VMEM((2,) + s_gate.shape, F32),
                          pltpu.VMEM(w_grp.shape, F32),
                          pltpu.SemaphoreType.DMA((7,))],
        compiler_params=pltpu.CompilerParams(dimension_semantics=("arbitrary",),
                                             vmem_limit_bytes=VMEM_LIMIT),
        name="mixer_router",
    )(x2d, g_mix, w_in, b_gate, w_grp, pool_scale, w_po, conv_w, w_co, w_o, g_ffn,
      wr, rbias, s_gate, s_up, s_down)


def _plan_kernel(n_blk, n_pad, counts_ref, idx_ref, rank_ref,
                 dest_ref, blk_e_ref, next_e_ref, n_used_ref, n_zero_ref, zero_rows_ref, pad_start):
    e_n = N_EXPERTS

    @pl.when(pl.program_id(0) == 0)
    def _():
        sub = lax.broadcasted_iota(jnp.int32, (e_n, e_n), 0)
        lane = lax.broadcasted_iota(jnp.int32, (e_n, e_n), 1)
        c_col = counts_ref[...]
        p_col = ((c_col + (ROW_BLOCK - 1)) // ROW_BLOCK) * ROW_BLOCK
        c_f = c_col.astype(F32)
        p_f = p_col.astype(F32)
        gap_f = p_f - c_f
        to_row = lambda col: jnp.sum(jnp.where(sub == lane, col, 0.0), axis=0, keepdims=True)
        p_row = to_row(p_f)
        gap_row = to_row(gap_f)
        pad_end_col = jnp.sum(jnp.where(lane <= sub, p_row, 0.0), axis=1, keepdims=True)
        gap_before_col = jnp.sum(jnp.where(lane < sub, gap_row, 0.0), axis=1, keepdims=True)
        pad_start[...] = pad_end_col - p_f
        pad_end_last = jnp.sum(p_row, axis=1, keepdims=True)
        n_used_ref[...] = jnp.broadcast_to(pad_end_last * (1.0 / ROW_BLOCK),
                                           n_used_ref.shape).astype(jnp.int32)
        gap_total = jnp.sum(gap_row, axis=1, keepdims=True)
        n_zero_ref[...] = jnp.broadcast_to(jnp.floor((gap_total + (SC_CHUNK - 1))
                                                     * (1.0 / SC_CHUNK)),
                                           n_zero_ref.shape).astype(jnp.int32)
        row0 = (lax.broadcasted_iota(jnp.int32, (e_n, blk_e_ref.shape[1]), 1)
                * ROW_BLOCK).astype(F32)
        owner = jnp.sum(jnp.where(pad_end_col <= row0, 1.0, 0.0), axis=0, keepdims=True)
        blk_e_ref[...] = jnp.minimum(owner, float(e_n - 1)).astype(jnp.int32)
        used_later = jnp.logical_and(sub > lane, p_f > 0.0)
        nxt = jnp.min(jnp.where(used_later, sub, e_n), axis=0, keepdims=True)
        next_e_ref[...] = jnp.full(next_e_ref.shape, -1, jnp.int32)
        next_e_ref[:, 0:e_n] = jnp.where(nxt < e_n, nxt, -1)
        for j0 in range(0, n_pad, PLAN_LANES):
            j = (j0 + lax.broadcasted_iota(jnp.int32, (e_n, PLAN_LANES), 1)).astype(F32)
            before = jnp.sum(jnp.where(gap_before_col <= j, c_f, 0.0), axis=0, keepdims=True)
            zero_rows_ref[:, j0:j0 + PLAN_LANES] = (j[0:1, :] + before).astype(jnp.int32)

    tm = idx_ref.shape[1]
    eidx = lax.broadcasted_iota(jnp.int32, (e_n, tm), 0)
    ps = pad_start[...]
    for k in range(TOP_K):
        oh = eidx == idx_ref[k:k + 1, :]
        start = jnp.sum(jnp.where(oh, ps, 0.0), axis=0, keepdims=True)
        dest_ref[k:k + 1, :] = start.astype(jnp.int32) + rank_ref[k:k + 1, :]


def _plan(counts, idx_t, rank_t, n_blk, n_pad):
    t_tok = idx_t.shape[1]
    lanes = 128
    n_blk_p = -(-n_blk // lanes) * lanes
    slot_blk = pl.BlockSpec((TOP_K, TM_DEST), lambda i: (0, i))
    whole = lambda n: pl.BlockSpec((1, n), lambda i: (0, 0))
    dest_t, blk_e, next_e, n_used, n_zero, zero_rows = pl.pallas_call(
        functools.partial(_plan_kernel, n_blk, n_pad),
        grid=(t_tok // TM_DEST,),
        in_specs=[pl.BlockSpec((N_EXPERTS, 1), lambda i: (0, 0)), slot_blk, slot_blk],
        out_specs=[slot_blk, whole(n_blk_p), whole(lanes), whole(lanes), whole(lanes),
                   whole(n_pad)],
        out_shape=[jax.ShapeDtypeStruct((TOP_K, t_tok), jnp.int32),
                   jax.ShapeDtypeStruct((1, n_blk_p), jnp.int32),
                   jax.ShapeDtypeStruct((1, lanes), jnp.int32),
                   jax.ShapeDtypeStruct((1, lanes), jnp.int32),
                   jax.ShapeDtypeStruct((1, lanes), jnp.int32),
                   jax.ShapeDtypeStruct((1, n_pad), jnp.int32)],
        scratch_shapes=[pltpu.VMEM((N_EXPERTS, 1), F32)],
        compiler_params=pltpu.CompilerParams(dimension_semantics=("arbitrary",)),
        name="plan",
    )(counts, idx_t, rank_t)
    return (dest_t, blk_e.reshape(-1), next_e.reshape(-1), n_used.reshape(-1), n_zero.reshape(-1),
            zero_rows.reshape(-1))


def _sc_mesh():
    return plsc.VectorSubcoreMesh(core_axis_name="c", subcore_axis_name="s")


def _sc_worker_id():
    return lax.axis_index("s") * SC_CORES + lax.axis_index("c")


def _dispatch(h2p, dest_t, zero_rows, n_zero, n_rows):
    t_tok, width = h2p.shape
    per_w = t_tok // SC_WORKERS
    n_chunks = per_w // SC_CHUNK
    z_chunks = zero_rows.shape[0] // (SC_WORKERS * SC_CHUNK)
    dest_w = (dest_t.reshape(TOP_K, SC_WORKERS, n_chunks, SC_CHUNK)
              .transpose(1, 2, 0, 3).reshape(SC_WORKERS * n_chunks * TOP_K, SC_CHUNK))
    zero_w = zero_rows.reshape(z_chunks * SC_WORKERS, SC_CHUNK)

    @functools.partial(
        pl.kernel, mesh=_sc_mesh(),
        out_type=jax.ShapeDtypeStruct((n_rows, width), h2p.dtype),
        scratch_types=[pltpu.VMEM((n_chunks * TOP_K, SC_CHUNK), jnp.int32),
                       pltpu.VMEM((z_chunks, SC_CHUNK), jnp.int32),
                       pltpu.VMEM((SC_LANES,), jnp.int32),
                       pltpu.VMEM((2, SC_CHUNK, width), h2p.dtype),
                       pltpu.VMEM((SC_CHUNK, width), h2p.dtype),
                       pltpu.SemaphoreType.DMA, pltpu.SemaphoreType.DMA,
                       pltpu.SemaphoreType.DMA, pltpu.SemaphoreType.DMA],
        compiler_params=pltpu.CompilerParams(needs_layout_passes=False),
        name="dispatch",
    )
    def k(h2p_hbm, dest_hbm, zidx_hbm, nz_hbm, zsrc_hbm, xs_hbm, idx_v, zidx_v, nz_v, rows_v, zero_v,
          gsem, wsem, zsem, isem):
        wid = _sc_worker_id()
        base = wid * per_w
        setup = [pltpu.make_async_copy(zidx_hbm.at[pl.ds(j * SC_WORKERS + wid, 1)],
                                       zidx_v.at[pl.ds(j, 1)], isem) for j in range(z_chunks)]
        setup += [pltpu.make_async_copy(dest_hbm.at[pl.ds(wid * n_chunks * TOP_K, n_chunks * TOP_K)],
                                        idx_v, isem),
                  pltpu.make_async_copy(nz_hbm.at[pl.ds(0, SC_LANES)], nz_v, isem),
                  pltpu.make_async_copy(zsrc_hbm, zero_v, isem)]
        for cp in setup:
            cp.start()
        for cp in setup:
            cp.wait()
        n_zero_chunks = jnp.max(nz_v[...])

        def zput(j):
            return pltpu.make_async_copy(zero_v, xs_hbm.at[zidx_v.at[j]], zsem)

        for j in range(z_chunks):
            @pl.when(j * SC_WORKERS + wid < n_zero_chunks)
            def _():
                zput(j).start()

        def get(j, slot):
            return pltpu.make_async_copy(h2p_hbm.at[pl.ds(base + j * SC_CHUNK, SC_CHUNK)],
                                         rows_v.at[slot], gsem)

        def put(j, slot, kk):
            return pltpu.make_async_copy(rows_v.at[slot], xs_hbm.at[idx_v.at[j * TOP_K + kk]], wsem)

        get(0, 0).start()

        @pl.loop(0, n_chunks, step=2)
        def _(j):
            for b in range(2):
                jj = j + b
                get(jj, b).wait()

                @pl.when(jj >= 1)
                def _():
                    for kk in range(TOP_K):
                        put(jj - 1, 1 - b, kk).wait()

                @pl.when(jj + 1 < n_chunks)
                def _():
                    get(jj + 1, 1 - b).start()
                for kk in range(TOP_K):
                    put(jj, b, kk).start()

        for kk in range(TOP_K):
            put(n_chunks - 1, (n_chunks - 1) % 2, kk).wait()
        for j in range(z_chunks):
            @pl.when(j * SC_WORKERS + wid < n_zero_chunks)
            def _():
                zput(j).wait()

    return k(h2p, dest_w, zero_w, n_zero, jnp.zeros((SC_CHUNK, width), h2p.dtype))


def _regroup_sum(ys, dest_t, wsel_t, xres):
    n_slots, t_tok = dest_t.shape
    width = ys.shape[1]
    d_out = xres.shape[1]
    grp = SC_SUM_GROUP
    r_rows = SC_SUM_ROW_RING
    r_acc = SC_SUM_ACC_RING
    per_w = t_tok // SC_WORKERS
    n_sub = per_w // grp

    @functools.partial(
        pl.kernel, mesh=_sc_mesh(),
        out_type=jax.ShapeDtypeStruct((t_tok, d_out), F32),
        scratch_types=[pltpu.VMEM((n_slots, per_w), jnp.int32),
                       pltpu.VMEM((n_slots, per_w), F32),
                       pltpu.VMEM((r_rows, n_slots * grp, width), ys.dtype),
                       pltpu.VMEM((r_acc, grp, d_out), F32),
                       pltpu.SemaphoreType.DMA((r_rows,)), pltpu.SemaphoreType.DMA((r_acc,)),
                       pltpu.SemaphoreType.DMA((r_acc,))],
        compiler_params=pltpu.CompilerParams(needs_layout_passes=False),
        name="regroup_sum",
    )
    def k(ys_hbm, idx_hbm, w_hbm, xres_hbm, out_hbm, idx_v, w_v, rows_v, acc_v, gsem, xsem, psem):
        wid = _sc_worker_id()
        base = wid * per_w
        setup = [pltpu.make_async_copy(idx_hbm.at[:, pl.ds(base, per_w)], idx_v, psem.at[0]),
                 pltpu.make_async_copy(w_hbm.at[:, pl.ds(base, per_w)], w_v, psem.at[1])]
        for cp in setup:
            cp.start()

        def gets(j, slot):
            return [pltpu.make_async_copy(ys_hbm.at[idx_v.at[kk, pl.ds(j * grp, grp)]],
                                          rows_v.at[slot, pl.ds(kk * grp, grp)], gsem.at[slot])
                    for kk in range(n_slots)]

        def xload(j, slot):
            return pltpu.make_async_copy(xres_hbm.at[pl.ds(base + j * grp, grp)], acc_v.at[slot],
                                         xsem.at[slot])

        def put(j, slot):
            return pltpu.make_async_copy(acc_v.at[slot], out_hbm.at[pl.ds(base + j * grp, grp)],
                                         psem.at[slot])

        def accumulate(j, rslot, aslot):
            @pl.loop(0, grp)
            def _(g):
                pos = jnp.full((SC_LANES,), j * grp + g, jnp.int32)
                wk = [plsc.load_gather(w_v, [jnp.full((SC_LANES,), kk, jnp.int32), pos])
                      for kk in range(n_slots)]

                @plsc.parallel_loop(0, width // SC_LANES)
                def _(v):
                    cols_a = pl.ds(v * SC_LANES, SC_LANES)
                    cols_b = pl.ds(width + v * SC_LANES, SC_LANES)
                    acc_a = acc_v[aslot, g, cols_a]
                    acc_b = acc_v[aslot, g, cols_b]
                    for kk in range(n_slots):
                        ya, yb = _unpack_pair(rows_v[rslot, kk * grp + g, cols_a])
                        acc_a = acc_a + ya * wk[kk]
                        acc_b = acc_b + yb * wk[kk]
                    acc_v[aslot, g, cols_a] = acc_a
                    acc_v[aslot, g, cols_b] = acc_b

        xload(0, 0).start()
        for cp in setup:
            cp.wait()
        for cp in gets(0, 0):
            cp.start()

        @pl.loop(0, n_sub, step=r_acc)
        def _(j):
            for b in range(r_acc):
                jj = j + b
                rs = b % r_rows
                na = (b + 1) % r_acc
                for cp in gets(jj, rs):
                    cp.wait()

                @pl.when(jj + 1 < n_sub)
                def _():
                    for cp in gets(jj + 1, (b + 1) % r_rows):
                        cp.start()
                xload(jj, b).wait()

                @pl.when(jj + 1 < n_sub)
                def _():
                    @pl.when(jj + 1 >= r_acc)
                    def _():
                        put(jj + 1 - r_acc, na).wait()
                    xload(jj + 1, na).start()
                accumulate(jj, rs, b)
                put(jj, b).start()

        for p in range(r_acc):
            put(n_sub - r_acc + p, p).wait()

    return k(ys, dest_t, wsel_t, xres)


def _experts_kernel(n_blk, blk_e_ref, next_e_ref, n_used_ref,
                    xs_hbm, eg_hbm, eu_hbm, ed_hbm, ys_hbm,
                    xbuf, ybuf, hid, stg_g, stg_u, stg_d, wg, wu, wd, zbuf, xsem, ysem, wsem, zsem):
    n = n_used_ref[0]

    def ring(b):
        return jnp.bitwise_and(b, ROW_RING - 1)

    def x_copy(b):
        return pltpu.make_async_copy(xs_hbm.at[pl.ds(pl.multiple_of(b * ROW_BLOCK, ROW_BLOCK),
                                                     ROW_BLOCK)], xbuf.at[ring(b)], xsem.at[ring(b)])

    def y_copy(b):
        return pltpu.make_async_copy(ybuf.at[ring(b)],
                                     ys_hbm.at[pl.ds(pl.multiple_of(b * ROW_BLOCK, ROW_BLOCK),
                                                     ROW_BLOCK)], ysem.at[ring(b)])

    def w_copies(e):
        return (pltpu.make_async_copy(eg_hbm.at[e], stg_g, wsem.at[0]),
                pltpu.make_async_copy(eu_hbm.at[e], stg_u, wsem.at[1]),
                pltpu.make_async_copy(ed_hbm.at[e], stg_d, wsem.at[2]))

    def switch_expert(e, wslot):
        for cp in w_copies(e):
            cp.wait()
        wg[wslot] = stg_g[...].astype(BF16)
        wu[wslot] = stg_u[...].astype(BF16)
        wd[wslot] = stg_d[...].astype(BF16)
        nxt = next_e_ref[e]

        @pl.when(nxt >= 0)
        def _():
            for cp in w_copies(nxt):
                cp.start()

    def gate_up(b, wslot):
        xa, xb = _unpack_pair(xbuf[ring(b)])
        xb16 = jnp.concatenate([xa, xb], axis=1).astype(BF16)
        g = _dot(xb16, wg[wslot])
        up = _dot(xb16, wu[wslot])
        hid[jnp.bitwise_and(b, 1)] = (g * _sigmoid(g) * up).astype(BF16)

    def down(b, wslot):
        y = _dot(hid[jnp.bitwise_and(b, 1)], wd[wslot])
        ybuf[ring(b)] = _pack_pair(y[:, 0:HALF], y[:, HALF:D_MODEL])

    def z_copy(b):
        return pltpu.make_async_copy(zbuf, ys_hbm.at[pl.ds(pl.multiple_of(b * ROW_BLOCK, ROW_BLOCK),
                                                           ROW_BLOCK)], zsem)

    zbuf[...] = jnp.zeros_like(zbuf)

    def z_start(b, c):
        z_copy(b).start(priority=1)
        return c
    lax.fori_loop(n, n_blk, z_start, 0)

    e0 = blk_e_ref[0]
    for cp in w_copies(e0):
        cp.start()
    for j in range(ROW_RING):
        @pl.when(j < n)
        def _():
            x_copy(j).start()
    switch_expert(e0, 0)
    x_copy(0).wait()
    gate_up(0, 0)

    def body(b, wslot_prev):
        e = blk_e_ref[b]
        first = e != blk_e_ref[b - 1]
        wslot = jnp.where(first, 1 - wslot_prev, wslot_prev)

        @pl.when(first)
        def _():
            switch_expert(e, wslot)

        x_copy(b).wait()

        @pl.when(b + ROW_RING - 1 < n)
        def _():
            x_copy(b + ROW_RING - 1).start()

        @pl.when(b >= ROW_RING + 1)
        def _():
            y_copy(b - 1 - ROW_RING).wait()

        down(b - 1, wslot_prev)
        gate_up(b, wslot)
        y_copy(b - 1).start(priority=1)
        return wslot

    wslot_last = lax.fori_loop(1, n, body, jnp.int32(0))

    last = n - 1

    @pl.when(last >= ROW_RING)
    def _():
        y_copy(last - ROW_RING).wait()
    down(last, wslot_last)
    y_copy(last).start(priority=1)
    for j in range(ROW_RING - 1, -1, -1):
        @pl.when(last - j >= 0)
        def _():
            y_copy(last - j).wait()

    def z_wait(b, c):
        z_copy(b).wait()
        return c
    lax.fori_loop(n, n_blk, z_wait, 0)


def _experts(blk_e, next_e, n_used, xs, e_gate, e_up, e_down):
    n_rows = xs.shape[0]
    n_blk = n_rows // ROW_BLOCK
    any_spec = pl.BlockSpec(memory_space=pl.ANY)
    grid_spec = pltpu.PrefetchScalarGridSpec(
        num_scalar_prefetch=3,
        grid=(1,),
        in_specs=[any_spec, any_spec, any_spec, any_spec],
        out_specs=any_spec,
        scratch_shapes=[pltpu.VMEM((ROW_RING, ROW_BLOCK, HALF), U32),
                        pltpu.VMEM((ROW_RING, ROW_BLOCK, HALF), U32),
                        pltpu.VMEM((2, ROW_BLOCK, EXPERT_HIDDEN), BF16),
                        pltpu.VMEM((D_MODEL, EXPERT_HIDDEN), F32),
                        pltpu.VMEM((D_MODEL, EXPERT_HIDDEN), F32),
                        pltpu.VMEM((EXPERT_HIDDEN, D_MODEL), F32),
                        pltpu.VMEM((2, D_MODEL, EXPERT_HIDDEN), BF16),
                        pltpu.VMEM((2, D_MODEL, EXPERT_HIDDEN), BF16),
                        pltpu.VMEM((2, EXPERT_HIDDEN, D_MODEL), BF16),
                        pltpu.VMEM((ROW_BLOCK, HALF), U32),
                        pltpu.SemaphoreType.DMA((ROW_RING,)),
                        pltpu.SemaphoreType.DMA((ROW_RING,)),
                        pltpu.SemaphoreType.DMA((3,)),
                        pltpu.SemaphoreType.DMA],
    )
    return pl.pallas_call(
        functools.partial(_experts_kernel, n_blk),
        grid_spec=grid_spec,
        out_shape=jax.ShapeDtypeStruct((n_rows, HALF), U32),
        compiler_params=pltpu.CompilerParams(dimension_semantics=("arbitrary",)),
        name="experts",
    )(blk_e, next_e, n_used, xs, e_gate, e_up, e_down)


def _final_norm_kernel(x_ref, g_ref, out_ref):
    out_ref[...] = _rms(x_ref[...], g_ref[...])


def _final_norm(xsum, g_final):
    t_tok = xsum.shape[0]
    blk = pl.BlockSpec((TM_NORM, D_MODEL), lambda i: (i, 0))
    return pl.pallas_call(
        _final_norm_kernel,
        grid=(t_tok // TM_NORM,),
        in_specs=[blk, pl.BlockSpec((1, D_MODEL), lambda i: (0, 0))],
        out_specs=blk,
        out_shape=jax.ShapeDtypeStruct((t_tok, D_MODEL), F32),
        compiler_params=pltpu.CompilerParams(dimension_semantics=("arbitrary",)),
        name="final_norm",
    )(xsum, g_final)


def kernel(x, g_mix, w_in, b_gate, w_pool_group, pool_scale, w_pool_out, conv_w, w_conv_out, w_o,
           g_ffn, w_router, router_bias, e_gate, e_up, e_down, s_gate, s_up, s_down, g_final):
    b, s, d = x.shape
    t_tok = b * s
    n_pad = N_EXPERTS * ROW_BLOCK
    n_rows = t_tok * TOP_K + n_pad
    n_blk = n_rows // ROW_BLOCK
    assert d == D_MODEL and s % TM_MIX == 0 and TM_MIX >= POOL_HALO
    assert t_tok % TM_DEST == 0 and t_tok % TM_NORM == 0 and n_pad % PLAN_LANES == 0
    assert t_tok % (2 * SC_WORKERS * SC_CHUNK) == 0 and n_pad % (SC_WORKERS * SC_CHUNK) == 0
    assert t_tok % (SC_WORKERS * SC_SUM_GROUP * SC_SUM_ACC_RING) == 0

    row = lambda a: a.reshape(1, -1)
    wr_t = w_router.T.astype(F32)
    wr_hi = wr_t.astype(BF16)
    wr = jnp.concatenate([wr_hi, (wr_t - wr_hi.astype(F32)).astype(BF16)], axis=0)

    xres, h2p, idx_t, wsel_t, rank_t, counts = _mixer_router(
        x.reshape(t_tok, d), s, row(g_mix), w_in, row(b_gate), w_pool_group,
        row(pool_scale), w_pool_out, conv_w, w_conv_out, w_o, row(g_ffn), wr,
        router_bias.astype(F32).reshape(N_EXPERTS, 1), s_gate, s_up, s_down)
    dest_t, blk_e, next_e, n_used, n_zero, zero_rows = _plan(counts, idx_t, rank_t, n_blk, n_pad)
    xs = _dispatch(h2p, dest_t, zero_rows, n_zero, n_rows)
    ys = _experts(blk_e, next_e, n_used, xs, e_gate, e_up, e_down)
    xsum = _regroup_sum(ys, dest_t, wsel_t, xres)
    return _final_norm(xsum, row(g_final)).reshape(b, s, d)
```

```python
import functools

import jax
import jax.numpy as jnp
from jax import lax
from jax.experimental import pallas as pl
from jax.experimental.pallas import tpu as pltpu
from jax.experimental.pallas import tpu_sc as plsc

D_MODEL = 1024
HALF = D_MODEL // 2
POOL_WIDTH = 512
N_POOL_GROUPS = 4
POOL_GROUP = 128
POOL_WINDOWS = (2, 4, 8, 16)
CONV_WIDTH = 512
N_EXPERTS = 64
TOP_K = 8
EXPERT_HIDDEN = 256
SHARED_HIDDEN = 256
ROUTED_SCALE = 2.5
EPS = 1e-6

POOL_HALO = 16
CONV_HALO = 8
TM_MIX = 512
W_IN_CHUNK = 128
TM_DEST = 4096
PLAN_LANES = 2048
ROW_BLOCK = 512
ROW_RING = 8
TM_NORM = 2048
VMEM_LIMIT = 56 * 1024 * 1024

SC_CORES = 2
SC_SUBCORES = 16
SC_WORKERS = SC_CORES * SC_SUBCORES
SC_LANES = 16
SC_CHUNK = 64
SC_SUM_GROUP = 8
SC_SUM_ROW_RING = 2
SC_SUM_ACC_RING = 4

BF16 = jnp.bfloat16
F32 = jnp.float32
U32 = jnp.uint32


def _rms(x, g):
    r = lax.rsqrt(jnp.mean(x * x, axis=-1, keepdims=True) + EPS)
    return (x * r) * g


def _dot(a, b):
    return jnp.dot(a, b, preferred_element_type=F32)


def _sigmoid(z):
    return 0.5 * jnp.tanh(0.5 * z) + 0.5


def _pack_pair(a, b):
    ra = lax.bitcast_convert_type(a.astype(BF16).astype(F32), U32)
    rb = lax.bitcast_convert_type(b.astype(BF16).astype(F32), U32)
    return ra | (rb >> 16)


def _unpack_pair(w):
    a = lax.bitcast_convert_type(w & jnp.uint32(0xFFFF0000), F32)
    b = lax.bitcast_convert_type(w << 16, F32)
    return a, b


def _load_weights_bf16(w_in_hbm, w_grp_hbm, w_po_hbm, w_co_hbm, w_o_hbm, s_gate_hbm, s_up_hbm,
                       s_down_hbm, w_in_ref, w_grp_ref, w_po_ref, w_co_ref, w_o_ref, s_gate_ref,
                       s_up_ref, s_down_ref, stg_in, stg_sq, stg_sh, stg_grp, wsem):
    copy = lambda src, dst, k: pltpu.make_async_copy(src, dst, wsem.at[k])
    rows = stg_in.shape[1]
    n_in = w_in_hbm.shape[0] // rows
    half = w_po_hbm.shape[0]
    c_in = [copy(w_in_hbm.at[pl.ds(j * rows, rows)], stg_in.at[j % 2], j % 2) for j in range(n_in)]
    c_o = copy(w_o_hbm, stg_sq, 2)
    c_sg = copy(s_gate_hbm, stg_sh.at[0], 3)
    c_su = copy(s_up_hbm, stg_sh.at[1], 4)
    c_grp = copy(w_grp_hbm, stg_grp, 5)
    c_po = copy(w_po_hbm, stg_sq.at[0:half], 2)
    c_co = copy(w_co_hbm, stg_sq.at[half:2 * half], 6)
    c_sd = copy(s_down_hbm, stg_sq.at[0:s_down_hbm.shape[0]], 2)
    for cp in (c_in[0], c_in[1], c_o, c_sg, c_su, c_grp):
        cp.start()
    for j in range(n_in):
        c_in[j].wait()
        w_in_ref[j * rows:(j + 1) * rows, :] = stg_in[j % 2].astype(BF16)
        if j + 2 < n_in:
            c_in[j + 2].start()
    c_o.wait()
    w_o_ref[...] = stg_sq[...].astype(BF16)
    c_po.start()
    c_co.start()
    c_sg.wait()
    s_gate_ref[...] = stg_sh[0].astype(BF16)
    c_su.wait()
    s_up_ref[...] = stg_sh[1].astype(BF16)
    c_grp.wait()
    w_grp_ref[...] = stg_grp[...].astype(BF16)
    c_po.wait()
    w_po_ref[...] = stg_sq[0:half, :].astype(BF16)
    c_co.wait()
    w_co_ref[...] = stg_sq[half:2 * half, :].astype(BF16)
    c_sd.start()
    c_sd.wait()
    s_down_ref[...] = stg_sq[0:s_down_hbm.shape[0], :].astype(BF16)


def _mixer_router_kernel(n_seq_tiles,
                         x_ref, g_mix_ref, w_in_hbm, b_gate_ref, w_grp_hbm, pool_scale_ref,
                         w_po_hbm, conv_w_ref, w_co_hbm, w_o_hbm, g_ffn_ref,
                         wr_ref, rbias_ref, s_gate_hbm, s_up_hbm, s_down_hbm,
                         xres_ref, h2p_ref, idx_ref, wsel_ref, rank_ref, counts_ref,
                         ext_pool, ext_conv, cnt_carry, tri,
                         w_in_ref, w_grp_ref, w_po_ref, w_co_ref, w_o_ref, s_gate_ref, s_up_ref,
                         s_down_ref, stg_in, stg_sq, stg_sh, stg_grp, wsem):
    tm = x_ref.shape[0]
    i = pl.program_id(0)
    st = i % n_seq_tiles

    @pl.when(i == 0)
    def _():
        _load_weights_bf16(w_in_hbm, w_grp_hbm, w_po_hbm, w_co_hbm, w_o_hbm, s_gate_hbm, s_up_hbm,
                           s_down_hbm, w_in_ref, w_grp_ref, w_po_ref, w_co_ref, w_o_ref,
                           s_gate_ref, s_up_ref, s_down_ref, stg_in, stg_sq, stg_sh, stg_grp, wsem)
        r = lax.broadcasted_iota(jnp.int32, (tm, tm), 0)
        c = lax.broadcasted_iota(jnp.int32, (tm, tm), 1)
        tri[...] = (r < c).astype(BF16)
        cnt_carry[...] = jnp.zeros_like(cnt_carry)

    @pl.when(st == 0)
    def _():
        ext_pool[0:POOL_HALO, :] = jnp.zeros((POOL_HALO, POOL_WIDTH), F32)
        ext_conv[0:CONV_HALO, :] = jnp.zeros((CONV_HALO, CONV_WIDTH), F32)

    x = x_ref[...]
    hb = _rms(x, g_mix_ref[...]).astype(BF16)

    o0 = POOL_WIDTH
    o1 = o0 + CONV_WIDTH
    o2 = o1 + CONV_WIDTH
    o3 = o2 + CONV_WIDTH

    u = _dot(hb, w_in_ref[:, 0:o0])
    ext_pool[POOL_HALO:POOL_HALO + tm, :] = u
    gc = _dot(hb, w_in_ref[:, o1:o2])
    v = _dot(hb, w_in_ref[:, o2:o3])
    pre_a = _dot(hb, w_in_ref[:, o3:o3 + D_MODEL])

    t_glob = st * tm + lax.broadcasted_iota(jnp.int32, (tm, 1), 0)
    mixed = []
    for gi, w in enumerate(POOL_WINDOWS):
        cols = slice(gi * POOL_GROUP, (gi + 1) * POOL_GROUP)
        ug = u[:, cols]
        acc = ug
        for j in range(1, w):
            acc = acc + ext_pool[POOL_HALO - j:POOL_HALO - j + tm, cols]
        cnt = jnp.minimum(t_glob + 1, w).astype(F32)
        pooled = acc * (1.0 / cnt) - ug
        mixed.append(_dot(pooled.astype(BF16), w_grp_ref[gi]))
    ext_pool[0:POOL_HALO, :] = ext_pool[tm:tm + POOL_HALO, :]
    pre_b = _dot(hb, w_in_ref[:, o3 + D_MODEL:o3 + 2 * D_MODEL])
    gb = _dot(hb, w_in_ref[:, o0:o1])

    cv = gc * v
    ext_conv[CONV_HALO:CONV_HALO + tm, :] = cv
    conv = (ext_conv[CONV_HALO - 2:CONV_HALO - 2 + tm, :] * conv_w_ref[0:1, :]
            + ext_conv[CONV_HALO - 1:CONV_HALO - 1 + tm, :] * conv_w_ref[1:2, :]
            + cv * conv_w_ref[2:3, :])
    ext_conv[0:CONV_HALO, :] = ext_conv[tm:tm + CONV_HALO, :]
    branch_b = _dot((gb * conv).astype(BF16), w_co_ref[...])
    mixed = jnp.concatenate(mixed, axis=1) * pool_scale_ref[...]
    branch_a = _dot(mixed.astype(BF16), w_po_ref[...])

    merged = (_sigmoid(pre_a + b_gate_ref[:, 0:D_MODEL]) * branch_a
              + _sigmoid(pre_b + b_gate_ref[:, D_MODEL:2 * D_MODEL]) * branch_b)
    x1 = x + _dot(merged.astype(BF16), w_o_ref[...])

    h2 = _rms(x1, g_ffn_ref[...])
    h2p_ref[...] = _pack_pair(h2[:, 0:HALF], h2[:, HALF:D_MODEL])
    h2b = h2.astype(BF16)

    nt = (((1,), (1,)), ((), ()))
    parts = lax.dot_general(wr_ref[...], h2b, nt, preferred_element_type=F32)
    logits = parts[0:N_EXPERTS, :] + parts[N_EXPERTS:2 * N_EXPERTS, :]
    sg = _dot(h2b, s_gate_ref[...])
    su = _dot(h2b, s_up_ref[...])
    scores = jax.nn.sigmoid(logits)
    sel = scores + rbias_ref[...]
    eidx = lax.broadcasted_iota(jnp.int32, (N_EXPERTS, tm), 0).astype(F32)
    e_rows, w_rows = [], []
    mask = jnp.zeros((N_EXPERTS, tm), F32)
    for _ in range(TOP_K):
        m = jnp.max(sel, axis=0, keepdims=True)
        ek = jnp.min(jnp.where(sel == m, eidx, float(N_EXPERTS)), axis=0, keepdims=True)
        oh = eidx == ek
        w_rows.append(jnp.sum(jnp.where(oh, scores, 0.0), axis=0, keepdims=True))
        e_rows.append(ek)
        mask = mask + oh.astype(F32)
        sel = jnp.where(oh, -jnp.inf, sel)

    shared = _dot((sg * _sigmoid(sg) * su).astype(BF16), s_down_ref[...])
    xres_ref[...] = x1 + shared

    wsum = w_rows[0]
    for k in range(1, TOP_K):
        wsum = wsum + w_rows[k]

    before = _dot(mask.astype(BF16), tri[...]) + cnt_carry[...]
    for k in range(TOP_K):
        oh = eidx == e_rows[k]
        rank_ref[k:k + 1, :] = jnp.sum(jnp.where(oh, before, 0.0), axis=0,
                                       keepdims=True).astype(jnp.int32)
        idx_ref[k:k + 1, :] = e_rows[k].astype(jnp.int32)
        wsel_ref[k:k + 1, :] = w_rows[k] / wsum * ROUTED_SCALE
    total = cnt_carry[...] + jnp.sum(mask, axis=1, keepdims=True)
    cnt_carry[...] = total
    counts_ref[...] = total.astype(jnp.int32)


def _mixer_router(x2d, seq_len, g_mix, w_in, b_gate, w_grp, pool_scale, w_po, conv_w,
                  w_co, w_o, g_ffn, wr, rbias, s_gate, s_up, s_down):
    t_tok = x2d.shape[0]
    tm = TM_MIX
    n_seq_tiles = seq_len // tm
    const = lambda shape: pl.BlockSpec(shape, lambda i: (0,) * len(shape),
                                       pipeline_mode=pl.Buffered(1))
    hbm = pl.BlockSpec(memory_space=pl.ANY)
    row_blk = pl.BlockSpec((tm, D_MODEL), lambda i: (i, 0))
    half_blk = pl.BlockSpec((tm, HALF), lambda i: (i, 0))
    slot_blk = pl.BlockSpec((TOP_K, tm), lambda i: (0, i))
    return pl.pallas_call(
        functools.partial(_mixer_router_kernel, n_seq_tiles),
        grid=(t_tok // tm,),
        in_specs=[row_blk, const(g_mix.shape), hbm, const(b_gate.shape),
                  hbm, const(pool_scale.shape), hbm,
                  const(conv_w.shape), hbm, hbm, const(g_ffn.shape),
                  const(wr.shape), const(rbias.shape), hbm, hbm, hbm],
        out_specs=[row_blk, half_blk, slot_blk, slot_blk, slot_blk,
                   pl.BlockSpec((N_EXPERTS, 1), lambda i: (0, 0))],
        out_shape=[jax.ShapeDtypeStruct((t_tok, D_MODEL), F32),
                   jax.ShapeDtypeStruct((t_tok, HALF), U32),
                   jax.ShapeDtypeStruct((TOP_K, t_tok), jnp.int32),
                   jax.ShapeDtypeStruct((TOP_K, t_tok), F32),
                   jax.ShapeDtypeStruct((TOP_K, t_tok), jnp.int32),
                   jax.ShapeDtypeStruct((N_EXPERTS, 1), jnp.int32)],
        scratch_shapes=[pltpu.VMEM((POOL_HALO + tm, POOL_WIDTH), F32),
                        pltpu.VMEM((CONV_HALO + tm, CONV_WIDTH), F32),
                        pltpu.VMEM((N_EXPERTS, 1), F32),
                        pltpu.VMEM((tm, tm), BF16)]
                       + [pltpu.VMEM(w.shape, BF16)
                          for w in (w_in, w_grp, w_po, w_co, w_o, s_gate, s_up, s_down)]
                       + [pltpu.VMEM((2, W_IN_CHUNK, w_in.shape[1]), F32),
                          pltpu.VMEM(w_o.shape, F32),
                          pltpu.VMEM((2,) + s_gate.shape, F32),
                          pltpu.VMEM(w_grp.shape, F32),
                          pltpu.SemaphoreType.DMA((7,))],
        compiler_params=pltpu.CompilerParams(dimension_semantics=("arbitrary",),
                                             vmem_limit_bytes=VMEM_LIMIT),
        name="mixer_router",
    )(x2d, g_mix, w_in, b_gate, w_grp, pool_scale, w_po, conv_w, w_co, w_o, g_ffn,
      wr, rbias, s_gate, s_up, s_down)


def _plan_kernel(n_blk, n_pad, counts_ref, idx_ref, rank_ref,
                 dest_ref, blk_e_ref, next_e_ref, n_used_ref, n_zero_ref, zero_rows_ref, pad_start):
    e_n = N_EXPERTS

    @pl.when(pl.program_id(0) == 0)
    def _():
        sub = lax.broadcasted_iota(jnp.int32, (e_n, e_n), 0)
        lane = lax.broadcasted_iota(jnp.int32, (e_n, e_n), 1)
        c_col = counts_ref[...]
        p_col = ((c_col + (ROW_BLOCK - 1)) // ROW_BLOCK) * ROW_BLOCK
        c_f = c_col.astype(F32)
        p_f = p_col.astype(F32)
        gap_f = p_f - c_f
        to_row = lambda col: jnp.sum(jnp.where(sub == lane, col, 0.0), axis=0, keepdims=True)
        p_row = to_row(p_f)
        gap_row = to_row(gap_f)
        pad_end_col = jnp.sum(jnp.where(lane <= sub, p_row, 0.0), axis=1, keepdims=True)
        gap_before_col = jnp.sum(jnp.where(lane < sub, gap_row, 0.0), axis=1, keepdims=True)
        pad_start[...] = pad_end_col - p_f
        pad_end_last = jnp.sum(p_row, axis=1, keepdims=True)
        n_used_ref[...] = jnp.broadcast_to(pad_end_last * (1.0 / ROW_BLOCK),
                                           n_used_ref.shape).astype(jnp.int32)
        gap_total = jnp.sum(gap_row, axis=1, keepdims=True)
        n_zero_ref[...] = jnp.broadcast_to(jnp.floor((gap_total + (SC_CHUNK - 1))
                                                     * (1.0 / SC_CHUNK)),
                                           n_zero_ref.shape).astype(jnp.int32)
        row0 = (lax.broadcasted_iota(jnp.int32, (e_n, blk_e_ref.shape[1]), 1)
                * ROW_BLOCK).astype(F32)
        owner = jnp.sum(jnp.where(pad_end_col <= row0, 1.0, 0.0), axis=0, keepdims=True)
        blk_e_ref[...] = jnp.minimum(owner, float(e_n - 1)).astype(jnp.int32)
        used_later = jnp.logical_and(sub > lane, p_f > 0.0)
        nxt = jnp.min(jnp.where(used_later, sub, e_n), axis=0, keepdims=True)
        next_e_ref[...] = jnp.full(next_e_ref.shape, -1, jnp.int32)
        next_e_ref[:, 0:e_n] = jnp.where(nxt < e_n, nxt, -1)
        for j0 in range(0, n_pad, PLAN_LANES):
            j = (j0 + lax.broadcasted_iota(jnp.int32, (e_n, PLAN_LANES), 1)).astype(F32)
            before = jnp.sum(jnp.where(gap_before_col <= j, c_f, 0.0), axis=0, keepdims=True)
            zero_rows_ref[:, j0:j0 + PLAN_LANES] = (j[0:1, :] + before).astype(jnp.int32)

    tm = idx_ref.shape[1]
    eidx = lax.broadcasted_iota(jnp.int32, (e_n, tm), 0)
    ps = pad_start[...]
    for k in range(TOP_K):
        oh = eidx == idx_ref[k:k + 1, :]
        start = jnp.sum(jnp.where(oh, ps, 0.0), axis=0, keepdims=True)
        dest_ref[k:k + 1, :] = start.astype(jnp.int32) + rank_ref[k:k + 1, :]


def _plan(counts, idx_t, rank_t, n_blk, n_pad):
    t_tok = idx_t.shape[1]
    lanes = 128
    n_blk_p = -(-n_blk // lanes) * lanes
    slot_blk = pl.BlockSpec((TOP_K, TM_DEST), lambda i: (0, i))
    whole = lambda n: pl.BlockSpec((1, n), lambda i: (0, 0))
    dest_t, blk_e, next_e, n_used, n_zero, zero_rows = pl.pallas_call(
        functools.partial(_plan_kernel, n_blk, n_pad),
        grid=(t_tok // TM_DEST,),
        in_specs=[pl.BlockSpec((N_EXPERTS, 1), lambda i: (0, 0)), slot_blk, slot_blk],
        out_specs=[slot_blk, whole(n_blk_p), whole(lanes), whole(lanes), whole(lanes),
                   whole(n_pad)],
        out_shape=[jax.ShapeDtypeStruct((TOP_K, t_tok), jnp.int32),
                   jax.ShapeDtypeStruct((1, n_blk_p), jnp.int32),
                   jax.ShapeDtypeStruct((1, lanes), jnp.int32),
                   jax.ShapeDtypeStruct((1, lanes), jnp.int32),
                   jax.ShapeDtypeStruct((1, lanes), jnp.int32),
                   jax.ShapeDtypeStruct((1, n_pad), jnp.int32)],
        scratch_shapes=[pltpu.VMEM((N_EXPERTS, 1), F32)],
        compiler_params=pltpu.CompilerParams(dimension_semantics=("arbitrary",)),
        name="plan",
    )(counts, idx_t, rank_t)
    return (dest_t, blk_e.reshape(-1), next_e.reshape(-1), n_used.reshape(-1), n_zero.reshape(-1),
            zero_rows.reshape(-1))


def _sc_mesh():
    return plsc.VectorSubcoreMesh(core_axis_name="c", subcore_axis_name="s")


def _sc_worker_id():
    return lax.axis_index("s") * SC_CORES + lax.axis_index("c")


def _dispatch(h2p, dest_t, zero_rows, n_zero, n_rows):
    t_tok, width = h2p.shape
    per_w = t_tok // SC_WORKERS
    n_chunks = per_w // SC_CHUNK
    z_chunks = zero_rows.shape[0] // (SC_WORKERS * SC_CHUNK)
    dest_w = (dest_t.reshape(TOP_K, SC_WORKERS, n_chunks, SC_CHUNK)
              .transpose(1, 2, 0, 3).reshape(SC_WORKERS * n_chunks * TOP_K, SC_CHUNK))
    zero_w = zero_rows.reshape(z_chunks * SC_WORKERS, SC_CHUNK)

    @functools.partial(
        pl.kernel, mesh=_sc_mesh(),
        out_type=jax.ShapeDtypeStruct((n_rows, width), h2p.dtype),
        scratch_types=[pltpu.VMEM((n_chunks * TOP_K, SC_CHUNK), jnp.int32),
                       pltpu.VMEM((z_chunks, SC_CHUNK), jnp.int32),
                       pltpu.VMEM((SC_LANES,), jnp.int32),
                       pltpu.VMEM((2, SC_CHUNK, width), h2p.dtype),
                       pltpu.VMEM((SC_CHUNK, width), h2p.dtype),
                       pltpu.SemaphoreType.DMA, pltpu.SemaphoreType.DMA,
                       pltpu.SemaphoreType.DMA, pltpu.SemaphoreType.DMA],
        compiler_params=pltpu.CompilerParams(needs_layout_passes=False),
        name="dispatch",
    )
    def k(h2p_hbm, dest_hbm, zidx_hbm, nz_hbm, zsrc_hbm, xs_hbm, idx_v, zidx_v, nz_v, rows_v, zero_v,
          gsem, wsem, zsem, isem):
        wid = _sc_worker_id()
        base = wid * per_w
        setup = [pltpu.make_async_copy(zidx_hbm.at[pl.ds(j * SC_WORKERS + wid, 1)],
                                       zidx_v.at[pl.ds(j, 1)], isem) for j in range(z_chunks)]
        setup += [pltpu.make_async_copy(dest_hbm.at[pl.ds(wid * n_chunks * TOP_K, n_chunks * TOP_K)],
                                        idx_v, isem),
                  pltpu.make_async_copy(nz_hbm.at[pl.ds(0, SC_LANES)], nz_v, isem),
                  pltpu.make_async_copy(zsrc_hbm, zero_v, isem)]
        for cp in setup:
            cp.start()
        for cp in setup:
            cp.wait()
        n_zero_chunks = jnp.max(nz_v[...])

        def zput(j):
            return pltpu.make_async_copy(zero_v, xs_hbm.at[zidx_v.at[j]], zsem)

        for j in range(z_chunks):
            @pl.when(j * SC_WORKERS + wid < n_zero_chunks)
            def _():
                zput(j).start()

        def get(j, slot):
            return pltpu.make_async_copy(h2p_hbm.at[pl.ds(base + j * SC_CHUNK, SC_CHUNK)],
                                         rows_v.at[slot], gsem)

        def put(j, slot, kk):
            return pltpu.make_async_copy(rows_v.at[slot], xs_hbm.at[idx_v.at[j * TOP_K + kk]], wsem)

        get(0, 0).start()

        @pl.loop(0, n_chunks, step=2)
        def _(j):
            for b in range(2):
                jj = j + b
                get(jj, b).wait()

                @pl.when(jj >= 1)
                def _():
                    for kk in range(TOP_K):
                        put(jj - 1, 1 - b, kk).wait()

                @pl.when(jj + 1 < n_chunks)
                def _():
                    get(jj + 1, 1 - b).start()
                for kk in range(TOP_K):
                    put(jj, b, kk).start()

        for kk in range(TOP_K):
            put(n_chunks - 1, (n_chunks - 1) % 2, kk).wait()
        for j in range(z_chunks):
            @pl.when(j * SC_WORKERS + wid < n_zero_chunks)
            def _():
                zput(j).wait()

    return k(h2p, dest_w, zero_w, n_zero, jnp.zeros((SC_CHUNK, width), h2p.dtype))


def _regroup_sum(ys, dest_t, wsel_t, xres):
    n_slots, t_tok = dest_t.shape
    width = ys.shape[1]
    d_out = xres.shape[1]
    grp = SC_SUM_GROUP
    r_rows = SC_SUM_ROW_RING
    r_acc = SC_SUM_ACC_RING
    per_w = t_tok // SC_WORKERS
    n_sub = per_w // grp

    @functools.partial(
        pl.kernel, mesh=_sc_mesh(),
        out_type=jax.ShapeDtypeStruct((t_tok, d_out), F32),
        scratch_types=[pltpu.VMEM((n_slots, per_w), jnp.int32),
                       pltpu.VMEM((n_slots, per_w), F32),
                       pltpu.VMEM((r_rows, n_slots * grp, width), ys.dtype),
                       pltpu.VMEM((r_acc, grp, d_out), F32),
                       pltpu.SemaphoreType.DMA((r_rows,)), pltpu.SemaphoreType.DMA((r_acc,)),
                       pltpu.SemaphoreType.DMA((r_acc,))],
        compiler_params=pltpu.CompilerParams(needs_layout_passes=False),
        name="regroup_sum",
    )
    def k(ys_hbm, idx_hbm, w_hbm, xres_hbm, out_hbm, idx_v, w_v, rows_v, acc_v, gsem, xsem, psem):
        wid = _sc_worker_id()
        base = wid * per_w
        setup = [pltpu.make_async_copy(idx_hbm.at[:, pl.ds(base, per_w)], idx_v, psem.at[0]),
                 pltpu.make_async_copy(w_hbm.at[:, pl.ds(base, per_w)], w_v, psem.at[1])]
        for cp in setup:
            cp.start()

        def gets(j, slot):
            return [pltpu.make_async_copy(ys_hbm.at[idx_v.at[kk, pl.ds(j * grp, grp)]],
                                          rows_v.at[slot, pl.ds(kk * grp, grp)], gsem.at[slot])
                    for kk in range(n_slots)]

        def xload(j, slot):
            return pltpu.make_async_copy(xres_hbm.at[pl.ds(base + j * grp, grp)], acc_v.at[slot],
                                         xsem.at[slot])

        def put(j, slot):
            return pltpu.make_async_copy(acc_v.at[slot], out_hbm.at[pl.ds(base + j * grp, grp)],
                                         psem.at[slot])

        def accumulate(j, rslot, aslot):
            @pl.loop(0, grp)
            def _(g):
                pos = jnp.full((SC_LANES,), j * grp + g, jnp.int32)
                wk = [plsc.load_gather(w_v, [jnp.full((SC_LANES,), kk, jnp.int32), pos])
                      for kk in range(n_slots)]

                @plsc.parallel_loop(0, width // SC_LANES)
                def _(v):
                    cols_a = pl.ds(v * SC_LANES, SC_LANES)
                    cols_b = pl.ds(width + v * SC_LANES, SC_LANES)
                    acc_a = acc_v[aslot, g, cols_a]
                    acc_b = acc_v[aslot, g, cols_b]
                    for kk in range(n_slots):
                        ya, yb = _unpack_pair(rows_v[rslot, kk * grp + g, cols_a])
                        acc_a = acc_a + ya * wk[kk]
                        acc_b = acc_b + yb * wk[kk]
                    acc_v[aslot, g, cols_a] = acc_a
                    acc_v[aslot, g, cols_b] = acc_b

        xload(0, 0).start()
        for cp in setup:
            cp.wait()
        for cp in gets(0, 0):
            cp.start()

        @pl.loop(0, n_sub, step=r_acc)
        def _(j):
            for b in range(r_acc):
                jj = j + b
                rs = b % r_rows
                na = (b + 1) % r_acc
                for cp in gets(jj, rs):
                    cp.wait()

                @pl.when(jj + 1 < n_sub)
                def _():
                    for cp in gets(jj + 1, (b + 1) % r_rows):
                        cp.start()
                xload(jj, b).wait()

                @pl.when(jj + 1 < n_sub)
                def _():
                    @pl.when(jj + 1 >= r_acc)
                    def _():
                        put(jj + 1 - r_acc, na).wait()
                    xload(jj + 1, na).start()
                accumulate(jj, rs, b)
                put(jj, b).start()

        for p in range(r_acc):
            put(n_sub - r_acc + p, p).wait()

    return k(ys, dest_t, wsel_t, xres)


def _experts_kernel(n_blk, blk_e_ref, next_e_ref, n_used_ref,
                    xs_hbm, eg_hbm, eu_hbm, ed_hbm, ys_hbm,
                    xbuf, ybuf, hid, stg_g, stg_u, stg_d, wg, wu, wd, zbuf, xsem, ysem, wsem, zsem):
    n = n_used_ref[0]

    def ring(b):
        return jnp.bitwise_and(b, ROW_RING - 1)

    def x_copy(b):
        return pltpu.make_async_copy(xs_hbm.at[pl.ds(pl.multiple_of(b * ROW_BLOCK, ROW_BLOCK),
                                                     ROW_BLOCK)], xbuf.at[ring(b)], xsem.at[ring(b)])

    def y_copy(b):
        return pltpu.make_async_copy(ybuf.at[ring(b)],
                                     ys_hbm.at[pl.ds(pl.multiple_of(b * ROW_BLOCK, ROW_BLOCK),
                                                     ROW_BLOCK)], ysem.at[ring(b)])

    def w_copies(e):
        return (pltpu.make_async_copy(eg_hbm.at[e], stg_g, wsem.at[0]),
                pltpu.make_async_copy(eu_hbm.at[e], stg_u, wsem.at[1]),
                pltpu.make_async_copy(ed_hbm.at[e], stg_d, wsem.at[2]))

    def switch_expert(e, wslot):
        for cp in w_copies(e):
            cp.wait()
        wg[wslot] = stg_g[...].astype(BF16)
        wu[wslot] = stg_u[...].astype(BF16)
        wd[wslot] = stg_d[...].astype(BF16)
        nxt = next_e_ref[e]

        @pl.when(nxt >= 0)
        def _():
            for cp in w_copies(nxt):
                cp.start()

    def gate_up(b, wslot):
        xa, xb = _unpack_pair(xbuf[ring(b)])
        xb16 = jnp.concatenate([xa, xb], axis=1).astype(BF16)
        g = _dot(xb16, wg[wslot])
        up = _dot(xb16, wu[wslot])
        hid[jnp.bitwise_and(b, 1)] = (g * _sigmoid(g) * up).astype(BF16)

    def down(b, wslot):
        y = _dot(hid[jnp.bitwise_and(b, 1)], wd[wslot])
        ybuf[ring(b)] = _pack_pair(y[:, 0:HALF], y[:, HALF:D_MODEL])

    def z_copy(b):
        return pltpu.make_async_copy(zbuf, ys_hbm.at[pl.ds(pl.multiple_of(b * ROW_BLOCK, ROW_BLOCK),
                                                           ROW_BLOCK)], zsem)

    zbuf[...] = jnp.zeros_like(zbuf)

    def z_start(b, c):
        z_copy(b).start()
        return c
    lax.fori_loop(n, n_blk, z_start, 0)

    e0 = blk_e_ref[0]
    for cp in w_copies(e0):
        cp.start()
    for j in range(ROW_RING):
        @pl.when(j < n)
        def _():
            x_copy(j).start()
    switch_expert(e0, 0)
    x_copy(0).wait()
    gate_up(0, 0)

    def body(b, wslot_prev):
        e = blk_e_ref[b]
        first = e != blk_e_ref[b - 1]
        wslot = jnp.where(first, 1 - wslot_prev, wslot_prev)

        @pl.when(first)
        def _():
            switch_expert(e, wslot)

        x_copy(b).wait()

        @pl.when(b >= ROW_RING + 1)
        def _():
            y_copy(b - 1 - ROW_RING).wait()

        down(b - 1, wslot_prev)
        gate_up(b, wslot)
        y_copy(b - 1).start()

        @pl.when(b + ROW_RING - 1 < n)
        def _():
            x_copy(b + ROW_RING - 1).start()
        return wslot

    wslot_last = lax.fori_loop(1, n, body, jnp.int32(0))

    last = n - 1

    @pl.when(last >= ROW_RING)
    def _():
        y_copy(last - ROW_RING).wait()
    down(last, wslot_last)
    y_copy(last).start()
    for j in range(ROW_RING - 1, -1, -1):
        @pl.when(last - j >= 0)
        def _():
            y_copy(last - j).wait()

    def z_wait(b, c):
        z_copy(b).wait()
        return c
    lax.fori_loop(n, n_blk, z_wait, 0)


def _experts(blk_e, next_e, n_used, xs, e_gate, e_up, e_down):
    n_rows = xs.shape[0]
    n_blk = n_rows // ROW_BLOCK
    any_spec = pl.BlockSpec(memory_space=pl.ANY)
    grid_spec = pltpu.PrefetchScalarGridSpec(
        num_scalar_prefetch=3,
        grid=(1,),
        in_specs=[any_spec, any_spec, any_spec, any_spec],
        out_specs=any_spec,
        scratch_shapes=[pltpu.VMEM((ROW_RING, ROW_BLOCK, HALF), U32),
                        pltpu.VMEM((ROW_RING, ROW_BLOCK, HALF), U32),
                        pltpu.VMEM((2, ROW_BLOCK, EXPERT_HIDDEN), BF16),
                        pltpu.VMEM((D_MODEL, EXPERT_HIDDEN), F32),
                        pltpu.VMEM((D_MODEL, EXPERT_HIDDEN), F32),
                        pltpu.VMEM((EXPERT_HIDDEN, D_MODEL), F32),
                        pltpu.VMEM((2, D_MODEL, EXPERT_HIDDEN), BF16),
                        pltpu.VMEM((2, D_MODEL, EXPERT_HIDDEN), BF16),
                        pltpu.VMEM((2, EXPERT_HIDDEN, D_MODEL), BF16),
                        pltpu.VMEM((ROW_BLOCK, HALF), U32),
                        pltpu.SemaphoreType.DMA((ROW_RING,)),
                        pltpu.SemaphoreType.DMA((ROW_RING,)),
                        pltpu.SemaphoreType.DMA((3,)),
                        pltpu.SemaphoreType.DMA],
    )
    return pl.pallas_call(
        functools.partial(_experts_kernel, n_blk),
        grid_spec=grid_spec,
        out_shape=jax.ShapeDtypeStruct((n_rows, HALF), U32),
        compiler_params=pltpu.CompilerParams(dimension_semantics=("arbitrary",)),
        name="experts",
    )(blk_e, next_e, n_used, xs, e_gate, e_up, e_down)


def _final_norm_kernel(x_ref, g_ref, out_ref):
    out_ref[...] = _rms(x_ref[...], g_ref[...])


def _final_norm(xsum, g_final):
    t_tok = xsum.shape[0]
    blk = pl.BlockSpec((TM_NORM, D_MODEL), lambda i: (i, 0))
    return pl.pallas_call(
        _final_norm_kernel,
        grid=(t_tok // TM_NORM,),
        in_specs=[blk, pl.BlockSpec((1, D_MODEL), lambda i: (0, 0))],
        out_specs=blk,
        out_shape=jax.ShapeDtypeStruct((t_tok, D_MODEL), F32),
        compiler_params=pltpu.CompilerParams(dimension_semantics=("arbitrary",)),
        name="final_norm",
    )(xsum, g_final)


def kernel(x, g_mix, w_in, b_gate, w_pool_group, pool_scale, w_pool_out, conv_w, w_conv_out, w_o,
           g_ffn, w_router, router_bias, e_gate, e_up, e_down, s_gate, s_up, s_down, g_final):
    b, s, d = x.shape
    t_tok = b * s
    n_pad = N_EXPERTS * ROW_BLOCK
    n_rows = t_tok * TOP_K + n_pad
    n_blk = n_rows // ROW_BLOCK
    assert d == D_MODEL and s % TM_MIX == 0 and TM_MIX >= POOL_HALO
    assert t_tok % TM_DEST == 0 and t_tok % TM_NORM == 0 and n_pad % PLAN_LANES == 0
    assert t_tok % (2 * SC_WORKERS * SC_CHUNK) == 0 and n_pad % (SC_WORKERS * SC_CHUNK) == 0
    assert t_tok % (SC_WORKERS * SC_SUM_GROUP * SC_SUM_ACC_RING) == 0

    row = lambda a: a.reshape(1, -1)
    wr_t = w_router.T.astype(F32)
    wr_hi = wr_t.astype(BF16)
    wr = jnp.concatenate([wr_hi, (wr_t - wr_hi.astype(F32)).astype(BF16)], axis=0)

    xres, h2p, idx_t, wsel_t, rank_t, counts = _mixer_router(
        x.reshape(t_tok, d), s, row(g_mix), w_in, row(b_gate), w_pool_group,
        row(pool_scale), w_pool_out, conv_w, w_conv_out, w_o, row(g_ffn), wr,
        router_bias.astype(F32).reshape(N_EXPERTS, 1), s_gate, s_up, s_down)
    dest_t, blk_e, next_e, n_used, n_zero, zero_rows = _plan(counts, idx_t, rank_t, n_blk, n_pad)
    xs = _dispatch(h2p, dest_t, zero_rows, n_zero, n_rows)
    ys = _experts(blk_e, next_e, n_used, xs, e_gate, e_up, e_down)
    xsum = _regroup_sum(ys, dest_t, wsel_t, xres)
    return _final_norm(xsum, row(g_final)).reshape(b, s, d)
```

```python
import functools

import jax
import jax.numpy as jnp
from jax import lax
from jax.experimental import pallas as pl
from jax.experimental.pallas import tpu as pltpu
from jax.experimental.pallas import tpu_sc as plsc

D_MODEL = 1024
HALF = D_MODEL // 2
POOL_WIDTH = 512
N_POOL_GROUPS = 4
POOL_GROUP = 128
POOL_WINDOWS = (2, 4, 8, 16)
CONV_WIDTH = 512
N_EXPERTS = 64
TOP_K = 8
EXPERT_HIDDEN = 256
SHARED_HIDDEN = 256
ROUTED_SCALE = 2.5
EPS = 1e-6

POOL_HALO = 16
CONV_HALO = 8
TM_MIX = 512
W_IN_CHUNK = 128
TM_DEST = 4096
PLAN_LANES = 2048
ROW_BLOCK = 512
ROW_RING = 8
TM_NORM = 2048
VMEM_LIMIT = 56 * 1024 * 1024

SC_CORES = 2
SC_SUBCORES = 16
SC_WORKERS = SC_CORES * SC_SUBCORES
SC_LANES = 16
SC_CHUNK = 64
SC_SUM_GROUP = 8
SC_SUM_ROW_RING = 2
SC_SUM_ACC_RING = 4

BF16 = jnp.bfloat16
F32 = jnp.float32
U32 = jnp.uint32


def _rms(x, g):
    r = lax.rsqrt(jnp.mean(x * x, axis=-1, keepdims=True) + EPS)
    return (x * r) * g


def _dot(a, b):
    return jnp.dot(a, b, preferred_element_type=F32)


def _sigmoid(z):
    return 0.5 * jnp.tanh(0.5 * z) + 0.5


def _pack_pair(a, b):
    ra = lax.bitcast_convert_type(a.astype(BF16).astype(F32), U32)
    rb = lax.bitcast_convert_type(b.astype(BF16).astype(F32), U32)
    return ra | (rb >> 16)


def _unpack_pair(w):
    a = lax.bitcast_convert_type(w & jnp.uint32(0xFFFF0000), F32)
    b = lax.bitcast_convert_type(w << 16, F32)
    return a, b


def _load_weights_bf16(w_in_hbm, w_grp_hbm, w_po_hbm, w_co_hbm, w_o_hbm, s_gate_hbm, s_up_hbm,
                       s_down_hbm, w_in_ref, w_grp_ref, w_po_ref, w_co_ref, w_o_ref, s_gate_ref,
                       s_up_ref, s_down_ref, stg_in, stg_sq, stg_sh, stg_grp, wsem):
    copy = lambda src, dst, k: pltpu.make_async_copy(src, dst, wsem.at[k])
    rows = stg_in.shape[1]
    n_in = w_in_hbm.shape[0] // rows
    half = w_po_hbm.shape[0]
    c_in = [copy(w_in_hbm.at[pl.ds(j * rows, rows)], stg_in.at[j % 2], j % 2) for j in range(n_in)]
    c_o = copy(w_o_hbm, stg_sq, 2)
    c_sg = copy(s_gate_hbm, stg_sh.at[0], 3)
    c_su = copy(s_up_hbm, stg_sh.at[1], 4)
    c_grp = copy(w_grp_hbm, stg_grp, 5)
    c_po = copy(w_po_hbm, stg_sq.at[0:half], 2)
    c_co = copy(w_co_hbm, stg_sq.at[half:2 * half], 6)
    c_sd = copy(s_down_hbm, stg_sq.at[0:s_down_hbm.shape[0]], 2)
    for cp in (c_in[0], c_in[1], c_o, c_sg, c_su, c_grp):
        cp.start()
    for j in range(n_in):
        c_in[j].wait()
        w_in_ref[j * rows:(j + 1) * rows, :] = stg_in[j % 2].astype(BF16)
        if j + 2 < n_in:
            c_in[j + 2].start()
    c_o.wait()
    w_o_ref[...] = stg_sq[...].astype(BF16)
    c_po.start()
    c_co.start()
    c_sg.wait()
    s_gate_ref[...] = stg_sh[0].astype(BF16)
    c_su.wait()
    s_up_ref[...] = stg_sh[1].astype(BF16)
    c_grp.wait()
    w_grp_ref[...] = stg_grp[...].astype(BF16)
    c_po.wait()
    w_po_ref[...] = stg_sq[0:half, :].astype(BF16)
    c_co.wait()
    w_co_ref[...] = stg_sq[half:2 * half, :].astype(BF16)
    c_sd.start()
    c_sd.wait()
    s_down_ref[...] = stg_sq[0:s_down_hbm.shape[0], :].astype(BF16)


def _mixer_router_kernel(n_seq_tiles,
                         x_ref, g_mix_ref, w_in_hbm, b_gate_ref, w_grp_hbm, pool_scale_ref,
                         w_po_hbm, conv_w_ref, w_co_hbm, w_o_hbm, g_ffn_ref,
                         wr_ref, rbias_ref, s_gate_hbm, s_up_hbm, s_down_hbm,
                         xres_ref, h2p_ref, idx_ref, wsel_ref, rank_ref, counts_ref,
                         ext_pool, ext_conv, cnt_carry, tri,
                         w_in_ref, w_grp_ref, w_po_ref, w_co_ref, w_o_ref, s_gate_ref, s_up_ref,
                         s_down_ref, stg_in, stg_sq, stg_sh, stg_grp, wsem):
    tm = x_ref.shape[0]
    i = pl.program_id(0)
    st = i % n_seq_tiles

    @pl.when(i == 0)
    def _():
        _load_weights_bf16(w_in_hbm, w_grp_hbm, w_po_hbm, w_co_hbm, w_o_hbm, s_gate_hbm, s_up_hbm,
                           s_down_hbm, w_in_ref, w_grp_ref, w_po_ref, w_co_ref, w_o_ref,
                           s_gate_ref, s_up_ref, s_down_ref, stg_in, stg_sq, stg_sh, stg_grp, wsem)
        r = lax.broadcasted_iota(jnp.int32, (tm, tm), 0)
        c = lax.broadcasted_iota(jnp.int32, (tm, tm), 1)
        tri[...] = (r < c).astype(BF16)
        cnt_carry[...] = jnp.zeros_like(cnt_carry)

    @pl.when(st == 0)
    def _():
        ext_pool[0:POOL_HALO, :] = jnp.zeros((POOL_HALO, POOL_WIDTH), F32)
        ext_conv[0:CONV_HALO, :] = jnp.zeros((CONV_HALO, CONV_WIDTH), F32)

    x = x_ref[...]
    hb = _rms(x, g_mix_ref[...]).astype(BF16)

    o0 = POOL_WIDTH
    o1 = o0 + CONV_WIDTH
    o2 = o1 + CONV_WIDTH
    o3 = o2 + CONV_WIDTH

    u = _dot(hb, w_in_ref[:, 0:o0])
    ext_pool[POOL_HALO:POOL_HALO + tm, :] = u
    gc = _dot(hb, w_in_ref[:, o1:o2])
    v = _dot(hb, w_in_ref[:, o2:o3])
    pre_a = _dot(hb, w_in_ref[:, o3:o3 + D_MODEL])

    t_glob = st * tm + lax.broadcasted_iota(jnp.int32, (tm, 1), 0)
    mixed = []
    for gi, w in enumerate(POOL_WINDOWS):
        cols = slice(gi * POOL_GROUP, (gi + 1) * POOL_GROUP)
        ug = u[:, cols]
        acc = ug
        for j in range(1, w):
            acc = acc + ext_pool[POOL_HALO - j:POOL_HALO - j + tm, cols]
        cnt = jnp.minimum(t_glob + 1, w).astype(F32)
        pooled = acc * (1.0 / cnt) - ug
        mixed.append(_dot(pooled.astype(BF16), w_grp_ref[gi]))
    ext_pool[0:POOL_HALO, :] = ext_pool[tm:tm + POOL_HALO, :]
    pre_b = _dot(hb, w_in_ref[:, o3 + D_MODEL:o3 + 2 * D_MODEL])
    gb = _dot(hb, w_in_ref[:, o0:o1])

    cv = gc * v
    ext_conv[CONV_HALO:CONV_HALO + tm, :] = cv
    conv = (ext_conv[CONV_HALO - 2:CONV_HALO - 2 + tm, :] * conv_w_ref[0:1, :]
            + ext_conv[CONV_HALO - 1:CONV_HALO - 1 + tm, :] * conv_w_ref[1:2, :]
            + cv * conv_w_ref[2:3, :])
    ext_conv[0:CONV_HALO, :] = ext_conv[tm:tm + CONV_HALO, :]
    branch_b = _dot((gb * conv).astype(BF16), w_co_ref[...])
    mixed = jnp.concatenate(mixed, axis=1) * pool_scale_ref[...]
    branch_a = _dot(mixed.astype(BF16), w_po_ref[...])

    merged = (_sigmoid(pre_a + b_gate_ref[:, 0:D_MODEL]) * branch_a
              + _sigmoid(pre_b + b_gate_ref[:, D_MODEL:2 * D_MODEL]) * branch_b)
    x1 = x + _dot(merged.astype(BF16), w_o_ref[...])

    h2 = _rms(x1, g_ffn_ref[...])
    h2p_ref[...] = _pack_pair(h2[:, 0:HALF], h2[:, HALF:D_MODEL])
    h2b = h2.astype(BF16)

    nt = (((1,), (1,)), ((), ()))
    parts = lax.dot_general(wr_ref[...], h2b, nt, preferred_element_type=F32)
    logits = parts[0:N_EXPERTS, :] + parts[N_EXPERTS:2 * N_EXPERTS, :]
    sg = _dot(h2b, s_gate_ref[...])
    su = _dot(h2b, s_up_ref[...])
    scores = jax.nn.sigmoid(logits)
    sel = scores + rbias_ref[...]
    eidx = lax.broadcasted_iota(jnp.int32, (N_EXPERTS, tm), 0).astype(F32)
    e_rows, w_rows = [], []
    mask = jnp.zeros((N_EXPERTS, tm), F32)
    for _ in range(TOP_K):
        m = jnp.max(sel, axis=0, keepdims=True)
        ek = jnp.min(jnp.where(sel == m, eidx, float(N_EXPERTS)), axis=0, keepdims=True)
        oh = eidx == ek
        w_rows.append(jnp.sum(jnp.where(oh, scores, 0.0), axis=0, keepdims=True))
        e_rows.append(ek)
        mask = mask + oh.astype(F32)
        sel = jnp.where(oh, -jnp.inf, sel)

    shared = _dot((sg * _sigmoid(sg) * su).astype(BF16), s_down_ref[...])
    xres_ref[...] = x1 + shared

    wsum = w_rows[0]
    for k in range(1, TOP_K):
        wsum = wsum + w_rows[k]

    before = _dot(mask.astype(BF16), tri[...]) + cnt_carry[...]
    for k in range(TOP_K):
        oh = eidx == e_rows[k]
        rank_ref[k:k + 1, :] = jnp.sum(jnp.where(oh, before, 0.0), axis=0,
                                       keepdims=True).astype(jnp.int32)
        idx_ref[k:k + 1, :] = e_rows[k].astype(jnp.int32)
        wsel_ref[k:k + 1, :] = w_rows[k] / wsum * ROUTED_SCALE
    total = cnt_carry[...] + jnp.sum(mask, axis=1, keepdims=True)
    cnt_carry[...] = total
    counts_ref[...] = total.astype(jnp.int32)


def _mixer_router(x2d, seq_len, g_mix, w_in, b_gate, w_grp, pool_scale, w_po, conv_w,
                  w_co, w_o, g_ffn, wr, rbias, s_gate, s_up, s_down):
    t_tok = x2d.shape[0]
    tm = TM_MIX
    n_seq_tiles = seq_len // tm
    const = lambda shape: pl.BlockSpec(shape, lambda i: (0,) * len(shape),
                                       pipeline_mode=pl.Buffered(1))
    hbm = pl.BlockSpec(memory_space=pl.ANY)
    row_blk = pl.BlockSpec((tm, D_MODEL), lambda i: (i, 0))
    half_blk = pl.BlockSpec((tm, HALF), lambda i: (i, 0))
    slot_blk = pl.BlockSpec((TOP_K, tm), lambda i: (0, i))
    return pl.pallas_call(
        functools.partial(_mixer_router_kernel, n_seq_tiles),
        grid=(t_tok // tm,),
        in_specs=[row_blk, const(g_mix.shape), hbm, const(b_gate.shape),
                  hbm, const(pool_scale.shape), hbm,
                  const(conv_w.shape), hbm, hbm, const(g_ffn.shape),
                  const(wr.shape), const(rbias.shape), hbm, hbm, hbm],
        out_specs=[row_blk, half_blk, slot_blk, slot_blk, slot_blk,
                   pl.BlockSpec((N_EXPERTS, 1), lambda i: (0, 0))],
        out_shape=[jax.ShapeDtypeStruct((t_tok, D_MODEL), F32),
                   jax.ShapeDtypeStruct((t_tok, HALF), U32),
                   jax.ShapeDtypeStruct((TOP_K, t_tok), jnp.int32),
                   jax.ShapeDtypeStruct((TOP_K, t_tok), F32),
                   jax.ShapeDtypeStruct((TOP_K, t_tok), jnp.int32),
                   jax.ShapeDtypeStruct((N_EXPERTS, 1), jnp.int32)],
        scratch_shapes=[pltpu.VMEM((POOL_HALO + tm, POOL_WIDTH), F32),
                        pltpu.VMEM((CONV_HALO + tm, CONV_WIDTH), F32),
                        pltpu.VMEM((N_EXPERTS, 1), F32),
                        pltpu.VMEM((tm, tm), BF16)]
                       + [pltpu.VMEM(w.shape, BF16)
                          for w in (w_in, w_grp, w_po, w_co, w_o, s_gate, s_up, s_down)]
                       + [pltpu.VMEM((2, W_IN_CHUNK, w_in.shape[1]), F32),
                          pltpu.VMEM(w_o.shape, F32),
                          pltpu.VMEM((2,) + s_gate.shape, F32),
                          pltpu.VMEM(w_grp.shape, F32),
                          pltpu.SemaphoreType.DMA((7,))],
        compiler_params=pltpu.CompilerParams(dimension_semantics=("arbitrary",),
                                             vmem_limit_bytes=VMEM_LIMIT),
        name="mixer_router",
    )(x2d, g_mix, w_in, b_gate, w_grp, pool_scale, w_po, conv_w, w_co, w_o, g_ffn,
      wr, rbias, s_gate, s_up, s_down)


def _plan_kernel(n_blk, n_pad, counts_ref, idx_ref, rank_ref,
                 dest_ref, blk_e_ref, next_e_ref, n_used_ref, n_zero_ref, zero_rows_ref, pad_start):
    e_n = N_EXPERTS

    @pl.when(pl.program_id(0) == 0)
    def _():
        sub = lax.broadcasted_iota(jnp.int32, (e_n, e_n), 0)
        lane = lax.broadcasted_iota(jnp.int32, (e_n, e_n), 1)
        c_col = counts_ref[...]
        p_col = ((c_col + (ROW_BLOCK - 1)) // ROW_BLOCK) * ROW_BLOCK
        c_f = c_col.astype(F32)
        p_f = p_col.astype(F32)
        gap_f = p_f - c_f
        to_row = lambda col: jnp.sum(jnp.where(sub == lane, col, 0.0), axis=0, keepdims=True)
        p_row = to_row(p_f)
        gap_row = to_row(gap_f)
        pad_end_col = jnp.sum(jnp.where(lane <= sub, p_row, 0.0), axis=1, keepdims=True)
        gap_before_col = jnp.sum(jnp.where(lane < sub, gap_row, 0.0), axis=1, keepdims=True)
        pad_start[...] = pad_end_col - p_f
        pad_end_last = jnp.sum(p_row, axis=1, keepdims=True)
        n_used_ref[...] = jnp.broadcast_to(pad_end_last * (1.0 / ROW_BLOCK),
                                           n_used_ref.shape).astype(jnp.int32)
        gap_total = jnp.sum(gap_row, axis=1, keepdims=True)
        n_zero_ref[...] = jnp.broadcast_to(jnp.floor((gap_total + (SC_CHUNK - 1))
                                                     * (1.0 / SC_CHUNK)),
                                           n_zero_ref.shape).astype(jnp.int32)
        row0 = (lax.broadcasted_iota(jnp.int32, (e_n, blk_e_ref.shape[1]), 1)
                * ROW_BLOCK).astype(F32)
        owner = jnp.sum(jnp.where(pad_end_col <= row0, 1.0, 0.0), axis=0, keepdims=True)
        blk_e_ref[...] = jnp.minimum(owner, float(e_n - 1)).astype(jnp.int32)
        used_later = jnp.logical_and(sub > lane, p_f > 0.0)
        nxt = jnp.min(jnp.where(used_later, sub, e_n), axis=0, keepdims=True)
        next_e_ref[...] = jnp.full(next_e_ref.shape, -1, jnp.int32)
        next_e_ref[:, 0:e_n] = jnp.where(nxt < e_n, nxt, -1)
        for j0 in range(0, n_pad, PLAN_LANES):
            j = (j0 + lax.broadcasted_iota(jnp.int32, (e_n, PLAN_LANES), 1)).astype(F32)
            before = jnp.sum(jnp.where(gap_before_col <= j, c_f, 0.0), axis=0, keepdims=True)
            zero_rows_ref[:, j0:j0 + PLAN_LANES] = (j[0:1, :] + before).astype(jnp.int32)

    tm = idx_ref.shape[1]
    eidx = lax.broadcasted_iota(jnp.int32, (e_n, tm), 0)
    ps = pad_start[...]
    for k in range(TOP_K):
        oh = eidx == idx_ref[k:k + 1, :]
        start = jnp.sum(jnp.where(oh, ps, 0.0), axis=0, keepdims=True)
        dest_ref[k:k + 1, :] = start.astype(jnp.int32) + rank_ref[k:k + 1, :]


def _plan(counts, idx_t, rank_t, n_blk, n_pad):
    t_tok = idx_t.shape[1]
    lanes = 128
    n_blk_p = -(-n_blk // lanes) * lanes
    slot_blk = pl.BlockSpec((TOP_K, TM_DEST), lambda i: (0, i))
    whole = lambda n: pl.BlockSpec((1, n), lambda i: (0, 0))
    dest_t, blk_e, next_e, n_used, n_zero, zero_rows = pl.pallas_call(
        functools.partial(_plan_kernel, n_blk, n_pad),
        grid=(t_tok // TM_DEST,),
        in_specs=[pl.BlockSpec((N_EXPERTS, 1), lambda i: (0, 0)), slot_blk, slot_blk],
        out_specs=[slot_blk, whole(n_blk_p), whole(lanes), whole(lanes), whole(lanes),
                   whole(n_pad)],
        out_shape=[jax.ShapeDtypeStruct((TOP_K, t_tok), jnp.int32),
                   jax.ShapeDtypeStruct((1, n_blk_p), jnp.int32),
                   jax.ShapeDtypeStruct((1, lanes), jnp.int32),
                   jax.ShapeDtypeStruct((1, lanes), jnp.int32),
                   jax.ShapeDtypeStruct((1, lanes), jnp.int32),
                   jax.ShapeDtypeStruct((1, n_pad), jnp.int32)],
        scratch_shapes=[pltpu.VMEM((N_EXPERTS, 1), F32)],
        compiler_params=pltpu.CompilerParams(dimension_semantics=("arbitrary",)),
        name="plan",
    )(counts, idx_t, rank_t)
    return (dest_t, blk_e.reshape(-1), next_e.reshape(-1), n_used.reshape(-1), n_zero.reshape(-1),
            zero_rows.reshape(-1))


def _sc_mesh():
    return plsc.VectorSubcoreMesh(core_axis_name="c", subcore_axis_name="s")


def _sc_worker_id():
    return lax.axis_index("s") * SC_CORES + lax.axis_index("c")


def _dispatch(h2p, dest_t, zero_rows, n_zero, n_rows):
    t_tok, width = h2p.shape
    per_w = t_tok // SC_WORKERS
    n_chunks = per_w // SC_CHUNK
    z_chunks = zero_rows.shape[0] // (SC_WORKERS * SC_CHUNK)
    dest_w = (dest_t.reshape(TOP_K, SC_WORKERS, n_chunks, SC_CHUNK)
              .transpose(1, 2, 0, 3).reshape(SC_WORKERS * n_chunks * TOP_K, SC_CHUNK))
    zero_w = zero_rows.reshape(z_chunks * SC_WORKERS, SC_CHUNK)

    @functools.partial(
        pl.kernel, mesh=_sc_mesh(),
        out_type=jax.ShapeDtypeStruct((n_rows, width), h2p.dtype),
        scratch_types=[pltpu.VMEM((n_chunks * TOP_K, SC_CHUNK), jnp.int32),
                       pltpu.VMEM((z_chunks, SC_CHUNK), jnp.int32),
                       pltpu.VMEM((SC_LANES,), jnp.int32),
                       pltpu.VMEM((2, SC_CHUNK, width), h2p.dtype),
                       pltpu.VMEM((SC_CHUNK, width), h2p.dtype),
                       pltpu.SemaphoreType.DMA, pltpu.SemaphoreType.DMA,
                       pltpu.SemaphoreType.DMA, pltpu.SemaphoreType.DMA],
        compiler_params=pltpu.CompilerParams(needs_layout_passes=False),
        name="dispatch",
    )
    def k(h2p_hbm, dest_hbm, zidx_hbm, nz_hbm, zsrc_hbm, xs_hbm, idx_v, zidx_v, nz_v, rows_v, zero_v,
          gsem, wsem, zsem, isem):
        wid = _sc_worker_id()
        base = wid * per_w
        setup = [pltpu.make_async_copy(zidx_hbm.at[pl.ds(j * SC_WORKERS + wid, 1)],
                                       zidx_v.at[pl.ds(j, 1)], isem) for j in range(z_chunks)]
        setup += [pltpu.make_async_copy(dest_hbm.at[pl.ds(wid * n_chunks * TOP_K, n_chunks * TOP_K)],
                                        idx_v, isem),
                  pltpu.make_async_copy(nz_hbm.at[pl.ds(0, SC_LANES)], nz_v, isem),
                  pltpu.make_async_copy(zsrc_hbm, zero_v, isem)]
        for cp in setup:
            cp.start()
        for cp in setup:
            cp.wait()
        n_zero_chunks = jnp.max(nz_v[...])

        def zput(j):
            return pltpu.make_async_copy(zero_v, xs_hbm.at[zidx_v.at[j]], zsem)

        for j in range(z_chunks):
            @pl.when(j * SC_WORKERS + wid < n_zero_chunks)
            def _():
                zput(j).start()

        def get(j, slot):
            return pltpu.make_async_copy(h2p_hbm.at[pl.ds(base + j * SC_CHUNK, SC_CHUNK)],
                                         rows_v.at[slot], gsem)

        def put(j, slot, kk):
            return pltpu.make_async_copy(rows_v.at[slot], xs_hbm.at[idx_v.at[j * TOP_K + kk]], wsem)

        get(0, 0).start()

        @pl.loop(0, n_chunks, step=2)
        def _(j):
            for b in range(2):
                jj = j + b
                get(jj, b).wait()

                @pl.when(jj >= 1)
                def _():
                    for kk in range(TOP_K):
                        put(jj - 1, 1 - b, kk).wait()

                @pl.when(jj + 1 < n_chunks)
                def _():
                    get(jj + 1, 1 - b).start()
                for kk in range(TOP_K):
                    put(jj, b, kk).start()

        for kk in range(TOP_K):
            put(n_chunks - 1, (n_chunks - 1) % 2, kk).wait()
        for j in range(z_chunks):
            @pl.when(j * SC_WORKERS + wid < n_zero_chunks)
            def _():
                zput(j).wait()

    return k(h2p, dest_w, zero_w, n_zero, jnp.zeros((SC_CHUNK, width), h2p.dtype))


def _regroup_sum(ys, dest_t, wsel_t, xres):
    n_slots, t_tok = dest_t.shape
    width = ys.shape[1]
    d_out = xres.shape[1]
    grp = SC_SUM_GROUP
    r_rows = SC_SUM_ROW_RING
    r_acc = SC_SUM_ACC_RING
    per_w = t_tok // SC_WORKERS
    n_sub = per_w // grp

    @functools.partial(
        pl.kernel, mesh=_sc_mesh(),
        out_type=jax.ShapeDtypeStruct((t_tok, d_out), F32),
        scratch_types=[pltpu.VMEM((n_slots, per_w), jnp.int32),
                       pltpu.VMEM((n_slots, per_w), F32),
                       pltpu.VMEM((r_rows, n_slots * grp, width), ys.dtype),
                       pltpu.VMEM((r_acc, grp, d_out), F32),
                       pltpu.SemaphoreType.DMA((r_rows,)), pltpu.SemaphoreType.DMA((r_acc,)),
                       pltpu.SemaphoreType.DMA((r_acc,))],
        compiler_params=pltpu.CompilerParams(needs_layout_passes=False),
        name="regroup_sum",
    )
    def k(ys_hbm, idx_hbm, w_hbm, xres_hbm, out_hbm, idx_v, w_v, rows_v, acc_v, gsem, xsem, psem):
        wid = _sc_worker_id()
        base = wid * per_w
        setup = [pltpu.make_async_copy(idx_hbm.at[:, pl.ds(base, per_w)], idx_v, psem.at[0]),
                 pltpu.make_async_copy(w_hbm.at[:, pl.ds(base, per_w)], w_v, psem.at[1])]
        for cp in setup:
            cp.start()

        def gets(j, slot):
            return [pltpu.make_async_copy(ys_hbm.at[idx_v.at[kk, pl.ds(j * grp, grp)]],
                                          rows_v.at[slot, pl.ds(kk * grp, grp)], gsem.at[slot])
                    for kk in range(n_slots)]

        def xload(j, slot):
            return pltpu.make_async_copy(xres_hbm.at[pl.ds(base + j * grp, grp)], acc_v.at[slot],
                                         xsem.at[slot])

        def put(j, slot):
            return pltpu.make_async_copy(acc_v.at[slot], out_hbm.at[pl.ds(base + j * grp, grp)],
                                         psem.at[slot])

        def accumulate(j, rslot, aslot):
            @pl.loop(0, grp)
            def _(g):
                pos = jnp.full((SC_LANES,), j * grp + g, jnp.int32)
                wk = [plsc.load_gather(w_v, [jnp.full((SC_LANES,), kk, jnp.int32), pos])
                      for kk in range(n_slots)]

                @plsc.parallel_loop(0, width // SC_LANES)
                def _(v):
                    cols_a = pl.ds(v * SC_LANES, SC_LANES)
                    cols_b = pl.ds(width + v * SC_LANES, SC_LANES)
                    acc_a = acc_v[aslot, g, cols_a]
                    acc_b = acc_v[aslot, g, cols_b]
                    for kk in range(n_slots):
                        ya, yb = _unpack_pair(rows_v[rslot, kk * grp + g, cols_a])
                        acc_a = acc_a + ya * wk[kk]
                        acc_b = acc_b + yb * wk[kk]
                    acc_v[aslot, g, cols_a] = acc_a
                    acc_v[aslot, g, cols_b] = acc_b

        xload(0, 0).start()
        for cp in setup:
            cp.wait()
        for cp in gets(0, 0):
            cp.start()

        @pl.loop(0, n_sub, step=r_acc)
        def _(j):
            for b in range(r_acc):
                jj = j + b
                rs = b % r_rows
                na = (b + 1) % r_acc
                for cp in gets(jj, rs):
                    cp.wait()

                @pl.when(jj + 1 < n_sub)
                def _():
                    for cp in gets(jj + 1, (b + 1) % r_rows):
                        cp.start()
                xload(jj, b).wait()

                @pl.when(jj + 1 < n_sub)
                def _():
                    @pl.when(jj + 1 >= r_acc)
                    def _():
                        put(jj + 1 - r_acc, na).wait()
                    xload(jj + 1, na).start()
                accumulate(jj, rs, b)
                put(jj, b).start()

        for p in range(r_acc):
            put(n_sub - r_acc + p, p).wait()

    return k(ys, dest_t, wsel_t, xres)


def _experts_kernel(n_blk, blk_e_ref, next_e_ref, n_used_ref,
                    xs_hbm, eg_hbm, eu_hbm, ed_hbm, ys_hbm,
                    xbuf, ybuf, hid, stg_g, stg_u, stg_d, wg, wu, wd, zbuf, xsem, ysem, wsem, zsem):
    n = n_used_ref[0]

    def ring(b):
        return jnp.bitwise_and(b, ROW_RING - 1)

    def x_copy(b):
        return pltpu.make_async_copy(xs_hbm.at[pl.ds(pl.multiple_of(b * ROW_BLOCK, ROW_BLOCK),
                                                     ROW_BLOCK)], xbuf.at[ring(b)], xsem.at[ring(b)])

    def y_copy(b):
        return pltpu.make_async_copy(ybuf.at[ring(b)],
                                     ys_hbm.at[pl.ds(pl.multiple_of(b * ROW_BLOCK, ROW_BLOCK),
                                                     ROW_BLOCK)], ysem.at[ring(b)])

    def w_copies(e):
        return (pltpu.make_async_copy(eg_hbm.at[e], stg_g, wsem.at[0]),
                pltpu.make_async_copy(eu_hbm.at[e], stg_u, wsem.at[1]),
                pltpu.make_async_copy(ed_hbm.at[e], stg_d, wsem.at[2]))

    def switch_expert(e, wslot):
        for cp in w_copies(e):
            cp.wait()
        wg[wslot] = stg_g[...].astype(BF16)
        wu[wslot] = stg_u[...].astype(BF16)
        wd[wslot] = stg_d[...].astype(BF16)

    def prefetch_after(e):
        nxt = next_e_ref[e]

        @pl.when(nxt >= 0)
        def _():
            for cp in w_copies(nxt):
                cp.start()

    def gate_up(b, wslot):
        xa, xb = _unpack_pair(xbuf[ring(b)])
        xb16 = jnp.concatenate([xa, xb], axis=1).astype(BF16)
        g = _dot(xb16, wg[wslot])
        up = _dot(xb16, wu[wslot])
        hid[jnp.bitwise_and(b, 1)] = (g * _sigmoid(g) * up).astype(BF16)

    def down(b, wslot):
        y = _dot(hid[jnp.bitwise_and(b, 1)], wd[wslot])
        ybuf[ring(b)] = _pack_pair(y[:, 0:HALF], y[:, HALF:D_MODEL])

    def z_copy(b):
        return pltpu.make_async_copy(zbuf, ys_hbm.at[pl.ds(pl.multiple_of(b * ROW_BLOCK, ROW_BLOCK),
                                                           ROW_BLOCK)], zsem)

    zbuf[...] = jnp.zeros_like(zbuf)

    def z_start(b, c):
        z_copy(b).start()
        return c
    lax.fori_loop(n, n_blk, z_start, 0)

    e0 = blk_e_ref[0]
    for cp in w_copies(e0):
        cp.start()
    for j in range(ROW_RING):
        @pl.when(j < n)
        def _():
            x_copy(j).start()
    switch_expert(e0, 0)
    prefetch_after(e0)
    x_copy(0).wait()
    gate_up(0, 0)

    def body(b, wslot_prev):
        e = blk_e_ref[b]
        first = e != blk_e_ref[b - 1]
        wslot = jnp.where(first, 1 - wslot_prev, wslot_prev)

        @pl.when(first)
        def _():
            switch_expert(e, wslot)

        x_copy(b).wait()

        @pl.when(b >= ROW_RING + 1)
        def _():
            y_copy(b - 1 - ROW_RING).wait()

        down(b - 1, wslot_prev)
        gate_up(b, wslot)
        y_copy(b - 1).start()

        @pl.when(b + ROW_RING - 1 < n)
        def _():
            x_copy(b + ROW_RING - 1).start()

        @pl.when(first)
        def _():
            prefetch_after(e)
        return wslot

    wslot_last = lax.fori_loop(1, n, body, jnp.int32(0))

    last = n - 1

    @pl.when(last >= ROW_RING)
    def _():
        y_copy(last - ROW_RING).wait()
    down(last, wslot_last)
    y_copy(last).start()
    for j in range(ROW_RING - 1, -1, -1):
        @pl.when(last - j >= 0)
        def _():
            y_copy(last - j).wait()

    def z_wait(b, c):
        z_copy(b).wait()
        return c
    lax.fori_loop(n, n_blk, z_wait, 0)


def _experts(blk_e, next_e, n_used, xs, e_gate, e_up, e_down):
    n_rows = xs.shape[0]
    n_blk = n_rows // ROW_BLOCK
    any_spec = pl.BlockSpec(memory_space=pl.ANY)
    grid_spec = pltpu.PrefetchScalarGridSpec(
        num_scalar_prefetch=3,
        grid=(1,),
        in_specs=[any_spec, any_spec, any_spec, any_spec],
        out_specs=any_spec,
        scratch_shapes=[pltpu.VMEM((ROW_RING, ROW_BLOCK, HALF), U32),
                        pltpu.VMEM((ROW_RING, ROW_BLOCK, HALF), U32),
                        pltpu.VMEM((2, ROW_BLOCK, EXPERT_HIDDEN), BF16),
                        pltpu.VMEM((D_MODEL, EXPERT_HIDDEN), F32),
                        pltpu.VMEM((D_MODEL, EXPERT_HIDDEN), F32),
                        pltpu.VMEM((EXPERT_HIDDEN, D_MODEL), F32),
                        pltpu.VMEM((2, D_MODEL, EXPERT_HIDDEN), BF16),
                        pltpu.VMEM((2, D_MODEL, EXPERT_HIDDEN), BF16),
                        pltpu.VMEM((2, EXPERT_HIDDEN, D_MODEL), BF16),
                        pltpu.VMEM((ROW_BLOCK, HALF), U32),
                        pltpu.SemaphoreType.DMA((ROW_RING,)),
                        pltpu.SemaphoreType.DMA((ROW_RING,)),
                        pltpu.SemaphoreType.DMA((3,)),
                        pltpu.SemaphoreType.DMA],
    )
    return pl.pallas_call(
        functools.partial(_experts_kernel, n_blk),
        grid_spec=grid_spec,
        out_shape=jax.ShapeDtypeStruct((n_rows, HALF), U32),
        compiler_params=pltpu.CompilerParams(dimension_semantics=("arbitrary",)),
        name="experts",
    )(blk_e, next_e, n_used, xs, e_gate, e_up, e_down)


def _final_norm_kernel(x_ref, g_ref, out_ref):
    out_ref[...] = _rms(x_ref[...], g_ref[...])


def _final_norm(xsum, g_final):
    t_tok = xsum.shape[0]
    blk = pl.BlockSpec((TM_NORM, D_MODEL), lambda i: (i, 0))
    return pl.pallas_call(
        _final_norm_kernel,
        grid=(t_tok // TM_NORM,),
        in_specs=[blk, pl.BlockSpec((1, D_MODEL), lambda i: (0, 0))],
        out_specs=blk,
        out_shape=jax.ShapeDtypeStruct((t_tok, D_MODEL), F32),
        compiler_params=pltpu.CompilerParams(dimension_semantics=("arbitrary",)),
        name="final_norm",
    )(xsum, g_final)


def kernel(x, g_mix, w_in, b_gate, w_pool_group, pool_scale, w_pool_out, conv_w, w_conv_out, w_o,
           g_ffn, w_router, router_bias, e_gate, e_up, e_down, s_gate, s_up, s_down, g_final):
    b, s, d = x.shape
    t_tok = b * s
    n_pad = N_EXPERTS * ROW_BLOCK
    n_rows = t_tok * TOP_K + n_pad
    n_blk = n_rows // ROW_BLOCK
    assert d == D_MODEL and s % TM_MIX == 0 and TM_MIX >= POOL_HALO
    assert t_tok % TM_DEST == 0 and t_tok % TM_NORM == 0 and n_pad % PLAN_LANES == 0
    assert t_tok % (2 * SC_WORKERS * SC_CHUNK) == 0 and n_pad % (SC_WORKERS * SC_CHUNK) == 0
    assert t_tok % (SC_WORKERS * SC_SUM_GROUP * SC_SUM_ACC_RING) == 0

    row = lambda a: a.reshape(1, -1)
    wr_t = w_router.T.astype(F32)
    wr_hi = wr_t.astype(BF16)
    wr = jnp.concatenate([wr_hi, (wr_t - wr_hi.astype(F32)).astype(BF16)], axis=0)

    xres, h2p, idx_t, wsel_t, rank_t, counts = _mixer_router(
        x.reshape(t_tok, d), s, row(g_mix), w_in, row(b_gate), w_pool_group,
        row(pool_scale), w_pool_out, conv_w, w_conv_out, w_o, row(g_ffn), wr,
        router_bias.astype(F32).reshape(N_EXPERTS, 1), s_gate, s_up, s_down)
    dest_t, blk_e, next_e, n_used, n_zero, zero_rows = _plan(counts, idx_t, rank_t, n_blk, n_pad)
    xs = _dispatch(h2p, dest_t, zero_rows, n_zero, n_rows)
    ys = _experts(blk_e, next_e, n_used, xs, e_gate, e_up, e_down)
    xsum = _regroup_sum(ys, dest_t, wsel_t, xres)
    return _final_norm(xsum, row(g_final)).reshape(b, s, d)
```
